```python
import jax
import jax.numpy as jnp
from jax import lax
import numpy as np

D_MODEL = 1024
BATCH = 4
SEQ = 4096
DEPTH = 2

GRID_W = 64
CTX_LEN = 256
HEAD_DIM = 64
NORM_EPS = 1e-6
L2_EPS = 1e-12
N_EVEN = (DEPTH + 1) // 2
N_ODD = DEPTH // 2

GLA_DV = 64
GLA_DK = 32
GLA_HEADS = (D_MODEL // 2) // GLA_DV
GLA_LOWRANK = 16
GLA_TAU = 16.0
GLA_CHUNK = 64

ATT_HEADS = (D_MODEL // 2) // HEAD_DIM
ATT_KV_HEADS = ATT_HEADS // 4
WINDOW = 128
ATT_BLOCK = 128
ROPE_THETA = 10000.0

FOURIER_GROUP_DIM = 64
FOURIER_WIDTH = D_MODEL // 4
FOURIER_GROUPS = FOURIER_WIDTH // FOURIER_GROUP_DIM

RWKV_DIM = D_MODEL - FOURIER_WIDTH
RWKV_HEADS = RWKV_DIM // HEAD_DIM
RWKV_W_RANK = 64
RWKV_A_RANK = 64
RWKV_G_RANK = 128
RWKV_GN_EPS = 64e-5

N_EXPERTS = 16
N_GROUPS = 4
TOP_K = 2
D_EXPERT = 512

EVEN_SPLIT = (GLA_HEADS * GLA_DK, GLA_HEADS * GLA_DK, GLA_HEADS * GLA_DV, GLA_HEADS * GLA_DV,
              2 * GLA_LOWRANK, ATT_HEADS * HEAD_DIM, ATT_KV_HEADS * HEAD_DIM, ATT_KV_HEADS * HEAD_DIM)
EVEN_COLS = sum(EVEN_SPLIT)
RWKV_SPLIT = (RWKV_DIM, RWKV_DIM, RWKV_DIM, 2 * RWKV_W_RANK, 2 * RWKV_A_RANK, RWKV_G_RANK)
RWKV_COLS = sum(RWKV_SPLIT)
ODD_COLS = FOURIER_WIDTH + RWKV_COLS

kernel_name = "hybrid_gla_swa_fourier_rwkv7_moe_prefix_dit"


def rms_norm(x, gain, eps=NORM_EPS):
    xf = x.astype(jnp.float32)
    y = xf * lax.rsqrt(jnp.mean(xf * xf, axis=-1, keepdims=True) + eps)
    return (y * gain.astype(jnp.float32)).astype(x.dtype)


def modulate(x, gain, shift, scale):
    return rms_norm(x, gain) * (1.0 + scale) + shift


def split_cols(z, sizes):
    return jnp.split(z, np.cumsum(sizes)[:-1].tolist(), axis=-1)


def _ident(t):
    return t


def _flip(t):
    return jnp.flip(t, axis=1)


def axial_rope_tables(n_lat):
    rows = n_lat // GRID_W
    row = jnp.repeat(jnp.arange(rows, dtype=jnp.float32), GRID_W)
    col = jnp.tile(jnp.arange(GRID_W, dtype=jnp.float32), rows)
    n_freq = HEAD_DIM // 4
    inv_freq = ROPE_THETA ** (-jnp.arange(n_freq, dtype=jnp.float32) / n_freq)
    ang = jnp.concatenate([row[:, None] * inv_freq, col[:, None] * inv_freq], axis=-1)
    return jnp.cos(ang), jnp.sin(ang)


def apply_rope(t, cos, sin):
    half = t.shape[-1] // 2
    t1, t2 = t[..., :half], t[..., half:]
    c, s = cos[None, :, None, :], sin[None, :, None, :]
    return jnp.concatenate([t1 * c - t2 * s, t1 * s + t2 * c], axis=-1).astype(t.dtype)


def gla_log_decay(z_dec, dec_w, dec_b, direction):
    B, L, _ = z_dec.shape
    zd = z_dec[..., direction * GLA_LOWRANK:(direction + 1) * GLA_LOWRANK].astype(jnp.float32)
    la = jax.nn.log_sigmoid(zd @ dec_w[direction].astype(jnp.float32)
                            + dec_b[direction].astype(jnp.float32)) / GLA_TAU
    return la.reshape(B, L, GLA_HEADS, GLA_DK)


def gla_chunk_scan(k, v, log_a, s0, q=None):
    B, L, H, _ = k.shape
    dv = v.shape[-1]
    C = GLA_CHUNK
    nc = L // C
    emit = q is not None

    def chunks(t):
        return jnp.moveaxis(t.reshape(B, nc, C, H, t.shape[-1]), 1, 0)

    causal = jnp.tril(jnp.ones((C, C), dtype=bool))[None, :, :, None, None]

    def step(S, inp):
        k_c, v_c, g_c = inp[0], inp[1], inp[2]
        b = jnp.cumsum(g_c, axis=1)
        b_last = b[:, -1]
        S_new = (jnp.exp(b_last)[..., None] * S
                 + jnp.einsum('bjhd,bjhe->bhde', k_c * jnp.exp(b_last[:, None] - b), v_c))
        if not emit:
            return S_new, None
        q_c = inp[3]
        diff = jnp.where(causal, b[:, :, None] - b[:, None], -jnp.inf)
        att = jnp.einsum('bihd,bjhd,bijhd->bhij', q_c, k_c, jnp.exp(diff))
        o = (jnp.einsum('bhij,bjhe->bihe', att, v_c)
             + jnp.einsum('bihd,bhde->bihe', q_c * jnp.exp(b), S))
        return S_new, o

    xs = (chunks(k), chunks(v), chunks(log_a)) + ((chunks(q),) if emit else ())
    S, o = lax.scan(step, s0, xs)
    o = jnp.moveaxis(o, 0, 1).reshape(B, L, H, dv) if emit else None
    return o, S


def gla_bidirectional(ctx_t, lat_t, need_ctx):
    qc, kc, vc, dc = ctx_t
    ql, kl, vl, dl = lat_t
    B = kl.shape[0]
    s0 = jnp.zeros((B, GLA_HEADS, GLA_DK, GLA_DV), jnp.float32)
    o_lat, o_ctx = 0.0, 0.0
    for d in range(2):
        f = _ident if d == 0 else _flip
        oc, s_ctx = gla_chunk_scan(f(kc), f(vc), f(dc[d]), s0, f(qc) if need_ctx else None)
        ol, _ = gla_chunk_scan(f(kl), f(vl), f(dl[d]), s_ctx, f(ql))
        o_lat = o_lat + f(ol)
        if need_ctx:
            o_ctx = o_ctx + f(oc)
    return o_lat, (o_ctx if need_ctx else None)


def window_attention(q, k, v, k_ctx, v_ctx, sink):
    B, N, Hq, hd = q.shape
    G = Hq // ATT_KV_HEADS
    nb = N // ATT_BLOCK
    scale = hd ** -0.5
    qb = q.reshape(B, nb, ATT_BLOCK, ATT_KV_HEADS, G, hd)

    def windows(t):
        tp = jnp.pad(t, ((0, 0), (ATT_BLOCK, ATT_BLOCK), (0, 0), (0, 0)))
        tp = tp.reshape(B, nb + 2, ATT_BLOCK, ATT_KV_HEADS, hd)
        return jnp.concatenate([tp[:, :-2], tp[:, 1:-1], tp[:, 2:]], axis=2)

    kw, vw = windows(k), windows(v)
    s_loc = jnp.einsum('bnqkgd,bnjkd->bnkgqj', qb, kw).astype(jnp.float32) * scale
    s_ctx = jnp.einsum('bnqkgd,bckd->bnkgqc', qb, k_ctx).astype(jnp.float32) * scale
    blk = jnp.arange(nb)[:, None, None] * ATT_BLOCK
    qpos = blk + jnp.arange(ATT_BLOCK)[None, :, None]
    kpos = blk - ATT_BLOCK + jnp.arange(3 * ATT_BLOCK)[None, None, :]
    valid = (jnp.abs(kpos - qpos) <= WINDOW) & (kpos >= 0) & (kpos < N)
    s_loc = jnp.where(valid[None, :, None, None], s_loc, -jnp.inf)
    sk = sink.reshape(ATT_KV_HEADS, G).astype(jnp.float32)[None, None, :, :, None, None]
    m = jnp.maximum(jnp.maximum(s_loc.max(-1, keepdims=True), s_ctx.max(-1, keepdims=True)), sk)
    p_loc = jnp.exp(s_loc - m)
    p_ctx = jnp.exp(s_ctx - m)
    denom = p_loc.sum(-1) + p_ctx.sum(-1) + jnp.exp(sk - m)[..., 0]
    o = (jnp.einsum('bnkgqj,bnjkd->bnqkgd', p_loc, vw.astype(jnp.float32))
         + jnp.einsum('bnkgqc,bckd->bnqkgd', p_ctx, v_ctx.astype(jnp.float32)))
    o = o * jnp.transpose(1.0 / denom, (0, 1, 4, 2, 3))[..., None]
    return o.reshape(B, N, Hq * hd)


def context_attention(q, k, v, sink):
    B, Lc, Hq, hd = q.shape
    G = Hq // ATT_KV_HEADS
    qc = q.reshape(B, Lc, ATT_KV_HEADS, G, hd)
    s = jnp.einsum('bqkgd,bckd->bkgqc', qc, k).astype(jnp.float32) * hd ** -0.5
    sk = sink.reshape(ATT_KV_HEADS, G).astype(jnp.float32)[None, :, :, None, None]
    m = jnp.maximum(s.max(-1, keepdims=True), sk)
    p = jnp.exp(s - m)
    p = p / (p.sum(-1, keepdims=True) + jnp.exp(sk - m))
    o = jnp.einsum('bkgqc,bckd->bqkgd', p, v.astype(jnp.float32))
    return o.reshape(B, Lc, Hq * hd)


def even_mixer(h_ctx, h_lat, cos, sin, w_in, w_out, dec_w, dec_b, out_norm,
               q_norm, k_norm, sink, need_ctx):
    def project(h):
        B, L, _ = h.shape
        gq, gk, gv, gg, gdec, aq, ak, av = split_cols(h @ w_in, EVEN_SPLIT)
        hf = lambda t: t.reshape(B, L, GLA_HEADS, -1).astype(jnp.float32)
        gla = (hf(gq) * GLA_DK ** -0.5, hf(gk), hf(gv),
               tuple(gla_log_decay(gdec, dec_w, dec_b, d) for d in range(2)))
        att = (rms_norm(aq.reshape(B, L, ATT_HEADS, HEAD_DIM), q_norm),
               rms_norm(ak.reshape(B, L, ATT_KV_HEADS, HEAD_DIM), k_norm),
               av.reshape(B, L, ATT_KV_HEADS, HEAD_DIM))
        return gla, att, gg

    gla_c, att_c, gg_c = project(h_ctx)
    gla_l, att_l, gg_l = project(h_lat)
    q_l = apply_rope(att_l[0], cos, sin)
    k_l = apply_rope(att_l[1], cos, sin)
    o_gla_l, o_gla_c = gla_bidirectional(gla_c, gla_l, need_ctx)

    def merge(o_gla, gg, o_att, dtype):
        B, L = gg.shape[:2]
        g = rms_norm(o_gla, out_norm).reshape(B, L, -1) * jax.nn.silu(gg.astype(jnp.float32))
        return jnp.concatenate([g, o_att], axis=-1).astype(dtype) @ w_out

    o_att_l = window_attention(q_l, k_l, att_l[2], att_c[1], att_c[2], sink)
    out_lat = merge(o_gla_l, gg_l, o_att_l, h_lat.dtype)
    out_ctx = None
    if need_ctx:
        o_att_c = context_attention(att_c[0], att_c[1], att_c[2], sink)
        out_ctx = merge(o_gla_c, gg_c, o_att_c, h_ctx.dtype)
    return out_lat, out_ctx


def fourier_mix(z):
    B, L, _ = z.shape
    zg = z.astype(jnp.float32).reshape(B, L, FOURIER_GROUPS, FOURIER_GROUP_DIM)
    y = jnp.real(jnp.fft.fft2(zg, axes=(1, 3), norm='ortho'))
    return y.reshape(B, L, FOURIER_WIDTH)


def centred_shift(z, mu):
    zp = jnp.pad(z, ((0, 0), (1, 1), (0, 0)))
    return z + mu * (0.5 * (zp[:, :-2] + zp[:, 2:]) - z)


def rwkv_prepare(z, mu, w0, w2, a0, a2, kk_scale, k_a):
    B, L, _ = z.shape
    z = centred_shift(z.astype(jnp.float32), mu)
    r, k, v, zw, za, zg = split_cols(z, RWKV_SPLIT)
    heads = lambda t: t.reshape(B, L, RWKV_HEADS, HEAD_DIM)
    kk = heads(k * kk_scale)
    kk = kk * lax.rsqrt(jnp.sum(kk * kk, axis=-1, keepdims=True) + L2_EPS)
    per_dir = []
    for d in range(2):
        zw_d = zw[..., d * RWKV_W_RANK:(d + 1) * RWKV_W_RANK]
        za_d = za[..., d * RWKV_A_RANK:(d + 1) * RWKV_A_RANK]
        w_log = -jax.nn.softplus(-(w0[d] + jnp.tanh(zw_d) @ w2[d])) - 0.5
        decay = jnp.exp(-jnp.exp(w_log))
        a = jax.nn.sigmoid(a0[d] + za_d @ a2[d])
        k_d = k * (1.0 + (a - 1.0) * k_a)
        per_dir.append((heads(decay), heads(k_d), heads(a)))
    return heads(r), heads(v), kk, zg, per_dir


def rwkv_scan(decay, k, v, kk, a, s0, r=None):
    emit = r is not None
    xs = (decay, k, v, kk, a) + ((r,) if emit else ())
    xs = tuple(jnp.moveaxis(t, 1, 0) for t in xs)

    def step(S, inp):
        w_t, k_t, v_t, kk_t, a_t = inp[:5]
        S = (S * w_t[:, :, None, :]
             - jnp.einsum('bhvk,bhk->bhv', S, kk_t)[..., None] * (kk_t * a_t)[:, :, None, :]
             + v_t[..., None] * k_t[:, :, None, :])
        y = jnp.einsum('bhvk,bhk->bhv', S, inp[5]) if emit else None
        return S, y

    S, ys = lax.scan(step, s0, xs)
    return (jnp.moveaxis(ys, 0, 1) if emit else None), S


def rwkv_bonus(r, k, v, r_k):
    return jnp.sum(r * k * r_k, axis=-1, keepdims=True) * v


def rwkv_bidirectional(ctx_p, lat_p, r_k, need_ctx):
    rc, vc, kkc, _, dirs_c = ctx_p
    rl, vl, kkl, _, dirs_l = lat_p
    B = rl.shape[0]
    s0 = jnp.zeros((B, RWKV_HEADS, HEAD_DIM, HEAD_DIM), jnp.float32)
    y_lat, y_ctx = 0.0, 0.0
    for d in range(2):
        f = _ident if d == 0 else _flip
        dec_c, k_c, a_c = dirs_c[d]
        dec_l, k_l, a_l = dirs_l[d]
        yc, s_ctx = rwkv_scan(f(dec_c), f(k_c), f(vc), f(kkc), f(a_c), s0,
                              f(rc) if need_ctx else None)
        yl, _ = rwkv_scan(f(dec_l), f(k_l), f(vl), f(kkl), f(a_l), s_ctx, f(rl))
        y_lat = y_lat + f(yl) + rwkv_bonus(rl, k_l, vl, r_k)
        if need_ctx:
            y_ctx = y_ctx + f(yc) + rwkv_bonus(rc, k_c, vc, r_k)
    return y_lat, (y_ctx if need_ctx else None)


def rwkv_output(y, zg, g2, ln_g, ln_b):
    B, L = y.shape[:2]
    mean = jnp.mean(y, axis=-1, keepdims=True)
    var = jnp.mean(jnp.square(y - mean), axis=-1, keepdims=True)
    yn = ((y - mean) * lax.rsqrt(var + RWKV_GN_EPS)).reshape(B, L, RWKV_DIM)
    return (yn * ln_g + ln_b) * (jax.nn.sigmoid(zg) @ g2)


def odd_mixer(h_ctx, h_lat, w_in, w_out, mu, w0, w2, a0, a2, g2, kk_scale, k_a, r_k,
              ln_g, ln_b, need_ctx):
    f32 = lambda t: t.astype(jnp.float32)
    mu, w0, w2, a0, a2, g2 = f32(mu), f32(w0), f32(w2), f32(a0), f32(a2), f32(g2)
    kk_scale, k_a, r_k, ln_g, ln_b = f32(kk_scale), f32(k_a), f32(r_k), f32(ln_g), f32(ln_b)
    z_ctx, z_lat = h_ctx @ w_in, h_lat @ w_in
    prep = lambda z: rwkv_prepare(z[..., FOURIER_WIDTH:], mu, w0, w2, a0, a2, kk_scale, k_a)
    p_ctx, p_lat = prep(z_ctx), prep(z_lat)
    y_lat, y_ctx = rwkv_bidirectional(p_ctx, p_lat, r_k, need_ctx)

    def merge(z, p, y, dtype):
        rw = rwkv_output(y, p[3], g2, ln_g, ln_b)
        cat = jnp.concatenate([fourier_mix(z[..., :FOURIER_WIDTH]), rw], axis=-1)
        return cat.astype(dtype) @ w_out

    out_lat = merge(z_lat, p_lat, y_lat, h_lat.dtype)
    out_ctx = merge(z_ctx, p_ctx, y_ctx, h_ctx.dtype) if need_ctx else None
    return out_lat, out_ctx


def routed_moe(h, router_w, router_b, w_gate, w_up, w_down):
    logits = jnp.einsum('bld,de->ble', h, router_w).astype(jnp.float32)
    scores = jax.nn.sigmoid(logits)
    sel = scores + router_b.astype(jnp.float32)
    B, L, E = sel.shape
    per_group = E // N_GROUPS
    grp_score = lax.top_k(sel.reshape(B, L, N_GROUPS, per_group), TOP_K)[0].sum(-1)
    best = jnp.argmax(grp_score, axis=-1)
    in_group = (jnp.arange(E) // per_group)[None, None, :] == best[..., None]
    _, idx = lax.top_k(jnp.where(in_group, sel, -jnp.inf), TOP_K)
    w = jnp.take_along_axis(scores, idx, axis=-1)
    w = w / jnp.sum(w, axis=-1, keepdims=True)
    gates = jnp.sum(jax.nn.one_hot(idx, E, dtype=jnp.float32) * w[..., None], axis=-2)
    out = jnp.zeros(h.shape, jnp.float32)
    for e in range(N_EXPERTS):
        he = (jax.nn.silu(h @ w_gate[e]) * (h @ w_up[e])) @ w_down[e]
        out = out + gates[..., e:e + 1] * he
    return out.astype(h.dtype)


def setup_inputs(seed: int = 0) -> dict:
    key = jax.random.key(seed)
    ks = iter(jax.random.split(key, 40))
    D = D_MODEL

    def nrm(shape, s):
        return jax.random.normal(next(ks), shape, jnp.float32) * s

    def gain(shape):
        return 1.0 + nrm(shape, 0.05)

    w0_base = jnp.tile(jnp.linspace(-6.0, -1.0, HEAD_DIM), RWKV_HEADS)[None, None, :]
    return {
        'x': nrm((BATCH, SEQ, D), 1.0),
        'c': nrm((BATCH, D), 1.0),
        'ctx': nrm((BATCH, CTX_LEN, D), 1.0),
        'c_ctx': nrm((D,), 1.0),
        'ada_w': nrm((DEPTH, D, 6 * D), 0.5 * D ** -0.5),
        'ada_b': nrm((DEPTH, 6 * D), 0.02),
        'norm_mix': gain((DEPTH, D)),
        'norm_ffn': gain((DEPTH, D)),
        'even_w_in': nrm((N_EVEN, D, EVEN_COLS), D ** -0.5),
        'even_w_out': nrm((N_EVEN, D, D), D ** -0.5),
        'gla_dec_w': nrm((N_EVEN, 2, GLA_LOWRANK, GLA_HEADS * GLA_DK), GLA_LOWRANK ** -0.5),
        'gla_dec_b': 1.0 + nrm((N_EVEN, 2, GLA_HEADS * GLA_DK), 0.5),
        'gla_out_norm': gain((N_EVEN, GLA_DV)),
        'att_q_norm': gain((N_EVEN, HEAD_DIM)),
        'att_k_norm': gain((N_EVEN, HEAD_DIM)),
        'att_sink': nrm((N_EVEN, ATT_HEADS), 0.5),
        'odd_w_in': nrm((N_ODD, D, ODD_COLS), D ** -0.5),
        'odd_w_out': nrm((N_ODD, D, D), D ** -0.5),
        'rwkv_mu': jax.random.uniform(next(ks), (N_ODD, RWKV_COLS), jnp.float32),
        'rwkv_w0': w0_base + nrm((N_ODD, 2, RWKV_DIM), 0.1),
        'rwkv_w2': nrm((N_ODD, 2, RWKV_W_RANK, RWKV_DIM), RWKV_W_RANK ** -0.5),
        'rwkv_a0': nrm((N_ODD, 2, RWKV_DIM), 0.1),
        'rwkv_a2': nrm((N_ODD, 2, RWKV_A_RANK, RWKV_DIM), RWKV_A_RANK ** -0.5),
        'rwkv_g2': nrm((N_ODD, RWKV_G_RANK, RWKV_DIM), RWKV_G_RANK ** -0.5),
        'rwkv_kk_scale': 0.85 + nrm((N_ODD, RWKV_DIM), 0.05),
        'rwkv_k_a': gain((N_ODD, RWKV_DIM)),
        'rwkv_r_k': nrm((N_ODD, RWKV_HEADS, HEAD_DIM), 0.1),
        'rwkv_ln_g': gain((N_ODD, RWKV_DIM)),
        'rwkv_ln_b': nrm((N_ODD, RWKV_DIM), 0.02),
        'router_w': nrm((D, N_EXPERTS), D ** -0.5),
        'router_b': nrm((N_EXPERTS,), 0.01),
        'moe_w_gate': nrm((DEPTH, N_EXPERTS, D, D_EXPERT), D ** -0.5),
        'moe_w_up': nrm((DEPTH, N_EXPERTS, D, D_EXPERT), D ** -0.5),
        'moe_w_down': nrm((DEPTH, N_EXPERTS, D_EXPERT, D), D_EXPERT ** -0.5),
    }


def reference(x, c, ctx, c_ctx, ada_w, ada_b, norm_mix, norm_ffn,
              even_w_in, even_w_out, gla_dec_w, gla_dec_b, gla_out_norm,
              att_q_norm, att_k_norm, att_sink,
              odd_w_in, odd_w_out, rwkv_mu, rwkv_w0, rwkv_w2, rwkv_a0, rwkv_a2,
              rwkv_g2, rwkv_kk_scale, rwkv_k_a, rwkv_r_k, rwkv_ln_g, rwkv_ln_b,
              router_w, router_b, moe_w_gate, moe_w_up, moe_w_down):
    n_lat = x.shape[1]
    lc = ctx.shape[1]
    cos, sin = axial_rope_tables(n_lat)
    silu_c = jax.nn.silu(c)
    silu_cc = jax.nn.silu(c_ctx)
    x_lat, x_ctx = x, ctx
    for layer in range(DEPTH):
        need_ctx = layer < DEPTH - 1
        mod_l = jnp.split((silu_c @ ada_w[layer] + ada_b[layer])[:, None, :], 6, axis=-1)
        mod_c = jnp.split((silu_cc @ ada_w[layer] + ada_b[layer])[None, None, :], 6, axis=-1)
        h_lat = modulate(x_lat, norm_mix[layer], mod_l[0], mod_l[1])
        h_ctx = modulate(x_ctx, norm_mix[layer], mod_c[0], mod_c[1])
        i = layer // 2
        if layer % 2 == 0:
            o_lat, o_ctx = even_mixer(h_ctx, h_lat, cos, sin, even_w_in[i], even_w_out[i],
                                      gla_dec_w[i], gla_dec_b[i], gla_out_norm[i],
                                      att_q_norm[i], att_k_norm[i], att_sink[i], need_ctx)
        else:
            o_lat, o_ctx = odd_mixer(h_ctx, h_lat, odd_w_in[i], odd_w_out[i], rwkv_mu[i],
                                     rwkv_w0[i], rwkv_w2[i], rwkv_a0[i], rwkv_a2[i], rwkv_g2[i],
                                     rwkv_kk_scale[i], rwkv_k_a[i], rwkv_r_k[i],
                                     rwkv_ln_g[i], rwkv_ln_b[i], need_ctx)
        x_lat = x_lat + mod_l[2] * o_lat
        h_lat = modulate(x_lat, norm_ffn[layer], mod_l[3], mod_l[4])
        if need_ctx:
            x_ctx = x_ctx + mod_c[2] * o_ctx
            h_ctx = modulate(x_ctx, norm_ffn[layer], mod_c[3], mod_c[4])
            y = routed_moe(jnp.concatenate([h_ctx, h_lat], axis=1), router_w, router_b,
                           moe_w_gate[layer], moe_w_up[layer], moe_w_down[layer])
            x_ctx = x_ctx + mod_c[5] * y[:, :lc]
            x_lat = x_lat + mod_l[5] * y[:, lc:]
        else:
            x_lat = x_lat + mod_l[5] * routed_moe(h_lat, router_w, router_b, moe_w_gate[layer],
                                                  moe_w_up[layer], moe_w_down[layer])
    return x_lat
```

```python
import functools

import jax
import jax.numpy as jnp
import numpy as np
from jax import lax
from jax.experimental import pallas as pl
from jax.experimental.pallas import tpu as pltpu

F32 = jnp.float32
BF16 = jnp.bfloat16

GRID_W = 64
HEAD_DIM = 64
NORM_EPS = 1e-6
L2_EPS = 1e-12

GLA_DV = 64
GLA_DK = 32
GLA_HEADS = 8
GLA_LOWRANK = 16
GLA_TAU = 16.0

ATT_HEADS = 8
ATT_KV_HEADS = 2
ATT_GROUP = ATT_HEADS // ATT_KV_HEADS
ATT_BLOCK = 128
ROPE_THETA = 10000.0

FOURIER_GROUP_DIM = 64
FOURIER_WIDTH = 256

RWKV_DIM = 768
RWKV_HEADS = 12
RWKV_RANK_PAD = 128
RWKV_GN_EPS = 64e-5

N_EXPERTS = 16
N_GROUPS = 4
PER_GROUP = N_EXPERTS // N_GROUPS
D_EXPERT = 512

SEQ_CHUNK = 64
VMEM_LIMIT = 56 * 1024 * 1024


def _cparams(sem):
    return pltpu.CompilerParams(dimension_semantics=sem, vmem_limit_bytes=VMEM_LIMIT)


def _dot(a, b):
    return jnp.dot(a, b, preferred_element_type=F32)


def _dot_nt(a, b):
    return lax.dot_general(a, b, (((1,), (1,)), ((), ())), preferred_element_type=F32)


def _dot_tn(a, b):
    return lax.dot_general(a, b, (((0,), (0,)), ((), ())), preferred_element_type=F32)


def _silu(x):
    return x * jax.nn.sigmoid(x)


def _log_sigmoid(x):
    return jnp.minimum(x, 0.0) - jnp.log(1.0 + jnp.exp(-jnp.abs(x)))


def _modulated_norm(x, gain, shift, scale):
    ms = jnp.mean(x * x, axis=-1, keepdims=True)
    return (x * lax.rsqrt(ms + NORM_EPS) * gain) * (1.0 + scale) + shift


def _block_ones(n_blocks, width):
    return jnp.kron(jnp.eye(n_blocks, dtype=F32), jnp.ones((width, width), F32))


def _modvec_kernel(c_ref, w_ref, b_ref, o_ref):
    o_ref[...] = _dot(_silu(c_ref[...]), w_ref[...]) + b_ref[...]


def _modvec(cc, w, b):
    d = cc.shape[1]
    n = w.shape[1]
    tn = n // 4
    out = pl.pallas_call(
        _modvec_kernel,
        grid=(n // tn,),
        in_specs=[pl.BlockSpec((8, d), lambda j: (0, 0)),
                  pl.BlockSpec((d, tn), lambda j: (0, j)),
                  pl.BlockSpec((1, tn), lambda j: (0, j))],
        out_specs=pl.BlockSpec((8, tn), lambda j: (0, j)),
        out_shape=jax.ShapeDtypeStruct((8, n), F32),
        compiler_params=_cparams(("parallel",)),
        name="modvec",
    )(cc, w, b.reshape(1, n))
    return out.reshape(8, 6, d)


def _proj_kernel(x_ref, mod_ref, gain_ref, *refs, n_out):
    w_refs, z_refs = refs[:n_out], refs[n_out:]
    m = mod_ref[0]
    h = _modulated_norm(x_ref[...], gain_ref[...], m[0:1], m[1:2]).astype(BF16)
    for w_ref, z_ref in zip(w_refs, z_refs):
        z_ref[...] = _dot(h, w_ref[...])


def _project(x, mods, gain, weights, tm, mod_row):
    t, d = x.shape
    n_out = len(weights)
    in_specs = [pl.BlockSpec((tm, d), lambda i: (i, 0)),
                pl.BlockSpec((1, 6, d), lambda i: (mod_row(i), 0, 0)),
                pl.BlockSpec((1, d), lambda i: (0, 0))]
    in_specs += [pl.BlockSpec(w.shape, lambda i: (0, 0)) for w in weights]
    return pl.pallas_call(
        functools.partial(_proj_kernel, n_out=n_out),
        grid=(t // tm,),
        in_specs=in_specs,
        out_specs=[pl.BlockSpec((tm, w.shape[1]), lambda i: (i, 0)) for w in weights],
        out_shape=[jax.ShapeDtypeStruct((t, w.shape[1]), F32) for w in weights],
        compiler_params=_cparams(("parallel",)),
        name="proj",
    )(x, mods, gain.reshape(1, d), *weights)


def _chunk_pos(s, d, nc_ctx, nc_tot):
    back = jnp.where(s < nc_ctx, nc_ctx - 1 - s, nc_tot + nc_ctx - 1 - s)
    return jnp.where(d == 0, s, back)


def _order_masks(d, c):
    row = lax.broadcasted_iota(jnp.int32, (c, c), 0)
    col = lax.broadcasted_iota(jnp.int32, (c, c), 1)
    ahead = (row - col) * jnp.where(d == 0, 1, -1)
    return ahead >= 0, ahead > 0


def _gla_kernel(q_ref, k_ref, v_ref, dec_ref, dw_ref, db_ref, o_ref, st_ref):
    d = pl.program_id(0)
    c = q_ref.shape[0]

    @pl.when(pl.program_id(2) == 0)
    def _():
        st_ref[...] = jnp.zeros_like(st_ref)

    g = _log_sigmoid(_dot(dec_ref[...], dw_ref[0]) + db_ref[0]) / GLA_TAU
    incl, _ = _order_masks(d, c)
    b = _dot(incl.astype(F32), g)
    b_tot = jnp.sum(g, axis=0, keepdims=True)
    q_in = q_ref[...] * (GLA_DK ** -0.5) * jnp.exp(b)
    k_out = k_ref[...] * jnp.exp(-b)
    k_end = k_ref[...] * jnp.exp(b_tot - b)
    decay_tot = jnp.exp(b_tot)
    v = v_ref[...]
    outs = []
    for h in range(GLA_HEADS):
        ks = slice(h * GLA_DK, (h + 1) * GLA_DK)
        vs = slice(h * GLA_DV, (h + 1) * GLA_DV)
        att = jnp.where(incl, _dot_nt(q_in[:, ks], k_out[:, ks]), 0.0)
        st = st_ref[h]
        outs.append(_dot(att, v[:, vs]) + _dot_nt(q_in[:, ks], st))
        st_ref[h] = st * decay_tot[:, ks] + _dot_tn(v[:, vs], k_end[:, ks])
    o_ref[0] = jnp.concatenate(outs, axis=1)


def _gla(zg, zdec, dec_w_pad, dec_b, batch, nc_ctx, nc_tot):
    t = zg.shape[0]
    c = SEQ_CHUNK
    hk, hv = GLA_HEADS * GLA_DK, GLA_HEADS * GLA_DV

    def rows(d, b, s):
        return b * nc_tot + _chunk_pos(s, d, nc_ctx, nc_tot)

    return pl.pallas_call(
        _gla_kernel,
        grid=(2, batch, nc_tot),
        in_specs=[pl.BlockSpec((c, hk), lambda d, b, s: (rows(d, b, s), 0)),
                  pl.BlockSpec((c, hk), lambda d, b, s: (rows(d, b, s), 1)),
                  pl.BlockSpec((c, hv), lambda d, b, s: (rows(d, b, s), 1)),
                  pl.BlockSpec((c, 128), lambda d, b, s: (rows(d, b, s), 0)),
                  pl.BlockSpec((1, 128, hk), lambda d, b, s: (d, 0, 0)),
                  pl.BlockSpec((1, 1, hk), lambda d, b, s: (d, 0, 0))],
        out_specs=pl.BlockSpec((1, c, hv), lambda d, b, s: (d, rows(d, b, s), 0)),
        out_shape=jax.ShapeDtypeStruct((2, t, hv), F32),
        scratch_shapes=[pltpu.VMEM((GLA_HEADS, GLA_DV, GLA_DK), F32)],
        compiler_params=_cparams(("parallel", "parallel", "arbitrary")),
        name="gla_scan",
    )(zg, zg, zg, zdec, dec_w_pad, dec_b)


def _rope_swap(x):
    n = x.shape[-1]
    lane = lax.broadcasted_iota(jnp.int32, x.shape, x.ndim - 1)
    half = HEAD_DIM // 2
    return jnp.where(lane % HEAD_DIM < half, pltpu.roll(x, n - half, x.ndim - 1), pltpu.roll(x, half, x.ndim - 1))


def _qk_prep_kernel(q_ref, kv_ref, cos_ref, sin_ref, qg_ref, kg_ref, bdq_ref, bdk_ref, qo_ref, ko_ref, vo_ref):
    def norm_rope(x, gain, bd, n_heads):
        ms = _dot(x * x, bd) * (1.0 / HEAD_DIM)
        xn = x * lax.rsqrt(ms + NORM_EPS) * gain
        cos = jnp.concatenate([cos_ref[...]] * n_heads, axis=1)
        sin = jnp.concatenate([sin_ref[...]] * n_heads, axis=1)
        return xn * cos + _rope_swap(xn) * sin

    q = norm_rope(q_ref[...], qg_ref[...], bdq_ref[...], ATT_HEADS)
    qo_ref[...] = (q * (HEAD_DIM ** -0.5)).astype(BF16)
    kw = ATT_KV_HEADS * HEAD_DIM
    kv = kv_ref[...]
    ko_ref[...] = norm_rope(kv[:, :kw], kg_ref[...], bdk_ref[...], ATT_KV_HEADS).astype(BF16)
    vo_ref[...] = kv[:, kw:].astype(BF16)


def _qk_prep(zq, zkv, cos64, sin64, q_gain, k_gain, tm, l_tot):
    t = zq.shape[0]
    qw, kw = ATT_HEADS * HEAD_DIM, ATT_KV_HEADS * HEAD_DIM
    n_pos = l_tot // tm
    return pl.pallas_call(
        _qk_prep_kernel,
        grid=(t // tm,),
        in_specs=[pl.BlockSpec((tm, qw), lambda i: (i, 0)),
                  pl.BlockSpec((tm, 2 * kw), lambda i: (i, 0)),
                  pl.BlockSpec((tm, HEAD_DIM), lambda i: (i % n_pos, 0)),
                  pl.BlockSpec((tm, HEAD_DIM), lambda i: (i % n_pos, 0)),
                  pl.BlockSpec((1, qw), lambda i: (0, 0)),
                  pl.BlockSpec((1, kw), lambda i: (0, 0)),
                  pl.BlockSpec((qw, qw), lambda i: (0, 0)),
                  pl.BlockSpec((kw, kw), lambda i: (0, 0))],
        out_specs=[pl.BlockSpec((tm, qw), lambda i: (i, 0)),
                   pl.BlockSpec((tm, kw), lambda i: (i, 0)),
                   pl.BlockSpec((tm, kw), lambda i: (i, 0))],
        out_shape=[jax.ShapeDtypeStruct((t, qw), BF16),
                   jax.ShapeDtypeStruct((t, kw), BF16),
                   jax.ShapeDtypeStruct((t, kw), BF16)],
        compiler_params=_cparams(("parallel",)),
        name="qk_prep",
    )(zq, zkv, cos64, sin64, jnp.tile(q_gain, ATT_HEADS).reshape(1, qw), jnp.tile(k_gain, ATT_KV_HEADS).reshape(1, kw),
      _block_ones(ATT_HEADS, HEAD_DIM), _block_ones(ATT_KV_HEADS, HEAD_DIM))


def _attn_kernel(q_ref, kp_ref, kc_ref, kn_ref, kx_ref, vp_ref, vc_ref, vn_ref, vx_ref, sink_ref, o_ref,
                 *, n_ctx_blocks, n_lat_blocks):
    blk = ATT_BLOCK
    n = pl.program_id(1)
    m = n - n_ctx_blocks
    is_lat = n >= n_ctx_blocks
    l_ctx = kx_ref.shape[0]
    width = 3 * blk + l_ctx
    rows = ATT_GROUP * blk
    r = lax.broadcasted_iota(jnp.int32, (rows, width), 0) % blk
    c = lax.broadcasted_iota(jnp.int32, (rows, width), 1)
    lat = is_lat.astype(jnp.int32)
    has_prev = lat * (m >= 1).astype(jnp.int32)
    has_next = lat * (m <= n_lat_blocks - 2).astype(jnp.int32)
    valid = jnp.where(c < blk, (c >= r).astype(jnp.int32) * has_prev,
                      jnp.where(c < 2 * blk, lat,
                                jnp.where(c < 3 * blk, (c - 2 * blk <= r).astype(jnp.int32) * has_next, 1))) > 0
    q = q_ref[...]
    sink = sink_ref[...]
    outs = [None] * ATT_HEADS
    for kvh in range(ATT_KV_HEADS):
        ks = slice(kvh * HEAD_DIM, (kvh + 1) * HEAD_DIM)
        kw = jnp.concatenate([kp_ref[:, ks], kc_ref[:, ks], kn_ref[:, ks], kx_ref[:, ks]], axis=0)
        vw = jnp.concatenate([vp_ref[:, ks], vc_ref[:, ks], vn_ref[:, ks], vx_ref[:, ks]], axis=0)
        heads = range(kvh * ATT_GROUP, (kvh + 1) * ATT_GROUP)
        qg = jnp.concatenate([q[:, h * HEAD_DIM:(h + 1) * HEAD_DIM] for h in heads], axis=0)
        s = jnp.where(valid, _dot_nt(qg, kw), -jnp.inf)
        sk = jnp.concatenate([jnp.broadcast_to(sink[h:h + 1, 0:1], (blk, 1)) for h in heads], axis=0)
        mx = jnp.maximum(jnp.max(s, axis=-1, keepdims=True), sk)
        p = jnp.exp(s - mx)
        denom = jnp.sum(p, axis=-1, keepdims=True) + jnp.exp(sk - mx)
        o = _dot(p.astype(BF16), vw) / denom
        for g, h in enumerate(heads):
            outs[h] = o[g * blk:(g + 1) * blk]
    o_ref[...] = jnp.concatenate(outs, axis=1)


def _attention(qn, kn, vn, sink, batch, l_ctx, l_tot):
    t = qn.shape[0]
    blk = ATT_BLOCK
    nq = l_tot // blk
    nc = l_ctx // blk
    nl = nq - nc
    qw, kw = ATT_HEADS * HEAD_DIM, ATT_KV_HEADS * HEAD_DIM

    def win(off):
        def index(b, n):
            m = jnp.clip(n - nc + off, 0, nl - 1)
            return (b * nq + nc + m, 0)
        return pl.BlockSpec((blk, kw), index)

    ctx_spec = pl.BlockSpec((l_ctx, kw), lambda b, n: (b * (l_tot // l_ctx), 0))
    return pl.pallas_call(
        functools.partial(_attn_kernel, n_ctx_blocks=nc, n_lat_blocks=nl),
        grid=(batch, nq),
        in_specs=[pl.BlockSpec((blk, qw), lambda b, n: (b * nq + n, 0)),
                  win(-1), win(0), win(1), ctx_spec,
                  win(-1), win(0), win(1), ctx_spec,
                  pl.BlockSpec((ATT_HEADS, 128), lambda b, n: (0, 0))],
        out_specs=pl.BlockSpec((blk, qw), lambda b, n: (b * nq + n, 0)),
        out_shape=jax.ShapeDtypeStruct((t, qw), F32),
        compiler_params=_cparams(("parallel", "parallel")),
        name="window_attention",
    )(qn, kn, kn, kn, kn, vn, vn, vn, vn, jnp.broadcast_to(sink.astype(F32)[:, None], (ATT_HEADS, 128)))


def _route(logits_t, bias_col):
    scores = jax.nn.sigmoid(logits_t)
    sel = scores + bias_col
    rows = [sel[e:e + 1] for e in range(N_EXPERTS)]
    grp = []
    for g in range(N_GROUPS):
        r = rows[g * PER_GROUP:(g + 1) * PER_GROUP]
        best = None
        for i in range(PER_GROUP):
            for j in range(i + 1, PER_GROUP):
                pair = r[i] + r[j]
                best = pair if best is None else jnp.maximum(best, pair)
        grp.append(best)
    g_best = jnp.zeros_like(grp[0], dtype=jnp.int32)
    g_val = grp[0]
    for g in range(1, N_GROUPS):
        take = grp[g] > g_val
        g_best = jnp.where(take, g, g_best)
        g_val = jnp.where(take, grp[g], g_val)
    neg = -jnp.inf
    masked = [jnp.where(g_best == e // PER_GROUP, rows[e], neg) for e in range(N_EXPERTS)]
    i1 = jnp.zeros_like(g_best)
    v1 = masked[0]
    for e in range(1, N_EXPERTS):
        take = masked[e] > v1
        i1 = jnp.where(take, e, i1)
        v1 = jnp.where(take, masked[e], v1)
    i2 = jnp.full_like(g_best, -1)
    v2 = jnp.full_like(v1, neg)
    for e in range(N_EXPERTS):
        take = jnp.logical_and(i1 != e, masked[e] > v2)
        i2 = jnp.where(take, e, i2)
        v2 = jnp.where(take, masked[e], v2)
    w1 = jnp.zeros_like(v1)
    w2 = jnp.zeros_like(v1)
    for e in range(N_EXPERTS):
        w1 = jnp.where(i1 == e, scores[e:e + 1], w1)
        w2 = jnp.where(i2 == e, scores[e:e + 1], w2)
    inv = 1.0 / (w1 + w2)
    gates = [jnp.where(i1 == e, w1 * inv, 0.0) + jnp.where(i2 == e, w2 * inv, 0.0) for e in range(N_EXPERTS)]
    return jnp.concatenate(gates, axis=0)


def _residual_and_route(x, out, m, ffn_gain, rw_ref, rb_ref, x_ref, h_ref, g_ref):
    x1 = x + m[2:3] * out
    x_ref[...] = x1
    h = _modulated_norm(x1, ffn_gain, m[3:4], m[4:5])
    h_ref[...] = h.astype(BF16)
    logits_t = lax.dot_general(rw_ref[...], h, (((1,), (1,)), ((), ())), precision=lax.Precision.HIGHEST,
                               preferred_element_type=F32)
    g_ref[...] = _route(logits_t, rb_ref[...]).T


def _merge_even_kernel(o0_ref, o1_ref, gg_ref, oa_ref, x_ref, mod_ref, gn_ref, bd_ref, w1_ref, w2_ref,
                       fg_ref, rw_ref, rb_ref, xo_ref, ho_ref, go_ref):
    og = o0_ref[0] + o1_ref[0]
    ms = _dot(og * og, bd_ref[...]) * (1.0 / GLA_DV)
    g = og * lax.rsqrt(ms + NORM_EPS) * gn_ref[...] * _silu(gg_ref[...])
    out = _dot(g.astype(BF16), w1_ref[...]) + _dot(oa_ref[...].astype(BF16), w2_ref[...])
    _residual_and_route(x_ref[...], out, mod_ref[0], fg_ref[...], rw_ref, rb_ref, xo_ref, ho_ref, go_ref)


def _merge_even(o_gla, zg, o_att, x, mods, out_norm, w_out, ffn_gain, router_wt, router_b, tm, mod_row):
    t, d = x.shape
    hv = GLA_HEADS * GLA_DV
    qw = ATT_HEADS * HEAD_DIM
    full = lambda a: pl.BlockSpec(a.shape, lambda i: (0,) * a.ndim)
    gn = jnp.tile(out_norm, GLA_HEADS).reshape(1, hv)
    bd = _block_ones(GLA_HEADS, GLA_DV)
    w1, w2 = w_out[:hv].astype(BF16), w_out[hv:].astype(BF16)
    fg = ffn_gain.reshape(1, d)
    rb = router_b.reshape(N_EXPERTS, 1)
    return pl.pallas_call(
        _merge_even_kernel,
        grid=(t // tm,),
        in_specs=[pl.BlockSpec((1, tm, hv), lambda i: (0, i, 0)),
                  pl.BlockSpec((1, tm, hv), lambda i: (1, i, 0)),
                  pl.BlockSpec((tm, hv), lambda i: (i, 2)),
                  pl.BlockSpec((tm, qw), lambda i: (i, 0)),
                  pl.BlockSpec((tm, d), lambda i: (i, 0)),
                  pl.BlockSpec((1, 6, d), lambda i: (mod_row(i), 0, 0)),
                  full(gn), full(bd), full(w1), full(w2), full(fg), full(router_wt), full(rb)],
        out_specs=[pl.BlockSpec((tm, d), lambda i: (i, 0)),
                   pl.BlockSpec((tm, d), lambda i: (i, 0)),
                   pl.BlockSpec((tm, N_EXPERTS), lambda i: (i, 0))],
        out_shape=[jax.ShapeDtypeStruct((t, d), F32),
                   jax.ShapeDtypeStruct((t, d), BF16),
                   jax.ShapeDtypeStruct((t, N_EXPERTS), F32)],
        compiler_params=_cparams(("parallel",)),
        name="merge_even",
    )(o_gla, o_gla, zg, o_att, x, mods, gn, bd, w1, w2, fg, router_wt, rb)


def _moe_kernel(h_ref, g_ref, x_ref, mod_ref, wg_ref, wu_ref, wd_ref, o_ref, acc_ref):
    e = pl.program_id(1)

    @pl.when(e == 0)
    def _():
        acc_ref[...] = jnp.zeros_like(acc_ref)

    h = h_ref[...]
    act = _silu(_dot(h, wg_ref[0])) * _dot(h, wu_ref[0])
    gates = g_ref[...]
    lane = lax.broadcasted_iota(jnp.int32, gates.shape, 1)
    gate = jnp.sum(jnp.where(lane == e, gates, 0.0), axis=1, keepdims=True)
    acc_ref[...] += _dot((act * gate).astype(BF16), wd_ref[0])

    @pl.when(e == N_EXPERTS - 1)
    def _():
        o_ref[...] = x_ref[...] + mod_ref[0][5:6] * acc_ref[...]


def _moe(h, gates, x, mods, w_gate, w_up, w_down, tm, mod_row):
    t, d = x.shape
    return pl.pallas_call(
        _moe_kernel,
        grid=(t // tm, N_EXPERTS),
        in_specs=[pl.BlockSpec((tm, d), lambda i, e: (i, 0)),
                  pl.BlockSpec((tm, N_EXPERTS), lambda i, e: (i, 0)),
                  pl.BlockSpec((tm, d), lambda i, e: (i, 0)),
                  pl.BlockSpec((1, 6, d), lambda i, e: (mod_row(i), 0, 0)),
                  pl.BlockSpec((1, d, D_EXPERT), lambda i, e: (e, 0, 0)),
                  pl.BlockSpec((1, d, D_EXPERT), lambda i, e: (e, 0, 0)),
                  pl.BlockSpec((1, D_EXPERT, d), lambda i, e: (e, 0, 0))],
        out_specs=pl.BlockSpec((tm, d), lambda i, e: (i, 0)),
        out_shape=jax.ShapeDtypeStruct((t, d), F32),
        scratch_shapes=[pltpu.VMEM((tm, d), F32)],
        compiler_params=_cparams(("parallel", "arbitrary")),
        name="moe_experts",
    )(h, gates, x, mods, w_gate, w_up, w_down)


def _chan_dft_kernel(z_ref, w_ref, o_ref):
    res = _dot(z_ref[...].astype(BF16), w_ref[...]).astype(BF16)
    o_ref[0] = res[:, :FOURIER_WIDTH]
    o_ref[1] = res[:, FOURIER_WIDTH:]


def _chan_dft(zf, w, batch, l_ctx, l_lat, tm):
    l_tot = l_ctx + l_lat
    nt = l_lat // tm
    fw = FOURIER_WIDTH
    return pl.pallas_call(
        _chan_dft_kernel,
        grid=(batch, nt),
        in_specs=[pl.BlockSpec((tm, fw), lambda b, i: (b * (l_tot // tm) + l_ctx // tm + i, 0)),
                  pl.BlockSpec(w.shape, lambda b, i: (0, 0))],
        out_specs=pl.BlockSpec((2, tm, fw), lambda b, i: (0, i, b)),
        out_shape=jax.ShapeDtypeStruct((2, l_lat, batch * fw), BF16),
        compiler_params=_cparams(("parallel", "parallel")),
        name="fourier_channels",
    )(zf, w)


def _seq_dft_kernel(t_ref, z_ref, o_ref, acc_ref):
    k = pl.program_id(1)

    @pl.when(k == 0)
    def _():
        acc_ref[...] = jnp.zeros_like(acc_ref)

    acc_ref[...] += _dot(t_ref[...], z_ref[...])

    @pl.when(k == pl.num_programs(1) - 1)
    def _():
        o_ref[...] = acc_ref[...]


def _seq_dft(table, zc, batch, tm, tk):
    l = table.shape[0]
    fw = FOURIER_WIDTH
    nk = l // tk
    zst = zc.reshape(2 * l, batch * fw)
    return pl.pallas_call(
        _seq_dft_kernel,
        grid=(l // tm, 2 * nk),
        in_specs=[pl.BlockSpec((tm, tk), lambda i, k: (i, k)),
                  pl.BlockSpec((tk, batch * fw), lambda i, k: (k, 0))],
        out_specs=pl.BlockSpec((tm, batch * fw), lambda i, k: (i, 0)),
        out_shape=jax.ShapeDtypeStruct((l, batch * fw), F32),
        scratch_shapes=[pltpu.VMEM((tm, batch * fw), F32)],
        compiler_params=_cparams(("parallel", "arbitrary")),
        name="fourier_sequence",
    )(table, zst)


def _dft_tables(l):
    m = jnp.arange(l, dtype=jnp.int32)[:, None]
    n1 = l // 64
    a = (m * (jnp.arange(n1, dtype=jnp.int32)[None, :] * 64)) % l
    b = (m * jnp.arange(64, dtype=jnp.int32)[None, :]) % l
    wa = a.astype(F32) * (2.0 * np.pi / l)
    wb = b.astype(F32) * (2.0 * np.pi / l)
    ca, sa, cb, sb = jnp.cos(wa), jnp.sin(wa), jnp.cos(wb), jnp.sin(wb)
    cos_t = (ca[:, :, None] * cb[:, None, :] - sa[:, :, None] * sb[:, None, :]).reshape(l, l)
    sin_t = (sa[:, :, None] * cb[:, None, :] + ca[:, :, None] * sb[:, None, :]).reshape(l, l)
    table = jnp.concatenate([cos_t, sin_t], axis=1).astype(BF16)
    gd = FOURIER_GROUP_DIM
    cc = (jnp.arange(gd, dtype=jnp.int32)[:, None] * jnp.arange(gd, dtype=jnp.int32)[None, :]) % gd
    wc = cc.astype(F32) * (2.0 * np.pi / gd)
    scale = 1.0 / np.sqrt(float(l) * gd)
    eye = jnp.eye(FOURIER_WIDTH // gd, dtype=F32)
    chan = jnp.concatenate([jnp.kron(eye, jnp.cos(wc)), -jnp.kron(eye, jnp.sin(wc))], axis=1) * scale
    return table, chan.astype(BF16)


def _rwkv_kernel(z_ref, zp_ref, zn_ref, mu_ref, kks_ref, ka_ref, rk_ref, w0_ref, w2_ref, a0_ref, a2_ref, g2_ref,
                 bd_ref, y_ref, gate_ref, st_ref, *, nc_ctx, nc_tot):
    d = pl.program_id(0)
    s = pl.program_id(2)
    c = z_ref.shape[0]
    n = RWKV_DIM
    hd = HEAD_DIM

    @pl.when(s == 0)
    def _():
        st_ref[...] = jnp.zeros_like(st_ref)

    pos = _chunk_pos(s, d, nc_ctx, nc_tot)
    seg_first = jnp.logical_or(pos == 0, pos == nc_ctx)
    seg_last = jnp.logical_or(pos == nc_ctx - 1, pos == nc_tot - 1)
    z = z_ref[...]
    row = lax.broadcasted_iota(jnp.int32, z.shape, 0)
    prev_row = jnp.where(seg_first, 0.0, zp_ref[7:8, :])
    next_row = jnp.where(seg_last, 0.0, zn_ref[0:1, :])
    z_prev = jnp.where(row == 0, prev_row, pltpu.roll(z, 1, 0))
    z_next = jnp.where(row == c - 1, next_row, pltpu.roll(z, c - 1, 0))
    mu = mu_ref[...]
    zs = z + mu * (0.5 * (z_prev + z_next) - z)

    r, k, v = zs[:, 0:n], zs[:, n:2 * n], zs[:, 2 * n:3 * n]
    zw = zs[:, 3 * n:3 * n + RWKV_RANK_PAD]
    za = zs[:, 3 * n + RWKV_RANK_PAD:3 * n + 2 * RWKV_RANK_PAD]
    zg = zs[:, 3 * n + 2 * RWKV_RANK_PAD:3 * n + 3 * RWKV_RANK_PAD]

    kk = k * kks_ref[...]
    kk = kk * lax.rsqrt(_dot(kk * kk, bd_ref[...]) + L2_EPS)
    w_pre = w0_ref[0] + _dot(jnp.tanh(zw), w2_ref[0])
    w_log = _log_sigmoid(w_pre) - 0.5
    lw = -jnp.exp(w_log)
    a = jax.nn.sigmoid(a0_ref[0] + _dot(za, a2_ref[0]))
    kd = k * (1.0 + (a - 1.0) * ka_ref[...])
    beta = kk * a
    gate_ref[0] = _dot(jax.nn.sigmoid(zg), g2_ref[...])

    incl, strict = _order_masks(d, c)
    cl = _dot(incl.astype(F32), lw)
    c_tot = jnp.sum(lw, axis=0, keepdims=True)
    grow = jnp.exp(-cl)
    k_s = kd * grow
    b_s = beta * grow
    kap_s = kk * jnp.exp(cl - lw)
    r_s = r * jnp.exp(cl)
    tail = jnp.exp(c_tot - cl)
    k_e = kd * tail
    b_e = beta * tail
    gam_tot = jnp.exp(c_tot)
    bonus_w = r * kd * rk_ref[...]

    eye = (lax.broadcasted_iota(jnp.int32, (c, c), 0) == lax.broadcasted_iota(jnp.int32, (c, c), 1)).astype(F32)
    ys = []
    for h in range(RWKV_HEADS):
        hs = slice(h * hd, (h + 1) * hd)
        lhs = jnp.concatenate([kap_s[:, hs], r_s[:, hs]], axis=0)
        rhs = jnp.concatenate([k_s[:, hs], b_s[:, hs]], axis=0)
        p = _dot_nt(lhs, rhs)
        m1 = jnp.where(strict, p[0:c, 0:c], 0.0)
        m2 = jnp.where(strict, p[0:c, c:2 * c], 0.0)
        n1 = jnp.where(incl, p[c:2 * c, 0:c], 0.0)
        n2 = jnp.where(incl, p[c:2 * c, c:2 * c], 0.0)
        t_inv = eye - m2
        q = _dot(m2, m2)
        span = 2
        while True:
            t_inv = t_inv + _dot(t_inv, q)
            span *= 2
            if span >= c:
                break
            q = _dot(q, q)
        vh = v[:, hs]
        tx = _dot(t_inv, jnp.concatenate([kap_s[:, hs], _dot(m1, vh)], axis=1))
        st = st_ref[h]
        u = _dot_nt(tx[:, 0:hd], st) + tx[:, hd:2 * hd]
        y = _dot_nt(r_s[:, hs], st) + _dot(n1, vh) - _dot(n2, u)
        y = y + jnp.sum(bonus_w[:, hs], axis=1, keepdims=True) * vh
        ys.append(y)
        st_ref[h] = st * gam_tot[:, hs] + _dot_tn(vh, k_e[:, hs]) - _dot_tn(u, b_e[:, hs])
    y_ref[0] = jnp.concatenate(ys, axis=1)


def _rwkv(zr, mu, kk_scale, k_a, r_k, w0, w2_pad, a0, a2_pad, g2, batch, nc_ctx, nc_tot):
    t, zw_ = zr.shape
    c = SEQ_CHUNK
    n = RWKV_DIM
    bd = _block_ones(RWKV_HEADS, HEAD_DIM)
    full = lambda a: pl.BlockSpec(a.shape, lambda d, b, s: (0,) * a.ndim)
    dirn = lambda a: pl.BlockSpec((1,) + a.shape[1:], lambda d, b, s: (d,) + (0,) * (a.ndim - 1))

    def rows(d, b, s):
        return b * nc_tot + _chunk_pos(s, d, nc_ctx, nc_tot)

    sub = c // 8
    n_sub = t // 8
    vec = lambda a: a.reshape(1, -1)
    args = (vec(mu), vec(kk_scale), vec(k_a), vec(r_k), w0.reshape(2, 1, n), w2_pad, a0.reshape(2, 1, n), a2_pad, g2, bd)
    return pl.pallas_call(
        functools.partial(_rwkv_kernel, nc_ctx=nc_ctx, nc_tot=nc_tot),
        grid=(2, batch, nc_tot),
        in_specs=[pl.BlockSpec((c, zw_), lambda d, b, s: (rows(d, b, s), 0)),
                  pl.BlockSpec((8, zw_), lambda d, b, s: (jnp.maximum(rows(d, b, s) * sub - 1, 0), 0)),
                  pl.BlockSpec((8, zw_), lambda d, b, s: (jnp.minimum((rows(d, b, s) + 1) * sub, n_sub - 1), 0)),
                  full(args[0]), full(args[1]), full(args[2]), full(args[3]),
                  dirn(args[4]), dirn(args[5]), dirn(args[6]), dirn(args[7]), full(args[8]), full(args[9])],
        out_specs=[pl.BlockSpec((1, c, n), lambda d, b, s: (d, rows(d, b, s), 0)),
                   pl.BlockSpec((1, c, n), lambda d, b, s: (d, rows(d, b, s), 0))],
        out_shape=[jax.ShapeDtypeStruct((2, t, n), F32), jax.ShapeDtypeStruct((2, t, n), F32)],
        scratch_shapes=[pltpu.VMEM((RWKV_HEADS, HEAD_DIM, HEAD_DIM), F32)],
        compiler_params=_cparams(("parallel", "parallel", "arbitrary")),
        name="rwkv_scan",
    )(zr, zr, zr, *args)


def _merge_odd_kernel(y0_ref, y1_ref, gate_ref, fo_ref, x_ref, mod_ref, lg_ref, lb_ref, bd_ref, w1_ref, w2_ref,
                      fg_ref, rw_ref, rb_ref, xo_ref, ho_ref, go_ref):
    y = y0_ref[0] + y1_ref[0]
    bd = bd_ref[...]
    mean = _dot(y, bd) * (1.0 / HEAD_DIM)
    yc = y - mean
    var = _dot(yc * yc, bd) * (1.0 / HEAD_DIM)
    rw = (yc * lax.rsqrt(var + RWKV_GN_EPS) * lg_ref[...] + lb_ref[...]) * gate_ref[0]
    out = _dot(fo_ref[...].astype(BF16), w1_ref[...]) + _dot(rw.astype(BF16), w2_ref[...])
    _residual_and_route(x_ref[...], out, mod_ref[0], fg_ref[...], rw_ref, rb_ref, xo_ref, ho_ref, go_ref)


def _merge_odd(y, gate, fo, x, mods, ln_g, ln_b, w_out, ffn_gain, router_wt, router_b, tm, batch, l_ctx, l_lat):
    d = x.shape[1]
    n = RWKV_DIM
    fw = FOURIER_WIDTH
    l_tot = l_ctx + l_lat
    nt = l_lat // tm
    t_out = batch * l_lat
    full = lambda a: pl.BlockSpec(a.shape, lambda b, i: (0,) * a.ndim)
    src = lambda b, i: b * (l_tot // tm) + l_ctx // tm + i
    bd = _block_ones(RWKV_HEADS, HEAD_DIM)
    w1, w2 = w_out[:fw].astype(BF16), w_out[fw:].astype(BF16)
    lg, lb, fg, rb = ln_g.reshape(1, n), ln_b.reshape(1, n), ffn_gain.reshape(1, d), router_b.reshape(N_EXPERTS, 1)
    return pl.pallas_call(
        _merge_odd_kernel,
        grid=(batch, nt),
        in_specs=[pl.BlockSpec((1, tm, n), lambda b, i: (0, src(b, i), 0)),
                  pl.BlockSpec((1, tm, n), lambda b, i: (1, src(b, i), 0)),
                  pl.BlockSpec((1, tm, n), lambda b, i: (0, src(b, i), 0)),
                  pl.BlockSpec((tm, fw), lambda b, i: (i, b)),
                  pl.BlockSpec((tm, d), lambda b, i: (src(b, i), 0)),
                  pl.BlockSpec((1, 6, d), lambda b, i: (b, 0, 0)),
                  full(lg), full(lb), full(bd), full(w1), full(w2), full(fg), full(router_wt), full(rb)],
        out_specs=[pl.BlockSpec((tm, d), lambda b, i: (b * nt + i, 0)),
                   pl.BlockSpec((tm, d), lambda b, i: (b * nt + i, 0)),
                   pl.BlockSpec((tm, N_EXPERTS), lambda b, i: (b * nt + i, 0))],
        out_shape=[jax.ShapeDtypeStruct((t_out, d), F32),
                   jax.ShapeDtypeStruct((t_out, d), BF16),
                   jax.ShapeDtypeStruct((t_out, N_EXPERTS), F32)],
        compiler_params=_cparams(("parallel", "parallel")),
        name="merge_odd",
    )(y, y, gate, fo, x, mods, lg, lb, bd, w1, w2, fg, router_wt, rb)


def _rope_tables(l_ctx, l_lat):
    rows = l_lat // GRID_W
    row = jnp.repeat(jnp.arange(rows, dtype=F32), GRID_W)
    col = jnp.tile(jnp.arange(GRID_W, dtype=F32), rows)
    n_freq = HEAD_DIM // 4
    inv_freq = ROPE_THETA ** (-jnp.arange(n_freq, dtype=F32) / n_freq)
    ang = jnp.concatenate([row[:, None] * inv_freq, col[:, None] * inv_freq], axis=-1)
    cos, sin = jnp.cos(ang), jnp.sin(ang)
    cos64 = jnp.concatenate([cos, cos], axis=1)
    sin64 = jnp.concatenate([-sin, sin], axis=1)
    cos64 = jnp.concatenate([jnp.ones((l_ctx, HEAD_DIM), F32), cos64], axis=0)
    sin64 = jnp.concatenate([jnp.zeros((l_ctx, HEAD_DIM), F32), sin64], axis=0)
    return cos64, sin64


def _pad_rank(w):
    _, r, n = w.shape
    out = jnp.zeros((2, RWKV_RANK_PAD, n), w.dtype)
    out = out.at[0, 0:r].set(w[0])
    return out.at[1, r:2 * r].set(w[1])


def _even_layer(x, mods, p, batch, l_ctx, l_lat, tm, mod_row, tm_moe):
    l_tot = l_ctx + l_lat
    d = x.shape[1]
    nc_ctx, nc_tot = l_ctx // SEQ_CHUNK, l_tot // SEQ_CHUNK
    hk, hv = GLA_HEADS * GLA_DK, GLA_HEADS * GLA_DV
    qw, kw = ATT_HEADS * HEAD_DIM, ATT_KV_HEADS * HEAD_DIM
    w_in = p['w_in']
    o = np.cumsum([0, hk, hk, hv, hv, 2 * GLA_LOWRANK, qw, kw, kw])
    w_gla = jnp.concatenate([w_in[:, o[0]:o[4]]], axis=1).astype(BF16)
    w_dec = jnp.pad(w_in[:, o[4]:o[5]], ((0, 0), (0, 128 - 2 * GLA_LOWRANK))).astype(BF16)
    w_q = w_in[:, o[5]:o[6]].astype(BF16)
    w_kv = w_in[:, o[6]:o[8]].astype(BF16)
    zg, zdec, zq, zkv = _project(x, mods, p['norm_mix'], [w_gla, w_dec, w_q, w_kv], tm, mod_row)

    dec_w_pad = _pad_rank(p['dec_w'])
    o_gla = _gla(zg, zdec, dec_w_pad, p['dec_b'].reshape(2, 1, hk), batch, nc_ctx, nc_tot)

    cos64, sin64 = _rope_tables(l_ctx, l_lat)
    qn, kn, vn = _qk_prep(zq, zkv, cos64, sin64, p['q_norm'], p['k_norm'], tm, l_tot)
    o_att = _attention(qn, kn, vn, p['sink'], batch, l_ctx, l_tot)

    x1, h, gates = _merge_even(o_gla, zg, o_att, x, mods, p['out_norm'], p['w_out'], p['norm_ffn'],
                               p['router_wt'], p['router_b'], tm, mod_row)
    return _moe(h, gates, x1, mods, p['moe_g'], p['moe_u'], p['moe_d'], tm, mod_row)


def _odd_layer(x, mods, p, batch, l_ctx, l_lat, tm, mod_row, tm_moe):
    l_tot = l_ctx + l_lat
    nc_ctx, nc_tot = l_ctx // SEQ_CHUNK, l_tot // SEQ_CHUNK
    n = RWKV_DIM
    fw = FOURIER_WIDTH
    w_in = p['w_in']
    rank_w, rank_a = p['w2'].shape[1], p['a2'].shape[1]
    o = np.cumsum([0, fw, n, n, n, 2 * rank_w, 2 * rank_a])
    pad_cols = lambda w: jnp.pad(w, ((0, 0), (0, RWKV_RANK_PAD - w.shape[1])))
    w_f = w_in[:, o[0]:o[1]].astype(BF16)
    w_r = jnp.concatenate([w_in[:, o[1]:o[4]], pad_cols(w_in[:, o[4]:o[5]]), pad_cols(w_in[:, o[5]:o[6]]),
                           w_in[:, o[6]:]], axis=1).astype(BF16)
    zf, zr = _project(x, mods, p['norm_mix'], [w_f, w_r], tm, mod_row)

    mu = p['mu']
    mu_r = jnp.concatenate([mu[0:3 * n], pad_cols(mu[None, 3 * n:3 * n + 2 * rank_w])[0],
                            pad_cols(mu[None, 3 * n + 2 * rank_w:3 * n + 2 * rank_w + 2 * rank_a])[0],
                            mu[3 * n + 2 * rank_w + 2 * rank_a:]])
    y, gate = _rwkv(zr, mu_r, p['kk_scale'], p['k_a'], p['r_k'].reshape(-1), p['w0'], _pad_rank(p['w2']),
                    p['a0'], _pad_rank(p['a2']), p['g2'], batch, nc_ctx, nc_tot)

    table, chan = _dft_tables(l_lat)
    zc = _chan_dft(zf, chan, batch, l_ctx, l_lat, tm)
    fo = _seq_dft(table, zc, batch, min(512, l_lat), min(1024, l_lat))

    x1, h, gates = _merge_odd(y, gate, fo, x, mods, p['ln_g'], p['ln_b'], p['w_out'], p['norm_ffn'],
                              p['router_wt'], p['router_b'], tm, batch, l_ctx, l_lat)
    lat_tiles = l_lat // tm_moe
    return _moe(h, gates, x1, mods, p['moe_g'], p['moe_u'], p['moe_d'], tm_moe, lambda i: i // lat_tiles)


def kernel(x, c, ctx, c_ctx, ada_w, ada_b, norm_mix, norm_ffn, even_w_in, even_w_out, gla_dec_w, gla_dec_b, gla_out_norm, att_q_norm, att_k_norm, att_sink, odd_w_in, odd_w_out, rwkv_mu, rwkv_w0, rwkv_w2, rwkv_a0, rwkv_a2, rwkv_g2, rwkv_kk_scale, rwkv_k_a, rwkv_r_k, rwkv_ln_g, rwkv_ln_b, router_w, router_b, moe_w_gate, moe_w_up, moe_w_down):
    batch, l_lat, d = x.shape
    l_ctx = ctx.shape[1]
    l_tot = l_ctx + l_lat
    assert batch < 8 and ada_w.shape[0] == 2
    tm = 256 if (l_ctx % 256 == 0 and l_lat % 256 == 0) else 128
    tm_moe = 512 if (l_lat % 512 == 0 and tm == 256) else tm
    assert l_ctx % tm == 0 and l_lat % tm == 0 and l_tot % l_ctx == 0 and l_lat % GRID_W == 0

    xs = jnp.concatenate([ctx, x], axis=1).reshape(batch * l_tot, d)
    cc = jnp.concatenate([c, c_ctx[None, :], jnp.zeros((8 - batch - 1, d), F32)], axis=0)
    tiles_per_b = l_tot // tm
    ctx_tiles = l_ctx // tm

    def mod_row(i):
        return jnp.where(i % tiles_per_b < ctx_tiles, batch, i // tiles_per_b)

    router_wt = router_w.T
    bf = lambda w: w.astype(BF16)
    moe = lambda layer: dict(moe_g=bf(moe_w_gate[layer]), moe_u=bf(moe_w_up[layer]), moe_d=bf(moe_w_down[layer]))

    mods0 = _modvec(cc, ada_w[0], ada_b[0])
    p0 = dict(w_in=even_w_in[0], w_out=even_w_out[0], dec_w=gla_dec_w[0], dec_b=gla_dec_b[0],
              out_norm=gla_out_norm[0], q_norm=att_q_norm[0], k_norm=att_k_norm[0], sink=att_sink[0],
              norm_mix=norm_mix[0], norm_ffn=norm_ffn[0], router_wt=router_wt, router_b=router_b, **moe(0))
    xs = _even_layer(xs, mods0, p0, batch, l_ctx, l_lat, tm, mod_row, tm_moe)

    mods1 = _modvec(cc, ada_w[1], ada_b[1])
    p1 = dict(w_in=odd_w_in[0], w_out=odd_w_out[0], mu=rwkv_mu[0], w0=rwkv_w0[0], w2=rwkv_w2[0], a0=rwkv_a0[0],
              a2=rwkv_a2[0], g2=rwkv_g2[0], kk_scale=rwkv_kk_scale[0], k_a=rwkv_k_a[0], r_k=rwkv_r_k[0],
              ln_g=rwkv_ln_g[0], ln_b=rwkv_ln_b[0], norm_mix=norm_mix[1], norm_ffn=norm_ffn[1],
              router_wt=router_wt, router_b=router_b, **moe(1))
    out = _odd_layer(xs, mods1, p1, batch, l_ctx, l_lat, tm, mod_row, tm_moe)
    return out.reshape(batch, l_lat, d)
```

```python
import functools

import jax
import jax.numpy as jnp
import numpy as np
from jax import lax
from jax.experimental import pallas as pl
from jax.experimental.pallas import tpu as pltpu

F32 = jnp.float32
BF16 = jnp.bfloat16

GRID_W = 64
HEAD_DIM = 64
NORM_EPS = 1e-6
L2_EPS = 1e-12

GLA_DV = 64
GLA_DK = 32
GLA_HEADS = 8
GLA_LOWRANK = 16
GLA_TAU = 16.0

ATT_HEADS = 8
ATT_KV_HEADS = 2
ATT_GROUP = ATT_HEADS // ATT_KV_HEADS
ATT_BLOCK = 128
ROPE_THETA = 10000.0

FOURIER_GROUP_DIM = 64
FOURIER_WIDTH = 256

RWKV_DIM = 768
RWKV_HEADS = 12
RWKV_RANK_PAD = 128
RWKV_GN_EPS = 64e-5

N_EXPERTS = 16
N_GROUPS = 4
PER_GROUP = N_EXPERTS // N_GROUPS
D_EXPERT = 512

SEQ_CHUNK = 64
VMEM_LIMIT = 56 * 1024 * 1024


def _cparams(sem):
    return pltpu.CompilerParams(dimension_semantics=sem, vmem_limit_bytes=VMEM_LIMIT)


def _dot(a, b):
    return jnp.dot(a, b, preferred_element_type=F32)


def _dot_nt(a, b):
    return lax.dot_general(a, b, (((1,), (1,)), ((), ())), preferred_element_type=F32)


def _dot_tn(a, b):
    return lax.dot_general(a, b, (((0,), (0,)), ((), ())), preferred_element_type=F32)


def _silu(x):
    return x * jax.nn.sigmoid(x)


def _log_sigmoid(x):
    return jnp.minimum(x, 0.0) - jnp.log(1.0 + jnp.exp(-jnp.abs(x)))


def _modulated_norm(x, gain, shift, scale):
    ms = jnp.mean(x * x, axis=-1, keepdims=True)
    return (x * lax.rsqrt(ms + NORM_EPS) * gain) * (1.0 + scale) + shift


def _block_ones(n_blocks, width):
    return jnp.kron(jnp.eye(n_blocks, dtype=F32), jnp.ones((width, width), F32))


def _modvec_kernel(c_ref, w_ref, b_ref, o_ref):
    o_ref[...] = _dot(_silu(c_ref[...]), w_ref[...]) + b_ref[...]


def _modvec(cc, w, b):
    d = cc.shape[1]
    n = w.shape[1]
    tn = n // 4
    out = pl.pallas_call(
        _modvec_kernel,
        grid=(n // tn,),
        in_specs=[pl.BlockSpec((8, d), lambda j: (0, 0)),
                  pl.BlockSpec((d, tn), lambda j: (0, j)),
                  pl.BlockSpec((1, tn), lambda j: (0, j))],
        out_specs=pl.BlockSpec((8, tn), lambda j: (0, j)),
        out_shape=jax.ShapeDtypeStruct((8, n), F32),
        compiler_params=_cparams(("parallel",)),
        name="modvec",
    )(cc, w, b.reshape(1, n))
    return out.reshape(8, 6, d)


def _proj_kernel(x_ref, mod_ref, gain_ref, *refs, n_out):
    w_refs, z_refs = refs[:n_out], refs[n_out:]
    m = mod_ref[0]
    h = _modulated_norm(x_ref[...], gain_ref[...], m[0:1], m[1:2]).astype(BF16)
    for w_ref, z_ref in zip(w_refs, z_refs):
        z_ref[...] = _dot(h, w_ref[...])


def _project(x, mods, gain, weights, tm, mod_row):
    t, d = x.shape
    n_out = len(weights)
    in_specs = [pl.BlockSpec((tm, d), lambda i: (i, 0)),
                pl.BlockSpec((1, 6, d), lambda i: (mod_row(i), 0, 0)),
                pl.BlockSpec((1, d), lambda i: (0, 0))]
    in_specs += [pl.BlockSpec(w.shape, lambda i: (0, 0)) for w in weights]
    return pl.pallas_call(
        functools.partial(_proj_kernel, n_out=n_out),
        grid=(t // tm,),
        in_specs=in_specs,
        out_specs=[pl.BlockSpec((tm, w.shape[1]), lambda i: (i, 0)) for w in weights],
        out_shape=[jax.ShapeDtypeStruct((t, w.shape[1]), F32) for w in weights],
        compiler_params=_cparams(("parallel",)),
        name="proj",
    )(x, mods, gain.reshape(1, d), *weights)


def _chunk_pos(s, d, nc_ctx, nc_tot):
    back = jnp.where(s < nc_ctx, nc_ctx - 1 - s, nc_tot + nc_ctx - 1 - s)
    return jnp.where(d == 0, s, back)


def _order_masks(d, c):
    row = lax.broadcasted_iota(jnp.int32, (c, c), 0)
    col = lax.broadcasted_iota(jnp.int32, (c, c), 1)
    ahead = (row - col) * jnp.where(d == 0, 1, -1)
    return ahead >= 0, ahead > 0


def _gla_kernel(q_ref, k_ref, v_ref, dec_ref, dw_ref, db_ref, o_ref, st_ref):
    d = pl.program_id(0)
    c = q_ref.shape[0]

    @pl.when(pl.program_id(2) == 0)
    def _():
        st_ref[...] = jnp.zeros_like(st_ref)

    g = _log_sigmoid(_dot(dec_ref[...], dw_ref[0]) + db_ref[0]) / GLA_TAU
    incl, _ = _order_masks(d, c)
    b = _dot(incl.astype(F32), g)
    b_tot = jnp.sum(g, axis=0, keepdims=True)
    q_in = q_ref[...] * (GLA_DK ** -0.5) * jnp.exp(b)
    k_out = k_ref[...] * jnp.exp(-b)
    k_end = k_ref[...] * jnp.exp(b_tot - b)
    decay_tot = jnp.exp(b_tot)
    v = v_ref[...]
    heads = range(GLA_HEADS)
    ks = [slice(h * GLA_DK, (h + 1) * GLA_DK) for h in heads]
    vs = [slice(h * GLA_DV, (h + 1) * GLA_DV) for h in heads]
    att = [jnp.where(incl, _dot_nt(q_in[:, s_], k_out[:, s_]), 0.0) for s_ in ks]
    st = [st_ref[h] for h in heads]
    o_ref[0] = jnp.concatenate([_dot(att[h], v[:, vs[h]]) + _dot_nt(q_in[:, ks[h]], st[h]) for h in heads], axis=1)
    for h in heads:
        st_ref[h] = st[h] * decay_tot[:, ks[h]] + _dot_tn(v[:, vs[h]], k_end[:, ks[h]])


def _gla(zg, zdec, dec_w_pad, dec_b, batch, nc_ctx, nc_tot):
    t = zg.shape[0]
    c = SEQ_CHUNK
    hk, hv = GLA_HEADS * GLA_DK, GLA_HEADS * GLA_DV

    def rows(d, b, s):
        return b * nc_tot + _chunk_pos(s, d, nc_ctx, nc_tot)

    return pl.pallas_call(
        _gla_kernel,
        grid=(2, batch, nc_tot),
        in_specs=[pl.BlockSpec((c, hk), lambda d, b, s: (rows(d, b, s), 0)),
                  pl.BlockSpec((c, hk), lambda d, b, s: (rows(d, b, s), 1)),
                  pl.BlockSpec((c, hv), lambda d, b, s: (rows(d, b, s), 1)),
                  pl.BlockSpec((c, 128), lambda d, b, s: (rows(d, b, s), 0)),
                  pl.BlockSpec((1, 128, hk), lambda d, b, s: (d, 0, 0)),
                  pl.BlockSpec((1, 1, hk), lambda d, b, s: (d, 0, 0))],
        out_specs=pl.BlockSpec((1, c, hv), lambda d, b, s: (d, rows(d, b, s), 0)),
        out_shape=jax.ShapeDtypeStruct((2, t, hv), F32),
        scratch_shapes=[pltpu.VMEM((GLA_HEADS, GLA_DV, GLA_DK), F32)],
        compiler_params=_cparams(("parallel", "parallel", "arbitrary")),
        name="gla_scan",
    )(zg, zg, zg, zdec, dec_w_pad, dec_b)


def _rope_swap(x):
    n = x.shape[-1]
    lane = lax.broadcasted_iota(jnp.int32, x.shape, x.ndim - 1)
    half = HEAD_DIM // 2
    return jnp.where(lane % HEAD_DIM < half, pltpu.roll(x, n - half, x.ndim - 1), pltpu.roll(x, half, x.ndim - 1))


def _qk_prep_kernel(q_ref, kv_ref, cos_ref, sin_ref, qg_ref, kg_ref, bdq_ref, bdk_ref, qo_ref, ko_ref, vo_ref):
    def norm_rope(x, gain, bd, n_heads):
        ms = _dot(x * x, bd) * (1.0 / HEAD_DIM)
        xn = x * lax.rsqrt(ms + NORM_EPS) * gain
        cos = jnp.concatenate([cos_ref[...]] * n_heads, axis=1)
        sin = jnp.concatenate([sin_ref[...]] * n_heads, axis=1)
        return xn * cos + _rope_swap(xn) * sin

    q = norm_rope(q_ref[...], qg_ref[...], bdq_ref[...], ATT_HEADS)
    qo_ref[...] = (q * (HEAD_DIM ** -0.5)).astype(BF16)
    kw = ATT_KV_HEADS * HEAD_DIM
    kv = kv_ref[...]
    ko_ref[...] = norm_rope(kv[:, :kw], kg_ref[...], bdk_ref[...], ATT_KV_HEADS).astype(BF16)
    vo_ref[...] = kv[:, kw:].astype(BF16)


def _qk_prep(zq, zkv, cos64, sin64, q_gain, k_gain, tm, l_tot):
    t = zq.shape[0]
    qw, kw = ATT_HEADS * HEAD_DIM, ATT_KV_HEADS * HEAD_DIM
    n_pos = l_tot // tm
    return pl.pallas_call(
        _qk_prep_kernel,
        grid=(t // tm,),
        in_specs=[pl.BlockSpec((tm, qw), lambda i: (i, 0)),
                  pl.BlockSpec((tm, 2 * kw), lambda i: (i, 0)),
                  pl.BlockSpec((tm, HEAD_DIM), lambda i: (i % n_pos, 0)),
                  pl.BlockSpec((tm, HEAD_DIM), lambda i: (i % n_pos, 0)),
                  pl.BlockSpec((1, qw), lambda i: (0, 0)),
                  pl.BlockSpec((1, kw), lambda i: (0, 0)),
                  pl.BlockSpec((qw, qw), lambda i: (0, 0)),
                  pl.BlockSpec((kw, kw), lambda i: (0, 0))],
        out_specs=[pl.BlockSpec((tm, qw), lambda i: (i, 0)),
                   pl.BlockSpec((tm, kw), lambda i: (i, 0)),
                   pl.BlockSpec((tm, kw), lambda i: (i, 0))],
        out_shape=[jax.ShapeDtypeStruct((t, qw), BF16),
                   jax.ShapeDtypeStruct((t, kw), BF16),
                   jax.ShapeDtypeStruct((t, kw), BF16)],
        compiler_params=_cparams(("parallel",)),
        name="qk_prep",
    )(zq, zkv, cos64, sin64, jnp.tile(q_gain, ATT_HEADS).reshape(1, qw), jnp.tile(k_gain, ATT_KV_HEADS).reshape(1, kw),
      _block_ones(ATT_HEADS, HEAD_DIM), _block_ones(ATT_KV_HEADS, HEAD_DIM))


def _attn_kernel(q_ref, kp_ref, kc_ref, kn_ref, kx_ref, vp_ref, vc_ref, vn_ref, vx_ref, sink_ref, o_ref,
                 *, n_ctx_blocks, n_lat_blocks):
    blk = ATT_BLOCK
    n = pl.program_id(1)
    m = n - n_ctx_blocks
    is_lat = n >= n_ctx_blocks
    l_ctx = kx_ref.shape[0]
    width = 3 * blk + l_ctx
    rows = ATT_GROUP * blk
    r = lax.broadcasted_iota(jnp.int32, (rows, width), 0) % blk
    c = lax.broadcasted_iota(jnp.int32, (rows, width), 1)
    lat = is_lat.astype(jnp.int32)
    has_prev = lat * (m >= 1).astype(jnp.int32)
    has_next = lat * (m <= n_lat_blocks - 2).astype(jnp.int32)
    valid = jnp.where(c < blk, (c >= r).astype(jnp.int32) * has_prev,
                      jnp.where(c < 2 * blk, lat,
                                jnp.where(c < 3 * blk, (c - 2 * blk <= r).astype(jnp.int32) * has_next, 1))) > 0
    q = q_ref[...]
    sink = sink_ref[...]
    outs = [None] * ATT_HEADS
    for kvh in range(ATT_KV_HEADS):
        ks = slice(kvh * HEAD_DIM, (kvh + 1) * HEAD_DIM)
        kw = jnp.concatenate([kp_ref[:, ks], kc_ref[:, ks], kn_ref[:, ks], kx_ref[:, ks]], axis=0)
        vw = jnp.concatenate([vp_ref[:, ks], vc_ref[:, ks], vn_ref[:, ks], vx_ref[:, ks]], axis=0)
        heads = range(kvh * ATT_GROUP, (kvh + 1) * ATT_GROUP)
        qg = jnp.concatenate([q[:, h * HEAD_DIM:(h + 1) * HEAD_DIM] for h in heads], axis=0)
        s = jnp.where(valid, _dot_nt(qg, kw), -jnp.inf)
        sk = jnp.concatenate([jnp.broadcast_to(sink[h:h + 1, 0:1], (blk, 1)) for h in heads], axis=0)
        mx = jnp.maximum(jnp.max(s, axis=-1, keepdims=True), sk)
        p = jnp.exp(s - mx)
        denom = jnp.sum(p, axis=-1, keepdims=True) + jnp.exp(sk - mx)
        o = _dot(p.astype(BF16), vw) / denom
        for g, h in enumerate(heads):
            outs[h] = o[g * blk:(g + 1) * blk]
    o_ref[...] = jnp.concatenate(outs, axis=1)


def _attention(qn, kn, vn, sink, batch, l_ctx, l_tot):
    t = qn.shape[0]
    blk = ATT_BLOCK
    nq = l_tot // blk
    nc = l_ctx // blk
    nl = nq - nc
    qw, kw = ATT_HEADS * HEAD_DIM, ATT_KV_HEADS * HEAD_DIM

    def win(off):
        def index(b, n):
            m = jnp.clip(n - nc + off, 0, nl - 1)
            return (b * nq + nc + m, 0)
        return pl.BlockSpec((blk, kw), index)

    ctx_spec = pl.BlockSpec((l_ctx, kw), lambda b, n: (b * (l_tot // l_ctx), 0))
    return pl.pallas_call(
        functools.partial(_attn_kernel, n_ctx_blocks=nc, n_lat_blocks=nl),
        grid=(batch, nq),
        in_specs=[pl.BlockSpec((blk, qw), lambda b, n: (b * nq + n, 0)),
                  win(-1), win(0), win(1), ctx_spec,
                  win(-1), win(0), win(1), ctx_spec,
                  pl.BlockSpec((ATT_HEADS, 128), lambda b, n: (0, 0))],
        out_specs=pl.BlockSpec((blk, qw), lambda b, n: (b * nq + n, 0)),
        out_shape=jax.ShapeDtypeStruct((t, qw), F32),
        compiler_params=_cparams(("parallel", "parallel")),
        name="window_attention",
    )(qn, kn, kn, kn, kn, vn, vn, vn, vn, jnp.broadcast_to(sink.astype(F32)[:, None], (ATT_HEADS, 128)))


def _route(logits_t, bias_col):
    scores = jax.nn.sigmoid(logits_t)
    sel = scores + bias_col
    rows = [sel[e:e + 1] for e in range(N_EXPERTS)]
    grp = []
    for g in range(N_GROUPS):
        r = rows[g * PER_GROUP:(g + 1) * PER_GROUP]
        best = None
        for i in range(PER_GROUP):
            for j in range(i + 1, PER_GROUP):
                pair = r[i] + r[j]
                best = pair if best is None else jnp.maximum(best, pair)
        grp.append(best)
    g_best = jnp.zeros_like(grp[0], dtype=jnp.int32)
    g_val = grp[0]
    for g in range(1, N_GROUPS):
        take = grp[g] > g_val
        g_best = jnp.where(take, g, g_best)
        g_val = jnp.where(take, grp[g], g_val)
    neg = -jnp.inf
    masked = [jnp.where(g_best == e // PER_GROUP, rows[e], neg) for e in range(N_EXPERTS)]
    i1 = jnp.zeros_like(g_best)
    v1 = masked[0]
    for e in range(1, N_EXPERTS):
        take = masked[e] > v1
        i1 = jnp.where(take, e, i1)
        v1 = jnp.where(take, masked[e], v1)
    i2 = jnp.full_like(g_best, -1)
    v2 = jnp.full_like(v1, neg)
    for e in range(N_EXPERTS):
        take = jnp.logical_and(i1 != e, masked[e] > v2)
        i2 = jnp.where(take, e, i2)
        v2 = jnp.where(take, masked[e], v2)
    w1 = jnp.zeros_like(v1)
    w2 = jnp.zeros_like(v1)
    for e in range(N_EXPERTS):
        w1 = jnp.where(i1 == e, scores[e:e + 1], w1)
        w2 = jnp.where(i2 == e, scores[e:e + 1], w2)
    inv = 1.0 / (w1 + w2)
    gates = [jnp.where(i1 == e, w1 * inv, 0.0) + jnp.where(i2 == e, w2 * inv, 0.0) for e in range(N_EXPERTS)]
    return jnp.concatenate(gates, axis=0)


def _residual_and_route(x, out, m, ffn_gain, rw_ref, rb_ref, x_ref, h_ref, g_ref):
    x1 = x + m[2:3] * out
    x_ref[...] = x1
    h = _modulated_norm(x1, ffn_gain, m[3:4], m[4:5])
    h_ref[...] = h.astype(BF16)
    logits_t = lax.dot_general(rw_ref[...], h, (((1,), (1,)), ((), ())), precision=lax.Precision.HIGHEST,
                               preferred_element_type=F32)
    g_ref[...] = _route(logits_t, rb_ref[...]).T


def _merge_even_kernel(o0_ref, o1_ref, gg_ref, oa_ref, x_ref, mod_ref, gn_ref, bd_ref, w1_ref, w2_ref,
                       fg_ref, rw_ref, rb_ref, xo_ref, ho_ref, go_ref):
    og = o0_ref[0] + o1_ref[0]
    ms = _dot(og * og, bd_ref[...]) * (1.0 / GLA_DV)
    g = og * lax.rsqrt(ms + NORM_EPS) * gn_ref[...] * _silu(gg_ref[...])
    out = _dot(g.astype(BF16), w1_ref[...]) + _dot(oa_ref[...].astype(BF16), w2_ref[...])
    _residual_and_route(x_ref[...], out, mod_ref[0], fg_ref[...], rw_ref, rb_ref, xo_ref, ho_ref, go_ref)


def _merge_even(o_gla, zg, o_att, x, mods, out_norm, w_out, ffn_gain, router_wt, router_b, tm, mod_row):
    t, d = x.shape
    hv = GLA_HEADS * GLA_DV
    qw = ATT_HEADS * HEAD_DIM
    full = lambda a: pl.BlockSpec(a.shape, lambda i: (0,) * a.ndim)
    gn = jnp.tile(out_norm, GLA_HEADS).reshape(1, hv)
    bd = _block_ones(GLA_HEADS, GLA_DV)
    w1, w2 = w_out[:hv].astype(BF16), w_out[hv:].astype(BF16)
    fg = ffn_gain.reshape(1, d)
    rb = router_b.reshape(N_EXPERTS, 1)
    return pl.pallas_call(
        _merge_even_kernel,
        grid=(t // tm,),
        in_specs=[pl.BlockSpec((1, tm, hv), lambda i: (0, i, 0)),
                  pl.BlockSpec((1, tm, hv), lambda i: (1, i, 0)),
                  pl.BlockSpec((tm, hv), lambda i: (i, 2)),
                  pl.BlockSpec((tm, qw), lambda i: (i, 0)),
                  pl.BlockSpec((tm, d), lambda i: (i, 0)),
                  pl.BlockSpec((1, 6, d), lambda i: (mod_row(i), 0, 0)),
                  full(gn), full(bd), full(w1), full(w2), full(fg), full(router_wt), full(rb)],
        out_specs=[pl.BlockSpec((tm, d), lambda i: (i, 0)),
                   pl.BlockSpec((tm, d), lambda i: (i, 0)),
                   pl.BlockSpec((tm, N_EXPERTS), lambda i: (i, 0))],
        out_shape=[jax.ShapeDtypeStruct((t, d), F32),
                   jax.ShapeDtypeStruct((t, d), BF16),
                   jax.ShapeDtypeStruct((t, N_EXPERTS), F32)],
        compiler_params=_cparams(("parallel",)),
        name="merge_even",
    )(o_gla, o_gla, zg, o_att, x, mods, gn, bd, w1, w2, fg, router_wt, rb)


def _moe_kernel(h_ref, g_ref, x_ref, mod_ref, wg_ref, wu_ref, wd_ref, o_ref, acc_ref):
    e = pl.program_id(1)

    @pl.when(e == 0)
    def _():
        acc_ref[...] = jnp.zeros_like(acc_ref)

    h = h_ref[...]
    act = _silu(_dot(h, wg_ref[0])) * _dot(h, wu_ref[0])
    gates = g_ref[...]
    lane = lax.broadcasted_iota(jnp.int32, gates.shape, 1)
    gate = jnp.sum(jnp.where(lane == e, gates, 0.0), axis=1, keepdims=True)
    acc_ref[...] += _dot((act * gate).astype(BF16), wd_ref[0])

    @pl.when(e == N_EXPERTS - 1)
    def _():
        o_ref[...] = x_ref[...] + mod_ref[0][5:6] * acc_ref[...]


def _moe(h, gates, x, mods, w_gate, w_up, w_down, tm, mod_row):
    t, d = x.shape
    return pl.pallas_call(
        _moe_kernel,
        grid=(t // tm, N_EXPERTS),
        in_specs=[pl.BlockSpec((tm, d), lambda i, e: (i, 0)),
                  pl.BlockSpec((tm, N_EXPERTS), lambda i, e: (i, 0)),
                  pl.BlockSpec((tm, d), lambda i, e: (i, 0)),
                  pl.BlockSpec((1, 6, d), lambda i, e: (mod_row(i), 0, 0)),
                  pl.BlockSpec((1, d, D_EXPERT), lambda i, e: (e, 0, 0)),
                  pl.BlockSpec((1, d, D_EXPERT), lambda i, e: (e, 0, 0)),
                  pl.BlockSpec((1, D_EXPERT, d), lambda i, e: (e, 0, 0))],
        out_specs=pl.BlockSpec((tm, d), lambda i, e: (i, 0)),
        out_shape=jax.ShapeDtypeStruct((t, d), F32),
        scratch_shapes=[pltpu.VMEM((tm, d), F32)],
        compiler_params=_cparams(("parallel", "arbitrary")),
        name="moe_experts",
    )(h, gates, x, mods, w_gate, w_up, w_down)


def _chan_dft_kernel(z_ref, w_ref, o_ref):
    res = _dot(z_ref[...].astype(BF16), w_ref[...]).astype(BF16)
    o_ref[0] = res[:, :FOURIER_WIDTH]
    o_ref[1] = res[:, FOURIER_WIDTH:]


def _chan_dft(zf, w, batch, l_ctx, l_lat, tm):
    l_tot = l_ctx + l_lat
    nt = l_lat // tm
    fw = FOURIER_WIDTH
    return pl.pallas_call(
        _chan_dft_kernel,
        grid=(batch, nt),
        in_specs=[pl.BlockSpec((tm, fw), lambda b, i: (b * (l_tot // tm) + l_ctx // tm + i, 0)),
                  pl.BlockSpec(w.shape, lambda b, i: (0, 0))],
        out_specs=pl.BlockSpec((2, tm, fw), lambda b, i: (0, i, b)),
        out_shape=jax.ShapeDtypeStruct((2, l_lat, batch * fw), BF16),
        compiler_params=_cparams(("parallel", "parallel")),
        name="fourier_channels",
    )(zf, w)


def _seq_dft_kernel(t_ref, z_ref, o_ref, acc_ref):
    k = pl.program_id(1)

    @pl.when(k == 0)
    def _():
        acc_ref[...] = jnp.zeros_like(acc_ref)

    acc_ref[...] += _dot(t_ref[...], z_ref[...])

    @pl.when(k == pl.num_programs(1) - 1)
    def _():
        o_ref[...] = acc_ref[...]


def _seq_dft(table, zc, batch, tm, tk):
    l = table.shape[0]
    fw = FOURIER_WIDTH
    nk = l // tk
    zst = zc.reshape(2 * l, batch * fw)
    return pl.pallas_call(
        _seq_dft_kernel,
        grid=(l // tm, 2 * nk),
        in_specs=[pl.BlockSpec((tm, tk), lambda i, k: (i, k)),
                  pl.BlockSpec((tk, batch * fw), lambda i, k: (k, 0))],
        out_specs=pl.BlockSpec((tm, batch * fw), lambda i, k: (i, 0)),
        out_shape=jax.ShapeDtypeStruct((l, batch * fw), F32),
        scratch_shapes=[pltpu.VMEM((tm, batch * fw), F32)],
        compiler_params=_cparams(("parallel", "arbitrary")),
        name="fourier_sequence",
    )(table, zst)


def _dft_tables(l):
    m = jnp.arange(l, dtype=jnp.int32)[:, None]
    n1 = l // 64
    a = (m * (jnp.arange(n1, dtype=jnp.int32)[None, :] * 64)) % l
    b = (m * jnp.arange(64, dtype=jnp.int32)[None, :]) % l
    wa = a.astype(F32) * (2.0 * np.pi / l)
    wb = b.astype(F32) * (2.0 * np.pi / l)
    ca, sa, cb, sb = jnp.cos(wa), jnp.sin(wa), jnp.cos(wb), jnp.sin(wb)
    cos_t = (ca[:, :, None] * cb[:, None, :] - sa[:, :, None] * sb[:, None, :]).reshape(l, l)
    sin_t = (sa[:, :, None] * cb[:, None, :] + ca[:, :, None] * sb[:, None, :]).reshape(l, l)
    table = jnp.concatenate([cos_t, sin_t], axis=1).astype(BF16)
    gd = FOURIER_GROUP_DIM
    cc = (jnp.arange(gd, dtype=jnp.int32)[:, None] * jnp.arange(gd, dtype=jnp.int32)[None, :]) % gd
    wc = cc.astype(F32) * (2.0 * np.pi / gd)
    scale = 1.0 / np.sqrt(float(l) * gd)
    eye = jnp.eye(FOURIER_WIDTH // gd, dtype=F32)
    chan = jnp.concatenate([jnp.kron(eye, jnp.cos(wc)), -jnp.kron(eye, jnp.sin(wc))], axis=1) * scale
    return table, chan.astype(BF16)


def _rwkv_kernel(z_ref, zp_ref, zn_ref, mu_ref, kks_ref, ka_ref, rk_ref, w0_ref, w2_ref, a0_ref, a2_ref, g2_ref,
                 bd_ref, y_ref, gate_ref, st_ref, *, nc_ctx, nc_tot):
    d = pl.program_id(0)
    s = pl.program_id(2)
    c = z_ref.shape[0]
    n = RWKV_DIM
    hd = HEAD_DIM

    @pl.when(s == 0)
    def _():
        st_ref[...] = jnp.zeros_like(st_ref)

    pos = _chunk_pos(s, d, nc_ctx, nc_tot)
    seg_first = jnp.logical_or(pos == 0, pos == nc_ctx)
    seg_last = jnp.logical_or(pos == nc_ctx - 1, pos == nc_tot - 1)
    z = z_ref[...]
    row = lax.broadcasted_iota(jnp.int32, z.shape, 0)
    prev_row = jnp.where(seg_first, 0.0, zp_ref[7:8, :])
    next_row = jnp.where(seg_last, 0.0, zn_ref[0:1, :])
    z_prev = jnp.where(row == 0, prev_row, pltpu.roll(z, 1, 0))
    z_next = jnp.where(row == c - 1, next_row, pltpu.roll(z, c - 1, 0))
    mu = mu_ref[...]
    zs = z + mu * (0.5 * (z_prev + z_next) - z)

    r, k, v = zs[:, 0:n], zs[:, n:2 * n], zs[:, 2 * n:3 * n]
    zw = zs[:, 3 * n:3 * n + RWKV_RANK_PAD]
    za = zs[:, 3 * n + RWKV_RANK_PAD:3 * n + 2 * RWKV_RANK_PAD]
    zg = zs[:, 3 * n + 2 * RWKV_RANK_PAD:3 * n + 3 * RWKV_RANK_PAD]

    kk = k * kks_ref[...]
    kk = kk * lax.rsqrt(_dot(kk * kk, bd_ref[...]) + L2_EPS)
    w_pre = w0_ref[0] + _dot(jnp.tanh(zw), w2_ref[0])
    w_log = _log_sigmoid(w_pre) - 0.5
    lw = -jnp.exp(w_log)
    a = jax.nn.sigmoid(a0_ref[0] + _dot(za, a2_ref[0]))
    kd = k * (1.0 + (a - 1.0) * ka_ref[...])
    beta = kk * a
    gate_ref[0] = _dot(jax.nn.sigmoid(zg), g2_ref[...])

    incl, strict = _order_masks(d, c)
    cl = _dot(incl.astype(F32), lw)
    c_tot = jnp.sum(lw, axis=0, keepdims=True)
    grow = jnp.exp(-cl)
    k_s = kd * grow
    b_s = beta * grow
    kap_s = kk * jnp.exp(cl - lw)
    r_s = r * jnp.exp(cl)
    tail = jnp.exp(c_tot - cl)
    k_e = kd * tail
    b_e = beta * tail
    gam_tot = jnp.exp(c_tot)
    bonus_w = r * kd * rk_ref[...]

    eye = (lax.broadcasted_iota(jnp.int32, (c, c), 0) == lax.broadcasted_iota(jnp.int32, (c, c), 1)).astype(F32)
    heads = range(RWKV_HEADS)
    col = [slice(h * hd, (h + 1) * hd) for h in heads]
    p = [_dot_nt(jnp.concatenate([kap_s[:, s_], r_s[:, s_]], axis=0),
                 jnp.concatenate([k_s[:, s_], b_s[:, s_]], axis=0)) for s_ in col]
    m1 = [jnp.where(strict, x[0:c, 0:c], 0.0) for x in p]
    m2 = [jnp.where(strict, x[0:c, c:2 * c], 0.0) for x in p]
    n1 = [jnp.where(incl, x[c:2 * c, 0:c], 0.0) for x in p]
    n2 = [jnp.where(incl, x[c:2 * c, c:2 * c], 0.0) for x in p]
    vh = [v[:, s_] for s_ in col]
    m1v = [_dot(a_, b_) for a_, b_ in zip(m1, vh)]
    t_inv = [eye - x for x in m2]
    q = [_dot(x, x) for x in m2]
    span = 2
    while True:
        t_new = [t_ + _dot(t_, q_) for t_, q_ in zip(t_inv, q)]
        span *= 2
        if span < c:
            q = [_dot(x, x) for x in q]
        t_inv = t_new
        if span >= c:
            break
    tx = [_dot(t_, jnp.concatenate([kap_s[:, s_], mv], axis=1)) for t_, s_, mv in zip(t_inv, col, m1v)]
    st = [st_ref[h] for h in heads]
    u = [_dot_nt(x[:, 0:hd], s_) + x[:, hd:2 * hd] for x, s_ in zip(tx, st)]
    ys = []
    for h in heads:
        y = _dot_nt(r_s[:, col[h]], st[h]) + _dot(n1[h], vh[h]) - _dot(n2[h], u[h])
        ys.append(y + jnp.sum(bonus_w[:, col[h]], axis=1, keepdims=True) * vh[h])
    for h in heads:
        st_ref[h] = st[h] * gam_tot[:, col[h]] + _dot_tn(vh[h], k_e[:, col[h]]) - _dot_tn(u[h], b_e[:, col[h]])
    y_ref[0] = jnp.concatenate(ys, axis=1)


def _rwkv(zr, mu, kk_scale, k_a, r_k, w0, w2_pad, a0, a2_pad, g2, batch, nc_ctx, nc_tot):
    t, zw_ = zr.shape
    c = SEQ_CHUNK
    n = RWKV_DIM
    bd = _block_ones(RWKV_HEADS, HEAD_DIM)
    full = lambda a: pl.BlockSpec(a.shape, lambda d, b, s: (0,) * a.ndim)
    dirn = lambda a: pl.BlockSpec((1,) + a.shape[1:], lambda d, b, s: (d,) + (0,) * (a.ndim - 1))

    def rows(d, b, s):
        return b * nc_tot + _chunk_pos(s, d, nc_ctx, nc_tot)

    sub = c // 8
    n_sub = t // 8
    vec = lambda a: a.reshape(1, -1)
    args = (vec(mu), vec(kk_scale), vec(k_a), vec(r_k), w0.reshape(2, 1, n), w2_pad, a0.reshape(2, 1, n), a2_pad, g2, bd)
    return pl.pallas_call(
        functools.partial(_rwkv_kernel, nc_ctx=nc_ctx, nc_tot=nc_tot),
        grid=(2, batch, nc_tot),
        in_specs=[pl.BlockSpec((c, zw_), lambda d, b, s: (rows(d, b, s), 0)),
                  pl.BlockSpec((8, zw_), lambda d, b, s: (jnp.maximum(rows(d, b, s) * sub - 1, 0), 0)),
                  pl.BlockSpec((8, zw_), lambda d, b, s: (jnp.minimum((rows(d, b, s) + 1) * sub, n_sub - 1), 0)),
                  full(args[0]), full(args[1]), full(args[2]), full(args[3]),
                  dirn(args[4]), dirn(args[5]), dirn(args[6]), dirn(args[7]), full(args[8]), full(args[9])],
        out_specs=[pl.BlockSpec((1, c, n), lambda d, b, s: (d, rows(d, b, s), 0)),
                   pl.BlockSpec((1, c, n), lambda d, b, s: (d, rows(d, b, s), 0))],
        out_shape=[jax.ShapeDtypeStruct((2, t, n), F32), jax.ShapeDtypeStruct((2, t, n), F32)],
        scratch_shapes=[pltpu.VMEM((RWKV_HEADS, HEAD_DIM, HEAD_DIM), F32)],
        compiler_params=_cparams(("parallel", "parallel", "arbitrary")),
        name="rwkv_scan",
    )(zr, zr, zr, *args)


def _merge_odd_kernel(y0_ref, y1_ref, gate_ref, fo_ref, x_ref, mod_ref, lg_ref, lb_ref, bd_ref, w1_ref, w2_ref,
                      fg_ref, rw_ref, rb_ref, xo_ref, ho_ref, go_ref):
    y = y0_ref[0] + y1_ref[0]
    bd = bd_ref[...]
    mean = _dot(y, bd) * (1.0 / HEAD_DIM)
    yc = y - mean
    var = _dot(yc * yc, bd) * (1.0 / HEAD_DIM)
    rw = (yc * lax.rsqrt(var + RWKV_GN_EPS) * lg_ref[...] + lb_ref[...]) * gate_ref[0]
    out = _dot(fo_ref[...].astype(BF16), w1_ref[...]) + _dot(rw.astype(BF16), w2_ref[...])
    _residual_and_route(x_ref[...], out, mod_ref[0], fg_ref[...], rw_ref, rb_ref, xo_ref, ho_ref, go_ref)


def _merge_odd(y, gate, fo, x, mods, ln_g, ln_b, w_out, ffn_gain, router_wt, router_b, tm, batch, l_ctx, l_lat):
    d = x.shape[1]
    n = RWKV_DIM
    fw = FOURIER_WIDTH
    l_tot = l_ctx + l_lat
    nt = l_lat // tm
    t_out = batch * l_lat
    full = lambda a: pl.BlockSpec(a.shape, lambda b, i: (0,) * a.ndim)
    src = lambda b, i: b * (l_tot // tm) + l_ctx // tm + i
    bd = _block_ones(RWKV_HEADS, HEAD_DIM)
    w1, w2 = w_out[:fw].astype(BF16), w_out[fw:].astype(BF16)
    lg, lb, fg, rb = ln_g.reshape(1, n), ln_b.reshape(1, n), ffn_gain.reshape(1, d), router_b.reshape(N_EXPERTS, 1)
    return pl.pallas_call(
        _merge_odd_kernel,
        grid=(batch, nt),
        in_specs=[pl.BlockSpec((1, tm, n), lambda b, i: (0, src(b, i), 0)),
                  pl.BlockSpec((1, tm, n), lambda b, i: (1, src(b, i), 0)),
                  pl.BlockSpec((1, tm, n), lambda b, i: (0, src(b, i), 0)),
                  pl.BlockSpec((tm, fw), lambda b, i: (i, b)),
                  pl.BlockSpec((tm, d), lambda b, i: (src(b, i), 0)),
                  pl.BlockSpec((1, 6, d), lambda b, i: (b, 0, 0)),
                  full(lg), full(lb), full(bd), full(w1), full(w2), full(fg), full(router_wt), full(rb)],
        out_specs=[pl.BlockSpec((tm, d), lambda b, i: (b * nt + i, 0)),
                   pl.BlockSpec((tm, d), lambda b, i: (b * nt + i, 0)),
                   pl.BlockSpec((tm, N_EXPERTS), lambda b, i: (b * nt + i, 0))],
        out_shape=[jax.ShapeDtypeStruct((t_out, d), F32),
                   jax.ShapeDtypeStruct((t_out, d), BF16),
                   jax.ShapeDtypeStruct((t_out, N_EXPERTS), F32)],
        compiler_params=_cparams(("parallel", "parallel")),
        name="merge_odd",
    )(y, y, gate, fo, x, mods, lg, lb, bd, w1, w2, fg, router_wt, rb)


def _rope_tables(l_ctx, l_lat):
    rows = l_lat // GRID_W
    row = jnp.repeat(jnp.arange(rows, dtype=F32), GRID_W)
    col = jnp.tile(jnp.arange(GRID_W, dtype=F32), rows)
    n_freq = HEAD_DIM // 4
    inv_freq = ROPE_THETA ** (-jnp.arange(n_freq, dtype=F32) / n_freq)
    ang = jnp.concatenate([row[:, None] * inv_freq, col[:, None] * inv_freq], axis=-1)
    cos, sin = jnp.cos(ang), jnp.sin(ang)
    cos64 = jnp.concatenate([cos, cos], axis=1)
    sin64 = jnp.concatenate([-sin, sin], axis=1)
    cos64 = jnp.concatenate([jnp.ones((l_ctx, HEAD_DIM), F32), cos64], axis=0)
    sin64 = jnp.concatenate([jnp.zeros((l_ctx, HEAD_DIM), F32), sin64], axis=0)
    return cos64, sin64


def _pad_rank(w):
    _, r, n = w.shape
    out = jnp.zeros((2, RWKV_RANK_PAD, n), w.dtype)
    out = out.at[0, 0:r].set(w[0])
    return out.at[1, r:2 * r].set(w[1])


def _even_layer(x, mods, p, batch, l_ctx, l_lat, tm, mod_row, tm_moe):
    l_tot = l_ctx + l_lat
    d = x.shape[1]
    nc_ctx, nc_tot = l_ctx // SEQ_CHUNK, l_tot // SEQ_CHUNK
    hk, hv = GLA_HEADS * GLA_DK, GLA_HEADS * GLA_DV
    qw, kw = ATT_HEADS * HEAD_DIM, ATT_KV_HEADS * HEAD_DIM
    w_in = p['w_in']
    o = np.cumsum([0, hk, hk, hv, hv, 2 * GLA_LOWRANK, qw, kw, kw])
    w_gla = jnp.concatenate([w_in[:, o[0]:o[4]]], axis=1).astype(BF16)
    w_dec = jnp.pad(w_in[:, o[4]:o[5]], ((0, 0), (0, 128 - 2 * GLA_LOWRANK))).astype(BF16)
    w_q = w_in[:, o[5]:o[6]].astype(BF16)
    w_kv = w_in[:, o[6]:o[8]].astype(BF16)
    zg, zdec, zq, zkv = _project(x, mods, p['norm_mix'], [w_gla, w_dec, w_q, w_kv], tm, mod_row)

    dec_w_pad = _pad_rank(p['dec_w'])
    o_gla = _gla(zg, zdec, dec_w_pad, p['dec_b'].reshape(2, 1, hk), batch, nc_ctx, nc_tot)

    cos64, sin64 = _rope_tables(l_ctx, l_lat)
    qn, kn, vn = _qk_prep(zq, zkv, cos64, sin64, p['q_norm'], p['k_norm'], tm, l_tot)
    o_att = _attention(qn, kn, vn, p['sink'], batch, l_ctx, l_tot)

    x1, h, gates = _merge_even(o_gla, zg, o_att, x, mods, p['out_norm'], p['w_out'], p['norm_ffn'],
                               p['router_wt'], p['router_b'], tm, mod_row)
    return _moe(h, gates, x1, mods, p['moe_g'], p['moe_u'], p['moe_d'], tm, mod_row)


def _odd_layer(x, mods, p, batch, l_ctx, l_lat, tm, mod_row, tm_moe):
    l_tot = l_ctx + l_lat
    nc_ctx, nc_tot = l_ctx // SEQ_CHUNK, l_tot // SEQ_CHUNK
    n = RWKV_DIM
    fw = FOURIER_WIDTH
    w_in = p['w_in']
    rank_w, rank_a = p['w2'].shape[1], p['a2'].shape[1]
    o = np.cumsum([0, fw, n, n, n, 2 * rank_w, 2 * rank_a])
    pad_cols = lambda w: jnp.pad(w, ((0, 0), (0, RWKV_RANK_PAD - w.shape[1])))
    w_f = w_in[:, o[0]:o[1]].astype(BF16)
    w_r = jnp.concatenate([w_in[:, o[1]:o[4]], pad_cols(w_in[:, o[4]:o[5]]), pad_cols(w_in[:, o[5]:o[6]]),
                           w_in[:, o[6]:]], axis=1).astype(BF16)
    zf, zr = _project(x, mods, p['norm_mix'], [w_f, w_r], tm, mod_row)

    mu = p['mu']
    mu_r = jnp.concatenate([mu[0:3 * n], pad_cols(mu[None, 3 * n:3 * n + 2 * rank_w])[0],
                            pad_cols(mu[None, 3 * n + 2 * rank_w:3 * n + 2 * rank_w + 2 * rank_a])[0],
                            mu[3 * n + 2 * rank_w + 2 * rank_a:]])
    y, gate = _rwkv(zr, mu_r, p['kk_scale'], p['k_a'], p['r_k'].reshape(-1), p['w0'], _pad_rank(p['w2']),
                    p['a0'], _pad_rank(p['a2']), p['g2'], batch, nc_ctx, nc_tot)

    table, chan = _dft_tables(l_lat)
    zc = _chan_dft(zf, chan, batch, l_ctx, l_lat, tm)
    fo = _seq_dft(table, zc, batch, min(512, l_lat), min(1024, l_lat))

    x1, h, gates = _merge_odd(y, gate, fo, x, mods, p['ln_g'], p['ln_b'], p['w_out'], p['norm_ffn'],
                              p['router_wt'], p['router_b'], tm, batch, l_ctx, l_lat)
    lat_tiles = l_lat // tm_moe
    return _moe(h, gates, x1, mods, p['moe_g'], p['moe_u'], p['moe_d'], tm_moe, lambda i: i // lat_tiles)


def kernel(x, c, ctx, c_ctx, ada_w, ada_b, norm_mix, norm_ffn, even_w_in, even_w_out, gla_dec_w, gla_dec_b, gla_out_norm, att_q_norm, att_k_norm, att_sink, odd_w_in, odd_w_out, rwkv_mu, rwkv_w0, rwkv_w2, rwkv_a0, rwkv_a2, rwkv_g2, rwkv_kk_scale, rwkv_k_a, rwkv_r_k, rwkv_ln_g, rwkv_ln_b, router_w, router_b, moe_w_gate, moe_w_up, moe_w_down):
    batch, l_lat, d = x.shape
    l_ctx = ctx.shape[1]
    l_tot = l_ctx + l_lat
    assert batch < 8 and ada_w.shape[0] == 2
    tm = 256 if (l_ctx % 256 == 0 and l_lat % 256 == 0) else 128
    tm_moe = 512 if (l_lat % 512 == 0 and tm == 256) else tm
    assert l_ctx % tm == 0 and l_lat % tm == 0 and l_tot % l_ctx == 0 and l_lat % GRID_W == 0

    xs = jnp.concatenate([ctx, x], axis=1).reshape(batch * l_tot, d)
    cc = jnp.concatenate([c, c_ctx[None, :], jnp.zeros((8 - batch - 1, d), F32)], axis=0)
    tiles_per_b = l_tot // tm
    ctx_tiles = l_ctx // tm

    def mod_row(i):
        return jnp.where(i % tiles_per_b < ctx_tiles, batch, i // tiles_per_b)

    router_wt = router_w.T
    bf = lambda w: w.astype(BF16)
    moe = lambda layer: dict(moe_g=bf(moe_w_gate[layer]), moe_u=bf(moe_w_up[layer]), moe_d=bf(moe_w_down[layer]))

    mods0 = _modvec(cc, ada_w[0], ada_b[0])
    p0 = dict(w_in=even_w_in[0], w_out=even_w_out[0], dec_w=gla_dec_w[0], dec_b=gla_dec_b[0],
              out_norm=gla_out_norm[0], q_norm=att_q_norm[0], k_norm=att_k_norm[0], sink=att_sink[0],
              norm_mix=norm_mix[0], norm_ffn=norm_ffn[0], router_wt=router_wt, router_b=router_b, **moe(0))
    xs = _even_layer(xs, mods0, p0, batch, l_ctx, l_lat, tm, mod_row, tm_moe)

    mods1 = _modvec(cc, ada_w[1], ada_b[1])
    p1 = dict(w_in=odd_w_in[0], w_out=odd_w_out[0], mu=rwkv_mu[0], w0=rwkv_w0[0], w2=rwkv_w2[0], a0=rwkv_a0[0],
              a2=rwkv_a2[0], g2=rwkv_g2[0], kk_scale=rwkv_kk_scale[0], k_a=rwkv_k_a[0], r_k=rwkv_r_k[0],
              ln_g=rwkv_ln_g[0], ln_b=rwkv_ln_b[0], norm_mix=norm_mix[1], norm_ffn=norm_ffn[1],
              router_wt=router_wt, router_b=router_b, **moe(1))
    out = _odd_layer(xs, mods1, p1, batch, l_ctx, l_lat, tm, mod_row, tm_moe)
    return out.reshape(batch, l_lat, d)
```

```python
import functools

import jax
import jax.numpy as jnp
import numpy as np
from jax import lax
from jax.experimental import pallas as pl
from jax.experimental.pallas import tpu as pltpu

F32 = jnp.float32
BF16 = jnp.bfloat16

GRID_W = 64
HEAD_DIM = 64
NORM_EPS = 1e-6
L2_EPS = 1e-12

GLA_DV = 64
GLA_DK = 32
GLA_HEADS = 8
GLA_LOWRANK = 16
GLA_TAU = 16.0

ATT_HEADS = 8
ATT_KV_HEADS = 2
ATT_GROUP = ATT_HEADS // ATT_KV_HEADS
ATT_BLOCK = 128
ROPE_THETA = 10000.0

FOURIER_GROUP_DIM = 64
FOURIER_WIDTH = 256

RWKV_DIM = 768
RWKV_HEADS = 12
RWKV_RANK_PAD = 128
RWKV_GN_EPS = 64e-5

N_EXPERTS = 16
N_GROUPS = 4
PER_GROUP = N_EXPERTS // N_GROUPS
D_EXPERT = 512
MOE_TILE = 256

SEQ_CHUNK = 64
VMEM_LIMIT = 56 * 1024 * 1024


def _cparams(sem):
    return pltpu.CompilerParams(dimension_semantics=sem, vmem_limit_bytes=VMEM_LIMIT)


def _dot(a, b):
    return jnp.dot(a, b, preferred_element_type=F32)


def _dot_nt(a, b):
    return lax.dot_general(a, b, (((1,), (1,)), ((), ())), preferred_element_type=F32)


def _dot_tn(a, b):
    return lax.dot_general(a, b, (((0,), (0,)), ((), ())), preferred_element_type=F32)


def _silu(x):
    return x * jax.nn.sigmoid(x)


def _log_sigmoid(x):
    return jnp.minimum(x, 0.0) - jnp.log(1.0 + jnp.exp(-jnp.abs(x)))


def _modulated_norm(x, gain, shift, scale):
    ms = jnp.mean(x * x, axis=-1, keepdims=True)
    return (x * lax.rsqrt(ms + NORM_EPS) * gain) * (1.0 + scale) + shift


def _block_ones(n_blocks, width):
    return jnp.kron(jnp.eye(n_blocks, dtype=F32), jnp.ones((width, width), F32))


def _modvec_kernel(c_ref, w_ref, b_ref, o_ref):
    o_ref[...] = _dot(_silu(c_ref[...]), w_ref[...]) + b_ref[...]


def _modvec(cc, w, b):
    d = cc.shape[1]
    n = w.shape[1]
    tn = n // 4
    out = pl.pallas_call(
        _modvec_kernel,
        grid=(n // tn,),
        in_specs=[pl.BlockSpec((8, d), lambda j: (0, 0)),
                  pl.BlockSpec((d, tn), lambda j: (0, j)),
                  pl.BlockSpec((1, tn), lambda j: (0, j))],
        out_specs=pl.BlockSpec((8, tn), lambda j: (0, j)),
        out_shape=jax.ShapeDtypeStruct((8, n), F32),
        compiler_params=_cparams(("parallel",)),
        name="modvec",
    )(cc, w, b.reshape(1, n))
    return out.reshape(8, 6, d)


def _proj_kernel(x_ref, mod_ref, gain_ref, *refs, n_out):
    w_refs, z_refs = refs[:n_out], refs[n_out:]
    m = mod_ref[0]
    h = _modulated_norm(x_ref[...], gain_ref[...], m[0:1], m[1:2]).astype(BF16)
    for w_ref, z_ref in zip(w_refs, z_refs):
        z_ref[...] = _dot(h, w_ref[...])


def _project(x, mods, gain, weights, tm, mod_row):
    t, d = x.shape
    n_out = len(weights)
    in_specs = [pl.BlockSpec((tm, d), lambda i: (i, 0)),
                pl.BlockSpec((1, 6, d), lambda i: (mod_row(i), 0, 0)),
                pl.BlockSpec((1, d), lambda i: (0, 0))]
    in_specs += [pl.BlockSpec(w.shape, lambda i: (0, 0)) for w in weights]
    return pl.pallas_call(
        functools.partial(_proj_kernel, n_out=n_out),
        grid=(t // tm,),
        in_specs=in_specs,
        out_specs=[pl.BlockSpec((tm, w.shape[1]), lambda i: (i, 0)) for w in weights],
        out_shape=[jax.ShapeDtypeStruct((t, w.shape[1]), F32) for w in weights],
        compiler_params=_cparams(("parallel",)),
        name="proj",
    )(x, mods, gain.reshape(1, d), *weights)


def _chunk_pos(s, d, nc_ctx, nc_tot):
    back = jnp.where(s < nc_ctx, nc_ctx - 1 - s, nc_tot + nc_ctx - 1 - s)
    return jnp.where(d == 0, s, back)


def _order_masks(d, c):
    row = lax.broadcasted_iota(jnp.int32, (c, c), 0)
    col = lax.broadcasted_iota(jnp.int32, (c, c), 1)
    ahead = (row - col) * jnp.where(d == 0, 1, -1)
    return ahead >= 0, ahead > 0


def _gla_kernel(q_ref, k_ref, v_ref, dec_ref, dw_ref, db_ref, o_ref, st_ref):
    d = pl.program_id(0)
    c = q_ref.shape[0]

    @pl.when(pl.program_id(2) == 0)
    def _():
        st_ref[...] = jnp.zeros_like(st_ref)

    g = _log_sigmoid(_dot(dec_ref[...], dw_ref[0]) + db_ref[0]) / GLA_TAU
    incl, _ = _order_masks(d, c)
    b = _dot(incl.astype(F32), g)
    b_tot = jnp.sum(g, axis=0, keepdims=True)
    q_in = q_ref[...] * (GLA_DK ** -0.5) * jnp.exp(b)
    k_out = k_ref[...] * jnp.exp(-b)
    k_end = k_ref[...] * jnp.exp(b_tot - b)
    decay_tot = jnp.exp(b_tot)
    v = v_ref[...]
    heads = range(GLA_HEADS)
    ks = [slice(h * GLA_DK, (h + 1) * GLA_DK) for h in heads]
    vs = [slice(h * GLA_DV, (h + 1) * GLA_DV) for h in heads]
    att = [jnp.where(incl, _dot_nt(q_in[:, s_], k_out[:, s_]), 0.0) for s_ in ks]
    st = [st_ref[h] for h in heads]
    o_ref[0] = jnp.concatenate([_dot(att[h], v[:, vs[h]]) + _dot_nt(q_in[:, ks[h]], st[h]) for h in heads], axis=1)
    for h in heads:
        st_ref[h] = st[h] * decay_tot[:, ks[h]] + _dot_tn(v[:, vs[h]], k_end[:, ks[h]])


def _gla(zg, zdec, dec_w_pad, dec_b, batch, nc_ctx, nc_tot):
    t = zg.shape[0]
    c = SEQ_CHUNK
    hk, hv = GLA_HEADS * GLA_DK, GLA_HEADS * GLA_DV

    def rows(d, b, s):
        return b * nc_tot + _chunk_pos(s, d, nc_ctx, nc_tot)

    return pl.pallas_call(
        _gla_kernel,
        grid=(2, batch, nc_tot),
        in_specs=[pl.BlockSpec((c, hk), lambda d, b, s: (rows(d, b, s), 0)),
                  pl.BlockSpec((c, hk), lambda d, b, s: (rows(d, b, s), 1)),
                  pl.BlockSpec((c, hv), lambda d, b, s: (rows(d, b, s), 1)),
                  pl.BlockSpec((c, 128), lambda d, b, s: (rows(d, b, s), 0)),
                  pl.BlockSpec((1, 128, hk), lambda d, b, s: (d, 0, 0)),
                  pl.BlockSpec((1, 1, hk), lambda d, b, s: (d, 0, 0))],
        out_specs=pl.BlockSpec((1, c, hv), lambda d, b, s: (d, rows(d, b, s), 0)),
        out_shape=jax.ShapeDtypeStruct((2, t, hv), F32),
        scratch_shapes=[pltpu.VMEM((GLA_HEADS, GLA_DV, GLA_DK), F32)],
        compiler_params=_cparams(("parallel", "parallel", "arbitrary")),
        name="gla_scan",
    )(zg, zg, zg, zdec, dec_w_pad, dec_b)


def _rope_swap(x):
    n = x.shape[-1]
    lane = lax.broadcasted_iota(jnp.int32, x.shape, x.ndim - 1)
    half = HEAD_DIM // 2
    return jnp.where(lane % HEAD_DIM < half, pltpu.roll(x, n - half, x.ndim - 1), pltpu.roll(x, half, x.ndim - 1))


def _qk_prep_kernel(q_ref, kv_ref, cos_ref, sin_ref, qg_ref, kg_ref, bdq_ref, bdk_ref, qo_ref, ko_ref, vo_ref):
    def norm_rope(x, gain, bd, n_heads):
        ms = _dot(x * x, bd) * (1.0 / HEAD_DIM)
        xn = x * lax.rsqrt(ms + NORM_EPS) * gain
        cos = jnp.concatenate([cos_ref[...]] * n_heads, axis=1)
        sin = jnp.concatenate([sin_ref[...]] * n_heads, axis=1)
        return xn * cos + _rope_swap(xn) * sin

    q = norm_rope(q_ref[...], qg_ref[...], bdq_ref[...], ATT_HEADS)
    qo_ref[...] = (q * (HEAD_DIM ** -0.5)).astype(BF16)
    kw = ATT_KV_HEADS * HEAD_DIM
    kv = kv_ref[...]
    ko_ref[...] = norm_rope(kv[:, :kw], kg_ref[...], bdk_ref[...], ATT_KV_HEADS).astype(BF16)
    vo_ref[...] = kv[:, kw:].astype(BF16)


def _qk_prep(zq, zkv, cos64, sin64, q_gain, k_gain, tm, l_tot):
    t = zq.shape[0]
    qw, kw = ATT_HEADS * HEAD_DIM, ATT_KV_HEADS * HEAD_DIM
    n_pos = l_tot // tm
    return pl.pallas_call(
        _qk_prep_kernel,
        grid=(t // tm,),
        in_specs=[pl.BlockSpec((tm, qw), lambda i: (i, 0)),
                  pl.BlockSpec((tm, 2 * kw), lambda i: (i, 0)),
                  pl.BlockSpec((tm, HEAD_DIM), lambda i: (i % n_pos, 0)),
                  pl.BlockSpec((tm, HEAD_DIM), lambda i: (i % n_pos, 0)),
                  pl.BlockSpec((1, qw), lambda i: (0, 0)),
                  pl.BlockSpec((1, kw), lambda i: (0, 0)),
                  pl.BlockSpec((qw, qw), lambda i: (0, 0)),
                  pl.BlockSpec((kw, kw), lambda i: (0, 0))],
        out_specs=[pl.BlockSpec((tm, qw), lambda i: (i, 0)),
                   pl.BlockSpec((tm, kw), lambda i: (i, 0)),
                   pl.BlockSpec((tm, kw), lambda i: (i, 0))],
        out_shape=[jax.ShapeDtypeStruct((t, qw), BF16),
                   jax.ShapeDtypeStruct((t, kw), BF16),
                   jax.ShapeDtypeStruct((t, kw), BF16)],
        compiler_params=_cparams(("parallel",)),
        name="qk_prep",
    )(zq, zkv, cos64, sin64, jnp.tile(q_gain, ATT_HEADS).reshape(1, qw), jnp.tile(k_gain, ATT_KV_HEADS).reshape(1, kw),
      _block_ones(ATT_HEADS, HEAD_DIM), _block_ones(ATT_KV_HEADS, HEAD_DIM))


def _attn_kernel(q_ref, kp_ref, kc_ref, kn_ref, kx_ref, vp_ref, vc_ref, vn_ref, vx_ref, sink_ref, o_ref,
                 *, n_ctx_blocks, n_lat_blocks):
    blk = ATT_BLOCK
    n = pl.program_id(1)
    m = n - n_ctx_blocks
    is_lat = n >= n_ctx_blocks
    l_ctx = kx_ref.shape[0]
    width = 3 * blk + l_ctx
    rows = ATT_GROUP * blk
    r = lax.broadcasted_iota(jnp.int32, (rows, width), 0) % blk
    c = lax.broadcasted_iota(jnp.int32, (rows, width), 1)
    lat = is_lat.astype(jnp.int32)
    has_prev = lat * (m >= 1).astype(jnp.int32)
    has_next = lat * (m <= n_lat_blocks - 2).astype(jnp.int32)
    valid = jnp.where(c < blk, (c >= r).astype(jnp.int32) * has_prev,
                      jnp.where(c < 2 * blk, lat,
                                jnp.where(c < 3 * blk, (c - 2 * blk <= r).astype(jnp.int32) * has_next, 1))) > 0
    q = q_ref[...]
    sink = sink_ref[...]
    outs = [None] * ATT_HEADS
    for kvh in range(ATT_KV_HEADS):
        ks = slice(kvh * HEAD_DIM, (kvh + 1) * HEAD_DIM)
        kw = jnp.concatenate([kp_ref[:, ks], kc_ref[:, ks], kn_ref[:, ks], kx_ref[:, ks]], axis=0)
        vw = jnp.concatenate([vp_ref[:, ks], vc_ref[:, ks], vn_ref[:, ks], vx_ref[:, ks]], axis=0)
        heads = range(kvh * ATT_GROUP, (kvh + 1) * ATT_GROUP)
        qg = jnp.concatenate([q[:, h * HEAD_DIM:(h + 1) * HEAD_DIM] for h in heads], axis=0)
        s = jnp.where(valid, _dot_nt(qg, kw), -jnp.inf)
        sk = jnp.concatenate([jnp.broadcast_to(sink[h:h + 1, 0:1], (blk, 1)) for h in heads], axis=0)
        mx = jnp.maximum(jnp.max(s, axis=-1, keepdims=True), sk)
        p = jnp.exp(s - mx)
        denom = jnp.sum(p, axis=-1, keepdims=True) + jnp.exp(sk - mx)
        o = _dot(p.astype(BF16), vw) / denom
        for g, h in enumerate(heads):
            outs[h] = o[g * blk:(g + 1) * blk]
    o_ref[...] = jnp.concatenate(outs, axis=1)


def _attention(qn, kn, vn, sink, batch, l_ctx, l_tot):
    t = qn.shape[0]
    blk = ATT_BLOCK
    nq = l_tot // blk
    nc = l_ctx // blk
    nl = nq - nc
    qw, kw = ATT_HEADS * HEAD_DIM, ATT_KV_HEADS * HEAD_DIM

    def win(off):
        def index(b, n):
            m = jnp.clip(n - nc + off, 0, nl - 1)
            return (b * nq + nc + m, 0)
        return pl.BlockSpec((blk, kw), index)

    ctx_spec = pl.BlockSpec((l_ctx, kw), lambda b, n: (b * (l_tot // l_ctx), 0))
    return pl.pallas_call(
        functools.partial(_attn_kernel, n_ctx_blocks=nc, n_lat_blocks=nl),
        grid=(batch, nq),
        in_specs=[pl.BlockSpec((blk, qw), lambda b, n: (b * nq + n, 0)),
                  win(-1), win(0), win(1), ctx_spec,
                  win(-1), win(0), win(1), ctx_spec,
                  pl.BlockSpec((ATT_HEADS, 128), lambda b, n: (0, 0))],
        out_specs=pl.BlockSpec((blk, qw), lambda b, n: (b * nq + n, 0)),
        out_shape=jax.ShapeDtypeStruct((t, qw), F32),
        compiler_params=_cparams(("parallel", "parallel")),
        name="window_attention",
    )(qn, kn, kn, kn, kn, vn, vn, vn, vn, jnp.broadcast_to(sink.astype(F32)[:, None], (ATT_HEADS, 128)))


def _route(logits_t, bias_col):
    scores = jax.nn.sigmoid(logits_t)
    sel = scores + bias_col
    rows = [sel[e:e + 1] for e in range(N_EXPERTS)]
    grp = []
    for g in range(N_GROUPS):
        r = rows[g * PER_GROUP:(g + 1) * PER_GROUP]
        best = None
        for i in range(PER_GROUP):
            for j in range(i + 1, PER_GROUP):
                pair = r[i] + r[j]
                best = pair if best is None else jnp.maximum(best, pair)
        grp.append(best)
    g_best = jnp.zeros_like(grp[0], dtype=jnp.int32)
    g_val = grp[0]
    for g in range(1, N_GROUPS):
        take = grp[g] > g_val
        g_best = jnp.where(take, g, g_best)
        g_val = jnp.where(take, grp[g], g_val)
    neg = -jnp.inf
    masked = [jnp.where(g_best == e // PER_GROUP, rows[e], neg) for e in range(N_EXPERTS)]
    i1 = jnp.zeros_like(g_best)
    v1 = masked[0]
    for e in range(1, N_EXPERTS):
        take = masked[e] > v1
        i1 = jnp.where(take, e, i1)
        v1 = jnp.where(take, masked[e], v1)
    i2 = jnp.full_like(g_best, -1)
    v2 = jnp.full_like(v1, neg)
    for e in range(N_EXPERTS):
        take = jnp.logical_and(i1 != e, masked[e] > v2)
        i2 = jnp.where(take, e, i2)
        v2 = jnp.where(take, masked[e], v2)
    w1 = jnp.zeros_like(v1)
    w2 = jnp.zeros_like(v1)
    for e in range(N_EXPERTS):
        w1 = jnp.where(i1 == e, scores[e:e + 1], w1)
        w2 = jnp.where(i2 == e, scores[e:e + 1], w2)
    inv = 1.0 / (w1 + w2)
    pad = jnp.zeros_like(w1)
    return jnp.concatenate([i1.astype(F32), i2.astype(F32), w1 * inv, w2 * inv, pad, pad, pad, pad], axis=0)


def _residual_and_route(x, out, m, ffn_gain, rw_ref, rb_ref, x_ref, h_ref, g_ref):
    x1 = x + m[2:3] * out
    x_ref[...] = x1
    h = _modulated_norm(x1, ffn_gain, m[3:4], m[4:5])
    h_ref[...] = h
    logits_t = lax.dot_general(rw_ref[...], h, (((1,), (1,)), ((), ())), precision=lax.Precision.HIGHEST,
                               preferred_element_type=F32)
    g_ref[...] = _route(logits_t, rb_ref[...])


def _merge_even_kernel(o0_ref, o1_ref, gg_ref, oa_ref, x_ref, mod_ref, gn_ref, bd_ref, w1_ref, w2_ref,
                       fg_ref, rw_ref, rb_ref, xo_ref, ho_ref, go_ref):
    og = o0_ref[0] + o1_ref[0]
    ms = _dot(og * og, bd_ref[...]) * (1.0 / GLA_DV)
    g = og * lax.rsqrt(ms + NORM_EPS) * gn_ref[...] * _silu(gg_ref[...])
    out = _dot(g.astype(BF16), w1_ref[...]) + _dot(oa_ref[...].astype(BF16), w2_ref[...])
    _residual_and_route(x_ref[...], out, mod_ref[0], fg_ref[...], rw_ref, rb_ref, xo_ref, ho_ref, go_ref)


def _merge_even(o_gla, zg, o_att, x, mods, out_norm, w_out, ffn_gain, router_wt, router_b, tm, mod_row):
    t, d = x.shape
    hv = GLA_HEADS * GLA_DV
    qw = ATT_HEADS * HEAD_DIM
    full = lambda a: pl.BlockSpec(a.shape, lambda i: (0,) * a.ndim)
    gn = jnp.tile(out_norm, GLA_HEADS).reshape(1, hv)
    bd = _block_ones(GLA_HEADS, GLA_DV)
    w1, w2 = w_out[:hv].astype(BF16), w_out[hv:].astype(BF16)
    fg = ffn_gain.reshape(1, d)
    rb = router_b.reshape(N_EXPERTS, 1)
    return pl.pallas_call(
        _merge_even_kernel,
        grid=(t // tm,),
        in_specs=[pl.BlockSpec((1, tm, hv), lambda i: (0, i, 0)),
                  pl.BlockSpec((1, tm, hv), lambda i: (1, i, 0)),
                  pl.BlockSpec((tm, hv), lambda i: (i, 2)),
                  pl.BlockSpec((tm, qw), lambda i: (i, 0)),
                  pl.BlockSpec((tm, d), lambda i: (i, 0)),
                  pl.BlockSpec((1, 6, d), lambda i: (mod_row(i), 0, 0)),
                  full(gn), full(bd), full(w1), full(w2), full(fg), full(router_wt), full(rb)],
        out_specs=[pl.BlockSpec((tm, d), lambda i: (i, 0)),
                   pl.BlockSpec((tm, d), lambda i: (i, 0)),
                   pl.BlockSpec((8, tm), lambda i: (0, i))],
        out_shape=[jax.ShapeDtypeStruct((t, d), F32),
                   jax.ShapeDtypeStruct((t, d), F32),
                   jax.ShapeDtypeStruct((8, t), F32)],
        compiler_params=_cparams(("parallel",)),
        name="merge_even",
    )(o_gla, o_gla, zg, o_att, x, mods, gn, bd, w1, w2, fg, router_wt, rb)


def _moe_plan(route, t, rows):
    n_tiles = 2 * t // rows + N_EXPERTS
    n_rows = n_tiles * rows
    eid = jnp.concatenate([route[0], route[1]]).astype(jnp.int32)
    tok = jnp.tile(jnp.arange(t, dtype=jnp.int32), 2)
    which = jnp.repeat(jnp.arange(2, dtype=jnp.int32), t)
    gate = jnp.concatenate([route[2], route[3]])
    _, s_tok, s_which, s_gate = lax.sort((eid, tok, which, gate), num_keys=1, is_stable=True)
    counts = jnp.sum((eid[:, None] == jnp.arange(N_EXPERTS, dtype=jnp.int32)[None, :]).astype(jnp.int32), axis=0)
    padded = (counts + rows - 1) // rows * rows
    p_end = jnp.cumsum(padded)
    p_start = p_end - padded
    c_start = jnp.cumsum(counts) - counts
    r_all = jnp.arange(n_rows, dtype=jnp.int32)
    e_of = jnp.minimum(jnp.sum((r_all[:, None] >= p_end[None, :]).astype(jnp.int32), axis=1), N_EXPERTS - 1)
    r_in = r_all - p_start[e_of]
    valid = jnp.logical_and(r_in < counts[e_of], r_all < p_end[-1])
    src = jnp.clip(c_start[e_of] + r_in, 0, 2 * t - 1)
    row_tok = jnp.where(valid, s_tok[src], 0)
    row_gate = jnp.where(valid, s_gate[src], 0.0)
    pad_rank = jnp.cumsum(jnp.logical_not(valid).astype(jnp.int32)) - 1
    row_dst = jnp.where(valid, s_tok[src] * 2 + s_which[src], 2 * t + pad_rank)
    n_used = (p_end[-1] // rows).astype(jnp.int32)
    e_tile = e_of[::rows]
    tile_e = jnp.where(jnp.arange(n_tiles) < n_used, e_tile, e_tile[jnp.maximum(n_used - 1, 0)])
    return (row_tok.reshape(n_tiles, 1, rows), row_gate.reshape(n_rows, 1), row_dst.reshape(n_tiles, 1, rows),
            tile_e.astype(jnp.int32), n_used.reshape(1))


def _moe_experts_kernel(te_ref, nu_ref, tok_ref, tokn_ref, dst_ref, gate_ref, wg_ref, wu_ref, wd_ref, h_hbm,
                        y_hbm, hbuf, ybuf, wgb, wub, wdb, sem_g, sem_s):
    j = pl.program_id(0)
    n_used = nu_ref[0]
    slot = j % 2
    rows = hbuf.shape[1]

    def start_gather(idx_ref, s):
        def body(r, carry):
            pltpu.make_async_copy(h_hbm.at[pl.ds(idx_ref[0, 0, r], 1)], hbuf.at[s, pl.ds(r, 1)], sem_g.at[s]).start()
            return carry
        lax.fori_loop(0, rows, body, 0, unroll=8)

    def wait_gather(s):
        pltpu.make_async_copy(h_hbm.at[pl.ds(0, rows)], hbuf.at[s], sem_g.at[s]).wait()

    def wait_scatter(s):
        pltpu.make_async_copy(ybuf.at[s], y_hbm.at[pl.ds(0, rows)], sem_s.at[s]).wait()

    @pl.when(j == 0)
    def _():
        start_gather(tok_ref, 0)
        ybuf[1] = jnp.zeros(ybuf.shape[1:], F32)
        n_real = y_hbm.shape[0] - N_EXPERTS * rows
        fills = [pltpu.make_async_copy(ybuf.at[1], y_hbm.at[pl.ds(n_real + k * rows, rows)], sem_s.at[1])
                 for k in range(N_EXPERTS)]
        for f in fills:
            f.start()
        for f in fills:
            f.wait()

    @pl.when(j + 1 < n_used)
    def _():
        start_gather(tokn_ref, 1 - slot)

    active = j < n_used
    changed = jnp.logical_or(j == 0, te_ref[j] != te_ref[jnp.maximum(j - 1, 0)])

    @pl.when(jnp.logical_and(active, changed))
    def _():
        wgb[...] = wg_ref[0].astype(BF16)
        wub[...] = wu_ref[0].astype(BF16)
        wdb[...] = wd_ref[0].astype(BF16)

    @pl.when(active)
    def _():
        wait_gather(slot)
        h = hbuf[slot].astype(BF16)
        act = _silu(_dot(h, wgb[...])) * _dot(h, wub[...])
        y = _dot((act * gate_ref[...]).astype(BF16), wdb[...])

        @pl.when(j >= 2)
        def _():
            wait_scatter(slot)

        ybuf[slot] = y

        def body(r, carry):
            pltpu.make_async_copy(ybuf.at[slot, pl.ds(r, 1)], y_hbm.at[pl.ds(dst_ref[0, 0, r], 1)], sem_s.at[slot]).start()
            return carry
        lax.fori_loop(0, rows, body, 0, unroll=8)

        @pl.when(j == n_used - 1)
        def _():
            wait_scatter(slot)

            @pl.when(j >= 1)
            def _():
                wait_scatter(1 - slot)


def _moe_combine_kernel(y_ref, x_ref, mod_ref, o_ref):
    d = x_ref.shape[1]
    y = y_ref[...]
    o_ref[...] = x_ref[...] + mod_ref[0][5:6] * (y[:, :d] + y[:, d:])


def _moe(h, route, x, mods, w_gate, w_up, w_down, tm, mod_row):
    t, d = x.shape
    rows = MOE_TILE
    row_tok, row_gate, row_dst, tile_e, n_used = _moe_plan(route, t, rows)
    n_tiles = row_tok.shape[0]
    n_pair_rows = 2 * t + N_EXPERTS * rows
    smem_tile = lambda off: pl.BlockSpec((1, 1, rows), lambda j, te, nu: (jnp.minimum(j + off, n_tiles - 1), 0, 0),
                                         memory_space=pltpu.SMEM)
    pairs = pl.pallas_call(
        _moe_experts_kernel,
        grid_spec=pltpu.PrefetchScalarGridSpec(
            num_scalar_prefetch=2,
            grid=(n_tiles,),
            in_specs=[smem_tile(0), smem_tile(1), smem_tile(0),
                      pl.BlockSpec((rows, 1), lambda j, te, nu: (j, 0)),
                      pl.BlockSpec((1, d, D_EXPERT), lambda j, te, nu: (te[j], 0, 0)),
                      pl.BlockSpec((1, d, D_EXPERT), lambda j, te, nu: (te[j], 0, 0)),
                      pl.BlockSpec((1, D_EXPERT, d), lambda j, te, nu: (te[j], 0, 0)),
                      pl.BlockSpec(memory_space=pl.ANY)],
            out_specs=pl.BlockSpec(memory_space=pl.ANY),
            scratch_shapes=[pltpu.VMEM((2, rows, d), F32), pltpu.VMEM((2, rows, d), F32),
                            pltpu.VMEM((d, D_EXPERT), BF16), pltpu.VMEM((d, D_EXPERT), BF16),
                            pltpu.VMEM((D_EXPERT, d), BF16),
                            pltpu.SemaphoreType.DMA((2,)), pltpu.SemaphoreType.DMA((2,))]),
        out_shape=jax.ShapeDtypeStruct((n_pair_rows, d), F32),
        compiler_params=_cparams(("arbitrary",)),
        name="moe_experts",
    )(tile_e, n_used, row_tok, row_tok, row_dst, row_gate, w_gate, w_up, w_down, h)
    return pl.pallas_call(
        _moe_combine_kernel,
        grid=(t // tm,),
        in_specs=[pl.BlockSpec((tm, 2 * d), lambda i: (i, 0)),
                  pl.BlockSpec((tm, d), lambda i: (i, 0)),
                  pl.BlockSpec((1, 6, d), lambda i: (mod_row(i), 0, 0))],
        out_specs=pl.BlockSpec((tm, d), lambda i: (i, 0)),
        out_shape=jax.ShapeDtypeStruct((t, d), F32),
        compiler_params=_cparams(("parallel",)),
        name="moe_combine",
    )(pairs.reshape(n_pair_rows // 2, 2 * d), x, mods)


def _chan_dft_kernel(z_ref, w_ref, o_ref):
    res = _dot(z_ref[...].astype(BF16), w_ref[...]).astype(BF16)
    o_ref[0] = res[:, :FOURIER_WIDTH]
    o_ref[1] = res[:, FOURIER_WIDTH:]


def _chan_dft(zf, w, batch, l_ctx, l_lat, tm):
    l_tot = l_ctx + l_lat
    nt = l_lat // tm
    fw = FOURIER_WIDTH
    return pl.pallas_call(
        _chan_dft_kernel,
        grid=(batch, nt),
        in_specs=[pl.BlockSpec((tm, fw), lambda b, i: (b * (l_tot // tm) + l_ctx // tm + i, 0)),
                  pl.BlockSpec(w.shape, lambda b, i: (0, 0))],
        out_specs=pl.BlockSpec((2, tm, fw), lambda b, i: (0, i, b)),
        out_shape=jax.ShapeDtypeStruct((2, l_lat, batch * fw), BF16),
        compiler_params=_cparams(("parallel", "parallel")),
        name="fourier_channels",
    )(zf, w)


def _seq_dft_kernel(t_ref, z_ref, o_ref, acc_ref):
    k = pl.program_id(1)

    @pl.when(k == 0)
    def _():
        acc_ref[...] = jnp.zeros_like(acc_ref)

    acc_ref[...] += _dot(t_ref[...], z_ref[...])

    @pl.when(k == pl.num_programs(1) - 1)
    def _():
        o_ref[...] = acc_ref[...]


def _seq_dft(table, zc, batch, tm, tk):
    l = table.shape[0]
    fw = FOURIER_WIDTH
    nk = l // tk
    zst = zc.reshape(2 * l, batch * fw)
    return pl.pallas_call(
        _seq_dft_kernel,
        grid=(l // tm, 2 * nk),
        in_specs=[pl.BlockSpec((tm, tk), lambda i, k: (i, k)),
                  pl.BlockSpec((tk, batch * fw), lambda i, k: (k, 0))],
        out_specs=pl.BlockSpec((tm, batch * fw), lambda i, k: (i, 0)),
        out_shape=jax.ShapeDtypeStruct((l, batch * fw), F32),
        scratch_shapes=[pltpu.VMEM((tm, batch * fw), F32)],
        compiler_params=_cparams(("parallel", "arbitrary")),
        name="fourier_sequence",
    )(table, zst)


def _dft_tables(l):
    m = jnp.arange(l, dtype=jnp.int32)[:, None]
    n1 = l // 64
    a = (m * (jnp.arange(n1, dtype=jnp.int32)[None, :] * 64)) % l
    b = (m * jnp.arange(64, dtype=jnp.int32)[None, :]) % l
    wa = a.astype(F32) * (2.0 * np.pi / l)
    wb = b.astype(F32) * (2.0 * np.pi / l)
    ca, sa, cb, sb = jnp.cos(wa), jnp.sin(wa), jnp.cos(wb), jnp.sin(wb)
    cos_t = (ca[:, :, None] * cb[:, None, :] - sa[:, :, None] * sb[:, None, :]).reshape(l, l)
    sin_t = (sa[:, :, None] * cb[:, None, :] + ca[:, :, None] * sb[:, None, :]).reshape(l, l)
    table = jnp.concatenate([cos_t, sin_t], axis=1).astype(BF16)
    gd = FOURIER_GROUP_DIM
    cc = (jnp.arange(gd, dtype=jnp.int32)[:, None] * jnp.arange(gd, dtype=jnp.int32)[None, :]) % gd
    wc = cc.astype(F32) * (2.0 * np.pi / gd)
    scale = 1.0 / np.sqrt(float(l) * gd)
    eye = jnp.eye(FOURIER_WIDTH // gd, dtype=F32)
    chan = jnp.concatenate([jnp.kron(eye, jnp.cos(wc)), -jnp.kron(eye, jnp.sin(wc))], axis=1) * scale
    return table, chan.astype(BF16)


def _rwkv_kernel(z_ref, zp_ref, zn_ref, mu_ref, kks_ref, ka_ref, rk_ref, w0_ref, w2_ref, a0_ref, a2_ref, g2_ref,
                 bd_ref, y_ref, gate_ref, st_ref, *, nc_ctx, nc_tot):
    d = pl.program_id(0)
    s = pl.program_id(2)
    c = z_ref.shape[0]
    n = RWKV_DIM
    hd = HEAD_DIM

    @pl.when(s == 0)
    def _():
        st_ref[...] = jnp.zeros_like(st_ref)

    pos = _chunk_pos(s, d, nc_ctx, nc_tot)
    seg_first = jnp.logical_or(pos == 0, pos == nc_ctx)
    seg_last = jnp.logical_or(pos == nc_ctx - 1, pos == nc_tot - 1)
    z = z_ref[...]
    row = lax.broadcasted_iota(jnp.int32, z.shape, 0)
    prev_row = jnp.where(seg_first, 0.0, zp_ref[7:8, :])
    next_row = jnp.where(seg_last, 0.0, zn_ref[0:1, :])
    z_prev = jnp.where(row == 0, prev_row, pltpu.roll(z, 1, 0))
    z_next = jnp.where(row == c - 1, next_row, pltpu.roll(z, c - 1, 0))
    mu = mu_ref[...]
    zs = z + mu * (0.5 * (z_prev + z_next) - z)

    r, k, v = zs[:, 0:n], zs[:, n:2 * n], zs[:, 2 * n:3 * n]
    zw = zs[:, 3 * n:3 * n + RWKV_RANK_PAD]
    za = zs[:, 3 * n + RWKV_RANK_PAD:3 * n + 2 * RWKV_RANK_PAD]
    zg = zs[:, 3 * n + 2 * RWKV_RANK_PAD:3 * n + 3 * RWKV_RANK_PAD]

    kk = k * kks_ref[...]
    kk = kk * lax.rsqrt(_dot(kk * kk, bd_ref[...]) + L2_EPS)
    w_pre = w0_ref[0] + _dot(jnp.tanh(zw), w2_ref[0])
    w_log = _log_sigmoid(w_pre) - 0.5
    lw = -jnp.exp(w_log)
    a = jax.nn.sigmoid(a0_ref[0] + _dot(za, a2_ref[0]))
    kd = k * (1.0 + (a - 1.0) * ka_ref[...])
    beta = kk * a
    gate_ref[0] = _dot(jax.nn.sigmoid(zg), g2_ref[...])

    incl, strict = _order_masks(d, c)
    cl = _dot(incl.astype(F32), lw)
    c_tot = jnp.sum(lw, axis=0, keepdims=True)
    grow = jnp.exp(-cl)
    k_s = kd * grow
    b_s = beta * grow
    kap_s = kk * jnp.exp(cl - lw)
    r_s = r * jnp.exp(cl)
    tail = jnp.exp(c_tot - cl)
    k_e = kd * tail
    b_e = beta * tail
    gam_tot = jnp.exp(c_tot)
    bonus_w = r * kd * rk_ref[...]

    eye = (lax.broadcasted_iota(jnp.int32, (c, c), 0) == lax.broadcasted_iota(jnp.int32, (c, c), 1)).astype(F32)
    heads = range(RWKV_HEADS)
    col = [slice(h * hd, (h + 1) * hd) for h in heads]
    p = [_dot_nt(jnp.concatenate([kap_s[:, s_], r_s[:, s_]], axis=0),
                 jnp.concatenate([k_s[:, s_], b_s[:, s_]], axis=0)) for s_ in col]
    m1 = [jnp.where(strict, x[0:c, 0:c], 0.0) for x in p]
    m2 = [jnp.where(strict, x[0:c, c:2 * c], 0.0) for x in p]
    n1 = [jnp.where(incl, x[c:2 * c, 0:c], 0.0) for x in p]
    n2 = [jnp.where(incl, x[c:2 * c, c:2 * c], 0.0) for x in p]
    vh = [v[:, s_] for s_ in col]
    m1v = [_dot(a_, b_) for a_, b_ in zip(m1, vh)]
    t_inv = [eye - x for x in m2]
    q = [_dot(x, x) for x in m2]
    span = 2
    while True:
        t_new = [t_ + _dot(t_, q_) for t_, q_ in zip(t_inv, q)]
        span *= 2
        if span < c:
            q = [_dot(x, x) for x in q]
        t_inv = t_new
        if span >= c:
            break
    tx = [_dot(t_, jnp.concatenate([kap_s[:, s_], mv], axis=1)) for t_, s_, mv in zip(t_inv, col, m1v)]
    st = [st_ref[h] for h in heads]
    u = [_dot_nt(x[:, 0:hd], s_) + x[:, hd:2 * hd] for x, s_ in zip(tx, st)]
    ys = []
    for h in heads:
        y = _dot_nt(r_s[:, col[h]], st[h]) + _dot(n1[h], vh[h]) - _dot(n2[h], u[h])
        ys.append(y + jnp.sum(bonus_w[:, col[h]], axis=1, keepdims=True) * vh[h])
    for h in heads:
        st_ref[h] = st[h] * gam_tot[:, col[h]] + _dot_tn(vh[h], k_e[:, col[h]]) - _dot_tn(u[h], b_e[:, col[h]])
    y_ref[0] = jnp.concatenate(ys, axis=1)


def _rwkv(zr, mu, kk_scale, k_a, r_k, w0, w2_pad, a0, a2_pad, g2, batch, nc_ctx, nc_tot):
    t, zw_ = zr.shape
    c = SEQ_CHUNK
    n = RWKV_DIM
    bd = _block_ones(RWKV_HEADS, HEAD_DIM)
    full = lambda a: pl.BlockSpec(a.shape, lambda d, b, s: (0,) * a.ndim)
    dirn = lambda a: pl.BlockSpec((1,) + a.shape[1:], lambda d, b, s: (d,) + (0,) * (a.ndim - 1))

    def rows(d, b, s):
        return b * nc_tot + _chunk_pos(s, d, nc_ctx, nc_tot)

    sub = c // 8
    n_sub = t // 8
    vec = lambda a: a.reshape(1, -1)
    args = (vec(mu), vec(kk_scale), vec(k_a), vec(r_k), w0.reshape(2, 1, n), w2_pad, a0.reshape(2, 1, n), a2_pad, g2, bd)
    return pl.pallas_call(
        functools.partial(_rwkv_kernel, nc_ctx=nc_ctx, nc_tot=nc_tot),
        grid=(2, batch, nc_tot),
        in_specs=[pl.BlockSpec((c, zw_), lambda d, b, s: (rows(d, b, s), 0)),
                  pl.BlockSpec((8, zw_), lambda d, b, s: (jnp.maximum(rows(d, b, s) * sub - 1, 0), 0)),
                  pl.BlockSpec((8, zw_), lambda d, b, s: (jnp.minimum((rows(d, b, s) + 1) * sub, n_sub - 1), 0)),
                  full(args[0]), full(args[1]), full(args[2]), full(args[3]),
                  dirn(args[4]), dirn(args[5]), dirn(args[6]), dirn(args[7]), full(args[8]), full(args[9])],
        out_specs=[pl.BlockSpec((1, c, n), lambda d, b, s: (d, rows(d, b, s), 0)),
                   pl.BlockSpec((1, c, n), lambda d, b, s: (d, rows(d, b, s), 0))],
        out_shape=[jax.ShapeDtypeStruct((2, t, n), F32), jax.ShapeDtypeStruct((2, t, n), F32)],
        scratch_shapes=[pltpu.VMEM((RWKV_HEADS, HEAD_DIM, HEAD_DIM), F32)],
        compiler_params=_cparams(("parallel", "parallel", "arbitrary")),
        name="rwkv_scan",
    )(zr, zr, zr, *args)


def _merge_odd_kernel(y0_ref, y1_ref, gate_ref, fo_ref, x_ref, mod_ref, lg_ref, lb_ref, bd_ref, w1_ref, w2_ref,
                      fg_ref, rw_ref, rb_ref, xo_ref, ho_ref, go_ref):
    y = y0_ref[0] + y1_ref[0]
    bd = bd_ref[...]
    mean = _dot(y, bd) * (1.0 / HEAD_DIM)
    yc = y - mean
    var = _dot(yc * yc, bd) * (1.0 / HEAD_DIM)
    rw = (yc * lax.rsqrt(var + RWKV_GN_EPS) * lg_ref[...] + lb_ref[...]) * gate_ref[0]
    out = _dot(fo_ref[...].astype(BF16), w1_ref[...]) + _dot(rw.astype(BF16), w2_ref[...])
    _residual_and_route(x_ref[...], out, mod_ref[0], fg_ref[...], rw_ref, rb_ref, xo_ref, ho_ref, go_ref)


def _merge_odd(y, gate, fo, x, mods, ln_g, ln_b, w_out, ffn_gain, router_wt, router_b, tm, batch, l_ctx, l_lat):
    d = x.shape[1]
    n = RWKV_DIM
    fw = FOURIER_WIDTH
    l_tot = l_ctx + l_lat
    nt = l_lat // tm
    t_out = batch * l_lat
    full = lambda a: pl.BlockSpec(a.shape, lambda b, i: (0,) * a.ndim)
    src = lambda b, i: b * (l_tot // tm) + l_ctx // tm + i
    bd = _block_ones(RWKV_HEADS, HEAD_DIM)
    w1, w2 = w_out[:fw].astype(BF16), w_out[fw:].astype(BF16)
    lg, lb, fg, rb = ln_g.reshape(1, n), ln_b.reshape(1, n), ffn_gain.reshape(1, d), router_b.reshape(N_EXPERTS, 1)
    return pl.pallas_call(
        _merge_odd_kernel,
        grid=(batch, nt),
        in_specs=[pl.BlockSpec((1, tm, n), lambda b, i: (0, src(b, i), 0)),
                  pl.BlockSpec((1, tm, n), lambda b, i: (1, src(b, i), 0)),
                  pl.BlockSpec((1, tm, n), lambda b, i: (0, src(b, i), 0)),
                  pl.BlockSpec((tm, fw), lambda b, i: (i, b)),
                  pl.BlockSpec((tm, d), lambda b, i: (src(b, i), 0)),
                  pl.BlockSpec((1, 6, d), lambda b, i: (b, 0, 0)),
                  full(lg), full(lb), full(bd), full(w1), full(w2), full(fg), full(router_wt), full(rb)],
        out_specs=[pl.BlockSpec((tm, d), lambda b, i: (b * nt + i, 0)),
                   pl.BlockSpec((tm, d), lambda b, i: (b * nt + i, 0)),
                   pl.BlockSpec((8, tm), lambda b, i: (0, b * nt + i))],
        out_shape=[jax.ShapeDtypeStruct((t_out, d), F32),
                   jax.ShapeDtypeStruct((t_out, d), F32),
                   jax.ShapeDtypeStruct((8, t_out), F32)],
        compiler_params=_cparams(("parallel", "parallel")),
        name="merge_odd",
    )(y, y, gate, fo, x, mods, lg, lb, bd, w1, w2, fg, router_wt, rb)


def _rope_tables(l_ctx, l_lat):
    rows = l_lat // GRID_W
    row = jnp.repeat(jnp.arange(rows, dtype=F32), GRID_W)
    col = jnp.tile(jnp.arange(GRID_W, dtype=F32), rows)
    n_freq = HEAD_DIM // 4
    inv_freq = ROPE_THETA ** (-jnp.arange(n_freq, dtype=F32) / n_freq)
    ang = jnp.concatenate([row[:, None] * inv_freq, col[:, None] * inv_freq], axis=-1)
    cos, sin = jnp.cos(ang), jnp.sin(ang)
    cos64 = jnp.concatenate([cos, cos], axis=1)
    sin64 = jnp.concatenate([-sin, sin], axis=1)
    cos64 = jnp.concatenate([jnp.ones((l_ctx, HEAD_DIM), F32), cos64], axis=0)
    sin64 = jnp.concatenate([jnp.zeros((l_ctx, HEAD_DIM), F32), sin64], axis=0)
    return cos64, sin64


def _pad_rank(w):
    _, r, n = w.shape
    out = jnp.zeros((2, RWKV_RANK_PAD, n), w.dtype)
    out = out.at[0, 0:r].set(w[0])
    return out.at[1, r:2 * r].set(w[1])


def _even_layer(x, mods, p, batch, l_ctx, l_lat, tm, mod_row, tm_moe):
    l_tot = l_ctx + l_lat
    d = x.shape[1]
    nc_ctx, nc_tot = l_ctx // SEQ_CHUNK, l_tot // SEQ_CHUNK
    hk, hv = GLA_HEADS * GLA_DK, GLA_HEADS * GLA_DV
    qw, kw = ATT_HEADS * HEAD_DIM, ATT_KV_HEADS * HEAD_DIM
    w_in = p['w_in']
    o = np.cumsum([0, hk, hk, hv, hv, 2 * GLA_LOWRANK, qw, kw, kw])
    w_gla = jnp.concatenate([w_in[:, o[0]:o[4]]], axis=1).astype(BF16)
    w_dec = jnp.pad(w_in[:, o[4]:o[5]], ((0, 0), (0, 128 - 2 * GLA_LOWRANK))).astype(BF16)
    w_q = w_in[:, o[5]:o[6]].astype(BF16)
    w_kv = w_in[:, o[6]:o[8]].astype(BF16)
    zg, zdec, zq, zkv = _project(x, mods, p['norm_mix'], [w_gla, w_dec, w_q, w_kv], tm, mod_row)

    dec_w_pad = _pad_rank(p['dec_w'])
    o_gla = _gla(zg, zdec, dec_w_pad, p['dec_b'].reshape(2, 1, hk), batch, nc_ctx, nc_tot)

    cos64, sin64 = _rope_tables(l_ctx, l_lat)
    qn, kn, vn = _qk_prep(zq, zkv, cos64, sin64, p['q_norm'], p['k_norm'], tm, l_tot)
    o_att = _attention(qn, kn, vn, p['sink'], batch, l_ctx, l_tot)

    x1, h, gates = _merge_even(o_gla, zg, o_att, x, mods, p['out_norm'], p['w_out'], p['norm_ffn'],
                               p['router_wt'], p['router_b'], tm, mod_row)
    return _moe(h, gates, x1, mods, p['moe_g'], p['moe_u'], p['moe_d'], tm, mod_row)


def _odd_layer(x, mods, p, batch, l_ctx, l_lat, tm, mod_row, tm_moe):
    l_tot = l_ctx + l_lat
    nc_ctx, nc_tot = l_ctx // SEQ_CHUNK, l_tot // SEQ_CHUNK
    n = RWKV_DIM
    fw = FOURIER_WIDTH
    w_in = p['w_in']
    rank_w, rank_a = p['w2'].shape[1], p['a2'].shape[1]
    o = np.cumsum([0, fw, n, n, n, 2 * rank_w, 2 * rank_a])
    pad_cols = lambda w: jnp.pad(w, ((0, 0), (0, RWKV_RANK_PAD - w.shape[1])))
    w_f = w_in[:, o[0]:o[1]].astype(BF16)
    w_r = jnp.concatenate([w_in[:, o[1]:o[4]], pad_cols(w_in[:, o[4]:o[5]]), pad_cols(w_in[:, o[5]:o[6]]),
                           w_in[:, o[6]:]], axis=1).astype(BF16)
    zf, zr = _project(x, mods, p['norm_mix'], [w_f, w_r], tm, mod_row)

    mu = p['mu']
    mu_r = jnp.concatenate([mu[0:3 * n], pad_cols(mu[None, 3 * n:3 * n + 2 * rank_w])[0],
                            pad_cols(mu[None, 3 * n + 2 * rank_w:3 * n + 2 * rank_w + 2 * rank_a])[0],
                            mu[3 * n + 2 * rank_w + 2 * rank_a:]])
    y, gate = _rwkv(zr, mu_r, p['kk_scale'], p['k_a'], p['r_k'].reshape(-1), p['w0'], _pad_rank(p['w2']),
                    p['a0'], _pad_rank(p['a2']), p['g2'], batch, nc_ctx, nc_tot)

    table, chan = _dft_tables(l_lat)
    zc = _chan_dft(zf, chan, batch, l_ctx, l_lat, tm)
    fo = _seq_dft(table, zc, batch, min(512, l_lat), min(1024, l_lat))

    x1, h, gates = _merge_odd(y, gate, fo, x, mods, p['ln_g'], p['ln_b'], p['w_out'], p['norm_ffn'],
                              p['router_wt'], p['router_b'], tm, batch, l_ctx, l_lat)
    lat_tiles = l_lat // tm_moe
    return _moe(h, gates, x1, mods, p['moe_g'], p['moe_u'], p['moe_d'], tm_moe, lambda i: i // lat_tiles)


def kernel(x, c, ctx, c_ctx, ada_w, ada_b, norm_mix, norm_ffn, even_w_in, even_w_out, gla_dec_w, gla_dec_b, gla_out_norm, att_q_norm, att_k_norm, att_sink, odd_w_in, odd_w_out, rwkv_mu, rwkv_w0, rwkv_w2, rwkv_a0, rwkv_a2, rwkv_g2, rwkv_kk_scale, rwkv_k_a, rwkv_r_k, rwkv_ln_g, rwkv_ln_b, router_w, router_b, moe_w_gate, moe_w_up, moe_w_down):
    batch, l_lat, d = x.shape
    l_ctx = ctx.shape[1]
    l_tot = l_ctx + l_lat
    assert batch < 8 and ada_w.shape[0] == 2
    tm = 256 if (l_ctx % 256 == 0 and l_lat % 256 == 0) else 128
    tm_moe = 512 if (l_lat % 512 == 0 and tm == 256) else tm
    assert l_ctx % tm == 0 and l_lat % tm == 0 and l_tot % l_ctx == 0 and l_lat % GRID_W == 0

    xs = jnp.concatenate([ctx, x], axis=1).reshape(batch * l_tot, d)
    cc = jnp.concatenate([c, c_ctx[None, :], jnp.zeros((8 - batch - 1, d), F32)], axis=0)
    tiles_per_b = l_tot // tm
    ctx_tiles = l_ctx // tm

    def mod_row(i):
        return jnp.where(i % tiles_per_b < ctx_tiles, batch, i // tiles_per_b)

    router_wt = router_w.T
    moe = lambda layer: dict(moe_g=moe_w_gate[layer], moe_u=moe_w_up[layer], moe_d=moe_w_down[layer])

    mods0 = _modvec(cc, ada_w[0], ada_b[0])
    p0 = dict(w_in=even_w_in[0], w_out=even_w_out[0], dec_w=gla_dec_w[0], dec_b=gla_dec_b[0],
              out_norm=gla_out_norm[0], q_norm=att_q_norm[0], k_norm=att_k_norm[0], sink=att_sink[0],
              norm_mix=norm_mix[0], norm_ffn=norm_ffn[0], router_wt=router_wt, router_b=router_b, **moe(0))
    xs = _even_layer(xs, mods0, p0, batch, l_ctx, l_lat, tm, mod_row, tm_moe)

    mods1 = _modvec(cc, ada_w[1], ada_b[1])
    p1 = dict(w_in=odd_w_in[0], w_out=odd_w_out[0], mu=rwkv_mu[0], w0=rwkv_w0[0], w2=rwkv_w2[0], a0=rwkv_a0[0],
              a2=rwkv_a2[0], g2=rwkv_g2[0], kk_scale=rwkv_kk_scale[0], k_a=rwkv_k_a[0], r_k=rwkv_r_k[0],
              ln_g=rwkv_ln_g[0], ln_b=rwkv_ln_b[0], norm_mix=norm_mix[1], norm_ffn=norm_ffn[1],
              router_wt=router_wt, router_b=router_b, **moe(1))
    out = _odd_layer(xs, mods1, p1, batch, l_ctx, l_lat, tm, mod_row, tm_moe)
    return out.reshape(batch, l_lat, d)
```

```python
import functools

import jax
import jax.numpy as jnp
import numpy as np
from jax import lax
from jax.experimental import pallas as pl
from jax.experimental.pallas import tpu as pltpu

F32 = jnp.float32
BF16 = jnp.bfloat16

GRID_W = 64
HEAD_DIM = 64
NORM_EPS = 1e-6
L2_EPS = 1e-12

GLA_DV = 64
GLA_DK = 32
GLA_HEADS = 8
GLA_LOWRANK = 16
GLA_TAU = 16.0

ATT_HEADS = 8
ATT_KV_HEADS = 2
ATT_GROUP = ATT_HEADS // ATT_KV_HEADS
ATT_BLOCK = 128
ROPE_THETA = 10000.0

FOURIER_GROUP_DIM = 64
FOURIER_WIDTH = 256

RWKV_DIM = 768
RWKV_HEADS = 12
RWKV_RANK_PAD = 128
RWKV_GN_EPS = 64e-5

N_EXPERTS = 16
N_GROUPS = 4
PER_GROUP = N_EXPERTS // N_GROUPS
D_EXPERT = 512
MOE_TILE = 256

SEQ_CHUNK = 64
VMEM_LIMIT = 56 * 1024 * 1024


def _cparams(sem):
    return pltpu.CompilerParams(dimension_semantics=sem, vmem_limit_bytes=VMEM_LIMIT)


def _dot(a, b):
    return jnp.dot(a, b, preferred_element_type=F32)


def _dot_nt(a, b):
    return lax.dot_general(a, b, (((1,), (1,)), ((), ())), preferred_element_type=F32)


def _dot_tn(a, b):
    return lax.dot_general(a, b, (((0,), (0,)), ((), ())), preferred_element_type=F32)


def _silu(x):
    return x * jax.nn.sigmoid(x)


def _log_sigmoid(x):
    return jnp.minimum(x, 0.0) - jnp.log(1.0 + jnp.exp(-jnp.abs(x)))


def _modulated_norm(x, gain, shift, scale):
    ms = jnp.mean(x * x, axis=-1, keepdims=True)
    return (x * lax.rsqrt(ms + NORM_EPS) * gain) * (1.0 + scale) + shift


def _block_ones(n_blocks, width):
    return jnp.kron(jnp.eye(n_blocks, dtype=F32), jnp.ones((width, width), F32))


def _modvec_kernel(c_ref, w_ref, b_ref, o_ref):
    o_ref[...] = _dot(_silu(c_ref[...]), w_ref[...]) + b_ref[...]


def _modvec(cc, w, b):
    d = cc.shape[1]
    layers, _, n = w.shape
    tn = n // 4
    out = pl.pallas_call(
        _modvec_kernel,
        grid=(layers, n // tn),
        in_specs=[pl.BlockSpec((8, d), lambda l, j: (0, 0)),
                  pl.BlockSpec((None, d, tn), lambda l, j: (l, 0, j)),
                  pl.BlockSpec((None, 1, tn), lambda l, j: (l, 0, j))],
        out_specs=pl.BlockSpec((None, 8, tn), lambda l, j: (l, 0, j)),
        out_shape=jax.ShapeDtypeStruct((layers, 8, n), F32),
        compiler_params=_cparams(("parallel", "parallel")),
        name="modvec",
    )(cc, w, b.reshape(layers, 1, n))
    return out.reshape(layers, 8, 6, d)


def _proj_kernel(x_ref, mod_ref, gain_ref, *refs, n_out):
    w_refs, z_refs = refs[:n_out], refs[n_out:]
    m = mod_ref[0]
    h = _modulated_norm(x_ref[...], gain_ref[...], m[0:1], m[1:2]).astype(BF16)
    for w_ref, z_ref in zip(w_refs, z_refs):
        z_ref[...] = _dot(h, w_ref[...])


def _project(x, mods, gain, weights, tm, mod_row):
    t, d = x.shape
    n_out = len(weights)
    in_specs = [pl.BlockSpec((tm, d), lambda i: (i, 0)),
                pl.BlockSpec((1, 6, d), lambda i: (mod_row(i), 0, 0)),
                pl.BlockSpec((1, d), lambda i: (0, 0))]
    in_specs += [pl.BlockSpec(w.shape, lambda i: (0, 0)) for w in weights]
    return pl.pallas_call(
        functools.partial(_proj_kernel, n_out=n_out),
        grid=(t // tm,),
        in_specs=in_specs,
        out_specs=[pl.BlockSpec((tm, w.shape[1]), lambda i: (i, 0)) for w in weights],
        out_shape=[jax.ShapeDtypeStruct((t, w.shape[1]), F32) for w in weights],
        compiler_params=_cparams(("parallel",)),
        name="proj",
    )(x, mods, gain.reshape(1, d), *weights)


def _chunk_pos(s, d, nc_ctx, nc_tot):
    back = jnp.where(s < nc_ctx, nc_ctx - 1 - s, nc_tot + nc_ctx - 1 - s)
    return jnp.where(d == 0, s, back)


def _order_masks(d, c):
    row = lax.broadcasted_iota(jnp.int32, (c, c), 0)
    col = lax.broadcasted_iota(jnp.int32, (c, c), 1)
    ahead = (row - col) * jnp.where(d == 0, 1, -1)
    return ahead >= 0, ahead > 0


def _gla_kernel(q_ref, k_ref, v_ref, dec_ref, dw_ref, db_ref, o_ref, st_ref):
    d = pl.program_id(0)
    c = q_ref.shape[0]

    @pl.when(pl.program_id(2) == 0)
    def _():
        st_ref[...] = jnp.zeros_like(st_ref)

    g = _log_sigmoid(_dot(dec_ref[...], dw_ref[0]) + db_ref[0]) / GLA_TAU
    incl, _ = _order_masks(d, c)
    b = _dot(incl.astype(F32), g)
    b_tot = jnp.sum(g, axis=0, keepdims=True)
    q_in = q_ref[...] * (GLA_DK ** -0.5) * jnp.exp(b)
    k_out = k_ref[...] * jnp.exp(-b)
    k_end = k_ref[...] * jnp.exp(b_tot - b)
    decay_tot = jnp.exp(b_tot)
    v = v_ref[...]
    heads = range(GLA_HEADS)
    ks = [slice(h * GLA_DK, (h + 1) * GLA_DK) for h in heads]
    vs = [slice(h * GLA_DV, (h + 1) * GLA_DV) for h in heads]
    att = [jnp.where(incl, _dot_nt(q_in[:, s_], k_out[:, s_]), 0.0) for s_ in ks]
    st = [st_ref[h] for h in heads]
    o_ref[0] = jnp.concatenate([_dot(att[h], v[:, vs[h]]) + _dot_nt(q_in[:, ks[h]], st[h]) for h in heads], axis=1)
    for h in heads:
        st_ref[h] = st[h] * decay_tot[:, ks[h]] + _dot_tn(v[:, vs[h]], k_end[:, ks[h]])


def _gla(zg, zdec, dec_w_pad, dec_b, batch, nc_ctx, nc_tot):
    t = zg.shape[0]
    c = SEQ_CHUNK
    hk, hv = GLA_HEADS * GLA_DK, GLA_HEADS * GLA_DV

    def rows(d, b, s):
        return b * nc_tot + _chunk_pos(s, d, nc_ctx, nc_tot)

    return pl.pallas_call(
        _gla_kernel,
        grid=(2, batch, nc_tot),
        in_specs=[pl.BlockSpec((c, hk), lambda d, b, s: (rows(d, b, s), 0)),
                  pl.BlockSpec((c, hk), lambda d, b, s: (rows(d, b, s), 1)),
                  pl.BlockSpec((c, hv), lambda d, b, s: (rows(d, b, s), 1)),
                  pl.BlockSpec((c, 128), lambda d, b, s: (rows(d, b, s), 0)),
                  pl.BlockSpec((1, 128, hk), lambda d, b, s: (d, 0, 0)),
                  pl.BlockSpec((1, 1, hk), lambda d, b, s: (d, 0, 0))],
        out_specs=pl.BlockSpec((1, c, hv), lambda d, b, s: (d, rows(d, b, s), 0)),
        out_shape=jax.ShapeDtypeStruct((2, t, hv), F32),
        scratch_shapes=[pltpu.VMEM((GLA_HEADS, GLA_DV, GLA_DK), F32)],
        compiler_params=_cparams(("parallel", "parallel", "arbitrary")),
        name="gla_scan",
    )(zg, zg, zg, zdec, dec_w_pad, dec_b)


def _rope_swap(x):
    n = x.shape[-1]
    lane = lax.broadcasted_iota(jnp.int32, x.shape, x.ndim - 1)
    half = HEAD_DIM // 2
    return jnp.where(lane % HEAD_DIM < half, pltpu.roll(x, n - half, x.ndim - 1), pltpu.roll(x, half, x.ndim - 1))


def _qk_prep_kernel(q_ref, kv_ref, cos_ref, sin_ref, qg_ref, kg_ref, bdq_ref, bdk_ref, qo_ref, ko_ref, vo_ref):
    def norm_rope(x, gain, bd, n_heads):
        ms = _dot(x * x, bd) * (1.0 / HEAD_DIM)
        xn = x * lax.rsqrt(ms + NORM_EPS) * gain
        cos = jnp.concatenate([cos_ref[...]] * n_heads, axis=1)
        sin = jnp.concatenate([sin_ref[...]] * n_heads, axis=1)
        return xn * cos + _rope_swap(xn) * sin

    q = norm_rope(q_ref[...], qg_ref[...], bdq_ref[...], ATT_HEADS)
    qo_ref[...] = (q * (HEAD_DIM ** -0.5)).astype(BF16)
    kw = ATT_KV_HEADS * HEAD_DIM
    kv = kv_ref[...]
    ko_ref[...] = norm_rope(kv[:, :kw], kg_ref[...], bdk_ref[...], ATT_KV_HEADS).astype(BF16)
    vo_ref[...] = kv[:, kw:].astype(BF16)


def _qk_prep(zq, zkv, cos64, sin64, q_gain, k_gain, tm, l_tot):
    t = zq.shape[0]
    qw, kw = ATT_HEADS * HEAD_DIM, ATT_KV_HEADS * HEAD_DIM
    n_pos = l_tot // tm
    return pl.pallas_call(
        _qk_prep_kernel,
        grid=(t // tm,),
        in_specs=[pl.BlockSpec((tm, qw), lambda i: (i, 0)),
                  pl.BlockSpec((tm, 2 * kw), lambda i: (i, 0)),
                  pl.BlockSpec((tm, HEAD_DIM), lambda i: (i % n_pos, 0)),
                  pl.BlockSpec((tm, HEAD_DIM), lambda i: (i % n_pos, 0)),
                  pl.BlockSpec((1, qw), lambda i: (0, 0)),
                  pl.BlockSpec((1, kw), lambda i: (0, 0)),
                  pl.BlockSpec((qw, qw), lambda i: (0, 0)),
                  pl.BlockSpec((kw, kw), lambda i: (0, 0))],
        out_specs=[pl.BlockSpec((tm, qw), lambda i: (i, 0)),
                   pl.BlockSpec((tm, kw), lambda i: (i, 0)),
                   pl.BlockSpec((tm, kw), lambda i: (i, 0))],
        out_shape=[jax.ShapeDtypeStruct((t, qw), BF16),
                   jax.ShapeDtypeStruct((t, kw), BF16),
                   jax.ShapeDtypeStruct((t, kw), BF16)],
        compiler_params=_cparams(("parallel",)),
        name="qk_prep",
    )(zq, zkv, cos64, sin64, jnp.tile(q_gain, ATT_HEADS).reshape(1, qw), jnp.tile(k_gain, ATT_KV_HEADS).reshape(1, kw),
      _block_ones(ATT_HEADS, HEAD_DIM), _block_ones(ATT_KV_HEADS, HEAD_DIM))


def _attn_kernel(q_ref, kp_ref, kc_ref, kn_ref, kx_ref, vp_ref, vc_ref, vn_ref, vx_ref, sink_ref, o_ref,
                 *, n_ctx_blocks, n_lat_blocks):
    blk = ATT_BLOCK
    n = pl.program_id(1)
    m = n - n_ctx_blocks
    is_lat = n >= n_ctx_blocks
    l_ctx = kx_ref.shape[0]
    width = 3 * blk + l_ctx
    rows = ATT_GROUP * blk
    r = lax.broadcasted_iota(jnp.int32, (rows, width), 0) % blk
    c = lax.broadcasted_iota(jnp.int32, (rows, width), 1)
    lat = is_lat.astype(jnp.int32)
    has_prev = lat * (m >= 1).astype(jnp.int32)
    has_next = lat * (m <= n_lat_blocks - 2).astype(jnp.int32)
    valid = jnp.where(c < blk, (c >= r).astype(jnp.int32) * has_prev,
                      jnp.where(c < 2 * blk, lat,
                                jnp.where(c < 3 * blk, (c - 2 * blk <= r).astype(jnp.int32) * has_next, 1))) > 0
    q = q_ref[...]
    sink = sink_ref[...]
    outs = [None] * ATT_HEADS
    for kvh in range(ATT_KV_HEADS):
        ks = slice(kvh * HEAD_DIM, (kvh + 1) * HEAD_DIM)
        kw = jnp.concatenate([kp_ref[:, ks], kc_ref[:, ks], kn_ref[:, ks], kx_ref[:, ks]], axis=0)
        vw = jnp.concatenate([vp_ref[:, ks], vc_ref[:, ks], vn_ref[:, ks], vx_ref[:, ks]], axis=0)
        heads = range(kvh * ATT_GROUP, (kvh + 1) * ATT_GROUP)
        qg = jnp.concatenate([q[:, h * HEAD_DIM:(h + 1) * HEAD_DIM] for h in heads], axis=0)
        s = jnp.where(valid, _dot_nt(qg, kw), -jnp.inf)
        sk = jnp.concatenate([jnp.broadcast_to(sink[h:h + 1, 0:1], (blk, 1)) for h in heads], axis=0)
        mx = jnp.maximum(jnp.max(s, axis=-1, keepdims=True), sk)
        p = jnp.exp(s - mx)
        denom = jnp.sum(p, axis=-1, keepdims=True) + jnp.exp(sk - mx)
        o = _dot(p.astype(BF16), vw) / denom
        for g, h in enumerate(heads):
            outs[h] = o[g * blk:(g + 1) * blk]
    o_ref[...] = jnp.concatenate(outs, axis=1)


def _attention(qn, kn, vn, sink, batch, l_ctx, l_tot):
    t = qn.shape[0]
    blk = ATT_BLOCK
    nq = l_tot // blk
    nc = l_ctx // blk
    nl = nq - nc
    qw, kw = ATT_HEADS * HEAD_DIM, ATT_KV_HEADS * HEAD_DIM

    def win(off):
        def index(b, n):
            m = jnp.clip(n - nc + off, 0, nl - 1)
            return (b * nq + nc + m, 0)
        return pl.BlockSpec((blk, kw), index)

    ctx_spec = pl.BlockSpec((l_ctx, kw), lambda b, n: (b * (l_tot // l_ctx), 0))
    return pl.pallas_call(
        functools.partial(_attn_kernel, n_ctx_blocks=nc, n_lat_blocks=nl),
        grid=(batch, nq),
        in_specs=[pl.BlockSpec((blk, qw), lambda b, n: (b * nq + n, 0)),
                  win(-1), win(0), win(1), ctx_spec,
                  win(-1), win(0), win(1), ctx_spec,
                  pl.BlockSpec((ATT_HEADS, 128), lambda b, n: (0, 0))],
        out_specs=pl.BlockSpec((blk, qw), lambda b, n: (b * nq + n, 0)),
        out_shape=jax.ShapeDtypeStruct((t, qw), F32),
        compiler_params=_cparams(("parallel", "parallel")),
        name="window_attention",
    )(qn, kn, kn, kn, kn, vn, vn, vn, vn, jnp.broadcast_to(sink.astype(F32)[:, None], (ATT_HEADS, 128)))


def _route(logits_t, bias_col):
    scores = jax.nn.sigmoid(logits_t)
    sel = scores + bias_col
    rows = [sel[e:e + 1] for e in range(N_EXPERTS)]
    grp = []
    for g in range(N_GROUPS):
        r = rows[g * PER_GROUP:(g + 1) * PER_GROUP]
        best = None
        for i in range(PER_GROUP):
            for j in range(i + 1, PER_GROUP):
                pair = r[i] + r[j]
                best = pair if best is None else jnp.maximum(best, pair)
        grp.append(best)
    g_best = jnp.zeros_like(grp[0], dtype=jnp.int32)
    g_val = grp[0]
    for g in range(1, N_GROUPS):
        take = grp[g] > g_val
        g_best = jnp.where(take, g, g_best)
        g_val = jnp.where(take, grp[g], g_val)
    neg = -jnp.inf
    masked = [jnp.where(g_best == e // PER_GROUP, rows[e], neg) for e in range(N_EXPERTS)]
    i1 = jnp.zeros_like(g_best)
    v1 = masked[0]
    for e in range(1, N_EXPERTS):
        take = masked[e] > v1
        i1 = jnp.where(take, e, i1)
        v1 = jnp.where(take, masked[e], v1)
    i2 = jnp.full_like(g_best, -1)
    v2 = jnp.full_like(v1, neg)
    for e in range(N_EXPERTS):
        take = jnp.logical_and(i1 != e, masked[e] > v2)
        i2 = jnp.where(take, e, i2)
        v2 = jnp.where(take, masked[e], v2)
    w1 = jnp.zeros_like(v1)
    w2 = jnp.zeros_like(v1)
    for e in range(N_EXPERTS):
        w1 = jnp.where(i1 == e, scores[e:e + 1], w1)
        w2 = jnp.where(i2 == e, scores[e:e + 1], w2)
    inv = 1.0 / (w1 + w2)
    pad = jnp.zeros_like(w1)
    return jnp.concatenate([i1.astype(F32), i2.astype(F32), w1 * inv, w2 * inv, pad, pad, pad, pad], axis=0)


def _residual_and_route(x, out, m, ffn_gain, rw_ref, rb_ref, x_ref, h_ref, g_ref):
    x1 = x + m[2:3] * out
    x_ref[...] = x1
    h = _modulated_norm(x1, ffn_gain, m[3:4], m[4:5])
    h_ref[...] = h
    logits_t = lax.dot_general(rw_ref[...], h, (((1,), (1,)), ((), ())), precision=lax.Precision.HIGHEST,
                               preferred_element_type=F32)
    g_ref[...] = _route(logits_t, rb_ref[...])


def _merge_even_kernel(o0_ref, o1_ref, gg_ref, oa_ref, x_ref, mod_ref, gn_ref, bd_ref, w1_ref, w2_ref,
                       fg_ref, rw_ref, rb_ref, xo_ref, ho_ref, go_ref):
    og = o0_ref[0] + o1_ref[0]
    ms = _dot(og * og, bd_ref[...]) * (1.0 / GLA_DV)
    g = og * lax.rsqrt(ms + NORM_EPS) * gn_ref[...] * _silu(gg_ref[...])
    out = _dot(g.astype(BF16), w1_ref[...]) + _dot(oa_ref[...].astype(BF16), w2_ref[...])
    _residual_and_route(x_ref[...], out, mod_ref[0], fg_ref[...], rw_ref, rb_ref, xo_ref, ho_ref, go_ref)


def _merge_even(o_gla, zg, o_att, x, mods, out_norm, w_out, ffn_gain, router_wt, router_b, tm, mod_row):
    t, d = x.shape
    hv = GLA_HEADS * GLA_DV
    qw = ATT_HEADS * HEAD_DIM
    full = lambda a: pl.BlockSpec(a.shape, lambda i: (0,) * a.ndim)
    gn = jnp.tile(out_norm, GLA_HEADS).reshape(1, hv)
    bd = _block_ones(GLA_HEADS, GLA_DV)
    w1, w2 = w_out[:hv].astype(BF16), w_out[hv:].astype(BF16)
    fg = ffn_gain.reshape(1, d)
    rb = router_b.reshape(N_EXPERTS, 1)
    return pl.pallas_call(
        _merge_even_kernel,
        grid=(t // tm,),
        in_specs=[pl.BlockSpec((1, tm, hv), lambda i: (0, i, 0)),
                  pl.BlockSpec((1, tm, hv), lambda i: (1, i, 0)),
                  pl.BlockSpec((tm, hv), lambda i: (i, 2)),
                  pl.BlockSpec((tm, qw), lambda i: (i, 0)),
                  pl.BlockSpec((tm, d), lambda i: (i, 0)),
                  pl.BlockSpec((1, 6, d), lambda i: (mod_row(i), 0, 0)),
                  full(gn), full(bd), full(w1), full(w2), full(fg), full(router_wt), full(rb)],
        out_specs=[pl.BlockSpec((tm, d), lambda i: (i, 0)),
                   pl.BlockSpec((tm, d), lambda i: (i, 0)),
                   pl.BlockSpec((8, tm), lambda i: (0, i))],
        out_shape=[jax.ShapeDtypeStruct((t, d), F32),
                   jax.ShapeDtypeStruct((t, d), F32),
                   jax.ShapeDtypeStruct((8, t), F32)],
        compiler_params=_cparams(("parallel",)),
        name="merge_even",
    )(o_gla, o_gla, zg, o_att, x, mods, gn, bd, w1, w2, fg, router_wt, rb)


def _moe_plan(route, t, rows):
    n_tiles = 2 * t // rows + N_EXPERTS
    n_rows = n_tiles * rows
    experts = jnp.arange(N_EXPERTS, dtype=jnp.int32)[:, None]
    eid = jnp.concatenate([route[0], route[1]]).astype(jnp.int32)
    slot = jnp.arange(2 * t, dtype=jnp.int32)
    gate = jnp.concatenate([route[2], route[3]])
    _, s_slot, s_gate = lax.sort((eid, slot, gate), num_keys=1, is_stable=True)
    counts = jnp.sum((eid[None, :] == experts).astype(jnp.int32), axis=1)
    padded = (counts + rows - 1) // rows * rows
    p_end = jnp.cumsum(padded)
    p_start = p_end - padded
    c_start = jnp.cumsum(counts) - counts
    r_all = jnp.arange(n_rows, dtype=jnp.int32)
    e_of = jnp.minimum(jnp.sum((r_all[None, :] >= p_end[:, None]).astype(jnp.int32), axis=0), N_EXPERTS - 1)
    r_in = r_all - p_start[e_of]
    over = r_in - counts[e_of]
    valid = over < 0
    src = jnp.clip(c_start[e_of] + r_in, 0, 2 * t - 1)
    g_slot = s_slot[src]
    row_tok = jnp.where(valid, jnp.where(g_slot >= t, g_slot - t, g_slot), 0)
    row_gate = jnp.where(valid, s_gate[src], 0.0)
    row_dst = jnp.where(valid, g_slot, 2 * t + e_of * rows + jnp.clip(over, 0, rows - 1))
    n_used = (p_end[-1] // rows).astype(jnp.int32)
    tile_e = e_of[::rows]
    return (row_tok.reshape(n_tiles, 1, rows), row_gate.reshape(n_rows, 1), row_dst.reshape(n_tiles, 1, rows),
            tile_e.astype(jnp.int32), n_used.reshape(1))


def _moe_experts_kernel(te_ref, nu_ref, tok_ref, tokn_ref, dst_ref, gate_ref, wg_ref, wu_ref, wd_ref, h_hbm,
                        y_hbm, hbuf, ybuf, wgb, wub, wdb, sem_g, sem_s):
    j = pl.program_id(0)
    n_used = nu_ref[0]
    slot = j % 2
    rows = hbuf.shape[1]

    def start_gather(idx_ref, s):
        for r in range(rows):
            pltpu.make_async_copy(h_hbm.at[pl.ds(idx_ref[0, 0, r], 1)], hbuf.at[s, pl.ds(r, 1)], sem_g.at[s]).start()

    def wait_gather(s):
        pltpu.make_async_copy(h_hbm.at[pl.ds(0, rows)], hbuf.at[s], sem_g.at[s]).wait()

    def wait_scatter(s):
        pltpu.make_async_copy(ybuf.at[s], y_hbm.at[pl.ds(0, rows)], sem_s.at[s]).wait()

    @pl.when(j == 0)
    def _():
        start_gather(tok_ref, 0)
        ybuf[...] = jnp.zeros(ybuf.shape, F32)
        n_real = y_hbm.shape[0] - N_EXPERTS * rows
        fills = [pltpu.make_async_copy(ybuf.at[k % 2], y_hbm.at[pl.ds(n_real + k * rows, rows)], sem_s.at[k % 2])
                 for k in range(N_EXPERTS)]
        for f in fills:
            f.start()
        for f in fills[2:]:
            f.wait()

    active = j < n_used
    changed = jnp.logical_or(j == 0, te_ref[j] != te_ref[jnp.maximum(j - 1, 0)])

    @pl.when(jnp.logical_and(active, changed))
    def _():
        wgb[...] = wg_ref[0].astype(BF16)
        wub[...] = wu_ref[0].astype(BF16)
        wdb[...] = wd_ref[0].astype(BF16)

    @pl.when(active)
    def _():
        wait_gather(slot)
        start_gather(tokn_ref, 1 - slot)
        h = hbuf[slot].astype(BF16)
        act = _silu(_dot(h, wgb[...])) * _dot(h, wub[...])
        y = _dot((act * gate_ref[...]).astype(BF16), wdb[...])
        wait_scatter(slot)
        ybuf[slot] = y
        for r in range(rows):
            pltpu.make_async_copy(ybuf.at[slot, pl.ds(r, 1)], y_hbm.at[pl.ds(dst_ref[0, 0, r], 1)], sem_s.at[slot]).start()

        @pl.when(j == n_used - 1)
        def _():
            wait_scatter(slot)
            wait_scatter(1 - slot)
            wait_gather(1 - slot)


def _moe_combine_kernel(ya_ref, yb_ref, x_ref, mod_ref, o_ref):
    o_ref[...] = x_ref[...] + mod_ref[0][5:6] * (ya_ref[...] + yb_ref[...])


def _moe(h, route, x, mods, w_gate, w_up, w_down, layer, tm, mod_row):
    t, d = x.shape
    rows = MOE_TILE
    row_tok, row_gate, row_dst, tile_e, n_used = _moe_plan(route, t, rows)
    n_tiles = row_tok.shape[0]
    n_pair_rows = 2 * t + N_EXPERTS * rows
    smem_tile = lambda nxt: pl.BlockSpec((1, 1, rows), lambda j, te, nu: (jnp.minimum(j + nxt, nu[0] - 1), 0, 0),
                                         memory_space=pltpu.SMEM)
    pairs = pl.pallas_call(
        _moe_experts_kernel,
        grid_spec=pltpu.PrefetchScalarGridSpec(
            num_scalar_prefetch=2,
            grid=(n_tiles,),
            in_specs=[smem_tile(0), smem_tile(1), smem_tile(0),
                      pl.BlockSpec((rows, 1), lambda j, te, nu: (j, 0)),
                      pl.BlockSpec((None, 1, d, D_EXPERT), lambda j, te, nu: (layer, te[j], 0, 0)),
                      pl.BlockSpec((None, 1, d, D_EXPERT), lambda j, te, nu: (layer, te[j], 0, 0)),
                      pl.BlockSpec((None, 1, D_EXPERT, d), lambda j, te, nu: (layer, te[j], 0, 0)),
                      pl.BlockSpec(memory_space=pl.ANY)],
            out_specs=pl.BlockSpec(memory_space=pl.ANY),
            scratch_shapes=[pltpu.VMEM((2, rows, d), F32), pltpu.VMEM((2, rows, d), F32),
                            pltpu.VMEM((d, D_EXPERT), BF16), pltpu.VMEM((d, D_EXPERT), BF16),
                            pltpu.VMEM((D_EXPERT, d), BF16),
                            pltpu.SemaphoreType.DMA((2,)), pltpu.SemaphoreType.DMA((2,))]),
        out_shape=jax.ShapeDtypeStruct((n_pair_rows, d), F32),
        compiler_params=_cparams(("arbitrary",)),
        name="moe_experts",
    )(tile_e, n_used, row_tok, row_tok, row_dst, row_gate, w_gate, w_up, w_down, h)
    return pl.pallas_call(
        _moe_combine_kernel,
        grid=(t // tm,),
        in_specs=[pl.BlockSpec((tm, d), lambda i: (i, 0)),
                  pl.BlockSpec((tm, d), lambda i: (t // tm + i, 0)),
                  pl.BlockSpec((tm, d), lambda i: (i, 0)),
                  pl.BlockSpec((1, 6, d), lambda i: (mod_row(i), 0, 0))],
        out_specs=pl.BlockSpec((tm, d), lambda i: (i, 0)),
        out_shape=jax.ShapeDtypeStruct((t, d), F32),
        compiler_params=_cparams(("parallel",)),
        name="moe_combine",
    )(pairs, pairs, x, mods)


def _chan_dft_kernel(z_ref, w_ref, o_ref):
    res = _dot(z_ref[...].astype(BF16), w_ref[...]).astype(BF16)
    o_ref[0] = res[:, :FOURIER_WIDTH]
    o_ref[1] = res[:, FOURIER_WIDTH:]


def _chan_dft(zf, w, batch, l_ctx, l_lat, tm):
    l_tot = l_ctx + l_lat
    nt = l_lat // tm
    fw = FOURIER_WIDTH
    return pl.pallas_call(
        _chan_dft_kernel,
        grid=(batch, nt),
        in_specs=[pl.BlockSpec((tm, fw), lambda b, i: (b * (l_tot // tm) + l_ctx // tm + i, 0)),
                  pl.BlockSpec(w.shape, lambda b, i: (0, 0))],
        out_specs=pl.BlockSpec((2, tm, fw), lambda b, i: (0, i, b)),
        out_shape=jax.ShapeDtypeStruct((2, l_lat, batch * fw), BF16),
        compiler_params=_cparams(("parallel", "parallel")),
        name="fourier_channels",
    )(zf, w)


def _seq_dft_kernel(c_ref, s_ref, z_ref, o_ref, acc_ref):
    k = pl.program_id(1)

    @pl.when(k == 0)
    def _():
        acc_ref[...] = jnp.zeros_like(acc_ref)

    acc_ref[...] += _dot(c_ref[...], z_ref[0]) + _dot(s_ref[...], z_ref[1])

    @pl.when(k == pl.num_programs(1) - 1)
    def _():
        o_ref[...] = acc_ref[...]


def _seq_dft(cos_t, sin_t, zc, batch, tm, tk):
    l = cos_t.shape[0]
    fw = FOURIER_WIDTH
    return pl.pallas_call(
        _seq_dft_kernel,
        grid=(l // tm, l // tk),
        in_specs=[pl.BlockSpec((tm, tk), lambda i, k: (i, k)),
                  pl.BlockSpec((tm, tk), lambda i, k: (i, k)),
                  pl.BlockSpec((2, tk, batch * fw), lambda i, k: (0, k, 0))],
        out_specs=pl.BlockSpec((tm, batch * fw), lambda i, k: (i, 0)),
        out_shape=jax.ShapeDtypeStruct((l, batch * fw), F32),
        scratch_shapes=[pltpu.VMEM((tm, batch * fw), F32)],
        compiler_params=_cparams(("parallel", "arbitrary")),
        name="fourier_sequence",
    )(cos_t, sin_t, zc)


def _dft_tables(l):
    m = jnp.arange(l, dtype=jnp.int32)[:, None]
    n1 = l // 64
    a = (m * (jnp.arange(n1, dtype=jnp.int32)[None, :] * 64)) % l
    b = (m * jnp.arange(64, dtype=jnp.int32)[None, :]) % l
    wa = a.astype(F32) * (2.0 * np.pi / l)
    wb = b.astype(F32) * (2.0 * np.pi / l)
    ca, sa, cb, sb = jnp.cos(wa), jnp.sin(wa), jnp.cos(wb), jnp.sin(wb)
    cos_t = (ca[:, :, None] * cb[:, None, :] - sa[:, :, None] * sb[:, None, :]).reshape(l, l).astype(BF16)
    sin_t = (sa[:, :, None] * cb[:, None, :] + ca[:, :, None] * sb[:, None, :]).reshape(l, l).astype(BF16)
    gd = FOURIER_GROUP_DIM
    cc = (jnp.arange(gd, dtype=jnp.int32)[:, None] * jnp.arange(gd, dtype=jnp.int32)[None, :]) % gd
    wc = cc.astype(F32) * (2.0 * np.pi / gd)
    scale = 1.0 / np.sqrt(float(l) * gd)
    eye = jnp.eye(FOURIER_WIDTH // gd, dtype=F32)
    chan = jnp.concatenate([jnp.kron(eye, jnp.cos(wc)), -jnp.kron(eye, jnp.sin(wc))], axis=1) * scale
    return cos_t, sin_t, chan.astype(BF16)


def _rwkv_kernel(z_ref, zp_ref, zn_ref, mu_ref, kks_ref, ka_ref, rk_ref, w0_ref, w2_ref, a0_ref, a2_ref, g2_ref,
                 bd_ref, y_ref, gate_ref, st_ref, *, nc_ctx, nc_tot):
    d = pl.program_id(0)
    s = pl.program_id(2)
    c = z_ref.shape[0]
    n = RWKV_DIM
    hd = HEAD_DIM

    @pl.when(s == 0)
    def _():
        st_ref[...] = jnp.zeros_like(st_ref)

    pos = _chunk_pos(s, d, nc_ctx, nc_tot)
    seg_first = jnp.logical_or(pos == 0, pos == nc_ctx)
    seg_last = jnp.logical_or(pos == nc_ctx - 1, pos == nc_tot - 1)
    z = z_ref[...]
    row = lax.broadcasted_iota(jnp.int32, z.shape, 0)
    prev_row = jnp.where(seg_first, 0.0, zp_ref[7:8, :])
    next_row = jnp.where(seg_last, 0.0, zn_ref[0:1, :])
    z_prev = jnp.where(row == 0, prev_row, pltpu.roll(z, 1, 0))
    z_next = jnp.where(row == c - 1, next_row, pltpu.roll(z, c - 1, 0))
    mu = mu_ref[...]
    zs = z + mu * (0.5 * (z_prev + z_next) - z)

    r, k, v = zs[:, 0:n], zs[:, n:2 * n], zs[:, 2 * n:3 * n]
    zw = zs[:, 3 * n:3 * n + RWKV_RANK_PAD]
    za = zs[:, 3 * n + RWKV_RANK_PAD:3 * n + 2 * RWKV_RANK_PAD]
    zg = zs[:, 3 * n + 2 * RWKV_RANK_PAD:3 * n + 3 * RWKV_RANK_PAD]

    kk = k * kks_ref[...]
    kk = kk * lax.rsqrt(_dot(kk * kk, bd_ref[...]) + L2_EPS)
    w_pre = w0_ref[0] + _dot(jnp.tanh(zw), w2_ref[0])
    w_log = _log_sigmoid(w_pre) - 0.5
    lw = -jnp.exp(w_log)
    a = jax.nn.sigmoid(a0_ref[0] + _dot(za, a2_ref[0]))
    kd = k * (1.0 + (a - 1.0) * ka_ref[...])
    beta = kk * a
    gate_ref[0] = _dot(jax.nn.sigmoid(zg), g2_ref[...])

    incl, strict = _order_masks(d, c)
    cl = _dot(incl.astype(F32), lw)
    c_tot = jnp.sum(lw, axis=0, keepdims=True)
    grow = jnp.exp(-cl)
    k_s = kd * grow
    b_s = beta * grow
    kap_s = kk * jnp.exp(cl - lw)
    r_s = r * jnp.exp(cl)
    tail = jnp.exp(c_tot - cl)
    k_e = kd * tail
    b_e = beta * tail
    gam_tot = jnp.exp(c_tot)
    bonus_w = r * kd * rk_ref[...]

    eye = (lax.broadcasted_iota(jnp.int32, (c, c), 0) == lax.broadcasted_iota(jnp.int32, (c, c), 1)).astype(F32)
    heads = range(RWKV_HEADS)
    col = [slice(h * hd, (h + 1) * hd) for h in heads]
    p = [_dot_nt(jnp.concatenate([kap_s[:, s_], r_s[:, s_]], axis=0),
                 jnp.concatenate([k_s[:, s_], b_s[:, s_]], axis=0)) for s_ in col]
    m1 = [jnp.where(strict, x[0:c, 0:c], 0.0) for x in p]
    m2 = [jnp.where(strict, x[0:c, c:2 * c], 0.0) for x in p]
    n1 = [jnp.where(incl, x[c:2 * c, 0:c], 0.0) for x in p]
    n2 = [jnp.where(incl, x[c:2 * c, c:2 * c], 0.0) for x in p]
    vh = [v[:, s_] for s_ in col]
    m1v = [_dot(a_, b_) for a_, b_ in zip(m1, vh)]
    t_inv = [eye - x for x in m2]
    q = [_dot(x, x) for x in m2]
    span = 2
    while True:
        t_new = [t_ + _dot(t_, q_) for t_, q_ in zip(t_inv, q)]
        span *= 2
        if span < c:
            q = [_dot(x, x) for x in q]
        t_inv = t_new
        if span >= c:
            break
    tx = [_dot(t_, jnp.concatenate([kap_s[:, s_], mv], axis=1)) for t_, s_, mv in zip(t_inv, col, m1v)]
    st = [st_ref[h] for h in heads]
    u = [_dot_nt(x[:, 0:hd], s_) + x[:, hd:2 * hd] for x, s_ in zip(tx, st)]
    ys = []
    for h in heads:
        y = _dot_nt(r_s[:, col[h]], st[h]) + _dot(n1[h], vh[h]) - _dot(n2[h], u[h])
        ys.append(y + jnp.sum(bonus_w[:, col[h]], axis=1, keepdims=True) * vh[h])
    for h in heads:
        st_ref[h] = st[h] * gam_tot[:, col[h]] + _dot_tn(vh[h], k_e[:, col[h]]) - _dot_tn(u[h], b_e[:, col[h]])
    y_ref[0] = jnp.concatenate(ys, axis=1)


def _rwkv(zr, mu, kk_scale, k_a, r_k, w0, w2_pad, a0, a2_pad, g2, batch, nc_ctx, nc_tot):
    t, zw_ = zr.shape
    c = SEQ_CHUNK
    n = RWKV_DIM
    bd = _block_ones(RWKV_HEADS, HEAD_DIM)
    full = lambda a: pl.BlockSpec(a.shape, lambda d, b, s: (0,) * a.ndim)
    dirn = lambda a: pl.BlockSpec((1,) + a.shape[1:], lambda d, b, s: (d,) + (0,) * (a.ndim - 1))

    def rows(d, b, s):
        return b * nc_tot + _chunk_pos(s, d, nc_ctx, nc_tot)

    sub = c // 8
    n_sub = t // 8
    vec = lambda a: a.reshape(1, -1)
    args = (vec(mu), vec(kk_scale), vec(k_a), vec(r_k), w0.reshape(2, 1, n), w2_pad, a0.reshape(2, 1, n), a2_pad, g2, bd)
    return pl.pallas_call(
        functools.partial(_rwkv_kernel, nc_ctx=nc_ctx, nc_tot=nc_tot),
        grid=(2, batch, nc_tot),
        in_specs=[pl.BlockSpec((c, zw_), lambda d, b, s: (rows(d, b, s), 0)),
                  pl.BlockSpec((8, zw_), lambda d, b, s: (jnp.maximum(rows(d, b, s) * sub - 1, 0), 0)),
                  pl.BlockSpec((8, zw_), lambda d, b, s: (jnp.minimum((rows(d, b, s) + 1) * sub, n_sub - 1), 0)),
                  full(args[0]), full(args[1]), full(args[2]), full(args[3]),
                  dirn(args[4]), dirn(args[5]), dirn(args[6]), dirn(args[7]), full(args[8]), full(args[9])],
        out_specs=[pl.BlockSpec((1, c, n), lambda d, b, s: (d, rows(d, b, s), 0)),
                   pl.BlockSpec((1, c, n), lambda d, b, s: (d, rows(d, b, s), 0))],
        out_shape=[jax.ShapeDtypeStruct((2, t, n), F32), jax.ShapeDtypeStruct((2, t, n), F32)],
        scratch_shapes=[pltpu.VMEM((RWKV_HEADS, HEAD_DIM, HEAD_DIM), F32)],
        compiler_params=_cparams(("parallel", "parallel", "arbitrary")),
        name="rwkv_scan",
    )(zr, zr, zr, *args)


def _merge_odd_kernel(y0_ref, y1_ref, gate_ref, fo_ref, x_ref, mod_ref, lg_ref, lb_ref, bd_ref, w1_ref, w2_ref,
                      fg_ref, rw_ref, rb_ref, xo_ref, ho_ref, go_ref):
    y = y0_ref[0] + y1_ref[0]
    bd = bd_ref[...]
    mean = _dot(y, bd) * (1.0 / HEAD_DIM)
    yc = y - mean
    var = _dot(yc * yc, bd) * (1.0 / HEAD_DIM)
    rw = (yc * lax.rsqrt(var + RWKV_GN_EPS) * lg_ref[...] + lb_ref[...]) * gate_ref[0]
    out = _dot(fo_ref[...].astype(BF16), w1_ref[...]) + _dot(rw.astype(BF16), w2_ref[...])
    _residual_and_route(x_ref[...], out, mod_ref[0], fg_ref[...], rw_ref, rb_ref, xo_ref, ho_ref, go_ref)


def _merge_odd(y, gate, fo, x, mods, ln_g, ln_b, w_out, ffn_gain, router_wt, router_b, tm, batch, l_ctx, l_lat):
    d = x.shape[1]
    n = RWKV_DIM
    fw = FOURIER_WIDTH
    l_tot = l_ctx + l_lat
    nt = l_lat // tm
    t_out = batch * l_lat
    full = lambda a: pl.BlockSpec(a.shape, lambda b, i: (0,) * a.ndim)
    src = lambda b, i: b * (l_tot // tm) + l_ctx // tm + i
    bd = _block_ones(RWKV_HEADS, HEAD_DIM)
    w1, w2 = w_out[:fw].astype(BF16), w_out[fw:].astype(BF16)
    lg, lb, fg, rb = ln_g.reshape(1, n), ln_b.reshape(1, n), ffn_gain.reshape(1, d), router_b.reshape(N_EXPERTS, 1)
    return pl.pallas_call(
        _merge_odd_kernel,
        grid=(batch, nt),
        in_specs=[pl.BlockSpec((1, tm, n), lambda b, i: (0, src(b, i), 0)),
                  pl.BlockSpec((1, tm, n), lambda b, i: (1, src(b, i), 0)),
                  pl.BlockSpec((1, tm, n), lambda b, i: (0, src(b, i), 0)),
                  pl.BlockSpec((tm, fw), lambda b, i: (i, b)),
                  pl.BlockSpec((tm, d), lambda b, i: (src(b, i), 0)),
                  pl.BlockSpec((1, 6, d), lambda b, i: (b, 0, 0)),
                  full(lg), full(lb), full(bd), full(w1), full(w2), full(fg), full(router_wt), full(rb)],
        out_specs=[pl.BlockSpec((tm, d), lambda b, i: (b * nt + i, 0)),
                   pl.BlockSpec((tm, d), lambda b, i: (b * nt + i, 0)),
                   pl.BlockSpec((8, tm), lambda b, i: (0, b * nt + i))],
        out_shape=[jax.ShapeDtypeStruct((t_out, d), F32),
                   jax.ShapeDtypeStruct((t_out, d), F32),
                   jax.ShapeDtypeStruct((8, t_out), F32)],
        compiler_params=_cparams(("parallel", "parallel")),
        name="merge_odd",
    )(y, y, gate, fo, x, mods, lg, lb, bd, w1, w2, fg, router_wt, rb)


def _rope_tables(l_ctx, l_lat):
    rows = l_lat // GRID_W
    row = jnp.repeat(jnp.arange(rows, dtype=F32), GRID_W)
    col = jnp.tile(jnp.arange(GRID_W, dtype=F32), rows)
    n_freq = HEAD_DIM // 4
    inv_freq = ROPE_THETA ** (-jnp.arange(n_freq, dtype=F32) / n_freq)
    ang = jnp.concatenate([row[:, None] * inv_freq, col[:, None] * inv_freq], axis=-1)
    cos, sin = jnp.cos(ang), jnp.sin(ang)
    cos64 = jnp.concatenate([cos, cos], axis=1)
    sin64 = jnp.concatenate([-sin, sin], axis=1)
    cos64 = jnp.concatenate([jnp.ones((l_ctx, HEAD_DIM), F32), cos64], axis=0)
    sin64 = jnp.concatenate([jnp.zeros((l_ctx, HEAD_DIM), F32), sin64], axis=0)
    return cos64, sin64


def _pad_rank(w):
    _, r, n = w.shape
    out = jnp.zeros((2, RWKV_RANK_PAD, n), w.dtype)
    out = out.at[0, 0:r].set(w[0])
    return out.at[1, r:2 * r].set(w[1])


def _even_layer(x, mods, p, batch, l_ctx, l_lat, tm, mod_row, tm_moe):
    l_tot = l_ctx + l_lat
    d = x.shape[1]
    nc_ctx, nc_tot = l_ctx // SEQ_CHUNK, l_tot // SEQ_CHUNK
    hk, hv = GLA_HEADS * GLA_DK, GLA_HEADS * GLA_DV
    qw, kw = ATT_HEADS * HEAD_DIM, ATT_KV_HEADS * HEAD_DIM
    w_in = p['w_in']
    o = np.cumsum([0, hk, hk, hv, hv, 2 * GLA_LOWRANK, qw, kw, kw])
    w_gla = jnp.concatenate([w_in[:, o[0]:o[4]]], axis=1).astype(BF16)
    w_dec = jnp.pad(w_in[:, o[4]:o[5]], ((0, 0), (0, 128 - 2 * GLA_LOWRANK))).astype(BF16)
    w_q = w_in[:, o[5]:o[6]].astype(BF16)
    w_kv = w_in[:, o[6]:o[8]].astype(BF16)
    zg, zdec, zq, zkv = _project(x, mods, p['norm_mix'], [w_gla, w_dec, w_q, w_kv], tm, mod_row)

    dec_w_pad = _pad_rank(p['dec_w'])
    o_gla = _gla(zg, zdec, dec_w_pad, p['dec_b'].reshape(2, 1, hk), batch, nc_ctx, nc_tot)

    cos64, sin64 = _rope_tables(l_ctx, l_lat)
    qn, kn, vn = _qk_prep(zq, zkv, cos64, sin64, p['q_norm'], p['k_norm'], tm, l_tot)
    o_att = _attention(qn, kn, vn, p['sink'], batch, l_ctx, l_tot)

    x1, h, gates = _merge_even(o_gla, zg, o_att, x, mods, p['out_norm'], p['w_out'], p['norm_ffn'],
                               p['router_wt'], p['router_b'], tm, mod_row)
    return _moe(h, gates, x1, mods, p['moe_g'], p['moe_u'], p['moe_d'], p['moe_layer'], tm, mod_row)


def _odd_layer(x, mods, p, batch, l_ctx, l_lat, tm, mod_row, tm_moe):
    l_tot = l_ctx + l_lat
    nc_ctx, nc_tot = l_ctx // SEQ_CHUNK, l_tot // SEQ_CHUNK
    n = RWKV_DIM
    fw = FOURIER_WIDTH
    w_in = p['w_in']
    rank_w, rank_a = p['w2'].shape[1], p['a2'].shape[1]
    o = np.cumsum([0, fw, n, n, n, 2 * rank_w, 2 * rank_a])
    pad_cols = lambda w: jnp.pad(w, ((0, 0), (0, RWKV_RANK_PAD - w.shape[1])))
    w_f = w_in[:, o[0]:o[1]].astype(BF16)
    w_r = jnp.concatenate([w_in[:, o[1]:o[4]], pad_cols(w_in[:, o[4]:o[5]]), pad_cols(w_in[:, o[5]:o[6]]),
                           w_in[:, o[6]:]], axis=1).astype(BF16)
    zf, zr = _project(x, mods, p['norm_mix'], [w_f, w_r], tm, mod_row)

    mu = p['mu']
    mu_r = jnp.concatenate([mu[0:3 * n], pad_cols(mu[None, 3 * n:3 * n + 2 * rank_w])[0],
                            pad_cols(mu[None, 3 * n + 2 * rank_w:3 * n + 2 * rank_w + 2 * rank_a])[0],
                            mu[3 * n + 2 * rank_w + 2 * rank_a:]])
    y, gate = _rwkv(zr, mu_r, p['kk_scale'], p['k_a'], p['r_k'].reshape(-1), p['w0'], _pad_rank(p['w2']),
                    p['a0'], _pad_rank(p['a2']), p['g2'], batch, nc_ctx, nc_tot)

    cos_t, sin_t, chan = _dft_tables(l_lat)
    zc = _chan_dft(zf, chan, batch, l_ctx, l_lat, tm)
    fo = _seq_dft(cos_t, sin_t, zc, batch, min(512, l_lat), min(1024, l_lat))

    x1, h, gates = _merge_odd(y, gate, fo, x, mods, p['ln_g'], p['ln_b'], p['w_out'], p['norm_ffn'],
                              p['router_wt'], p['router_b'], tm, batch, l_ctx, l_lat)
    lat_tiles = l_lat // tm_moe
    return _moe(h, gates, x1, mods, p['moe_g'], p['moe_u'], p['moe_d'], p['moe_layer'], tm_moe,
                lambda i: i // lat_tiles)


def kernel(x, c, ctx, c_ctx, ada_w, ada_b, norm_mix, norm_ffn, even_w_in, even_w_out, gla_dec_w, gla_dec_b, gla_out_norm, att_q_norm, att_k_norm, att_sink, odd_w_in, odd_w_out, rwkv_mu, rwkv_w0, rwkv_w2, rwkv_a0, rwkv_a2, rwkv_g2, rwkv_kk_scale, rwkv_k_a, rwkv_r_k, rwkv_ln_g, rwkv_ln_b, router_w, router_b, moe_w_gate, moe_w_up, moe_w_down):
    batch, l_lat, d = x.shape
    l_ctx = ctx.shape[1]
    l_tot = l_ctx + l_lat
    assert batch < 8 and ada_w.shape[0] == 2
    tm = 256 if (l_ctx % 256 == 0 and l_lat % 256 == 0) else 128
    tm_moe = 512 if (l_lat % 512 == 0 and tm == 256) else tm
    assert l_ctx % tm == 0 and l_lat % tm == 0 and l_tot % l_ctx == 0 and l_lat % GRID_W == 0

    xs = jnp.concatenate([ctx, x], axis=1).reshape(batch * l_tot, d)
    cc = jnp.concatenate([c, c_ctx[None, :], jnp.zeros((8 - batch - 1, d), F32)], axis=0)
    tiles_per_b = l_tot // tm
    ctx_tiles = l_ctx // tm

    def mod_row(i):
        return jnp.where(i % tiles_per_b < ctx_tiles, batch, i // tiles_per_b)

    router_wt = router_w.T
    moe = lambda layer: dict(moe_g=moe_w_gate, moe_u=moe_w_up, moe_d=moe_w_down, moe_layer=layer)

    mods_all = _modvec(cc, ada_w, ada_b)
    mods0 = mods_all[0]
    p0 = dict(w_in=even_w_in[0], w_out=even_w_out[0], dec_w=gla_dec_w[0], dec_b=gla_dec_b[0],
              out_norm=gla_out_norm[0], q_norm=att_q_norm[0], k_norm=att_k_norm[0], sink=att_sink[0],
              norm_mix=norm_mix[0], norm_ffn=norm_ffn[0], router_wt=router_wt, router_b=router_b, **moe(0))
    xs = _even_layer(xs, mods0, p0, batch, l_ctx, l_lat, tm, mod_row, tm_moe)

    mods1 = mods_all[1]
    p1 = dict(w_in=odd_w_in[0], w_out=odd_w_out[0], mu=rwkv_mu[0], w0=rwkv_w0[0], w2=rwkv_w2[0], a0=rwkv_a0[0],
              a2=rwkv_a2[0], g2=rwkv_g2[0], kk_scale=rwkv_kk_scale[0], k_a=rwkv_k_a[0], r_k=rwkv_r_k[0],
              ln_g=rwkv_ln_g[0], ln_b=rwkv_ln_b[0], norm_mix=norm_mix[1], norm_ffn=norm_ffn[1],
              router_wt=router_wt, router_b=router_b, **moe(1))
    out = _odd_layer(xs, mods1, p1, batch, l_ctx, l_lat, tm, mod_row, tm_moe)
    return out.reshape(batch, l_lat, d)
```

```python
import functools

import jax
import jax.numpy as jnp
import numpy as np
from jax import lax
from jax.experimental import pallas as pl
from jax.experimental.pallas import tpu as pltpu

F32 = jnp.float32
BF16 = jnp.bfloat16

GRID_W = 64
HEAD_DIM = 64
NORM_EPS = 1e-6
L2_EPS = 1e-12

GLA_DV = 64
GLA_DK = 32
GLA_HEADS = 8
GLA_LOWRANK = 16
GLA_TAU = 16.0

ATT_HEADS = 8
ATT_KV_HEADS = 2
ATT_GROUP = ATT_HEADS // ATT_KV_HEADS
ATT_BLOCK = 128
ROPE_THETA = 10000.0

FOURIER_GROUP_DIM = 64
FOURIER_WIDTH = 256

RWKV_DIM = 768
RWKV_HEADS = 12
RWKV_RANK_PAD = 128
RWKV_GN_EPS = 64e-5

N_EXPERTS = 16
N_GROUPS = 4
PER_GROUP = N_EXPERTS // N_GROUPS
D_EXPERT = 512
MOE_TILE = 256

SEQ_CHUNK = 64
VMEM_LIMIT = 56 * 1024 * 1024


def _cparams(sem):
    return pltpu.CompilerParams(dimension_semantics=sem, vmem_limit_bytes=VMEM_LIMIT)


def _dot(a, b):
    return jnp.dot(a, b, preferred_element_type=F32)


def _dot_nt(a, b):
    return lax.dot_general(a, b, (((1,), (1,)), ((), ())), preferred_element_type=F32)


def _dot_tn(a, b):
    return lax.dot_general(a, b, (((0,), (0,)), ((), ())), preferred_element_type=F32)


def _silu(x):
    return x * jax.nn.sigmoid(x)


def _log_sigmoid(x):
    return jnp.minimum(x, 0.0) - jnp.log(1.0 + jnp.exp(-jnp.abs(x)))


def _modulated_norm(x, gain, shift, scale):
    ms = jnp.mean(x * x, axis=-1, keepdims=True)
    return (x * lax.rsqrt(ms + NORM_EPS) * gain) * (1.0 + scale) + shift


def _block_ones(n_blocks, width):
    return jnp.kron(jnp.eye(n_blocks, dtype=F32), jnp.ones((width, width), F32))


def _modvec_kernel(c_ref, w_ref, b_ref, o_ref):
    o_ref[...] = _dot(_silu(c_ref[...]), w_ref[...]) + b_ref[...]


def _modvec(cc, w, b):
    d = cc.shape[1]
    layers, _, n = w.shape
    tn = n // 4
    out = pl.pallas_call(
        _modvec_kernel,
        grid=(layers, n // tn),
        in_specs=[pl.BlockSpec((8, d), lambda l, j: (0, 0)),
                  pl.BlockSpec((None, d, tn), lambda l, j: (l, 0, j)),
                  pl.BlockSpec((None, 1, tn), lambda l, j: (l, 0, j))],
        out_specs=pl.BlockSpec((None, 8, tn), lambda l, j: (l, 0, j)),
        out_shape=jax.ShapeDtypeStruct((layers, 8, n), F32),
        compiler_params=_cparams(("parallel", "parallel")),
        name="modvec",
    )(cc, w, b.reshape(layers, 1, n))
    return out.reshape(layers, 8, 6, d)


def _proj_kernel(x_ref, mod_ref, gain_ref, *refs, n_out):
    w_refs, z_refs = refs[:n_out], refs[n_out:]
    m = mod_ref[0]
    h = _modulated_norm(x_ref[...], gain_ref[...], m[0:1], m[1:2]).astype(BF16)
    for w_ref, z_ref in zip(w_refs, z_refs):
        z_ref[...] = _dot(h, w_ref[...])


def _project(x, mods, gain, weights, tm, mod_row):
    t, d = x.shape
    n_out = len(weights)
    in_specs = [pl.BlockSpec((tm, d), lambda i: (i, 0)),
                pl.BlockSpec((1, 6, d), lambda i: (mod_row(i), 0, 0)),
                pl.BlockSpec((1, d), lambda i: (0, 0))]
    in_specs += [pl.BlockSpec(w.shape, lambda i: (0, 0)) for w in weights]
    return pl.pallas_call(
        functools.partial(_proj_kernel, n_out=n_out),
        grid=(t // tm,),
        in_specs=in_specs,
        out_specs=[pl.BlockSpec((tm, w.shape[1]), lambda i: (i, 0)) for w in weights],
        out_shape=[jax.ShapeDtypeStruct((t, w.shape[1]), F32) for w in weights],
        compiler_params=_cparams(("parallel",)),
        name="proj",
    )(x, mods, gain.reshape(1, d), *weights)


def _chunk_pos(s, d, nc_ctx, nc_tot):
    back = jnp.where(s < nc_ctx, nc_ctx - 1 - s, nc_tot + nc_ctx - 1 - s)
    return jnp.where(d == 0, s, back)


def _visit_order(d, c):
    sign = 1 if d == 0 else -1
    return (lax.broadcasted_iota(jnp.int32, (c, c), 0) - lax.broadcasted_iota(jnp.int32, (c, c), 1)) * sign


def _gla_prepare(d, q_ref, k_ref, v_ref, dec_ref, dw_ref, db_ref):
    c = q_ref.shape[0]
    g = _log_sigmoid(_dot(dec_ref[...], dw_ref[d]) + db_ref[d]) / GLA_TAU
    b = _dot((_visit_order(d, c) >= 0).astype(F32), g)
    b_tot = jnp.sum(g, axis=0, keepdims=True)
    k = k_ref[...]
    return dict(q_in=q_ref[...] * (GLA_DK ** -0.5) * jnp.exp(b), k_out=k * jnp.exp(-b), k_end=k * jnp.exp(b_tot - b),
                decay=jnp.exp(b_tot), v=v_ref[...], sign=1 if d == 0 else -1)


def _gla_kernel(qf_ref, kf_ref, vf_ref, df_ref, qb_ref, kb_ref, vb_ref, db_ref, dw_ref, dbias_ref, of_ref, ob_ref, st_ref):
    c = qf_ref.shape[0]
    group = 4
    kw, vw = group * GLA_DK, group * GLA_DV

    @pl.when(pl.program_id(1) == 0)
    def _():
        st_ref[...] = jnp.zeros_like(st_ref)

    fwd = _gla_prepare(0, qf_ref, kf_ref, vf_ref, df_ref, dw_ref, dbias_ref)
    bwd = _gla_prepare(1, qb_ref, kb_ref, vb_ref, db_ref, dw_ref, dbias_ref)
    n_quads = GLA_HEADS // group
    prob = [(o, i) for o in (fwd, bwd) for i in range(n_quads)]
    klane = lax.broadcasted_iota(jnp.int32, (c, kw), 1) // GLA_DK
    half = lax.broadcasted_iota(jnp.int32, (c, 2 * GLA_DV), 1) < GLA_DV
    rowi = lax.broadcasted_iota(jnp.int32, (c, group * c), 0)
    coli = lax.broadcasted_iota(jnp.int32, (c, group * c), 1) % c
    own = (lax.broadcasted_iota(jnp.int32, (vw, kw), 0) // GLA_DV) == (lax.broadcasted_iota(jnp.int32, (vw, kw), 1) // GLA_DK)

    def bd_keys(y):
        return jnp.concatenate([jnp.where(klane == h, y, 0.0) for h in range(group)], axis=0)

    def bd_vals(y):
        return jnp.concatenate([jnp.where(half, y, 0.0), jnp.where(half, 0.0, y)], axis=0)

    ksl = lambda i: slice(i * kw, (i + 1) * kw)
    vsl = lambda i: slice(i * vw, (i + 1) * vw)
    q_in = [o['q_in'][:, ksl(i)] for o, i in prob]
    att = [jnp.where((rowi - coli) * o['sign'] >= 0, _dot_nt(q_, bd_keys(o['k_out'][:, ksl(i)])), 0.0)
           for q_, (o, i) in zip(q_in, prob)]
    st = [st_ref[j] for j in range(len(prob))]
    outs = []
    for j, (o, i) in enumerate(prob):
        v = o['v'][:, vsl(i)]
        intra = jnp.concatenate([_dot(att[j][:, p * 2 * c:(p + 1) * 2 * c], bd_vals(v[:, p * 2 * GLA_DV:(p + 1) * 2 * GLA_DV]))
                                 for p in range(group // 2)], axis=1)
        outs.append(intra + _dot_nt(q_in[j], st[j]))
    for j, (o, i) in enumerate(prob):
        upd = _dot_tn(o['v'][:, vsl(i)], o['k_end'][:, ksl(i)])
        st_ref[j] = st[j] * o['decay'][:, ksl(i)] + jnp.where(own, upd, 0.0)
    of_ref[...] = jnp.concatenate(outs[:n_quads], axis=1)
    ob_ref[...] = jnp.concatenate(outs[n_quads:], axis=1)


def _gla(zg, zdec, dec_w_pad, dec_b, batch, nc_ctx, nc_tot):
    t = zg.shape[0]
    c = SEQ_CHUNK
    hk, hv = GLA_HEADS * GLA_DK, GLA_HEADS * GLA_DV

    def specs(d):
        rw = lambda b, s: b * nc_tot + _chunk_pos(s, d, nc_ctx, nc_tot)
        return [pl.BlockSpec((c, hk), lambda b, s: (rw(b, s), 0)),
                pl.BlockSpec((c, hk), lambda b, s: (rw(b, s), 1)),
                pl.BlockSpec((c, hv), lambda b, s: (rw(b, s), 1)),
                pl.BlockSpec((c, 128), lambda b, s: (rw(b, s), 0))], pl.BlockSpec((c, hv), lambda b, s: (rw(b, s), 0))

    in_f, out_f = specs(0)
    in_b, out_b = specs(1)
    return pl.pallas_call(
        _gla_kernel,
        grid=(batch, nc_tot),
        in_specs=in_f + in_b + [pl.BlockSpec(dec_w_pad.shape, lambda b, s: (0, 0, 0)),
                                pl.BlockSpec(dec_b.shape, lambda b, s: (0, 0, 0))],
        out_specs=[out_f, out_b],
        out_shape=[jax.ShapeDtypeStruct((t, hv), F32)] * 2,
        scratch_shapes=[pltpu.VMEM((2 * GLA_HEADS // 4, 4 * GLA_DV, 4 * GLA_DK), F32)],
        compiler_params=_cparams(("parallel", "arbitrary")),
        name="gla_scan",
    )(zg, zg, zg, zdec, zg, zg, zg, zdec, dec_w_pad, dec_b)


def _rope_swap(x):
    n = x.shape[-1]
    lane = lax.broadcasted_iota(jnp.int32, x.shape, x.ndim - 1)
    half = HEAD_DIM // 2
    return jnp.where(lane % HEAD_DIM < half, pltpu.roll(x, n - half, x.ndim - 1), pltpu.roll(x, half, x.ndim - 1))


def _qk_prep_kernel(q_ref, kv_ref, cos_ref, sin_ref, qg_ref, kg_ref, bdq_ref, bdk_ref, qo_ref, ko_ref, vo_ref):
    def norm_rope(x, gain, bd, n_heads):
        ms = _dot(x * x, bd) * (1.0 / HEAD_DIM)
        xn = x * lax.rsqrt(ms + NORM_EPS) * gain
        cos = jnp.concatenate([cos_ref[...]] * n_heads, axis=1)
        sin = jnp.concatenate([sin_ref[...]] * n_heads, axis=1)
        return xn * cos + _rope_swap(xn) * sin

    q = norm_rope(q_ref[...], qg_ref[...], bdq_ref[...], ATT_HEADS)
    qo_ref[...] = (q * (HEAD_DIM ** -0.5)).astype(BF16)
    kw = ATT_KV_HEADS * HEAD_DIM
    kv = kv_ref[...]
    ko_ref[...] = norm_rope(kv[:, :kw], kg_ref[...], bdk_ref[...], ATT_KV_HEADS).astype(BF16)
    vo_ref[...] = kv[:, kw:].astype(BF16)


def _qk_prep(zq, zkv, cos64, sin64, q_gain, k_gain, tm, l_tot):
    t = zq.shape[0]
    qw, kw = ATT_HEADS * HEAD_DIM, ATT_KV_HEADS * HEAD_DIM
    n_pos = l_tot // tm
    return pl.pallas_call(
        _qk_prep_kernel,
        grid=(t // tm,),
        in_specs=[pl.BlockSpec((tm, qw), lambda i: (i, 0)),
                  pl.BlockSpec((tm, 2 * kw), lambda i: (i, 0)),
                  pl.BlockSpec((tm, HEAD_DIM), lambda i: (i % n_pos, 0)),
                  pl.BlockSpec((tm, HEAD_DIM), lambda i: (i % n_pos, 0)),
                  pl.BlockSpec((1, qw), lambda i: (0, 0)),
                  pl.BlockSpec((1, kw), lambda i: (0, 0)),
                  pl.BlockSpec((qw, qw), lambda i: (0, 0)),
                  pl.BlockSpec((kw, kw), lambda i: (0, 0))],
        out_specs=[pl.BlockSpec((tm, qw), lambda i: (i, 0)),
                   pl.BlockSpec((tm, kw), lambda i: (i, 0)),
                   pl.BlockSpec((tm, kw), lambda i: (i, 0))],
        out_shape=[jax.ShapeDtypeStruct((t, qw), BF16),
                   jax.ShapeDtypeStruct((t, kw), BF16),
                   jax.ShapeDtypeStruct((t, kw), BF16)],
        compiler_params=_cparams(("parallel",)),
        name="qk_prep",
    )(zq, zkv, cos64, sin64, jnp.tile(q_gain, ATT_HEADS).reshape(1, qw), jnp.tile(k_gain, ATT_KV_HEADS).reshape(1, kw),
      _block_ones(ATT_HEADS, HEAD_DIM), _block_ones(ATT_KV_HEADS, HEAD_DIM))


def _attn_kernel(q_ref, kp_ref, kc_ref, kn_ref, kx_ref, vp_ref, vc_ref, vn_ref, vx_ref, sink_ref, o_ref,
                 *, n_ctx_blocks, n_lat_blocks):
    blk = ATT_BLOCK
    n = pl.program_id(1)
    m = n - n_ctx_blocks
    is_lat = n >= n_ctx_blocks
    l_ctx = kx_ref.shape[0]
    width = 3 * blk + l_ctx
    rows = ATT_GROUP * blk
    r = lax.broadcasted_iota(jnp.int32, (rows, width), 0) % blk
    c = lax.broadcasted_iota(jnp.int32, (rows, width), 1)
    lat = is_lat.astype(jnp.int32)
    has_prev = lat * (m >= 1).astype(jnp.int32)
    has_next = lat * (m <= n_lat_blocks - 2).astype(jnp.int32)
    valid = jnp.where(c < blk, (c >= r).astype(jnp.int32) * has_prev,
                      jnp.where(c < 2 * blk, lat,
                                jnp.where(c < 3 * blk, (c - 2 * blk <= r).astype(jnp.int32) * has_next, 1))) > 0
    q = q_ref[...]
    sink = sink_ref[...]
    outs = [None] * ATT_HEADS
    for kvh in range(ATT_KV_HEADS):
        ks = slice(kvh * HEAD_DIM, (kvh + 1) * HEAD_DIM)
        kw = jnp.concatenate([kp_ref[:, ks], kc_ref[:, ks], kn_ref[:, ks], kx_ref[:, ks]], axis=0)
        vw = jnp.concatenate([vp_ref[:, ks], vc_ref[:, ks], vn_ref[:, ks], vx_ref[:, ks]], axis=0)
        heads = range(kvh * ATT_GROUP, (kvh + 1) * ATT_GROUP)
        qg = jnp.concatenate([q[:, h * HEAD_DIM:(h + 1) * HEAD_DIM] for h in heads], axis=0)
        s = jnp.where(valid, _dot_nt(qg, kw), -jnp.inf)
        sk = jnp.concatenate([jnp.broadcast_to(sink[h:h + 1, 0:1], (blk, 1)) for h in heads], axis=0)
        mx = jnp.maximum(jnp.max(s, axis=-1, keepdims=True), sk)
        p = jnp.exp(s - mx)
        denom = jnp.sum(p, axis=-1, keepdims=True) + jnp.exp(sk - mx)
        o = _dot(p.astype(BF16), vw) / denom
        for g, h in enumerate(heads):
            outs[h] = o[g * blk:(g + 1) * blk]
    o_ref[...] = jnp.concatenate(outs, axis=1)


def _attention(qn, kn, vn, sink, batch, l_ctx, l_tot):
    t = qn.shape[0]
    blk = ATT_BLOCK
    nq = l_tot // blk
    nc = l_ctx // blk
    nl = nq - nc
    qw, kw = ATT_HEADS * HEAD_DIM, ATT_KV_HEADS * HEAD_DIM

    def win(off):
        def index(b, n):
            m = jnp.clip(n - nc + off, 0, nl - 1)
            return (b * nq + nc + m, 0)
        return pl.BlockSpec((blk, kw), index)

    ctx_spec = pl.BlockSpec((l_ctx, kw), lambda b, n: (b * (l_tot // l_ctx), 0))
    return pl.pallas_call(
        functools.partial(_attn_kernel, n_ctx_blocks=nc, n_lat_blocks=nl),
        grid=(batch, nq),
        in_specs=[pl.BlockSpec((blk, qw), lambda b, n: (b * nq + n, 0)),
                  win(-1), win(0), win(1), ctx_spec,
                  win(-1), win(0), win(1), ctx_spec,
                  pl.BlockSpec((ATT_HEADS, 128), lambda b, n: (0, 0))],
        out_specs=pl.BlockSpec((blk, qw), lambda b, n: (b * nq + n, 0)),
        out_shape=jax.ShapeDtypeStruct((t, qw), F32),
        compiler_params=_cparams(("parallel", "parallel")),
        name="window_attention",
    )(qn, kn, kn, kn, kn, vn, vn, vn, vn, jnp.broadcast_to(sink.astype(F32)[:, None], (ATT_HEADS, 128)))


def _route(logits_t, bias_col):
    scores = jax.nn.sigmoid(logits_t)
    sel = scores + bias_col
    rows = [sel[e:e + 1] for e in range(N_EXPERTS)]
    grp = []
    for g in range(N_GROUPS):
        r = rows[g * PER_GROUP:(g + 1) * PER_GROUP]
        best = None
        for i in range(PER_GROUP):
            for j in range(i + 1, PER_GROUP):
                pair = r[i] + r[j]
                best = pair if best is None else jnp.maximum(best, pair)
        grp.append(best)
    g_best = jnp.zeros_like(grp[0], dtype=jnp.int32)
    g_val = grp[0]
    for g in range(1, N_GROUPS):
        take = grp[g] > g_val
        g_best = jnp.where(take, g, g_best)
        g_val = jnp.where(take, grp[g], g_val)
    neg = -jnp.inf
    masked = [jnp.where(g_best == e // PER_GROUP, rows[e], neg) for e in range(N_EXPERTS)]
    i1 = jnp.zeros_like(g_best)
    v1 = masked[0]
    for e in range(1, N_EXPERTS):
        take = masked[e] > v1
        i1 = jnp.where(take, e, i1)
        v1 = jnp.where(take, masked[e], v1)
    i2 = jnp.full_like(g_best, -1)
    v2 = jnp.full_like(v1, neg)
    for e in range(N_EXPERTS):
        take = jnp.logical_and(i1 != e, masked[e] > v2)
        i2 = jnp.where(take, e, i2)
        v2 = jnp.where(take, masked[e], v2)
    w1 = jnp.zeros_like(v1)
    w2 = jnp.zeros_like(v1)
    for e in range(N_EXPERTS):
        w1 = jnp.where(i1 == e, scores[e:e + 1], w1)
        w2 = jnp.where(i2 == e, scores[e:e + 1], w2)
    inv = 1.0 / (w1 + w2)
    pad = jnp.zeros_like(w1)
    return jnp.concatenate([i1.astype(F32), i2.astype(F32), w1 * inv, w2 * inv, pad, pad, pad, pad], axis=0)


def _residual_and_route(x, out, m, ffn_gain, rw_ref, rb_ref, x_ref, h_ref, g_ref):
    x1 = x + m[2:3] * out
    x_ref[...] = x1
    h = _modulated_norm(x1, ffn_gain, m[3:4], m[4:5])
    h_ref[...] = h
    logits_t = lax.dot_general(rw_ref[...], h, (((1,), (1,)), ((), ())), precision=lax.Precision.HIGHEST,
                               preferred_element_type=F32)
    g_ref[...] = _route(logits_t, rb_ref[...])


def _merge_even_kernel(o0_ref, o1_ref, gg_ref, oa_ref, x_ref, mod_ref, gn_ref, bd_ref, w1_ref, w2_ref,
                       fg_ref, rw_ref, rb_ref, xo_ref, ho_ref, go_ref):
    og = o0_ref[...] + o1_ref[...]
    ms = _dot(og * og, bd_ref[...]) * (1.0 / GLA_DV)
    g = og * lax.rsqrt(ms + NORM_EPS) * gn_ref[...] * _silu(gg_ref[...])
    out = _dot(g.astype(BF16), w1_ref[...]) + _dot(oa_ref[...].astype(BF16), w2_ref[...])
    _residual_and_route(x_ref[...], out, mod_ref[0], fg_ref[...], rw_ref, rb_ref, xo_ref, ho_ref, go_ref)


def _merge_even(o_f, o_b, zg, o_att, x, mods, out_norm, w_out, ffn_gain, router_wt, router_b, tm, mod_row):
    t, d = x.shape
    hv = GLA_HEADS * GLA_DV
    qw = ATT_HEADS * HEAD_DIM
    full = lambda a: pl.BlockSpec(a.shape, lambda i: (0,) * a.ndim)
    gn = jnp.tile(out_norm, GLA_HEADS).reshape(1, hv)
    bd = _block_ones(GLA_HEADS, GLA_DV)
    w1, w2 = w_out[:hv].astype(BF16), w_out[hv:].astype(BF16)
    fg = ffn_gain.reshape(1, d)
    rb = router_b.reshape(N_EXPERTS, 1)
    return pl.pallas_call(
        _merge_even_kernel,
        grid=(t // tm,),
        in_specs=[pl.BlockSpec((tm, hv), lambda i: (i, 0)),
                  pl.BlockSpec((tm, hv), lambda i: (i, 0)),
                  pl.BlockSpec((tm, hv), lambda i: (i, 2)),
                  pl.BlockSpec((tm, qw), lambda i: (i, 0)),
                  pl.BlockSpec((tm, d), lambda i: (i, 0)),
                  pl.BlockSpec((1, 6, d), lambda i: (mod_row(i), 0, 0)),
                  full(gn), full(bd), full(w1), full(w2), full(fg), full(router_wt), full(rb)],
        out_specs=[pl.BlockSpec((tm, d), lambda i: (i, 0)),
                   pl.BlockSpec((tm, d), lambda i: (i, 0)),
                   pl.BlockSpec((8, tm), lambda i: (0, i))],
        out_shape=[jax.ShapeDtypeStruct((t, d), F32),
                   jax.ShapeDtypeStruct((t, d), F32),
                   jax.ShapeDtypeStruct((8, t), F32)],
        compiler_params=_cparams(("parallel",)),
        name="merge_even",
    )(o_f, o_b, zg, o_att, x, mods, gn, bd, w1, w2, fg, router_wt, rb)


def _moe_plan(route, t, rows):
    n_tiles = 2 * t // rows + N_EXPERTS
    n_rows = n_tiles * rows
    experts = jnp.arange(N_EXPERTS, dtype=jnp.int32)[:, None]
    eid = jnp.concatenate([route[0], route[1]]).astype(jnp.int32)
    slot = jnp.arange(2 * t, dtype=jnp.int32)
    gate = jnp.concatenate([route[2], route[3]])
    _, s_slot, s_gate = lax.sort((eid, slot, gate), num_keys=1, is_stable=True)
    counts = jnp.sum((eid[None, :] == experts).astype(jnp.int32), axis=1)
    padded = (counts + rows - 1) // rows * rows
    p_end = jnp.cumsum(padded)
    p_start = p_end - padded
    c_start = jnp.cumsum(counts) - counts
    r_all = jnp.arange(n_rows, dtype=jnp.int32)
    e_of = jnp.minimum(jnp.sum((r_all[None, :] >= p_end[:, None]).astype(jnp.int32), axis=0), N_EXPERTS - 1)
    r_in = r_all - p_start[e_of]
    over = r_in - counts[e_of]
    valid = over < 0
    src = jnp.clip(c_start[e_of] + r_in, 0, 2 * t - 1)
    g_slot = s_slot[src]
    row_tok = jnp.where(valid, jnp.where(g_slot >= t, g_slot - t, g_slot), 0)
    row_gate = jnp.where(valid, s_gate[src], 0.0)
    row_dst = jnp.where(valid, g_slot, 2 * t + e_of * rows + jnp.clip(over, 0, rows - 1))
    n_used = (p_end[-1] // rows).astype(jnp.int32)
    tile_e = e_of[::rows]
    return (row_tok.reshape(n_tiles, 1, rows), row_gate.reshape(n_rows, 1), row_dst.reshape(n_tiles, 1, rows),
            tile_e.astype(jnp.int32), n_used.reshape(1))


def _moe_experts_kernel(te_ref, nu_ref, tok_ref, tokn_ref, dst_ref, gate_ref, wg_ref, wu_ref, wd_ref, h_hbm,
                        y_hbm, hbuf, ybuf, wgb, wub, wdb, sem_g, sem_s):
    j = pl.program_id(0)
    n_used = nu_ref[0]
    slot = j % 2
    rows = hbuf.shape[1]

    def start_gather(idx_ref, s):
        for r in range(rows):
            pltpu.make_async_copy(h_hbm.at[pl.ds(idx_ref[0, 0, r], 1)], hbuf.at[s, pl.ds(r, 1)], sem_g.at[s]).start()

    def wait_gather(s):
        pltpu.make_async_copy(h_hbm.at[pl.ds(0, rows)], hbuf.at[s], sem_g.at[s]).wait()

    def wait_scatter(s):
        pltpu.make_async_copy(ybuf.at[s], y_hbm.at[pl.ds(0, rows)], sem_s.at[s]).wait()

    @pl.when(j == 0)
    def _():
        start_gather(tok_ref, 0)
        ybuf[...] = jnp.zeros(ybuf.shape, F32)
        n_real = y_hbm.shape[0] - N_EXPERTS * rows
        fills = [pltpu.make_async_copy(ybuf.at[k % 2], y_hbm.at[pl.ds(n_real + k * rows, rows)], sem_s.at[k % 2])
                 for k in range(N_EXPERTS)]
        for f in fills:
            f.start()
        for f in fills[2:]:
            f.wait()

    active = j < n_used
    changed = jnp.logical_or(j == 0, te_ref[j] != te_ref[jnp.maximum(j - 1, 0)])

    @pl.when(jnp.logical_and(active, changed))
    def _():
        wgb[...] = wg_ref[0].astype(BF16)
        wub[...] = wu_ref[0].astype(BF16)
        wdb[...] = wd_ref[0].astype(BF16)

    @pl.when(active)
    def _():
        wait_gather(slot)
        start_gather(tokn_ref, 1 - slot)
        h = hbuf[slot].astype(BF16)
        act = _silu(_dot(h, wgb[...])) * _dot(h, wub[...])
        y = _dot((act * gate_ref[...]).astype(BF16), wdb[...])
        wait_scatter(slot)
        ybuf[slot] = y
        for r in range(rows):
            pltpu.make_async_copy(ybuf.at[slot, pl.ds(r, 1)], y_hbm.at[pl.ds(dst_ref[0, 0, r], 1)], sem_s.at[slot]).start()

        @pl.when(j == n_used - 1)
        def _():
            wait_scatter(slot)
            wait_scatter(1 - slot)
            wait_gather(1 - slot)


def _moe_combine_kernel(ya_ref, yb_ref, x_ref, mod_ref, o_ref):
    o_ref[...] = x_ref[...] + mod_ref[0][5:6] * (ya_ref[...] + yb_ref[...])


def _moe(h, route, x, mods, w_gate, w_up, w_down, layer, tm, mod_row):
    t, d = x.shape
    rows = MOE_TILE
    row_tok, row_gate, row_dst, tile_e, n_used = _moe_plan(route, t, rows)
    n_tiles = row_tok.shape[0]
    n_pair_rows = 2 * t + N_EXPERTS * rows
    smem_tile = lambda nxt: pl.BlockSpec((1, 1, rows), lambda j, te, nu: (jnp.minimum(j + nxt, nu[0] - 1), 0, 0),
                                         memory_space=pltpu.SMEM)
    pairs = pl.pallas_call(
        _moe_experts_kernel,
        grid_spec=pltpu.PrefetchScalarGridSpec(
            num_scalar_prefetch=2,
            grid=(n_tiles,),
            in_specs=[smem_tile(0), smem_tile(1), smem_tile(0),
                      pl.BlockSpec((rows, 1), lambda j, te, nu: (j, 0)),
                      pl.BlockSpec((None, 1, d, D_EXPERT), lambda j, te, nu: (layer, te[j], 0, 0)),
                      pl.BlockSpec((None, 1, d, D_EXPERT), lambda j, te, nu: (layer, te[j], 0, 0)),
                      pl.BlockSpec((None, 1, D_EXPERT, d), lambda j, te, nu: (layer, te[j], 0, 0)),
                      pl.BlockSpec(memory_space=pl.ANY)],
            out_specs=pl.BlockSpec(memory_space=pl.ANY),
            scratch_shapes=[pltpu.VMEM((2, rows, d), F32), pltpu.VMEM((2, rows, d), F32),
                            pltpu.VMEM((d, D_EXPERT), BF16), pltpu.VMEM((d, D_EXPERT), BF16),
                            pltpu.VMEM((D_EXPERT, d), BF16),
                            pltpu.SemaphoreType.DMA((2,)), pltpu.SemaphoreType.DMA((2,))]),
        out_shape=jax.ShapeDtypeStruct((n_pair_rows, d), F32),
        compiler_params=_cparams(("arbitrary",)),
        name="moe_experts",
    )(tile_e, n_used, row_tok, row_tok, row_dst, row_gate, w_gate, w_up, w_down, h)
    return pl.pallas_call(
        _moe_combine_kernel,
        grid=(t // tm,),
        in_specs=[pl.BlockSpec((tm, d), lambda i: (i, 0)),
                  pl.BlockSpec((tm, d), lambda i: (t // tm + i, 0)),
                  pl.BlockSpec((tm, d), lambda i: (i, 0)),
                  pl.BlockSpec((1, 6, d), lambda i: (mod_row(i), 0, 0))],
        out_specs=pl.BlockSpec((tm, d), lambda i: (i, 0)),
        out_shape=jax.ShapeDtypeStruct((t, d), F32),
        compiler_params=_cparams(("parallel",)),
        name="moe_combine",
    )(pairs, pairs, x, mods)


def _chan_dft_kernel(z_ref, w_ref, o_ref):
    res = _dot(z_ref[...].astype(BF16), w_ref[...]).astype(BF16)
    o_ref[0] = res[:, :FOURIER_WIDTH]
    o_ref[1] = res[:, FOURIER_WIDTH:]


def _chan_dft(zf, w, batch, l_ctx, l_lat, tm):
    l_tot = l_ctx + l_lat
    nt = l_lat // tm
    fw = FOURIER_WIDTH
    return pl.pallas_call(
        _chan_dft_kernel,
        grid=(batch, nt),
        in_specs=[pl.BlockSpec((tm, fw), lambda b, i: (b * (l_tot // tm) + l_ctx // tm + i, 0)),
                  pl.BlockSpec(w.shape, lambda b, i: (0, 0))],
        out_specs=pl.BlockSpec((2, tm, fw), lambda b, i: (0, i, b)),
        out_shape=jax.ShapeDtypeStruct((2, l_lat, batch * fw), BF16),
        compiler_params=_cparams(("parallel", "parallel")),
        name="fourier_channels",
    )(zf, w)


def _seq_dft_kernel(c_ref, s_ref, z_ref, o_ref, acc_ref):
    k = pl.program_id(1)

    @pl.when(k == 0)
    def _():
        acc_ref[...] = jnp.zeros_like(acc_ref)

    acc_ref[...] += _dot(c_ref[...], z_ref[0]) + _dot(s_ref[...], z_ref[1])

    @pl.when(k == pl.num_programs(1) - 1)
    def _():
        o_ref[...] = acc_ref[...]


def _seq_dft(cos_t, sin_t, zc, batch, tm, tk):
    l = cos_t.shape[0]
    fw = FOURIER_WIDTH
    return pl.pallas_call(
        _seq_dft_kernel,
        grid=(l // tm, l // tk),
        in_specs=[pl.BlockSpec((tm, tk), lambda i, k: (i, k)),
                  pl.BlockSpec((tm, tk), lambda i, k: (i, k)),
                  pl.BlockSpec((2, tk, batch * fw), lambda i, k: (0, k, 0))],
        out_specs=pl.BlockSpec((tm, batch * fw), lambda i, k: (i, 0)),
        out_shape=jax.ShapeDtypeStruct((l, batch * fw), F32),
        scratch_shapes=[pltpu.VMEM((tm, batch * fw), F32)],
        compiler_params=_cparams(("parallel", "arbitrary")),
        name="fourier_sequence",
    )(cos_t, sin_t, zc)


def _dft_tables(l):
    m = jnp.arange(l, dtype=jnp.int32)[:, None]
    n1 = l // 64
    a = (m * (jnp.arange(n1, dtype=jnp.int32)[None, :] * 64)) % l
    b = (m * jnp.arange(64, dtype=jnp.int32)[None, :]) % l
    wa = a.astype(F32) * (2.0 * np.pi / l)
    wb = b.astype(F32) * (2.0 * np.pi / l)
    ca, sa, cb, sb = jnp.cos(wa), jnp.sin(wa), jnp.cos(wb), jnp.sin(wb)
    cos_t = (ca[:, :, None] * cb[:, None, :] - sa[:, :, None] * sb[:, None, :]).reshape(l, l).astype(BF16)
    sin_t = (sa[:, :, None] * cb[:, None, :] + ca[:, :, None] * sb[:, None, :]).reshape(l, l).astype(BF16)
    gd = FOURIER_GROUP_DIM
    cc = (jnp.arange(gd, dtype=jnp.int32)[:, None] * jnp.arange(gd, dtype=jnp.int32)[None, :]) % gd
    wc = cc.astype(F32) * (2.0 * np.pi / gd)
    scale = 1.0 / np.sqrt(float(l) * gd)
    eye = jnp.eye(FOURIER_WIDTH // gd, dtype=F32)
    chan = jnp.concatenate([jnp.kron(eye, jnp.cos(wc)), -jnp.kron(eye, jnp.sin(wc))], axis=1) * scale
    return cos_t, sin_t, chan.astype(BF16)


def _rwkv_prepare(d, pos, z_ref, zp_ref, zn_ref, par, nc_ctx, nc_tot):
    mu, kks, ka, rk, w0, w2, a0, a2, bd = par
    c = z_ref.shape[0]
    n = RWKV_DIM
    seg_first = jnp.logical_or(pos == 0, pos == nc_ctx)
    seg_last = jnp.logical_or(pos == nc_ctx - 1, pos == nc_tot - 1)
    z = z_ref[...]
    row = lax.broadcasted_iota(jnp.int32, z.shape, 0)
    prev_row = jnp.where(seg_first, 0.0, zp_ref[7:8, :])
    next_row = jnp.where(seg_last, 0.0, zn_ref[0:1, :])
    z_prev = jnp.where(row == 0, prev_row, pltpu.roll(z, 1, 0))
    z_next = jnp.where(row == c - 1, next_row, pltpu.roll(z, c - 1, 0))
    zs = z + mu * (0.5 * (z_prev + z_next) - z)

    r, k, v = zs[:, 0:n], zs[:, n:2 * n], zs[:, 2 * n:3 * n]
    zw = zs[:, 3 * n:3 * n + RWKV_RANK_PAD]
    za = zs[:, 3 * n + RWKV_RANK_PAD:3 * n + 2 * RWKV_RANK_PAD]
    zg = zs[:, 3 * n + 2 * RWKV_RANK_PAD:3 * n + 3 * RWKV_RANK_PAD]

    kk = k * kks
    kk = kk * lax.rsqrt(_dot(kk * kk, bd) + L2_EPS)
    w_log = _log_sigmoid(w0[d] + _dot(jnp.tanh(zw), w2[d])) - 0.5
    lw = -jnp.exp(w_log)
    a = jax.nn.sigmoid(a0[d] + _dot(za, a2[d]))
    kd = k * (1.0 + (a - 1.0) * ka)
    beta = kk * a

    cl = _dot((_visit_order(d, c) >= 0).astype(F32), lw)
    c_tot = jnp.sum(lw, axis=0, keepdims=True)
    grow = jnp.exp(-cl)
    tail = jnp.exp(c_tot - cl)
    ops = dict(k_s=kd * grow,
               b_s=beta * grow,
               kap_s=kk * jnp.exp(cl - lw),
               r_s=r * jnp.exp(cl),
               k_e=kd * tail,
               b_e=beta * tail,
               gam=jnp.exp(c_tot), v=v,
               bonus=_dot(r * kd * rk, bd),
               sign=1 if d == 0 else -1)
    return ops, zg


def _rwkv_kernel(zf_ref, zfp_ref, zfn_ref, zb_ref, zbp_ref, zbn_ref, mu_ref, kks_ref, ka_ref, rk_ref, w0_ref, w2_ref,
                 a0_ref, a2_ref, g2_ref, bd_ref, yf_ref, yb_ref, gate_ref, st_ref, *, nc_ctx, nc_tot):
    s = pl.program_id(1)
    c = zf_ref.shape[0]
    hd = HEAD_DIM
    pw = 2 * hd

    @pl.when(s == 0)
    def _():
        st_ref[...] = jnp.zeros_like(st_ref)

    par = (mu_ref[...], kks_ref[...], ka_ref[...], rk_ref[...], w0_ref, w2_ref, a0_ref, a2_ref, bd_ref[...])
    fwd, zg = _rwkv_prepare(0, _chunk_pos(s, 0, nc_ctx, nc_tot), zf_ref, zfp_ref, zfn_ref, par, nc_ctx, nc_tot)
    bwd, _ = _rwkv_prepare(1, _chunk_pos(s, 1, nc_ctx, nc_tot), zb_ref, zbp_ref, zbn_ref, par, nc_ctx, nc_tot)
    gate_ref[...] = _dot(jax.nn.sigmoid(zg), g2_ref[...])

    lane = lax.broadcasted_iota(jnp.int32, (c, pw), 1)
    rowi = lax.broadcasted_iota(jnp.int32, (c, pw), 0)
    left = lane < hd
    eye_p = (rowi == lane % hd).astype(F32)
    same_head = (lax.broadcasted_iota(jnp.int32, (pw, pw), 0) < hd) == (lax.broadcasted_iota(jnp.int32, (pw, pw), 1) < hd)

    def bd(y):
        return jnp.concatenate([jnp.where(left, y, 0.0), jnp.where(left, 0.0, y)], axis=0)

    n_pairs = RWKV_HEADS // 2
    prob = [(o, slice(i * pw, (i + 1) * pw)) for o in (fwd, bwd) for i in range(n_pairs)]
    ahead = [(rowi - lane % hd) * o['sign'] for o, _ in prob]
    get = lambda name: [o[name][:, s_] for o, s_ in prob]
    kap, r_s, k_s, b_s, k_e, b_e, vp, gam, bonus = (get(x) for x in ('kap_s', 'r_s', 'k_s', 'b_s', 'k_e', 'b_e', 'v', 'gam', 'bonus'))
    p = [_dot_nt(jnp.concatenate([a_, b_], axis=0), jnp.concatenate([bd(c_), bd(d_)], axis=0))
         for a_, b_, c_, d_ in zip(kap, r_s, k_s, b_s)]
    m1 = [jnp.where(h_ > 0, x[0:c, 0:pw], 0.0) for x, h_ in zip(p, ahead)]
    m2 = [jnp.where(h_ > 0, x[0:c, pw:2 * pw], 0.0) for x, h_ in zip(p, ahead)]
    n1 = [jnp.where(h_ >= 0, x[c:2 * c, 0:pw], 0.0) for x, h_ in zip(p, ahead)]
    n2 = [jnp.where(h_ >= 0, x[c:2 * c, pw:2 * pw], 0.0) for x, h_ in zip(p, ahead)]
    m1v = [_dot(a_, bd(b_)) for a_, b_ in zip(m1, vp)]
    t_inv = [eye_p - x for x in m2]
    q = [_dot(x, bd(x)) for x in m2]
    span = 2
    while 2 * span < c:
        both = [_dot(jnp.concatenate([t_, q_], axis=0), bd(q_)) for t_, q_ in zip(t_inv, q)]
        t_inv = [t_ + x[0:c] for t_, x in zip(t_inv, both)]
        q = [x[c:2 * c] for x in both]
        span *= 2
    t_inv = [t_ + _dot(t_, bd(q_)) for t_, q_ in zip(t_inv, q)]
    tx = [_dot(t_, jnp.concatenate([bd(a_), bd(mv)], axis=1)) for t_, a_, mv in zip(t_inv, kap, m1v)]
    st = [st_ref[i] for i in range(len(prob))]
    su = [_dot_nt(jnp.concatenate([x[:, 0:pw], r_], axis=0), s0) for x, r_, s0 in zip(tx, r_s, st)]
    u = [x[0:c] + y_[:, pw:2 * pw] for x, y_ in zip(su, tx)]
    ys = [x[c:2 * c] + _dot(jnp.concatenate([a_, -b_], axis=1), jnp.concatenate([bd(v_), bd(u_)], axis=0)) + bo * v_
          for x, a_, b_, v_, u_, bo in zip(su, n1, n2, vp, u, bonus)]
    for i in range(len(prob)):
        upd = _dot_tn(jnp.concatenate([vp[i], u[i]], axis=0), jnp.concatenate([k_e[i], -b_e[i]], axis=0))
        st_ref[i] = st[i] * gam[i] + jnp.where(same_head, upd, 0.0)
    yf_ref[...] = jnp.concatenate(ys[:n_pairs], axis=1)
    yb_ref[...] = jnp.concatenate(ys[n_pairs:], axis=1)


def _rwkv(zr, mu, kk_scale, k_a, r_k, w0, w2_pad, a0, a2_pad, g2, batch, nc_ctx, nc_tot):
    t, zw_ = zr.shape
    c = SEQ_CHUNK
    n = RWKV_DIM
    bd = _block_ones(RWKV_HEADS, HEAD_DIM)
    full = lambda a: pl.BlockSpec(a.shape, lambda b, s: (0,) * a.ndim)
    sub = c // 8
    n_sub = t // 8

    def rows(d):
        return lambda b, s: b * nc_tot + _chunk_pos(s, d, nc_ctx, nc_tot)

    def z_specs(d):
        rw = rows(d)
        return [pl.BlockSpec((c, zw_), lambda b, s: (rw(b, s), 0)),
                pl.BlockSpec((8, zw_), lambda b, s: (jnp.maximum(rw(b, s) * sub - 1, 0), 0)),
                pl.BlockSpec((8, zw_), lambda b, s: (jnp.minimum((rw(b, s) + 1) * sub, n_sub - 1), 0))]

    vec = lambda a: a.reshape(1, -1)
    args = (vec(mu), vec(kk_scale), vec(k_a), vec(r_k), w0.reshape(2, 1, n), w2_pad, a0.reshape(2, 1, n), a2_pad, g2, bd)
    out = lambda d: pl.BlockSpec((c, n), lambda b, s: (rows(d)(b, s), 0))
    return pl.pallas_call(
        functools.partial(_rwkv_kernel, nc_ctx=nc_ctx, nc_tot=nc_tot),
        grid=(batch, nc_tot),
        in_specs=z_specs(0) + z_specs(1) + [full(a) for a in args],
        out_specs=[out(0), out(1), out(0)],
        out_shape=[jax.ShapeDtypeStruct((t, n), F32)] * 3,
        scratch_shapes=[pltpu.VMEM((RWKV_HEADS, 2 * HEAD_DIM, 2 * HEAD_DIM), F32)],
        compiler_params=_cparams(("parallel", "arbitrary")),
        name="rwkv_scan",
    )(zr, zr, zr, zr, zr, zr, *args)


def _merge_odd_kernel(y0_ref, y1_ref, gate_ref, fo_ref, x_ref, mod_ref, lg_ref, lb_ref, bd_ref, w1_ref, w2_ref,
                      fg_ref, rw_ref, rb_ref, xo_ref, ho_ref, go_ref):
    y = y0_ref[...] + y1_ref[...]
    bd = bd_ref[...]
    mean = _dot(y, bd) * (1.0 / HEAD_DIM)
    yc = y - mean
    var = _dot(yc * yc, bd) * (1.0 / HEAD_DIM)
    rw = (yc * lax.rsqrt(var + RWKV_GN_EPS) * lg_ref[...] + lb_ref[...]) * gate_ref[...]
    out = _dot(fo_ref[...].astype(BF16), w1_ref[...]) + _dot(rw.astype(BF16), w2_ref[...])
    _residual_and_route(x_ref[...], out, mod_ref[0], fg_ref[...], rw_ref, rb_ref, xo_ref, ho_ref, go_ref)


def _merge_odd(y0, y1, gate, fo, x, mods, ln_g, ln_b, w_out, ffn_gain, router_wt, router_b, tm, batch, l_ctx, l_lat):
    d = x.shape[1]
    n = RWKV_DIM
    fw = FOURIER_WIDTH
    l_tot = l_ctx + l_lat
    nt = l_lat // tm
    t_out = batch * l_lat
    full = lambda a: pl.BlockSpec(a.shape, lambda b, i: (0,) * a.ndim)
    src = lambda b, i: b * (l_tot // tm) + l_ctx // tm + i
    bd = _block_ones(RWKV_HEADS, HEAD_DIM)
    w1, w2 = w_out[:fw].astype(BF16), w_out[fw:].astype(BF16)
    lg, lb, fg, rb = ln_g.reshape(1, n), ln_b.reshape(1, n), ffn_gain.reshape(1, d), router_b.reshape(N_EXPERTS, 1)
    return pl.pallas_call(
        _merge_odd_kernel,
        grid=(batch, nt),
        in_specs=[pl.BlockSpec((tm, n), lambda b, i: (src(b, i), 0)),
                  pl.BlockSpec((tm, n), lambda b, i: (src(b, i), 0)),
                  pl.BlockSpec((tm, n), lambda b, i: (src(b, i), 0)),
                  pl.BlockSpec((tm, fw), lambda b, i: (i, b)),
                  pl.BlockSpec((tm, d), lambda b, i: (src(b, i), 0)),
                  pl.BlockSpec((1, 6, d), lambda b, i: (b, 0, 0)),
                  full(lg), full(lb), full(bd), full(w1), full(w2), full(fg), full(router_wt), full(rb)],
        out_specs=[pl.BlockSpec((tm, d), lambda b, i: (b * nt + i, 0)),
                   pl.BlockSpec((tm, d), lambda b, i: (b * nt + i, 0)),
                   pl.BlockSpec((8, tm), lambda b, i: (0, b * nt + i))],
        out_shape=[jax.ShapeDtypeStruct((t_out, d), F32),
                   jax.ShapeDtypeStruct((t_out, d), F32),
                   jax.ShapeDtypeStruct((8, t_out), F32)],
        compiler_params=_cparams(("parallel", "parallel")),
        name="merge_odd",
    )(y0, y1, gate, fo, x, mods, lg, lb, bd, w1, w2, fg, router_wt, rb)


def _rope_tables(l_ctx, l_lat):
    rows = l_lat // GRID_W
    row = jnp.repeat(jnp.arange(rows, dtype=F32), GRID_W)
    col = jnp.tile(jnp.arange(GRID_W, dtype=F32), rows)
    n_freq = HEAD_DIM // 4
    inv_freq = ROPE_THETA ** (-jnp.arange(n_freq, dtype=F32) / n_freq)
    ang = jnp.concatenate([row[:, None] * inv_freq, col[:, None] * inv_freq], axis=-1)
    cos, sin = jnp.cos(ang), jnp.sin(ang)
    cos64 = jnp.concatenate([cos, cos], axis=1)
    sin64 = jnp.concatenate([-sin, sin], axis=1)
    cos64 = jnp.concatenate([jnp.ones((l_ctx, HEAD_DIM), F32), cos64], axis=0)
    sin64 = jnp.concatenate([jnp.zeros((l_ctx, HEAD_DIM), F32), sin64], axis=0)
    return cos64, sin64


def _pad_rank(w):
    _, r, n = w.shape
    out = jnp.zeros((2, RWKV_RANK_PAD, n), w.dtype)
    out = out.at[0, 0:r].set(w[0])
    return out.at[1, r:2 * r].set(w[1])


def _even_layer(x, mods, p, batch, l_ctx, l_lat, tm, mod_row, tm_moe):
    l_tot = l_ctx + l_lat
    d = x.shape[1]
    nc_ctx, nc_tot = l_ctx // SEQ_CHUNK, l_tot // SEQ_CHUNK
    hk, hv = GLA_HEADS * GLA_DK, GLA_HEADS * GLA_DV
    qw, kw = ATT_HEADS * HEAD_DIM, ATT_KV_HEADS * HEAD_DIM
    w_in = p['w_in']
    o = np.cumsum([0, hk, hk, hv, hv, 2 * GLA_LOWRANK, qw, kw, kw])
    w_gla = jnp.concatenate([w_in[:, o[0]:o[4]]], axis=1).astype(BF16)
    w_dec = jnp.pad(w_in[:, o[4]:o[5]], ((0, 0), (0, 128 - 2 * GLA_LOWRANK))).astype(BF16)
    w_q = w_in[:, o[5]:o[6]].astype(BF16)
    w_kv = w_in[:, o[6]:o[8]].astype(BF16)
    zg, zdec, zq, zkv = _project(x, mods, p['norm_mix'], [w_gla, w_dec, w_q, w_kv], tm, mod_row)

    dec_w_pad = _pad_rank(p['dec_w'])
    o_f, o_b = _gla(zg, zdec, dec_w_pad, p['dec_b'].reshape(2, 1, hk), batch, nc_ctx, nc_tot)

    cos64, sin64 = _rope_tables(l_ctx, l_lat)
    qn, kn, vn = _qk_prep(zq, zkv, cos64, sin64, p['q_norm'], p['k_norm'], tm, l_tot)
    o_att = _attention(qn, kn, vn, p['sink'], batch, l_ctx, l_tot)

    x1, h, gates = _merge_even(o_f, o_b, zg, o_att, x, mods, p['out_norm'], p['w_out'], p['norm_ffn'],
                               p['router_wt'], p['router_b'], tm, mod_row)
    return _moe(h, gates, x1, mods, p['moe_g'], p['moe_u'], p['moe_d'], p['moe_layer'], tm, mod_row)


def _odd_layer(x, mods, p, batch, l_ctx, l_lat, tm, mod_row, tm_moe):
    l_tot = l_ctx + l_lat
    nc_ctx, nc_tot = l_ctx // SEQ_CHUNK, l_tot // SEQ_CHUNK
    n = RWKV_DIM
    fw = FOURIER_WIDTH
    w_in = p['w_in']
    rank_w, rank_a = p['w2'].shape[1], p['a2'].shape[1]
    o = np.cumsum([0, fw, n, n, n, 2 * rank_w, 2 * rank_a])
    pad_cols = lambda w: jnp.pad(w, ((0, 0), (0, RWKV_RANK_PAD - w.shape[1])))
    w_f = w_in[:, o[0]:o[1]].astype(BF16)
    w_r = jnp.concatenate([w_in[:, o[1]:o[4]], pad_cols(w_in[:, o[4]:o[5]]), pad_cols(w_in[:, o[5]:o[6]]),
                           w_in[:, o[6]:]], axis=1).astype(BF16)
    zf, zr = _project(x, mods, p['norm_mix'], [w_f, w_r], tm, mod_row)

    mu = p['mu']
    mu_r = jnp.concatenate([mu[0:3 * n], pad_cols(mu[None, 3 * n:3 * n + 2 * rank_w])[0],
                            pad_cols(mu[None, 3 * n + 2 * rank_w:3 * n + 2 * rank_w + 2 * rank_a])[0],
                            mu[3 * n + 2 * rank_w + 2 * rank_a:]])
    y0, y1, gate = _rwkv(zr, mu_r, p['kk_scale'], p['k_a'], p['r_k'].reshape(-1), p['w0'], _pad_rank(p['w2']),
                    p['a0'], _pad_rank(p['a2']), p['g2'], batch, nc_ctx, nc_tot)

    cos_t, sin_t, chan = _dft_tables(l_lat)
    zc = _chan_dft(zf, chan, batch, l_ctx, l_lat, tm)
    fo = _seq_dft(cos_t, sin_t, zc, batch, min(512, l_lat), min(1024, l_lat))

    x1, h, gates = _merge_odd(y0, y1, gate, fo, x, mods, p['ln_g'], p['ln_b'], p['w_out'], p['norm_ffn'],
                              p['router_wt'], p['router_b'], tm, batch, l_ctx, l_lat)
    lat_tiles = l_lat // tm_moe
    return _moe(h, gates, x1, mods, p['moe_g'], p['moe_u'], p['moe_d'], p['moe_layer'], tm_moe,
                lambda i: i // lat_tiles)


def kernel(x, c, ctx, c_ctx, ada_w, ada_b, norm_mix, norm_ffn, even_w_in, even_w_out, gla_dec_w, gla_dec_b, gla_out_norm, att_q_norm, att_k_norm, att_sink, odd_w_in, odd_w_out, rwkv_mu, rwkv_w0, rwkv_w2, rwkv_a0, rwkv_a2, rwkv_g2, rwkv_kk_scale, rwkv_k_a, rwkv_r_k, rwkv_ln_g, rwkv_ln_b, router_w, router_b, moe_w_gate, moe_w_up, moe_w_down):
    batch, l_lat, d = x.shape
    l_ctx = ctx.shape[1]
    l_tot = l_ctx + l_lat
    assert batch < 8 and ada_w.shape[0] == 2
    tm = 256 if (l_ctx % 256 == 0 and l_lat % 256 == 0) else 128
    tm_moe = 512 if (l_lat % 512 == 0 and tm == 256) else tm
    assert l_ctx % tm == 0 and l_lat % tm == 0 and l_tot % l_ctx == 0 and l_lat % GRID_W == 0

    xs = jnp.concatenate([ctx, x], axis=1).reshape(batch * l_tot, d)
    cc = jnp.concatenate([c, c_ctx[None, :], jnp.zeros((8 - batch - 1, d), F32)], axis=0)
    tiles_per_b = l_tot // tm
    ctx_tiles = l_ctx // tm

    def mod_row(i):
        return jnp.where(i % tiles_per_b < ctx_tiles, batch, i // tiles_per_b)

    router_wt = router_w.T
    moe = lambda layer: dict(moe_g=moe_w_gate, moe_u=moe_w_up, moe_d=moe_w_down, moe_layer=layer)

    mods_all = _modvec(cc, ada_w, ada_b)
    mods0 = mods_all[0]
    p0 = dict(w_in=even_w_in[0], w_out=even_w_out[0], dec_w=gla_dec_w[0], dec_b=gla_dec_b[0],
              out_norm=gla_out_norm[0], q_norm=att_q_norm[0], k_norm=att_k_norm[0], sink=att_sink[0],
              norm_mix=norm_mix[0], norm_ffn=norm_ffn[0], router_wt=router_wt, router_b=router_b, **moe(0))
    xs = _even_layer(xs, mods0, p0, batch, l_ctx, l_lat, tm, mod_row, tm_moe)

    mods1 = mods_all[1]
    p1 = dict(w_in=odd_w_in[0], w_out=odd_w_out[0], mu=rwkv_mu[0], w0=rwkv_w0[0], w2=rwkv_w2[0], a0=rwkv_a0[0],
              a2=rwkv_a2[0], g2=rwkv_g2[0], kk_scale=rwkv_kk_scale[0], k_a=rwkv_k_a[0], r_k=rwkv_r_k[0],
              ln_g=rwkv_ln_g[0], ln_b=rwkv_ln_b[0], norm_mix=norm_mix[1], norm_ffn=norm_ffn[1],
              router_wt=router_wt, router_b=router_b, **moe(1))
    out = _odd_layer(xs, mods1, p1, batch, l_ctx, l_lat, tm, mod_row, tm_moe)
    return out.reshape(batch, l_lat, d)
```

```python
import functools

import jax
import jax.numpy as jnp
import numpy as np
from jax import lax
from jax.experimental import pallas as pl
from jax.experimental.pallas import tpu as pltpu

F32 = jnp.float32
BF16 = jnp.bfloat16

GRID_W = 64
HEAD_DIM = 64
NORM_EPS = 1e-6
L2_EPS = 1e-12

GLA_DV = 64
GLA_DK = 32
GLA_HEADS = 8
GLA_LOWRANK = 16
GLA_TAU = 16.0

ATT_HEADS = 8
ATT_KV_HEADS = 2
ATT_GROUP = ATT_HEADS // ATT_KV_HEADS
ATT_BLOCK = 128
ROPE_THETA = 10000.0

FOURIER_GROUP_DIM = 64
FOURIER_WIDTH = 256
DFT_SPLIT = 64

RWKV_DIM = 768
RWKV_HEADS = 12
RWKV_RANK_PAD = 128
RWKV_GN_EPS = 64e-5

N_EXPERTS = 16
N_GROUPS = 4
PER_GROUP = N_EXPERTS // N_GROUPS
D_EXPERT = 512
MOE_TILE = 256

SEQ_CHUNK = 64
VMEM_LIMIT = 56 * 1024 * 1024


def _cparams(sem):
    return pltpu.CompilerParams(dimension_semantics=sem, vmem_limit_bytes=VMEM_LIMIT)


def _dot(a, b):
    return jnp.dot(a, b, preferred_element_type=F32)


def _dot_nt(a, b):
    return lax.dot_general(a, b, (((1,), (1,)), ((), ())), preferred_element_type=F32)


def _dot_tn(a, b):
    return lax.dot_general(a, b, (((0,), (0,)), ((), ())), preferred_element_type=F32)


def _silu(x):
    return x * jax.nn.sigmoid(x)


def _log_sigmoid(x):
    return jnp.minimum(x, 0.0) - jnp.log(1.0 + jnp.exp(-jnp.abs(x)))


def _modulated_norm(x, gain, shift, scale):
    ms = jnp.mean(x * x, axis=-1, keepdims=True)
    return (x * lax.rsqrt(ms + NORM_EPS) * gain) * (1.0 + scale) + shift


def _block_ones(n_blocks, width):
    return jnp.kron(jnp.eye(n_blocks, dtype=F32), jnp.ones((width, width), F32))


def _modvec_kernel(c_ref, w_ref, b_ref, o_ref):
    o_ref[...] = _dot(_silu(c_ref[...]), w_ref[...]) + b_ref[...]


def _modvec(cc, w, b):
    d = cc.shape[1]
    layers, _, n = w.shape
    tn = n // 4
    out = pl.pallas_call(
        _modvec_kernel,
        grid=(layers, n // tn),
        in_specs=[pl.BlockSpec((8, d), lambda l, j: (0, 0)),
                  pl.BlockSpec((None, d, tn), lambda l, j: (l, 0, j)),
                  pl.BlockSpec((None, 1, tn), lambda l, j: (l, 0, j))],
        out_specs=pl.BlockSpec((None, 8, tn), lambda l, j: (l, 0, j)),
        out_shape=jax.ShapeDtypeStruct((layers, 8, n), F32),
        compiler_params=_cparams(("parallel", "parallel")),
        name="modvec",
    )(cc, w, b.reshape(layers, 1, n))
    return out.reshape(layers, 8, 6, d)


def _proj_kernel(x_ref, mod_ref, gain_ref, *refs, n_out):
    w_refs, z_refs = refs[:n_out], refs[n_out:]
    m = mod_ref[0]
    h = _modulated_norm(x_ref[...], gain_ref[...], m[0:1], m[1:2]).astype(BF16)
    for w_ref, z_ref in zip(w_refs, z_refs):
        z_ref[...] = _dot(h, w_ref[...])


def _project(x, mods, gain, weights, tm, mod_row):
    t, d = x.shape
    n_out = len(weights)
    in_specs = [pl.BlockSpec((tm, d), lambda i: (i, 0)),
                pl.BlockSpec((1, 6, d), lambda i: (mod_row(i), 0, 0)),
                pl.BlockSpec((1, d), lambda i: (0, 0))]
    in_specs += [pl.BlockSpec(w.shape, lambda i: (0, 0)) for w in weights]
    return pl.pallas_call(
        functools.partial(_proj_kernel, n_out=n_out),
        grid=(t // tm,),
        in_specs=in_specs,
        out_specs=[pl.BlockSpec((tm, w.shape[1]), lambda i: (i, 0)) for w in weights],
        out_shape=[jax.ShapeDtypeStruct((t, w.shape[1]), F32) for w in weights],
        compiler_params=_cparams(("parallel",)),
        name="proj",
    )(x, mods, gain.reshape(1, d), *weights)


def _chunk_pos(s, d, nc_ctx, nc_tot):
    back = jnp.where(s < nc_ctx, nc_ctx - 1 - s, nc_tot + nc_ctx - 1 - s)
    return jnp.where(d == 0, s, back)


def _visit_order(d, c):
    sign = 1 if d == 0 else -1
    return (lax.broadcasted_iota(jnp.int32, (c, c), 0) - lax.broadcasted_iota(jnp.int32, (c, c), 1)) * sign


def _gla_prepare(d, q_ref, k_ref, v_ref, dec_ref, dw_ref, db_ref):
    c = q_ref.shape[0]
    g = _log_sigmoid(_dot(dec_ref[...], dw_ref[d]) + db_ref[d]) / GLA_TAU
    b = _dot((_visit_order(d, c) >= 0).astype(F32), g)
    b_tot = jnp.sum(g, axis=0, keepdims=True)
    k = k_ref[...]
    return dict(q_in=q_ref[...] * (GLA_DK ** -0.5) * jnp.exp(b), k_out=k * jnp.exp(-b), k_end=k * jnp.exp(b_tot - b),
                decay=jnp.exp(b_tot), v=v_ref[...], sign=1 if d == 0 else -1)


def _gla_kernel(qf_ref, kf_ref, vf_ref, df_ref, qb_ref, kb_ref, vb_ref, db_ref, dw_ref, dbias_ref, of_ref, ob_ref, st_ref):
    c = qf_ref.shape[0]
    group = 4
    kw, vw = group * GLA_DK, group * GLA_DV

    @pl.when(pl.program_id(1) == 0)
    def _():
        st_ref[...] = jnp.zeros_like(st_ref)

    fwd = _gla_prepare(0, qf_ref, kf_ref, vf_ref, df_ref, dw_ref, dbias_ref)
    bwd = _gla_prepare(1, qb_ref, kb_ref, vb_ref, db_ref, dw_ref, dbias_ref)
    n_quads = GLA_HEADS // group
    prob = [(o, i) for o in (fwd, bwd) for i in range(n_quads)]
    klane = lax.broadcasted_iota(jnp.int32, (c, kw), 1) // GLA_DK
    half = lax.broadcasted_iota(jnp.int32, (c, 2 * GLA_DV), 1) < GLA_DV
    rowi = lax.broadcasted_iota(jnp.int32, (c, group * c), 0)
    coli = lax.broadcasted_iota(jnp.int32, (c, group * c), 1) % c
    own = (lax.broadcasted_iota(jnp.int32, (vw, kw), 0) // GLA_DV) == (lax.broadcasted_iota(jnp.int32, (vw, kw), 1) // GLA_DK)

    def bd_keys(y):
        return jnp.concatenate([jnp.where(klane == h, y, 0.0) for h in range(group)], axis=0)

    def bd_vals(y):
        return jnp.concatenate([jnp.where(half, y, 0.0), jnp.where(half, 0.0, y)], axis=0)

    ksl = lambda i: slice(i * kw, (i + 1) * kw)
    vsl = lambda i: slice(i * vw, (i + 1) * vw)
    q_in = [o['q_in'][:, ksl(i)] for o, i in prob]
    att = [jnp.where((rowi - coli) * o['sign'] >= 0, _dot_nt(q_, bd_keys(o['k_out'][:, ksl(i)])), 0.0)
           for q_, (o, i) in zip(q_in, prob)]
    st = [st_ref[j] for j in range(len(prob))]
    outs = []
    for j, (o, i) in enumerate(prob):
        v = o['v'][:, vsl(i)]
        intra = jnp.concatenate([_dot(att[j][:, p * 2 * c:(p + 1) * 2 * c], bd_vals(v[:, p * 2 * GLA_DV:(p + 1) * 2 * GLA_DV]))
                                 for p in range(group // 2)], axis=1)
        outs.append(intra + _dot_nt(q_in[j], st[j]))
    for j, (o, i) in enumerate(prob):
        upd = _dot_tn(o['v'][:, vsl(i)], o['k_end'][:, ksl(i)])
        st_ref[j] = st[j] * o['decay'][:, ksl(i)] + jnp.where(own, upd, 0.0)
    of_ref[...] = jnp.concatenate(outs[:n_quads], axis=1)
    ob_ref[...] = jnp.concatenate(outs[n_quads:], axis=1)


def _gla(zg, zdec, dec_w_pad, dec_b, batch, nc_ctx, nc_tot):
    t = zg.shape[0]
    c = SEQ_CHUNK
    hk, hv = GLA_HEADS * GLA_DK, GLA_HEADS * GLA_DV

    def specs(d):
        rw = lambda b, s: b * nc_tot + _chunk_pos(s, d, nc_ctx, nc_tot)
        return [pl.BlockSpec((c, hk), lambda b, s: (rw(b, s), 0)),
                pl.BlockSpec((c, hk), lambda b, s: (rw(b, s), 1)),
                pl.BlockSpec((c, hv), lambda b, s: (rw(b, s), 1)),
                pl.BlockSpec((c, 128), lambda b, s: (rw(b, s), 0))], pl.BlockSpec((c, hv), lambda b, s: (rw(b, s), 0))

    in_f, out_f = specs(0)
    in_b, out_b = specs(1)
    return pl.pallas_call(
        _gla_kernel,
        grid=(batch, nc_tot),
        in_specs=in_f + in_b + [pl.BlockSpec(dec_w_pad.shape, lambda b, s: (0, 0, 0)),
                                pl.BlockSpec(dec_b.shape, lambda b, s: (0, 0, 0))],
        out_specs=[out_f, out_b],
        out_shape=[jax.ShapeDtypeStruct((t, hv), F32)] * 2,
        scratch_shapes=[pltpu.VMEM((2 * GLA_HEADS // 4, 4 * GLA_DV, 4 * GLA_DK), F32)],
        compiler_params=_cparams(("parallel", "arbitrary")),
        name="gla_scan",
    )(zg, zg, zg, zdec, zg, zg, zg, zdec, dec_w_pad, dec_b)


def _rope_swap(x):
    n = x.shape[-1]
    lane = lax.broadcasted_iota(jnp.int32, x.shape, x.ndim - 1)
    half = HEAD_DIM // 2
    return jnp.where(lane % HEAD_DIM < half, pltpu.roll(x, n - half, x.ndim - 1), pltpu.roll(x, half, x.ndim - 1))


def _qk_prep_kernel(q_ref, kv_ref, cos_ref, sin_ref, qg_ref, kg_ref, bdq_ref, bdk_ref, qo_ref, ko_ref, vo_ref):
    def norm_rope(x, gain, bd, n_heads):
        ms = _dot(x * x, bd) * (1.0 / HEAD_DIM)
        xn = x * lax.rsqrt(ms + NORM_EPS) * gain
        cos = jnp.concatenate([cos_ref[...]] * n_heads, axis=1)
        sin = jnp.concatenate([sin_ref[...]] * n_heads, axis=1)
        return xn * cos + _rope_swap(xn) * sin

    q = norm_rope(q_ref[...], qg_ref[...], bdq_ref[...], ATT_HEADS)
    qo_ref[...] = (q * (HEAD_DIM ** -0.5)).astype(BF16)
    kw = ATT_KV_HEADS * HEAD_DIM
    kv = kv_ref[...]
    ko_ref[...] = norm_rope(kv[:, :kw], kg_ref[...], bdk_ref[...], ATT_KV_HEADS).astype(BF16)
    vo_ref[...] = kv[:, kw:].astype(BF16)


def _qk_prep(zq, zkv, cos64, sin64, q_gain, k_gain, tm, l_tot):
    t = zq.shape[0]
    qw, kw = ATT_HEADS * HEAD_DIM, ATT_KV_HEADS * HEAD_DIM
    n_pos = l_tot // tm
    return pl.pallas_call(
        _qk_prep_kernel,
        grid=(t // tm,),
        in_specs=[pl.BlockSpec((tm, qw), lambda i: (i, 0)),
                  pl.BlockSpec((tm, 2 * kw), lambda i: (i, 0)),
                  pl.BlockSpec((tm, HEAD_DIM), lambda i: (i % n_pos, 0)),
                  pl.BlockSpec((tm, HEAD_DIM), lambda i: (i % n_pos, 0)),
                  pl.BlockSpec((1, qw), lambda i: (0, 0)),
                  pl.BlockSpec((1, kw), lambda i: (0, 0)),
                  pl.BlockSpec((qw, qw), lambda i: (0, 0)),
                  pl.BlockSpec((kw, kw), lambda i: (0, 0))],
        out_specs=[pl.BlockSpec((tm, qw), lambda i: (i, 0)),
                   pl.BlockSpec((tm, kw), lambda i: (i, 0)),
                   pl.BlockSpec((tm, kw), lambda i: (i, 0))],
        out_shape=[jax.ShapeDtypeStruct((t, qw), BF16),
                   jax.ShapeDtypeStruct((t, kw), BF16),
                   jax.ShapeDtypeStruct((t, kw), BF16)],
        compiler_params=_cparams(("parallel",)),
        name="qk_prep",
    )(zq, zkv, cos64, sin64, jnp.tile(q_gain, ATT_HEADS).reshape(1, qw), jnp.tile(k_gain, ATT_KV_HEADS).reshape(1, kw),
      _block_ones(ATT_HEADS, HEAD_DIM), _block_ones(ATT_KV_HEADS, HEAD_DIM))


def _attn_kernel(q_ref, kp_ref, kc_ref, kn_ref, kx_ref, vp_ref, vc_ref, vn_ref, vx_ref, sink_ref, o_ref,
                 *, n_ctx_blocks, n_lat_blocks):
    blk = ATT_BLOCK
    n = pl.program_id(1)
    m = n - n_ctx_blocks
    is_lat = n >= n_ctx_blocks
    l_ctx = kx_ref.shape[0]
    width = 3 * blk + l_ctx
    rows = ATT_GROUP * blk
    r = lax.broadcasted_iota(jnp.int32, (rows, width), 0) % blk
    c = lax.broadcasted_iota(jnp.int32, (rows, width), 1)
    lat = is_lat.astype(jnp.int32)
    has_prev = lat * (m >= 1).astype(jnp.int32)
    has_next = lat * (m <= n_lat_blocks - 2).astype(jnp.int32)
    valid = jnp.where(c < blk, (c >= r).astype(jnp.int32) * has_prev,
                      jnp.where(c < 2 * blk, lat,
                                jnp.where(c < 3 * blk, (c - 2 * blk <= r).astype(jnp.int32) * has_next, 1))) > 0
    q = q_ref[...]
    sink = sink_ref[...]
    outs = [None] * ATT_HEADS
    for kvh in range(ATT_KV_HEADS):
        ks = slice(kvh * HEAD_DIM, (kvh + 1) * HEAD_DIM)
        kw = jnp.concatenate([kp_ref[:, ks], kc_ref[:, ks], kn_ref[:, ks], kx_ref[:, ks]], axis=0)
        vw = jnp.concatenate([vp_ref[:, ks], vc_ref[:, ks], vn_ref[:, ks], vx_ref[:, ks]], axis=0)
        heads = range(kvh * ATT_GROUP, (kvh + 1) * ATT_GROUP)
        qg = jnp.concatenate([q[:, h * HEAD_DIM:(h + 1) * HEAD_DIM] for h in heads], axis=0)
        s = jnp.where(valid, _dot_nt(qg, kw), -jnp.inf)
        sk = jnp.concatenate([jnp.broadcast_to(sink[h:h + 1, 0:1], (blk, 1)) for h in heads], axis=0)
        mx = jnp.maximum(jnp.max(s, axis=-1, keepdims=True), sk)
        p = jnp.exp(s - mx)
        denom = jnp.sum(p, axis=-1, keepdims=True) + jnp.exp(sk - mx)
        o = _dot(p.astype(BF16), vw) / denom
        for g, h in enumerate(heads):
            outs[h] = o[g * blk:(g + 1) * blk]
    o_ref[...] = jnp.concatenate(outs, axis=1)


def _attention(qn, kn, vn, sink, batch, l_ctx, l_tot):
    t = qn.shape[0]
    blk = ATT_BLOCK
    nq = l_tot // blk
    nc = l_ctx // blk
    nl = nq - nc
    qw, kw = ATT_HEADS * HEAD_DIM, ATT_KV_HEADS * HEAD_DIM

    def win(off):
        def index(b, n):
            m = jnp.clip(n - nc + off, 0, nl - 1)
            return (b * nq + nc + m, 0)
        return pl.BlockSpec((blk, kw), index)

    ctx_spec = pl.BlockSpec((l_ctx, kw), lambda b, n: (b * (l_tot // l_ctx), 0))
    return pl.pallas_call(
        functools.partial(_attn_kernel, n_ctx_blocks=nc, n_lat_blocks=nl),
        grid=(batch, nq),
        in_specs=[pl.BlockSpec((blk, qw), lambda b, n: (b * nq + n, 0)),
                  win(-1), win(0), win(1), ctx_spec,
                  win(-1), win(0), win(1), ctx_spec,
                  pl.BlockSpec((ATT_HEADS, 128), lambda b, n: (0, 0))],
        out_specs=pl.BlockSpec((blk, qw), lambda b, n: (b * nq + n, 0)),
        out_shape=jax.ShapeDtypeStruct((t, qw), F32),
        compiler_params=_cparams(("parallel", "parallel")),
        name="window_attention",
    )(qn, kn, kn, kn, kn, vn, vn, vn, vn, jnp.broadcast_to(sink.astype(F32)[:, None], (ATT_HEADS, 128)))


def _route(logits_t, bias_col):
    scores = jax.nn.sigmoid(logits_t)
    sel = scores + bias_col
    rows = [sel[e:e + 1] for e in range(N_EXPERTS)]
    grp = []
    for g in range(N_GROUPS):
        r = rows[g * PER_GROUP:(g + 1) * PER_GROUP]
        best = None
        for i in range(PER_GROUP):
            for j in range(i + 1, PER_GROUP):
                pair = r[i] + r[j]
                best = pair if best is None else jnp.maximum(best, pair)
        grp.append(best)
    g_best = jnp.zeros_like(grp[0], dtype=jnp.int32)
    g_val = grp[0]
    for g in range(1, N_GROUPS):
        take = grp[g] > g_val
        g_best = jnp.where(take, g, g_best)
        g_val = jnp.where(take, grp[g], g_val)
    neg = -jnp.inf
    masked = [jnp.where(g_best == e // PER_GROUP, rows[e], neg) for e in range(N_EXPERTS)]
    i1 = jnp.zeros_like(g_best)
    v1 = masked[0]
    for e in range(1, N_EXPERTS):
        take = masked[e] > v1
        i1 = jnp.where(take, e, i1)
        v1 = jnp.where(take, masked[e], v1)
    i2 = jnp.full_like(g_best, -1)
    v2 = jnp.full_like(v1, neg)
    for e in range(N_EXPERTS):
        take = jnp.logical_and(i1 != e, masked[e] > v2)
        i2 = jnp.where(take, e, i2)
        v2 = jnp.where(take, masked[e], v2)
    w1 = jnp.zeros_like(v1)
    w2 = jnp.zeros_like(v1)
    for e in range(N_EXPERTS):
        w1 = jnp.where(i1 == e, scores[e:e + 1], w1)
        w2 = jnp.where(i2 == e, scores[e:e + 1], w2)
    inv = 1.0 / (w1 + w2)
    pad = jnp.zeros_like(w1)
    lane = lax.broadcasted_iota(jnp.int32, w1.shape, 1)
    hist = pad
    for e in range(N_EXPERTS):
        n_e = jnp.sum((i1 == e).astype(F32) + (i2 == e).astype(F32), axis=1, keepdims=True)
        hist = jnp.where(lane == e, n_e, hist)
    return jnp.concatenate([i1.astype(F32), i2.astype(F32), w1 * inv, w2 * inv, hist, pad, pad, pad], axis=0)


def _residual_and_route(x, out, m, ffn_gain, rw_ref, rb_ref, x_ref, h_ref, g_ref):
    x1 = x + m[2:3] * out
    x_ref[...] = x1
    h = _modulated_norm(x1, ffn_gain, m[3:4], m[4:5])
    h_ref[...] = h
    logits_t = lax.dot_general(rw_ref[...], h, (((1,), (1,)), ((), ())), precision=lax.Precision.HIGHEST,
                               preferred_element_type=F32)
    g_ref[...] = _route(logits_t, rb_ref[...])


def _merge_even_kernel(o0_ref, o1_ref, gg_ref, oa_ref, x_ref, mod_ref, gn_ref, bd_ref, w1_ref, w2_ref,
                       fg_ref, rw_ref, rb_ref, xo_ref, ho_ref, go_ref):
    og = o0_ref[...] + o1_ref[...]
    ms = _dot(og * og, bd_ref[...]) * (1.0 / GLA_DV)
    g = og * lax.rsqrt(ms + NORM_EPS) * gn_ref[...] * _silu(gg_ref[...])
    out = _dot(g.astype(BF16), w1_ref[...]) + _dot(oa_ref[...].astype(BF16), w2_ref[...])
    _residual_and_route(x_ref[...], out, mod_ref[0], fg_ref[...], rw_ref, rb_ref, xo_ref, ho_ref, go_ref)


def _merge_even(o_f, o_b, zg, o_att, x, mods, out_norm, w_out, ffn_gain, router_wt, router_b, tm, mod_row):
    t, d = x.shape
    hv = GLA_HEADS * GLA_DV
    qw = ATT_HEADS * HEAD_DIM
    full = lambda a: pl.BlockSpec(a.shape, lambda i: (0,) * a.ndim)
    gn = jnp.tile(out_norm, GLA_HEADS).reshape(1, hv)
    bd = _block_ones(GLA_HEADS, GLA_DV)
    w1, w2 = w_out[:hv].astype(BF16), w_out[hv:].astype(BF16)
    fg = ffn_gain.reshape(1, d)
    rb = router_b.reshape(N_EXPERTS, 1)
    return pl.pallas_call(
        _merge_even_kernel,
        grid=(t // tm,),
        in_specs=[pl.BlockSpec((tm, hv), lambda i: (i, 0)),
                  pl.BlockSpec((tm, hv), lambda i: (i, 0)),
                  pl.BlockSpec((tm, hv), lambda i: (i, 2)),
                  pl.BlockSpec((tm, qw), lambda i: (i, 0)),
                  pl.BlockSpec((tm, d), lambda i: (i, 0)),
                  pl.BlockSpec((1, 6, d), lambda i: (mod_row(i), 0, 0)),
                  full(gn), full(bd), full(w1), full(w2), full(fg), full(router_wt), full(rb)],
        out_specs=[pl.BlockSpec((tm, d), lambda i: (i, 0)),
                   pl.BlockSpec((tm, d), lambda i: (i, 0)),
                   pl.BlockSpec((8, tm), lambda i: (0, i))],
        out_shape=[jax.ShapeDtypeStruct((t, d), F32),
                   jax.ShapeDtypeStruct((t, d), F32),
                   jax.ShapeDtypeStruct((8, t), F32)],
        compiler_params=_cparams(("parallel",)),
        name="merge_even",
    )(o_f, o_b, zg, o_att, x, mods, gn, bd, w1, w2, fg, router_wt, rb)


def _moe_plan(route, t, rows, route_tile):
    n_tiles = 2 * t // rows + N_EXPERTS
    eid = jnp.concatenate([route[0], route[1]]).astype(jnp.int32)
    slot = jnp.arange(2 * t, dtype=jnp.int32)
    gate = jnp.concatenate([route[2], route[3]])
    _, s_slot, s_gate = lax.sort((eid, slot, gate), num_keys=1, is_stable=True)
    counts = jnp.sum(route[4].reshape(t // route_tile, route_tile)[:, :N_EXPERTS], axis=0).astype(jnp.int32)
    padded = (counts + rows - 1) // rows * rows
    p_end = jnp.cumsum(padded)
    p_start = p_end - padded
    c_start = jnp.cumsum(counts) - counts
    tile_start = jnp.arange(n_tiles, dtype=jnp.int32) * rows
    tile_e = jnp.minimum(jnp.sum((tile_start[:, None] >= p_end[None, :]).astype(jnp.int32), axis=1), N_EXPERTS - 1)
    r_in = (tile_start - p_start[tile_e])[:, None] + jnp.arange(rows, dtype=jnp.int32)[None, :]
    over = r_in - counts[tile_e][:, None]
    valid = over < 0
    src = jnp.clip(c_start[tile_e][:, None] + r_in, 0, 2 * t - 1)
    g_slot = s_slot[src]
    row_tok = jnp.where(valid, jnp.where(g_slot >= t, g_slot - t, g_slot), 0)
    row_gate = jnp.where(valid, s_gate[src], 0.0)
    row_dst = jnp.where(valid, g_slot, 2 * t + tile_e[:, None] * rows + jnp.clip(over, 0, rows - 1))
    n_used = (p_end[-1] // rows).astype(jnp.int32)
    tile_e = jnp.where(tile_start < p_end[-1], tile_e, tile_e[jnp.maximum(n_used - 1, 0)])
    return (row_tok[:, None, :], row_gate[:, None, :], row_dst[:, None, :], tile_e, n_used.reshape(1))


def _moe_experts_kernel(te_ref, nu_ref, tok_ref, tokn_ref, dst_ref, gate_ref, wg_ref, wu_ref, wd_ref, h_hbm,
                        y_hbm, hbuf, ybuf, wgb, wub, wdb, sem_g, sem_s):
    j = pl.program_id(0)
    n_used = nu_ref[0]
    slot = j % 2
    rows = hbuf.shape[1]

    def start_gather(idx_ref, s):
        for r in range(rows):
            pltpu.make_async_copy(h_hbm.at[pl.ds(idx_ref[0, 0, r], 1)], hbuf.at[s, pl.ds(r, 1)], sem_g.at[s]).start()

    def wait_gather(s):
        pltpu.make_async_copy(h_hbm.at[pl.ds(0, rows)], hbuf.at[s], sem_g.at[s]).wait()

    def wait_scatter(s):
        pltpu.make_async_copy(ybuf.at[s], y_hbm.at[pl.ds(0, rows)], sem_s.at[s]).wait()

    @pl.when(j == 0)
    def _():
        start_gather(tok_ref, 0)
        ybuf[...] = jnp.zeros(ybuf.shape, F32)
        n_real = y_hbm.shape[0] - N_EXPERTS * rows
        fills = [pltpu.make_async_copy(ybuf.at[k % 2], y_hbm.at[pl.ds(n_real + k * rows, rows)], sem_s.at[k % 2])
                 for k in range(N_EXPERTS)]
        for f in fills:
            f.start()
        for f in fills[2:]:
            f.wait()

    active = j < n_used
    changed = jnp.logical_or(j == 0, te_ref[j] != te_ref[jnp.maximum(j - 1, 0)])

    @pl.when(jnp.logical_and(active, changed))
    def _():
        wgb[...] = wg_ref[0].astype(BF16)
        wub[...] = wu_ref[0].astype(BF16)
        wdb[...] = wd_ref[0].astype(BF16)

    @pl.when(active)
    def _():
        wait_gather(slot)
        start_gather(tokn_ref, 1 - slot)
        h = hbuf[slot].astype(BF16)
        act = _silu(_dot(h, wgb[...])) * _dot(h, wub[...])
        ri = lax.broadcasted_iota(jnp.int32, (rows, rows), 0)
        ci = lax.broadcasted_iota(jnp.int32, (rows, rows), 1)
        gate = jnp.sum(jnp.where(ri == ci, gate_ref[0], 0.0), axis=1, keepdims=True)
        y = _dot((act * gate).astype(BF16), wdb[...])
        wait_scatter(slot)
        ybuf[slot] = y
        for r in range(rows):
            pltpu.make_async_copy(ybuf.at[slot, pl.ds(r, 1)], y_hbm.at[pl.ds(dst_ref[0, 0, r], 1)], sem_s.at[slot]).start()

        @pl.when(j == n_used - 1)
        def _():
            wait_scatter(slot)
            wait_scatter(1 - slot)
            wait_gather(1 - slot)


def _moe_combine_kernel(ya_ref, yb_ref, x_ref, mod_ref, o_ref):
    o_ref[...] = x_ref[...] + mod_ref[0][5:6] * (ya_ref[...] + yb_ref[...])


def _moe(h, route, x, mods, w_gate, w_up, w_down, layer, tm, mod_row, route_tile):
    t, d = x.shape
    rows = MOE_TILE
    row_tok, row_gate, row_dst, tile_e, n_used = _moe_plan(route, t, rows, route_tile)
    n_tiles = row_tok.shape[0]
    n_pair_rows = 2 * t + N_EXPERTS * rows
    smem_tile = lambda nxt: pl.BlockSpec((1, 1, rows), lambda j, te, nu: (jnp.minimum(j + nxt, nu[0] - 1), 0, 0),
                                         memory_space=pltpu.SMEM)
    pairs = pl.pallas_call(
        _moe_experts_kernel,
        grid_spec=pltpu.PrefetchScalarGridSpec(
            num_scalar_prefetch=2,
            grid=(n_tiles,),
            in_specs=[smem_tile(0), smem_tile(1), smem_tile(0),
                      pl.BlockSpec((1, 1, rows), lambda j, te, nu: (j, 0, 0)),
                      pl.BlockSpec((None, 1, d, D_EXPERT), lambda j, te, nu: (layer, te[j], 0, 0)),
                      pl.BlockSpec((None, 1, d, D_EXPERT), lambda j, te, nu: (layer, te[j], 0, 0)),
                      pl.BlockSpec((None, 1, D_EXPERT, d), lambda j, te, nu: (layer, te[j], 0, 0)),
                      pl.BlockSpec(memory_space=pl.ANY)],
            out_specs=pl.BlockSpec(memory_space=pl.ANY),
            scratch_shapes=[pltpu.VMEM((2, rows, d), F32), pltpu.VMEM((2, rows, d), F32),
                            pltpu.VMEM((d, D_EXPERT), BF16), pltpu.VMEM((d, D_EXPERT), BF16),
                            pltpu.VMEM((D_EXPERT, d), BF16),
                            pltpu.SemaphoreType.DMA((2,)), pltpu.SemaphoreType.DMA((2,))]),
        out_shape=jax.ShapeDtypeStruct((n_pair_rows, d), F32),
        compiler_params=_cparams(("arbitrary",)),
        name="moe_experts",
    )(tile_e, n_used, row_tok, row_tok, row_dst, row_gate, w_gate, w_up, w_down, h)
    return pl.pallas_call(
        _moe_combine_kernel,
        grid=(t // tm,),
        in_specs=[pl.BlockSpec((tm, d), lambda i: (i, 0)),
                  pl.BlockSpec((tm, d), lambda i: (t // tm + i, 0)),
                  pl.BlockSpec((tm, d), lambda i: (i, 0)),
                  pl.BlockSpec((1, 6, d), lambda i: (mod_row(i), 0, 0))],
        out_specs=pl.BlockSpec((tm, d), lambda i: (i, 0)),
        out_shape=jax.ShapeDtypeStruct((t, d), F32),
        compiler_params=_cparams(("parallel",)),
        name="moe_combine",
    )(pairs, pairs, x, mods)


def _chan_dft_kernel(z_ref, w_ref, o_ref):
    res = _dot(z_ref[...].astype(BF16), w_ref[...]).astype(BF16)
    o_ref[0] = res[:, :FOURIER_WIDTH]
    o_ref[1] = res[:, FOURIER_WIDTH:]


def _chan_dft(zf, w, batch, l_ctx, l_lat, tm):
    l_tot = l_ctx + l_lat
    nt = l_lat // tm
    fw = FOURIER_WIDTH
    return pl.pallas_call(
        _chan_dft_kernel,
        grid=(batch, nt),
        in_specs=[pl.BlockSpec((tm, fw), lambda b, i: (b * (l_tot // tm) + l_ctx // tm + i, 0)),
                  pl.BlockSpec(w.shape, lambda b, i: (0, 0))],
        out_specs=pl.BlockSpec((2, tm, fw), lambda b, i: (0, i, b)),
        out_shape=jax.ShapeDtypeStruct((2, l_lat, batch * fw), BF16),
        compiler_params=_cparams(("parallel", "parallel")),
        name="fourier_channels",
    )(zf, w)


def _seq_dft_kernel(ca_ref, sa_ref, cb_ref, sb_ref, z_ref, o_ref, acc_ref):
    k = pl.program_id(1)
    tk = z_ref.shape[1]
    sub = DFT_SPLIT
    n_a = ca_ref.shape[1]

    @pl.when(k == 0)
    def _():
        acc_ref[...] = jnp.zeros_like(acc_ref)

    col_a = k * tk + lax.broadcasted_iota(jnp.int32, (n_a, tk), 1)
    pick_a = (col_a // sub == lax.broadcasted_iota(jnp.int32, (n_a, tk), 0)).astype(F32)
    pick_b = (lax.broadcasted_iota(jnp.int32, (sub, tk), 1) % sub == lax.broadcasted_iota(jnp.int32, (sub, tk), 0)).astype(F32)
    ca, sa = _dot(ca_ref[...], pick_a), _dot(sa_ref[...], pick_a)
    cb, sb = _dot(cb_ref[...], pick_b), _dot(sb_ref[...], pick_b)
    cos_t = (ca * cb - sa * sb).astype(BF16)
    sin_t = (sa * cb + ca * sb).astype(BF16)
    acc_ref[...] += _dot(cos_t, z_ref[0]) + _dot(sin_t, z_ref[1])

    @pl.when(k == pl.num_programs(1) - 1)
    def _():
        o_ref[...] = acc_ref[...]


def _seq_dft(tables, zc, batch, tm, tk):
    ca, sa, cb, sb = tables
    l = ca.shape[0]
    fw = FOURIER_WIDTH
    small = lambda a: pl.BlockSpec((tm, a.shape[1]), lambda i, k: (i, 0))
    return pl.pallas_call(
        _seq_dft_kernel,
        grid=(l // tm, l // tk),
        in_specs=[small(ca), small(sa), small(cb), small(sb),
                  pl.BlockSpec((2, tk, batch * fw), lambda i, k: (0, k, 0))],
        out_specs=pl.BlockSpec((tm, batch * fw), lambda i, k: (i, 0)),
        out_shape=jax.ShapeDtypeStruct((l, batch * fw), F32),
        scratch_shapes=[pltpu.VMEM((tm, batch * fw), F32)],
        compiler_params=_cparams(("parallel", "arbitrary")),
        name="fourier_sequence",
    )(ca, sa, cb, sb, zc)


def _dft_tables(l):
    m = jnp.arange(l, dtype=jnp.int32)[:, None]
    n1 = l // DFT_SPLIT
    a = (m * (jnp.arange(n1, dtype=jnp.int32)[None, :] * DFT_SPLIT)) % l
    b = (m * jnp.arange(DFT_SPLIT, dtype=jnp.int32)[None, :]) % l
    wa = a.astype(F32) * (2.0 * np.pi / l)
    wb = b.astype(F32) * (2.0 * np.pi / l)
    tables = (jnp.cos(wa), jnp.sin(wa), jnp.cos(wb), jnp.sin(wb))
    gd = FOURIER_GROUP_DIM
    cc = (jnp.arange(gd, dtype=jnp.int32)[:, None] * jnp.arange(gd, dtype=jnp.int32)[None, :]) % gd
    wc = cc.astype(F32) * (2.0 * np.pi / gd)
    scale = 1.0 / np.sqrt(float(l) * gd)
    eye = jnp.eye(FOURIER_WIDTH // gd, dtype=F32)
    chan = jnp.concatenate([jnp.kron(eye, jnp.cos(wc)), -jnp.kron(eye, jnp.sin(wc))], axis=1) * scale
    return tables, chan.astype(BF16)


def _rwkv_prepare(d, pos, z_ref, zp_ref, zn_ref, par, nc_ctx, nc_tot):
    mu, kks, ka, rk, w0, w2, a0, a2, bd = par
    c = z_ref.shape[0]
    n = RWKV_DIM
    seg_first = jnp.logical_or(pos == 0, pos == nc_ctx)
    seg_last = jnp.logical_or(pos == nc_ctx - 1, pos == nc_tot - 1)
    z = z_ref[...]
    row = lax.broadcasted_iota(jnp.int32, z.shape, 0)
    prev_row = jnp.where(seg_first, 0.0, zp_ref[7:8, :])
    next_row = jnp.where(seg_last, 0.0, zn_ref[0:1, :])
    z_prev = jnp.where(row == 0, prev_row, pltpu.roll(z, 1, 0))
    z_next = jnp.where(row == c - 1, next_row, pltpu.roll(z, c - 1, 0))
    zs = z + mu * (0.5 * (z_prev + z_next) - z)

    r, k, v = zs[:, 0:n], zs[:, n:2 * n], zs[:, 2 * n:3 * n]
    zw = zs[:, 3 * n:3 * n + RWKV_RANK_PAD]
    za = zs[:, 3 * n + RWKV_RANK_PAD:3 * n + 2 * RWKV_RANK_PAD]
    zg = zs[:, 3 * n + 2 * RWKV_RANK_PAD:3 * n + 3 * RWKV_RANK_PAD]

    kk = k * kks
    kk = kk * lax.rsqrt(_dot(kk * kk, bd) + L2_EPS)
    w_log = _log_sigmoid(w0[d] + _dot(jnp.tanh(zw), w2[d])) - 0.5
    lw = -jnp.exp(w_log)
    a = jax.nn.sigmoid(a0[d] + _dot(za, a2[d]))
    kd = k * (1.0 + (a - 1.0) * ka)
    beta = kk * a

    cl = _dot((_visit_order(d, c) >= 0).astype(F32), lw)
    c_tot = jnp.sum(lw, axis=0, keepdims=True)
    grow = jnp.exp(-cl)
    tail = jnp.exp(c_tot - cl)
    ops = dict(k_s=kd * grow,
               b_s=beta * grow,
               kap_s=kk * jnp.exp(cl - lw),
               r_s=r * jnp.exp(cl),
               k_e=kd * tail,
               b_e=beta * tail,
               gam=jnp.exp(c_tot), v=v,
               bonus=_dot(r * kd * rk, bd),
               sign=1 if d == 0 else -1)
    return ops, zg


def _rwkv_kernel(zf_ref, zfp_ref, zfn_ref, zb_ref, zbp_ref, zbn_ref, mu_ref, kks_ref, ka_ref, rk_ref, w0_ref, w2_ref,
                 a0_ref, a2_ref, g2_ref, bd_ref, yf_ref, yb_ref, gate_ref, st_ref, *, nc_ctx, nc_tot):
    s = pl.program_id(1)
    c = zf_ref.shape[0]
    hd = HEAD_DIM
    pw = 2 * hd

    @pl.when(s == 0)
    def _():
        st_ref[...] = jnp.zeros_like(st_ref)

    par = (mu_ref[...], kks_ref[...], ka_ref[...], rk_ref[...], w0_ref, w2_ref, a0_ref, a2_ref, bd_ref[...])
    fwd, zg = _rwkv_prepare(0, _chunk_pos(s, 0, nc_ctx, nc_tot), zf_ref, zfp_ref, zfn_ref, par, nc_ctx, nc_tot)
    bwd, _ = _rwkv_prepare(1, _chunk_pos(s, 1, nc_ctx, nc_tot), zb_ref, zbp_ref, zbn_ref, par, nc_ctx, nc_tot)
    gate_ref[...] = _dot(jax.nn.sigmoid(zg), g2_ref[...])

    lane = lax.broadcasted_iota(jnp.int32, (c, pw), 1)
    rowi = lax.broadcasted_iota(jnp.int32, (c, pw), 0)
    left = lane < hd
    eye_p = (rowi == lane % hd).astype(F32)
    same_head = (lax.broadcasted_iota(jnp.int32, (pw, pw), 0) < hd) == (lax.broadcasted_iota(jnp.int32, (pw, pw), 1) < hd)

    def bd(y):
        return jnp.concatenate([jnp.where(left, y, 0.0), jnp.where(left, 0.0, y)], axis=0)

    n_pairs = RWKV_HEADS // 2
    prob = [(o, slice(i * pw, (i + 1) * pw)) for o in (fwd, bwd) for i in range(n_pairs)]
    ahead = [(rowi - lane % hd) * o['sign'] for o, _ in prob]
    get = lambda name: [o[name][:, s_] for o, s_ in prob]
    kap, r_s, k_s, b_s, k_e, b_e, vp, gam, bonus = (get(x) for x in ('kap_s', 'r_s', 'k_s', 'b_s', 'k_e', 'b_e', 'v', 'gam', 'bonus'))
    p = [_dot_nt(jnp.concatenate([a_, b_], axis=0), jnp.concatenate([bd(c_), bd(d_)], axis=0))
         for a_, b_, c_, d_ in zip(kap, r_s, k_s, b_s)]
    m1 = [jnp.where(h_ > 0, x[0:c, 0:pw], 0.0) for x, h_ in zip(p, ahead)]
    m2 = [jnp.where(h_ > 0, x[0:c, pw:2 * pw], 0.0) for x, h_ in zip(p, ahead)]
    n1 = [jnp.where(h_ >= 0, x[c:2 * c, 0:pw], 0.0) for x, h_ in zip(p, ahead)]
    n2 = [jnp.where(h_ >= 0, x[c:2 * c, pw:2 * pw], 0.0) for x, h_ in zip(p, ahead)]
    m1v = [_dot(a_, bd(b_)) for a_, b_ in zip(m1, vp)]
    t_inv = [eye_p - x for x in m2]
    q = [_dot(x, bd(x)) for x in m2]
    span = 2
    while 2 * span < c:
        both = [_dot(jnp.concatenate([t_, q_], axis=0), bd(q_)) for t_, q_ in zip(t_inv, q)]
        t_inv = [t_ + x[0:c] for t_, x in zip(t_inv, both)]
        q = [x[c:2 * c] for x in both]
        span *= 2
    t_inv = [t_ + _dot(t_, bd(q_)) for t_, q_ in zip(t_inv, q)]
    tx = [_dot(t_, jnp.concatenate([bd(a_), bd(mv)], axis=1)) for t_, a_, mv in zip(t_inv, kap, m1v)]
    st = [st_ref[i] for i in range(len(prob))]
    su = [_dot_nt(jnp.concatenate([x[:, 0:pw], r_], axis=0), s0) for x, r_, s0 in zip(tx, r_s, st)]
    u = [x[0:c] + y_[:, pw:2 * pw] for x, y_ in zip(su, tx)]
    ys = [x[c:2 * c] + _dot(jnp.concatenate([a_, -b_], axis=1), jnp.concatenate([bd(v_), bd(u_)], axis=0)) + bo * v_
          for x, a_, b_, v_, u_, bo in zip(su, n1, n2, vp, u, bonus)]
    for i in range(len(prob)):
        upd = _dot_tn(jnp.concatenate([vp[i], u[i]], axis=0), jnp.concatenate([k_e[i], -b_e[i]], axis=0))
        st_ref[i] = st[i] * gam[i] + jnp.where(same_head, upd, 0.0)
    yf_ref[...] = jnp.concatenate(ys[:n_pairs], axis=1)
    yb_ref[...] = jnp.concatenate(ys[n_pairs:], axis=1)


def _rwkv(zr, mu, kk_scale, k_a, r_k, w0, w2_pad, a0, a2_pad, g2, batch, nc_ctx, nc_tot):
    t, zw_ = zr.shape
    c = SEQ_CHUNK
    n = RWKV_DIM
    bd = _block_ones(RWKV_HEADS, HEAD_DIM)
    full = lambda a: pl.BlockSpec(a.shape, lambda b, s: (0,) * a.ndim)
    sub = c // 8
    n_sub = t // 8

    def rows(d):
        return lambda b, s: b * nc_tot + _chunk_pos(s, d, nc_ctx, nc_tot)

    def z_specs(d):
        rw = rows(d)
        return [pl.BlockSpec((c, zw_), lambda b, s: (rw(b, s), 0)),
                pl.BlockSpec((8, zw_), lambda b, s: (jnp.maximum(rw(b, s) * sub - 1, 0), 0)),
                pl.BlockSpec((8, zw_), lambda b, s: (jnp.minimum((rw(b, s) + 1) * sub, n_sub - 1), 0))]

    vec = lambda a: a.reshape(1, -1)
    args = (vec(mu), vec(kk_scale), vec(k_a), vec(r_k), w0.reshape(2, 1, n), w2_pad, a0.reshape(2, 1, n), a2_pad, g2, bd)
    out = lambda d: pl.BlockSpec((c, n), lambda b, s: (rows(d)(b, s), 0))
    return pl.pallas_call(
        functools.partial(_rwkv_kernel, nc_ctx=nc_ctx, nc_tot=nc_tot),
        grid=(batch, nc_tot),
        in_specs=z_specs(0) + z_specs(1) + [full(a) for a in args],
        out_specs=[out(0), out(1), out(0)],
        out_shape=[jax.ShapeDtypeStruct((t, n), F32)] * 3,
        scratch_shapes=[pltpu.VMEM((RWKV_HEADS, 2 * HEAD_DIM, 2 * HEAD_DIM), F32)],
        compiler_params=_cparams(("parallel", "arbitrary")),
        name="rwkv_scan",
    )(zr, zr, zr, zr, zr, zr, *args)


def _merge_odd_kernel(y0_ref, y1_ref, gate_ref, fo_ref, x_ref, mod_ref, lg_ref, lb_ref, bd_ref, w1_ref, w2_ref,
                      fg_ref, rw_ref, rb_ref, xo_ref, ho_ref, go_ref):
    y = y0_ref[...] + y1_ref[...]
    bd = bd_ref[...]
    mean = _dot(y, bd) * (1.0 / HEAD_DIM)
    yc = y - mean
    var = _dot(yc * yc, bd) * (1.0 / HEAD_DIM)
    rw = (yc * lax.rsqrt(var + RWKV_GN_EPS) * lg_ref[...] + lb_ref[...]) * gate_ref[...]
    out = _dot(fo_ref[...].astype(BF16), w1_ref[...]) + _dot(rw.astype(BF16), w2_ref[...])
    _residual_and_route(x_ref[...], out, mod_ref[0], fg_ref[...], rw_ref, rb_ref, xo_ref, ho_ref, go_ref)


def _merge_odd(y0, y1, gate, fo, x, mods, ln_g, ln_b, w_out, ffn_gain, router_wt, router_b, tm, batch, l_ctx, l_lat):
    d = x.shape[1]
    n = RWKV_DIM
    fw = FOURIER_WIDTH
    l_tot = l_ctx + l_lat
    nt = l_lat // tm
    t_out = batch * l_lat
    full = lambda a: pl.BlockSpec(a.shape, lambda b, i: (0,) * a.ndim)
    src = lambda b, i: b * (l_tot // tm) + l_ctx // tm + i
    bd = _block_ones(RWKV_HEADS, HEAD_DIM)
    w1, w2 = w_out[:fw].astype(BF16), w_out[fw:].astype(BF16)
    lg, lb, fg, rb = ln_g.reshape(1, n), ln_b.reshape(1, n), ffn_gain.reshape(1, d), router_b.reshape(N_EXPERTS, 1)
    return pl.pallas_call(
        _merge_odd_kernel,
        grid=(batch, nt),
        in_specs=[pl.BlockSpec((tm, n), lambda b, i: (src(b, i), 0)),
                  pl.BlockSpec((tm, n), lambda b, i: (src(b, i), 0)),
                  pl.BlockSpec((tm, n), lambda b, i: (src(b, i), 0)),
                  pl.BlockSpec((tm, fw), lambda b, i: (i, b)),
                  pl.BlockSpec((tm, d), lambda b, i: (src(b, i), 0)),
                  pl.BlockSpec((1, 6, d), lambda b, i: (b, 0, 0)),
                  full(lg), full(lb), full(bd), full(w1), full(w2), full(fg), full(router_wt), full(rb)],
        out_specs=[pl.BlockSpec((tm, d), lambda b, i: (b * nt + i, 0)),
                   pl.BlockSpec((tm, d), lambda b, i: (b * nt + i, 0)),
                   pl.BlockSpec((8, tm), lambda b, i: (0, b * nt + i))],
        out_shape=[jax.ShapeDtypeStruct((t_out, d), F32),
                   jax.ShapeDtypeStruct((t_out, d), F32),
                   jax.ShapeDtypeStruct((8, t_out), F32)],
        compiler_params=_cparams(("parallel", "parallel")),
        name="merge_odd",
    )(y0, y1, gate, fo, x, mods, lg, lb, bd, w1, w2, fg, router_wt, rb)


def _rope_tables(l_ctx, l_lat):
    rows = l_lat // GRID_W
    row = jnp.repeat(jnp.arange(rows, dtype=F32), GRID_W)
    col = jnp.tile(jnp.arange(GRID_W, dtype=F32), rows)
    n_freq = HEAD_DIM // 4
    inv_freq = ROPE_THETA ** (-jnp.arange(n_freq, dtype=F32) / n_freq)
    ang = jnp.concatenate([row[:, None] * inv_freq, col[:, None] * inv_freq], axis=-1)
    cos, sin = jnp.cos(ang), jnp.sin(ang)
    cos64 = jnp.concatenate([cos, cos], axis=1)
    sin64 = jnp.concatenate([-sin, sin], axis=1)
    cos64 = jnp.concatenate([jnp.ones((l_ctx, HEAD_DIM), F32), cos64], axis=0)
    sin64 = jnp.concatenate([jnp.zeros((l_ctx, HEAD_DIM), F32), sin64], axis=0)
    return cos64, sin64


def _pad_rank(w):
    _, r, n = w.shape
    out = jnp.zeros((2, RWKV_RANK_PAD, n), w.dtype)
    out = out.at[0, 0:r].set(w[0])
    return out.at[1, r:2 * r].set(w[1])


def _even_layer(x, mods, p, batch, l_ctx, l_lat, tm, mod_row, tm_moe):
    l_tot = l_ctx + l_lat
    d = x.shape[1]
    nc_ctx, nc_tot = l_ctx // SEQ_CHUNK, l_tot // SEQ_CHUNK
    hk, hv = GLA_HEADS * GLA_DK, GLA_HEADS * GLA_DV
    qw, kw = ATT_HEADS * HEAD_DIM, ATT_KV_HEADS * HEAD_DIM
    w_in = p['w_in']
    o = np.cumsum([0, hk, hk, hv, hv, 2 * GLA_LOWRANK, qw, kw, kw])
    w_gla = jnp.concatenate([w_in[:, o[0]:o[4]]], axis=1).astype(BF16)
    w_dec = jnp.pad(w_in[:, o[4]:o[5]], ((0, 0), (0, 128 - 2 * GLA_LOWRANK))).astype(BF16)
    w_q = w_in[:, o[5]:o[6]].astype(BF16)
    w_kv = w_in[:, o[6]:o[8]].astype(BF16)
    zg, zdec, zq, zkv = _project(x, mods, p['norm_mix'], [w_gla, w_dec, w_q, w_kv], tm, mod_row)

    dec_w_pad = _pad_rank(p['dec_w'])
    o_f, o_b = _gla(zg, zdec, dec_w_pad, p['dec_b'].reshape(2, 1, hk), batch, nc_ctx, nc_tot)

    cos64, sin64 = _rope_tables(l_ctx, l_lat)
    qn, kn, vn = _qk_prep(zq, zkv, cos64, sin64, p['q_norm'], p['k_norm'], tm, l_tot)
    o_att = _attention(qn, kn, vn, p['sink'], batch, l_ctx, l_tot)

    x1, h, gates = _merge_even(o_f, o_b, zg, o_att, x, mods, p['out_norm'], p['w_out'], p['norm_ffn'],
                               p['router_wt'], p['router_b'], tm, mod_row)
    return _moe(h, gates, x1, mods, p['moe_g'], p['moe_u'], p['moe_d'], p['moe_layer'], tm, mod_row, tm)


def _odd_layer(x, mods, p, batch, l_ctx, l_lat, tm, mod_row, tm_moe):
    l_tot = l_ctx + l_lat
    nc_ctx, nc_tot = l_ctx // SEQ_CHUNK, l_tot // SEQ_CHUNK
    n = RWKV_DIM
    fw = FOURIER_WIDTH
    w_in = p['w_in']
    rank_w, rank_a = p['w2'].shape[1], p['a2'].shape[1]
    o = np.cumsum([0, fw, n, n, n, 2 * rank_w, 2 * rank_a])
    pad_cols = lambda w: jnp.pad(w, ((0, 0), (0, RWKV_RANK_PAD - w.shape[1])))
    w_f = w_in[:, o[0]:o[1]].astype(BF16)
    w_r = jnp.concatenate([w_in[:, o[1]:o[4]], pad_cols(w_in[:, o[4]:o[5]]), pad_cols(w_in[:, o[5]:o[6]]),
                           w_in[:, o[6]:]], axis=1).astype(BF16)
    zf, zr = _project(x, mods, p['norm_mix'], [w_f, w_r], tm, mod_row)

    mu = p['mu']
    mu_r = jnp.concatenate([mu[0:3 * n], pad_cols(mu[None, 3 * n:3 * n + 2 * rank_w])[0],
                            pad_cols(mu[None, 3 * n + 2 * rank_w:3 * n + 2 * rank_w + 2 * rank_a])[0],
                            mu[3 * n + 2 * rank_w + 2 * rank_a:]])
    y0, y1, gate = _rwkv(zr, mu_r, p['kk_scale'], p['k_a'], p['r_k'].reshape(-1), p['w0'], _pad_rank(p['w2']),
                    p['a0'], _pad_rank(p['a2']), p['g2'], batch, nc_ctx, nc_tot)

    tables, chan = _dft_tables(l_lat)
    zc = _chan_dft(zf, chan, batch, l_ctx, l_lat, tm)
    fo = _seq_dft(tables, zc, batch, min(512, l_lat), min(1024, l_lat))

    x1, h, gates = _merge_odd(y0, y1, gate, fo, x, mods, p['ln_g'], p['ln_b'], p['w_out'], p['norm_ffn'],
                              p['router_wt'], p['router_b'], tm, batch, l_ctx, l_lat)
    lat_tiles = l_lat // tm_moe
    return _moe(h, gates, x1, mods, p['moe_g'], p['moe_u'], p['moe_d'], p['moe_layer'], tm_moe,
                lambda i: i // lat_tiles, tm)


def kernel(x, c, ctx, c_ctx, ada_w, ada_b, norm_mix, norm_ffn, even_w_in, even_w_out, gla_dec_w, gla_dec_b, gla_out_norm, att_q_norm, att_k_norm, att_sink, odd_w_in, odd_w_out, rwkv_mu, rwkv_w0, rwkv_w2, rwkv_a0, rwkv_a2, rwkv_g2, rwkv_kk_scale, rwkv_k_a, rwkv_r_k, rwkv_ln_g, rwkv_ln_b, router_w, router_b, moe_w_gate, moe_w_up, moe_w_down):
    batch, l_lat, d = x.shape
    l_ctx = ctx.shape[1]
    l_tot = l_ctx + l_lat
    assert batch < 8 and ada_w.shape[0] == 2
    tm = 256 if (l_ctx % 256 == 0 and l_lat % 256 == 0) else 128
    tm_moe = 512 if (l_lat % 512 == 0 and tm == 256) else tm
    assert l_ctx % tm == 0 and l_lat % tm == 0 and l_tot % l_ctx == 0 and l_lat % GRID_W == 0

    xs = jnp.concatenate([ctx, x], axis=1).reshape(batch * l_tot, d)
    cc = jnp.concatenate([c, c_ctx[None, :], jnp.zeros((8 - batch - 1, d), F32)], axis=0)
    tiles_per_b = l_tot // tm
    ctx_tiles = l_ctx // tm

    def mod_row(i):
        return jnp.where(i % tiles_per_b < ctx_tiles, batch, i // tiles_per_b)

    router_wt = router_w.T
    moe = lambda layer: dict(moe_g=moe_w_gate, moe_u=moe_w_up, moe_d=moe_w_down, moe_layer=layer)

    mods_all = _modvec(cc, ada_w, ada_b)
    mods0 = mods_all[0]
    p0 = dict(w_in=even_w_in[0], w_out=even_w_out[0], dec_w=gla_dec_w[0], dec_b=gla_dec_b[0],
              out_norm=gla_out_norm[0], q_norm=att_q_norm[0], k_norm=att_k_norm[0], sink=att_sink[0],
              norm_mix=norm_mix[0], norm_ffn=norm_ffn[0], router_wt=router_wt, router_b=router_b, **moe(0))
    xs = _even_layer(xs, mods0, p0, batch, l_ctx, l_lat, tm, mod_row, tm_moe)

    mods1 = mods_all[1]
    p1 = dict(w_in=odd_w_in[0], w_out=odd_w_out[0], mu=rwkv_mu[0], w0=rwkv_w0[0], w2=rwkv_w2[0], a0=rwkv_a0[0],
              a2=rwkv_a2[0], g2=rwkv_g2[0], kk_scale=rwkv_kk_scale[0], k_a=rwkv_k_a[0], r_k=rwkv_r_k[0],
              ln_g=rwkv_ln_g[0], ln_b=rwkv_ln_b[0], norm_mix=norm_mix[1], norm_ffn=norm_ffn[1],
              router_wt=router_wt, router_b=router_b, **moe(1))
    out = _odd_layer(xs, mods1, p1, batch, l_ctx, l_lat, tm, mod_row, tm_moe)
    return out.reshape(batch, l_lat, d)
```

```python
import functools

import jax
import jax.numpy as jnp
import numpy as np
from jax import lax
from jax.experimental import pallas as pl
from jax.experimental.pallas import tpu as pltpu

F32 = jnp.float32
BF16 = jnp.bfloat16

GRID_W = 64
HEAD_DIM = 64
NORM_EPS = 1e-6
L2_EPS = 1e-12

GLA_DV = 64
GLA_DK = 32
GLA_HEADS = 8
GLA_LOWRANK = 16
GLA_TAU = 16.0

ATT_HEADS = 8
ATT_KV_HEADS = 2
ATT_GROUP = ATT_HEADS // ATT_KV_HEADS
ATT_BLOCK = 128
ROPE_THETA = 10000.0

FOURIER_GROUP_DIM = 64
FOURIER_WIDTH = 256
DFT_SPLIT = 64

RWKV_DIM = 768
RWKV_HEADS = 12
RWKV_RANK_PAD = 128
RWKV_GN_EPS = 64e-5

N_EXPERTS = 16
N_GROUPS = 4
PER_GROUP = N_EXPERTS // N_GROUPS
D_EXPERT = 512
MOE_TILE = 256
LANES = 128
SUBLANES = 8

SEQ_CHUNK = 64
VMEM_LIMIT = 56 * 1024 * 1024


def _cparams(sem):
    return pltpu.CompilerParams(dimension_semantics=sem, vmem_limit_bytes=VMEM_LIMIT)


def _dot(a, b):
    return jnp.dot(a, b, preferred_element_type=F32)


def _dot_nt(a, b):
    return lax.dot_general(a, b, (((1,), (1,)), ((), ())), preferred_element_type=F32)


def _dot_tn(a, b):
    return lax.dot_general(a, b, (((0,), (0,)), ((), ())), preferred_element_type=F32)


def _silu(x):
    return x * jax.nn.sigmoid(x)


def _log_sigmoid(x):
    return jnp.minimum(x, 0.0) - jnp.log(1.0 + jnp.exp(-jnp.abs(x)))


def _modulated_norm(x, gain, shift, scale):
    ms = jnp.mean(x * x, axis=-1, keepdims=True)
    return (x * lax.rsqrt(ms + NORM_EPS) * gain) * (1.0 + scale) + shift


def _block_ones(n_blocks, width):
    return jnp.kron(jnp.eye(n_blocks, dtype=F32), jnp.ones((width, width), F32))


def _store_row_tiles(ref, x, first_row=0):
    n, w = x.shape
    g = w // LANES
    for k in range(g):
        ref[pl.ds(first_row * g + k, n, stride=g), :] = x[:, k * LANES:(k + 1) * LANES]


def _load_row_tiles(ref, n, g, first_row=0):
    return jnp.concatenate([ref[pl.ds(first_row * g + k, n, stride=g), :] for k in range(g)], axis=1)


def _modvec_kernel(c_ref, w_ref, b_ref, o_ref):
    o_ref[...] = _dot(_silu(c_ref[...]), w_ref[...]) + b_ref[...]


def _modvec(cc, w, b):
    d = cc.shape[1]
    layers, _, n = w.shape
    tn = n // 4
    out = pl.pallas_call(
        _modvec_kernel,
        grid=(layers, n // tn),
        in_specs=[pl.BlockSpec((8, d), lambda l, j: (0, 0)),
                  pl.BlockSpec((None, d, tn), lambda l, j: (l, 0, j)),
                  pl.BlockSpec((None, 1, tn), lambda l, j: (l, 0, j))],
        out_specs=pl.BlockSpec((None, 8, tn), lambda l, j: (l, 0, j)),
        out_shape=jax.ShapeDtypeStruct((layers, 8, n), F32),
        compiler_params=_cparams(("parallel", "parallel")),
        name="modvec",
    )(cc, w, b.reshape(layers, 1, n))
    return out.reshape(layers, 8, 6, d)


def _proj_kernel(x_ref, mod_ref, gain_ref, *refs, n_out):
    w_refs, z_refs = refs[:n_out], refs[n_out:]
    m = mod_ref[0]
    h = _modulated_norm(x_ref[...], gain_ref[...], m[0:1], m[1:2]).astype(BF16)
    for w_ref, z_ref in zip(w_refs, z_refs):
        z_ref[...] = _dot(h, w_ref[...])


def _project(x, mods, gain, weights, tm, mod_row):
    t, d = x.shape
    n_out = len(weights)
    in_specs = [pl.BlockSpec((tm, d), lambda i: (i, 0)),
                pl.BlockSpec((1, 6, d), lambda i: (mod_row(i), 0, 0)),
                pl.BlockSpec((1, d), lambda i: (0, 0))]
    in_specs += [pl.BlockSpec(w.shape, lambda i: (0, 0)) for w in weights]
    return pl.pallas_call(
        functools.partial(_proj_kernel, n_out=n_out),
        grid=(t // tm,),
        in_specs=in_specs,
        out_specs=[pl.BlockSpec((tm, w.shape[1]), lambda i: (i, 0)) for w in weights],
        out_shape=[jax.ShapeDtypeStruct((t, w.shape[1]), F32) for w in weights],
        compiler_params=_cparams(("parallel",)),
        name="proj",
    )(x, mods, gain.reshape(1, d), *weights)


def _chunk_pos(s, d, nc_ctx, nc_tot):
    back = jnp.where(s < nc_ctx, nc_ctx - 1 - s, nc_tot + nc_ctx - 1 - s)
    return jnp.where(d == 0, s, back)


def _visit_order(d, c):
    sign = 1 if d == 0 else -1
    return (lax.broadcasted_iota(jnp.int32, (c, c), 0) - lax.broadcasted_iota(jnp.int32, (c, c), 1)) * sign


def _gla_prepare(d, q_ref, k_ref, v_ref, dec_ref, dw_ref, db_ref):
    c = q_ref.shape[0]
    g = _log_sigmoid(_dot(dec_ref[...], dw_ref[d]) + db_ref[d]) / GLA_TAU
    b = _dot((_visit_order(d, c) >= 0).astype(F32), g)
    b_tot = jnp.sum(g, axis=0, keepdims=True)
    k = k_ref[...]
    return dict(q_in=q_ref[...] * (GLA_DK ** -0.5) * jnp.exp(b), k_out=k * jnp.exp(-b), k_end=k * jnp.exp(b_tot - b),
                decay=jnp.exp(b_tot), v=v_ref[...], sign=1 if d == 0 else -1)


def _gla_kernel(qf_ref, kf_ref, vf_ref, df_ref, qb_ref, kb_ref, vb_ref, db_ref, dw_ref, dbias_ref, of_ref, ob_ref, st_ref):
    c = qf_ref.shape[0]
    group = 4
    kw, vw = group * GLA_DK, group * GLA_DV

    @pl.when(pl.program_id(1) == 0)
    def _():
        st_ref[...] = jnp.zeros_like(st_ref)

    fwd = _gla_prepare(0, qf_ref, kf_ref, vf_ref, df_ref, dw_ref, dbias_ref)
    bwd = _gla_prepare(1, qb_ref, kb_ref, vb_ref, db_ref, dw_ref, dbias_ref)
    n_quads = GLA_HEADS // group
    prob = [(o, i) for o in (fwd, bwd) for i in range(n_quads)]
    klane = lax.broadcasted_iota(jnp.int32, (c, kw), 1) // GLA_DK
    half = lax.broadcasted_iota(jnp.int32, (c, 2 * GLA_DV), 1) < GLA_DV
    rowi = lax.broadcasted_iota(jnp.int32, (c, group * c), 0)
    coli = lax.broadcasted_iota(jnp.int32, (c, group * c), 1) % c
    own = (lax.broadcasted_iota(jnp.int32, (vw, kw), 0) // GLA_DV) == (lax.broadcasted_iota(jnp.int32, (vw, kw), 1) // GLA_DK)

    def bd_keys(y):
        return jnp.concatenate([jnp.where(klane == h, y, 0.0) for h in range(group)], axis=0)

    def bd_vals(y):
        return jnp.concatenate([jnp.where(half, y, 0.0), jnp.where(half, 0.0, y)], axis=0)

    ksl = lambda i: slice(i * kw, (i + 1) * kw)
    vsl = lambda i: slice(i * vw, (i + 1) * vw)
    q_in = [o['q_in'][:, ksl(i)] for o, i in prob]
    att = [jnp.where((rowi - coli) * o['sign'] >= 0, _dot_nt(q_, bd_keys(o['k_out'][:, ksl(i)])), 0.0)
           for q_, (o, i) in zip(q_in, prob)]
    st = [st_ref[j] for j in range(len(prob))]
    outs = []
    for j, (o, i) in enumerate(prob):
        v = o['v'][:, vsl(i)]
        intra = jnp.concatenate([_dot(att[j][:, p * 2 * c:(p + 1) * 2 * c], bd_vals(v[:, p * 2 * GLA_DV:(p + 1) * 2 * GLA_DV]))
                                 for p in range(group // 2)], axis=1)
        outs.append(intra + _dot_nt(q_in[j], st[j]))
    for j, (o, i) in enumerate(prob):
        upd = _dot_tn(o['v'][:, vsl(i)], o['k_end'][:, ksl(i)])
        st_ref[j] = st[j] * o['decay'][:, ksl(i)] + jnp.where(own, upd, 0.0)
    of_ref[...] = jnp.concatenate(outs[:n_quads], axis=1)
    ob_ref[...] = jnp.concatenate(outs[n_quads:], axis=1)


def _gla(zg, zdec, dec_w_pad, dec_b, batch, nc_ctx, nc_tot):
    t = zg.shape[0]
    c = SEQ_CHUNK
    hk, hv = GLA_HEADS * GLA_DK, GLA_HEADS * GLA_DV

    def specs(d):
        rw = lambda b, s: b * nc_tot + _chunk_pos(s, d, nc_ctx, nc_tot)
        return [pl.BlockSpec((c, hk), lambda b, s: (rw(b, s), 0)),
                pl.BlockSpec((c, hk), lambda b, s: (rw(b, s), 1)),
                pl.BlockSpec((c, hv), lambda b, s: (rw(b, s), 1)),
                pl.BlockSpec((c, 128), lambda b, s: (rw(b, s), 0))], pl.BlockSpec((c, hv), lambda b, s: (rw(b, s), 0))

    in_f, out_f = specs(0)
    in_b, out_b = specs(1)
    return pl.pallas_call(
        _gla_kernel,
        grid=(batch, nc_tot),
        in_specs=in_f + in_b + [pl.BlockSpec(dec_w_pad.shape, lambda b, s: (0, 0, 0)),
                                pl.BlockSpec(dec_b.shape, lambda b, s: (0, 0, 0))],
        out_specs=[out_f, out_b],
        out_shape=[jax.ShapeDtypeStruct((t, hv), F32)] * 2,
        scratch_shapes=[pltpu.VMEM((2 * GLA_HEADS // 4, 4 * GLA_DV, 4 * GLA_DK), F32)],
        compiler_params=_cparams(("parallel", "arbitrary")),
        name="gla_scan",
    )(zg, zg, zg, zdec, zg, zg, zg, zdec, dec_w_pad, dec_b)


def _rope_swap(x):
    n = x.shape[-1]
    lane = lax.broadcasted_iota(jnp.int32, x.shape, x.ndim - 1)
    half = HEAD_DIM // 2
    return jnp.where(lane % HEAD_DIM < half, pltpu.roll(x, n - half, x.ndim - 1), pltpu.roll(x, half, x.ndim - 1))


def _qk_prep_kernel(q_ref, kv_ref, cos_ref, sin_ref, qg_ref, kg_ref, bdq_ref, bdk_ref, qo_ref, ko_ref, vo_ref):
    def norm_rope(x, gain, bd, n_heads):
        ms = _dot(x * x, bd) * (1.0 / HEAD_DIM)
        xn = x * lax.rsqrt(ms + NORM_EPS) * gain
        cos = jnp.concatenate([cos_ref[...]] * n_heads, axis=1)
        sin = jnp.concatenate([sin_ref[...]] * n_heads, axis=1)
        return xn * cos + _rope_swap(xn) * sin

    q = norm_rope(q_ref[...], qg_ref[...], bdq_ref[...], ATT_HEADS)
    qo_ref[...] = (q * (HEAD_DIM ** -0.5)).astype(BF16)
    kw = ATT_KV_HEADS * HEAD_DIM
    kv = kv_ref[...]
    ko_ref[...] = norm_rope(kv[:, :kw], kg_ref[...], bdk_ref[...], ATT_KV_HEADS).astype(BF16)
    vo_ref[...] = kv[:, kw:].astype(BF16)


def _qk_prep(zq, zkv, cos64, sin64, q_gain, k_gain, tm, l_tot):
    t = zq.shape[0]
    qw, kw = ATT_HEADS * HEAD_DIM, ATT_KV_HEADS * HEAD_DIM
    n_pos = l_tot // tm
    return pl.pallas_call(
        _qk_prep_kernel,
        grid=(t // tm,),
        in_specs=[pl.BlockSpec((tm, qw), lambda i: (i, 0)),
                  pl.BlockSpec((tm, 2 * kw), lambda i: (i, 0)),
                  pl.BlockSpec((tm, HEAD_DIM), lambda i: (i % n_pos, 0)),
                  pl.BlockSpec((tm, HEAD_DIM), lambda i: (i % n_pos, 0)),
                  pl.BlockSpec((1, qw), lambda i: (0, 0)),
                  pl.BlockSpec((1, kw), lambda i: (0, 0)),
                  pl.BlockSpec((qw, qw), lambda i: (0, 0)),
                  pl.BlockSpec((kw, kw), lambda i: (0, 0))],
        out_specs=[pl.BlockSpec((tm, qw), lambda i: (i, 0)),
                   pl.BlockSpec((tm, kw), lambda i: (i, 0)),
                   pl.BlockSpec((tm, kw), lambda i: (i, 0))],
        out_shape=[jax.ShapeDtypeStruct((t, qw), BF16),
                   jax.ShapeDtypeStruct((t, kw), BF16),
                   jax.ShapeDtypeStruct((t, kw), BF16)],
        compiler_params=_cparams(("parallel",)),
        name="qk_prep",
    )(zq, zkv, cos64, sin64, jnp.tile(q_gain, ATT_HEADS).reshape(1, qw), jnp.tile(k_gain, ATT_KV_HEADS).reshape(1, kw),
      _block_ones(ATT_HEADS, HEAD_DIM), _block_ones(ATT_KV_HEADS, HEAD_DIM))


def _attn_kernel(q_ref, kp_ref, kc_ref, kn_ref, kx_ref, vp_ref, vc_ref, vn_ref, vx_ref, sink_ref, o_ref,
                 *, n_ctx_blocks, n_lat_blocks):
    blk = ATT_BLOCK
    n = pl.program_id(1)
    m = n - n_ctx_blocks
    is_lat = n >= n_ctx_blocks
    l_ctx = kx_ref.shape[0]
    width = 3 * blk + l_ctx
    rows = ATT_GROUP * blk
    r = lax.broadcasted_iota(jnp.int32, (rows, width), 0) % blk
    c = lax.broadcasted_iota(jnp.int32, (rows, width), 1)
    lat = is_lat.astype(jnp.int32)
    has_prev = lat * (m >= 1).astype(jnp.int32)
    has_next = lat * (m <= n_lat_blocks - 2).astype(jnp.int32)
    valid = jnp.where(c < blk, (c >= r).astype(jnp.int32) * has_prev,
                      jnp.where(c < 2 * blk, lat,
                                jnp.where(c < 3 * blk, (c - 2 * blk <= r).astype(jnp.int32) * has_next, 1))) > 0
    q = q_ref[...]
    sink = sink_ref[...]
    outs = [None] * ATT_HEADS
    for kvh in range(ATT_KV_HEADS):
        ks = slice(kvh * HEAD_DIM, (kvh + 1) * HEAD_DIM)
        kw = jnp.concatenate([kp_ref[:, ks], kc_ref[:, ks], kn_ref[:, ks], kx_ref[:, ks]], axis=0)
        vw = jnp.concatenate([vp_ref[:, ks], vc_ref[:, ks], vn_ref[:, ks], vx_ref[:, ks]], axis=0)
        heads = range(kvh * ATT_GROUP, (kvh + 1) * ATT_GROUP)
        qg = jnp.concatenate([q[:, h * HEAD_DIM:(h + 1) * HEAD_DIM] for h in heads], axis=0)
        s = jnp.where(valid, _dot_nt(qg, kw), -jnp.inf)
        sk = jnp.concatenate([jnp.broadcast_to(sink[h:h + 1, 0:1], (blk, 1)) for h in heads], axis=0)
        mx = jnp.maximum(jnp.max(s, axis=-1, keepdims=True), sk)
        p = jnp.exp(s - mx)
        denom = jnp.sum(p, axis=-1, keepdims=True) + jnp.exp(sk - mx)
        o = _dot(p.astype(BF16), vw) / denom
        for g, h in enumerate(heads):
            outs[h] = o[g * blk:(g + 1) * blk]
    o_ref[...] = jnp.concatenate(outs, axis=1)


def _attention(qn, kn, vn, sink, batch, l_ctx, l_tot):
    t = qn.shape[0]
    blk = ATT_BLOCK
    nq = l_tot // blk
    nc = l_ctx // blk
    nl = nq - nc
    qw, kw = ATT_HEADS * HEAD_DIM, ATT_KV_HEADS * HEAD_DIM

    def win(off):
        def index(b, n):
            m = jnp.clip(n - nc + off, 0, nl - 1)
            return (b * nq + nc + m, 0)
        return pl.BlockSpec((blk, kw), index)

    ctx_spec = pl.BlockSpec((l_ctx, kw), lambda b, n: (b * (l_tot // l_ctx), 0))
    return pl.pallas_call(
        functools.partial(_attn_kernel, n_ctx_blocks=nc, n_lat_blocks=nl),
        grid=(batch, nq),
        in_specs=[pl.BlockSpec((blk, qw), lambda b, n: (b * nq + n, 0)),
                  win(-1), win(0), win(1), ctx_spec,
                  win(-1), win(0), win(1), ctx_spec,
                  pl.BlockSpec((ATT_HEADS, 128), lambda b, n: (0, 0))],
        out_specs=pl.BlockSpec((blk, qw), lambda b, n: (b * nq + n, 0)),
        out_shape=jax.ShapeDtypeStruct((t, qw), F32),
        compiler_params=_cparams(("parallel", "parallel")),
        name="window_attention",
    )(qn, kn, kn, kn, kn, vn, vn, vn, vn, jnp.broadcast_to(sink.astype(F32)[:, None], (ATT_HEADS, 128)))


def _route(logits_t, bias_col):
    scores = jax.nn.sigmoid(logits_t)
    sel = scores + bias_col
    rows = [sel[e:e + 1] for e in range(N_EXPERTS)]
    grp = []
    for g in range(N_GROUPS):
        r = rows[g * PER_GROUP:(g + 1) * PER_GROUP]
        best = None
        for i in range(PER_GROUP):
            for j in range(i + 1, PER_GROUP):
                pair = r[i] + r[j]
                best = pair if best is None else jnp.maximum(best, pair)
        grp.append(best)
    g_best = jnp.zeros_like(grp[0], dtype=jnp.int32)
    g_val = grp[0]
    for g in range(1, N_GROUPS):
        take = grp[g] > g_val
        g_best = jnp.where(take, g, g_best)
        g_val = jnp.where(take, grp[g], g_val)
    neg = -jnp.inf
    masked = [jnp.where(g_best == e // PER_GROUP, rows[e], neg) for e in range(N_EXPERTS)]
    i1 = jnp.zeros_like(g_best)
    v1 = masked[0]
    for e in range(1, N_EXPERTS):
        take = masked[e] > v1
        i1 = jnp.where(take, e, i1)
        v1 = jnp.where(take, masked[e], v1)
    i2 = jnp.full_like(g_best, -1)
    v2 = jnp.full_like(v1, neg)
    for e in range(N_EXPERTS):
        take = jnp.logical_and(i1 != e, masked[e] > v2)
        i2 = jnp.where(take, e, i2)
        v2 = jnp.where(take, masked[e], v2)
    w1 = jnp.zeros_like(v1)
    w2 = jnp.zeros_like(v1)
    for e in range(N_EXPERTS):
        w1 = jnp.where(i1 == e, scores[e:e + 1], w1)
        w2 = jnp.where(i2 == e, scores[e:e + 1], w2)
    inv = 1.0 / (w1 + w2)
    pad = jnp.zeros_like(w1)
    lane = lax.broadcasted_iota(jnp.int32, w1.shape, 1)
    hist = pad
    for e in range(N_EXPERTS):
        n_e = jnp.sum((i1 == e).astype(F32) + (i2 == e).astype(F32), axis=1, keepdims=True)
        hist = jnp.where(lane == e, n_e, hist)
    return jnp.concatenate([i1.astype(F32), i2.astype(F32), w1 * inv, w2 * inv, hist, pad, pad, pad], axis=0)


def _residual_and_route(x, out, m, ffn_gain, rw_ref, rb_ref, x_ref, h_ref, g_ref):
    x1 = x + m[2:3] * out
    x_ref[...] = x1
    h = _modulated_norm(x1, ffn_gain, m[3:4], m[4:5])
    _store_row_tiles(h_ref, h)
    logits_t = lax.dot_general(rw_ref[...], h, (((1,), (1,)), ((), ())), precision=lax.Precision.HIGHEST,
                               preferred_element_type=F32)
    g_ref[...] = _route(logits_t, rb_ref[...])


def _merge_even_kernel(o0_ref, o1_ref, gg_ref, oa_ref, x_ref, mod_ref, gn_ref, bd_ref, w1_ref, w2_ref,
                       fg_ref, rw_ref, rb_ref, xo_ref, ho_ref, go_ref):
    og = o0_ref[...] + o1_ref[...]
    ms = _dot(og * og, bd_ref[...]) * (1.0 / GLA_DV)
    g = og * lax.rsqrt(ms + NORM_EPS) * gn_ref[...] * _silu(gg_ref[...])
    out = _dot(g.astype(BF16), w1_ref[...]) + _dot(oa_ref[...].astype(BF16), w2_ref[...])
    _residual_and_route(x_ref[...], out, mod_ref[0], fg_ref[...], rw_ref, rb_ref, xo_ref, ho_ref, go_ref)


def _merge_even(o_f, o_b, zg, o_att, x, mods, out_norm, w_out, ffn_gain, router_wt, router_b, tm, mod_row):
    t, d = x.shape
    hv = GLA_HEADS * GLA_DV
    qw = ATT_HEADS * HEAD_DIM
    full = lambda a: pl.BlockSpec(a.shape, lambda i: (0,) * a.ndim)
    gn = jnp.tile(out_norm, GLA_HEADS).reshape(1, hv)
    bd = _block_ones(GLA_HEADS, GLA_DV)
    w1, w2 = w_out[:hv].astype(BF16), w_out[hv:].astype(BF16)
    fg = ffn_gain.reshape(1, d)
    rb = router_b.reshape(N_EXPERTS, 1)
    return pl.pallas_call(
        _merge_even_kernel,
        grid=(t // tm,),
        in_specs=[pl.BlockSpec((tm, hv), lambda i: (i, 0)),
                  pl.BlockSpec((tm, hv), lambda i: (i, 0)),
                  pl.BlockSpec((tm, hv), lambda i: (i, 2)),
                  pl.BlockSpec((tm, qw), lambda i: (i, 0)),
                  pl.BlockSpec((tm, d), lambda i: (i, 0)),
                  pl.BlockSpec((1, 6, d), lambda i: (mod_row(i), 0, 0)),
                  full(gn), full(bd), full(w1), full(w2), full(fg), full(router_wt), full(rb)],
        out_specs=[pl.BlockSpec((tm, d), lambda i: (i, 0)),
                   pl.BlockSpec((tm * (d // LANES), LANES), lambda i: (i, 0)),
                   pl.BlockSpec((8, tm), lambda i: (0, i))],
        out_shape=[jax.ShapeDtypeStruct((t, d), F32),
                   jax.ShapeDtypeStruct((t * (d // LANES), LANES), F32),
                   jax.ShapeDtypeStruct((8, t), F32)],
        compiler_params=_cparams(("parallel",)),
        name="merge_even",
    )(o_f, o_b, zg, o_att, x, mods, gn, bd, w1, w2, fg, router_wt, rb)


def _moe_plan(route, t, rows, route_tile):
    n_tiles = 2 * t // rows + N_EXPERTS
    eid = jnp.concatenate([route[0], route[1]]).astype(jnp.int32)
    slot = jnp.arange(2 * t, dtype=jnp.int32)
    gate = jnp.concatenate([route[2], route[3]])
    _, s_slot, s_gate = lax.sort((eid, slot, gate), num_keys=1, is_stable=True)
    counts = jnp.sum(route[4].reshape(t // route_tile, route_tile)[:, :N_EXPERTS], axis=0).astype(jnp.int32)
    padded = (counts + rows - 1) // rows * rows
    p_end = jnp.cumsum(padded)
    p_start = p_end - padded
    c_start = jnp.cumsum(counts) - counts
    tile_start = jnp.arange(n_tiles, dtype=jnp.int32) * rows
    tile_e = jnp.minimum(jnp.sum((tile_start[:, None] >= p_end[None, :]).astype(jnp.int32), axis=1), N_EXPERTS - 1)
    r_in = (tile_start - p_start[tile_e])[:, None] + jnp.arange(rows, dtype=jnp.int32)[None, :]
    over = r_in - counts[tile_e][:, None]
    valid = over < 0
    src = jnp.clip(c_start[tile_e][:, None] + r_in, 0, 2 * t - 1)
    g_slot = s_slot[src]
    row_tok = jnp.where(valid, jnp.where(g_slot >= t, g_slot - t, g_slot), 0)
    row_gate = jnp.where(valid, s_gate[src], 0.0)
    row_dst = jnp.where(valid, g_slot, 2 * t + tile_e[:, None] * rows + jnp.clip(over, 0, rows - 1))
    n_used = (p_end[-1] // rows).astype(jnp.int32)
    tile_e = jnp.where(tile_start < p_end[-1], tile_e, tile_e[jnp.maximum(n_used - 1, 0)])
    return (row_tok[:, None, :], row_gate[:, None, :], row_dst[:, None, :], tile_e, n_used.reshape(1))


def _moe_experts_kernel(te_ref, nu_ref, tok_ref, tokn_ref, dst_ref, gate_ref, wg_ref, wu_ref, wd_ref, h_hbm,
                        y_hbm, hbuf, ybuf, wgb, wub, wdb, sem_g, sem_s):
    j = pl.program_id(0)
    n_used = nu_ref[0]
    slot = j % 2
    g = SUBLANES
    rows = hbuf.shape[0] // (2 * g)
    tile = rows * g

    def row(ref, i):
        return ref.at[pl.ds(pl.multiple_of(i * g, g), g)]

    def start_gather(idx_ref, s):
        for r in range(rows):
            pltpu.make_async_copy(row(h_hbm, idx_ref[0, 0, r]), row(hbuf, s * rows + r), sem_g.at[s]).start()

    def wait_gather(s):
        pltpu.make_async_copy(h_hbm.at[pl.ds(0, tile)], hbuf.at[pl.ds(pl.multiple_of(s * tile, tile), tile)],
                              sem_g.at[s]).wait()

    def wait_scatter(s):
        pltpu.make_async_copy(ybuf.at[pl.ds(pl.multiple_of(s * tile, tile), tile)], y_hbm.at[pl.ds(0, tile)],
                              sem_s.at[s]).wait()

    @pl.when(j == 0)
    def _():
        start_gather(tok_ref, 0)
        ybuf[...] = jnp.zeros(ybuf.shape, F32)
        n_real = y_hbm.shape[0] // g - N_EXPERTS * rows
        fills = [pltpu.make_async_copy(ybuf.at[pl.ds((k % 2) * tile, tile)],
                                       y_hbm.at[pl.ds((n_real + k * rows) * g, tile)], sem_s.at[k % 2])
                 for k in range(N_EXPERTS)]
        for f in fills:
            f.start()
        for f in fills[2:]:
            f.wait()

    active = j < n_used
    changed = jnp.logical_or(j == 0, te_ref[j] != te_ref[jnp.maximum(j - 1, 0)])

    @pl.when(jnp.logical_and(active, changed))
    def _():
        wgb[...] = wg_ref[0].astype(BF16)
        wub[...] = wu_ref[0].astype(BF16)
        wdb[...] = wd_ref[0].astype(BF16)

    @pl.when(active)
    def _():
        wait_gather(slot)
        start_gather(tokn_ref, 1 - slot)
        h = _load_row_tiles(hbuf, rows, g, first_row=slot * rows).astype(BF16)
        act = _silu(_dot(h, wgb[...])) * _dot(h, wub[...])
        ri = lax.broadcasted_iota(jnp.int32, (rows, rows), 0)
        ci = lax.broadcasted_iota(jnp.int32, (rows, rows), 1)
        gate = jnp.sum(jnp.where(ri == ci, gate_ref[0], 0.0), axis=1, keepdims=True)
        y = _dot((act * gate).astype(BF16), wdb[...])
        wait_scatter(slot)
        _store_row_tiles(ybuf, y, first_row=slot * rows)
        for r in range(rows):
            pltpu.make_async_copy(row(ybuf, slot * rows + r), row(y_hbm, dst_ref[0, 0, r]), sem_s.at[slot]).start()

        @pl.when(j == n_used - 1)
        def _():
            wait_scatter(slot)
            wait_scatter(1 - slot)
            wait_gather(1 - slot)


def _moe_combine_kernel(ya_ref, yb_ref, x_ref, mod_ref, o_ref):
    n, d = x_ref.shape
    y = _load_row_tiles(ya_ref, n, d // LANES) + _load_row_tiles(yb_ref, n, d // LANES)
    o_ref[...] = x_ref[...] + mod_ref[0][5:6] * y


def _moe(h, route, x, mods, w_gate, w_up, w_down, layer, tm, mod_row, route_tile):
    t, d = x.shape
    g = d // LANES
    assert g == SUBLANES
    rows = MOE_TILE
    row_tok, row_gate, row_dst, tile_e, n_used = _moe_plan(route, t, rows, route_tile)
    n_tiles = row_tok.shape[0]
    n_pair_rows = 2 * t + N_EXPERTS * rows
    smem_tile = lambda nxt: pl.BlockSpec((1, 1, rows), lambda j, te, nu: (jnp.minimum(j + nxt, nu[0] - 1), 0, 0),
                                         memory_space=pltpu.SMEM)
    pairs = pl.pallas_call(
        _moe_experts_kernel,
        grid_spec=pltpu.PrefetchScalarGridSpec(
            num_scalar_prefetch=2,
            grid=(n_tiles,),
            in_specs=[smem_tile(0), smem_tile(1), smem_tile(0),
                      pl.BlockSpec((1, 1, rows), lambda j, te, nu: (j, 0, 0)),
                      pl.BlockSpec((None, 1, d, D_EXPERT), lambda j, te, nu: (layer, te[j], 0, 0)),
                      pl.BlockSpec((None, 1, d, D_EXPERT), lambda j, te, nu: (layer, te[j], 0, 0)),
                      pl.BlockSpec((None, 1, D_EXPERT, d), lambda j, te, nu: (layer, te[j], 0, 0)),
                      pl.BlockSpec(memory_space=pl.ANY)],
            out_specs=pl.BlockSpec(memory_space=pl.ANY),
            scratch_shapes=[pltpu.VMEM((2 * rows * g, LANES), F32), pltpu.VMEM((2 * rows * g, LANES), F32),
                            pltpu.VMEM((d, D_EXPERT), BF16), pltpu.VMEM((d, D_EXPERT), BF16),
                            pltpu.VMEM((D_EXPERT, d), BF16),
                            pltpu.SemaphoreType.DMA((2,)), pltpu.SemaphoreType.DMA((2,))]),
        out_shape=jax.ShapeDtypeStruct((n_pair_rows * g, LANES), F32),
        compiler_params=_cparams(("arbitrary",)),
        name="moe_experts",
    )(tile_e, n_used, row_tok, row_tok, row_dst, row_gate, w_gate, w_up, w_down, h)
    return pl.pallas_call(
        _moe_combine_kernel,
        grid=(t // tm,),
        in_specs=[pl.BlockSpec((tm * g, LANES), lambda i: (i, 0)),
                  pl.BlockSpec((tm * g, LANES), lambda i: (t // tm + i, 0)),
                  pl.BlockSpec((tm, d), lambda i: (i, 0)),
                  pl.BlockSpec((1, 6, d), lambda i: (mod_row(i), 0, 0))],
        out_specs=pl.BlockSpec((tm, d), lambda i: (i, 0)),
        out_shape=jax.ShapeDtypeStruct((t, d), F32),
        compiler_params=_cparams(("parallel",)),
        name="moe_combine",
    )(pairs, pairs, x, mods)


def _chan_dft_kernel(z_ref, w_ref, o_ref):
    res = _dot(z_ref[...].astype(BF16), w_ref[...]).astype(BF16)
    o_ref[0] = res[:, :FOURIER_WIDTH]
    o_ref[1] = res[:, FOURIER_WIDTH:]


def _chan_dft(zf, w, batch, l_ctx, l_lat, tm):
    l_tot = l_ctx + l_lat
    nt = l_lat // tm
    fw = FOURIER_WIDTH
    return pl.pallas_call(
        _chan_dft_kernel,
        grid=(batch, nt),
        in_specs=[pl.BlockSpec((tm, fw), lambda b, i: (b * (l_tot // tm) + l_ctx // tm + i, 0)),
                  pl.BlockSpec(w.shape, lambda b, i: (0, 0))],
        out_specs=pl.BlockSpec((2, tm, fw), lambda b, i: (0, i, b)),
        out_shape=jax.ShapeDtypeStruct((2, l_lat, batch * fw), BF16),
        compiler_params=_cparams(("parallel", "parallel")),
        name="fourier_channels",
    )(zf, w)


def _seq_dft_kernel(ca_ref, sa_ref, cb_ref, sb_ref, z_ref, o_ref, acc_ref):
    k = pl.program_id(1)
    tk = z_ref.shape[1]
    sub = DFT_SPLIT
    n_a = ca_ref.shape[1]

    @pl.when(k == 0)
    def _():
        acc_ref[...] = jnp.zeros_like(acc_ref)

    col_a = k * tk + lax.broadcasted_iota(jnp.int32, (n_a, tk), 1)
    pick_a = (col_a // sub == lax.broadcasted_iota(jnp.int32, (n_a, tk), 0)).astype(F32)
    pick_b = (lax.broadcasted_iota(jnp.int32, (sub, tk), 1) % sub == lax.broadcasted_iota(jnp.int32, (sub, tk), 0)).astype(F32)
    ca, sa = _dot(ca_ref[...], pick_a), _dot(sa_ref[...], pick_a)
    cb, sb = _dot(cb_ref[...], pick_b), _dot(sb_ref[...], pick_b)
    cos_t = (ca * cb - sa * sb).astype(BF16)
    sin_t = (sa * cb + ca * sb).astype(BF16)
    acc_ref[...] += _dot(cos_t, z_ref[0]) + _dot(sin_t, z_ref[1])

    @pl.when(k == pl.num_programs(1) - 1)
    def _():
        o_ref[...] = acc_ref[...]


def _seq_dft(tables, zc, batch, tm, tk):
    ca, sa, cb, sb = tables
    l = ca.shape[0]
    fw = FOURIER_WIDTH
    small = lambda a: pl.BlockSpec((tm, a.shape[1]), lambda i, k: (i, 0))
    return pl.pallas_call(
        _seq_dft_kernel,
        grid=(l // tm, l // tk),
        in_specs=[small(ca), small(sa), small(cb), small(sb),
                  pl.BlockSpec((2, tk, batch * fw), lambda i, k: (0, k, 0))],
        out_specs=pl.BlockSpec((tm, batch * fw), lambda i, k: (i, 0)),
        out_shape=jax.ShapeDtypeStruct((l, batch * fw), F32),
        scratch_shapes=[pltpu.VMEM((tm, batch * fw), F32)],
        compiler_params=_cparams(("parallel", "arbitrary")),
        name="fourier_sequence",
    )(ca, sa, cb, sb, zc)


def _dft_tables(l):
    m = jnp.arange(l, dtype=jnp.int32)[:, None]
    n1 = l // DFT_SPLIT
    a = (m * (jnp.arange(n1, dtype=jnp.int32)[None, :] * DFT_SPLIT)) % l
    b = (m * jnp.arange(DFT_SPLIT, dtype=jnp.int32)[None, :]) % l
    wa = a.astype(F32) * (2.0 * np.pi / l)
    wb = b.astype(F32) * (2.0 * np.pi / l)
    tables = (jnp.cos(wa), jnp.sin(wa), jnp.cos(wb), jnp.sin(wb))
    gd = FOURIER_GROUP_DIM
    cc = (jnp.arange(gd, dtype=jnp.int32)[:, None] * jnp.arange(gd, dtype=jnp.int32)[None, :]) % gd
    wc = cc.astype(F32) * (2.0 * np.pi / gd)
    scale = 1.0 / np.sqrt(float(l) * gd)
    eye = jnp.eye(FOURIER_WIDTH // gd, dtype=F32)
    chan = jnp.concatenate([jnp.kron(eye, jnp.cos(wc)), -jnp.kron(eye, jnp.sin(wc))], axis=1) * scale
    return tables, chan.astype(BF16)


def _rwkv_prepare(d, pos, z_ref, zp_ref, zn_ref, par, nc_ctx, nc_tot):
    mu, kks, ka, rk, w0, w2, a0, a2, bd = par
    c = z_ref.shape[0]
    n = RWKV_DIM
    seg_first = jnp.logical_or(pos == 0, pos == nc_ctx)
    seg_last = jnp.logical_or(pos == nc_ctx - 1, pos == nc_tot - 1)
    z = z_ref[...]
    row = lax.broadcasted_iota(jnp.int32, z.shape, 0)
    prev_row = jnp.where(seg_first, 0.0, zp_ref[7:8, :])
    next_row = jnp.where(seg_last, 0.0, zn_ref[0:1, :])
    z_prev = jnp.where(row == 0, prev_row, pltpu.roll(z, 1, 0))
    z_next = jnp.where(row == c - 1, next_row, pltpu.roll(z, c - 1, 0))
    zs = z + mu * (0.5 * (z_prev + z_next) - z)

    r, k, v = zs[:, 0:n], zs[:, n:2 * n], zs[:, 2 * n:3 * n]
    zw = zs[:, 3 * n:3 * n + RWKV_RANK_PAD]
    za = zs[:, 3 * n + RWKV_RANK_PAD:3 * n + 2 * RWKV_RANK_PAD]
    zg = zs[:, 3 * n + 2 * RWKV_RANK_PAD:3 * n + 3 * RWKV_RANK_PAD]

    kk = k * kks
    kk = kk * lax.rsqrt(_dot(kk * kk, bd) + L2_EPS)
    w_log = _log_sigmoid(w0[d] + _dot(jnp.tanh(zw), w2[d])) - 0.5
    lw = -jnp.exp(w_log)
    a = jax.nn.sigmoid(a0[d] + _dot(za, a2[d]))
    kd = k * (1.0 + (a - 1.0) * ka)
    beta = kk * a

    cl = _dot((_visit_order(d, c) >= 0).astype(F32), lw)
    c_tot = jnp.sum(lw, axis=0, keepdims=True)
    grow = jnp.exp(-cl)
    tail = jnp.exp(c_tot - cl)
    ops = dict(k_s=kd * grow,
               b_s=beta * grow,
               kap_s=kk * jnp.exp(cl - lw),
               r_s=r * jnp.exp(cl),
               k_e=kd * tail,
               b_e=beta * tail,
               gam=jnp.exp(c_tot), v=v,
               bonus=_dot(r * kd * rk, bd),
               sign=1 if d == 0 else -1)
    return ops, zg


def _rwkv_kernel(zf_ref, zfp_ref, zfn_ref, zb_ref, zbp_ref, zbn_ref, mu_ref, kks_ref, ka_ref, rk_ref, w0_ref, w2_ref,
                 a0_ref, a2_ref, g2_ref, bd_ref, yf_ref, yb_ref, gate_ref, st_ref, *, nc_ctx, nc_tot):
    s = pl.program_id(1)
    c = zf_ref.shape[0]
    hd = HEAD_DIM
    pw = 2 * hd

    @pl.when(s == 0)
    def _():
        st_ref[...] = jnp.zeros_like(st_ref)

    par = (mu_ref[...], kks_ref[...], ka_ref[...], rk_ref[...], w0_ref, w2_ref, a0_ref, a2_ref, bd_ref[...])
    fwd, zg = _rwkv_prepare(0, _chunk_pos(s, 0, nc_ctx, nc_tot), zf_ref, zfp_ref, zfn_ref, par, nc_ctx, nc_tot)
    bwd, _ = _rwkv_prepare(1, _chunk_pos(s, 1, nc_ctx, nc_tot), zb_ref, zbp_ref, zbn_ref, par, nc_ctx, nc_tot)
    gate_ref[...] = _dot(jax.nn.sigmoid(zg), g2_ref[...])

    lane = lax.broadcasted_iota(jnp.int32, (c, pw), 1)
    rowi = lax.broadcasted_iota(jnp.int32, (c, pw), 0)
    left = lane < hd
    eye_p = (rowi == lane % hd).astype(F32)
    same_head = (lax.broadcasted_iota(jnp.int32, (pw, pw), 0) < hd) == (lax.broadcasted_iota(jnp.int32, (pw, pw), 1) < hd)

    def bd(y):
        return jnp.concatenate([jnp.where(left, y, 0.0), jnp.where(left, 0.0, y)], axis=0)

    n_pairs = RWKV_HEADS // 2
    prob = [(o, slice(i * pw, (i + 1) * pw)) for o in (fwd, bwd) for i in range(n_pairs)]
    ahead = [(rowi - lane % hd) * o['sign'] for o, _ in prob]
    get = lambda name: [o[name][:, s_] for o, s_ in prob]
    kap, r_s, k_s, b_s, k_e, b_e, vp, gam, bonus = (get(x) for x in ('kap_s', 'r_s', 'k_s', 'b_s', 'k_e', 'b_e', 'v', 'gam', 'bonus'))
    p = [_dot_nt(jnp.concatenate([a_, b_], axis=0), jnp.concatenate([bd(c_), bd(d_)], axis=0))
         for a_, b_, c_, d_ in zip(kap, r_s, k_s, b_s)]
    m1 = [jnp.where(h_ > 0, x[0:c, 0:pw], 0.0) for x, h_ in zip(p, ahead)]
    m2 = [jnp.where(h_ > 0, x[0:c, pw:2 * pw], 0.0) for x, h_ in zip(p, ahead)]
    n1 = [jnp.where(h_ >= 0, x[c:2 * c, 0:pw], 0.0) for x, h_ in zip(p, ahead)]
    n2 = [jnp.where(h_ >= 0, x[c:2 * c, pw:2 * pw], 0.0) for x, h_ in zip(p, ahead)]
    m1v = [_dot(a_, bd(b_)) for a_, b_ in zip(m1, vp)]
    t_inv = [eye_p - x for x in m2]
    q = [_dot(x, bd(x)) for x in m2]
    span = 2
    while 2 * span < c:
        both = [_dot(jnp.concatenate([t_, q_], axis=0), bd(q_)) for t_, q_ in zip(t_inv, q)]
        t_inv = [t_ + x[0:c] for t_, x in zip(t_inv, both)]
        q = [x[c:2 * c] for x in both]
        span *= 2
    t_inv = [t_ + _dot(t_, bd(q_)) for t_, q_ in zip(t_inv, q)]
    tx = [_dot(t_, jnp.concatenate([bd(a_), bd(mv)], axis=1)) for t_, a_, mv in zip(t_inv, kap, m1v)]
    st = [st_ref[i] for i in range(len(prob))]
    su = [_dot_nt(jnp.concatenate([x[:, 0:pw], r_], axis=0), s0) for x, r_, s0 in zip(tx, r_s, st)]
    u = [x[0:c] + y_[:, pw:2 * pw] for x, y_ in zip(su, tx)]
    ys = [x[c:2 * c] + _dot(jnp.concatenate([a_, -b_], axis=1), jnp.concatenate([bd(v_), bd(u_)], axis=0)) + bo * v_
          for x, a_, b_, v_, u_, bo in zip(su, n1, n2, vp, u, bonus)]
    for i in range(len(prob)):
        upd = _dot_tn(jnp.concatenate([vp[i], u[i]], axis=0), jnp.concatenate([k_e[i], -b_e[i]], axis=0))
        st_ref[i] = st[i] * gam[i] + jnp.where(same_head, upd, 0.0)
    yf_ref[...] = jnp.concatenate(ys[:n_pairs], axis=1)
    yb_ref[...] = jnp.concatenate(ys[n_pairs:], axis=1)


def _rwkv(zr, mu, kk_scale, k_a, r_k, w0, w2_pad, a0, a2_pad, g2, batch, nc_ctx, nc_tot):
    t, zw_ = zr.shape
    c = SEQ_CHUNK
    n = RWKV_DIM
    bd = _block_ones(RWKV_HEADS, HEAD_DIM)
    full = lambda a: pl.BlockSpec(a.shape, lambda b, s: (0,) * a.ndim)
    sub = c // 8
    n_sub = t // 8

    def rows(d):
        return lambda b, s: b * nc_tot + _chunk_pos(s, d, nc_ctx, nc_tot)

    def z_specs(d):
        rw = rows(d)
        return [pl.BlockSpec((c, zw_), lambda b, s: (rw(b, s), 0)),
                pl.BlockSpec((8, zw_), lambda b, s: (jnp.maximum(rw(b, s) * sub - 1, 0), 0)),
                pl.BlockSpec((8, zw_), lambda b, s: (jnp.minimum((rw(b, s) + 1) * sub, n_sub - 1), 0))]

    vec = lambda a: a.reshape(1, -1)
    args = (vec(mu), vec(kk_scale), vec(k_a), vec(r_k), w0.reshape(2, 1, n), w2_pad, a0.reshape(2, 1, n), a2_pad, g2, bd)
    out = lambda d: pl.BlockSpec((c, n), lambda b, s: (rows(d)(b, s), 0))
    return pl.pallas_call(
        functools.partial(_rwkv_kernel, nc_ctx=nc_ctx, nc_tot=nc_tot),
        grid=(batch, nc_tot),
        in_specs=z_specs(0) + z_specs(1) + [full(a) for a in args],
        out_specs=[out(0), out(1), out(0)],
        out_shape=[jax.ShapeDtypeStruct((t, n), F32)] * 3,
        scratch_shapes=[pltpu.VMEM((RWKV_HEADS, 2 * HEAD_DIM, 2 * HEAD_DIM), F32)],
        compiler_params=_cparams(("parallel", "arbitrary")),
        name="rwkv_scan",
    )(zr, zr, zr, zr, zr, zr, *args)


def _merge_odd_kernel(y0_ref, y1_ref, gate_ref, fo_ref, x_ref, mod_ref, lg_ref, lb_ref, bd_ref, w1_ref, w2_ref,
                      fg_ref, rw_ref, rb_ref, xo_ref, ho_ref, go_ref):
    y = y0_ref[...] + y1_ref[...]
    bd = bd_ref[...]
    mean = _dot(y, bd) * (1.0 / HEAD_DIM)
    yc = y - mean
    var = _dot(yc * yc, bd) * (1.0 / HEAD_DIM)
    rw = (yc * lax.rsqrt(var + RWKV_GN_EPS) * lg_ref[...] + lb_ref[...]) * gate_ref[...]
    out = _dot(fo_ref[...].astype(BF16), w1_ref[...]) + _dot(rw.astype(BF16), w2_ref[...])
    _residual_and_route(x_ref[...], out, mod_ref[0], fg_ref[...], rw_ref, rb_ref, xo_ref, ho_ref, go_ref)


def _merge_odd(y0, y1, gate, fo, x, mods, ln_g, ln_b, w_out, ffn_gain, router_wt, router_b, tm, batch, l_ctx, l_lat):
    d = x.shape[1]
    n = RWKV_DIM
    fw = FOURIER_WIDTH
    l_tot = l_ctx + l_lat
    nt = l_lat // tm
    t_out = batch * l_lat
    full = lambda a: pl.BlockSpec(a.shape, lambda b, i: (0,) * a.ndim)
    src = lambda b, i: b * (l_tot // tm) + l_ctx // tm + i
    bd = _block_ones(RWKV_HEADS, HEAD_DIM)
    w1, w2 = w_out[:fw].astype(BF16), w_out[fw:].astype(BF16)
    lg, lb, fg, rb = ln_g.reshape(1, n), ln_b.reshape(1, n), ffn_gain.reshape(1, d), router_b.reshape(N_EXPERTS, 1)
    return pl.pallas_call(
        _merge_odd_kernel,
        grid=(batch, nt),
        in_specs=[pl.BlockSpec((tm, n), lambda b, i: (src(b, i), 0)),
                  pl.BlockSpec((tm, n), lambda b, i: (src(b, i), 0)),
                  pl.BlockSpec((tm, n), lambda b, i: (src(b, i), 0)),
                  pl.BlockSpec((tm, fw), lambda b, i: (i, b)),
                  pl.BlockSpec((tm, d), lambda b, i: (src(b, i), 0)),
                  pl.BlockSpec((1, 6, d), lambda b, i: (b, 0, 0)),
                  full(lg), full(lb), full(bd), full(w1), full(w2), full(fg), full(router_wt), full(rb)],
        out_specs=[pl.BlockSpec((tm, d), lambda b, i: (b * nt + i, 0)),
                   pl.BlockSpec((tm * (d // LANES), LANES), lambda b, i: (b * nt + i, 0)),
                   pl.BlockSpec((8, tm), lambda b, i: (0, b * nt + i))],
        out_shape=[jax.ShapeDtypeStruct((t_out, d), F32),
                   jax.ShapeDtypeStruct((t_out * (d // LANES), LANES), F32),
                   jax.ShapeDtypeStruct((8, t_out), F32)],
        compiler_params=_cparams(("parallel", "parallel")),
        name="merge_odd",
    )(y0, y1, gate, fo, x, mods, lg, lb, bd, w1, w2, fg, router_wt, rb)


def _rope_tables(l_ctx, l_lat):
    rows = l_lat // GRID_W
    row = jnp.repeat(jnp.arange(rows, dtype=F32), GRID_W)
    col = jnp.tile(jnp.arange(GRID_W, dtype=F32), rows)
    n_freq = HEAD_DIM // 4
    inv_freq = ROPE_THETA ** (-jnp.arange(n_freq, dtype=F32) / n_freq)
    ang = jnp.concatenate([row[:, None] * inv_freq, col[:, None] * inv_freq], axis=-1)
    cos, sin = jnp.cos(ang), jnp.sin(ang)
    cos64 = jnp.concatenate([cos, cos], axis=1)
    sin64 = jnp.concatenate([-sin, sin], axis=1)
    cos64 = jnp.concatenate([jnp.ones((l_ctx, HEAD_DIM), F32), cos64], axis=0)
    sin64 = jnp.concatenate([jnp.zeros((l_ctx, HEAD_DIM), F32), sin64], axis=0)
    return cos64, sin64


def _pad_rank(w):
    _, r, n = w.shape
    out = jnp.zeros((2, RWKV_RANK_PAD, n), w.dtype)
    out = out.at[0, 0:r].set(w[0])
    return out.at[1, r:2 * r].set(w[1])


def _even_layer(x, mods, p, batch, l_ctx, l_lat, tm, mod_row, tm_moe):
    l_tot = l_ctx + l_lat
    d = x.shape[1]
    nc_ctx, nc_tot = l_ctx // SEQ_CHUNK, l_tot // SEQ_CHUNK
    hk, hv = GLA_HEADS * GLA_DK, GLA_HEADS * GLA_DV
    qw, kw = ATT_HEADS * HEAD_DIM, ATT_KV_HEADS * HEAD_DIM
    w_in = p['w_in']
    o = np.cumsum([0, hk, hk, hv, hv, 2 * GLA_LOWRANK, qw, kw, kw])
    w_gla = jnp.concatenate([w_in[:, o[0]:o[4]]], axis=1).astype(BF16)
    w_dec = jnp.pad(w_in[:, o[4]:o[5]], ((0, 0), (0, 128 - 2 * GLA_LOWRANK))).astype(BF16)
    w_q = w_in[:, o[5]:o[6]].astype(BF16)
    w_kv = w_in[:, o[6]:o[8]].astype(BF16)
    zg, zdec, zq, zkv = _project(x, mods, p['norm_mix'], [w_gla, w_dec, w_q, w_kv], tm, mod_row)

    dec_w_pad = _pad_rank(p['dec_w'])
    o_f, o_b = _gla(zg, zdec, dec_w_pad, p['dec_b'].reshape(2, 1, hk), batch, nc_ctx, nc_tot)

    cos64, sin64 = _rope_tables(l_ctx, l_lat)
    qn, kn, vn = _qk_prep(zq, zkv, cos64, sin64, p['q_norm'], p['k_norm'], tm, l_tot)
    o_att = _attention(qn, kn, vn, p['sink'], batch, l_ctx, l_tot)

    x1, h, gates = _merge_even(o_f, o_b, zg, o_att, x, mods, p['out_norm'], p['w_out'], p['norm_ffn'],
                               p['router_wt'], p['router_b'], tm, mod_row)
    return _moe(h, gates, x1, mods, p['moe_g'], p['moe_u'], p['moe_d'], p['moe_layer'], tm, mod_row, tm)


def _odd_layer(x, mods, p, batch, l_ctx, l_lat, tm, mod_row, tm_moe):
    l_tot = l_ctx + l_lat
    nc_ctx, nc_tot = l_ctx // SEQ_CHUNK, l_tot // SEQ_CHUNK
    n = RWKV_DIM
    fw = FOURIER_WIDTH
    w_in = p['w_in']
    rank_w, rank_a = p['w2'].shape[1], p['a2'].shape[1]
    o = np.cumsum([0, fw, n, n, n, 2 * rank_w, 2 * rank_a])
    pad_cols = lambda w: jnp.pad(w, ((0, 0), (0, RWKV_RANK_PAD - w.shape[1])))
    w_f = w_in[:, o[0]:o[1]].astype(BF16)
    w_r = jnp.concatenate([w_in[:, o[1]:o[4]], pad_cols(w_in[:, o[4]:o[5]]), pad_cols(w_in[:, o[5]:o[6]]),
                           w_in[:, o[6]:]], axis=1).astype(BF16)
    zf, zr = _project(x, mods, p['norm_mix'], [w_f, w_r], tm, mod_row)

    mu = p['mu']
    mu_r = jnp.concatenate([mu[0:3 * n], pad_cols(mu[None, 3 * n:3 * n + 2 * rank_w])[0],
                            pad_cols(mu[None, 3 * n + 2 * rank_w:3 * n + 2 * rank_w + 2 * rank_a])[0],
                            mu[3 * n + 2 * rank_w + 2 * rank_a:]])
    y0, y1, gate = _rwkv(zr, mu_r, p['kk_scale'], p['k_a'], p['r_k'].reshape(-1), p['w0'], _pad_rank(p['w2']),
                    p['a0'], _pad_rank(p['a2']), p['g2'], batch, nc_ctx, nc_tot)

    tables, chan = _dft_tables(l_lat)
    zc = _chan_dft(zf, chan, batch, l_ctx, l_lat, tm)
    fo = _seq_dft(tables, zc, batch, min(512, l_lat), min(1024, l_lat))

    x1, h, gates = _merge_odd(y0, y1, gate, fo, x, mods, p['ln_g'], p['ln_b'], p['w_out'], p['norm_ffn'],
                              p['router_wt'], p['router_b'], tm, batch, l_ctx, l_lat)
    lat_tiles = l_lat // tm_moe
    return _moe(h, gates, x1, mods, p['moe_g'], p['moe_u'], p['moe_d'], p['moe_layer'], tm_moe,
                lambda i: i // lat_tiles, tm)


def kernel(x, c, ctx, c_ctx, ada_w, ada_b, norm_mix, norm_ffn, even_w_in, even_w_out, gla_dec_w, gla_dec_b, gla_out_norm, att_q_norm, att_k_norm, att_sink, odd_w_in, odd_w_out, rwkv_mu, rwkv_w0, rwkv_w2, rwkv_a0, rwkv_a2, rwkv_g2, rwkv_kk_scale, rwkv_k_a, rwkv_r_k, rwkv_ln_g, rwkv_ln_b, router_w, router_b, moe_w_gate, moe_w_up, moe_w_down):
    batch, l_lat, d = x.shape
    l_ctx = ctx.shape[1]
    l_tot = l_ctx + l_lat
    assert batch < 8 and ada_w.shape[0] == 2
    tm = 256 if (l_ctx % 256 == 0 and l_lat % 256 == 0) else 128
    tm_moe = 512 if (l_lat % 512 == 0 and tm == 256) else tm
    assert l_ctx % tm == 0 and l_lat % tm == 0 and l_tot % l_ctx == 0 and l_lat % GRID_W == 0

    xs = jnp.concatenate([ctx, x], axis=1).reshape(batch * l_tot, d)
    cc = jnp.concatenate([c, c_ctx[None, :], jnp.zeros((8 - batch - 1, d), F32)], axis=0)
    tiles_per_b = l_tot // tm
    ctx_tiles = l_ctx // tm

    def mod_row(i):
        return jnp.where(i % tiles_per_b < ctx_tiles, batch, i // tiles_per_b)

    router_wt = router_w.T
    moe = lambda layer: dict(moe_g=moe_w_gate, moe_u=moe_w_up, moe_d=moe_w_down, moe_layer=layer)

    mods_all = _modvec(cc, ada_w, ada_b)
    mods0 = mods_all[0]
    p0 = dict(w_in=even_w_in[0], w_out=even_w_out[0], dec_w=gla_dec_w[0], dec_b=gla_dec_b[0],
              out_norm=gla_out_norm[0], q_norm=att_q_norm[0], k_norm=att_k_norm[0], sink=att_sink[0],
              norm_mix=norm_mix[0], norm_ffn=norm_ffn[0], router_wt=router_wt, router_b=router_b, **moe(0))
    xs = _even_layer(xs, mods0, p0, batch, l_ctx, l_lat, tm, mod_row, tm_moe)

    mods1 = mods_all[1]
    p1 = dict(w_in=odd_w_in[0], w_out=odd_w_out[0], mu=rwkv_mu[0], w0=rwkv_w0[0], w2=rwkv_w2[0], a0=rwkv_a0[0],
              a2=rwkv_a2[0], g2=rwkv_g2[0], kk_scale=rwkv_kk_scale[0], k_a=rwkv_k_a[0], r_k=rwkv_r_k[0],
              ln_g=rwkv_ln_g[0], ln_b=rwkv_ln_b[0], norm_mix=norm_mix[1], norm_ffn=norm_ffn[1],
              router_wt=router_wt, router_b=router_b, **moe(1))
    out = _odd_layer(xs, mods1, p1, batch, l_ctx, l_lat, tm, mod_row, tm_moe)
    return out.reshape(batch, l_lat, d)
```

```python
import functools

import jax
import jax.numpy as jnp
import numpy as np
from jax import lax
from jax.experimental import pallas as pl
from jax.experimental.pallas import tpu as pltpu

F32 = jnp.float32
BF16 = jnp.bfloat16

GRID_W = 64
HEAD_DIM = 64
NORM_EPS = 1e-6
L2_EPS = 1e-12

GLA_DV = 64
GLA_DK = 32
GLA_HEADS = 8
GLA_LOWRANK = 16
GLA_TAU = 16.0

ATT_HEADS = 8
ATT_KV_HEADS = 2
ATT_GROUP = ATT_HEADS // ATT_KV_HEADS
ATT_BLOCK = 128
ROPE_THETA = 10000.0

FOURIER_GROUP_DIM = 64
FOURIER_WIDTH = 256
DFT_SPLIT = 64

RWKV_DIM = 768
RWKV_HEADS = 12
RWKV_RANK_PAD = 128
RWKV_GN_EPS = 64e-5

N_EXPERTS = 16
N_GROUPS = 4
PER_GROUP = N_EXPERTS // N_GROUPS
D_EXPERT = 512
MOE_TILE = 256
LANES = 128
SUBLANES = 8

SEQ_CHUNK = 64
VMEM_LIMIT = 56 * 1024 * 1024


def _cparams(sem):
    return pltpu.CompilerParams(dimension_semantics=sem, vmem_limit_bytes=VMEM_LIMIT)


def _dot(a, b):
    return jnp.dot(a, b, preferred_element_type=F32)


def _dot_nt(a, b):
    return lax.dot_general(a, b, (((1,), (1,)), ((), ())), preferred_element_type=F32)


def _dot_tn(a, b):
    return lax.dot_general(a, b, (((0,), (0,)), ((), ())), preferred_element_type=F32)


def _silu(x):
    return x * jax.nn.sigmoid(x)


def _log_sigmoid(x):
    return jnp.minimum(x, 0.0) - jnp.log(1.0 + jnp.exp(-jnp.abs(x)))


def _modulated_norm(x, gain, shift, scale):
    ms = jnp.mean(x * x, axis=-1, keepdims=True)
    return (x * lax.rsqrt(ms + NORM_EPS) * gain) * (1.0 + scale) + shift


def _block_ones(n_blocks, width):
    return jnp.kron(jnp.eye(n_blocks, dtype=F32), jnp.ones((width, width), F32))


def _store_row_tiles(ref, x, first_row=0):
    n, w = x.shape
    g = w // LANES
    for k in range(g):
        ref[pl.ds(first_row * g + k, n, stride=g), :] = x[:, k * LANES:(k + 1) * LANES]


def _load_row_tiles(ref, n, g, first_row=0):
    return jnp.concatenate([ref[pl.ds(first_row * g + k, n, stride=g), :] for k in range(g)], axis=1)


def _modvec_kernel(c_ref, w_ref, b_ref, o_ref):
    o_ref[...] = _dot(_silu(c_ref[...]), w_ref[...]) + b_ref[...]


def _modvec(cc, w, b):
    d = cc.shape[1]
    layers, _, n = w.shape
    tn = n // 4
    out = pl.pallas_call(
        _modvec_kernel,
        grid=(layers, n // tn),
        in_specs=[pl.BlockSpec((8, d), lambda l, j: (0, 0)),
                  pl.BlockSpec((None, d, tn), lambda l, j: (l, 0, j)),
                  pl.BlockSpec((None, 1, tn), lambda l, j: (l, 0, j))],
        out_specs=pl.BlockSpec((None, 8, tn), lambda l, j: (l, 0, j)),
        out_shape=jax.ShapeDtypeStruct((layers, 8, n), F32),
        compiler_params=_cparams(("parallel", "parallel")),
        name="modvec",
    )(cc, w, b.reshape(layers, 1, n))
    return out.reshape(layers, 8, 6, d)


def _proj_kernel(x_ref, mod_ref, gain_ref, *refs, n_out):
    w_refs, z_refs = refs[:n_out], refs[n_out:]
    m = mod_ref[0]
    h = _modulated_norm(x_ref[...], gain_ref[...], m[0:1], m[1:2]).astype(BF16)
    for w_ref, z_ref in zip(w_refs, z_refs):
        z_ref[...] = _dot(h, w_ref[...])


def _project(x, mods, gain, weights, tm, mod_row):
    t, d = x.shape
    n_out = len(weights)
    in_specs = [pl.BlockSpec((tm, d), lambda i: (i, 0)),
                pl.BlockSpec((1, 6, d), lambda i: (mod_row(i), 0, 0)),
                pl.BlockSpec((1, d), lambda i: (0, 0))]
    in_specs += [pl.BlockSpec(w.shape, lambda i: (0, 0)) for w in weights]
    return pl.pallas_call(
        functools.partial(_proj_kernel, n_out=n_out),
        grid=(t // tm,),
        in_specs=in_specs,
        out_specs=[pl.BlockSpec((tm, w.shape[1]), lambda i: (i, 0)) for w in weights],
        out_shape=[jax.ShapeDtypeStruct((t, w.shape[1]), F32) for w in weights],
        compiler_params=_cparams(("parallel",)),
        name="proj",
    )(x, mods, gain.reshape(1, d), *weights)


def _chunk_pos(s, d, nc_ctx, nc_tot):
    back = jnp.where(s < nc_ctx, nc_ctx - 1 - s, nc_tot + nc_ctx - 1 - s)
    return jnp.where(d == 0, s, back)


def _visit_order(d, c):
    sign = 1 if d == 0 else -1
    return (lax.broadcasted_iota(jnp.int32, (c, c), 0) - lax.broadcasted_iota(jnp.int32, (c, c), 1)) * sign


def _gla_prepare(d, q_ref, k_ref, v_ref, dec_ref, dw_ref, db_ref):
    c = q_ref.shape[0]
    g = _log_sigmoid(_dot(dec_ref[...], dw_ref[d]) + db_ref[d]) / GLA_TAU
    b = _dot((_visit_order(d, c) >= 0).astype(F32), g)
    b_tot = jnp.sum(g, axis=0, keepdims=True)
    k = k_ref[...]
    return dict(q_in=q_ref[...] * (GLA_DK ** -0.5) * jnp.exp(b), k_out=k * jnp.exp(-b), k_end=k * jnp.exp(b_tot - b),
                decay=jnp.exp(b_tot), v=v_ref[...], sign=1 if d == 0 else -1)


def _gla_kernel(qf_ref, kf_ref, vf_ref, df_ref, qb_ref, kb_ref, vb_ref, db_ref, dw_ref, dbias_ref, of_ref, ob_ref, st_ref):
    c = qf_ref.shape[0]
    group = 4
    kw, vw = group * GLA_DK, group * GLA_DV

    @pl.when(pl.program_id(1) == 0)
    def _():
        st_ref[...] = jnp.zeros_like(st_ref)

    fwd = _gla_prepare(0, qf_ref, kf_ref, vf_ref, df_ref, dw_ref, dbias_ref)
    bwd = _gla_prepare(1, qb_ref, kb_ref, vb_ref, db_ref, dw_ref, dbias_ref)
    n_quads = GLA_HEADS // group
    prob = [(o, i) for o in (fwd, bwd) for i in range(n_quads)]
    klane = lax.broadcasted_iota(jnp.int32, (c, kw), 1) // GLA_DK
    half = lax.broadcasted_iota(jnp.int32, (c, 2 * GLA_DV), 1) < GLA_DV
    rowi = lax.broadcasted_iota(jnp.int32, (c, group * c), 0)
    coli = lax.broadcasted_iota(jnp.int32, (c, group * c), 1) % c
    own = (lax.broadcasted_iota(jnp.int32, (vw, kw), 0) // GLA_DV) == (lax.broadcasted_iota(jnp.int32, (vw, kw), 1) // GLA_DK)

    def bd_keys(y):
        return jnp.concatenate([jnp.where(klane == h, y, 0.0) for h in range(group)], axis=0)

    def bd_vals(y):
        return jnp.concatenate([jnp.where(half, y, 0.0), jnp.where(half, 0.0, y)], axis=0)

    ksl = lambda i: slice(i * kw, (i + 1) * kw)
    vsl = lambda i: slice(i * vw, (i + 1) * vw)
    q_in = [o['q_in'][:, ksl(i)] for o, i in prob]
    att = [jnp.where((rowi - coli) * o['sign'] >= 0, _dot_nt(q_, bd_keys(o['k_out'][:, ksl(i)])), 0.0)
           for q_, (o, i) in zip(q_in, prob)]
    st = [st_ref[j] for j in range(len(prob))]
    outs = []
    for j, (o, i) in enumerate(prob):
        v = o['v'][:, vsl(i)]
        intra = jnp.concatenate([_dot(att[j][:, p * 2 * c:(p + 1) * 2 * c], bd_vals(v[:, p * 2 * GLA_DV:(p + 1) * 2 * GLA_DV]))
                                 for p in range(group // 2)], axis=1)
        outs.append(intra + _dot_nt(q_in[j], st[j]))
    for j, (o, i) in enumerate(prob):
        upd = _dot_tn(o['v'][:, vsl(i)], o['k_end'][:, ksl(i)])
        st_ref[j] = st[j] * o['decay'][:, ksl(i)] + jnp.where(own, upd, 0.0)
    of_ref[...] = jnp.concatenate(outs[:n_quads], axis=1)
    ob_ref[...] = jnp.concatenate(outs[n_quads:], axis=1)


def _gla(zg, zdec, dec_w_pad, dec_b, batch, nc_ctx, nc_tot):
    t = zg.shape[0]
    c = SEQ_CHUNK
    hk, hv = GLA_HEADS * GLA_DK, GLA_HEADS * GLA_DV

    def specs(d):
        rw = lambda b, s: b * nc_tot + _chunk_pos(s, d, nc_ctx, nc_tot)
        return [pl.BlockSpec((c, hk), lambda b, s: (rw(b, s), 0)),
                pl.BlockSpec((c, hk), lambda b, s: (rw(b, s), 1)),
                pl.BlockSpec((c, hv), lambda b, s: (rw(b, s), 1)),
                pl.BlockSpec((c, 128), lambda b, s: (rw(b, s), 0))], pl.BlockSpec((c, hv), lambda b, s: (rw(b, s), 0))

    in_f, out_f = specs(0)
    in_b, out_b = specs(1)
    return pl.pallas_call(
        _gla_kernel,
        grid=(batch, nc_tot),
        in_specs=in_f + in_b + [pl.BlockSpec(dec_w_pad.shape, lambda b, s: (0, 0, 0)),
                                pl.BlockSpec(dec_b.shape, lambda b, s: (0, 0, 0))],
        out_specs=[out_f, out_b],
        out_shape=[jax.ShapeDtypeStruct((t, hv), F32)] * 2,
        scratch_shapes=[pltpu.VMEM((2 * GLA_HEADS // 4, 4 * GLA_DV, 4 * GLA_DK), F32)],
        compiler_params=_cparams(("parallel", "arbitrary")),
        name="gla_scan",
    )(zg, zg, zg, zdec, zg, zg, zg, zdec, dec_w_pad, dec_b)


def _rope_swap(x):
    n = x.shape[-1]
    lane = lax.broadcasted_iota(jnp.int32, x.shape, x.ndim - 1)
    half = HEAD_DIM // 2
    return jnp.where(lane % HEAD_DIM < half, pltpu.roll(x, n - half, x.ndim - 1), pltpu.roll(x, half, x.ndim - 1))


def _qk_prep_kernel(q_ref, kv_ref, cos_ref, sin_ref, qg_ref, kg_ref, bdq_ref, bdk_ref, qo_ref, ko_ref, vo_ref):
    def norm_rope(x, gain, bd, n_heads):
        ms = _dot(x * x, bd) * (1.0 / HEAD_DIM)
        xn = x * lax.rsqrt(ms + NORM_EPS) * gain
        cos = jnp.concatenate([cos_ref[...]] * n_heads, axis=1)
        sin = jnp.concatenate([sin_ref[...]] * n_heads, axis=1)
        return xn * cos + _rope_swap(xn) * sin

    q = norm_rope(q_ref[...], qg_ref[...], bdq_ref[...], ATT_HEADS)
    qo_ref[...] = (q * (HEAD_DIM ** -0.5)).astype(BF16)
    kw = ATT_KV_HEADS * HEAD_DIM
    kv = kv_ref[...]
    ko_ref[...] = norm_rope(kv[:, :kw], kg_ref[...], bdk_ref[...], ATT_KV_HEADS).astype(BF16)
    vo_ref[...] = kv[:, kw:].astype(BF16)


def _qk_prep(zq, zkv, cos64, sin64, q_gain, k_gain, tm, l_tot):
    t = zq.shape[0]
    qw, kw = ATT_HEADS * HEAD_DIM, ATT_KV_HEADS * HEAD_DIM
    n_pos = l_tot // tm
    return pl.pallas_call(
        _qk_prep_kernel,
        grid=(t // tm,),
        in_specs=[pl.BlockSpec((tm, qw), lambda i: (i, 0)),
                  pl.BlockSpec((tm, 2 * kw), lambda i: (i, 0)),
                  pl.BlockSpec((tm, HEAD_DIM), lambda i: (i % n_pos, 0)),
                  pl.BlockSpec((tm, HEAD_DIM), lambda i: (i % n_pos, 0)),
                  pl.BlockSpec((1, qw), lambda i: (0, 0)),
                  pl.BlockSpec((1, kw), lambda i: (0, 0)),
                  pl.BlockSpec((qw, qw), lambda i: (0, 0)),
                  pl.BlockSpec((kw, kw), lambda i: (0, 0))],
        out_specs=[pl.BlockSpec((tm, qw), lambda i: (i, 0)),
                   pl.BlockSpec((tm, kw), lambda i: (i, 0)),
                   pl.BlockSpec((tm, kw), lambda i: (i, 0))],
        out_shape=[jax.ShapeDtypeStruct((t, qw), BF16),
                   jax.ShapeDtypeStruct((t, kw), BF16),
                   jax.ShapeDtypeStruct((t, kw), BF16)],
        compiler_params=_cparams(("parallel",)),
        name="qk_prep",
    )(zq, zkv, cos64, sin64, jnp.tile(q_gain, ATT_HEADS).reshape(1, qw), jnp.tile(k_gain, ATT_KV_HEADS).reshape(1, kw),
      _block_ones(ATT_HEADS, HEAD_DIM), _block_ones(ATT_KV_HEADS, HEAD_DIM))


def _attn_kernel(q_ref, kp_ref, kc_ref, kn_ref, kx_ref, vp_ref, vc_ref, vn_ref, vx_ref, sink_ref, o_ref,
                 *, n_ctx_blocks, n_lat_blocks):
    blk = ATT_BLOCK
    n = pl.program_id(1)
    m = n - n_ctx_blocks
    is_lat = n >= n_ctx_blocks
    l_ctx = kx_ref.shape[0]
    width = 3 * blk + l_ctx
    rows = ATT_GROUP * blk
    r = lax.broadcasted_iota(jnp.int32, (rows, width), 0) % blk
    c = lax.broadcasted_iota(jnp.int32, (rows, width), 1)
    lat = is_lat.astype(jnp.int32)
    has_prev = lat * (m >= 1).astype(jnp.int32)
    has_next = lat * (m <= n_lat_blocks - 2).astype(jnp.int32)
    valid = jnp.where(c < blk, (c >= r).astype(jnp.int32) * has_prev,
                      jnp.where(c < 2 * blk, lat,
                                jnp.where(c < 3 * blk, (c - 2 * blk <= r).astype(jnp.int32) * has_next, 1))) > 0
    q = q_ref[...]
    sink = sink_ref[...]
    outs = [None] * ATT_HEADS
    for kvh in range(ATT_KV_HEADS):
        ks = slice(kvh * HEAD_DIM, (kvh + 1) * HEAD_DIM)
        kw = jnp.concatenate([kp_ref[:, ks], kc_ref[:, ks], kn_ref[:, ks], kx_ref[:, ks]], axis=0)
        vw = jnp.concatenate([vp_ref[:, ks], vc_ref[:, ks], vn_ref[:, ks], vx_ref[:, ks]], axis=0)
        heads = range(kvh * ATT_GROUP, (kvh + 1) * ATT_GROUP)
        qg = jnp.concatenate([q[:, h * HEAD_DIM:(h + 1) * HEAD_DIM] for h in heads], axis=0)
        s = jnp.where(valid, _dot_nt(qg, kw), -jnp.inf)
        sk = jnp.concatenate([jnp.broadcast_to(sink[h:h + 1, 0:1], (blk, 1)) for h in heads], axis=0)
        mx = jnp.maximum(jnp.max(s, axis=-1, keepdims=True), sk)
        p = jnp.exp(s - mx)
        denom = jnp.sum(p, axis=-1, keepdims=True) + jnp.exp(sk - mx)
        o = _dot(p.astype(BF16), vw) / denom
        for g, h in enumerate(heads):
            outs[h] = o[g * blk:(g + 1) * blk]
    o_ref[...] = jnp.concatenate(outs, axis=1)


def _attention(qn, kn, vn, sink, batch, l_ctx, l_tot):
    t = qn.shape[0]
    blk = ATT_BLOCK
    nq = l_tot // blk
    nc = l_ctx // blk
    nl = nq - nc
    qw, kw = ATT_HEADS * HEAD_DIM, ATT_KV_HEADS * HEAD_DIM

    def win(off):
        def index(b, n):
            m = jnp.clip(n - nc + off, 0, nl - 1)
            return (b * nq + nc + m, 0)
        return pl.BlockSpec((blk, kw), index)

    ctx_spec = pl.BlockSpec((l_ctx, kw), lambda b, n: (b * (l_tot // l_ctx), 0))
    return pl.pallas_call(
        functools.partial(_attn_kernel, n_ctx_blocks=nc, n_lat_blocks=nl),
        grid=(batch, nq),
        in_specs=[pl.BlockSpec((blk, qw), lambda b, n: (b * nq + n, 0)),
                  win(-1), win(0), win(1), ctx_spec,
                  win(-1), win(0), win(1), ctx_spec,
                  pl.BlockSpec((ATT_HEADS, 128), lambda b, n: (0, 0))],
        out_specs=pl.BlockSpec((blk, qw), lambda b, n: (b * nq + n, 0)),
        out_shape=jax.ShapeDtypeStruct((t, qw), F32),
        compiler_params=_cparams(("parallel", "parallel")),
        name="window_attention",
    )(qn, kn, kn, kn, kn, vn, vn, vn, vn, jnp.broadcast_to(sink.astype(F32)[:, None], (ATT_HEADS, 128)))


def _route(logits_t, bias_col):
    scores = jax.nn.sigmoid(logits_t)
    sel = scores + bias_col
    rows = [sel[e:e + 1] for e in range(N_EXPERTS)]
    grp = []
    for g in range(N_GROUPS):
        r = rows[g * PER_GROUP:(g + 1) * PER_GROUP]
        best = None
        for i in range(PER_GROUP):
            for j in range(i + 1, PER_GROUP):
                pair = r[i] + r[j]
                best = pair if best is None else jnp.maximum(best, pair)
        grp.append(best)
    g_best = jnp.zeros_like(grp[0], dtype=jnp.int32)
    g_val = grp[0]
    for g in range(1, N_GROUPS):
        take = grp[g] > g_val
        g_best = jnp.where(take, g, g_best)
        g_val = jnp.where(take, grp[g], g_val)
    neg = -jnp.inf
    masked = [jnp.where(g_best == e // PER_GROUP, rows[e], neg) for e in range(N_EXPERTS)]
    i1 = jnp.zeros_like(g_best)
    v1 = masked[0]
    for e in range(1, N_EXPERTS):
        take = masked[e] > v1
        i1 = jnp.where(take, e, i1)
        v1 = jnp.where(take, masked[e], v1)
    i2 = jnp.full_like(g_best, -1)
    v2 = jnp.full_like(v1, neg)
    for e in range(N_EXPERTS):
        take = jnp.logical_and(i1 != e, masked[e] > v2)
        i2 = jnp.where(take, e, i2)
        v2 = jnp.where(take, masked[e], v2)
    w1 = jnp.zeros_like(v1)
    w2 = jnp.zeros_like(v1)
    for e in range(N_EXPERTS):
        w1 = jnp.where(i1 == e, scores[e:e + 1], w1)
        w2 = jnp.where(i2 == e, scores[e:e + 1], w2)
    inv = 1.0 / (w1 + w2)
    pad = jnp.zeros_like(w1)
    lane = lax.broadcasted_iota(jnp.int32, w1.shape, 1)
    hist = pad
    for e in range(N_EXPERTS):
        n_e = jnp.sum((i1 == e).astype(F32) + (i2 == e).astype(F32), axis=1, keepdims=True)
        hist = jnp.where(lane == e, n_e, hist)
    return jnp.concatenate([i1.astype(F32), i2.astype(F32), w1 * inv, w2 * inv, hist, pad, pad, pad], axis=0)


def _residual_and_route(x, out, m, ffn_gain, rw_ref, rb_ref, x_ref, h_ref, g_ref):
    x1 = x + m[2:3] * out
    x_ref[...] = x1
    h = _modulated_norm(x1, ffn_gain, m[3:4], m[4:5])
    _store_row_tiles(h_ref, h)
    h_hi = h.astype(BF16)
    h_lo = (h - h_hi.astype(F32)).astype(BF16)
    logits = _dot(h_hi, rw_ref[0]) + _dot(h_lo, rw_ref[0]) + _dot(h_hi, rw_ref[1])
    g_ref[...] = _route(logits.T[:N_EXPERTS], rb_ref[...])


def _merge_even_kernel(o0_ref, o1_ref, gg_ref, oa_ref, x_ref, mod_ref, gn_ref, bd_ref, w1_ref, w2_ref,
                       fg_ref, rw_ref, rb_ref, xo_ref, ho_ref, go_ref):
    og = o0_ref[...] + o1_ref[...]
    ms = _dot(og * og, bd_ref[...]) * (1.0 / GLA_DV)
    g = og * lax.rsqrt(ms + NORM_EPS) * gn_ref[...] * _silu(gg_ref[...])
    out = _dot(g.astype(BF16), w1_ref[...]) + _dot(oa_ref[...].astype(BF16), w2_ref[...])
    _residual_and_route(x_ref[...], out, mod_ref[0], fg_ref[...], rw_ref, rb_ref, xo_ref, ho_ref, go_ref)


def _merge_even(o_f, o_b, zg, o_att, x, mods, out_norm, w_out, ffn_gain, router_wt, router_b, tm, mod_row):
    t, d = x.shape
    hv = GLA_HEADS * GLA_DV
    qw = ATT_HEADS * HEAD_DIM
    full = lambda a: pl.BlockSpec(a.shape, lambda i: (0,) * a.ndim)
    gn = jnp.tile(out_norm, GLA_HEADS).reshape(1, hv)
    bd = _block_ones(GLA_HEADS, GLA_DV)
    w1, w2 = w_out[:hv].astype(BF16), w_out[hv:].astype(BF16)
    fg = ffn_gain.reshape(1, d)
    rb = router_b.reshape(N_EXPERTS, 1)
    return pl.pallas_call(
        _merge_even_kernel,
        grid=(t // tm,),
        in_specs=[pl.BlockSpec((tm, hv), lambda i: (i, 0)),
                  pl.BlockSpec((tm, hv), lambda i: (i, 0)),
                  pl.BlockSpec((tm, hv), lambda i: (i, 2)),
                  pl.BlockSpec((tm, qw), lambda i: (i, 0)),
                  pl.BlockSpec((tm, d), lambda i: (i, 0)),
                  pl.BlockSpec((1, 6, d), lambda i: (mod_row(i), 0, 0)),
                  full(gn), full(bd), full(w1), full(w2), full(fg), full(router_wt), full(rb)],
        out_specs=[pl.BlockSpec((tm, d), lambda i: (i, 0)),
                   pl.BlockSpec((tm * (d // LANES), LANES), lambda i: (i, 0)),
                   pl.BlockSpec((8, tm), lambda i: (0, i))],
        out_shape=[jax.ShapeDtypeStruct((t, d), F32),
                   jax.ShapeDtypeStruct((t * (d // LANES), LANES), F32),
                   jax.ShapeDtypeStruct((8, t), F32)],
        compiler_params=_cparams(("parallel",)),
        name="merge_even",
    )(o_f, o_b, zg, o_att, x, mods, gn, bd, w1, w2, fg, router_wt, rb)


def _moe_plan(route, t, rows, route_tile):
    n_tiles = 2 * t // rows + N_EXPERTS
    eid = jnp.concatenate([route[0], route[1]]).astype(jnp.int32)
    slot = jnp.arange(2 * t, dtype=jnp.int32)
    gate = jnp.concatenate([route[2], route[3]])
    _, s_slot, s_gate = lax.sort((eid, slot, gate), num_keys=1, is_stable=True)
    counts = jnp.sum(route[4].reshape(t // route_tile, route_tile)[:, :N_EXPERTS], axis=0).astype(jnp.int32)
    padded = (counts + rows - 1) // rows * rows
    p_end = jnp.cumsum(padded)
    p_start = p_end - padded
    c_start = jnp.cumsum(counts) - counts
    tile_start = jnp.arange(n_tiles, dtype=jnp.int32) * rows
    tile_e = jnp.minimum(jnp.sum((tile_start[:, None] >= p_end[None, :]).astype(jnp.int32), axis=1), N_EXPERTS - 1)
    r_in = (tile_start - p_start[tile_e])[:, None] + jnp.arange(rows, dtype=jnp.int32)[None, :]
    over = r_in - counts[tile_e][:, None]
    valid = over < 0
    src = jnp.clip(c_start[tile_e][:, None] + r_in, 0, 2 * t - 1)
    g_slot = s_slot[src]
    row_tok = jnp.where(valid, jnp.where(g_slot >= t, g_slot - t, g_slot), 0)
    row_gate = jnp.where(valid, s_gate[src], 0.0)
    row_dst = jnp.where(valid, g_slot, 2 * t + tile_e[:, None] * rows + jnp.clip(over, 0, rows - 1))
    n_used = (p_end[-1] // rows).astype(jnp.int32)
    tile_e = jnp.where(tile_start < p_end[-1], tile_e, tile_e[jnp.maximum(n_used - 1, 0)])
    return (row_tok[:, None, :], row_gate[:, None, :], row_dst[:, None, :], tile_e, n_used.reshape(1))


def _moe_experts_kernel(te_ref, nu_ref, tok_ref, tokn_ref, dst_ref, gate_ref, wg_ref, wu_ref, wd_ref, h_hbm,
                        y_hbm, hbuf, ybuf, wgb, wub, wdb, sem_g, sem_s):
    j = pl.program_id(0)
    n_used = nu_ref[0]
    slot = j % 2
    g = SUBLANES
    rows = hbuf.shape[0] // (2 * g)
    tile = rows * g

    def row(ref, i):
        return ref.at[pl.ds(pl.multiple_of(i * g, g), g)]

    def start_gather(idx_ref, s):
        for r in range(rows):
            pltpu.make_async_copy(row(h_hbm, idx_ref[0, 0, r]), row(hbuf, s * rows + r), sem_g.at[s]).start(priority=r % 2)

    def wait_gather(s):
        pltpu.make_async_copy(h_hbm.at[pl.ds(0, tile)], hbuf.at[pl.ds(pl.multiple_of(s * tile, tile), tile)],
                              sem_g.at[s]).wait()

    def wait_scatter(s):
        pltpu.make_async_copy(ybuf.at[pl.ds(pl.multiple_of(s * tile, tile), tile)], y_hbm.at[pl.ds(0, tile)],
                              sem_s.at[s]).wait()

    @pl.when(j == 0)
    def _():
        start_gather(tok_ref, 0)
        ybuf[...] = jnp.zeros(ybuf.shape, F32)
        n_real = y_hbm.shape[0] // g - N_EXPERTS * rows
        fills = [pltpu.make_async_copy(ybuf.at[pl.ds((k % 2) * tile, tile)],
                                       y_hbm.at[pl.ds((n_real + k * rows) * g, tile)], sem_s.at[k % 2])
                 for k in range(N_EXPERTS)]
        for f in fills:
            f.start()
        for f in fills[2:]:
            f.wait()

    active = j < n_used
    changed = jnp.logical_or(j == 0, te_ref[j] != te_ref[jnp.maximum(j - 1, 0)])

    @pl.when(jnp.logical_and(active, changed))
    def _():
        wgb[...] = wg_ref[0].astype(BF16)
        wub[...] = wu_ref[0].astype(BF16)
        wdb[...] = wd_ref[0].astype(BF16)

    @pl.when(active)
    def _():
        wait_gather(slot)
        start_gather(tokn_ref, 1 - slot)
        h = _load_row_tiles(hbuf, rows, g, first_row=slot * rows).astype(BF16)
        act = _silu(_dot(h, wgb[...])) * _dot(h, wub[...])
        ri = lax.broadcasted_iota(jnp.int32, (rows, rows), 0)
        ci = lax.broadcasted_iota(jnp.int32, (rows, rows), 1)
        gate = jnp.sum(jnp.where(ri == ci, gate_ref[0], 0.0), axis=1, keepdims=True)
        y = _dot((act * gate).astype(BF16), wdb[...])
        wait_scatter(slot)
        _store_row_tiles(ybuf, y, first_row=slot * rows)
        for r in range(rows):
            pltpu.make_async_copy(row(ybuf, slot * rows + r), row(y_hbm, dst_ref[0, 0, r]), sem_s.at[slot]).start(priority=r % 2)

        @pl.when(j == n_used - 1)
        def _():
            wait_scatter(slot)
            wait_scatter(1 - slot)
            wait_gather(1 - slot)


def _moe_combine_kernel(ya_ref, yb_ref, x_ref, mod_ref, o_ref):
    n, d = x_ref.shape
    y = _load_row_tiles(ya_ref, n, d // LANES) + _load_row_tiles(yb_ref, n, d // LANES)
    o_ref[...] = x_ref[...] + mod_ref[0][5:6] * y


def _moe(h, route, x, mods, w_gate, w_up, w_down, layer, tm, mod_row, route_tile):
    t, d = x.shape
    g = d // LANES
    assert g == SUBLANES
    rows = MOE_TILE
    row_tok, row_gate, row_dst, tile_e, n_used = _moe_plan(route, t, rows, route_tile)
    n_tiles = row_tok.shape[0]
    n_pair_rows = 2 * t + N_EXPERTS * rows
    smem_tile = lambda nxt: pl.BlockSpec((1, 1, rows), lambda j, te, nu: (jnp.minimum(j + nxt, nu[0] - 1), 0, 0),
                                         memory_space=pltpu.SMEM)
    pairs = pl.pallas_call(
        _moe_experts_kernel,
        grid_spec=pltpu.PrefetchScalarGridSpec(
            num_scalar_prefetch=2,
            grid=(n_tiles,),
            in_specs=[smem_tile(0), smem_tile(1), smem_tile(0),
                      pl.BlockSpec((1, 1, rows), lambda j, te, nu: (j, 0, 0)),
                      pl.BlockSpec((None, 1, d, D_EXPERT), lambda j, te, nu: (layer, te[j], 0, 0)),
                      pl.BlockSpec((None, 1, d, D_EXPERT), lambda j, te, nu: (layer, te[j], 0, 0)),
                      pl.BlockSpec((None, 1, D_EXPERT, d), lambda j, te, nu: (layer, te[j], 0, 0)),
                      pl.BlockSpec(memory_space=pl.ANY)],
            out_specs=pl.BlockSpec(memory_space=pl.ANY),
            scratch_shapes=[pltpu.VMEM((2 * rows * g, LANES), F32), pltpu.VMEM((2 * rows * g, LANES), F32),
                            pltpu.VMEM((d, D_EXPERT), BF16), pltpu.VMEM((d, D_EXPERT), BF16),
                            pltpu.VMEM((D_EXPERT, d), BF16),
                            pltpu.SemaphoreType.DMA((2,)), pltpu.SemaphoreType.DMA((2,))]),
        out_shape=jax.ShapeDtypeStruct((n_pair_rows * g, LANES), F32),
        compiler_params=_cparams(("arbitrary",)),
        name="moe_experts",
    )(tile_e, n_used, row_tok, row_tok, row_dst, row_gate, w_gate, w_up, w_down, h)
    return pl.pallas_call(
        _moe_combine_kernel,
        grid=(t // tm,),
        in_specs=[pl.BlockSpec((tm * g, LANES), lambda i: (i, 0)),
                  pl.BlockSpec((tm * g, LANES), lambda i: (t // tm + i, 0)),
                  pl.BlockSpec((tm, d), lambda i: (i, 0)),
                  pl.BlockSpec((1, 6, d), lambda i: (mod_row(i), 0, 0))],
        out_specs=pl.BlockSpec((tm, d), lambda i: (i, 0)),
        out_shape=jax.ShapeDtypeStruct((t, d), F32),
        compiler_params=_cparams(("parallel",)),
        name="moe_combine",
    )(pairs, pairs, x, mods)


def _chan_dft_kernel(z_ref, w_ref, o_ref):
    res = _dot(z_ref[...].astype(BF16), w_ref[...]).astype(BF16)
    o_ref[0] = res[:, :FOURIER_WIDTH]
    o_ref[1] = res[:, FOURIER_WIDTH:]


def _chan_dft(zf, w, batch, l_ctx, l_lat, tm):
    l_tot = l_ctx + l_lat
    nt = l_lat // tm
    fw = FOURIER_WIDTH
    return pl.pallas_call(
        _chan_dft_kernel,
        grid=(batch, nt),
        in_specs=[pl.BlockSpec((tm, fw), lambda b, i: (b * (l_tot // tm) + l_ctx // tm + i, 0)),
                  pl.BlockSpec(w.shape, lambda b, i: (0, 0))],
        out_specs=pl.BlockSpec((2, tm, fw), lambda b, i: (0, i, b)),
        out_shape=jax.ShapeDtypeStruct((2, l_lat, batch * fw), BF16),
        compiler_params=_cparams(("parallel", "parallel")),
        name="fourier_channels",
    )(zf, w)


def _seq_dft_kernel(ca_ref, sa_ref, cb_ref, sb_ref, z_ref, o_ref, acc_ref):
    k = pl.program_id(1)
    tk = z_ref.shape[1]
    sub = DFT_SPLIT
    n_a = ca_ref.shape[1]

    @pl.when(k == 0)
    def _():
        acc_ref[...] = jnp.zeros_like(acc_ref)

    col_a = k * tk + lax.broadcasted_iota(jnp.int32, (n_a, tk), 1)
    pick_a = (col_a // sub == lax.broadcasted_iota(jnp.int32, (n_a, tk), 0)).astype(F32)
    pick_b = (lax.broadcasted_iota(jnp.int32, (sub, tk), 1) % sub == lax.broadcasted_iota(jnp.int32, (sub, tk), 0)).astype(F32)
    ca, sa = _dot(ca_ref[...], pick_a), _dot(sa_ref[...], pick_a)
    cb, sb = _dot(cb_ref[...], pick_b), _dot(sb_ref[...], pick_b)
    cos_t = (ca * cb - sa * sb).astype(BF16)
    sin_t = (sa * cb + ca * sb).astype(BF16)
    acc_ref[...] += _dot(cos_t, z_ref[0]) + _dot(sin_t, z_ref[1])

    @pl.when(k == pl.num_programs(1) - 1)
    def _():
        o_ref[...] = acc_ref[...]


def _seq_dft(tables, zc, batch, tm, tk):
    ca, sa, cb, sb = tables
    l = ca.shape[0]
    fw = FOURIER_WIDTH
    small = lambda a: pl.BlockSpec((tm, a.shape[1]), lambda i, k: (i, 0))
    return pl.pallas_call(
        _seq_dft_kernel,
        grid=(l // tm, l // tk),
        in_specs=[small(ca), small(sa), small(cb), small(sb),
                  pl.BlockSpec((2, tk, batch * fw), lambda i, k: (0, k, 0))],
        out_specs=pl.BlockSpec((tm, batch * fw), lambda i, k: (i, 0)),
        out_shape=jax.ShapeDtypeStruct((l, batch * fw), F32),
        scratch_shapes=[pltpu.VMEM((tm, batch * fw), F32)],
        compiler_params=_cparams(("parallel", "arbitrary")),
        name="fourier_sequence",
    )(ca, sa, cb, sb, zc)


def _dft_tables(l):
    m = jnp.arange(l, dtype=jnp.int32)[:, None]
    n1 = l // DFT_SPLIT
    a = (m * (jnp.arange(n1, dtype=jnp.int32)[None, :] * DFT_SPLIT)) % l
    b = (m * jnp.arange(DFT_SPLIT, dtype=jnp.int32)[None, :]) % l
    wa = a.astype(F32) * (2.0 * np.pi / l)
    wb = b.astype(F32) * (2.0 * np.pi / l)
    tables = (jnp.cos(wa), jnp.sin(wa), jnp.cos(wb), jnp.sin(wb))
    gd = FOURIER_GROUP_DIM
    cc = (jnp.arange(gd, dtype=jnp.int32)[:, None] * jnp.arange(gd, dtype=jnp.int32)[None, :]) % gd
    wc = cc.astype(F32) * (2.0 * np.pi / gd)
    scale = 1.0 / np.sqrt(float(l) * gd)
    eye = jnp.eye(FOURIER_WIDTH // gd, dtype=F32)
    chan = jnp.concatenate([jnp.kron(eye, jnp.cos(wc)), -jnp.kron(eye, jnp.sin(wc))], axis=1) * scale
    return tables, chan.astype(BF16)


def _rwkv_prepare(d, pos, z_ref, zp_ref, zn_ref, par, nc_ctx, nc_tot):
    mu, kks, ka, rk, w0, w2, a0, a2, bd = par
    c = z_ref.shape[0]
    n = RWKV_DIM
    seg_first = jnp.logical_or(pos == 0, pos == nc_ctx)
    seg_last = jnp.logical_or(pos == nc_ctx - 1, pos == nc_tot - 1)
    z = z_ref[...]
    row = lax.broadcasted_iota(jnp.int32, z.shape, 0)
    prev_row = jnp.where(seg_first, 0.0, zp_ref[7:8, :])
    next_row = jnp.where(seg_last, 0.0, zn_ref[0:1, :])
    z_prev = jnp.where(row == 0, prev_row, pltpu.roll(z, 1, 0))
    z_next = jnp.where(row == c - 1, next_row, pltpu.roll(z, c - 1, 0))
    zs = z + mu * (0.5 * (z_prev + z_next) - z)

    r, k, v = zs[:, 0:n], zs[:, n:2 * n], zs[:, 2 * n:3 * n]
    zw = zs[:, 3 * n:3 * n + RWKV_RANK_PAD]
    za = zs[:, 3 * n + RWKV_RANK_PAD:3 * n + 2 * RWKV_RANK_PAD]
    zg = zs[:, 3 * n + 2 * RWKV_RANK_PAD:3 * n + 3 * RWKV_RANK_PAD]

    kk = k * kks
    kk = kk * lax.rsqrt(_dot(kk * kk, bd) + L2_EPS)
    w_log = _log_sigmoid(w0[d] + _dot(jnp.tanh(zw), w2[d])) - 0.5
    lw = -jnp.exp(w_log)
    a = jax.nn.sigmoid(a0[d] + _dot(za, a2[d]))
    kd = k * (1.0 + (a - 1.0) * ka)
    beta = kk * a

    cl = _dot((_visit_order(d, c) >= 0).astype(F32), lw)
    c_tot = jnp.sum(lw, axis=0, keepdims=True)
    grow = jnp.exp(-cl)
    tail = jnp.exp(c_tot - cl)
    ops = dict(k_s=kd * grow,
               b_s=beta * grow,
               kap_s=kk * jnp.exp(cl - lw),
               r_s=r * jnp.exp(cl),
               k_e=kd * tail,
               b_e=beta * tail,
               gam=jnp.exp(c_tot), v=v,
               bonus=_dot(r * kd * rk, bd),
               sign=1 if d == 0 else -1)
    return ops, zg


def _rwkv_kernel(zf_ref, zfp_ref, zfn_ref, zb_ref, zbp_ref, zbn_ref, mu_ref, kks_ref, ka_ref, rk_ref, w0_ref, w2_ref,
                 a0_ref, a2_ref, g2_ref, bd_ref, yf_ref, yb_ref, gate_ref, st_ref, *, nc_ctx, nc_tot):
    s = pl.program_id(1)
    c = zf_ref.shape[0]
    hd = HEAD_DIM
    pw = 2 * hd

    @pl.when(s == 0)
    def _():
        st_ref[...] = jnp.zeros_like(st_ref)

    par = (mu_ref[...], kks_ref[...], ka_ref[...], rk_ref[...], w0_ref, w2_ref, a0_ref, a2_ref, bd_ref[...])
    fwd, zg = _rwkv_prepare(0, _chunk_pos(s, 0, nc_ctx, nc_tot), zf_ref, zfp_ref, zfn_ref, par, nc_ctx, nc_tot)
    bwd, _ = _rwkv_prepare(1, _chunk_pos(s, 1, nc_ctx, nc_tot), zb_ref, zbp_ref, zbn_ref, par, nc_ctx, nc_tot)
    gate_ref[...] = _dot(jax.nn.sigmoid(zg), g2_ref[...])

    lane = lax.broadcasted_iota(jnp.int32, (c, pw), 1)
    rowi = lax.broadcasted_iota(jnp.int32, (c, pw), 0)
    left = lane < hd
    eye_p = (rowi == lane % hd).astype(F32)
    same_head = (lax.broadcasted_iota(jnp.int32, (pw, pw), 0) < hd) == (lax.broadcasted_iota(jnp.int32, (pw, pw), 1) < hd)

    def bd(y):
        return jnp.concatenate([jnp.where(left, y, 0.0), jnp.where(left, 0.0, y)], axis=0)

    n_pairs = RWKV_HEADS // 2
    prob = [(o, slice(i * pw, (i + 1) * pw)) for o in (fwd, bwd) for i in range(n_pairs)]
    ahead = [(rowi - lane % hd) * o['sign'] for o, _ in prob]
    get = lambda name: [o[name][:, s_] for o, s_ in prob]
    kap, r_s, k_s, b_s, k_e, b_e, vp, gam, bonus = (get(x) for x in ('kap_s', 'r_s', 'k_s', 'b_s', 'k_e', 'b_e', 'v', 'gam', 'bonus'))
    p = [_dot_nt(jnp.concatenate([a_, b_], axis=0), jnp.concatenate([bd(c_), bd(d_)], axis=0))
         for a_, b_, c_, d_ in zip(kap, r_s, k_s, b_s)]
    m1 = [jnp.where(h_ > 0, x[0:c, 0:pw], 0.0) for x, h_ in zip(p, ahead)]
    m2 = [jnp.where(h_ > 0, x[0:c, pw:2 * pw], 0.0) for x, h_ in zip(p, ahead)]
    n1 = [jnp.where(h_ >= 0, x[c:2 * c, 0:pw], 0.0) for x, h_ in zip(p, ahead)]
    n2 = [jnp.where(h_ >= 0, x[c:2 * c, pw:2 * pw], 0.0) for x, h_ in zip(p, ahead)]
    m1v = [_dot(a_, bd(b_)) for a_, b_ in zip(m1, vp)]
    t_inv = [eye_p - x for x in m2]
    q = [_dot(x, bd(x)) for x in m2]
    span = 2
    while 2 * span < c:
        both = [_dot(jnp.concatenate([t_, q_], axis=0), bd(q_)) for t_, q_ in zip(t_inv, q)]
        t_inv = [t_ + x[0:c] for t_, x in zip(t_inv, both)]
        q = [x[c:2 * c] for x in both]
        span *= 2
    t_inv = [t_ + _dot(t_, bd(q_)) for t_, q_ in zip(t_inv, q)]
    tx = [_dot(t_, jnp.concatenate([bd(a_), bd(mv)], axis=1)) for t_, a_, mv in zip(t_inv, kap, m1v)]
    st = [st_ref[i] for i in range(len(prob))]
    su = [_dot_nt(jnp.concatenate([x[:, 0:pw], r_], axis=0), s0) for x, r_, s0 in zip(tx, r_s, st)]
    u = [x[0:c] + y_[:, pw:2 * pw] for x, y_ in zip(su, tx)]
    ys = [x[c:2 * c] + _dot(jnp.concatenate([a_, -b_], axis=1), jnp.concatenate([bd(v_), bd(u_)], axis=0)) + bo * v_
          for x, a_, b_, v_, u_, bo in zip(su, n1, n2, vp, u, bonus)]
    for i in range(len(prob)):
        upd = _dot_tn(jnp.concatenate([vp[i], u[i]], axis=0), jnp.concatenate([k_e[i], -b_e[i]], axis=0))
        st_ref[i] = st[i] * gam[i] + jnp.where(same_head, upd, 0.0)
    yf_ref[...] = jnp.concatenate(ys[:n_pairs], axis=1)
    yb_ref[...] = jnp.concatenate(ys[n_pairs:], axis=1)


def _rwkv(zr, mu, kk_scale, k_a, r_k, w0, w2_pad, a0, a2_pad, g2, batch, nc_ctx, nc_tot):
    t, zw_ = zr.shape
    c = SEQ_CHUNK
    n = RWKV_DIM
    bd = _block_ones(RWKV_HEADS, HEAD_DIM)
    full = lambda a: pl.BlockSpec(a.shape, lambda b, s: (0,) * a.ndim)
    sub = c // 8
    n_sub = t // 8

    def rows(d):
        return lambda b, s: b * nc_tot + _chunk_pos(s, d, nc_ctx, nc_tot)

    def z_specs(d):
        rw = rows(d)
        return [pl.BlockSpec((c, zw_), lambda b, s: (rw(b, s), 0)),
                pl.BlockSpec((8, zw_), lambda b, s: (jnp.maximum(rw(b, s) * sub - 1, 0), 0)),
                pl.BlockSpec((8, zw_), lambda b, s: (jnp.minimum((rw(b, s) + 1) * sub, n_sub - 1), 0))]

    vec = lambda a: a.reshape(1, -1)
    args = (vec(mu), vec(kk_scale), vec(k_a), vec(r_k), w0.reshape(2, 1, n), w2_pad, a0.reshape(2, 1, n), a2_pad, g2, bd)
    out = lambda d: pl.BlockSpec((c, n), lambda b, s: (rows(d)(b, s), 0))
    return pl.pallas_call(
        functools.partial(_rwkv_kernel, nc_ctx=nc_ctx, nc_tot=nc_tot),
        grid=(batch, nc_tot),
        in_specs=z_specs(0) + z_specs(1) + [full(a) for a in args],
        out_specs=[out(0), out(1), out(0)],
        out_shape=[jax.ShapeDtypeStruct((t, n), F32)] * 3,
        scratch_shapes=[pltpu.VMEM((RWKV_HEADS, 2 * HEAD_DIM, 2 * HEAD_DIM), F32)],
        compiler_params=_cparams(("parallel", "arbitrary")),
        name="rwkv_scan",
    )(zr, zr, zr, zr, zr, zr, *args)


def _merge_odd_kernel(y0_ref, y1_ref, gate_ref, fo_ref, x_ref, mod_ref, lg_ref, lb_ref, bd_ref, w1_ref, w2_ref,
                      fg_ref, rw_ref, rb_ref, xo_ref, ho_ref, go_ref):
    y = y0_ref[...] + y1_ref[...]
    bd = bd_ref[...]
    mean = _dot(y, bd) * (1.0 / HEAD_DIM)
    yc = y - mean
    var = _dot(yc * yc, bd) * (1.0 / HEAD_DIM)
    rw = (yc * lax.rsqrt(var + RWKV_GN_EPS) * lg_ref[...] + lb_ref[...]) * gate_ref[...]
    out = _dot(fo_ref[...].astype(BF16), w1_ref[...]) + _dot(rw.astype(BF16), w2_ref[...])
    _residual_and_route(x_ref[...], out, mod_ref[0], fg_ref[...], rw_ref, rb_ref, xo_ref, ho_ref, go_ref)


def _merge_odd(y0, y1, gate, fo, x, mods, ln_g, ln_b, w_out, ffn_gain, router_wt, router_b, tm, batch, l_ctx, l_lat):
    d = x.shape[1]
    n = RWKV_DIM
    fw = FOURIER_WIDTH
    l_tot = l_ctx + l_lat
    nt = l_lat // tm
    t_out = batch * l_lat
    full = lambda a: pl.BlockSpec(a.shape, lambda b, i: (0,) * a.ndim)
    src = lambda b, i: b * (l_tot // tm) + l_ctx // tm + i
    bd = _block_ones(RWKV_HEADS, HEAD_DIM)
    w1, w2 = w_out[:fw].astype(BF16), w_out[fw:].astype(BF16)
    lg, lb, fg, rb = ln_g.reshape(1, n), ln_b.reshape(1, n), ffn_gain.reshape(1, d), router_b.reshape(N_EXPERTS, 1)
    return pl.pallas_call(
        _merge_odd_kernel,
        grid=(batch, nt),
        in_specs=[pl.BlockSpec((tm, n), lambda b, i: (src(b, i), 0)),
                  pl.BlockSpec((tm, n), lambda b, i: (src(b, i), 0)),
                  pl.BlockSpec((tm, n), lambda b, i: (src(b, i), 0)),
                  pl.BlockSpec((tm, fw), lambda b, i: (i, b)),
                  pl.BlockSpec((tm, d), lambda b, i: (src(b, i), 0)),
                  pl.BlockSpec((1, 6, d), lambda b, i: (b, 0, 0)),
                  full(lg), full(lb), full(bd), full(w1), full(w2), full(fg), full(router_wt), full(rb)],
        out_specs=[pl.BlockSpec((tm, d), lambda b, i: (b * nt + i, 0)),
                   pl.BlockSpec((tm * (d // LANES), LANES), lambda b, i: (b * nt + i, 0)),
                   pl.BlockSpec((8, tm), lambda b, i: (0, b * nt + i))],
        out_shape=[jax.ShapeDtypeStruct((t_out, d), F32),
                   jax.ShapeDtypeStruct((t_out * (d // LANES), LANES), F32),
                   jax.ShapeDtypeStruct((8, t_out), F32)],
        compiler_params=_cparams(("parallel", "parallel")),
        name="merge_odd",
    )(y0, y1, gate, fo, x, mods, lg, lb, bd, w1, w2, fg, router_wt, rb)


def _rope_tables(l_ctx, l_lat):
    rows = l_lat // GRID_W
    row = jnp.repeat(jnp.arange(rows, dtype=F32), GRID_W)
    col = jnp.tile(jnp.arange(GRID_W, dtype=F32), rows)
    n_freq = HEAD_DIM // 4
    inv_freq = ROPE_THETA ** (-jnp.arange(n_freq, dtype=F32) / n_freq)
    ang = jnp.concatenate([row[:, None] * inv_freq, col[:, None] * inv_freq], axis=-1)
    cos, sin = jnp.cos(ang), jnp.sin(ang)
    cos64 = jnp.concatenate([cos, cos], axis=1)
    sin64 = jnp.concatenate([-sin, sin], axis=1)
    cos64 = jnp.concatenate([jnp.ones((l_ctx, HEAD_DIM), F32), cos64], axis=0)
    sin64 = jnp.concatenate([jnp.zeros((l_ctx, HEAD_DIM), F32), sin64], axis=0)
    return cos64, sin64


def _pad_rank(w):
    _, r, n = w.shape
    out = jnp.zeros((2, RWKV_RANK_PAD, n), w.dtype)
    out = out.at[0, 0:r].set(w[0])
    return out.at[1, r:2 * r].set(w[1])


def _even_layer(x, mods, p, batch, l_ctx, l_lat, tm, mod_row, tm_moe):
    l_tot = l_ctx + l_lat
    d = x.shape[1]
    nc_ctx, nc_tot = l_ctx // SEQ_CHUNK, l_tot // SEQ_CHUNK
    hk, hv = GLA_HEADS * GLA_DK, GLA_HEADS * GLA_DV
    qw, kw = ATT_HEADS * HEAD_DIM, ATT_KV_HEADS * HEAD_DIM
    w_in = p['w_in']
    o = np.cumsum([0, hk, hk, hv, hv, 2 * GLA_LOWRANK, qw, kw, kw])
    w_gla = jnp.concatenate([w_in[:, o[0]:o[4]]], axis=1).astype(BF16)
    w_dec = jnp.pad(w_in[:, o[4]:o[5]], ((0, 0), (0, 128 - 2 * GLA_LOWRANK))).astype(BF16)
    w_q = w_in[:, o[5]:o[6]].astype(BF16)
    w_kv = w_in[:, o[6]:o[8]].astype(BF16)
    zg, zdec, zq, zkv = _project(x, mods, p['norm_mix'], [w_gla, w_dec, w_q, w_kv], tm, mod_row)

    dec_w_pad = _pad_rank(p['dec_w'])
    o_f, o_b = _gla(zg, zdec, dec_w_pad, p['dec_b'].reshape(2, 1, hk), batch, nc_ctx, nc_tot)

    cos64, sin64 = _rope_tables(l_ctx, l_lat)
    qn, kn, vn = _qk_prep(zq, zkv, cos64, sin64, p['q_norm'], p['k_norm'], tm, l_tot)
    o_att = _attention(qn, kn, vn, p['sink'], batch, l_ctx, l_tot)

    x1, h, gates = _merge_even(o_f, o_b, zg, o_att, x, mods, p['out_norm'], p['w_out'], p['norm_ffn'],
                               p['router_wt'], p['router_b'], tm, mod_row)
    return _moe(h, gates, x1, mods, p['moe_g'], p['moe_u'], p['moe_d'], p['moe_layer'], tm, mod_row, tm)


def _odd_layer(x, mods, p, batch, l_ctx, l_lat, tm, mod_row, tm_moe):
    l_tot = l_ctx + l_lat
    nc_ctx, nc_tot = l_ctx // SEQ_CHUNK, l_tot // SEQ_CHUNK
    n = RWKV_DIM
    fw = FOURIER_WIDTH
    w_in = p['w_in']
    rank_w, rank_a = p['w2'].shape[1], p['a2'].shape[1]
    o = np.cumsum([0, fw, n, n, n, 2 * rank_w, 2 * rank_a])
    pad_cols = lambda w: jnp.pad(w, ((0, 0), (0, RWKV_RANK_PAD - w.shape[1])))
    w_f = w_in[:, o[0]:o[1]].astype(BF16)
    w_r = jnp.concatenate([w_in[:, o[1]:o[4]], pad_cols(w_in[:, o[4]:o[5]]), pad_cols(w_in[:, o[5]:o[6]]),
                           w_in[:, o[6]:]], axis=1).astype(BF16)
    zf, zr = _project(x, mods, p['norm_mix'], [w_f, w_r], tm, mod_row)

    mu = p['mu']
    mu_r = jnp.concatenate([mu[0:3 * n], pad_cols(mu[None, 3 * n:3 * n + 2 * rank_w])[0],
                            pad_cols(mu[None, 3 * n + 2 * rank_w:3 * n + 2 * rank_w + 2 * rank_a])[0],
                            mu[3 * n + 2 * rank_w + 2 * rank_a:]])
    y0, y1, gate = _rwkv(zr, mu_r, p['kk_scale'], p['k_a'], p['r_k'].reshape(-1), p['w0'], _pad_rank(p['w2']),
                    p['a0'], _pad_rank(p['a2']), p['g2'], batch, nc_ctx, nc_tot)

    tables, chan = _dft_tables(l_lat)
    zc = _chan_dft(zf, chan, batch, l_ctx, l_lat, tm)
    fo = _seq_dft(tables, zc, batch, min(512, l_lat), min(1024, l_lat))

    x1, h, gates = _merge_odd(y0, y1, gate, fo, x, mods, p['ln_g'], p['ln_b'], p['w_out'], p['norm_ffn'],
                              p['router_wt'], p['router_b'], tm, batch, l_ctx, l_lat)
    lat_tiles = l_lat // tm_moe
    return _moe(h, gates, x1, mods, p['moe_g'], p['moe_u'], p['moe_d'], p['moe_layer'], tm_moe,
                lambda i: i // lat_tiles, tm)


def kernel(x, c, ctx, c_ctx, ada_w, ada_b, norm_mix, norm_ffn, even_w_in, even_w_out, gla_dec_w, gla_dec_b, gla_out_norm, att_q_norm, att_k_norm, att_sink, odd_w_in, odd_w_out, rwkv_mu, rwkv_w0, rwkv_w2, rwkv_a0, rwkv_a2, rwkv_g2, rwkv_kk_scale, rwkv_k_a, rwkv_r_k, rwkv_ln_g, rwkv_ln_b, router_w, router_b, moe_w_gate, moe_w_up, moe_w_down):
    batch, l_lat, d = x.shape
    l_ctx = ctx.shape[1]
    l_tot = l_ctx + l_lat
    assert batch < 8 and ada_w.shape[0] == 2
    tm = 256 if (l_ctx % 256 == 0 and l_lat % 256 == 0) else 128
    tm_moe = 512 if (l_lat % 512 == 0 and tm == 256) else tm
    assert l_ctx % tm == 0 and l_lat % tm == 0 and l_tot % l_ctx == 0 and l_lat % GRID_W == 0

    xs = jnp.concatenate([ctx, x], axis=1).reshape(batch * l_tot, d)
    cc = jnp.concatenate([c, c_ctx[None, :], jnp.zeros((8 - batch - 1, d), F32)], axis=0)
    tiles_per_b = l_tot // tm
    ctx_tiles = l_ctx // tm

    def mod_row(i):
        return jnp.where(i % tiles_per_b < ctx_tiles, batch, i // tiles_per_b)

    rw_pad = jnp.pad(router_w, ((0, 0), (0, LANES - N_EXPERTS)))
    rw_hi = rw_pad.astype(BF16)
    router_wt = jnp.stack([rw_hi, (rw_pad - rw_hi.astype(F32)).astype(BF16)])
    moe = lambda layer: dict(moe_g=moe_w_gate, moe_u=moe_w_up, moe_d=moe_w_down, moe_layer=layer)

    mods_all = _modvec(cc, ada_w, ada_b)
    mods0 = mods_all[0]
    p0 = dict(w_in=even_w_in[0], w_out=even_w_out[0], dec_w=gla_dec_w[0], dec_b=gla_dec_b[0],
              out_norm=gla_out_norm[0], q_norm=att_q_norm[0], k_norm=att_k_norm[0], sink=att_sink[0],
              norm_mix=norm_mix[0], norm_ffn=norm_ffn[0], router_wt=router_wt, router_b=router_b, **moe(0))
    xs = _even_layer(xs, mods0, p0, batch, l_ctx, l_lat, tm, mod_row, tm_moe)

    mods1 = mods_all[1]
    p1 = dict(w_in=odd_w_in[0], w_out=odd_w_out[0], mu=rwkv_mu[0], w0=rwkv_w0[0], w2=rwkv_w2[0], a0=rwkv_a0[0],
              a2=rwkv_a2[0], g2=rwkv_g2[0], kk_scale=rwkv_kk_scale[0], k_a=rwkv_k_a[0], r_k=rwkv_r_k[0],
              ln_g=rwkv_ln_g[0], ln_b=rwkv_ln_b[0], norm_mix=norm_mix[1], norm_ffn=norm_ffn[1],
              router_wt=router_wt, router_b=router_b, **moe(1))
    out = _odd_layer(xs, mods1, p1, batch, l_ctx, l_lat, tm, mod_row, tm_moe)
    return out.reshape(batch, l_lat, d)
```

```python
import functools

import jax
import jax.numpy as jnp
import numpy as np
from jax import lax
from jax.experimental import pallas as pl
from jax.experimental.pallas import tpu as pltpu

F32 = jnp.float32
BF16 = jnp.bfloat16

GRID_W = 64
HEAD_DIM = 64
NORM_EPS = 1e-6
L2_EPS = 1e-12

GLA_DV = 64
GLA_DK = 32
GLA_HEADS = 8
GLA_LOWRANK = 16
GLA_TAU = 16.0

ATT_HEADS = 8
ATT_KV_HEADS = 2
ATT_GROUP = ATT_HEADS // ATT_KV_HEADS
ATT_BLOCK = 128
ROPE_THETA = 10000.0

FOURIER_GROUP_DIM = 64
FOURIER_WIDTH = 256
DFT_SPLIT = 64

RWKV_DIM = 768
RWKV_HEADS = 12
RWKV_RANK_PAD = 128
RWKV_GN_EPS = 64e-5

N_EXPERTS = 16
N_GROUPS = 4
PER_GROUP = N_EXPERTS // N_GROUPS
D_EXPERT = 512
MOE_TILE = 256
MOE_GATHER_DEPTH = 3
LANES = 128
SUBLANES = 8

SEQ_CHUNK = 64
VMEM_LIMIT = 56 * 1024 * 1024


def _cparams(sem):
    return pltpu.CompilerParams(dimension_semantics=sem, vmem_limit_bytes=VMEM_LIMIT)


def _dot(a, b):
    return jnp.dot(a, b, preferred_element_type=F32)


def _dot_nt(a, b):
    return lax.dot_general(a, b, (((1,), (1,)), ((), ())), preferred_element_type=F32)


def _dot_tn(a, b):
    return lax.dot_general(a, b, (((0,), (0,)), ((), ())), preferred_element_type=F32)


def _silu(x):
    return x * jax.nn.sigmoid(x)


def _log_sigmoid(x):
    return jnp.minimum(x, 0.0) - jnp.log(1.0 + jnp.exp(-jnp.abs(x)))


def _modulated_norm(x, gain, shift, scale):
    ms = jnp.mean(x * x, axis=-1, keepdims=True)
    return (x * lax.rsqrt(ms + NORM_EPS) * gain) * (1.0 + scale) + shift


def _block_ones(n_blocks, width):
    return jnp.kron(jnp.eye(n_blocks, dtype=F32), jnp.ones((width, width), F32))


def _store_row_tiles(ref, x, first_row=0):
    n, w = x.shape
    g = w // LANES
    for k in range(g):
        ref[pl.ds(first_row * g + k, n, stride=g), :] = x[:, k * LANES:(k + 1) * LANES]


def _load_row_tiles(ref, n, g, first_row=0):
    return jnp.concatenate([ref[pl.ds(first_row * g + k, n, stride=g), :] for k in range(g)], axis=1)


def _modvec_kernel(c_ref, w_ref, b_ref, o_ref):
    o_ref[...] = _dot(_silu(c_ref[...]), w_ref[...]) + b_ref[...]


def _modvec(cc, w, b):
    d = cc.shape[1]
    layers, _, n = w.shape
    tn = n // 4
    out = pl.pallas_call(
        _modvec_kernel,
        grid=(layers, n // tn),
        in_specs=[pl.BlockSpec((8, d), lambda l, j: (0, 0)),
                  pl.BlockSpec((None, d, tn), lambda l, j: (l, 0, j)),
                  pl.BlockSpec((None, 1, tn), lambda l, j: (l, 0, j))],
        out_specs=pl.BlockSpec((None, 8, tn), lambda l, j: (l, 0, j)),
        out_shape=jax.ShapeDtypeStruct((layers, 8, n), F32),
        compiler_params=_cparams(("parallel", "parallel")),
        name="modvec",
    )(cc, w, b.reshape(layers, 1, n))
    return out.reshape(layers, 8, 6, d)


def _proj_kernel(x_ref, mod_ref, gain_ref, *refs, n_out):
    w_refs, z_refs = refs[:n_out], refs[n_out:]
    m = mod_ref[0]
    h = _modulated_norm(x_ref[...], gain_ref[...], m[0:1], m[1:2]).astype(BF16)
    for w_ref, z_ref in zip(w_refs, z_refs):
        z_ref[...] = _dot(h, w_ref[...])


def _project(x, mods, gain, weights, tm, mod_row):
    t, d = x.shape
    n_out = len(weights)
    in_specs = [pl.BlockSpec((tm, d), lambda i: (i, 0)),
                pl.BlockSpec((1, 6, d), lambda i: (mod_row(i), 0, 0)),
                pl.BlockSpec((1, d), lambda i: (0, 0))]
    in_specs += [pl.BlockSpec(w.shape, lambda i: (0, 0)) for w in weights]
    return pl.pallas_call(
        functools.partial(_proj_kernel, n_out=n_out),
        grid=(t // tm,),
        in_specs=in_specs,
        out_specs=[pl.BlockSpec((tm, w.shape[1]), lambda i: (i, 0)) for w in weights],
        out_shape=[jax.ShapeDtypeStruct((t, w.shape[1]), F32) for w in weights],
        compiler_params=_cparams(("parallel",)),
        name="proj",
    )(x, mods, gain.reshape(1, d), *weights)


def _chunk_pos(s, d, nc_ctx, nc_tot):
    back = jnp.where(s < nc_ctx, nc_ctx - 1 - s, nc_tot + nc_ctx - 1 - s)
    return jnp.where(d == 0, s, back)


def _visit_order(d, c):
    sign = 1 if d == 0 else -1
    return (lax.broadcasted_iota(jnp.int32, (c, c), 0) - lax.broadcasted_iota(jnp.int32, (c, c), 1)) * sign


def _gla_prepare(d, q_ref, k_ref, v_ref, dec_ref, dw_ref, db_ref):
    c = q_ref.shape[0]
    g = _log_sigmoid(_dot(dec_ref[...], dw_ref[d]) + db_ref[d]) / GLA_TAU
    b = _dot((_visit_order(d, c) >= 0).astype(F32), g)
    b_tot = jnp.sum(g, axis=0, keepdims=True)
    k = k_ref[...]
    return dict(q_in=q_ref[...] * (GLA_DK ** -0.5) * jnp.exp(b), k_out=k * jnp.exp(-b), k_end=k * jnp.exp(b_tot - b),
                decay=jnp.exp(b_tot), v=v_ref[...], sign=1 if d == 0 else -1)


def _gla_kernel(qf_ref, kf_ref, vf_ref, df_ref, qb_ref, kb_ref, vb_ref, db_ref, dw_ref, dbias_ref, of_ref, ob_ref, st_ref):
    c = qf_ref.shape[0]
    group = 4
    kw, vw = group * GLA_DK, group * GLA_DV

    @pl.when(pl.program_id(1) == 0)
    def _():
        st_ref[...] = jnp.zeros_like(st_ref)

    fwd = _gla_prepare(0, qf_ref, kf_ref, vf_ref, df_ref, dw_ref, dbias_ref)
    bwd = _gla_prepare(1, qb_ref, kb_ref, vb_ref, db_ref, dw_ref, dbias_ref)
    n_quads = GLA_HEADS // group
    prob = [(o, i) for o in (fwd, bwd) for i in range(n_quads)]
    klane = lax.broadcasted_iota(jnp.int32, (c, kw), 1) // GLA_DK
    half = lax.broadcasted_iota(jnp.int32, (c, 2 * GLA_DV), 1) < GLA_DV
    rowi = lax.broadcasted_iota(jnp.int32, (c, group * c), 0)
    coli = lax.broadcasted_iota(jnp.int32, (c, group * c), 1) % c
    own = (lax.broadcasted_iota(jnp.int32, (vw, kw), 0) // GLA_DV) == (lax.broadcasted_iota(jnp.int32, (vw, kw), 1) // GLA_DK)

    def bd_keys(y):
        return jnp.concatenate([jnp.where(klane == h, y, 0.0) for h in range(group)], axis=0)

    def bd_vals(y):
        return jnp.concatenate([jnp.where(half, y, 0.0), jnp.where(half, 0.0, y)], axis=0)

    ksl = lambda i: slice(i * kw, (i + 1) * kw)
    vsl = lambda i: slice(i * vw, (i + 1) * vw)
    q_in = [o['q_in'][:, ksl(i)] for o, i in prob]
    att = [jnp.where((rowi - coli) * o['sign'] >= 0, _dot_nt(q_, bd_keys(o['k_out'][:, ksl(i)])), 0.0)
           for q_, (o, i) in zip(q_in, prob)]
    st = [st_ref[j] for j in range(len(prob))]
    outs = []
    for j, (o, i) in enumerate(prob):
        v = o['v'][:, vsl(i)]
        intra = jnp.concatenate([_dot(att[j][:, p * 2 * c:(p + 1) * 2 * c], bd_vals(v[:, p * 2 * GLA_DV:(p + 1) * 2 * GLA_DV]))
                                 for p in range(group // 2)], axis=1)
        outs.append(intra + _dot_nt(q_in[j], st[j]))
    for j, (o, i) in enumerate(prob):
        upd = _dot_tn(o['v'][:, vsl(i)], o['k_end'][:, ksl(i)])
        st_ref[j] = st[j] * o['decay'][:, ksl(i)] + jnp.where(own, upd, 0.0)
    of_ref[...] = jnp.concatenate(outs[:n_quads], axis=1)
    ob_ref[...] = jnp.concatenate(outs[n_quads:], axis=1)


def _gla(zg, zdec, dec_w_pad, dec_b, batch, nc_ctx, nc_tot):
    t = zg.shape[0]
    c = SEQ_CHUNK
    hk, hv = GLA_HEADS * GLA_DK, GLA_HEADS * GLA_DV

    def specs(d):
        rw = lambda b, s: b * nc_tot + _chunk_pos(s, d, nc_ctx, nc_tot)
        return [pl.BlockSpec((c, hk), lambda b, s: (rw(b, s), 0)),
                pl.BlockSpec((c, hk), lambda b, s: (rw(b, s), 1)),
                pl.BlockSpec((c, hv), lambda b, s: (rw(b, s), 1)),
                pl.BlockSpec((c, 128), lambda b, s: (rw(b, s), 0))], pl.BlockSpec((c, hv), lambda b, s: (rw(b, s), 0))

    in_f, out_f = specs(0)
    in_b, out_b = specs(1)
    return pl.pallas_call(
        _gla_kernel,
        grid=(batch, nc_tot),
        in_specs=in_f + in_b + [pl.BlockSpec(dec_w_pad.shape, lambda b, s: (0, 0, 0)),
                                pl.BlockSpec(dec_b.shape, lambda b, s: (0, 0, 0))],
        out_specs=[out_f, out_b],
        out_shape=[jax.ShapeDtypeStruct((t, hv), F32)] * 2,
        scratch_shapes=[pltpu.VMEM((2 * GLA_HEADS // 4, 4 * GLA_DV, 4 * GLA_DK), F32)],
        compiler_params=_cparams(("parallel", "arbitrary")),
        name="gla_scan",
    )(zg, zg, zg, zdec, zg, zg, zg, zdec, dec_w_pad, dec_b)


def _rope_swap(x):
    n = x.shape[-1]
    lane = lax.broadcasted_iota(jnp.int32, x.shape, x.ndim - 1)
    half = HEAD_DIM // 2
    return jnp.where(lane % HEAD_DIM < half, pltpu.roll(x, n - half, x.ndim - 1), pltpu.roll(x, half, x.ndim - 1))


def _qk_prep_kernel(q_ref, kv_ref, cos_ref, sin_ref, qg_ref, kg_ref, bdq_ref, bdk_ref, qo_ref, ko_ref, vo_ref):
    def norm_rope(x, gain, bd, n_heads):
        ms = _dot(x * x, bd) * (1.0 / HEAD_DIM)
        xn = x * lax.rsqrt(ms + NORM_EPS) * gain
        cos = jnp.concatenate([cos_ref[...]] * n_heads, axis=1)
        sin = jnp.concatenate([sin_ref[...]] * n_heads, axis=1)
        return xn * cos + _rope_swap(xn) * sin

    q = norm_rope(q_ref[...], qg_ref[...], bdq_ref[...], ATT_HEADS)
    qo_ref[...] = (q * (HEAD_DIM ** -0.5)).astype(BF16)
    kw = ATT_KV_HEADS * HEAD_DIM
    kv = kv_ref[...]
    ko_ref[...] = norm_rope(kv[:, :kw], kg_ref[...], bdk_ref[...], ATT_KV_HEADS).astype(BF16)
    vo_ref[...] = kv[:, kw:].astype(BF16)


def _qk_prep(zq, zkv, cos64, sin64, q_gain, k_gain, tm, l_tot):
    t = zq.shape[0]
    qw, kw = ATT_HEADS * HEAD_DIM, ATT_KV_HEADS * HEAD_DIM
    n_pos = l_tot // tm
    return pl.pallas_call(
        _qk_prep_kernel,
        grid=(t // tm,),
        in_specs=[pl.BlockSpec((tm, qw), lambda i: (i, 0)),
                  pl.BlockSpec((tm, 2 * kw), lambda i: (i, 0)),
                  pl.BlockSpec((tm, HEAD_DIM), lambda i: (i % n_pos, 0)),
                  pl.BlockSpec((tm, HEAD_DIM), lambda i: (i % n_pos, 0)),
                  pl.BlockSpec((1, qw), lambda i: (0, 0)),
                  pl.BlockSpec((1, kw), lambda i: (0, 0)),
                  pl.BlockSpec((qw, qw), lambda i: (0, 0)),
                  pl.BlockSpec((kw, kw), lambda i: (0, 0))],
        out_specs=[pl.BlockSpec((tm, qw), lambda i: (i, 0)),
                   pl.BlockSpec((tm, kw), lambda i: (i, 0)),
                   pl.BlockSpec((tm, kw), lambda i: (i, 0))],
        out_shape=[jax.ShapeDtypeStruct((t, qw), BF16),
                   jax.ShapeDtypeStruct((t, kw), BF16),
                   jax.ShapeDtypeStruct((t, kw), BF16)],
        compiler_params=_cparams(("parallel",)),
        name="qk_prep",
    )(zq, zkv, cos64, sin64, jnp.tile(q_gain, ATT_HEADS).reshape(1, qw), jnp.tile(k_gain, ATT_KV_HEADS).reshape(1, kw),
      _block_ones(ATT_HEADS, HEAD_DIM), _block_ones(ATT_KV_HEADS, HEAD_DIM))


def _attn_kernel(q_ref, kp_ref, kc_ref, kn_ref, kx_ref, vp_ref, vc_ref, vn_ref, vx_ref, sink_ref, o_ref,
                 *, n_ctx_blocks, n_lat_blocks):
    blk = ATT_BLOCK
    n = pl.program_id(1)
    m = n - n_ctx_blocks
    is_lat = n >= n_ctx_blocks
    l_ctx = kx_ref.shape[0]
    width = 3 * blk + l_ctx
    rows = ATT_GROUP * blk
    r = lax.broadcasted_iota(jnp.int32, (rows, width), 0) % blk
    c = lax.broadcasted_iota(jnp.int32, (rows, width), 1)
    lat = is_lat.astype(jnp.int32)
    has_prev = lat * (m >= 1).astype(jnp.int32)
    has_next = lat * (m <= n_lat_blocks - 2).astype(jnp.int32)
    valid = jnp.where(c < blk, (c >= r).astype(jnp.int32) * has_prev,
                      jnp.where(c < 2 * blk, lat,
                                jnp.where(c < 3 * blk, (c - 2 * blk <= r).astype(jnp.int32) * has_next, 1))) > 0
    q = q_ref[...]
    sink = sink_ref[...]
    outs = [None] * ATT_HEADS
    for kvh in range(ATT_KV_HEADS):
        ks = slice(kvh * HEAD_DIM, (kvh + 1) * HEAD_DIM)
        kw = jnp.concatenate([kp_ref[:, ks], kc_ref[:, ks], kn_ref[:, ks], kx_ref[:, ks]], axis=0)
        vw = jnp.concatenate([vp_ref[:, ks], vc_ref[:, ks], vn_ref[:, ks], vx_ref[:, ks]], axis=0)
        heads = range(kvh * ATT_GROUP, (kvh + 1) * ATT_GROUP)
        qg = jnp.concatenate([q[:, h * HEAD_DIM:(h + 1) * HEAD_DIM] for h in heads], axis=0)
        s = jnp.where(valid, _dot_nt(qg, kw), -jnp.inf)
        sk = jnp.concatenate([jnp.broadcast_to(sink[h:h + 1, 0:1], (blk, 1)) for h in heads], axis=0)
        mx = jnp.maximum(jnp.max(s, axis=-1, keepdims=True), sk)
        p = jnp.exp(s - mx)
        denom = jnp.sum(p, axis=-1, keepdims=True) + jnp.exp(sk - mx)
        o = _dot(p.astype(BF16), vw) / denom
        for g, h in enumerate(heads):
            outs[h] = o[g * blk:(g + 1) * blk]
    o_ref[...] = jnp.concatenate(outs, axis=1)


def _attention(qn, kn, vn, sink, batch, l_ctx, l_tot):
    t = qn.shape[0]
    blk = ATT_BLOCK
    nq = l_tot // blk
    nc = l_ctx // blk
    nl = nq - nc
    qw, kw = ATT_HEADS * HEAD_DIM, ATT_KV_HEADS * HEAD_DIM

    def win(off):
        def index(b, n):
            m = jnp.clip(n - nc + off, 0, nl - 1)
            return (b * nq + nc + m, 0)
        return pl.BlockSpec((blk, kw), index)

    ctx_spec = pl.BlockSpec((l_ctx, kw), lambda b, n: (b * (l_tot // l_ctx), 0))
    return pl.pallas_call(
        functools.partial(_attn_kernel, n_ctx_blocks=nc, n_lat_blocks=nl),
        grid=(batch, nq),
        in_specs=[pl.BlockSpec((blk, qw), lambda b, n: (b * nq + n, 0)),
                  win(-1), win(0), win(1), ctx_spec,
                  win(-1), win(0), win(1), ctx_spec,
                  pl.BlockSpec((ATT_HEADS, 128), lambda b, n: (0, 0))],
        out_specs=pl.BlockSpec((blk, qw), lambda b, n: (b * nq + n, 0)),
        out_shape=jax.ShapeDtypeStruct((t, qw), F32),
        compiler_params=_cparams(("parallel", "parallel")),
        name="window_attention",
    )(qn, kn, kn, kn, kn, vn, vn, vn, vn, jnp.broadcast_to(sink.astype(F32)[:, None], (ATT_HEADS, 128)))


def _route(logits_t, bias_col):
    scores = jax.nn.sigmoid(logits_t)
    sel = scores + bias_col
    rows = [sel[e:e + 1] for e in range(N_EXPERTS)]
    grp = []
    for g in range(N_GROUPS):
        r = rows[g * PER_GROUP:(g + 1) * PER_GROUP]
        best = None
        for i in range(PER_GROUP):
            for j in range(i + 1, PER_GROUP):
                pair = r[i] + r[j]
                best = pair if best is None else jnp.maximum(best, pair)
        grp.append(best)
    g_best = jnp.zeros_like(grp[0], dtype=jnp.int32)
    g_val = grp[0]
    for g in range(1, N_GROUPS):
        take = grp[g] > g_val
        g_best = jnp.where(take, g, g_best)
        g_val = jnp.where(take, grp[g], g_val)
    neg = -jnp.inf
    masked = [jnp.where(g_best == e // PER_GROUP, rows[e], neg) for e in range(N_EXPERTS)]
    i1 = jnp.zeros_like(g_best)
    v1 = masked[0]
    for e in range(1, N_EXPERTS):
        take = masked[e] > v1
        i1 = jnp.where(take, e, i1)
        v1 = jnp.where(take, masked[e], v1)
    i2 = jnp.full_like(g_best, -1)
    v2 = jnp.full_like(v1, neg)
    for e in range(N_EXPERTS):
        take = jnp.logical_and(i1 != e, masked[e] > v2)
        i2 = jnp.where(take, e, i2)
        v2 = jnp.where(take, masked[e], v2)
    w1 = jnp.zeros_like(v1)
    w2 = jnp.zeros_like(v1)
    for e in range(N_EXPERTS):
        w1 = jnp.where(i1 == e, scores[e:e + 1], w1)
        w2 = jnp.where(i2 == e, scores[e:e + 1], w2)
    inv = 1.0 / (w1 + w2)
    pad = jnp.zeros_like(w1)
    lane = lax.broadcasted_iota(jnp.int32, w1.shape, 1)
    hist = pad
    for e in range(N_EXPERTS):
        n_e = jnp.sum((i1 == e).astype(F32) + (i2 == e).astype(F32), axis=1, keepdims=True)
        hist = jnp.where(lane == e, n_e, hist)
    return jnp.concatenate([i1.astype(F32), i2.astype(F32), w1 * inv, w2 * inv, hist, pad, pad, pad], axis=0)


def _residual_and_route(x, out, m, ffn_gain, rw_ref, rb_ref, x_ref, h_ref, g_ref):
    x1 = x + m[2:3] * out
    x_ref[...] = x1
    h = _modulated_norm(x1, ffn_gain, m[3:4], m[4:5])
    _store_row_tiles(h_ref, h)
    h_hi = h.astype(BF16)
    h_lo = (h - h_hi.astype(F32)).astype(BF16)
    logits = _dot(h_hi, rw_ref[0]) + _dot(h_lo, rw_ref[0]) + _dot(h_hi, rw_ref[1])
    g_ref[...] = _route(logits.T[:N_EXPERTS], rb_ref[...])


def _merge_even_kernel(o0_ref, o1_ref, gg_ref, oa_ref, x_ref, mod_ref, gn_ref, bd_ref, w1_ref, w2_ref,
                       fg_ref, rw_ref, rb_ref, xo_ref, ho_ref, go_ref):
    og = o0_ref[...] + o1_ref[...]
    ms = _dot(og * og, bd_ref[...]) * (1.0 / GLA_DV)
    g = og * lax.rsqrt(ms + NORM_EPS) * gn_ref[...] * _silu(gg_ref[...])
    out = _dot(g.astype(BF16), w1_ref[...]) + _dot(oa_ref[...].astype(BF16), w2_ref[...])
    _residual_and_route(x_ref[...], out, mod_ref[0], fg_ref[...], rw_ref, rb_ref, xo_ref, ho_ref, go_ref)


def _merge_even(o_f, o_b, zg, o_att, x, mods, out_norm, w_out, ffn_gain, router_wt, router_b, tm, mod_row):
    t, d = x.shape
    hv = GLA_HEADS * GLA_DV
    qw = ATT_HEADS * HEAD_DIM
    full = lambda a: pl.BlockSpec(a.shape, lambda i: (0,) * a.ndim)
    gn = jnp.tile(out_norm, GLA_HEADS).reshape(1, hv)
    bd = _block_ones(GLA_HEADS, GLA_DV)
    w1, w2 = w_out[:hv].astype(BF16), w_out[hv:].astype(BF16)
    fg = ffn_gain.reshape(1, d)
    rb = router_b.reshape(N_EXPERTS, 1)
    return pl.pallas_call(
        _merge_even_kernel,
        grid=(t // tm,),
        in_specs=[pl.BlockSpec((tm, hv), lambda i: (i, 0)),
                  pl.BlockSpec((tm, hv), lambda i: (i, 0)),
                  pl.BlockSpec((tm, hv), lambda i: (i, 2)),
                  pl.BlockSpec((tm, qw), lambda i: (i, 0)),
                  pl.BlockSpec((tm, d), lambda i: (i, 0)),
                  pl.BlockSpec((1, 6, d), lambda i: (mod_row(i), 0, 0)),
                  full(gn), full(bd), full(w1), full(w2), full(fg), full(router_wt), full(rb)],
        out_specs=[pl.BlockSpec((tm, d), lambda i: (i, 0)),
                   pl.BlockSpec((tm * (d // LANES), LANES), lambda i: (i, 0)),
                   pl.BlockSpec((8, tm), lambda i: (0, i))],
        out_shape=[jax.ShapeDtypeStruct((t, d), F32),
                   jax.ShapeDtypeStruct((t * (d // LANES), LANES), F32),
                   jax.ShapeDtypeStruct((8, t), F32)],
        compiler_params=_cparams(("parallel",)),
        name="merge_even",
    )(o_f, o_b, zg, o_att, x, mods, gn, bd, w1, w2, fg, router_wt, rb)


def _moe_plan(route, t, rows, route_tile):
    n_tiles = 2 * t // rows + N_EXPERTS
    eid = jnp.concatenate([route[0], route[1]]).astype(jnp.int32)
    slot = jnp.arange(2 * t, dtype=jnp.int32)
    gate = jnp.concatenate([route[2], route[3]])
    _, s_slot, s_gate = lax.sort((eid, slot, gate), num_keys=1, is_stable=True)
    counts = jnp.sum(route[4].reshape(t // route_tile, route_tile)[:, :N_EXPERTS], axis=0).astype(jnp.int32)
    padded = (counts + rows - 1) // rows * rows
    p_end = jnp.cumsum(padded)
    p_start = p_end - padded
    c_start = jnp.cumsum(counts) - counts
    tile_start = jnp.arange(n_tiles, dtype=jnp.int32) * rows
    tile_e = jnp.minimum(jnp.sum((tile_start[:, None] >= p_end[None, :]).astype(jnp.int32), axis=1), N_EXPERTS - 1)
    r_in = (tile_start - p_start[tile_e])[:, None] + jnp.arange(rows, dtype=jnp.int32)[None, :]
    over = r_in - counts[tile_e][:, None]
    valid = over < 0
    src = jnp.clip(c_start[tile_e][:, None] + r_in, 0, 2 * t - 1)
    g_slot = s_slot[src]
    row_tok = jnp.where(valid, jnp.where(g_slot >= t, g_slot - t, g_slot), 0)
    row_gate = jnp.where(valid, s_gate[src], 0.0)
    row_dst = jnp.where(valid, g_slot, 2 * t + tile_e[:, None] * rows + jnp.clip(over, 0, rows - 1))
    n_used = (p_end[-1] // rows).astype(jnp.int32)
    tile_e = jnp.where(tile_start < p_end[-1], tile_e, tile_e[jnp.maximum(n_used - 1, 0)])
    return (row_tok[:, None, :], row_gate[:, None, :], row_dst[:, None, :], tile_e, n_used.reshape(1))


def _moe_experts_kernel(te_ref, nu_ref, tok0_ref, tok1_ref, tokn_ref, dst_ref, gate_ref, wg_ref, wu_ref, wd_ref, h_hbm,
                        y_hbm, hbuf, ybuf, wgb, wub, wdb, sem_g, sem_s):
    j = pl.program_id(0)
    n_used = nu_ref[0]
    slot = j % 2
    g = SUBLANES
    rows = ybuf.shape[0] // (2 * g)
    tile = rows * g
    depth = hbuf.shape[0] // tile
    gslot = lax.rem(j, depth)

    def row(ref, i):
        return ref.at[pl.ds(pl.multiple_of(i * g, g), g)]

    def start_gather(idx_ref, s):
        for r in range(rows):
            pltpu.make_async_copy(row(h_hbm, idx_ref[0, 0, r]), row(hbuf, s * rows + r), sem_g.at[s]).start(priority=r % 2)

    def wait_gather(s):
        pltpu.make_async_copy(h_hbm.at[pl.ds(0, tile)], hbuf.at[pl.ds(pl.multiple_of(s * tile, tile), tile)],
                              sem_g.at[s]).wait()

    def wait_scatter(s):
        pltpu.make_async_copy(ybuf.at[pl.ds(pl.multiple_of(s * tile, tile), tile)], y_hbm.at[pl.ds(0, tile)],
                              sem_s.at[s]).wait()

    @pl.when(j == 0)
    def _():
        start_gather(tok0_ref, 0)
        start_gather(tok1_ref, 1)
        ybuf[...] = jnp.zeros(ybuf.shape, F32)
        n_real = y_hbm.shape[0] // g - N_EXPERTS * rows
        fills = [pltpu.make_async_copy(ybuf.at[pl.ds((k % 2) * tile, tile)],
                                       y_hbm.at[pl.ds((n_real + k * rows) * g, tile)], sem_s.at[k % 2])
                 for k in range(N_EXPERTS)]
        for f in fills:
            f.start()
        for f in fills[2:]:
            f.wait()

    active = j < n_used
    changed = jnp.logical_or(j == 0, te_ref[j] != te_ref[jnp.maximum(j - 1, 0)])

    @pl.when(jnp.logical_and(active, changed))
    def _():
        wgb[...] = wg_ref[0].astype(BF16)
        wub[...] = wu_ref[0].astype(BF16)
        wdb[...] = wd_ref[0].astype(BF16)

    @pl.when(active)
    def _():
        wait_gather(gslot)
        start_gather(tokn_ref, lax.rem(j + depth - 1, depth))
        h = _load_row_tiles(hbuf, rows, g, first_row=gslot * rows).astype(BF16)
        act = _silu(_dot(h, wgb[...])) * _dot(h, wub[...])
        ri = lax.broadcasted_iota(jnp.int32, (rows, rows), 0)
        ci = lax.broadcasted_iota(jnp.int32, (rows, rows), 1)
        gate = jnp.sum(jnp.where(ri == ci, gate_ref[0], 0.0), axis=1, keepdims=True)
        y = _dot((act * gate).astype(BF16), wdb[...])
        wait_scatter(slot)
        _store_row_tiles(ybuf, y, first_row=slot * rows)
        for r in range(rows):
            pltpu.make_async_copy(row(ybuf, slot * rows + r), row(y_hbm, dst_ref[0, 0, r]), sem_s.at[slot]).start(priority=r % 2)

        @pl.when(j == n_used - 1)
        def _():
            wait_scatter(slot)
            wait_scatter(1 - slot)
            for ahead in range(1, depth):
                wait_gather(lax.rem(j + ahead, depth))


def _moe_combine_kernel(ya_ref, yb_ref, x_ref, mod_ref, o_ref):
    n, d = x_ref.shape
    y = _load_row_tiles(ya_ref, n, d // LANES) + _load_row_tiles(yb_ref, n, d // LANES)
    o_ref[...] = x_ref[...] + mod_ref[0][5:6] * y


def _moe(h, route, x, mods, w_gate, w_up, w_down, layer, tm, mod_row, route_tile):
    t, d = x.shape
    g = d // LANES
    assert g == SUBLANES
    rows = MOE_TILE
    row_tok, row_gate, row_dst, tile_e, n_used = _moe_plan(route, t, rows, route_tile)
    n_tiles = row_tok.shape[0]
    n_pair_rows = 2 * t + N_EXPERTS * rows
    assert MOE_GATHER_DEPTH == 3
    smem_tile = lambda which: pl.BlockSpec((1, 1, rows), lambda j, te, nu: (jnp.minimum(which(j), nu[0] - 1), 0, 0),
                                           memory_space=pltpu.SMEM)
    pairs = pl.pallas_call(
        _moe_experts_kernel,
        grid_spec=pltpu.PrefetchScalarGridSpec(
            num_scalar_prefetch=2,
            grid=(n_tiles,),
            in_specs=[smem_tile(lambda j: 0), smem_tile(lambda j: 1), smem_tile(lambda j: j + MOE_GATHER_DEPTH - 1),
                      smem_tile(lambda j: j),
                      pl.BlockSpec((1, 1, rows), lambda j, te, nu: (j, 0, 0)),
                      pl.BlockSpec((None, 1, d, D_EXPERT), lambda j, te, nu: (layer, te[j], 0, 0)),
                      pl.BlockSpec((None, 1, d, D_EXPERT), lambda j, te, nu: (layer, te[j], 0, 0)),
                      pl.BlockSpec((None, 1, D_EXPERT, d), lambda j, te, nu: (layer, te[j], 0, 0)),
                      pl.BlockSpec(memory_space=pl.ANY)],
            out_specs=pl.BlockSpec(memory_space=pl.ANY),
            scratch_shapes=[pltpu.VMEM((MOE_GATHER_DEPTH * rows * g, LANES), F32), pltpu.VMEM((2 * rows * g, LANES), F32),
                            pltpu.VMEM((d, D_EXPERT), BF16), pltpu.VMEM((d, D_EXPERT), BF16),
                            pltpu.VMEM((D_EXPERT, d), BF16),
                            pltpu.SemaphoreType.DMA((MOE_GATHER_DEPTH,)), pltpu.SemaphoreType.DMA((2,))]),
        out_shape=jax.ShapeDtypeStruct((n_pair_rows * g, LANES), F32),
        compiler_params=_cparams(("arbitrary",)),
        name="moe_experts",
    )(tile_e, n_used, row_tok, row_tok, row_tok, row_dst, row_gate, w_gate, w_up, w_down, h)
    return pl.pallas_call(
        _moe_combine_kernel,
        grid=(t // tm,),
        in_specs=[pl.BlockSpec((tm * g, LANES), lambda i: (i, 0)),
                  pl.BlockSpec((tm * g, LANES), lambda i: (t // tm + i, 0)),
                  pl.BlockSpec((tm, d), lambda i: (i, 0)),
                  pl.BlockSpec((1, 6, d), lambda i: (mod_row(i), 0, 0))],
        out_specs=pl.BlockSpec((tm, d), lambda i: (i, 0)),
        out_shape=jax.ShapeDtypeStruct((t, d), F32),
        compiler_params=_cparams(("parallel",)),
        name="moe_combine",
    )(pairs, pairs, x, mods)


def _chan_dft_kernel(z_ref, w_ref, o_ref):
    res = _dot(z_ref[...].astype(BF16), w_ref[...]).astype(BF16)
    o_ref[0] = res[:, :FOURIER_WIDTH]
    o_ref[1] = res[:, FOURIER_WIDTH:]


def _chan_dft(zf, w, batch, l_ctx, l_lat, tm):
    l_tot = l_ctx + l_lat
    nt = l_lat // tm
    fw = FOURIER_WIDTH
    return pl.pallas_call(
        _chan_dft_kernel,
        grid=(batch, nt),
        in_specs=[pl.BlockSpec((tm, fw), lambda b, i: (b * (l_tot // tm) + l_ctx // tm + i, 0)),
                  pl.BlockSpec(w.shape, lambda b, i: (0, 0))],
        out_specs=pl.BlockSpec((2, tm, fw), lambda b, i: (0, i, b)),
        out_shape=jax.ShapeDtypeStruct((2, l_lat, batch * fw), BF16),
        compiler_params=_cparams(("parallel", "parallel")),
        name="fourier_channels",
    )(zf, w)


def _seq_dft_kernel(ca_ref, sa_ref, cb_ref, sb_ref, z_ref, o_ref, acc_ref):
    k = pl.program_id(1)
    tk = z_ref.shape[1]
    sub = DFT_SPLIT
    n_a = ca_ref.shape[1]

    @pl.when(k == 0)
    def _():
        acc_ref[...] = jnp.zeros_like(acc_ref)

    col_a = k * tk + lax.broadcasted_iota(jnp.int32, (n_a, tk), 1)
    pick_a = (col_a // sub == lax.broadcasted_iota(jnp.int32, (n_a, tk), 0)).astype(F32)
    pick_b = (lax.broadcasted_iota(jnp.int32, (sub, tk), 1) % sub == lax.broadcasted_iota(jnp.int32, (sub, tk), 0)).astype(F32)
    ca, sa = _dot(ca_ref[...], pick_a), _dot(sa_ref[...], pick_a)
    cb, sb = _dot(cb_ref[...], pick_b), _dot(sb_ref[...], pick_b)
    cos_t = (ca * cb - sa * sb).astype(BF16)
    sin_t = (sa * cb + ca * sb).astype(BF16)
    acc_ref[...] += _dot(cos_t, z_ref[0]) + _dot(sin_t, z_ref[1])

    @pl.when(k == pl.num_programs(1) - 1)
    def _():
        o_ref[...] = acc_ref[...]


def _seq_dft(tables, zc, batch, tm, tk):
    ca, sa, cb, sb = tables
    l = ca.shape[0]
    fw = FOURIER_WIDTH
    small = lambda a: pl.BlockSpec((tm, a.shape[1]), lambda i, k: (i, 0))
    return pl.pallas_call(
        _seq_dft_kernel,
        grid=(l // tm, l // tk),
        in_specs=[small(ca), small(sa), small(cb), small(sb),
                  pl.BlockSpec((2, tk, batch * fw), lambda i, k: (0, k, 0))],
        out_specs=pl.BlockSpec((tm, batch * fw), lambda i, k: (i, 0)),
        out_shape=jax.ShapeDtypeStruct((l, batch * fw), F32),
        scratch_shapes=[pltpu.VMEM((tm, batch * fw), F32)],
        compiler_params=_cparams(("parallel", "arbitrary")),
        name="fourier_sequence",
    )(ca, sa, cb, sb, zc)


def _dft_tables(l):
    m = jnp.arange(l, dtype=jnp.int32)[:, None]
    n1 = l // DFT_SPLIT
    a = (m * (jnp.arange(n1, dtype=jnp.int32)[None, :] * DFT_SPLIT)) % l
    b = (m * jnp.arange(DFT_SPLIT, dtype=jnp.int32)[None, :]) % l
    wa = a.astype(F32) * (2.0 * np.pi / l)
    wb = b.astype(F32) * (2.0 * np.pi / l)
    tables = (jnp.cos(wa), jnp.sin(wa), jnp.cos(wb), jnp.sin(wb))
    gd = FOURIER_GROUP_DIM
    cc = (jnp.arange(gd, dtype=jnp.int32)[:, None] * jnp.arange(gd, dtype=jnp.int32)[None, :]) % gd
    wc = cc.astype(F32) * (2.0 * np.pi / gd)
    scale = 1.0 / np.sqrt(float(l) * gd)
    eye = jnp.eye(FOURIER_WIDTH // gd, dtype=F32)
    chan = jnp.concatenate([jnp.kron(eye, jnp.cos(wc)), -jnp.kron(eye, jnp.sin(wc))], axis=1) * scale
    return tables, chan.astype(BF16)


def _rwkv_prepare(d, pos, z_ref, zp_ref, zn_ref, par, nc_ctx, nc_tot):
    mu, kks, ka, rk, w0, w2, a0, a2, bd = par
    c = z_ref.shape[0]
    n = RWKV_DIM
    seg_first = jnp.logical_or(pos == 0, pos == nc_ctx)
    seg_last = jnp.logical_or(pos == nc_ctx - 1, pos == nc_tot - 1)
    z = z_ref[...]
    row = lax.broadcasted_iota(jnp.int32, z.shape, 0)
    prev_row = jnp.where(seg_first, 0.0, zp_ref[7:8, :])
    next_row = jnp.where(seg_last, 0.0, zn_ref[0:1, :])
    z_prev = jnp.where(row == 0, prev_row, pltpu.roll(z, 1, 0))
    z_next = jnp.where(row == c - 1, next_row, pltpu.roll(z, c - 1, 0))
    zs = z + mu * (0.5 * (z_prev + z_next) - z)

    r, k, v = zs[:, 0:n], zs[:, n:2 * n], zs[:, 2 * n:3 * n]
    zw = zs[:, 3 * n:3 * n + RWKV_RANK_PAD]
    za = zs[:, 3 * n + RWKV_RANK_PAD:3 * n + 2 * RWKV_RANK_PAD]
    zg = zs[:, 3 * n + 2 * RWKV_RANK_PAD:3 * n + 3 * RWKV_RANK_PAD]

    kk = k * kks
    kk = kk * lax.rsqrt(_dot(kk * kk, bd) + L2_EPS)
    w_log = _log_sigmoid(w0[d] + _dot(jnp.tanh(zw), w2[d])) - 0.5
    lw = -jnp.exp(w_log)
    a = jax.nn.sigmoid(a0[d] + _dot(za, a2[d]))
    kd = k * (1.0 + (a - 1.0) * ka)
    beta = kk * a

    cl = _dot((_visit_order(d, c) >= 0).astype(F32), lw)
    c_tot = jnp.sum(lw, axis=0, keepdims=True)
    grow = jnp.exp(-cl)
    tail = jnp.exp(c_tot - cl)
    ops = dict(k_s=kd * grow,
               b_s=beta * grow,
               kap_s=kk * jnp.exp(cl - lw),
               r_s=r * jnp.exp(cl),
               k_e=kd * tail,
               b_e=beta * tail,
               gam=jnp.exp(c_tot), v=v,
               bonus=_dot(r * kd * rk, bd),
               sign=1 if d == 0 else -1)
    return ops, zg


def _rwkv_kernel(zf_ref, zfp_ref, zfn_ref, zb_ref, zbp_ref, zbn_ref, mu_ref, kks_ref, ka_ref, rk_ref, w0_ref, w2_ref,
                 a0_ref, a2_ref, g2_ref, bd_ref, yf_ref, yb_ref, gate_ref, st_ref, *, nc_ctx, nc_tot):
    s = pl.program_id(1)
    c = zf_ref.shape[0]
    hd = HEAD_DIM
    pw = 2 * hd

    @pl.when(s == 0)
    def _():
        st_ref[...] = jnp.zeros_like(st_ref)

    par = (mu_ref[...], kks_ref[...], ka_ref[...], rk_ref[...], w0_ref, w2_ref, a0_ref, a2_ref, bd_ref[...])
    fwd, zg = _rwkv_prepare(0, _chunk_pos(s, 0, nc_ctx, nc_tot), zf_ref, zfp_ref, zfn_ref, par, nc_ctx, nc_tot)
    bwd, _ = _rwkv_prepare(1, _chunk_pos(s, 1, nc_ctx, nc_tot), zb_ref, zbp_ref, zbn_ref, par, nc_ctx, nc_tot)
    gate_ref[...] = _dot(jax.nn.sigmoid(zg), g2_ref[...])

    lane = lax.broadcasted_iota(jnp.int32, (c, pw), 1)
    rowi = lax.broadcasted_iota(jnp.int32, (c, pw), 0)
    left = lane < hd
    eye_p = (rowi == lane % hd).astype(F32)
    same_head = (lax.broadcasted_iota(jnp.int32, (pw, pw), 0) < hd) == (lax.broadcasted_iota(jnp.int32, (pw, pw), 1) < hd)

    def bd(y):
        return jnp.concatenate([jnp.where(left, y, 0.0), jnp.where(left, 0.0, y)], axis=0)

    n_pairs = RWKV_HEADS // 2
    prob = [(o, slice(i * pw, (i + 1) * pw)) for o in (fwd, bwd) for i in range(n_pairs)]
    ahead = [(rowi - lane % hd) * o['sign'] for o, _ in prob]
    get = lambda name: [o[name][:, s_] for o, s_ in prob]
    kap, r_s, k_s, b_s, k_e, b_e, vp, gam, bonus = (get(x) for x in ('kap_s', 'r_s', 'k_s', 'b_s', 'k_e', 'b_e', 'v', 'gam', 'bonus'))
    p = [_dot_nt(jnp.concatenate([a_, b_], axis=0), jnp.concatenate([bd(c_), bd(d_)], axis=0))
         for a_, b_, c_, d_ in zip(kap, r_s, k_s, b_s)]
    m1 = [jnp.where(h_ > 0, x[0:c, 0:pw], 0.0) for x, h_ in zip(p, ahead)]
    m2 = [jnp.where(h_ > 0, x[0:c, pw:2 * pw], 0.0) for x, h_ in zip(p, ahead)]
    n1 = [jnp.where(h_ >= 0, x[c:2 * c, 0:pw], 0.0) for x, h_ in zip(p, ahead)]
    n2 = [jnp.where(h_ >= 0, x[c:2 * c, pw:2 * pw], 0.0) for x, h_ in zip(p, ahead)]
    m1v = [_dot(a_, bd(b_)) for a_, b_ in zip(m1, vp)]
    t_inv = [eye_p - x for x in m2]
    q = [_dot(x, bd(x)) for x in m2]
    span = 2
    while 2 * span < c:
        both = [_dot(jnp.concatenate([t_, q_], axis=0), bd(q_)) for t_, q_ in zip(t_inv, q)]
        t_inv = [t_ + x[0:c] for t_, x in zip(t_inv, both)]
        q = [x[c:2 * c] for x in both]
        span *= 2
    t_inv = [t_ + _dot(t_, bd(q_)) for t_, q_ in zip(t_inv, q)]
    tx = [_dot(t_, jnp.concatenate([bd(a_), bd(mv)], axis=1)) for t_, a_, mv in zip(t_inv, kap, m1v)]
    st = [st_ref[i] for i in range(len(prob))]
    su = [_dot_nt(jnp.concatenate([x[:, 0:pw], r_], axis=0), s0) for x, r_, s0 in zip(tx, r_s, st)]
    u = [x[0:c] + y_[:, pw:2 * pw] for x, y_ in zip(su, tx)]
    ys = [x[c:2 * c] + _dot(jnp.concatenate([a_, -b_], axis=1), jnp.concatenate([bd(v_), bd(u_)], axis=0)) + bo * v_
          for x, a_, b_, v_, u_, bo in zip(su, n1, n2, vp, u, bonus)]
    for i in range(len(prob)):
        upd = _dot_tn(jnp.concatenate([vp[i], u[i]], axis=0), jnp.concatenate([k_e[i], -b_e[i]], axis=0))
        st_ref[i] = st[i] * gam[i] + jnp.where(same_head, upd, 0.0)
    yf_ref[...] = jnp.concatenate(ys[:n_pairs], axis=1)
    yb_ref[...] = jnp.concatenate(ys[n_pairs:], axis=1)


def _rwkv(zr, mu, kk_scale, k_a, r_k, w0, w2_pad, a0, a2_pad, g2, batch, nc_ctx, nc_tot):
    t, zw_ = zr.shape
    c = SEQ_CHUNK
    n = RWKV_DIM
    bd = _block_ones(RWKV_HEADS, HEAD_DIM)
    full = lambda a: pl.BlockSpec(a.shape, lambda b, s: (0,) * a.ndim)
    sub = c // 8
    n_sub = t // 8

    def rows(d):
        return lambda b, s: b * nc_tot + _chunk_pos(s, d, nc_ctx, nc_tot)

    def z_specs(d):
        rw = rows(d)
        return [pl.BlockSpec((c, zw_), lambda b, s: (rw(b, s), 0)),
                pl.BlockSpec((8, zw_), lambda b, s: (jnp.maximum(rw(b, s) * sub - 1, 0), 0)),
                pl.BlockSpec((8, zw_), lambda b, s: (jnp.minimum((rw(b, s) + 1) * sub, n_sub - 1), 0))]

    vec = lambda a: a.reshape(1, -1)
    args = (vec(mu), vec(kk_scale), vec(k_a), vec(r_k), w0.reshape(2, 1, n), w2_pad, a0.reshape(2, 1, n), a2_pad, g2, bd)
    out = lambda d: pl.BlockSpec((c, n), lambda b, s: (rows(d)(b, s), 0))
    return pl.pallas_call(
        functools.partial(_rwkv_kernel, nc_ctx=nc_ctx, nc_tot=nc_tot),
        grid=(batch, nc_tot),
        in_specs=z_specs(0) + z_specs(1) + [full(a) for a in args],
        out_specs=[out(0), out(1), out(0)],
        out_shape=[jax.ShapeDtypeStruct((t, n), F32)] * 3,
        scratch_shapes=[pltpu.VMEM((RWKV_HEADS, 2 * HEAD_DIM, 2 * HEAD_DIM), F32)],
        compiler_params=_cparams(("parallel", "arbitrary")),
        name="rwkv_scan",
    )(zr, zr, zr, zr, zr, zr, *args)


def _merge_odd_kernel(y0_ref, y1_ref, gate_ref, fo_ref, x_ref, mod_ref, lg_ref, lb_ref, bd_ref, w1_ref, w2_ref,
                      fg_ref, rw_ref, rb_ref, xo_ref, ho_ref, go_ref):
    y = y0_ref[...] + y1_ref[...]
    bd = bd_ref[...]
    mean = _dot(y, bd) * (1.0 / HEAD_DIM)
    yc = y - mean
    var = _dot(yc * yc, bd) * (1.0 / HEAD_DIM)
    rw = (yc * lax.rsqrt(var + RWKV_GN_EPS) * lg_ref[...] + lb_ref[...]) * gate_ref[...]
    out = _dot(fo_ref[...].astype(BF16), w1_ref[...]) + _dot(rw.astype(BF16), w2_ref[...])
    _residual_and_route(x_ref[...], out, mod_ref[0], fg_ref[...], rw_ref, rb_ref, xo_ref, ho_ref, go_ref)


def _merge_odd(y0, y1, gate, fo, x, mods, ln_g, ln_b, w_out, ffn_gain, router_wt, router_b, tm, batch, l_ctx, l_lat):
    d = x.shape[1]
    n = RWKV_DIM
    fw = FOURIER_WIDTH
    l_tot = l_ctx + l_lat
    nt = l_lat // tm
    t_out = batch * l_lat
    full = lambda a: pl.BlockSpec(a.shape, lambda b, i: (0,) * a.ndim)
    src = lambda b, i: b * (l_tot // tm) + l_ctx // tm + i
    bd = _block_ones(RWKV_HEADS, HEAD_DIM)
    w1, w2 = w_out[:fw].astype(BF16), w_out[fw:].astype(BF16)
    lg, lb, fg, rb = ln_g.reshape(1, n), ln_b.reshape(1, n), ffn_gain.reshape(1, d), router_b.reshape(N_EXPERTS, 1)
    return pl.pallas_call(
        _merge_odd_kernel,
        grid=(batch, nt),
        in_specs=[pl.BlockSpec((tm, n), lambda b, i: (src(b, i), 0)),
                  pl.BlockSpec((tm, n), lambda b, i: (src(b, i), 0)),
                  pl.BlockSpec((tm, n), lambda b, i: (src(b, i), 0)),
                  pl.BlockSpec((tm, fw), lambda b, i: (i, b)),
                  pl.BlockSpec((tm, d), lambda b, i: (src(b, i), 0)),
                  pl.BlockSpec((1, 6, d), lambda b, i: (b, 0, 0)),
                  full(lg), full(lb), full(bd), full(w1), full(w2), full(fg), full(router_wt), full(rb)],
        out_specs=[pl.BlockSpec((tm, d), lambda b, i: (b * nt + i, 0)),
                   pl.BlockSpec((tm * (d // LANES), LANES), lambda b, i: (b * nt + i, 0)),
                   pl.BlockSpec((8, tm), lambda b, i: (0, b * nt + i))],
        out_shape=[jax.ShapeDtypeStruct((t_out, d), F32),
                   jax.ShapeDtypeStruct((t_out * (d // LANES), LANES), F32),
                   jax.ShapeDtypeStruct((8, t_out), F32)],
        compiler_params=_cparams(("parallel", "parallel")),
        name="merge_odd",
    )(y0, y1, gate, fo, x, mods, lg, lb, bd, w1, w2, fg, router_wt, rb)


def _rope_tables(l_ctx, l_lat):
    rows = l_lat // GRID_W
    row = jnp.repeat(jnp.arange(rows, dtype=F32), GRID_W)
    col = jnp.tile(jnp.arange(GRID_W, dtype=F32), rows)
    n_freq = HEAD_DIM // 4
    inv_freq = ROPE_THETA ** (-jnp.arange(n_freq, dtype=F32) / n_freq)
    ang = jnp.concatenate([row[:, None] * inv_freq, col[:, None] * inv_freq], axis=-1)
    cos, sin = jnp.cos(ang), jnp.sin(ang)
    cos64 = jnp.concatenate([cos, cos], axis=1)
    sin64 = jnp.concatenate([-sin, sin], axis=1)
    cos64 = jnp.concatenate([jnp.ones((l_ctx, HEAD_DIM), F32), cos64], axis=0)
    sin64 = jnp.concatenate([jnp.zeros((l_ctx, HEAD_DIM), F32), sin64], axis=0)
    return cos64, sin64


def _pad_rank(w):
    _, r, n = w.shape
    out = jnp.zeros((2, RWKV_RANK_PAD, n), w.dtype)
    out = out.at[0, 0:r].set(w[0])
    return out.at[1, r:2 * r].set(w[1])


def _even_layer(x, mods, p, batch, l_ctx, l_lat, tm, mod_row, tm_moe):
    l_tot = l_ctx + l_lat
    d = x.shape[1]
    nc_ctx, nc_tot = l_ctx // SEQ_CHUNK, l_tot // SEQ_CHUNK
    hk, hv = GLA_HEADS * GLA_DK, GLA_HEADS * GLA_DV
    qw, kw = ATT_HEADS * HEAD_DIM, ATT_KV_HEADS * HEAD_DIM
    w_in = p['w_in']
    o = np.cumsum([0, hk, hk, hv, hv, 2 * GLA_LOWRANK, qw, kw, kw])
    w_gla = jnp.concatenate([w_in[:, o[0]:o[4]]], axis=1).astype(BF16)
    w_dec = jnp.pad(w_in[:, o[4]:o[5]], ((0, 0), (0, 128 - 2 * GLA_LOWRANK))).astype(BF16)
    w_q = w_in[:, o[5]:o[6]].astype(BF16)
    w_kv = w_in[:, o[6]:o[8]].astype(BF16)
    zg, zdec, zq, zkv = _project(x, mods, p['norm_mix'], [w_gla, w_dec, w_q, w_kv], tm, mod_row)

    dec_w_pad = _pad_rank(p['dec_w'])
    o_f, o_b = _gla(zg, zdec, dec_w_pad, p['dec_b'].reshape(2, 1, hk), batch, nc_ctx, nc_tot)

    cos64, sin64 = _rope_tables(l_ctx, l_lat)
    qn, kn, vn = _qk_prep(zq, zkv, cos64, sin64, p['q_norm'], p['k_norm'], tm, l_tot)
    o_att = _attention(qn, kn, vn, p['sink'], batch, l_ctx, l_tot)

    x1, h, gates = _merge_even(o_f, o_b, zg, o_att, x, mods, p['out_norm'], p['w_out'], p['norm_ffn'],
                               p['router_wt'], p['router_b'], tm, mod_row)
    return _moe(h, gates, x1, mods, p['moe_g'], p['moe_u'], p['moe_d'], p['moe_layer'], tm, mod_row, tm)


def _odd_layer(x, mods, p, batch, l_ctx, l_lat, tm, mod_row, tm_moe):
    l_tot = l_ctx + l_lat
    nc_ctx, nc_tot = l_ctx // SEQ_CHUNK, l_tot // SEQ_CHUNK
    n = RWKV_DIM
    fw = FOURIER_WIDTH
    w_in = p['w_in']
    rank_w, rank_a = p['w2'].shape[1], p['a2'].shape[1]
    o = np.cumsum([0, fw, n, n, n, 2 * rank_w, 2 * rank_a])
    pad_cols = lambda w: jnp.pad(w, ((0, 0), (0, RWKV_RANK_PAD - w.shape[1])))
    w_f = w_in[:, o[0]:o[1]].astype(BF16)
    w_r = jnp.concatenate([w_in[:, o[1]:o[4]], pad_cols(w_in[:, o[4]:o[5]]), pad_cols(w_in[:, o[5]:o[6]]),
                           w_in[:, o[6]:]], axis=1).astype(BF16)
    zf, zr = _project(x, mods, p['norm_mix'], [w_f, w_r], tm, mod_row)

    mu = p['mu']
    mu_r = jnp.concatenate([mu[0:3 * n], pad_cols(mu[None, 3 * n:3 * n + 2 * rank_w])[0],
                            pad_cols(mu[None, 3 * n + 2 * rank_w:3 * n + 2 * rank_w + 2 * rank_a])[0],
                            mu[3 * n + 2 * rank_w + 2 * rank_a:]])
    y0, y1, gate = _rwkv(zr, mu_r, p['kk_scale'], p['k_a'], p['r_k'].reshape(-1), p['w0'], _pad_rank(p['w2']),
                    p['a0'], _pad_rank(p['a2']), p['g2'], batch, nc_ctx, nc_tot)

    tables, chan = _dft_tables(l_lat)
    zc = _chan_dft(zf, chan, batch, l_ctx, l_lat, tm)
    fo = _seq_dft(tables, zc, batch, min(512, l_lat), min(1024, l_lat))

    x1, h, gates = _merge_odd(y0, y1, gate, fo, x, mods, p['ln_g'], p['ln_b'], p['w_out'], p['norm_ffn'],
                              p['router_wt'], p['router_b'], tm, batch, l_ctx, l_lat)
    lat_tiles = l_lat // tm_moe
    return _moe(h, gates, x1, mods, p['moe_g'], p['moe_u'], p['moe_d'], p['moe_layer'], tm_moe,
                lambda i: i // lat_tiles, tm)


def kernel(x, c, ctx, c_ctx, ada_w, ada_b, norm_mix, norm_ffn, even_w_in, even_w_out, gla_dec_w, gla_dec_b, gla_out_norm, att_q_norm, att_k_norm, att_sink, odd_w_in, odd_w_out, rwkv_mu, rwkv_w0, rwkv_w2, rwkv_a0, rwkv_a2, rwkv_g2, rwkv_kk_scale, rwkv_k_a, rwkv_r_k, rwkv_ln_g, rwkv_ln_b, router_w, router_b, moe_w_gate, moe_w_up, moe_w_down):
    batch, l_lat, d = x.shape
    l_ctx = ctx.shape[1]
    l_tot = l_ctx + l_lat
    assert batch < 8 and ada_w.shape[0] == 2
    tm = 256 if (l_ctx % 256 == 0 and l_lat % 256 == 0) else 128
    tm_moe = 512 if (l_lat % 512 == 0 and tm == 256) else tm
    assert l_ctx % tm == 0 and l_lat % tm == 0 and l_tot % l_ctx == 0 and l_lat % GRID_W == 0

    xs = jnp.concatenate([ctx, x], axis=1).reshape(batch * l_tot, d)
    cc = jnp.concatenate([c, c_ctx[None, :], jnp.zeros((8 - batch - 1, d), F32)], axis=0)
    tiles_per_b = l_tot // tm
    ctx_tiles = l_ctx // tm

    def mod_row(i):
        return jnp.where(i % tiles_per_b < ctx_tiles, batch, i // tiles_per_b)

    rw_pad = jnp.pad(router_w, ((0, 0), (0, LANES - N_EXPERTS)))
    rw_hi = rw_pad.astype(BF16)
    router_wt = jnp.stack([rw_hi, (rw_pad - rw_hi.astype(F32)).astype(BF16)])
    moe = lambda layer: dict(moe_g=moe_w_gate, moe_u=moe_w_up, moe_d=moe_w_down, moe_layer=layer)

    mods_all = _modvec(cc, ada_w, ada_b)
    mods0 = mods_all[0]
    p0 = dict(w_in=even_w_in[0], w_out=even_w_out[0], dec_w=gla_dec_w[0], dec_b=gla_dec_b[0],
              out_norm=gla_out_norm[0], q_norm=att_q_norm[0], k_norm=att_k_norm[0], sink=att_sink[0],
              norm_mix=norm_mix[0], norm_ffn=norm_ffn[0], router_wt=router_wt, router_b=router_b, **moe(0))
    xs = _even_layer(xs, mods0, p0, batch, l_ctx, l_lat, tm, mod_row, tm_moe)

    mods1 = mods_all[1]
    p1 = dict(w_in=odd_w_in[0], w_out=odd_w_out[0], mu=rwkv_mu[0], w0=rwkv_w0[0], w2=rwkv_w2[0], a0=rwkv_a0[0],
              a2=rwkv_a2[0], g2=rwkv_g2[0], kk_scale=rwkv_kk_scale[0], k_a=rwkv_k_a[0], r_k=rwkv_r_k[0],
              ln_g=rwkv_ln_g[0], ln_b=rwkv_ln_b[0], norm_mix=norm_mix[1], norm_ffn=norm_ffn[1],
              router_wt=router_wt, router_b=router_b, **moe(1))
    out = _odd_layer(xs, mods1, p1, batch, l_ctx, l_lat, tm, mod_row, tm_moe)
    return out.reshape(batch, l_lat, d)
```

```python
import functools

import jax
import jax.numpy as jnp
import numpy as np
from jax import lax
from jax.experimental import pallas as pl
from jax.experimental.pallas import tpu as pltpu

F32 = jnp.float32
BF16 = jnp.bfloat16

GRID_W = 64
HEAD_DIM = 64
NORM_EPS = 1e-6
L2_EPS = 1e-12

GLA_DV = 64
GLA_DK = 32
GLA_HEADS = 8
GLA_LOWRANK = 16
GLA_TAU = 16.0

ATT_HEADS = 8
ATT_KV_HEADS = 2
ATT_GROUP = ATT_HEADS // ATT_KV_HEADS
ATT_BLOCK = 128
ROPE_THETA = 10000.0

FOURIER_GROUP_DIM = 64
FOURIER_WIDTH = 256
DFT_SPLIT = 64

RWKV_DIM = 768
RWKV_HEADS = 12
RWKV_RANK_PAD = 128
RWKV_GN_EPS = 64e-5

N_EXPERTS = 16
N_GROUPS = 4
PER_GROUP = N_EXPERTS // N_GROUPS
D_EXPERT = 512
MOE_TILE = 256
MOE_GATHER_DEPTH = 3
LANES = 128
SUBLANES = 8

SEQ_CHUNK = 64
VMEM_LIMIT = 56 * 1024 * 1024


def _cparams(sem):
    return pltpu.CompilerParams(dimension_semantics=sem, vmem_limit_bytes=VMEM_LIMIT)


def _dot(a, b):
    return jnp.dot(a, b, preferred_element_type=F32)


def _dot_nt(a, b):
    return lax.dot_general(a, b, (((1,), (1,)), ((), ())), preferred_element_type=F32)


def _dot_tn(a, b):
    return lax.dot_general(a, b, (((0,), (0,)), ((), ())), preferred_element_type=F32)


def _silu(x):
    return x * jax.nn.sigmoid(x)


def _log_sigmoid(x):
    return jnp.minimum(x, 0.0) - jnp.log(1.0 + jnp.exp(-jnp.abs(x)))


def _modulated_norm(x, gain, shift, scale):
    ms = jnp.mean(x * x, axis=-1, keepdims=True)
    return (x * lax.rsqrt(ms + NORM_EPS) * gain) * (1.0 + scale) + shift


def _block_ones(n_blocks, width):
    return jnp.kron(jnp.eye(n_blocks, dtype=F32), jnp.ones((width, width), F32))


def _store_row_tiles(ref, x, first_row=0):
    n, w = x.shape
    g = w // LANES
    for k in range(g):
        ref[pl.ds(first_row * g + k, n, stride=g), :] = x[:, k * LANES:(k + 1) * LANES]


def _load_row_tiles(ref, n, g, first_row=0):
    return jnp.concatenate([ref[pl.ds(first_row * g + k, n, stride=g), :] for k in range(g)], axis=1)


def _modvec_kernel(c_ref, w_ref, b_ref, o_ref):
    o_ref[...] = _dot(_silu(c_ref[...]), w_ref[...]) + b_ref[...]


def _modvec(cc, w, b):
    d = cc.shape[1]
    layers, _, n = w.shape
    tn = n // 4
    out = pl.pallas_call(
        _modvec_kernel,
        grid=(layers, n // tn),
        in_specs=[pl.BlockSpec((8, d), lambda l, j: (0, 0)),
                  pl.BlockSpec((None, d, tn), lambda l, j: (l, 0, j)),
                  pl.BlockSpec((None, 1, tn), lambda l, j: (l, 0, j))],
        out_specs=pl.BlockSpec((None, 8, tn), lambda l, j: (l, 0, j)),
        out_shape=jax.ShapeDtypeStruct((layers, 8, n), F32),
        compiler_params=_cparams(("parallel", "parallel")),
        name="modvec",
    )(cc, w, b.reshape(layers, 1, n))
    return out.reshape(layers, 8, 6, d)


def _proj_kernel(ya_ref, yb_ref, x_ref, mod_prev_ref, mod_ref, gain_ref, *refs, n_out):
    w_refs, x_out_ref, z_refs = refs[:n_out], refs[n_out], refs[n_out + 1:]
    n, d = x_ref.shape
    y = _load_row_tiles(ya_ref, n, d // LANES) + _load_row_tiles(yb_ref, n, d // LANES)
    x = x_ref[...] + mod_prev_ref[0][5:6] * y
    x_out_ref[...] = x
    m = mod_ref[0]
    h = _modulated_norm(x, gain_ref[...], m[0:1], m[1:2]).astype(BF16)
    for w_ref, z_ref in zip(w_refs, z_refs):
        z_ref[...] = _dot(h, w_ref[...])


def _project(pairs, x, mods_prev, mods, gain, weights, tm, mod_row):
    t, d = x.shape
    g = d // LANES
    n_out = len(weights)
    mod_spec = pl.BlockSpec((1, 6, d), lambda i: (mod_row(i), 0, 0))
    in_specs = [pl.BlockSpec((tm * g, LANES), lambda i: (i, 0)),
                pl.BlockSpec((tm * g, LANES), lambda i: (t // tm + i, 0)),
                pl.BlockSpec((tm, d), lambda i: (i, 0)),
                mod_spec, mod_spec,
                pl.BlockSpec((1, d), lambda i: (0, 0))]
    in_specs += [pl.BlockSpec(w.shape, lambda i: (0, 0)) for w in weights]
    outs = pl.pallas_call(
        functools.partial(_proj_kernel, n_out=n_out),
        grid=(t // tm,),
        in_specs=in_specs,
        out_specs=[pl.BlockSpec((tm, d), lambda i: (i, 0))] + [pl.BlockSpec((tm, w.shape[1]), lambda i: (i, 0)) for w in weights],
        out_shape=[jax.ShapeDtypeStruct((t, d), F32)] + [jax.ShapeDtypeStruct((t, w.shape[1]), F32) for w in weights],
        compiler_params=_cparams(("parallel",)),
        name="proj",
    )(pairs, pairs, x, mods_prev, mods, gain.reshape(1, d), *weights)
    return outs[0], outs[1:]


def _chunk_pos(s, d, nc_ctx, nc_tot):
    back = jnp.where(s < nc_ctx, nc_ctx - 1 - s, nc_tot + nc_ctx - 1 - s)
    return jnp.where(d == 0, s, back)


def _visit_order(d, c):
    sign = 1 if d == 0 else -1
    return (lax.broadcasted_iota(jnp.int32, (c, c), 0) - lax.broadcasted_iota(jnp.int32, (c, c), 1)) * sign


def _gla_prepare(d, q_ref, k_ref, v_ref, dec_ref, dw_ref, db_ref):
    c = q_ref.shape[0]
    g = _log_sigmoid(_dot(dec_ref[...], dw_ref[d]) + db_ref[d]) / GLA_TAU
    b = _dot((_visit_order(d, c) >= 0).astype(F32), g)
    b_tot = jnp.sum(g, axis=0, keepdims=True)
    k = k_ref[...]
    return dict(q_in=q_ref[...] * (GLA_DK ** -0.5) * jnp.exp(b), k_out=k * jnp.exp(-b), k_end=k * jnp.exp(b_tot - b),
                decay=jnp.exp(b_tot), v=v_ref[...], sign=1 if d == 0 else -1)


def _gla_kernel(qf_ref, kf_ref, vf_ref, df_ref, qb_ref, kb_ref, vb_ref, db_ref, dw_ref, dbias_ref, of_ref, ob_ref, st_ref):
    c = qf_ref.shape[0]
    group = 4
    kw, vw = group * GLA_DK, group * GLA_DV

    @pl.when(pl.program_id(1) == 0)
    def _():
        st_ref[...] = jnp.zeros_like(st_ref)

    fwd = _gla_prepare(0, qf_ref, kf_ref, vf_ref, df_ref, dw_ref, dbias_ref)
    bwd = _gla_prepare(1, qb_ref, kb_ref, vb_ref, db_ref, dw_ref, dbias_ref)
    n_quads = GLA_HEADS // group
    prob = [(o, i) for o in (fwd, bwd) for i in range(n_quads)]
    klane = lax.broadcasted_iota(jnp.int32, (c, kw), 1) // GLA_DK
    half = lax.broadcasted_iota(jnp.int32, (c, 2 * GLA_DV), 1) < GLA_DV
    rowi = lax.broadcasted_iota(jnp.int32, (c, group * c), 0)
    coli = lax.broadcasted_iota(jnp.int32, (c, group * c), 1) % c
    own = (lax.broadcasted_iota(jnp.int32, (vw, kw), 0) // GLA_DV) == (lax.broadcasted_iota(jnp.int32, (vw, kw), 1) // GLA_DK)

    def bd_keys(y):
        return jnp.concatenate([jnp.where(klane == h, y, 0.0) for h in range(group)], axis=0)

    def bd_vals(y):
        return jnp.concatenate([jnp.where(half, y, 0.0), jnp.where(half, 0.0, y)], axis=0)

    ksl = lambda i: slice(i * kw, (i + 1) * kw)
    vsl = lambda i: slice(i * vw, (i + 1) * vw)
    q_in = [o['q_in'][:, ksl(i)] for o, i in prob]
    att = [jnp.where((rowi - coli) * o['sign'] >= 0, _dot_nt(q_, bd_keys(o['k_out'][:, ksl(i)])), 0.0)
           for q_, (o, i) in zip(q_in, prob)]
    st = [st_ref[j] for j in range(len(prob))]
    outs = []
    for j, (o, i) in enumerate(prob):
        v = o['v'][:, vsl(i)]
        intra = jnp.concatenate([_dot(att[j][:, p * 2 * c:(p + 1) * 2 * c], bd_vals(v[:, p * 2 * GLA_DV:(p + 1) * 2 * GLA_DV]))
                                 for p in range(group // 2)], axis=1)
        outs.append(intra + _dot_nt(q_in[j], st[j]))
    for j, (o, i) in enumerate(prob):
        upd = _dot_tn(o['v'][:, vsl(i)], o['k_end'][:, ksl(i)])
        st_ref[j] = st[j] * o['decay'][:, ksl(i)] + jnp.where(own, upd, 0.0)
    of_ref[...] = jnp.concatenate(outs[:n_quads], axis=1)
    ob_ref[...] = jnp.concatenate(outs[n_quads:], axis=1)


def _gla(zg, zdec, dec_w_pad, dec_b, batch, nc_ctx, nc_tot):
    t = zg.shape[0]
    c = SEQ_CHUNK
    hk, hv = GLA_HEADS * GLA_DK, GLA_HEADS * GLA_DV

    def specs(d):
        rw = lambda b, s: b * nc_tot + _chunk_pos(s, d, nc_ctx, nc_tot)
        return [pl.BlockSpec((c, hk), lambda b, s: (rw(b, s), 0)),
                pl.BlockSpec((c, hk), lambda b, s: (rw(b, s), 1)),
                pl.BlockSpec((c, hv), lambda b, s: (rw(b, s), 1)),
                pl.BlockSpec((c, 128), lambda b, s: (rw(b, s), 0))], pl.BlockSpec((c, hv), lambda b, s: (rw(b, s), 0))

    in_f, out_f = specs(0)
    in_b, out_b = specs(1)
    return pl.pallas_call(
        _gla_kernel,
        grid=(batch, nc_tot),
        in_specs=in_f + in_b + [pl.BlockSpec(dec_w_pad.shape, lambda b, s: (0, 0, 0)),
                                pl.BlockSpec(dec_b.shape, lambda b, s: (0, 0, 0))],
        out_specs=[out_f, out_b],
        out_shape=[jax.ShapeDtypeStruct((t, hv), F32)] * 2,
        scratch_shapes=[pltpu.VMEM((2 * GLA_HEADS // 4, 4 * GLA_DV, 4 * GLA_DK), F32)],
        compiler_params=_cparams(("parallel", "arbitrary")),
        name="gla_scan",
    )(zg, zg, zg, zdec, zg, zg, zg, zdec, dec_w_pad, dec_b)


def _rope_swap(x):
    n = x.shape[-1]
    lane = lax.broadcasted_iota(jnp.int32, x.shape, x.ndim - 1)
    half = HEAD_DIM // 2
    return jnp.where(lane % HEAD_DIM < half, pltpu.roll(x, n - half, x.ndim - 1), pltpu.roll(x, half, x.ndim - 1))


def _proj_even_kernel(ctx_ref, x_ref, mod_ref, gain_ref, wg_ref, wd_ref, wq_ref, wkv_ref, cos_ref, sin_ref, qg_ref, kg_ref,
                      bdq_ref, bdk_ref, xs_ref, zg_ref, zdec_ref, qo_ref, ko_ref, vo_ref, *, tiles_per_b, ctx_tiles):
    is_ctx = lax.rem(pl.program_id(0), tiles_per_b) < ctx_tiles
    x = jnp.where(is_ctx, ctx_ref[...], x_ref[...])
    xs_ref[...] = x
    m = mod_ref[0]
    h = _modulated_norm(x, gain_ref[...], m[0:1], m[1:2]).astype(BF16)
    zg_ref[...] = _dot(h, wg_ref[...])
    zdec_ref[...] = _dot(h, wd_ref[...])

    def norm_rope(z, gain, bd, n_heads):
        ms = _dot(z * z, bd) * (1.0 / HEAD_DIM)
        zn = z * lax.rsqrt(ms + NORM_EPS) * gain
        cos = jnp.concatenate([cos_ref[...]] * n_heads, axis=1)
        sin = jnp.concatenate([sin_ref[...]] * n_heads, axis=1)
        return zn * cos + _rope_swap(zn) * sin

    q = norm_rope(_dot(h, wq_ref[...]), qg_ref[...], bdq_ref[...], ATT_HEADS)
    qo_ref[...] = (q * (HEAD_DIM ** -0.5)).astype(BF16)
    kw = ATT_KV_HEADS * HEAD_DIM
    kv = _dot(h, wkv_ref[...])
    ko_ref[...] = norm_rope(kv[:, :kw], kg_ref[...], bdk_ref[...], ATT_KV_HEADS).astype(BF16)
    vo_ref[...] = kv[:, kw:].astype(BF16)


def _project_even(ctx, x, mods, gain, weights, cos64, sin64, q_gain, k_gain, tm, mod_row, batch, l_ctx, l_lat):
    d = x.shape[1]
    l_tot = l_ctx + l_lat
    t = batch * l_tot
    tiles_per_b, ctx_tiles, lat_tiles = l_tot // tm, l_ctx // tm, l_lat // tm
    qw, kw = ATT_HEADS * HEAD_DIM, ATT_KV_HEADS * HEAD_DIM
    w_gla, w_dec, w_q, w_kv = weights
    full = lambda a: pl.BlockSpec(a.shape, lambda i: (0,) * a.ndim)
    qg = jnp.tile(q_gain, ATT_HEADS).reshape(1, qw)
    kg = jnp.tile(k_gain, ATT_KV_HEADS).reshape(1, kw)
    bdq, bdk = _block_ones(ATT_HEADS, HEAD_DIM), _block_ones(ATT_KV_HEADS, HEAD_DIM)
    gain = gain.reshape(1, d)
    row = lambda n: pl.BlockSpec((tm, n), lambda i: (i, 0))
    pos = pl.BlockSpec((tm, HEAD_DIM), lambda i: (i % tiles_per_b, 0))
    return pl.pallas_call(
        functools.partial(_proj_even_kernel, tiles_per_b=tiles_per_b, ctx_tiles=ctx_tiles),
        grid=(t // tm,),
        in_specs=[pl.BlockSpec((tm, d), lambda i: ((i // tiles_per_b) * ctx_tiles + jnp.minimum(i % tiles_per_b, ctx_tiles - 1), 0)),
                  pl.BlockSpec((tm, d), lambda i: ((i // tiles_per_b) * lat_tiles + jnp.maximum(i % tiles_per_b - ctx_tiles, 0), 0)),
                  pl.BlockSpec((1, 6, d), lambda i: (mod_row(i), 0, 0)),
                  full(gain), full(w_gla), full(w_dec), full(w_q), full(w_kv), pos, pos,
                  full(qg), full(kg), full(bdq), full(bdk)],
        out_specs=[row(d), row(w_gla.shape[1]), row(w_dec.shape[1]), row(qw), row(kw), row(kw)],
        out_shape=[jax.ShapeDtypeStruct((t, d), F32), jax.ShapeDtypeStruct((t, w_gla.shape[1]), F32),
                   jax.ShapeDtypeStruct((t, w_dec.shape[1]), F32), jax.ShapeDtypeStruct((t, qw), BF16),
                   jax.ShapeDtypeStruct((t, kw), BF16), jax.ShapeDtypeStruct((t, kw), BF16)],
        compiler_params=_cparams(("parallel",)),
        name="proj_even",
    )(ctx, x, mods, gain, w_gla, w_dec, w_q, w_kv, cos64, sin64, qg, kg, bdq, bdk)


def _attn_kernel(q_ref, kp_ref, kc_ref, kn_ref, kx_ref, vp_ref, vc_ref, vn_ref, vx_ref, sink_ref, o_ref,
                 *, n_ctx_blocks, n_lat_blocks):
    blk = ATT_BLOCK
    n = pl.program_id(1)
    m = n - n_ctx_blocks
    is_lat = n >= n_ctx_blocks
    l_ctx = kx_ref.shape[0]
    width = 3 * blk + l_ctx
    rows = ATT_GROUP * blk
    r = lax.broadcasted_iota(jnp.int32, (rows, width), 0) % blk
    c = lax.broadcasted_iota(jnp.int32, (rows, width), 1)
    lat = is_lat.astype(jnp.int32)
    has_prev = lat * (m >= 1).astype(jnp.int32)
    has_next = lat * (m <= n_lat_blocks - 2).astype(jnp.int32)
    valid = jnp.where(c < blk, (c >= r).astype(jnp.int32) * has_prev,
                      jnp.where(c < 2 * blk, lat,
                                jnp.where(c < 3 * blk, (c - 2 * blk <= r).astype(jnp.int32) * has_next, 1))) > 0
    q = q_ref[...]
    sink = sink_ref[...]
    outs = [None] * ATT_HEADS
    for kvh in range(ATT_KV_HEADS):
        ks = slice(kvh * HEAD_DIM, (kvh + 1) * HEAD_DIM)
        kw = jnp.concatenate([kp_ref[:, ks], kc_ref[:, ks], kn_ref[:, ks], kx_ref[:, ks]], axis=0)
        vw = jnp.concatenate([vp_ref[:, ks], vc_ref[:, ks], vn_ref[:, ks], vx_ref[:, ks]], axis=0)
        heads = range(kvh * ATT_GROUP, (kvh + 1) * ATT_GROUP)
        qg = jnp.concatenate([q[:, h * HEAD_DIM:(h + 1) * HEAD_DIM] for h in heads], axis=0)
        s = jnp.where(valid, _dot_nt(qg, kw), -jnp.inf)
        sk = jnp.concatenate([jnp.broadcast_to(sink[h:h + 1, 0:1], (blk, 1)) for h in heads], axis=0)
        mx = jnp.maximum(jnp.max(s, axis=-1, keepdims=True), sk)
        p = jnp.exp(s - mx)
        denom = jnp.sum(p, axis=-1, keepdims=True) + jnp.exp(sk - mx)
        o = _dot(p.astype(BF16), vw) / denom
        for g, h in enumerate(heads):
            outs[h] = o[g * blk:(g + 1) * blk]
    o_ref[...] = jnp.concatenate(outs, axis=1)


def _attention(qn, kn, vn, sink, batch, l_ctx, l_tot):
    t = qn.shape[0]
    blk = ATT_BLOCK
    nq = l_tot // blk
    nc = l_ctx // blk
    nl = nq - nc
    qw, kw = ATT_HEADS * HEAD_DIM, ATT_KV_HEADS * HEAD_DIM

    def win(off):
        def index(b, n):
            m = jnp.clip(n - nc + off, 0, nl - 1)
            return (b * nq + nc + m, 0)
        return pl.BlockSpec((blk, kw), index)

    ctx_spec = pl.BlockSpec((l_ctx, kw), lambda b, n: (b * (l_tot // l_ctx), 0))
    return pl.pallas_call(
        functools.partial(_attn_kernel, n_ctx_blocks=nc, n_lat_blocks=nl),
        grid=(batch, nq),
        in_specs=[pl.BlockSpec((blk, qw), lambda b, n: (b * nq + n, 0)),
                  win(-1), win(0), win(1), ctx_spec,
                  win(-1), win(0), win(1), ctx_spec,
                  pl.BlockSpec((ATT_HEADS, 128), lambda b, n: (0, 0))],
        out_specs=pl.BlockSpec((blk, qw), lambda b, n: (b * nq + n, 0)),
        out_shape=jax.ShapeDtypeStruct((t, qw), F32),
        compiler_params=_cparams(("parallel", "parallel")),
        name="window_attention",
    )(qn, kn, kn, kn, kn, vn, vn, vn, vn, jnp.broadcast_to(sink.astype(F32)[:, None], (ATT_HEADS, 128)))


def _route(logits_t, bias_col):
    scores = jax.nn.sigmoid(logits_t)
    sel = scores + bias_col
    rows = [sel[e:e + 1] for e in range(N_EXPERTS)]
    grp = []
    for g in range(N_GROUPS):
        r = rows[g * PER_GROUP:(g + 1) * PER_GROUP]
        best = None
        for i in range(PER_GROUP):
            for j in range(i + 1, PER_GROUP):
                pair = r[i] + r[j]
                best = pair if best is None else jnp.maximum(best, pair)
        grp.append(best)
    g_best = jnp.zeros_like(grp[0], dtype=jnp.int32)
    g_val = grp[0]
    for g in range(1, N_GROUPS):
        take = grp[g] > g_val
        g_best = jnp.where(take, g, g_best)
        g_val = jnp.where(take, grp[g], g_val)
    neg = -jnp.inf
    masked = [jnp.where(g_best == e // PER_GROUP, rows[e], neg) for e in range(N_EXPERTS)]
    i1 = jnp.zeros_like(g_best)
    v1 = masked[0]
    for e in range(1, N_EXPERTS):
        take = masked[e] > v1
        i1 = jnp.where(take, e, i1)
        v1 = jnp.where(take, masked[e], v1)
    i2 = jnp.full_like(g_best, -1)
    v2 = jnp.full_like(v1, neg)
    for e in range(N_EXPERTS):
        take = jnp.logical_and(i1 != e, masked[e] > v2)
        i2 = jnp.where(take, e, i2)
        v2 = jnp.where(take, masked[e], v2)
    w1 = jnp.zeros_like(v1)
    w2 = jnp.zeros_like(v1)
    for e in range(N_EXPERTS):
        w1 = jnp.where(i1 == e, scores[e:e + 1], w1)
        w2 = jnp.where(i2 == e, scores[e:e + 1], w2)
    inv = 1.0 / (w1 + w2)
    pad = jnp.zeros_like(w1)
    lane = lax.broadcasted_iota(jnp.int32, w1.shape, 1)
    hist = pad
    for e in range(N_EXPERTS):
        n_e = jnp.sum((i1 == e).astype(F32) + (i2 == e).astype(F32), axis=1, keepdims=True)
        hist = jnp.where(lane == e, n_e, hist)
    return jnp.concatenate([i1.astype(F32), i2.astype(F32), w1 * inv, w2 * inv, hist, pad, pad, pad], axis=0)


def _residual_and_route(x, out, m, ffn_gain, rw_ref, rb_ref, x_ref, h_ref, g_ref):
    x1 = x + m[2:3] * out
    x_ref[...] = x1
    h = _modulated_norm(x1, ffn_gain, m[3:4], m[4:5])
    _store_row_tiles(h_ref, h)
    h_hi = h.astype(BF16)
    h_lo = (h - h_hi.astype(F32)).astype(BF16)
    logits = _dot(h_hi, rw_ref[0]) + _dot(h_lo, rw_ref[0]) + _dot(h_hi, rw_ref[1])
    g_ref[...] = _route(logits.T[:N_EXPERTS], rb_ref[...])


def _merge_even_kernel(o0_ref, o1_ref, gg_ref, oa_ref, x_ref, mod_ref, gn_ref, bd_ref, w1_ref, w2_ref,
                       fg_ref, rw_ref, rb_ref, xo_ref, ho_ref, go_ref):
    og = o0_ref[...] + o1_ref[...]
    ms = _dot(og * og, bd_ref[...]) * (1.0 / GLA_DV)
    g = og * lax.rsqrt(ms + NORM_EPS) * gn_ref[...] * _silu(gg_ref[...])
    out = _dot(g.astype(BF16), w1_ref[...]) + _dot(oa_ref[...].astype(BF16), w2_ref[...])
    _residual_and_route(x_ref[...], out, mod_ref[0], fg_ref[...], rw_ref, rb_ref, xo_ref, ho_ref, go_ref)


def _merge_even(o_f, o_b, zg, o_att, x, mods, out_norm, w_out, ffn_gain, router_wt, router_b, tm, mod_row):
    t, d = x.shape
    hv = GLA_HEADS * GLA_DV
    qw = ATT_HEADS * HEAD_DIM
    full = lambda a: pl.BlockSpec(a.shape, lambda i: (0,) * a.ndim)
    gn = jnp.tile(out_norm, GLA_HEADS).reshape(1, hv)
    bd = _block_ones(GLA_HEADS, GLA_DV)
    w1, w2 = w_out[:hv].astype(BF16), w_out[hv:].astype(BF16)
    fg = ffn_gain.reshape(1, d)
    rb = router_b.reshape(N_EXPERTS, 1)
    return pl.pallas_call(
        _merge_even_kernel,
        grid=(t // tm,),
        in_specs=[pl.BlockSpec((tm, hv), lambda i: (i, 0)),
                  pl.BlockSpec((tm, hv), lambda i: (i, 0)),
                  pl.BlockSpec((tm, hv), lambda i: (i, 2)),
                  pl.BlockSpec((tm, qw), lambda i: (i, 0)),
                  pl.BlockSpec((tm, d), lambda i: (i, 0)),
                  pl.BlockSpec((1, 6, d), lambda i: (mod_row(i), 0, 0)),
                  full(gn), full(bd), full(w1), full(w2), full(fg), full(router_wt), full(rb)],
        out_specs=[pl.BlockSpec((tm, d), lambda i: (i, 0)),
                   pl.BlockSpec((tm * (d // LANES), LANES), lambda i: (i, 0)),
                   pl.BlockSpec((8, tm), lambda i: (0, i))],
        out_shape=[jax.ShapeDtypeStruct((t, d), F32),
                   jax.ShapeDtypeStruct((t * (d // LANES), LANES), F32),
                   jax.ShapeDtypeStruct((8, t), F32)],
        compiler_params=_cparams(("parallel",)),
        name="merge_even",
    )(o_f, o_b, zg, o_att, x, mods, gn, bd, w1, w2, fg, router_wt, rb)


def _moe_plan(route, t, rows, route_tile):
    n_tiles = 2 * t // rows + N_EXPERTS
    eid = jnp.concatenate([route[0], route[1]]).astype(jnp.int32)
    slot = jnp.arange(2 * t, dtype=jnp.int32)
    gate = jnp.concatenate([route[2], route[3]])
    _, s_slot, s_gate = lax.sort((eid, slot, gate), num_keys=1, is_stable=True)
    counts = jnp.sum(route[4].reshape(t // route_tile, route_tile)[:, :N_EXPERTS], axis=0).astype(jnp.int32)
    padded = (counts + rows - 1) // rows * rows
    p_end = jnp.cumsum(padded)
    p_start = p_end - padded
    c_start = jnp.cumsum(counts) - counts
    tile_start = jnp.arange(n_tiles, dtype=jnp.int32) * rows
    tile_e = jnp.minimum(jnp.sum((tile_start[:, None] >= p_end[None, :]).astype(jnp.int32), axis=1), N_EXPERTS - 1)
    r_in = (tile_start - p_start[tile_e])[:, None] + jnp.arange(rows, dtype=jnp.int32)[None, :]
    over = r_in - counts[tile_e][:, None]
    valid = over < 0
    src = jnp.clip(c_start[tile_e][:, None] + r_in, 0, 2 * t - 1)
    g_slot = s_slot[src]
    row_tok = jnp.where(valid, jnp.where(g_slot >= t, g_slot - t, g_slot), 0)
    row_gate = jnp.where(valid, s_gate[src], 0.0)
    row_dst = jnp.where(valid, g_slot, 2 * t + tile_e[:, None] * rows + jnp.clip(over, 0, rows - 1))
    n_used = (p_end[-1] // rows).astype(jnp.int32)
    tile_e = jnp.where(tile_start < p_end[-1], tile_e, tile_e[jnp.maximum(n_used - 1, 0)])
    return (row_tok[:, None, :], row_gate[:, None, :], row_dst[:, None, :], tile_e, n_used.reshape(1))


def _moe_experts_kernel(te_ref, nu_ref, tok0_ref, tok1_ref, tokn_ref, dst_ref, gate_ref, wg_ref, wu_ref, wd_ref, h_hbm,
                        y_hbm, hbuf, ybuf, wgb, wub, wdb, sem_g, sem_s):
    j = pl.program_id(0)
    n_used = nu_ref[0]
    slot = j % 2
    g = SUBLANES
    rows = ybuf.shape[0] // (2 * g)
    tile = rows * g
    depth = hbuf.shape[0] // tile
    gslot = lax.rem(j, depth)

    def row(ref, i):
        return ref.at[pl.ds(pl.multiple_of(i * g, g), g)]

    def start_gather(idx_ref, s):
        for r in range(rows):
            pltpu.make_async_copy(row(h_hbm, idx_ref[0, 0, r]), row(hbuf, s * rows + r), sem_g.at[s]).start(priority=r % 2)

    def wait_gather(s):
        pltpu.make_async_copy(h_hbm.at[pl.ds(0, tile)], hbuf.at[pl.ds(pl.multiple_of(s * tile, tile), tile)],
                              sem_g.at[s]).wait()

    def wait_scatter(s):
        pltpu.make_async_copy(ybuf.at[pl.ds(pl.multiple_of(s * tile, tile), tile)], y_hbm.at[pl.ds(0, tile)],
                              sem_s.at[s]).wait()

    @pl.when(j == 0)
    def _():
        start_gather(tok0_ref, 0)
        start_gather(tok1_ref, 1)
        ybuf[...] = jnp.zeros(ybuf.shape, F32)
        n_real = y_hbm.shape[0] // g - N_EXPERTS * rows
        fills = [pltpu.make_async_copy(ybuf.at[pl.ds((k % 2) * tile, tile)],
                                       y_hbm.at[pl.ds((n_real + k * rows) * g, tile)], sem_s.at[k % 2])
                 for k in range(N_EXPERTS)]
        for f in fills:
            f.start()
        for f in fills[2:]:
            f.wait()

    active = j < n_used
    changed = jnp.logical_or(j == 0, te_ref[j] != te_ref[jnp.maximum(j - 1, 0)])

    @pl.when(jnp.logical_and(active, changed))
    def _():
        wgb[...] = wg_ref[0].astype(BF16)
        wub[...] = wu_ref[0].astype(BF16)
        wdb[...] = wd_ref[0].astype(BF16)

    @pl.when(active)
    def _():
        wait_gather(gslot)
        start_gather(tokn_ref, lax.rem(j + depth - 1, depth))
        h = _load_row_tiles(hbuf, rows, g, first_row=gslot * rows).astype(BF16)
        act = _silu(_dot(h, wgb[...])) * _dot(h, wub[...])
        ri = lax.broadcasted_iota(jnp.int32, (rows, rows), 0)
        ci = lax.broadcasted_iota(jnp.int32, (rows, rows), 1)
        gate = jnp.sum(jnp.where(ri == ci, gate_ref[0], 0.0), axis=1, keepdims=True)
        y = _dot((act * gate).astype(BF16), wdb[...])
        wait_scatter(slot)
        _store_row_tiles(ybuf, y, first_row=slot * rows)
        for r in range(rows):
            pltpu.make_async_copy(row(ybuf, slot * rows + r), row(y_hbm, dst_ref[0, 0, r]), sem_s.at[slot]).start(priority=r % 2)

        @pl.when(j == n_used - 1)
        def _():
            wait_scatter(slot)
            wait_scatter(1 - slot)
            for ahead in range(1, depth):
                wait_gather(lax.rem(j + ahead, depth))


def _moe_combine_kernel(ya_ref, yb_ref, x_ref, mod_ref, o_ref):
    n, d = x_ref.shape
    y = _load_row_tiles(ya_ref, n, d // LANES) + _load_row_tiles(yb_ref, n, d // LANES)
    o_ref[...] = x_ref[...] + mod_ref[0][5:6] * y


def _moe(h, route, x, mods, w_gate, w_up, w_down, layer, tm, mod_row, route_tile, combine=True):
    t, d = x.shape
    g = d // LANES
    assert g == SUBLANES
    rows = MOE_TILE
    row_tok, row_gate, row_dst, tile_e, n_used = _moe_plan(route, t, rows, route_tile)
    n_tiles = row_tok.shape[0]
    n_pair_rows = 2 * t + N_EXPERTS * rows
    assert MOE_GATHER_DEPTH == 3
    smem_tile = lambda which: pl.BlockSpec((1, 1, rows), lambda j, te, nu: (jnp.minimum(which(j), nu[0] - 1), 0, 0),
                                           memory_space=pltpu.SMEM)
    pairs = pl.pallas_call(
        _moe_experts_kernel,
        grid_spec=pltpu.PrefetchScalarGridSpec(
            num_scalar_prefetch=2,
            grid=(n_tiles,),
            in_specs=[smem_tile(lambda j: 0), smem_tile(lambda j: 1), smem_tile(lambda j: j + MOE_GATHER_DEPTH - 1),
                      smem_tile(lambda j: j),
                      pl.BlockSpec((1, 1, rows), lambda j, te, nu: (j, 0, 0)),
                      pl.BlockSpec((None, 1, d, D_EXPERT), lambda j, te, nu: (layer, te[j], 0, 0)),
                      pl.BlockSpec((None, 1, d, D_EXPERT), lambda j, te, nu: (layer, te[j], 0, 0)),
                      pl.BlockSpec((None, 1, D_EXPERT, d), lambda j, te, nu: (layer, te[j], 0, 0)),
                      pl.BlockSpec(memory_space=pl.ANY)],
            out_specs=pl.BlockSpec(memory_space=pl.ANY),
            scratch_shapes=[pltpu.VMEM((MOE_GATHER_DEPTH * rows * g, LANES), F32), pltpu.VMEM((2 * rows * g, LANES), F32),
                            pltpu.VMEM((d, D_EXPERT), BF16), pltpu.VMEM((d, D_EXPERT), BF16),
                            pltpu.VMEM((D_EXPERT, d), BF16),
                            pltpu.SemaphoreType.DMA((MOE_GATHER_DEPTH,)), pltpu.SemaphoreType.DMA((2,))]),
        out_shape=jax.ShapeDtypeStruct((n_pair_rows * g, LANES), F32),
        compiler_params=_cparams(("arbitrary",)),
        name="moe_experts",
    )(tile_e, n_used, row_tok, row_tok, row_tok, row_dst, row_gate, w_gate, w_up, w_down, h)
    if not combine:
        return pairs
    return pl.pallas_call(
        _moe_combine_kernel,
        grid=(t // tm,),
        in_specs=[pl.BlockSpec((tm * g, LANES), lambda i: (i, 0)),
                  pl.BlockSpec((tm * g, LANES), lambda i: (t // tm + i, 0)),
                  pl.BlockSpec((tm, d), lambda i: (i, 0)),
                  pl.BlockSpec((1, 6, d), lambda i: (mod_row(i), 0, 0))],
        out_specs=pl.BlockSpec((tm, d), lambda i: (i, 0)),
        out_shape=jax.ShapeDtypeStruct((t, d), F32),
        compiler_params=_cparams(("parallel",)),
        name="moe_combine",
    )(pairs, pairs, x, mods)


def _chan_dft_kernel(z_ref, w_ref, o_ref):
    res = _dot(z_ref[...].astype(BF16), w_ref[...]).astype(BF16)
    o_ref[0] = res[:, :FOURIER_WIDTH]
    o_ref[1] = res[:, FOURIER_WIDTH:]


def _chan_dft(zf, w, batch, l_ctx, l_lat, tm):
    l_tot = l_ctx + l_lat
    nt = l_lat // tm
    fw = FOURIER_WIDTH
    return pl.pallas_call(
        _chan_dft_kernel,
        grid=(batch, nt),
        in_specs=[pl.BlockSpec((tm, fw), lambda b, i: (b * (l_tot // tm) + l_ctx // tm + i, 0)),
                  pl.BlockSpec(w.shape, lambda b, i: (0, 0))],
        out_specs=pl.BlockSpec((2, tm, fw), lambda b, i: (0, i, b)),
        out_shape=jax.ShapeDtypeStruct((2, l_lat, batch * fw), BF16),
        compiler_params=_cparams(("parallel", "parallel")),
        name="fourier_channels",
    )(zf, w)


def _seq_dft_kernel(ca_ref, sa_ref, cb_ref, sb_ref, z_ref, o_ref, acc_ref):
    k = pl.program_id(1)
    tk = z_ref.shape[1]
    sub = DFT_SPLIT
    n_a = ca_ref.shape[1]

    @pl.when(k == 0)
    def _():
        acc_ref[...] = jnp.zeros_like(acc_ref)

    col_a = k * tk + lax.broadcasted_iota(jnp.int32, (n_a, tk), 1)
    pick_a = (col_a // sub == lax.broadcasted_iota(jnp.int32, (n_a, tk), 0)).astype(F32)
    pick_b = (lax.broadcasted_iota(jnp.int32, (sub, tk), 1) % sub == lax.broadcasted_iota(jnp.int32, (sub, tk), 0)).astype(F32)
    ca, sa = _dot(ca_ref[...], pick_a), _dot(sa_ref[...], pick_a)
    cb, sb = _dot(cb_ref[...], pick_b), _dot(sb_ref[...], pick_b)
    cos_t = (ca * cb - sa * sb).astype(BF16)
    sin_t = (sa * cb + ca * sb).astype(BF16)
    acc_ref[...] += _dot(cos_t, z_ref[0]) + _dot(sin_t, z_ref[1])

    @pl.when(k == pl.num_programs(1) - 1)
    def _():
        o_ref[...] = acc_ref[...]


def _seq_dft(tables, zc, batch, tm, tk):
    ca, sa, cb, sb = tables
    l = ca.shape[0]
    fw = FOURIER_WIDTH
    small = lambda a: pl.BlockSpec((tm, a.shape[1]), lambda i, k: (i, 0))
    return pl.pallas_call(
        _seq_dft_kernel,
        grid=(l // tm, l // tk),
        in_specs=[small(ca), small(sa), small(cb), small(sb),
                  pl.BlockSpec((2, tk, batch * fw), lambda i, k: (0, k, 0))],
        out_specs=pl.BlockSpec((tm, batch * fw), lambda i, k: (i, 0)),
        out_shape=jax.ShapeDtypeStruct((l, batch * fw), F32),
        scratch_shapes=[pltpu.VMEM((tm, batch * fw), F32)],
        compiler_params=_cparams(("parallel", "arbitrary")),
        name="fourier_sequence",
    )(ca, sa, cb, sb, zc)


def _dft_tables(l):
    m = jnp.arange(l, dtype=jnp.int32)[:, None]
    n1 = l // DFT_SPLIT
    a = (m * (jnp.arange(n1, dtype=jnp.int32)[None, :] * DFT_SPLIT)) % l
    b = (m * jnp.arange(DFT_SPLIT, dtype=jnp.int32)[None, :]) % l
    wa = a.astype(F32) * (2.0 * np.pi / l)
    wb = b.astype(F32) * (2.0 * np.pi / l)
    tables = (jnp.cos(wa), jnp.sin(wa), jnp.cos(wb), jnp.sin(wb))
    gd = FOURIER_GROUP_DIM
    cc = (jnp.arange(gd, dtype=jnp.int32)[:, None] * jnp.arange(gd, dtype=jnp.int32)[None, :]) % gd
    wc = cc.astype(F32) * (2.0 * np.pi / gd)
    scale = 1.0 / np.sqrt(float(l) * gd)
    eye = jnp.eye(FOURIER_WIDTH // gd, dtype=F32)
    chan = jnp.concatenate([jnp.kron(eye, jnp.cos(wc)), -jnp.kron(eye, jnp.sin(wc))], axis=1) * scale
    return tables, chan.astype(BF16)


def _rwkv_prepare(d, pos, z_ref, zp_ref, zn_ref, par, nc_ctx, nc_tot):
    mu, kks, ka, rk, w0, w2, a0, a2, bd = par
    c = z_ref.shape[0]
    n = RWKV_DIM
    seg_first = jnp.logical_or(pos == 0, pos == nc_ctx)
    seg_last = jnp.logical_or(pos == nc_ctx - 1, pos == nc_tot - 1)
    z = z_ref[...]
    row = lax.broadcasted_iota(jnp.int32, z.shape, 0)
    prev_row = jnp.where(seg_first, 0.0, zp_ref[7:8, :])
    next_row = jnp.where(seg_last, 0.0, zn_ref[0:1, :])
    z_prev = jnp.where(row == 0, prev_row, pltpu.roll(z, 1, 0))
    z_next = jnp.where(row == c - 1, next_row, pltpu.roll(z, c - 1, 0))
    zs = z + mu * (0.5 * (z_prev + z_next) - z)

    r, k, v = zs[:, 0:n], zs[:, n:2 * n], zs[:, 2 * n:3 * n]
    zw = zs[:, 3 * n:3 * n + RWKV_RANK_PAD]
    za = zs[:, 3 * n + RWKV_RANK_PAD:3 * n + 2 * RWKV_RANK_PAD]
    zg = zs[:, 3 * n + 2 * RWKV_RANK_PAD:3 * n + 3 * RWKV_RANK_PAD]

    kk = k * kks
    kk = kk * lax.rsqrt(_dot(kk * kk, bd) + L2_EPS)
    w_log = _log_sigmoid(w0[d] + _dot(jnp.tanh(zw), w2[d])) - 0.5
    lw = -jnp.exp(w_log)
    a = jax.nn.sigmoid(a0[d] + _dot(za, a2[d]))
    kd = k * (1.0 + (a - 1.0) * ka)
    beta = kk * a

    cl = _dot((_visit_order(d, c) >= 0).astype(F32), lw)
    c_tot = jnp.sum(lw, axis=0, keepdims=True)
    grow = jnp.exp(-cl)
    tail = jnp.exp(c_tot - cl)
    ops = dict(k_s=kd * grow,
               b_s=beta * grow,
               kap_s=kk * jnp.exp(cl - lw),
               r_s=r * jnp.exp(cl),
               k_e=kd * tail,
               b_e=beta * tail,
               gam=jnp.exp(c_tot), v=v,
               bonus=_dot(r * kd * rk, bd),
               sign=1 if d == 0 else -1)
    return ops, zg


def _rwkv_kernel(zf_ref, zfp_ref, zfn_ref, zb_ref, zbp_ref, zbn_ref, mu_ref, kks_ref, ka_ref, rk_ref, w0_ref, w2_ref,
                 a0_ref, a2_ref, g2_ref, bd_ref, yf_ref, yb_ref, gate_ref, st_ref, *, nc_ctx, nc_tot):
    s = pl.program_id(1)
    c = zf_ref.shape[0]
    hd = HEAD_DIM
    pw = 2 * hd

    @pl.when(s == 0)
    def _():
        st_ref[...] = jnp.zeros_like(st_ref)

    par = (mu_ref[...], kks_ref[...], ka_ref[...], rk_ref[...], w0_ref, w2_ref, a0_ref, a2_ref, bd_ref[...])
    fwd, zg = _rwkv_prepare(0, _chunk_pos(s, 0, nc_ctx, nc_tot), zf_ref, zfp_ref, zfn_ref, par, nc_ctx, nc_tot)
    bwd, _ = _rwkv_prepare(1, _chunk_pos(s, 1, nc_ctx, nc_tot), zb_ref, zbp_ref, zbn_ref, par, nc_ctx, nc_tot)
    gate_ref[...] = _dot(jax.nn.sigmoid(zg), g2_ref[...])

    lane = lax.broadcasted_iota(jnp.int32, (c, pw), 1)
    rowi = lax.broadcasted_iota(jnp.int32, (c, pw), 0)
    left = lane < hd
    eye_p = (rowi == lane % hd).astype(F32)
    same_head = (lax.broadcasted_iota(jnp.int32, (pw, pw), 0) < hd) == (lax.broadcasted_iota(jnp.int32, (pw, pw), 1) < hd)

    def bd(y):
        return jnp.concatenate([jnp.where(left, y, 0.0), jnp.where(left, 0.0, y)], axis=0)

    n_pairs = RWKV_HEADS // 2
    prob = [(o, slice(i * pw, (i + 1) * pw)) for o in (fwd, bwd) for i in range(n_pairs)]
    ahead = [(rowi - lane % hd) * o['sign'] for o, _ in prob]
    get = lambda name: [o[name][:, s_] for o, s_ in prob]
    kap, r_s, k_s, b_s, k_e, b_e, vp, gam, bonus = (get(x) for x in ('kap_s', 'r_s', 'k_s', 'b_s', 'k_e', 'b_e', 'v', 'gam', 'bonus'))
    p = [_dot_nt(jnp.concatenate([a_, b_], axis=0), jnp.concatenate([bd(c_), bd(d_)], axis=0))
         for a_, b_, c_, d_ in zip(kap, r_s, k_s, b_s)]
    m1 = [jnp.where(h_ > 0, x[0:c, 0:pw], 0.0) for x, h_ in zip(p, ahead)]
    m2 = [jnp.where(h_ > 0, x[0:c, pw:2 * pw], 0.0) for x, h_ in zip(p, ahead)]
    n1 = [jnp.where(h_ >= 0, x[c:2 * c, 0:pw], 0.0) for x, h_ in zip(p, ahead)]
    n2 = [jnp.where(h_ >= 0, x[c:2 * c, pw:2 * pw], 0.0) for x, h_ in zip(p, ahead)]
    m1v = [_dot(a_, bd(b_)) for a_, b_ in zip(m1, vp)]
    t_inv = [eye_p - x for x in m2]
    q = [_dot(x, bd(x)) for x in m2]
    span = 2
    while 2 * span < c:
        both = [_dot(jnp.concatenate([t_, q_], axis=0), bd(q_)) for t_, q_ in zip(t_inv, q)]
        t_inv = [t_ + x[0:c] for t_, x in zip(t_inv, both)]
        q = [x[c:2 * c] for x in both]
        span *= 2
    t_inv = [t_ + _dot(t_, bd(q_)) for t_, q_ in zip(t_inv, q)]
    tx = [_dot(t_, jnp.concatenate([bd(a_), bd(mv)], axis=1)) for t_, a_, mv in zip(t_inv, kap, m1v)]
    st = [st_ref[i] for i in range(len(prob))]
    su = [_dot_nt(jnp.concatenate([x[:, 0:pw], r_], axis=0), s0) for x, r_, s0 in zip(tx, r_s, st)]
    u = [x[0:c] + y_[:, pw:2 * pw] for x, y_ in zip(su, tx)]
    ys = [x[c:2 * c] + _dot(jnp.concatenate([a_, -b_], axis=1), jnp.concatenate([bd(v_), bd(u_)], axis=0)) + bo * v_
          for x, a_, b_, v_, u_, bo in zip(su, n1, n2, vp, u, bonus)]
    for i in range(len(prob)):
        upd = _dot_tn(jnp.concatenate([vp[i], u[i]], axis=0), jnp.concatenate([k_e[i], -b_e[i]], axis=0))
        st_ref[i] = st[i] * gam[i] + jnp.where(same_head, upd, 0.0)
    yf_ref[...] = jnp.concatenate(ys[:n_pairs], axis=1)
    yb_ref[...] = jnp.concatenate(ys[n_pairs:], axis=1)


def _rwkv(zr, mu, kk_scale, k_a, r_k, w0, w2_pad, a0, a2_pad, g2, batch, nc_ctx, nc_tot):
    t, zw_ = zr.shape
    c = SEQ_CHUNK
    n = RWKV_DIM
    bd = _block_ones(RWKV_HEADS, HEAD_DIM)
    full = lambda a: pl.BlockSpec(a.shape, lambda b, s: (0,) * a.ndim)
    sub = c // 8
    n_sub = t // 8

    def rows(d):
        return lambda b, s: b * nc_tot + _chunk_pos(s, d, nc_ctx, nc_tot)

    def z_specs(d):
        rw = rows(d)
        return [pl.BlockSpec((c, zw_), lambda b, s: (rw(b, s), 0)),
                pl.BlockSpec((8, zw_), lambda b, s: (jnp.maximum(rw(b, s) * sub - 1, 0), 0)),
                pl.BlockSpec((8, zw_), lambda b, s: (jnp.minimum((rw(b, s) + 1) * sub, n_sub - 1), 0))]

    vec = lambda a: a.reshape(1, -1)
    args = (vec(mu), vec(kk_scale), vec(k_a), vec(r_k), w0.reshape(2, 1, n), w2_pad, a0.reshape(2, 1, n), a2_pad, g2, bd)
    out = lambda d: pl.BlockSpec((c, n), lambda b, s: (rows(d)(b, s), 0))
    return pl.pallas_call(
        functools.partial(_rwkv_kernel, nc_ctx=nc_ctx, nc_tot=nc_tot),
        grid=(batch, nc_tot),
        in_specs=z_specs(0) + z_specs(1) + [full(a) for a in args],
        out_specs=[out(0), out(1), out(0)],
        out_shape=[jax.ShapeDtypeStruct((t, n), F32)] * 3,
        scratch_shapes=[pltpu.VMEM((RWKV_HEADS, 2 * HEAD_DIM, 2 * HEAD_DIM), F32)],
        compiler_params=_cparams(("parallel", "arbitrary")),
        name="rwkv_scan",
    )(zr, zr, zr, zr, zr, zr, *args)


def _merge_odd_kernel(y0_ref, y1_ref, gate_ref, fo_ref, x_ref, mod_ref, lg_ref, lb_ref, bd_ref, w1_ref, w2_ref,
                      fg_ref, rw_ref, rb_ref, xo_ref, ho_ref, go_ref):
    y = y0_ref[...] + y1_ref[...]
    bd = bd_ref[...]
    mean = _dot(y, bd) * (1.0 / HEAD_DIM)
    yc = y - mean
    var = _dot(yc * yc, bd) * (1.0 / HEAD_DIM)
    rw = (yc * lax.rsqrt(var + RWKV_GN_EPS) * lg_ref[...] + lb_ref[...]) * gate_ref[...]
    out = _dot(fo_ref[...].astype(BF16), w1_ref[...]) + _dot(rw.astype(BF16), w2_ref[...])
    _residual_and_route(x_ref[...], out, mod_ref[0], fg_ref[...], rw_ref, rb_ref, xo_ref, ho_ref, go_ref)


def _merge_odd(y0, y1, gate, fo, x, mods, ln_g, ln_b, w_out, ffn_gain, router_wt, router_b, tm, batch, l_ctx, l_lat):
    d = x.shape[1]
    n = RWKV_DIM
    fw = FOURIER_WIDTH
    l_tot = l_ctx + l_lat
    nt = l_lat // tm
    t_out = batch * l_lat
    full = lambda a: pl.BlockSpec(a.shape, lambda b, i: (0,) * a.ndim)
    src = lambda b, i: b * (l_tot // tm) + l_ctx // tm + i
    bd = _block_ones(RWKV_HEADS, HEAD_DIM)
    w1, w2 = w_out[:fw].astype(BF16), w_out[fw:].astype(BF16)
    lg, lb, fg, rb = ln_g.reshape(1, n), ln_b.reshape(1, n), ffn_gain.reshape(1, d), router_b.reshape(N_EXPERTS, 1)
    return pl.pallas_call(
        _merge_odd_kernel,
        grid=(batch, nt),
        in_specs=[pl.BlockSpec((tm, n), lambda b, i: (src(b, i), 0)),
                  pl.BlockSpec((tm, n), lambda b, i: (src(b, i), 0)),
                  pl.BlockSpec((tm, n), lambda b, i: (src(b, i), 0)),
                  pl.BlockSpec((tm, fw), lambda b, i: (i, b)),
                  pl.BlockSpec((tm, d), lambda b, i: (src(b, i), 0)),
                  pl.BlockSpec((1, 6, d), lambda b, i: (b, 0, 0)),
                  full(lg), full(lb), full(bd), full(w1), full(w2), full(fg), full(router_wt), full(rb)],
        out_specs=[pl.BlockSpec((tm, d), lambda b, i: (b * nt + i, 0)),
                   pl.BlockSpec((tm * (d // LANES), LANES), lambda b, i: (b * nt + i, 0)),
                   pl.BlockSpec((8, tm), lambda b, i: (0, b * nt + i))],
        out_shape=[jax.ShapeDtypeStruct((t_out, d), F32),
                   jax.ShapeDtypeStruct((t_out * (d // LANES), LANES), F32),
                   jax.ShapeDtypeStruct((8, t_out), F32)],
        compiler_params=_cparams(("parallel", "parallel")),
        name="merge_odd",
    )(y0, y1, gate, fo, x, mods, lg, lb, bd, w1, w2, fg, router_wt, rb)


def _rope_tables(l_ctx, l_lat):
    rows = l_lat // GRID_W
    row = jnp.repeat(jnp.arange(rows, dtype=F32), GRID_W)
    col = jnp.tile(jnp.arange(GRID_W, dtype=F32), rows)
    n_freq = HEAD_DIM // 4
    inv_freq = ROPE_THETA ** (-jnp.arange(n_freq, dtype=F32) / n_freq)
    ang = jnp.concatenate([row[:, None] * inv_freq, col[:, None] * inv_freq], axis=-1)
    cos, sin = jnp.cos(ang), jnp.sin(ang)
    cos64 = jnp.concatenate([cos, cos], axis=1)
    sin64 = jnp.concatenate([-sin, sin], axis=1)
    cos64 = jnp.concatenate([jnp.ones((l_ctx, HEAD_DIM), F32), cos64], axis=0)
    sin64 = jnp.concatenate([jnp.zeros((l_ctx, HEAD_DIM), F32), sin64], axis=0)
    return cos64, sin64


def _pad_rank(w):
    _, r, n = w.shape
    out = jnp.zeros((2, RWKV_RANK_PAD, n), w.dtype)
    out = out.at[0, 0:r].set(w[0])
    return out.at[1, r:2 * r].set(w[1])


def _even_layer(ctx, x_lat, mods, p, batch, l_ctx, l_lat, tm, mod_row):
    l_tot = l_ctx + l_lat
    nc_ctx, nc_tot = l_ctx // SEQ_CHUNK, l_tot // SEQ_CHUNK
    hk, hv = GLA_HEADS * GLA_DK, GLA_HEADS * GLA_DV
    qw, kw = ATT_HEADS * HEAD_DIM, ATT_KV_HEADS * HEAD_DIM
    w_in = p['w_in']
    o = np.cumsum([0, hk, hk, hv, hv, 2 * GLA_LOWRANK, qw, kw, kw])
    w_gla = jnp.concatenate([w_in[:, o[0]:o[4]]], axis=1).astype(BF16)
    w_dec = jnp.pad(w_in[:, o[4]:o[5]], ((0, 0), (0, 128 - 2 * GLA_LOWRANK))).astype(BF16)
    w_q = w_in[:, o[5]:o[6]].astype(BF16)
    w_kv = w_in[:, o[6]:o[8]].astype(BF16)
    cos64, sin64 = _rope_tables(l_ctx, l_lat)
    x, zg, zdec, qn, kn, vn = _project_even(ctx, x_lat, mods, p['norm_mix'], [w_gla, w_dec, w_q, w_kv], cos64, sin64,
                                            p['q_norm'], p['k_norm'], tm, mod_row, batch, l_ctx, l_lat)

    dec_w_pad = _pad_rank(p['dec_w'])
    o_f, o_b = _gla(zg, zdec, dec_w_pad, p['dec_b'].reshape(2, 1, hk), batch, nc_ctx, nc_tot)
    o_att = _attention(qn, kn, vn, p['sink'], batch, l_ctx, l_tot)

    x1, h, gates = _merge_even(o_f, o_b, zg, o_att, x, mods, p['out_norm'], p['w_out'], p['norm_ffn'],
                               p['router_wt'], p['router_b'], tm, mod_row)
    pairs = _moe(h, gates, x1, mods, p['moe_g'], p['moe_u'], p['moe_d'], p['moe_layer'], tm, mod_row, tm, combine=False)
    return pairs, x1


def _odd_layer(pairs, x_prev, mods_prev, mods, p, batch, l_ctx, l_lat, tm, mod_row, tm_moe):
    l_tot = l_ctx + l_lat
    nc_ctx, nc_tot = l_ctx // SEQ_CHUNK, l_tot // SEQ_CHUNK
    n = RWKV_DIM
    fw = FOURIER_WIDTH
    w_in = p['w_in']
    rank_w, rank_a = p['w2'].shape[1], p['a2'].shape[1]
    o = np.cumsum([0, fw, n, n, n, 2 * rank_w, 2 * rank_a])
    pad_cols = lambda w: jnp.pad(w, ((0, 0), (0, RWKV_RANK_PAD - w.shape[1])))
    w_f = w_in[:, o[0]:o[1]].astype(BF16)
    w_r = jnp.concatenate([w_in[:, o[1]:o[4]], pad_cols(w_in[:, o[4]:o[5]]), pad_cols(w_in[:, o[5]:o[6]]),
                           w_in[:, o[6]:]], axis=1).astype(BF16)
    x, (zf, zr) = _project(pairs, x_prev, mods_prev, mods, p['norm_mix'], [w_f, w_r], tm, mod_row)

    mu = p['mu']
    mu_r = jnp.concatenate([mu[0:3 * n], pad_cols(mu[None, 3 * n:3 * n + 2 * rank_w])[0],
                            pad_cols(mu[None, 3 * n + 2 * rank_w:3 * n + 2 * rank_w + 2 * rank_a])[0],
                            mu[3 * n + 2 * rank_w + 2 * rank_a:]])
    y0, y1, gate = _rwkv(zr, mu_r, p['kk_scale'], p['k_a'], p['r_k'].reshape(-1), p['w0'], _pad_rank(p['w2']),
                    p['a0'], _pad_rank(p['a2']), p['g2'], batch, nc_ctx, nc_tot)

    tables, chan = _dft_tables(l_lat)
    zc = _chan_dft(zf, chan, batch, l_ctx, l_lat, tm)
    fo = _seq_dft(tables, zc, batch, min(512, l_lat), min(1024, l_lat))

    x1, h, gates = _merge_odd(y0, y1, gate, fo, x, mods, p['ln_g'], p['ln_b'], p['w_out'], p['norm_ffn'],
                              p['router_wt'], p['router_b'], tm, batch, l_ctx, l_lat)
    lat_tiles = l_lat // tm_moe
    return _moe(h, gates, x1, mods, p['moe_g'], p['moe_u'], p['moe_d'], p['moe_layer'], tm_moe,
                lambda i: i // lat_tiles, tm)


def kernel(x, c, ctx, c_ctx, ada_w, ada_b, norm_mix, norm_ffn, even_w_in, even_w_out, gla_dec_w, gla_dec_b, gla_out_norm, att_q_norm, att_k_norm, att_sink, odd_w_in, odd_w_out, rwkv_mu, rwkv_w0, rwkv_w2, rwkv_a0, rwkv_a2, rwkv_g2, rwkv_kk_scale, rwkv_k_a, rwkv_r_k, rwkv_ln_g, rwkv_ln_b, router_w, router_b, moe_w_gate, moe_w_up, moe_w_down):
    batch, l_lat, d = x.shape
    l_ctx = ctx.shape[1]
    l_tot = l_ctx + l_lat
    assert batch < 8 and ada_w.shape[0] == 2
    tm = 256 if (l_ctx % 256 == 0 and l_lat % 256 == 0) else 128
    tm_moe = 512 if (l_lat % 512 == 0 and tm == 256) else tm
    assert l_ctx % tm == 0 and l_lat % tm == 0 and l_tot % l_ctx == 0 and l_lat % GRID_W == 0

    cc = jnp.concatenate([c, c_ctx[None, :], jnp.zeros((8 - batch - 1, d), F32)], axis=0)
    tiles_per_b = l_tot // tm
    ctx_tiles = l_ctx // tm

    def mod_row(i):
        return jnp.where(i % tiles_per_b < ctx_tiles, batch, i // tiles_per_b)

    rw_pad = jnp.pad(router_w, ((0, 0), (0, LANES - N_EXPERTS)))
    rw_hi = rw_pad.astype(BF16)
    router_wt = jnp.stack([rw_hi, (rw_pad - rw_hi.astype(F32)).astype(BF16)])
    moe = lambda layer: dict(moe_g=moe_w_gate, moe_u=moe_w_up, moe_d=moe_w_down, moe_layer=layer)

    mods_all = _modvec(cc, ada_w, ada_b)
    mods0 = mods_all[0]
    p0 = dict(w_in=even_w_in[0], w_out=even_w_out[0], dec_w=gla_dec_w[0], dec_b=gla_dec_b[0],
              out_norm=gla_out_norm[0], q_norm=att_q_norm[0], k_norm=att_k_norm[0], sink=att_sink[0],
              norm_mix=norm_mix[0], norm_ffn=norm_ffn[0], router_wt=router_wt, router_b=router_b, **moe(0))
    pairs, x1 = _even_layer(ctx.reshape(batch * l_ctx, d), x.reshape(batch * l_lat, d), mods0, p0, batch, l_ctx, l_lat,
                            tm, mod_row)

    mods1 = mods_all[1]
    p1 = dict(w_in=odd_w_in[0], w_out=odd_w_out[0], mu=rwkv_mu[0], w0=rwkv_w0[0], w2=rwkv_w2[0], a0=rwkv_a0[0],
              a2=rwkv_a2[0], g2=rwkv_g2[0], kk_scale=rwkv_kk_scale[0], k_a=rwkv_k_a[0], r_k=rwkv_r_k[0],
              ln_g=rwkv_ln_g[0], ln_b=rwkv_ln_b[0], norm_mix=norm_mix[1], norm_ffn=norm_ffn[1],
              router_wt=router_wt, router_b=router_b, **moe(1))
    out = _odd_layer(pairs, x1, mods0, mods1, p1, batch, l_ctx, l_lat, tm, mod_row, tm_moe)
    return out.reshape(batch, l_lat, d)
```

```python
import functools
import math

import jax
import jax.numpy as jnp
import numpy as np
from jax import lax
from jax.experimental import pallas as pl
from jax.experimental.pallas import tpu as pltpu

F32 = jnp.float32
BF16 = jnp.bfloat16

GRID_W = 64
HEAD_DIM = 64
NORM_EPS = 1e-6
L2_EPS = 1e-12

GLA_DV = 64
GLA_DK = 32
GLA_HEADS = 8
GLA_LOWRANK = 16
GLA_TAU = 16.0

ATT_HEADS = 8
ATT_KV_HEADS = 2
ATT_GROUP = ATT_HEADS // ATT_KV_HEADS
ATT_BLOCK = 128
ROPE_THETA = 10000.0

FOURIER_GROUP_DIM = 64
FOURIER_WIDTH = 256
DFT_SPLIT = 64

RWKV_DIM = 768
RWKV_HEADS = 12
RWKV_RANK_PAD = 128
RWKV_GN_EPS = 64e-5

N_EXPERTS = 16
N_GROUPS = 4
PER_GROUP = N_EXPERTS // N_GROUPS
D_EXPERT = 512
MOE_TILE = 256
MOE_GATHER_DEPTH = 3
LANES = 128
SUBLANES = 8

SEQ_CHUNK = 64
GLA_SCAN_BATCH = 4
RWKV_SCAN_BATCH = 2
VMEM_LIMIT = 56 * 1024 * 1024


def _cparams(sem):
    return pltpu.CompilerParams(dimension_semantics=sem, vmem_limit_bytes=VMEM_LIMIT)


def _dot(a, b):
    return jnp.dot(a, b, preferred_element_type=F32)


def _dot_nt(a, b):
    return lax.dot_general(a, b, (((1,), (1,)), ((), ())), preferred_element_type=F32)


def _dot_tn(a, b):
    return lax.dot_general(a, b, (((0,), (0,)), ((), ())), preferred_element_type=F32)


def _silu(x):
    return x * jax.nn.sigmoid(x)


def _log_sigmoid(x):
    return jnp.minimum(x, 0.0) - jnp.log(1.0 + jnp.exp(-jnp.abs(x)))


def _modulated_norm(x, gain, shift, scale):
    ms = jnp.mean(x * x, axis=-1, keepdims=True)
    return (x * lax.rsqrt(ms + NORM_EPS) * gain) * (1.0 + scale) + shift


def _block_ones(n_blocks, width):
    return jnp.kron(jnp.eye(n_blocks, dtype=F32), jnp.ones((width, width), F32))


def _store_row_tiles(ref, x, first_row=0):
    n, w = x.shape
    g = w // LANES
    for k in range(g):
        ref[pl.ds(first_row * g + k, n, stride=g), :] = x[:, k * LANES:(k + 1) * LANES]


def _load_row_tiles(ref, n, g, first_row=0):
    return jnp.concatenate([ref[pl.ds(first_row * g + k, n, stride=g), :] for k in range(g)], axis=1)


def _modvec_kernel(c_ref, w_ref, b_ref, o_ref):
    o_ref[...] = _dot(_silu(c_ref[...]), w_ref[...]) + b_ref[...]


def _modvec(cc, w, b):
    d = cc.shape[1]
    layers, _, n = w.shape
    tn = n // 4
    out = pl.pallas_call(
        _modvec_kernel,
        grid=(layers, n // tn),
        in_specs=[pl.BlockSpec((8, d), lambda l, j: (0, 0)),
                  pl.BlockSpec((None, d, tn), lambda l, j: (l, 0, j)),
                  pl.BlockSpec((None, 1, tn), lambda l, j: (l, 0, j))],
        out_specs=pl.BlockSpec((None, 8, tn), lambda l, j: (l, 0, j)),
        out_shape=jax.ShapeDtypeStruct((layers, 8, n), F32),
        compiler_params=_cparams(("parallel", "parallel")),
        name="modvec",
    )(cc, w, b.reshape(layers, 1, n))
    return out.reshape(layers, 8, 6, d)


def _proj_kernel(ya_ref, yb_ref, x_ref, mod_prev_ref, mod_ref, gain_ref, *refs, n_out):
    w_refs, x_out_ref, z_refs = refs[:n_out], refs[n_out], refs[n_out + 1:]
    n, d = x_ref.shape
    y = _load_row_tiles(ya_ref, n, d // LANES) + _load_row_tiles(yb_ref, n, d // LANES)
    x = x_ref[...] + mod_prev_ref[0][5:6] * y
    x_out_ref[...] = x
    m = mod_ref[0]
    h = _modulated_norm(x, gain_ref[...], m[0:1], m[1:2]).astype(BF16)
    for w_ref, z_ref in zip(w_refs, z_refs):
        z_ref[...] = _dot(h, w_ref[...])


def _project(pairs, x, mods_prev, mods, gain, weights, tm, mod_row):
    t, d = x.shape
    g = d // LANES
    n_out = len(weights)
    mod_spec = pl.BlockSpec((1, 6, d), lambda i: (mod_row(i), 0, 0))
    in_specs = [pl.BlockSpec((tm * g, LANES), lambda i: (i, 0)),
                pl.BlockSpec((tm * g, LANES), lambda i: (t // tm + i, 0)),
                pl.BlockSpec((tm, d), lambda i: (i, 0)),
                mod_spec, mod_spec,
                pl.BlockSpec((1, d), lambda i: (0, 0))]
    in_specs += [pl.BlockSpec(w.shape, lambda i: (0, 0)) for w in weights]
    outs = pl.pallas_call(
        functools.partial(_proj_kernel, n_out=n_out),
        grid=(t // tm,),
        in_specs=in_specs,
        out_specs=[pl.BlockSpec((tm, d), lambda i: (i, 0))] + [pl.BlockSpec((tm, w.shape[1]), lambda i: (i, 0)) for w in weights],
        out_shape=[jax.ShapeDtypeStruct((t, d), F32)] + [jax.ShapeDtypeStruct((t, w.shape[1]), F32) for w in weights],
        compiler_params=_cparams(("parallel",)),
        name="proj",
    )(pairs, pairs, x, mods_prev, mods, gain.reshape(1, d), *weights)
    return outs[0], outs[1:]


def _chunk_pos(s, d, nc_ctx, nc_tot):
    back = jnp.where(s < nc_ctx, nc_ctx - 1 - s, nc_tot + nc_ctx - 1 - s)
    return jnp.where(d == 0, s, back)


def _visit_order(d, c):
    sign = 1 if d == 0 else -1
    return (lax.broadcasted_iota(jnp.int32, (c, c), 0) - lax.broadcasted_iota(jnp.int32, (c, c), 1)) * sign


def _gla_prepare(d, q, k, v, dec, dw_ref, db_ref):
    c = q.shape[0]
    g = _log_sigmoid(_dot(dec, dw_ref[d]) + db_ref[d]) / GLA_TAU
    b = _dot((_visit_order(d, c) >= 0).astype(F32), g)
    b_tot = jnp.sum(g, axis=0, keepdims=True)
    return dict(q_in=q * (GLA_DK ** -0.5) * jnp.exp(b), k_out=k * jnp.exp(-b), k_end=k * jnp.exp(b_tot - b),
                decay=jnp.exp(b_tot), v=v, sign=1 if d == 0 else -1)


def _gla_kernel(qf_ref, kf_ref, vf_ref, df_ref, qb_ref, kb_ref, vb_ref, db_ref, dw_ref, dbias_ref, of_ref, ob_ref, st_ref):
    nb, c = qf_ref.shape[0], qf_ref.shape[1]
    group = 4
    kw, vw = group * GLA_DK, group * GLA_DV

    @pl.when(pl.program_id(1) == 0)
    def _():
        st_ref[...] = jnp.zeros_like(st_ref)

    ctxs = []
    for u in range(nb):
        ctxs.append(_gla_prepare(0, qf_ref[u], kf_ref[u], vf_ref[u], df_ref[u], dw_ref, dbias_ref))
        ctxs.append(_gla_prepare(1, qb_ref[u], kb_ref[u], vb_ref[u], db_ref[u], dw_ref, dbias_ref))
    n_quads = GLA_HEADS // group
    prob = [(o, i) for o in ctxs for i in range(n_quads)]
    klane = lax.broadcasted_iota(jnp.int32, (c, kw), 1) // GLA_DK
    half = lax.broadcasted_iota(jnp.int32, (c, 2 * GLA_DV), 1) < GLA_DV
    rowi = lax.broadcasted_iota(jnp.int32, (c, group * c), 0)
    coli = lax.broadcasted_iota(jnp.int32, (c, group * c), 1) % c
    own = (lax.broadcasted_iota(jnp.int32, (vw, kw), 0) // GLA_DV) == (lax.broadcasted_iota(jnp.int32, (vw, kw), 1) // GLA_DK)

    def bd_keys(y):
        return jnp.concatenate([jnp.where(klane == h, y, 0.0) for h in range(group)], axis=0)

    def bd_vals(y):
        return jnp.concatenate([jnp.where(half, y, 0.0), jnp.where(half, 0.0, y)], axis=0)

    ksl = lambda i: slice(i * kw, (i + 1) * kw)
    vsl = lambda i: slice(i * vw, (i + 1) * vw)
    q_in = [o['q_in'][:, ksl(i)] for o, i in prob]
    att = [jnp.where((rowi - coli) * o['sign'] >= 0, _dot_nt(q_, bd_keys(o['k_out'][:, ksl(i)])), 0.0)
           for q_, (o, i) in zip(q_in, prob)]
    st = [st_ref[j] for j in range(len(prob))]
    outs = []
    for j, (o, i) in enumerate(prob):
        v = o['v'][:, vsl(i)]
        intra = jnp.concatenate([_dot(att[j][:, p * 2 * c:(p + 1) * 2 * c], bd_vals(v[:, p * 2 * GLA_DV:(p + 1) * 2 * GLA_DV]))
                                 for p in range(group // 2)], axis=1)
        outs.append(intra + _dot_nt(q_in[j], st[j]))
    for j, (o, i) in enumerate(prob):
        upd = _dot_tn(o['v'][:, vsl(i)], o['k_end'][:, ksl(i)])
        st_ref[j] = st[j] * o['decay'][:, ksl(i)] + jnp.where(own, upd, 0.0)
    for u in range(nb):
        of_ref[u] = jnp.concatenate(outs[2 * u * n_quads:(2 * u + 1) * n_quads], axis=1)
        ob_ref[u] = jnp.concatenate(outs[(2 * u + 1) * n_quads:(2 * u + 2) * n_quads], axis=1)


def _gla(zg, zdec, dec_w_pad, dec_b, batch, nc_ctx, nc_tot):
    t = zg.shape[0]
    c = SEQ_CHUNK
    hk, hv = GLA_HEADS * GLA_DK, GLA_HEADS * GLA_DV
    nb = math.gcd(batch, GLA_SCAN_BATCH)
    zg3, zdec3 = zg.reshape(batch, t // batch, -1), zdec.reshape(batch, t // batch, -1)

    def specs(d):
        pos = lambda s: _chunk_pos(s, d, nc_ctx, nc_tot)
        return [pl.BlockSpec((nb, c, hk), lambda b, s: (b, pos(s), 0)),
                pl.BlockSpec((nb, c, hk), lambda b, s: (b, pos(s), 1)),
                pl.BlockSpec((nb, c, hv), lambda b, s: (b, pos(s), 1)),
                pl.BlockSpec((nb, c, 128), lambda b, s: (b, pos(s), 0))], pl.BlockSpec((nb, c, hv), lambda b, s: (b, pos(s), 0))

    in_f, out_f = specs(0)
    in_b, out_b = specs(1)
    o_f, o_b = pl.pallas_call(
        _gla_kernel,
        grid=(batch // nb, nc_tot),
        in_specs=in_f + in_b + [pl.BlockSpec(dec_w_pad.shape, lambda b, s: (0, 0, 0)),
                                pl.BlockSpec(dec_b.shape, lambda b, s: (0, 0, 0))],
        out_specs=[out_f, out_b],
        out_shape=[jax.ShapeDtypeStruct((batch, t // batch, hv), F32)] * 2,
        scratch_shapes=[pltpu.VMEM((nb * 2 * GLA_HEADS // 4, 4 * GLA_DV, 4 * GLA_DK), F32)],
        compiler_params=_cparams(("parallel", "arbitrary")),
        name="gla_scan",
    )(zg3, zg3, zg3, zdec3, zg3, zg3, zg3, zdec3, dec_w_pad, dec_b)
    return o_f.reshape(t, hv), o_b.reshape(t, hv)


def _rope_swap(x):
    n = x.shape[-1]
    lane = lax.broadcasted_iota(jnp.int32, x.shape, x.ndim - 1)
    half = HEAD_DIM // 2
    return jnp.where(lane % HEAD_DIM < half, pltpu.roll(x, n - half, x.ndim - 1), pltpu.roll(x, half, x.ndim - 1))


def _proj_even_kernel(ctx_ref, x_ref, mod_ref, gain_ref, wg_ref, wd_ref, wq_ref, wkv_ref, cos_ref, sin_ref, qg_ref, kg_ref,
                      bdq_ref, bdk_ref, xs_ref, zg_ref, zdec_ref, qo_ref, ko_ref, vo_ref, *, tiles_per_b, ctx_tiles):
    is_ctx = lax.rem(pl.program_id(0), tiles_per_b) < ctx_tiles
    x = jnp.where(is_ctx, ctx_ref[...], x_ref[...])
    xs_ref[...] = x
    m = mod_ref[0]
    h = _modulated_norm(x, gain_ref[...], m[0:1], m[1:2]).astype(BF16)
    zg_ref[...] = _dot(h, wg_ref[...])
    zdec_ref[...] = _dot(h, wd_ref[...])

    def norm_rope(z, gain, bd, n_heads):
        ms = _dot(z * z, bd) * (1.0 / HEAD_DIM)
        zn = z * lax.rsqrt(ms + NORM_EPS) * gain
        cos = jnp.concatenate([cos_ref[...]] * n_heads, axis=1)
        sin = jnp.concatenate([sin_ref[...]] * n_heads, axis=1)
        return zn * cos + _rope_swap(zn) * sin

    q = norm_rope(_dot(h, wq_ref[...]), qg_ref[...], bdq_ref[...], ATT_HEADS)
    qo_ref[...] = (q * (HEAD_DIM ** -0.5)).astype(BF16)
    kw = ATT_KV_HEADS * HEAD_DIM
    kv = _dot(h, wkv_ref[...])
    ko_ref[...] = norm_rope(kv[:, :kw], kg_ref[...], bdk_ref[...], ATT_KV_HEADS).astype(BF16)
    vo_ref[...] = kv[:, kw:].astype(BF16)


def _project_even(ctx, x, mods, gain, weights, cos64, sin64, q_gain, k_gain, tm, mod_row, batch, l_ctx, l_lat):
    d = x.shape[1]
    l_tot = l_ctx + l_lat
    t = batch * l_tot
    tiles_per_b, ctx_tiles, lat_tiles = l_tot // tm, l_ctx // tm, l_lat // tm
    qw, kw = ATT_HEADS * HEAD_DIM, ATT_KV_HEADS * HEAD_DIM
    w_gla, w_dec, w_q, w_kv = weights
    full = lambda a: pl.BlockSpec(a.shape, lambda i: (0,) * a.ndim)
    qg = jnp.tile(q_gain, ATT_HEADS).reshape(1, qw)
    kg = jnp.tile(k_gain, ATT_KV_HEADS).reshape(1, kw)
    bdq, bdk = _block_ones(ATT_HEADS, HEAD_DIM), _block_ones(ATT_KV_HEADS, HEAD_DIM)
    gain = gain.reshape(1, d)
    row = lambda n: pl.BlockSpec((tm, n), lambda i: (i, 0))
    pos = pl.BlockSpec((tm, HEAD_DIM), lambda i: (i % tiles_per_b, 0))
    return pl.pallas_call(
        functools.partial(_proj_even_kernel, tiles_per_b=tiles_per_b, ctx_tiles=ctx_tiles),
        grid=(t // tm,),
        in_specs=[pl.BlockSpec((tm, d), lambda i: ((i // tiles_per_b) * ctx_tiles + jnp.minimum(i % tiles_per_b, ctx_tiles - 1), 0)),
                  pl.BlockSpec((tm, d), lambda i: ((i // tiles_per_b) * lat_tiles + jnp.maximum(i % tiles_per_b - ctx_tiles, 0), 0)),
                  pl.BlockSpec((1, 6, d), lambda i: (mod_row(i), 0, 0)),
                  full(gain), full(w_gla), full(w_dec), full(w_q), full(w_kv), pos, pos,
                  full(qg), full(kg), full(bdq), full(bdk)],
        out_specs=[row(d), row(w_gla.shape[1]), row(w_dec.shape[1]), row(qw), row(kw), row(kw)],
        out_shape=[jax.ShapeDtypeStruct((t, d), F32), jax.ShapeDtypeStruct((t, w_gla.shape[1]), F32),
                   jax.ShapeDtypeStruct((t, w_dec.shape[1]), F32), jax.ShapeDtypeStruct((t, qw), BF16),
                   jax.ShapeDtypeStruct((t, kw), BF16), jax.ShapeDtypeStruct((t, kw), BF16)],
        compiler_params=_cparams(("parallel",)),
        name="proj_even",
    )(ctx, x, mods, gain, w_gla, w_dec, w_q, w_kv, cos64, sin64, qg, kg, bdq, bdk)


def _attn_kernel(q_ref, kp_ref, kc_ref, kn_ref, kx_ref, vp_ref, vc_ref, vn_ref, vx_ref, sink_ref, o_ref,
                 *, n_ctx_blocks, n_lat_blocks):
    blk = ATT_BLOCK
    n = pl.program_id(1)
    m = n - n_ctx_blocks
    is_lat = n >= n_ctx_blocks
    l_ctx = kx_ref.shape[0]
    width = 3 * blk + l_ctx
    rows = ATT_GROUP * blk
    r = lax.broadcasted_iota(jnp.int32, (rows, width), 0) % blk
    c = lax.broadcasted_iota(jnp.int32, (rows, width), 1)
    lat = is_lat.astype(jnp.int32)
    has_prev = lat * (m >= 1).astype(jnp.int32)
    has_next = lat * (m <= n_lat_blocks - 2).astype(jnp.int32)
    valid = jnp.where(c < blk, (c >= r).astype(jnp.int32) * has_prev,
                      jnp.where(c < 2 * blk, lat,
                                jnp.where(c < 3 * blk, (c - 2 * blk <= r).astype(jnp.int32) * has_next, 1))) > 0
    q = q_ref[...]
    sink = sink_ref[...]
    outs = [None] * ATT_HEADS
    for kvh in range(ATT_KV_HEADS):
        ks = slice(kvh * HEAD_DIM, (kvh + 1) * HEAD_DIM)
        kw = jnp.concatenate([kp_ref[:, ks], kc_ref[:, ks], kn_ref[:, ks], kx_ref[:, ks]], axis=0)
        vw = jnp.concatenate([vp_ref[:, ks], vc_ref[:, ks], vn_ref[:, ks], vx_ref[:, ks]], axis=0)
        heads = range(kvh * ATT_GROUP, (kvh + 1) * ATT_GROUP)
        qg = jnp.concatenate([q[:, h * HEAD_DIM:(h + 1) * HEAD_DIM] for h in heads], axis=0)
        s = jnp.where(valid, _dot_nt(qg, kw), -jnp.inf)
        sk = jnp.concatenate([jnp.broadcast_to(sink[h:h + 1, 0:1], (blk, 1)) for h in heads], axis=0)
        mx = jnp.maximum(jnp.max(s, axis=-1, keepdims=True), sk)
        p = jnp.exp(s - mx)
        denom = jnp.sum(p, axis=-1, keepdims=True) + jnp.exp(sk - mx)
        o = _dot(p.astype(BF16), vw) / denom
        for g, h in enumerate(heads):
            outs[h] = o[g * blk:(g + 1) * blk]
    o_ref[...] = jnp.concatenate(outs, axis=1)


def _attention(qn, kn, vn, sink, batch, l_ctx, l_tot):
    t = qn.shape[0]
    blk = ATT_BLOCK
    nq = l_tot // blk
    nc = l_ctx // blk
    nl = nq - nc
    qw, kw = ATT_HEADS * HEAD_DIM, ATT_KV_HEADS * HEAD_DIM

    def win(off):
        def index(b, n):
            m = jnp.clip(n - nc + off, 0, nl - 1)
            return (b * nq + nc + m, 0)
        return pl.BlockSpec((blk, kw), index)

    ctx_spec = pl.BlockSpec((l_ctx, kw), lambda b, n: (b * (l_tot // l_ctx), 0))
    return pl.pallas_call(
        functools.partial(_attn_kernel, n_ctx_blocks=nc, n_lat_blocks=nl),
        grid=(batch, nq),
        in_specs=[pl.BlockSpec((blk, qw), lambda b, n: (b * nq + n, 0)),
                  win(-1), win(0), win(1), ctx_spec,
                  win(-1), win(0), win(1), ctx_spec,
                  pl.BlockSpec((ATT_HEADS, 128), lambda b, n: (0, 0))],
        out_specs=pl.BlockSpec((blk, qw), lambda b, n: (b * nq + n, 0)),
        out_shape=jax.ShapeDtypeStruct((t, qw), F32),
        compiler_params=_cparams(("parallel", "parallel")),
        name="window_attention",
    )(qn, kn, kn, kn, kn, vn, vn, vn, vn, jnp.broadcast_to(sink.astype(F32)[:, None], (ATT_HEADS, 128)))


def _route(logits_t, bias_col):
    scores = jax.nn.sigmoid(logits_t)
    sel = scores + bias_col
    rows = [sel[e:e + 1] for e in range(N_EXPERTS)]
    grp = []
    for g in range(N_GROUPS):
        r = rows[g * PER_GROUP:(g + 1) * PER_GROUP]
        best = None
        for i in range(PER_GROUP):
            for j in range(i + 1, PER_GROUP):
                pair = r[i] + r[j]
                best = pair if best is None else jnp.maximum(best, pair)
        grp.append(best)
    g_best = jnp.zeros_like(grp[0], dtype=jnp.int32)
    g_val = grp[0]
    for g in range(1, N_GROUPS):
        take = grp[g] > g_val
        g_best = jnp.where(take, g, g_best)
        g_val = jnp.where(take, grp[g], g_val)
    neg = -jnp.inf
    masked = [jnp.where(g_best == e // PER_GROUP, rows[e], neg) for e in range(N_EXPERTS)]
    i1 = jnp.zeros_like(g_best)
    v1 = masked[0]
    for e in range(1, N_EXPERTS):
        take = masked[e] > v1
        i1 = jnp.where(take, e, i1)
        v1 = jnp.where(take, masked[e], v1)
    i2 = jnp.full_like(g_best, -1)
    v2 = jnp.full_like(v1, neg)
    for e in range(N_EXPERTS):
        take = jnp.logical_and(i1 != e, masked[e] > v2)
        i2 = jnp.where(take, e, i2)
        v2 = jnp.where(take, masked[e], v2)
    w1 = jnp.zeros_like(v1)
    w2 = jnp.zeros_like(v1)
    for e in range(N_EXPERTS):
        w1 = jnp.where(i1 == e, scores[e:e + 1], w1)
        w2 = jnp.where(i2 == e, scores[e:e + 1], w2)
    inv = 1.0 / (w1 + w2)
    pad = jnp.zeros_like(w1)
    lane = lax.broadcasted_iota(jnp.int32, w1.shape, 1)
    hist = pad
    for e in range(N_EXPERTS):
        n_e = jnp.sum((i1 == e).astype(F32) + (i2 == e).astype(F32), axis=1, keepdims=True)
        hist = jnp.where(lane == e, n_e, hist)
    return jnp.concatenate([i1.astype(F32), i2.astype(F32), w1 * inv, w2 * inv, hist, pad, pad, pad], axis=0)


def _residual_and_route(x, out, m, ffn_gain, rw_ref, rb_ref, x_ref, h_ref, g_ref):
    x1 = x + m[2:3] * out
    x_ref[...] = x1
    h = _modulated_norm(x1, ffn_gain, m[3:4], m[4:5])
    _store_row_tiles(h_ref, h)
    h_hi = h.astype(BF16)
    h_lo = (h - h_hi.astype(F32)).astype(BF16)
    logits = _dot(h_hi, rw_ref[0]) + _dot(h_lo, rw_ref[0]) + _dot(h_hi, rw_ref[1])
    g_ref[...] = _route(logits.T[:N_EXPERTS], rb_ref[...])


def _merge_even_kernel(o0_ref, o1_ref, gg_ref, oa_ref, x_ref, mod_ref, gn_ref, bd_ref, w1_ref, w2_ref,
                       fg_ref, rw_ref, rb_ref, xo_ref, ho_ref, go_ref):
    og = o0_ref[...] + o1_ref[...]
    ms = _dot(og * og, bd_ref[...]) * (1.0 / GLA_DV)
    g = og * lax.rsqrt(ms + NORM_EPS) * gn_ref[...] * _silu(gg_ref[...])
    out = _dot(g.astype(BF16), w1_ref[...]) + _dot(oa_ref[...].astype(BF16), w2_ref[...])
    _residual_and_route(x_ref[...], out, mod_ref[0], fg_ref[...], rw_ref, rb_ref, xo_ref, ho_ref, go_ref)


def _merge_even(o_f, o_b, zg, o_att, x, mods, out_norm, w_out, ffn_gain, router_wt, router_b, tm, mod_row):
    t, d = x.shape
    hv = GLA_HEADS * GLA_DV
    qw = ATT_HEADS * HEAD_DIM
    full = lambda a: pl.BlockSpec(a.shape, lambda i: (0,) * a.ndim)
    gn = jnp.tile(out_norm, GLA_HEADS).reshape(1, hv)
    bd = _block_ones(GLA_HEADS, GLA_DV)
    w1, w2 = w_out[:hv].astype(BF16), w_out[hv:].astype(BF16)
    fg = ffn_gain.reshape(1, d)
    rb = router_b.reshape(N_EXPERTS, 1)
    return pl.pallas_call(
        _merge_even_kernel,
        grid=(t // tm,),
        in_specs=[pl.BlockSpec((tm, hv), lambda i: (i, 0)),
                  pl.BlockSpec((tm, hv), lambda i: (i, 0)),
                  pl.BlockSpec((tm, hv), lambda i: (i, 2)),
                  pl.BlockSpec((tm, qw), lambda i: (i, 0)),
                  pl.BlockSpec((tm, d), lambda i: (i, 0)),
                  pl.BlockSpec((1, 6, d), lambda i: (mod_row(i), 0, 0)),
                  full(gn), full(bd), full(w1), full(w2), full(fg), full(router_wt), full(rb)],
        out_specs=[pl.BlockSpec((tm, d), lambda i: (i, 0)),
                   pl.BlockSpec((tm * (d // LANES), LANES), lambda i: (i, 0)),
                   pl.BlockSpec((8, tm), lambda i: (0, i))],
        out_shape=[jax.ShapeDtypeStruct((t, d), F32),
                   jax.ShapeDtypeStruct((t * (d // LANES), LANES), F32),
                   jax.ShapeDtypeStruct((8, t), F32)],
        compiler_params=_cparams(("parallel",)),
        name="merge_even",
    )(o_f, o_b, zg, o_att, x, mods, gn, bd, w1, w2, fg, router_wt, rb)


def _moe_plan(route, t, rows, route_tile):
    n_tiles = 2 * t // rows + N_EXPERTS
    eid = jnp.concatenate([route[0], route[1]]).astype(jnp.int32)
    slot = jnp.arange(2 * t, dtype=jnp.int32)
    gate = jnp.concatenate([route[2], route[3]])
    _, s_slot, s_gate = lax.sort((eid, slot, gate), num_keys=1, is_stable=True)
    counts = jnp.sum(route[4].reshape(t // route_tile, route_tile)[:, :N_EXPERTS], axis=0).astype(jnp.int32)
    padded = (counts + rows - 1) // rows * rows
    p_end = jnp.cumsum(padded)
    p_start = p_end - padded
    c_start = jnp.cumsum(counts) - counts
    tile_start = jnp.arange(n_tiles, dtype=jnp.int32) * rows
    tile_e = jnp.minimum(jnp.sum((tile_start[:, None] >= p_end[None, :]).astype(jnp.int32), axis=1), N_EXPERTS - 1)
    r_in = (tile_start - p_start[tile_e])[:, None] + jnp.arange(rows, dtype=jnp.int32)[None, :]
    over = r_in - counts[tile_e][:, None]
    valid = over < 0
    src = jnp.clip(c_start[tile_e][:, None] + r_in, 0, 2 * t - 1)
    g_slot = s_slot[src]
    row_tok = jnp.where(valid, jnp.where(g_slot >= t, g_slot - t, g_slot), 0)
    row_gate = jnp.where(valid, s_gate[src], 0.0)
    row_dst = jnp.where(valid, g_slot, 2 * t + tile_e[:, None] * rows + jnp.clip(over, 0, rows - 1))
    n_used = (p_end[-1] // rows).astype(jnp.int32)
    tile_e = jnp.where(tile_start < p_end[-1], tile_e, tile_e[jnp.maximum(n_used - 1, 0)])
    return (row_tok[:, None, :], row_gate[:, None, :], row_dst[:, None, :], tile_e, n_used.reshape(1))


def _moe_experts_kernel(te_ref, nu_ref, tok0_ref, tok1_ref, tokn_ref, dst_ref, gate_ref, wg_ref, wu_ref, wd_ref, h_hbm,
                        y_hbm, hbuf, ybuf, wgb, wub, wdb, sem_g, sem_s):
    j = pl.program_id(0)
    n_used = nu_ref[0]
    slot = j % 2
    g = SUBLANES
    rows = ybuf.shape[0] // (2 * g)
    tile = rows * g
    depth = hbuf.shape[0] // tile
    gslot = lax.rem(j, depth)

    def row(ref, i):
        return ref.at[pl.ds(pl.multiple_of(i * g, g), g)]

    def start_gather(idx_ref, s):
        for r in range(rows):
            pltpu.make_async_copy(row(h_hbm, idx_ref[0, 0, r]), row(hbuf, s * rows + r), sem_g.at[s]).start(priority=r % 2)

    def wait_gather(s):
        pltpu.make_async_copy(h_hbm.at[pl.ds(0, tile)], hbuf.at[pl.ds(pl.multiple_of(s * tile, tile), tile)],
                              sem_g.at[s]).wait()

    def wait_scatter(s):
        pltpu.make_async_copy(ybuf.at[pl.ds(pl.multiple_of(s * tile, tile), tile)], y_hbm.at[pl.ds(0, tile)],
                              sem_s.at[s]).wait()

    @pl.when(j == 0)
    def _():
        start_gather(tok0_ref, 0)
        start_gather(tok1_ref, 1)
        ybuf[...] = jnp.zeros(ybuf.shape, F32)
        n_real = y_hbm.shape[0] // g - N_EXPERTS * rows
        fills = [pltpu.make_async_copy(ybuf.at[pl.ds((k % 2) * tile, tile)],
                                       y_hbm.at[pl.ds((n_real + k * rows) * g, tile)], sem_s.at[k % 2])
                 for k in range(N_EXPERTS)]
        for f in fills:
            f.start()
        for f in fills[2:]:
            f.wait()

    active = j < n_used
    changed = jnp.logical_or(j == 0, te_ref[j] != te_ref[jnp.maximum(j - 1, 0)])

    @pl.when(jnp.logical_and(active, changed))
    def _():
        wgb[...] = wg_ref[0].astype(BF16)
        wub[...] = wu_ref[0].astype(BF16)
        wdb[...] = wd_ref[0].astype(BF16)

    @pl.when(active)
    def _():
        wait_gather(gslot)
        start_gather(tokn_ref, lax.rem(j + depth - 1, depth))
        h = _load_row_tiles(hbuf, rows, g, first_row=gslot * rows).astype(BF16)
        act = _silu(_dot(h, wgb[...])) * _dot(h, wub[...])
        ri = lax.broadcasted_iota(jnp.int32, (rows, rows), 0)
        ci = lax.broadcasted_iota(jnp.int32, (rows, rows), 1)
        gate = jnp.sum(jnp.where(ri == ci, gate_ref[0], 0.0), axis=1, keepdims=True)
        y = _dot((act * gate).astype(BF16), wdb[...])
        wait_scatter(slot)
        _store_row_tiles(ybuf, y, first_row=slot * rows)
        for r in range(rows):
            pltpu.make_async_copy(row(ybuf, slot * rows + r), row(y_hbm, dst_ref[0, 0, r]), sem_s.at[slot]).start(priority=r % 2)

        @pl.when(j == n_used - 1)
        def _():
            wait_scatter(slot)
            wait_scatter(1 - slot)
            for ahead in range(1, depth):
                wait_gather(lax.rem(j + ahead, depth))


def _moe_combine_kernel(ya_ref, yb_ref, x_ref, mod_ref, o_ref):
    n, d = x_ref.shape
    y = _load_row_tiles(ya_ref, n, d // LANES) + _load_row_tiles(yb_ref, n, d // LANES)
    o_ref[...] = x_ref[...] + mod_ref[0][5:6] * y


def _moe(h, route, x, mods, w_gate, w_up, w_down, layer, tm, mod_row, route_tile, combine=True):
    t, d = x.shape
    g = d // LANES
    assert g == SUBLANES
    rows = MOE_TILE
    row_tok, row_gate, row_dst, tile_e, n_used = _moe_plan(route, t, rows, route_tile)
    n_tiles = row_tok.shape[0]
    n_pair_rows = 2 * t + N_EXPERTS * rows
    assert MOE_GATHER_DEPTH == 3
    smem_tile = lambda which: pl.BlockSpec((1, 1, rows), lambda j, te, nu: (jnp.minimum(which(j), nu[0] - 1), 0, 0),
                                           memory_space=pltpu.SMEM)
    pairs = pl.pallas_call(
        _moe_experts_kernel,
        grid_spec=pltpu.PrefetchScalarGridSpec(
            num_scalar_prefetch=2,
            grid=(n_tiles,),
            in_specs=[smem_tile(lambda j: 0), smem_tile(lambda j: 1), smem_tile(lambda j: j + MOE_GATHER_DEPTH - 1),
                      smem_tile(lambda j: j),
                      pl.BlockSpec((1, 1, rows), lambda j, te, nu: (j, 0, 0)),
                      pl.BlockSpec((None, 1, d, D_EXPERT), lambda j, te, nu: (layer, te[j], 0, 0)),
                      pl.BlockSpec((None, 1, d, D_EXPERT), lambda j, te, nu: (layer, te[j], 0, 0)),
                      pl.BlockSpec((None, 1, D_EXPERT, d), lambda j, te, nu: (layer, te[j], 0, 0)),
                      pl.BlockSpec(memory_space=pl.ANY)],
            out_specs=pl.BlockSpec(memory_space=pl.ANY),
            scratch_shapes=[pltpu.VMEM((MOE_GATHER_DEPTH * rows * g, LANES), F32), pltpu.VMEM((2 * rows * g, LANES), F32),
                            pltpu.VMEM((d, D_EXPERT), BF16), pltpu.VMEM((d, D_EXPERT), BF16),
                            pltpu.VMEM((D_EXPERT, d), BF16),
                            pltpu.SemaphoreType.DMA((MOE_GATHER_DEPTH,)), pltpu.SemaphoreType.DMA((2,))]),
        out_shape=jax.ShapeDtypeStruct((n_pair_rows * g, LANES), F32),
        compiler_params=_cparams(("arbitrary",)),
        name="moe_experts",
    )(tile_e, n_used, row_tok, row_tok, row_tok, row_dst, row_gate, w_gate, w_up, w_down, h)
    if not combine:
        return pairs
    return pl.pallas_call(
        _moe_combine_kernel,
        grid=(t // tm,),
        in_specs=[pl.BlockSpec((tm * g, LANES), lambda i: (i, 0)),
                  pl.BlockSpec((tm * g, LANES), lambda i: (t // tm + i, 0)),
                  pl.BlockSpec((tm, d), lambda i: (i, 0)),
                  pl.BlockSpec((1, 6, d), lambda i: (mod_row(i), 0, 0))],
        out_specs=pl.BlockSpec((tm, d), lambda i: (i, 0)),
        out_shape=jax.ShapeDtypeStruct((t, d), F32),
        compiler_params=_cparams(("parallel",)),
        name="moe_combine",
    )(pairs, pairs, x, mods)


def _chan_dft_kernel(z_ref, w_ref, o_ref):
    res = _dot(z_ref[...].astype(BF16), w_ref[...]).astype(BF16)
    o_ref[0] = res[:, :FOURIER_WIDTH]
    o_ref[1] = res[:, FOURIER_WIDTH:]


def _chan_dft(zf, w, batch, l_ctx, l_lat, tm):
    l_tot = l_ctx + l_lat
    nt = l_lat // tm
    fw = FOURIER_WIDTH
    return pl.pallas_call(
        _chan_dft_kernel,
        grid=(batch, nt),
        in_specs=[pl.BlockSpec((tm, fw), lambda b, i: (b * (l_tot // tm) + l_ctx // tm + i, 0)),
                  pl.BlockSpec(w.shape, lambda b, i: (0, 0))],
        out_specs=pl.BlockSpec((2, tm, fw), lambda b, i: (0, i, b)),
        out_shape=jax.ShapeDtypeStruct((2, l_lat, batch * fw), BF16),
        compiler_params=_cparams(("parallel", "parallel")),
        name="fourier_channels",
    )(zf, w)


def _seq_dft_kernel(ca_ref, sa_ref, cb_ref, sb_ref, z_ref, o_ref, acc_ref):
    k = pl.program_id(1)
    tk = z_ref.shape[1]
    sub = DFT_SPLIT
    n_a = ca_ref.shape[1]

    @pl.when(k == 0)
    def _():
        acc_ref[...] = jnp.zeros_like(acc_ref)

    col_a = k * tk + lax.broadcasted_iota(jnp.int32, (n_a, tk), 1)
    pick_a = (col_a // sub == lax.broadcasted_iota(jnp.int32, (n_a, tk), 0)).astype(F32)
    pick_b = (lax.broadcasted_iota(jnp.int32, (sub, tk), 1) % sub == lax.broadcasted_iota(jnp.int32, (sub, tk), 0)).astype(F32)
    ca, sa = _dot(ca_ref[...], pick_a), _dot(sa_ref[...], pick_a)
    cb, sb = _dot(cb_ref[...], pick_b), _dot(sb_ref[...], pick_b)
    cos_t = (ca * cb - sa * sb).astype(BF16)
    sin_t = (sa * cb + ca * sb).astype(BF16)
    acc_ref[...] += _dot(cos_t, z_ref[0]) + _dot(sin_t, z_ref[1])

    @pl.when(k == pl.num_programs(1) - 1)
    def _():
        o_ref[...] = acc_ref[...]


def _seq_dft(tables, zc, batch, tm, tk):
    ca, sa, cb, sb = tables
    l = ca.shape[0]
    fw = FOURIER_WIDTH
    small = lambda a: pl.BlockSpec((tm, a.shape[1]), lambda i, k: (i, 0))
    return pl.pallas_call(
        _seq_dft_kernel,
        grid=(l // tm, l // tk),
        in_specs=[small(ca), small(sa), small(cb), small(sb),
                  pl.BlockSpec((2, tk, batch * fw), lambda i, k: (0, k, 0))],
        out_specs=pl.BlockSpec((tm, batch * fw), lambda i, k: (i, 0)),
        out_shape=jax.ShapeDtypeStruct((l, batch * fw), F32),
        scratch_shapes=[pltpu.VMEM((tm, batch * fw), F32)],
        compiler_params=_cparams(("parallel", "arbitrary")),
        name="fourier_sequence",
    )(ca, sa, cb, sb, zc)


def _dft_tables(l):
    m = jnp.arange(l, dtype=jnp.int32)[:, None]
    n1 = l // DFT_SPLIT
    a = (m * (jnp.arange(n1, dtype=jnp.int32)[None, :] * DFT_SPLIT)) % l
    b = (m * jnp.arange(DFT_SPLIT, dtype=jnp.int32)[None, :]) % l
    wa = a.astype(F32) * (2.0 * np.pi / l)
    wb = b.astype(F32) * (2.0 * np.pi / l)
    tables = (jnp.cos(wa), jnp.sin(wa), jnp.cos(wb), jnp.sin(wb))
    gd = FOURIER_GROUP_DIM
    cc = (jnp.arange(gd, dtype=jnp.int32)[:, None] * jnp.arange(gd, dtype=jnp.int32)[None, :]) % gd
    wc = cc.astype(F32) * (2.0 * np.pi / gd)
    scale = 1.0 / np.sqrt(float(l) * gd)
    eye = jnp.eye(FOURIER_WIDTH // gd, dtype=F32)
    chan = jnp.concatenate([jnp.kron(eye, jnp.cos(wc)), -jnp.kron(eye, jnp.sin(wc))], axis=1) * scale
    return tables, chan.astype(BF16)


def _rwkv_prepare(d, pos, z, z_before, z_after, par, nc_ctx, nc_tot):
    mu, kks, ka, rk, w0, w2, a0, a2, bd = par
    c = z.shape[0]
    n = RWKV_DIM
    seg_first = jnp.logical_or(pos == 0, pos == nc_ctx)
    seg_last = jnp.logical_or(pos == nc_ctx - 1, pos == nc_tot - 1)
    row = lax.broadcasted_iota(jnp.int32, z.shape, 0)
    prev_row = jnp.where(seg_first, 0.0, z_before[7:8, :])
    next_row = jnp.where(seg_last, 0.0, z_after[0:1, :])
    z_prev = jnp.where(row == 0, prev_row, pltpu.roll(z, 1, 0))
    z_next = jnp.where(row == c - 1, next_row, pltpu.roll(z, c - 1, 0))
    zs = z + mu * (0.5 * (z_prev + z_next) - z)

    r, k, v = zs[:, 0:n], zs[:, n:2 * n], zs[:, 2 * n:3 * n]
    zw = zs[:, 3 * n:3 * n + RWKV_RANK_PAD]
    za = zs[:, 3 * n + RWKV_RANK_PAD:3 * n + 2 * RWKV_RANK_PAD]
    zg = zs[:, 3 * n + 2 * RWKV_RANK_PAD:3 * n + 3 * RWKV_RANK_PAD]

    kk = k * kks
    kk = kk * lax.rsqrt(_dot(kk * kk, bd) + L2_EPS)
    w_log = _log_sigmoid(w0[d] + _dot(jnp.tanh(zw), w2[d])) - 0.5
    lw = -jnp.exp(w_log)
    a = jax.nn.sigmoid(a0[d] + _dot(za, a2[d]))
    kd = k * (1.0 + (a - 1.0) * ka)
    beta = kk * a

    cl = _dot((_visit_order(d, c) >= 0).astype(F32), lw)
    c_tot = jnp.sum(lw, axis=0, keepdims=True)
    grow = jnp.exp(-cl)
    tail = jnp.exp(c_tot - cl)
    ops = dict(k_s=kd * grow,
               b_s=beta * grow,
               kap_s=kk * jnp.exp(cl - lw),
               r_s=r * jnp.exp(cl),
               k_e=kd * tail,
               b_e=beta * tail,
               gam=jnp.exp(c_tot), v=v,
               bonus=_dot(r * kd * rk, bd),
               sign=1 if d == 0 else -1)
    return ops, zg


def _rwkv_kernel(zf_ref, zfp_ref, zfn_ref, zb_ref, zbp_ref, zbn_ref, mu_ref, kks_ref, ka_ref, rk_ref, w0_ref, w2_ref,
                 a0_ref, a2_ref, g2_ref, bd_ref, yf_ref, yb_ref, gate_ref, st_ref, *, nc_ctx, nc_tot):
    s = pl.program_id(1)
    nb, c = zf_ref.shape[0], zf_ref.shape[1]
    hd = HEAD_DIM
    pw = 2 * hd

    @pl.when(s == 0)
    def _():
        st_ref[...] = jnp.zeros_like(st_ref)

    par = (mu_ref[...], kks_ref[...], ka_ref[...], rk_ref[...], w0_ref, w2_ref, a0_ref, a2_ref, bd_ref[...])
    ctxs = []
    for b in range(nb):
        fwd, zg = _rwkv_prepare(0, _chunk_pos(s, 0, nc_ctx, nc_tot), zf_ref[b], zfp_ref[b], zfn_ref[b], par, nc_ctx, nc_tot)
        bwd, _ = _rwkv_prepare(1, _chunk_pos(s, 1, nc_ctx, nc_tot), zb_ref[b], zbp_ref[b], zbn_ref[b], par, nc_ctx, nc_tot)
        gate_ref[b] = _dot(jax.nn.sigmoid(zg), g2_ref[...])
        ctxs += [fwd, bwd]

    lane = lax.broadcasted_iota(jnp.int32, (c, pw), 1)
    rowi = lax.broadcasted_iota(jnp.int32, (c, pw), 0)
    left = lane < hd
    eye_p = (rowi == lane % hd).astype(F32)
    same_head = (lax.broadcasted_iota(jnp.int32, (pw, pw), 0) < hd) == (lax.broadcasted_iota(jnp.int32, (pw, pw), 1) < hd)

    def bd(y):
        return jnp.concatenate([jnp.where(left, y, 0.0), jnp.where(left, 0.0, y)], axis=0)

    n_pairs = RWKV_HEADS // 2
    prob = [(o, slice(i * pw, (i + 1) * pw)) for o in ctxs for i in range(n_pairs)]
    ahead = [(rowi - lane % hd) * o['sign'] for o, _ in prob]
    get = lambda name: [o[name][:, s_] for o, s_ in prob]
    kap, r_s, k_s, b_s, k_e, b_e, vp, gam, bonus = (get(x) for x in ('kap_s', 'r_s', 'k_s', 'b_s', 'k_e', 'b_e', 'v', 'gam', 'bonus'))
    p = [_dot_nt(jnp.concatenate([a_, b_], axis=0), jnp.concatenate([bd(c_), bd(d_)], axis=0))
         for a_, b_, c_, d_ in zip(kap, r_s, k_s, b_s)]
    m1 = [jnp.where(h_ > 0, x[0:c, 0:pw], 0.0) for x, h_ in zip(p, ahead)]
    m2 = [jnp.where(h_ > 0, x[0:c, pw:2 * pw], 0.0) for x, h_ in zip(p, ahead)]
    n1 = [jnp.where(h_ >= 0, x[c:2 * c, 0:pw], 0.0) for x, h_ in zip(p, ahead)]
    n2 = [jnp.where(h_ >= 0, x[c:2 * c, pw:2 * pw], 0.0) for x, h_ in zip(p, ahead)]
    m1v = [_dot(a_, bd(b_)) for a_, b_ in zip(m1, vp)]
    t_inv = [eye_p - x for x in m2]
    q = [_dot(x, bd(x)) for x in m2]
    span = 2
    while 2 * span < c:
        both = [_dot(jnp.concatenate([t_, q_], axis=0), bd(q_)) for t_, q_ in zip(t_inv, q)]
        t_inv = [t_ + x[0:c] for t_, x in zip(t_inv, both)]
        q = [x[c:2 * c] for x in both]
        span *= 2
    t_inv = [t_ + _dot(t_, bd(q_)) for t_, q_ in zip(t_inv, q)]
    tx = [_dot(t_, jnp.concatenate([bd(a_), bd(mv)], axis=1)) for t_, a_, mv in zip(t_inv, kap, m1v)]
    st = [st_ref[i] for i in range(len(prob))]
    su = [_dot_nt(jnp.concatenate([x[:, 0:pw], r_], axis=0), s0) for x, r_, s0 in zip(tx, r_s, st)]
    u = [x[0:c] + y_[:, pw:2 * pw] for x, y_ in zip(su, tx)]
    ys = [x[c:2 * c] + _dot(jnp.concatenate([a_, -b_], axis=1), jnp.concatenate([bd(v_), bd(u_)], axis=0)) + bo * v_
          for x, a_, b_, v_, u_, bo in zip(su, n1, n2, vp, u, bonus)]
    for i in range(len(prob)):
        upd = _dot_tn(jnp.concatenate([vp[i], u[i]], axis=0), jnp.concatenate([k_e[i], -b_e[i]], axis=0))
        st_ref[i] = st[i] * gam[i] + jnp.where(same_head, upd, 0.0)
    for b in range(nb):
        yf_ref[b] = jnp.concatenate(ys[2 * b * n_pairs:(2 * b + 1) * n_pairs], axis=1)
        yb_ref[b] = jnp.concatenate(ys[(2 * b + 1) * n_pairs:(2 * b + 2) * n_pairs], axis=1)


def _rwkv(zr, mu, kk_scale, k_a, r_k, w0, w2_pad, a0, a2_pad, g2, batch, nc_ctx, nc_tot):
    t, zw_ = zr.shape
    c = SEQ_CHUNK
    n = RWKV_DIM
    bd = _block_ones(RWKV_HEADS, HEAD_DIM)
    full = lambda a: pl.BlockSpec(a.shape, lambda b, s: (0,) * a.ndim)
    sub = c // SUBLANES
    l_tot = t // batch
    n_sub = l_tot // SUBLANES
    nb = math.gcd(batch, RWKV_SCAN_BATCH)
    z3 =zr.reshape(batch, l_tot, zw_)

    def z_specs(d):
        pos = lambda s: _chunk_pos(s, d, nc_ctx, nc_tot)
        return [pl.BlockSpec((nb, c, zw_), lambda b, s: (b, pos(s), 0)),
                pl.BlockSpec((nb, SUBLANES, zw_), lambda b, s: (b, jnp.maximum(pos(s) * sub - 1, 0), 0)),
                pl.BlockSpec((nb, SUBLANES, zw_), lambda b, s: (b, jnp.minimum((pos(s) + 1) * sub, n_sub - 1), 0))]

    vec = lambda a: a.reshape(1, -1)
    args = (vec(mu), vec(kk_scale), vec(k_a), vec(r_k), w0.reshape(2, 1, n), w2_pad, a0.reshape(2, 1, n), a2_pad, g2, bd)
    out = lambda d: pl.BlockSpec((nb, c, n), lambda b, s: (b, _chunk_pos(s, d, nc_ctx, nc_tot), 0))
    outs = pl.pallas_call(
        functools.partial(_rwkv_kernel, nc_ctx=nc_ctx, nc_tot=nc_tot),
        grid=(batch // nb, nc_tot),
        in_specs=z_specs(0) + z_specs(1) + [full(a) for a in args],
        out_specs=[out(0), out(1), out(0)],
        out_shape=[jax.ShapeDtypeStruct((batch, l_tot, n), F32)] * 3,
        scratch_shapes=[pltpu.VMEM((nb * RWKV_HEADS, 2 * HEAD_DIM, 2 * HEAD_DIM), F32)],
        compiler_params=_cparams(("parallel", "arbitrary")),
        name="rwkv_scan",
    )(z3, z3, z3, z3, z3, z3, *args)
    return [o.reshape(t, n) for o in outs]


def _merge_odd_kernel(y0_ref, y1_ref, gate_ref, fo_ref, x_ref, mod_ref, lg_ref, lb_ref, bd_ref, w1_ref, w2_ref,
                      fg_ref, rw_ref, rb_ref, xo_ref, ho_ref, go_ref):
    y = y0_ref[...] + y1_ref[...]
    bd = bd_ref[...]
    mean = _dot(y, bd) * (1.0 / HEAD_DIM)
    yc = y - mean
    var = _dot(yc * yc, bd) * (1.0 / HEAD_DIM)
    rw = (yc * lax.rsqrt(var + RWKV_GN_EPS) * lg_ref[...] + lb_ref[...]) * gate_ref[...]
    out = _dot(fo_ref[...].astype(BF16), w1_ref[...]) + _dot(rw.astype(BF16), w2_ref[...])
    _residual_and_route(x_ref[...], out, mod_ref[0], fg_ref[...], rw_ref, rb_ref, xo_ref, ho_ref, go_ref)


def _merge_odd(y0, y1, gate, fo, x, mods, ln_g, ln_b, w_out, ffn_gain, router_wt, router_b, tm, batch, l_ctx, l_lat):
    d = x.shape[1]
    n = RWKV_DIM
    fw = FOURIER_WIDTH
    l_tot = l_ctx + l_lat
    nt = l_lat // tm
    t_out = batch * l_lat
    full = lambda a: pl.BlockSpec(a.shape, lambda b, i: (0,) * a.ndim)
    src = lambda b, i: b * (l_tot // tm) + l_ctx // tm + i
    bd = _block_ones(RWKV_HEADS, HEAD_DIM)
    w1, w2 = w_out[:fw].astype(BF16), w_out[fw:].astype(BF16)
    lg, lb, fg, rb = ln_g.reshape(1, n), ln_b.reshape(1, n), ffn_gain.reshape(1, d), router_b.reshape(N_EXPERTS, 1)
    return pl.pallas_call(
        _merge_odd_kernel,
        grid=(batch, nt),
        in_specs=[pl.BlockSpec((tm, n), lambda b, i: (src(b, i), 0)),
                  pl.BlockSpec((tm, n), lambda b, i: (src(b, i), 0)),
                  pl.BlockSpec((tm, n), lambda b, i: (src(b, i), 0)),
                  pl.BlockSpec((tm, fw), lambda b, i: (i, b)),
                  pl.BlockSpec((tm, d), lambda b, i: (src(b, i), 0)),
                  pl.BlockSpec((1, 6, d), lambda b, i: (b, 0, 0)),
                  full(lg), full(lb), full(bd), full(w1), full(w2), full(fg), full(router_wt), full(rb)],
        out_specs=[pl.BlockSpec((tm, d), lambda b, i: (b * nt + i, 0)),
                   pl.BlockSpec((tm * (d // LANES), LANES), lambda b, i: (b * nt + i, 0)),
                   pl.BlockSpec((8, tm), lambda b, i: (0, b * nt + i))],
        out_shape=[jax.ShapeDtypeStruct((t_out, d), F32),
                   jax.ShapeDtypeStruct((t_out * (d // LANES), LANES), F32),
                   jax.ShapeDtypeStruct((8, t_out), F32)],
        compiler_params=_cparams(("parallel", "parallel")),
        name="merge_odd",
    )(y0, y1, gate, fo, x, mods, lg, lb, bd, w1, w2, fg, router_wt, rb)


def _rope_tables(l_ctx, l_lat):
    rows = l_lat // GRID_W
    row = jnp.repeat(jnp.arange(rows, dtype=F32), GRID_W)
    col = jnp.tile(jnp.arange(GRID_W, dtype=F32), rows)
    n_freq = HEAD_DIM // 4
    inv_freq = ROPE_THETA ** (-jnp.arange(n_freq, dtype=F32) / n_freq)
    ang = jnp.concatenate([row[:, None] * inv_freq, col[:, None] * inv_freq], axis=-1)
    cos, sin = jnp.cos(ang), jnp.sin(ang)
    cos64 = jnp.concatenate([cos, cos], axis=1)
    sin64 = jnp.concatenate([-sin, sin], axis=1)
    cos64 = jnp.concatenate([jnp.ones((l_ctx, HEAD_DIM), F32), cos64], axis=0)
    sin64 = jnp.concatenate([jnp.zeros((l_ctx, HEAD_DIM), F32), sin64], axis=0)
    return cos64, sin64


def _pad_rank(w):
    _, r, n = w.shape
    out = jnp.zeros((2, RWKV_RANK_PAD, n), w.dtype)
    out = out.at[0, 0:r].set(w[0])
    return out.at[1, r:2 * r].set(w[1])


def _even_layer(ctx, x_lat, mods, p, batch, l_ctx, l_lat, tm, mod_row):
    l_tot = l_ctx + l_lat
    nc_ctx, nc_tot = l_ctx // SEQ_CHUNK, l_tot // SEQ_CHUNK
    hk, hv = GLA_HEADS * GLA_DK, GLA_HEADS * GLA_DV
    qw, kw = ATT_HEADS * HEAD_DIM, ATT_KV_HEADS * HEAD_DIM
    w_in = p['w_in']
    o = np.cumsum([0, hk, hk, hv, hv, 2 * GLA_LOWRANK, qw, kw, kw])
    w_gla = jnp.concatenate([w_in[:, o[0]:o[4]]], axis=1).astype(BF16)
    w_dec = jnp.pad(w_in[:, o[4]:o[5]], ((0, 0), (0, 128 - 2 * GLA_LOWRANK))).astype(BF16)
    w_q = w_in[:, o[5]:o[6]].astype(BF16)
    w_kv = w_in[:, o[6]:o[8]].astype(BF16)
    cos64, sin64 = _rope_tables(l_ctx, l_lat)
    x, zg, zdec, qn, kn, vn = _project_even(ctx, x_lat, mods, p['norm_mix'], [w_gla, w_dec, w_q, w_kv], cos64, sin64,
                                            p['q_norm'], p['k_norm'], tm, mod_row, batch, l_ctx, l_lat)

    dec_w_pad = _pad_rank(p['dec_w'])
    o_f, o_b = _gla(zg, zdec, dec_w_pad, p['dec_b'].reshape(2, 1, hk), batch, nc_ctx, nc_tot)
    o_att = _attention(qn, kn, vn, p['sink'], batch, l_ctx, l_tot)

    x1, h, gates = _merge_even(o_f, o_b, zg, o_att, x, mods, p['out_norm'], p['w_out'], p['norm_ffn'],
                               p['router_wt'], p['router_b'], tm, mod_row)
    pairs = _moe(h, gates, x1, mods, p['moe_g'], p['moe_u'], p['moe_d'], p['moe_layer'], tm, mod_row, tm, combine=False)
    return pairs, x1


def _odd_layer(pairs, x_prev, mods_prev, mods, p, batch, l_ctx, l_lat, tm, mod_row, tm_moe):
    l_tot = l_ctx + l_lat
    nc_ctx, nc_tot = l_ctx // SEQ_CHUNK, l_tot // SEQ_CHUNK
    n = RWKV_DIM
    fw = FOURIER_WIDTH
    w_in = p['w_in']
    rank_w, rank_a = p['w2'].shape[1], p['a2'].shape[1]
    o = np.cumsum([0, fw, n, n, n, 2 * rank_w, 2 * rank_a])
    pad_cols = lambda w: jnp.pad(w, ((0, 0), (0, RWKV_RANK_PAD - w.shape[1])))
    w_f = w_in[:, o[0]:o[1]].astype(BF16)
    w_r = jnp.concatenate([w_in[:, o[1]:o[4]], pad_cols(w_in[:, o[4]:o[5]]), pad_cols(w_in[:, o[5]:o[6]]),
                           w_in[:, o[6]:]], axis=1).astype(BF16)
    x, (zf, zr) = _project(pairs, x_prev, mods_prev, mods, p['norm_mix'], [w_f, w_r], tm, mod_row)

    mu = p['mu']
    mu_r = jnp.concatenate([mu[0:3 * n], pad_cols(mu[None, 3 * n:3 * n + 2 * rank_w])[0],
                            pad_cols(mu[None, 3 * n + 2 * rank_w:3 * n + 2 * rank_w + 2 * rank_a])[0],
                            mu[3 * n + 2 * rank_w + 2 * rank_a:]])
    y0, y1, gate = _rwkv(zr, mu_r, p['kk_scale'], p['k_a'], p['r_k'].reshape(-1), p['w0'], _pad_rank(p['w2']),
                    p['a0'], _pad_rank(p['a2']), p['g2'], batch, nc_ctx, nc_tot)

    tables, chan = _dft_tables(l_lat)
    zc = _chan_dft(zf, chan, batch, l_ctx, l_lat, tm)
    fo = _seq_dft(tables, zc, batch, min(512, l_lat), min(1024, l_lat))

    x1, h, gates = _merge_odd(y0, y1, gate, fo, x, mods, p['ln_g'], p['ln_b'], p['w_out'], p['norm_ffn'],
                              p['router_wt'], p['router_b'], tm, batch, l_ctx, l_lat)
    lat_tiles = l_lat // tm_moe
    return _moe(h, gates, x1, mods, p['moe_g'], p['moe_u'], p['moe_d'], p['moe_layer'], tm_moe,
                lambda i: i // lat_tiles, tm)


def kernel(x, c, ctx, c_ctx, ada_w, ada_b, norm_mix, norm_ffn, even_w_in, even_w_out, gla_dec_w, gla_dec_b, gla_out_norm, att_q_norm, att_k_norm, att_sink, odd_w_in, odd_w_out, rwkv_mu, rwkv_w0, rwkv_w2, rwkv_a0, rwkv_a2, rwkv_g2, rwkv_kk_scale, rwkv_k_a, rwkv_r_k, rwkv_ln_g, rwkv_ln_b, router_w, router_b, moe_w_gate, moe_w_up, moe_w_down):
    batch, l_lat, d = x.shape
    l_ctx = ctx.shape[1]
    l_tot = l_ctx + l_lat
    assert batch < 8 and ada_w.shape[0] == 2
    tm = 256 if (l_ctx % 256 == 0 and l_lat % 256 == 0) else 128
    tm_moe = 512 if (l_lat % 512 == 0 and tm == 256) else tm
    assert l_ctx % tm == 0 and l_lat % tm == 0 and l_tot % l_ctx == 0 and l_lat % GRID_W == 0

    cc = jnp.concatenate([c, c_ctx[None, :], jnp.zeros((8 - batch - 1, d), F32)], axis=0)
    tiles_per_b = l_tot // tm
    ctx_tiles = l_ctx // tm

    def mod_row(i):
        return jnp.where(i % tiles_per_b < ctx_tiles, batch, i // tiles_per_b)

    rw_pad = jnp.pad(router_w, ((0, 0), (0, LANES - N_EXPERTS)))
    rw_hi = rw_pad.astype(BF16)
    router_wt = jnp.stack([rw_hi, (rw_pad - rw_hi.astype(F32)).astype(BF16)])
    moe = lambda layer: dict(moe_g=moe_w_gate, moe_u=moe_w_up, moe_d=moe_w_down, moe_layer=layer)

    mods_all = _modvec(cc, ada_w, ada_b)
    mods0 = mods_all[0]
    p0 = dict(w_in=even_w_in[0], w_out=even_w_out[0], dec_w=gla_dec_w[0], dec_b=gla_dec_b[0],
              out_norm=gla_out_norm[0], q_norm=att_q_norm[0], k_norm=att_k_norm[0], sink=att_sink[0],
              norm_mix=norm_mix[0], norm_ffn=norm_ffn[0], router_wt=router_wt, router_b=router_b, **moe(0))
    pairs, x1 = _even_layer(ctx.reshape(batch * l_ctx, d), x.reshape(batch * l_lat, d), mods0, p0, batch, l_ctx, l_lat,
                            tm, mod_row)

    mods1 = mods_all[1]
    p1 = dict(w_in=odd_w_in[0], w_out=odd_w_out[0], mu=rwkv_mu[0], w0=rwkv_w0[0], w2=rwkv_w2[0], a0=rwkv_a0[0],
              a2=rwkv_a2[0], g2=rwkv_g2[0], kk_scale=rwkv_kk_scale[0], k_a=rwkv_k_a[0], r_k=rwkv_r_k[0],
              ln_g=rwkv_ln_g[0], ln_b=rwkv_ln_b[0], norm_mix=norm_mix[1], norm_ffn=norm_ffn[1],
              router_wt=router_wt, router_b=router_b, **moe(1))
    out = _odd_layer(pairs, x1, mods0, mods1, p1, batch, l_ctx, l_lat, tm, mod_row, tm_moe)
    return out.reshape(batch, l_lat, d)
```

```python
import functools
import math

import jax
import jax.numpy as jnp
import numpy as np
from jax import lax
from jax.experimental import pallas as pl
from jax.experimental.pallas import tpu as pltpu

F32 = jnp.float32
BF16 = jnp.bfloat16

GRID_W = 64
HEAD_DIM = 64
NORM_EPS = 1e-6
L2_EPS = 1e-12

GLA_DV = 64
GLA_DK = 32
GLA_HEADS = 8
GLA_LOWRANK = 16
GLA_TAU = 16.0

ATT_HEADS = 8
ATT_KV_HEADS = 2
ATT_GROUP = ATT_HEADS // ATT_KV_HEADS
ATT_BLOCK = 128
ROPE_THETA = 10000.0

FOURIER_GROUP_DIM = 64
FOURIER_WIDTH = 256
DFT_SPLIT = 64

RWKV_DIM = 768
RWKV_HEADS = 12
RWKV_RANK_PAD = 128
RWKV_GN_EPS = 64e-5

N_EXPERTS = 16
N_GROUPS = 4
PER_GROUP = N_EXPERTS // N_GROUPS
D_EXPERT = 512
MOE_TILE = 256
MOE_GATHER_DEPTH = 3
LANES = 128
SUBLANES = 8

SEQ_CHUNK = 64
GLA_SCAN_BATCH = 4
RWKV_SCAN_BATCH = 2
VMEM_LIMIT = 56 * 1024 * 1024


def _cparams(sem):
    return pltpu.CompilerParams(dimension_semantics=sem, vmem_limit_bytes=VMEM_LIMIT)


def _dot(a, b):
    return jnp.dot(a, b, preferred_element_type=F32)


def _dot_nt(a, b):
    return lax.dot_general(a, b, (((1,), (1,)), ((), ())), preferred_element_type=F32)


def _dot_tn(a, b):
    return lax.dot_general(a, b, (((0,), (0,)), ((), ())), preferred_element_type=F32)


def _silu(x):
    return x * jax.nn.sigmoid(x)


def _log_sigmoid(x):
    return jnp.minimum(x, 0.0) - jnp.log(1.0 + jnp.exp(-jnp.abs(x)))


def _modulated_norm(x, gain, shift, scale):
    ms = jnp.mean(x * x, axis=-1, keepdims=True)
    return (x * lax.rsqrt(ms + NORM_EPS) * gain) * (1.0 + scale) + shift


def _block_ones(n_blocks, width):
    return jnp.kron(jnp.eye(n_blocks, dtype=F32), jnp.ones((width, width), F32))


def _store_row_tiles(ref, x, first_row=0):
    n, w = x.shape
    g = w // LANES
    for k in range(g):
        ref[pl.ds(first_row * g + k, n, stride=g), :] = x[:, k * LANES:(k + 1) * LANES]


def _load_row_tiles(ref, n, g, first_row=0):
    return jnp.concatenate([ref[pl.ds(first_row * g + k, n, stride=g), :] for k in range(g)], axis=1)


def _modvec_kernel(c_ref, w_ref, b_ref, o_ref):
    o_ref[...] = _dot(_silu(c_ref[...]), w_ref[...]) + b_ref[...]


def _modvec(cc, w, b):
    d = cc.shape[1]
    layers, _, n = w.shape
    tn = n // 4
    out = pl.pallas_call(
        _modvec_kernel,
        grid=(layers, n // tn),
        in_specs=[pl.BlockSpec((8, d), lambda l, j: (0, 0)),
                  pl.BlockSpec((None, d, tn), lambda l, j: (l, 0, j)),
                  pl.BlockSpec((None, 1, tn), lambda l, j: (l, 0, j))],
        out_specs=pl.BlockSpec((None, 8, tn), lambda l, j: (l, 0, j)),
        out_shape=jax.ShapeDtypeStruct((layers, 8, n), F32),
        compiler_params=_cparams(("parallel", "parallel")),
        name="modvec",
    )(cc, w, b.reshape(layers, 1, n))
    return out.reshape(layers, 8, 6, d)


def _proj_kernel(ya_ref, yb_ref, x_ref, mp0_ref, mp1_ref, m0_ref, m1_ref, gain_ref, *refs, n_out):
    w_refs, x_out_ref, z_refs = refs[:n_out], refs[n_out], refs[n_out + 1:]
    n, d = x_ref.shape[0] // 2, x_ref.shape[1]
    g = d // LANES
    for half, (mp_ref, m_ref) in enumerate(((mp0_ref, m0_ref), (mp1_ref, m1_ref))):
        rows = pl.ds(half * n, n)
        y = _load_row_tiles(ya_ref, n, g, first_row=half * n) + _load_row_tiles(yb_ref, n, g, first_row=half * n)
        x = x_ref[rows, :] + mp_ref[0][5:6] * y
        x_out_ref[rows, :] = x
        m = m_ref[0]
        h = _modulated_norm(x, gain_ref[...], m[0:1], m[1:2]).astype(BF16)
        for w_ref, z_ref in zip(w_refs, z_refs):
            z_ref[rows, :] = _dot(h, w_ref[...])


def _project(pairs, x, mods_prev, mods, gain, weights, tm, mod_row):
    t, d = x.shape
    g = d // LANES
    n_out = len(weights)
    assert (t // tm) % 2 == 0
    tb = 2 * tm
    mod_spec = lambda half: pl.BlockSpec((1, 6, d), lambda i: (mod_row(2 * i + half), 0, 0))
    in_specs = [pl.BlockSpec((tb * g, LANES), lambda i: (i, 0)),
                pl.BlockSpec((tb * g, LANES), lambda i: (t // tb + i, 0)),
                pl.BlockSpec((tb, d), lambda i: (i, 0)),
                mod_spec(0), mod_spec(1), mod_spec(0), mod_spec(1),
                pl.BlockSpec((1, d), lambda i: (0, 0))]
    in_specs += [pl.BlockSpec(w.shape, lambda i: (0, 0)) for w in weights]
    outs = pl.pallas_call(
        functools.partial(_proj_kernel, n_out=n_out),
        grid=(t // tb,),
        in_specs=in_specs,
        out_specs=[pl.BlockSpec((tb, d), lambda i: (i, 0))] + [pl.BlockSpec((tb, w.shape[1]), lambda i: (i, 0)) for w in weights],
        out_shape=[jax.ShapeDtypeStruct((t, d), F32)] + [jax.ShapeDtypeStruct((t, w.shape[1]), F32) for w in weights],
        compiler_params=_cparams(("parallel",)),
        name="proj",
    )(pairs, pairs, x, mods_prev, mods_prev, mods, mods, gain.reshape(1, d), *weights)
    return outs[0], outs[1:]


def _chunk_pos(s, d, nc_ctx, nc_tot):
    back = jnp.where(s < nc_ctx, nc_ctx - 1 - s, nc_tot + nc_ctx - 1 - s)
    return jnp.where(d == 0, s, back)


def _visit_order(d, c):
    sign = 1 if d == 0 else -1
    return (lax.broadcasted_iota(jnp.int32, (c, c), 0) - lax.broadcasted_iota(jnp.int32, (c, c), 1)) * sign


def _gla_prepare(d, q, k, v, dec, dw_ref, db_ref):
    c = q.shape[0]
    g = _log_sigmoid(_dot(dec, dw_ref[d]) + db_ref[d]) / GLA_TAU
    b = _dot((_visit_order(d, c) >= 0).astype(F32), g)
    b_tot = jnp.sum(g, axis=0, keepdims=True)
    return dict(q_in=q * (GLA_DK ** -0.5) * jnp.exp(b), k_out=k * jnp.exp(-b), k_end=k * jnp.exp(b_tot - b),
                decay=jnp.exp(b_tot), v=v, sign=1 if d == 0 else -1)


def _gla_kernel(qf_ref, kf_ref, vf_ref, df_ref, qb_ref, kb_ref, vb_ref, db_ref, dw_ref, dbias_ref, of_ref, ob_ref, st_ref):
    nb, c = qf_ref.shape[0], qf_ref.shape[1]
    group = 4
    kw, vw = group * GLA_DK, group * GLA_DV

    @pl.when(pl.program_id(1) == 0)
    def _():
        st_ref[...] = jnp.zeros_like(st_ref)

    ctxs = []
    for u in range(nb):
        ctxs.append(_gla_prepare(0, qf_ref[u], kf_ref[u], vf_ref[u], df_ref[u], dw_ref, dbias_ref))
        ctxs.append(_gla_prepare(1, qb_ref[u], kb_ref[u], vb_ref[u], db_ref[u], dw_ref, dbias_ref))
    n_quads = GLA_HEADS // group
    prob = [(o, i) for o in ctxs for i in range(n_quads)]
    klane = lax.broadcasted_iota(jnp.int32, (c, kw), 1) // GLA_DK
    half = lax.broadcasted_iota(jnp.int32, (c, 2 * GLA_DV), 1) < GLA_DV
    rowi = lax.broadcasted_iota(jnp.int32, (c, group * c), 0)
    coli = lax.broadcasted_iota(jnp.int32, (c, group * c), 1) % c
    own = (lax.broadcasted_iota(jnp.int32, (vw, kw), 0) // GLA_DV) == (lax.broadcasted_iota(jnp.int32, (vw, kw), 1) // GLA_DK)

    def bd_keys(y):
        return jnp.concatenate([jnp.where(klane == h, y, 0.0) for h in range(group)], axis=0)

    def bd_vals(y):
        return jnp.concatenate([jnp.where(half, y, 0.0), jnp.where(half, 0.0, y)], axis=0)

    ksl = lambda i: slice(i * kw, (i + 1) * kw)
    vsl = lambda i: slice(i * vw, (i + 1) * vw)
    q_in = [o['q_in'][:, ksl(i)] for o, i in prob]
    att = [jnp.where((rowi - coli) * o['sign'] >= 0, _dot_nt(q_, bd_keys(o['k_out'][:, ksl(i)])), 0.0)
           for q_, (o, i) in zip(q_in, prob)]
    st = [st_ref[j] for j in range(len(prob))]
    outs = []
    for j, (o, i) in enumerate(prob):
        v = o['v'][:, vsl(i)]
        intra = jnp.concatenate([_dot(att[j][:, p * 2 * c:(p + 1) * 2 * c], bd_vals(v[:, p * 2 * GLA_DV:(p + 1) * 2 * GLA_DV]))
                                 for p in range(group // 2)], axis=1)
        outs.append(intra + _dot_nt(q_in[j], st[j]))
    for j, (o, i) in enumerate(prob):
        upd = _dot_tn(o['v'][:, vsl(i)], o['k_end'][:, ksl(i)])
        st_ref[j] = st[j] * o['decay'][:, ksl(i)] + jnp.where(own, upd, 0.0)
    for u in range(nb):
        of_ref[u] = jnp.concatenate(outs[2 * u * n_quads:(2 * u + 1) * n_quads], axis=1)
        ob_ref[u] = jnp.concatenate(outs[(2 * u + 1) * n_quads:(2 * u + 2) * n_quads], axis=1)


def _gla(zg, zdec, dec_w_pad, dec_b, batch, nc_ctx, nc_tot):
    t = zg.shape[0]
    c = SEQ_CHUNK
    hk, hv = GLA_HEADS * GLA_DK, GLA_HEADS * GLA_DV
    nb = math.gcd(batch, GLA_SCAN_BATCH)
    zg3, zdec3 = zg.reshape(batch, t // batch, -1), zdec.reshape(batch, t // batch, -1)

    def specs(d):
        pos = lambda s: _chunk_pos(s, d, nc_ctx, nc_tot)
        return [pl.BlockSpec((nb, c, hk), lambda b, s: (b, pos(s), 0)),
                pl.BlockSpec((nb, c, hk), lambda b, s: (b, pos(s), 1)),
                pl.BlockSpec((nb, c, hv), lambda b, s: (b, pos(s), 1)),
                pl.BlockSpec((nb, c, 128), lambda b, s: (b, pos(s), 0))], pl.BlockSpec((nb, c, hv), lambda b, s: (b, pos(s), 0))

    in_f, out_f = specs(0)
    in_b, out_b = specs(1)
    o_f, o_b = pl.pallas_call(
        _gla_kernel,
        grid=(batch // nb, nc_tot),
        in_specs=in_f + in_b + [pl.BlockSpec(dec_w_pad.shape, lambda b, s: (0, 0, 0)),
                                pl.BlockSpec(dec_b.shape, lambda b, s: (0, 0, 0))],
        out_specs=[out_f, out_b],
        out_shape=[jax.ShapeDtypeStruct((batch, t // batch, hv), F32)] * 2,
        scratch_shapes=[pltpu.VMEM((nb * 2 * GLA_HEADS // 4, 4 * GLA_DV, 4 * GLA_DK), F32)],
        compiler_params=_cparams(("parallel", "arbitrary")),
        name="gla_scan",
    )(zg3, zg3, zg3, zdec3, zg3, zg3, zg3, zdec3, dec_w_pad, dec_b)
    return o_f.reshape(t, hv), o_b.reshape(t, hv)


def _rope_swap(x):
    n = x.shape[-1]
    lane = lax.broadcasted_iota(jnp.int32, x.shape, x.ndim - 1)
    half = HEAD_DIM // 2
    return jnp.where(lane % HEAD_DIM < half, pltpu.roll(x, n - half, x.ndim - 1), pltpu.roll(x, half, x.ndim - 1))


def _proj_even_kernel(ctx_ref, x_ref, mod_ref, gain_ref, wg_ref, wd_ref, wq_ref, wkv_ref, cos_ref, sin_ref, qg_ref, kg_ref,
                      bdq_ref, bdk_ref, xs_ref, zg_ref, zdec_ref, qo_ref, ko_ref, vo_ref, *, tiles_per_b, ctx_tiles):
    is_ctx = lax.rem(pl.program_id(0), tiles_per_b) < ctx_tiles
    x = jnp.where(is_ctx, ctx_ref[...], x_ref[...])
    xs_ref[...] = x
    m = mod_ref[0]
    h = _modulated_norm(x, gain_ref[...], m[0:1], m[1:2]).astype(BF16)
    zg_ref[...] = _dot(h, wg_ref[...])
    zdec_ref[...] = _dot(h, wd_ref[...])

    def norm_rope(z, gain, bd, n_heads):
        ms = _dot(z * z, bd) * (1.0 / HEAD_DIM)
        zn = z * lax.rsqrt(ms + NORM_EPS) * gain
        cos = jnp.concatenate([cos_ref[...]] * n_heads, axis=1)
        sin = jnp.concatenate([sin_ref[...]] * n_heads, axis=1)
        return zn * cos + _rope_swap(zn) * sin

    q = norm_rope(_dot(h, wq_ref[...]), qg_ref[...], bdq_ref[...], ATT_HEADS)
    qo_ref[...] = (q * (HEAD_DIM ** -0.5)).astype(BF16)
    kw = ATT_KV_HEADS * HEAD_DIM
    kv = _dot(h, wkv_ref[...])
    ko_ref[...] = norm_rope(kv[:, :kw], kg_ref[...], bdk_ref[...], ATT_KV_HEADS).astype(BF16)
    vo_ref[...] = kv[:, kw:].astype(BF16)


def _project_even(ctx, x, mods, gain, weights, cos64, sin64, q_gain, k_gain, tm, mod_row, batch, l_ctx, l_lat):
    d = x.shape[1]
    l_tot = l_ctx + l_lat
    t = batch * l_tot
    tiles_per_b, ctx_tiles, lat_tiles = l_tot // tm, l_ctx // tm, l_lat // tm
    qw, kw = ATT_HEADS * HEAD_DIM, ATT_KV_HEADS * HEAD_DIM
    w_gla, w_dec, w_q, w_kv = weights
    full = lambda a: pl.BlockSpec(a.shape, lambda i: (0,) * a.ndim)
    qg = jnp.tile(q_gain, ATT_HEADS).reshape(1, qw)
    kg = jnp.tile(k_gain, ATT_KV_HEADS).reshape(1, kw)
    bdq, bdk = _block_ones(ATT_HEADS, HEAD_DIM), _block_ones(ATT_KV_HEADS, HEAD_DIM)
    gain = gain.reshape(1, d)
    row = lambda n: pl.BlockSpec((tm, n), lambda i: (i, 0))
    pos = pl.BlockSpec((tm, HEAD_DIM), lambda i: (i % tiles_per_b, 0))
    return pl.pallas_call(
        functools.partial(_proj_even_kernel, tiles_per_b=tiles_per_b, ctx_tiles=ctx_tiles),
        grid=(t // tm,),
        in_specs=[pl.BlockSpec((tm, d), lambda i: ((i // tiles_per_b) * ctx_tiles + jnp.minimum(i % tiles_per_b, ctx_tiles - 1), 0)),
                  pl.BlockSpec((tm, d), lambda i: ((i // tiles_per_b) * lat_tiles + jnp.maximum(i % tiles_per_b - ctx_tiles, 0), 0)),
                  pl.BlockSpec((1, 6, d), lambda i: (mod_row(i), 0, 0)),
                  full(gain), full(w_gla), full(w_dec), full(w_q), full(w_kv), pos, pos,
                  full(qg), full(kg), full(bdq), full(bdk)],
        out_specs=[row(d), row(w_gla.shape[1]), row(w_dec.shape[1]), row(qw), row(kw), row(kw)],
        out_shape=[jax.ShapeDtypeStruct((t, d), F32), jax.ShapeDtypeStruct((t, w_gla.shape[1]), F32),
                   jax.ShapeDtypeStruct((t, w_dec.shape[1]), F32), jax.ShapeDtypeStruct((t, qw), BF16),
                   jax.ShapeDtypeStruct((t, kw), BF16), jax.ShapeDtypeStruct((t, kw), BF16)],
        compiler_params=_cparams(("parallel",)),
        name="proj_even",
    )(ctx, x, mods, gain, w_gla, w_dec, w_q, w_kv, cos64, sin64, qg, kg, bdq, bdk)


def _attn_kernel(q_ref, kp_ref, kc_ref, kn_ref, kx_ref, vp_ref, vc_ref, vn_ref, vx_ref, sink_ref, o_ref,
                 *, n_ctx_blocks, n_lat_blocks):
    blk = ATT_BLOCK
    n = pl.program_id(1)
    m = n - n_ctx_blocks
    is_lat = n >= n_ctx_blocks
    l_ctx = kx_ref.shape[0]
    width = 3 * blk + l_ctx
    rows = ATT_GROUP * blk
    r = lax.broadcasted_iota(jnp.int32, (rows, width), 0) % blk
    c = lax.broadcasted_iota(jnp.int32, (rows, width), 1)
    lat = is_lat.astype(jnp.int32)
    has_prev = lat * (m >= 1).astype(jnp.int32)
    has_next = lat * (m <= n_lat_blocks - 2).astype(jnp.int32)
    valid = jnp.where(c < blk, (c >= r).astype(jnp.int32) * has_prev,
                      jnp.where(c < 2 * blk, lat,
                                jnp.where(c < 3 * blk, (c - 2 * blk <= r).astype(jnp.int32) * has_next, 1))) > 0
    q = q_ref[...]
    sink = sink_ref[...]
    groups = range(ATT_KV_HEADS)
    ks = [slice(kvh * HEAD_DIM, (kvh + 1) * HEAD_DIM) for kvh in groups]
    heads = [range(kvh * ATT_GROUP, (kvh + 1) * ATT_GROUP) for kvh in groups]
    kw = [jnp.concatenate([kp_ref[:, s_], kc_ref[:, s_], kn_ref[:, s_], kx_ref[:, s_]], axis=0) for s_ in ks]
    vw = [jnp.concatenate([vp_ref[:, s_], vc_ref[:, s_], vn_ref[:, s_], vx_ref[:, s_]], axis=0) for s_ in ks]
    qg = [jnp.concatenate([q[:, h * HEAD_DIM:(h + 1) * HEAD_DIM] for h in hs], axis=0) for hs in heads]
    sk = [jnp.concatenate([jnp.broadcast_to(sink[h:h + 1, 0:1], (blk, 1)) for h in hs], axis=0) for hs in heads]
    s = [jnp.where(valid, _dot_nt(a_, b_), -jnp.inf) for a_, b_ in zip(qg, kw)]
    mx = [jnp.maximum(jnp.max(a_, axis=-1, keepdims=True), b_) for a_, b_ in zip(s, sk)]
    p = [jnp.exp(a_ - b_) for a_, b_ in zip(s, mx)]
    denom = [jnp.sum(a_, axis=-1, keepdims=True) + jnp.exp(b_ - c_) for a_, b_, c_ in zip(p, sk, mx)]
    o = [_dot(a_.astype(BF16), b_) / c_ for a_, b_, c_ in zip(p, vw, denom)]
    outs = [o[kvh][g * blk:(g + 1) * blk] for kvh in groups for g in range(ATT_GROUP)]
    o_ref[...] = jnp.concatenate(outs, axis=1)


def _attention(qn, kn, vn, sink, batch, l_ctx, l_tot):
    t = qn.shape[0]
    blk = ATT_BLOCK
    nq = l_tot // blk
    nc = l_ctx // blk
    nl = nq - nc
    qw, kw = ATT_HEADS * HEAD_DIM, ATT_KV_HEADS * HEAD_DIM

    def win(off):
        def index(b, n):
            m = jnp.clip(n - nc + off, 0, nl - 1)
            return (b * nq + nc + m, 0)
        return pl.BlockSpec((blk, kw), index)

    ctx_spec = pl.BlockSpec((l_ctx, kw), lambda b, n: (b * (l_tot // l_ctx), 0))
    return pl.pallas_call(
        functools.partial(_attn_kernel, n_ctx_blocks=nc, n_lat_blocks=nl),
        grid=(batch, nq),
        in_specs=[pl.BlockSpec((blk, qw), lambda b, n: (b * nq + n, 0)),
                  win(-1), win(0), win(1), ctx_spec,
                  win(-1), win(0), win(1), ctx_spec,
                  pl.BlockSpec((ATT_HEADS, 128), lambda b, n: (0, 0))],
        out_specs=pl.BlockSpec((blk, qw), lambda b, n: (b * nq + n, 0)),
        out_shape=jax.ShapeDtypeStruct((t, qw), F32),
        compiler_params=_cparams(("parallel", "parallel")),
        name="window_attention",
    )(qn, kn, kn, kn, kn, vn, vn, vn, vn, jnp.broadcast_to(sink.astype(F32)[:, None], (ATT_HEADS, 128)))


def _route(logits_t, bias_col):
    scores = jax.nn.sigmoid(logits_t)
    sel = scores + bias_col
    rows = [sel[e:e + 1] for e in range(N_EXPERTS)]
    grp = []
    for g in range(N_GROUPS):
        r = rows[g * PER_GROUP:(g + 1) * PER_GROUP]
        best = None
        for i in range(PER_GROUP):
            for j in range(i + 1, PER_GROUP):
                pair = r[i] + r[j]
                best = pair if best is None else jnp.maximum(best, pair)
        grp.append(best)
    g_best = jnp.zeros_like(grp[0], dtype=jnp.int32)
    g_val = grp[0]
    for g in range(1, N_GROUPS):
        take = grp[g] > g_val
        g_best = jnp.where(take, g, g_best)
        g_val = jnp.where(take, grp[g], g_val)
    neg = -jnp.inf
    masked = [jnp.where(g_best == e // PER_GROUP, rows[e], neg) for e in range(N_EXPERTS)]
    i1 = jnp.zeros_like(g_best)
    v1 = masked[0]
    for e in range(1, N_EXPERTS):
        take = masked[e] > v1
        i1 = jnp.where(take, e, i1)
        v1 = jnp.where(take, masked[e], v1)
    i2 = jnp.full_like(g_best, -1)
    v2 = jnp.full_like(v1, neg)
    for e in range(N_EXPERTS):
        take = jnp.logical_and(i1 != e, masked[e] > v2)
        i2 = jnp.where(take, e, i2)
        v2 = jnp.where(take, masked[e], v2)
    w1 = jnp.zeros_like(v1)
    w2 = jnp.zeros_like(v1)
    for e in range(N_EXPERTS):
        w1 = jnp.where(i1 == e, scores[e:e + 1], w1)
        w2 = jnp.where(i2 == e, scores[e:e + 1], w2)
    inv = 1.0 / (w1 + w2)
    pad = jnp.zeros_like(w1)
    lane = lax.broadcasted_iota(jnp.int32, w1.shape, 1)
    hist = pad
    for e in range(N_EXPERTS):
        n_e = jnp.sum((i1 == e).astype(F32) + (i2 == e).astype(F32), axis=1, keepdims=True)
        hist = jnp.where(lane == e, n_e, hist)
    return jnp.concatenate([i1.astype(F32), i2.astype(F32), w1 * inv, w2 * inv, hist, pad, pad, pad], axis=0)


def _residual_and_route(x, out, m, ffn_gain, rw_ref, rb_ref, x_ref, h_ref, g_ref):
    x1 = x + m[2:3] * out
    x_ref[...] = x1
    h = _modulated_norm(x1, ffn_gain, m[3:4], m[4:5])
    _store_row_tiles(h_ref, h)
    h_hi = h.astype(BF16)
    h_lo = (h - h_hi.astype(F32)).astype(BF16)
    logits = _dot(h_hi, rw_ref[0]) + _dot(h_lo, rw_ref[0]) + _dot(h_hi, rw_ref[1])
    g_ref[...] = _route(logits.T[:N_EXPERTS], rb_ref[...])


def _merge_even_kernel(o0_ref, o1_ref, gg_ref, oa_ref, x_ref, mod_ref, gn_ref, bd_ref, w1_ref, w2_ref,
                       fg_ref, rw_ref, rb_ref, xo_ref, ho_ref, go_ref):
    og = o0_ref[...] + o1_ref[...]
    ms = _dot(og * og, bd_ref[...]) * (1.0 / GLA_DV)
    g = og * lax.rsqrt(ms + NORM_EPS) * gn_ref[...] * _silu(gg_ref[...])
    out = _dot(g.astype(BF16), w1_ref[...]) + _dot(oa_ref[...].astype(BF16), w2_ref[...])
    _residual_and_route(x_ref[...], out, mod_ref[0], fg_ref[...], rw_ref, rb_ref, xo_ref, ho_ref, go_ref)


def _merge_even(o_f, o_b, zg, o_att, x, mods, out_norm, w_out, ffn_gain, router_wt, router_b, tm, mod_row):
    t, d = x.shape
    hv = GLA_HEADS * GLA_DV
    qw = ATT_HEADS * HEAD_DIM
    full = lambda a: pl.BlockSpec(a.shape, lambda i: (0,) * a.ndim)
    gn = jnp.tile(out_norm, GLA_HEADS).reshape(1, hv)
    bd = _block_ones(GLA_HEADS, GLA_DV)
    w1, w2 = w_out[:hv].astype(BF16), w_out[hv:].astype(BF16)
    fg = ffn_gain.reshape(1, d)
    rb = router_b.reshape(N_EXPERTS, 1)
    return pl.pallas_call(
        _merge_even_kernel,
        grid=(t // tm,),
        in_specs=[pl.BlockSpec((tm, hv), lambda i: (i, 0)),
                  pl.BlockSpec((tm, hv), lambda i: (i, 0)),
                  pl.BlockSpec((tm, hv), lambda i: (i, 2)),
                  pl.BlockSpec((tm, qw), lambda i: (i, 0)),
                  pl.BlockSpec((tm, d), lambda i: (i, 0)),
                  pl.BlockSpec((1, 6, d), lambda i: (mod_row(i), 0, 0)),
                  full(gn), full(bd), full(w1), full(w2), full(fg), full(router_wt), full(rb)],
        out_specs=[pl.BlockSpec((tm, d), lambda i: (i, 0)),
                   pl.BlockSpec((tm * (d // LANES), LANES), lambda i: (i, 0)),
                   pl.BlockSpec((8, tm), lambda i: (0, i))],
        out_shape=[jax.ShapeDtypeStruct((t, d), F32),
                   jax.ShapeDtypeStruct((t * (d // LANES), LANES), F32),
                   jax.ShapeDtypeStruct((8, t), F32)],
        compiler_params=_cparams(("parallel",)),
        name="merge_even",
    )(o_f, o_b, zg, o_att, x, mods, gn, bd, w1, w2, fg, router_wt, rb)


def _moe_plan(route, t, rows, route_tile):
    n_tiles = 2 * t // rows + N_EXPERTS
    eid = jnp.concatenate([route[0], route[1]]).astype(jnp.int32)
    slot = jnp.arange(2 * t, dtype=jnp.int32)
    gate = jnp.concatenate([route[2], route[3]])
    _, s_slot, s_gate = lax.sort((eid, slot, gate), num_keys=1, is_stable=True)
    counts = jnp.sum(route[4].reshape(t // route_tile, route_tile)[:, :N_EXPERTS], axis=0).astype(jnp.int32)
    padded = (counts + rows - 1) // rows * rows
    p_end = jnp.cumsum(padded)
    p_start = p_end - padded
    c_start = jnp.cumsum(counts) - counts
    tile_start = jnp.arange(n_tiles, dtype=jnp.int32) * rows
    tile_e = jnp.minimum(jnp.sum((tile_start[:, None] >= p_end[None, :]).astype(jnp.int32), axis=1), N_EXPERTS - 1)
    r_in = (tile_start - p_start[tile_e])[:, None] + jnp.arange(rows, dtype=jnp.int32)[None, :]
    over = r_in - counts[tile_e][:, None]
    valid = over < 0
    src = jnp.clip(c_start[tile_e][:, None] + r_in, 0, 2 * t - 1)
    g_slot = s_slot[src]
    row_tok = jnp.where(valid, jnp.where(g_slot >= t, g_slot - t, g_slot), 0)
    row_gate = jnp.where(valid, s_gate[src], 0.0)
    row_dst = jnp.where(valid, g_slot, 2 * t + tile_e[:, None] * rows + jnp.clip(over, 0, rows - 1))
    n_used = (p_end[-1] // rows).astype(jnp.int32)
    tile_e = jnp.where(tile_start < p_end[-1], tile_e, tile_e[jnp.maximum(n_used - 1, 0)])
    return (row_tok[:, None, :], row_gate[:, None, :], row_dst[:, None, :], tile_e, n_used.reshape(1))


def _moe_experts_kernel(te_ref, nu_ref, tok0_ref, tok1_ref, tokn_ref, dst_ref, gate_ref, wg_ref, wu_ref, wd_ref, h_hbm,
                        y_hbm, hbuf, ybuf, wgb, wub, wdb, sem_g, sem_s):
    j = pl.program_id(0)
    n_used = nu_ref[0]
    slot = j % 2
    g = SUBLANES
    rows = ybuf.shape[0] // (2 * g)
    tile = rows * g
    depth = hbuf.shape[0] // tile
    gslot = lax.rem(j, depth)

    def row(ref, i):
        return ref.at[pl.ds(pl.multiple_of(i * g, g), g)]

    def start_gather(idx_ref, s):
        for r in range(rows):
            pltpu.make_async_copy(row(h_hbm, idx_ref[0, 0, r]), row(hbuf, s * rows + r), sem_g.at[s]).start(priority=r % 2)

    def wait_gather(s):
        pltpu.make_async_copy(h_hbm.at[pl.ds(0, tile)], hbuf.at[pl.ds(pl.multiple_of(s * tile, tile), tile)],
                              sem_g.at[s]).wait()

    def wait_scatter(s):
        pltpu.make_async_copy(ybuf.at[pl.ds(pl.multiple_of(s * tile, tile), tile)], y_hbm.at[pl.ds(0, tile)],
                              sem_s.at[s]).wait()

    @pl.when(j == 0)
    def _():
        start_gather(tok0_ref, 0)
        start_gather(tok1_ref, 1)
        ybuf[...] = jnp.zeros(ybuf.shape, F32)
        n_real = y_hbm.shape[0] // g - N_EXPERTS * rows
        fills = [pltpu.make_async_copy(ybuf.at[pl.ds((k % 2) * tile, tile)],
                                       y_hbm.at[pl.ds((n_real + k * rows) * g, tile)], sem_s.at[k % 2])
                 for k in range(N_EXPERTS)]
        for f in fills:
            f.start()
        for f in fills[2:]:
            f.wait()

    active = j < n_used
    changed = jnp.logical_or(j == 0, te_ref[j] != te_ref[jnp.maximum(j - 1, 0)])

    @pl.when(jnp.logical_and(active, changed))
    def _():
        wgb[...] = wg_ref[0].astype(BF16)
        wub[...] = wu_ref[0].astype(BF16)
        wdb[...] = wd_ref[0].astype(BF16)

    @pl.when(active)
    def _():
        wait_gather(gslot)
        start_gather(tokn_ref, lax.rem(j + depth - 1, depth))
        h = _load_row_tiles(hbuf, rows, g, first_row=gslot * rows).astype(BF16)
        act = _silu(_dot(h, wgb[...])) * _dot(h, wub[...])
        ri = lax.broadcasted_iota(jnp.int32, (rows, rows), 0)
        ci = lax.broadcasted_iota(jnp.int32, (rows, rows), 1)
        gate = jnp.sum(jnp.where(ri == ci, gate_ref[0], 0.0), axis=1, keepdims=True)
        y = _dot((act * gate).astype(BF16), wdb[...])
        wait_scatter(slot)
        _store_row_tiles(ybuf, y, first_row=slot * rows)
        for r in range(rows):
            pltpu.make_async_copy(row(ybuf, slot * rows + r), row(y_hbm, dst_ref[0, 0, r]), sem_s.at[slot]).start(priority=r % 2)

        @pl.when(j == n_used - 1)
        def _():
            wait_scatter(slot)
            wait_scatter(1 - slot)
            for ahead in range(1, depth):
                wait_gather(lax.rem(j + ahead, depth))


def _moe_combine_kernel(ya_ref, yb_ref, x_ref, mod_ref, o_ref):
    n, d = x_ref.shape
    y = _load_row_tiles(ya_ref, n, d // LANES) + _load_row_tiles(yb_ref, n, d // LANES)
    o_ref[...] = x_ref[...] + mod_ref[0][5:6] * y


def _moe(h, route, x, mods, w_gate, w_up, w_down, layer, tm, mod_row, route_tile, combine=True):
    t, d = x.shape
    g = d // LANES
    assert g == SUBLANES
    rows = MOE_TILE
    row_tok, row_gate, row_dst, tile_e, n_used = _moe_plan(route, t, rows, route_tile)
    n_tiles = row_tok.shape[0]
    n_pair_rows = 2 * t + N_EXPERTS * rows
    assert MOE_GATHER_DEPTH == 3
    smem_tile = lambda which: pl.BlockSpec((1, 1, rows), lambda j, te, nu: (jnp.minimum(which(j), nu[0] - 1), 0, 0),
                                           memory_space=pltpu.SMEM)
    pairs = pl.pallas_call(
        _moe_experts_kernel,
        grid_spec=pltpu.PrefetchScalarGridSpec(
            num_scalar_prefetch=2,
            grid=(n_tiles,),
            in_specs=[smem_tile(lambda j: 0), smem_tile(lambda j: 1), smem_tile(lambda j: j + MOE_GATHER_DEPTH - 1),
                      smem_tile(lambda j: j),
                      pl.BlockSpec((1, 1, rows), lambda j, te, nu: (j, 0, 0)),
                      pl.BlockSpec((None, 1, d, D_EXPERT), lambda j, te, nu: (layer, te[j], 0, 0)),
                      pl.BlockSpec((None, 1, d, D_EXPERT), lambda j, te, nu: (layer, te[j], 0, 0)),
                      pl.BlockSpec((None, 1, D_EXPERT, d), lambda j, te, nu: (layer, te[j], 0, 0)),
                      pl.BlockSpec(memory_space=pl.ANY)],
            out_specs=pl.BlockSpec(memory_space=pl.ANY),
            scratch_shapes=[pltpu.VMEM((MOE_GATHER_DEPTH * rows * g, LANES), F32), pltpu.VMEM((2 * rows * g, LANES), F32),
                            pltpu.VMEM((d, D_EXPERT), BF16), pltpu.VMEM((d, D_EXPERT), BF16),
                            pltpu.VMEM((D_EXPERT, d), BF16),
                            pltpu.SemaphoreType.DMA((MOE_GATHER_DEPTH,)), pltpu.SemaphoreType.DMA((2,))]),
        out_shape=jax.ShapeDtypeStruct((n_pair_rows * g, LANES), F32),
        compiler_params=_cparams(("arbitrary",)),
        name="moe_experts",
    )(tile_e, n_used, row_tok, row_tok, row_tok, row_dst, row_gate, w_gate, w_up, w_down, h)
    if not combine:
        return pairs
    return pl.pallas_call(
        _moe_combine_kernel,
        grid=(t // tm,),
        in_specs=[pl.BlockSpec((tm * g, LANES), lambda i: (i, 0)),
                  pl.BlockSpec((tm * g, LANES), lambda i: (t // tm + i, 0)),
                  pl.BlockSpec((tm, d), lambda i: (i, 0)),
                  pl.BlockSpec((1, 6, d), lambda i: (mod_row(i), 0, 0))],
        out_specs=pl.BlockSpec((tm, d), lambda i: (i, 0)),
        out_shape=jax.ShapeDtypeStruct((t, d), F32),
        compiler_params=_cparams(("parallel",)),
        name="moe_combine",
    )(pairs, pairs, x, mods)


def _chan_dft_kernel(z_ref, w_ref, o_ref):
    res = _dot(z_ref[...].astype(BF16), w_ref[...]).astype(BF16)
    o_ref[0] = res[:, :FOURIER_WIDTH]
    o_ref[1] = res[:, FOURIER_WIDTH:]


def _chan_dft(zf, w, batch, l_ctx, l_lat, tm):
    l_tot = l_ctx + l_lat
    nt = l_lat // tm
    fw = FOURIER_WIDTH
    return pl.pallas_call(
        _chan_dft_kernel,
        grid=(batch, nt),
        in_specs=[pl.BlockSpec((tm, fw), lambda b, i: (b * (l_tot // tm) + l_ctx // tm + i, 0)),
                  pl.BlockSpec(w.shape, lambda b, i: (0, 0))],
        out_specs=pl.BlockSpec((2, tm, fw), lambda b, i: (0, i, b)),
        out_shape=jax.ShapeDtypeStruct((2, l_lat, batch * fw), BF16),
        compiler_params=_cparams(("parallel", "parallel")),
        name="fourier_channels",
    )(zf, w)


def _seq_dft_kernel(ca_ref, sa_ref, cb_ref, sb_ref, z_ref, o_ref, acc_ref):
    k = pl.program_id(1)
    tk = z_ref.shape[1]
    sub = DFT_SPLIT
    n_a = ca_ref.shape[1]

    @pl.when(k == 0)
    def _():
        acc_ref[...] = jnp.zeros_like(acc_ref)

    col_a = k * tk + lax.broadcasted_iota(jnp.int32, (n_a, tk), 1)
    pick_a = (col_a // sub == lax.broadcasted_iota(jnp.int32, (n_a, tk), 0)).astype(F32)
    pick_b = (lax.broadcasted_iota(jnp.int32, (sub, tk), 1) % sub == lax.broadcasted_iota(jnp.int32, (sub, tk), 0)).astype(F32)
    ca, sa = _dot(ca_ref[...], pick_a), _dot(sa_ref[...], pick_a)
    cb, sb = _dot(cb_ref[...], pick_b), _dot(sb_ref[...], pick_b)
    cos_t = (ca * cb - sa * sb).astype(BF16)
    sin_t = (sa * cb + ca * sb).astype(BF16)
    acc_ref[...] += _dot(cos_t, z_ref[0]) + _dot(sin_t, z_ref[1])

    @pl.when(k == pl.num_programs(1) - 1)
    def _():
        o_ref[...] = acc_ref[...]


def _seq_dft(tables, zc, batch, tm, tk):
    ca, sa, cb, sb = tables
    l = ca.shape[0]
    fw = FOURIER_WIDTH
    small = lambda a: pl.BlockSpec((tm, a.shape[1]), lambda i, k: (i, 0))
    return pl.pallas_call(
        _seq_dft_kernel,
        grid=(l // tm, l // tk),
        in_specs=[small(ca), small(sa), small(cb), small(sb),
                  pl.BlockSpec((2, tk, batch * fw), lambda i, k: (0, k, 0))],
        out_specs=pl.BlockSpec((tm, batch * fw), lambda i, k: (i, 0)),
        out_shape=jax.ShapeDtypeStruct((l, batch * fw), F32),
        scratch_shapes=[pltpu.VMEM((tm, batch * fw), F32)],
        compiler_params=_cparams(("parallel", "arbitrary")),
        name="fourier_sequence",
    )(ca, sa, cb, sb, zc)


def _dft_tables(l):
    m = jnp.arange(l, dtype=jnp.int32)[:, None]
    n1 = l // DFT_SPLIT
    a = (m * (jnp.arange(n1, dtype=jnp.int32)[None, :] * DFT_SPLIT)) % l
    b = (m * jnp.arange(DFT_SPLIT, dtype=jnp.int32)[None, :]) % l
    wa = a.astype(F32) * (2.0 * np.pi / l)
    wb = b.astype(F32) * (2.0 * np.pi / l)
    tables = (jnp.cos(wa), jnp.sin(wa), jnp.cos(wb), jnp.sin(wb))
    gd = FOURIER_GROUP_DIM
    cc = (jnp.arange(gd, dtype=jnp.int32)[:, None] * jnp.arange(gd, dtype=jnp.int32)[None, :]) % gd
    wc = cc.astype(F32) * (2.0 * np.pi / gd)
    scale = 1.0 / np.sqrt(float(l) * gd)
    eye = jnp.eye(FOURIER_WIDTH // gd, dtype=F32)
    chan = jnp.concatenate([jnp.kron(eye, jnp.cos(wc)), -jnp.kron(eye, jnp.sin(wc))], axis=1) * scale
    return tables, chan.astype(BF16)


def _rwkv_prepare(d, pos, z, z_before, z_after, par, nc_ctx, nc_tot):
    mu, kks, ka, rk, w0, w2, a0, a2, bd = par
    c = z.shape[0]
    n = RWKV_DIM
    seg_first = jnp.logical_or(pos == 0, pos == nc_ctx)
    seg_last = jnp.logical_or(pos == nc_ctx - 1, pos == nc_tot - 1)
    row = lax.broadcasted_iota(jnp.int32, z.shape, 0)
    prev_row = jnp.where(seg_first, 0.0, z_before[7:8, :])
    next_row = jnp.where(seg_last, 0.0, z_after[0:1, :])
    z_prev = jnp.where(row == 0, prev_row, pltpu.roll(z, 1, 0))
    z_next = jnp.where(row == c - 1, next_row, pltpu.roll(z, c - 1, 0))
    zs = z + mu * (0.5 * (z_prev + z_next) - z)

    r, k, v = zs[:, 0:n], zs[:, n:2 * n], zs[:, 2 * n:3 * n]
    zw = zs[:, 3 * n:3 * n + RWKV_RANK_PAD]
    za = zs[:, 3 * n + RWKV_RANK_PAD:3 * n + 2 * RWKV_RANK_PAD]
    zg = zs[:, 3 * n + 2 * RWKV_RANK_PAD:3 * n + 3 * RWKV_RANK_PAD]

    kk = k * kks
    kk = kk * lax.rsqrt(_dot(kk * kk, bd) + L2_EPS)
    w_log = _log_sigmoid(w0[d] + _dot(jnp.tanh(zw), w2[d])) - 0.5
    lw = -jnp.exp(w_log)
    a = jax.nn.sigmoid(a0[d] + _dot(za, a2[d]))
    kd = k * (1.0 + (a - 1.0) * ka)
    beta = kk * a

    cl = _dot((_visit_order(d, c) >= 0).astype(F32), lw)
    c_tot = jnp.sum(lw, axis=0, keepdims=True)
    grow = jnp.exp(-cl)
    tail = jnp.exp(c_tot - cl)
    ops = dict(k_s=kd * grow,
               b_s=beta * grow,
               kap_s=kk * jnp.exp(cl - lw),
               r_s=r * jnp.exp(cl),
               k_e=kd * tail,
               b_e=beta * tail,
               gam=jnp.exp(c_tot), v=v,
               bonus=_dot(r * kd * rk, bd),
               sign=1 if d == 0 else -1)
    return ops, zg


def _rwkv_kernel(zf_ref, zfp_ref, zfn_ref, zb_ref, zbp_ref, zbn_ref, mu_ref, kks_ref, ka_ref, rk_ref, w0_ref, w2_ref,
                 a0_ref, a2_ref, g2_ref, bd_ref, yf_ref, yb_ref, gate_ref, st_ref, *, nc_ctx, nc_tot):
    s = pl.program_id(1)
    nb, c = zf_ref.shape[0], zf_ref.shape[1]
    hd = HEAD_DIM
    pw = 2 * hd

    @pl.when(s == 0)
    def _():
        st_ref[...] = jnp.zeros_like(st_ref)

    par = (mu_ref[...], kks_ref[...], ka_ref[...], rk_ref[...], w0_ref, w2_ref, a0_ref, a2_ref, bd_ref[...])
    ctxs = []
    for b in range(nb):
        fwd, zg = _rwkv_prepare(0, _chunk_pos(s, 0, nc_ctx, nc_tot), zf_ref[b], zfp_ref[b], zfn_ref[b], par, nc_ctx, nc_tot)
        bwd, _ = _rwkv_prepare(1, _chunk_pos(s, 1, nc_ctx, nc_tot), zb_ref[b], zbp_ref[b], zbn_ref[b], par, nc_ctx, nc_tot)
        gate_ref[b] = _dot(jax.nn.sigmoid(zg), g2_ref[...])
        ctxs += [fwd, bwd]

    lane = lax.broadcasted_iota(jnp.int32, (c, pw), 1)
    rowi = lax.broadcasted_iota(jnp.int32, (c, pw), 0)
    left = lane < hd
    eye_p = (rowi == lane % hd).astype(F32)
    same_head = (lax.broadcasted_iota(jnp.int32, (pw, pw), 0) < hd) == (lax.broadcasted_iota(jnp.int32, (pw, pw), 1) < hd)

    def bd(y):
        return jnp.concatenate([jnp.where(left, y, 0.0), jnp.where(left, 0.0, y)], axis=0)

    n_pairs = RWKV_HEADS // 2
    prob = [(o, slice(i * pw, (i + 1) * pw)) for o in ctxs for i in range(n_pairs)]
    ahead = [(rowi - lane % hd) * o['sign'] for o, _ in prob]
    get = lambda name: [o[name][:, s_] for o, s_ in prob]
    kap, r_s, k_s, b_s, k_e, b_e, vp, gam, bonus = (get(x) for x in ('kap_s', 'r_s', 'k_s', 'b_s', 'k_e', 'b_e', 'v', 'gam', 'bonus'))
    p = [_dot_nt(jnp.concatenate([a_, b_], axis=0), jnp.concatenate([bd(c_), bd(d_)], axis=0))
         for a_, b_, c_, d_ in zip(kap, r_s, k_s, b_s)]
    m1 = [jnp.where(h_ > 0, x[0:c, 0:pw], 0.0) for x, h_ in zip(p, ahead)]
    m2 = [jnp.where(h_ > 0, x[0:c, pw:2 * pw], 0.0) for x, h_ in zip(p, ahead)]
    n1 = [jnp.where(h_ >= 0, x[c:2 * c, 0:pw], 0.0) for x, h_ in zip(p, ahead)]
    n2 = [jnp.where(h_ >= 0, x[c:2 * c, pw:2 * pw], 0.0) for x, h_ in zip(p, ahead)]
    m1v = [_dot(a_, bd(b_)) for a_, b_ in zip(m1, vp)]
    t_inv = [eye_p - x for x in m2]
    q = [_dot(x, bd(x)) for x in m2]
    span = 2
    while 2 * span < c:
        both = [_dot(jnp.concatenate([t_, q_], axis=0), bd(q_)) for t_, q_ in zip(t_inv, q)]
        t_inv = [t_ + x[0:c] for t_, x in zip(t_inv, both)]
        q = [x[c:2 * c] for x in both]
        span *= 2
    t_inv = [t_ + _dot(t_, bd(q_)) for t_, q_ in zip(t_inv, q)]
    tx = [_dot(t_, jnp.concatenate([bd(a_), bd(mv)], axis=1)) for t_, a_, mv in zip(t_inv, kap, m1v)]
    st = [st_ref[i] for i in range(len(prob))]
    su = [_dot_nt(jnp.concatenate([x[:, 0:pw], r_], axis=0), s0) for x, r_, s0 in zip(tx, r_s, st)]
    u = [x[0:c] + y_[:, pw:2 * pw] for x, y_ in zip(su, tx)]
    ys = [x[c:2 * c] + _dot(jnp.concatenate([a_, -b_], axis=1), jnp.concatenate([bd(v_), bd(u_)], axis=0)) + bo * v_
          for x, a_, b_, v_, u_, bo in zip(su, n1, n2, vp, u, bonus)]
    for i in range(len(prob)):
        upd = _dot_tn(jnp.concatenate([vp[i], u[i]], axis=0), jnp.concatenate([k_e[i], -b_e[i]], axis=0))
        st_ref[i] = st[i] * gam[i] + jnp.where(same_head, upd, 0.0)
    for b in range(nb):
        yf_ref[b] = jnp.concatenate(ys[2 * b * n_pairs:(2 * b + 1) * n_pairs], axis=1)
        yb_ref[b] = jnp.concatenate(ys[(2 * b + 1) * n_pairs:(2 * b + 2) * n_pairs], axis=1)


def _rwkv(zr, mu, kk_scale, k_a, r_k, w0, w2_pad, a0, a2_pad, g2, batch, nc_ctx, nc_tot):
    t, zw_ = zr.shape
    c = SEQ_CHUNK
    n = RWKV_DIM
    bd = _block_ones(RWKV_HEADS, HEAD_DIM)
    full = lambda a: pl.BlockSpec(a.shape, lambda b, s: (0,) * a.ndim)
    sub = c // SUBLANES
    l_tot = t // batch
    n_sub = l_tot // SUBLANES
    nb = math.gcd(batch, RWKV_SCAN_BATCH)
    z3 =zr.reshape(batch, l_tot, zw_)

    def z_specs(d):
        pos = lambda s: _chunk_pos(s, d, nc_ctx, nc_tot)
        return [pl.BlockSpec((nb, c, zw_), lambda b, s: (b, pos(s), 0)),
                pl.BlockSpec((nb, SUBLANES, zw_), lambda b, s: (b, jnp.maximum(pos(s) * sub - 1, 0), 0)),
                pl.BlockSpec((nb, SUBLANES, zw_), lambda b, s: (b, jnp.minimum((pos(s) + 1) * sub, n_sub - 1), 0))]

    vec = lambda a: a.reshape(1, -1)
    args = (vec(mu), vec(kk_scale), vec(k_a), vec(r_k), w0.reshape(2, 1, n), w2_pad, a0.reshape(2, 1, n), a2_pad, g2, bd)
    out = lambda d: pl.BlockSpec((nb, c, n), lambda b, s: (b, _chunk_pos(s, d, nc_ctx, nc_tot), 0))
    outs = pl.pallas_call(
        functools.partial(_rwkv_kernel, nc_ctx=nc_ctx, nc_tot=nc_tot),
        grid=(batch // nb, nc_tot),
        in_specs=z_specs(0) + z_specs(1) + [full(a) for a in args],
        out_specs=[out(0), out(1), out(0)],
        out_shape=[jax.ShapeDtypeStruct((batch, l_tot, n), F32)] * 3,
        scratch_shapes=[pltpu.VMEM((nb * RWKV_HEADS, 2 * HEAD_DIM, 2 * HEAD_DIM), F32)],
        compiler_params=_cparams(("parallel", "arbitrary")),
        name="rwkv_scan",
    )(z3, z3, z3, z3, z3, z3, *args)
    return [o.reshape(t, n) for o in outs]


def _merge_odd_kernel(y0_ref, y1_ref, gate_ref, fo_ref, x_ref, mod_ref, lg_ref, lb_ref, bd_ref, w1_ref, w2_ref,
                      fg_ref, rw_ref, rb_ref, xo_ref, ho_ref, go_ref):
    y = y0_ref[...] + y1_ref[...]
    bd = bd_ref[...]
    mean = _dot(y, bd) * (1.0 / HEAD_DIM)
    yc = y - mean
    var = _dot(yc * yc, bd) * (1.0 / HEAD_DIM)
    rw = (yc * lax.rsqrt(var + RWKV_GN_EPS) * lg_ref[...] + lb_ref[...]) * gate_ref[...]
    out = _dot(fo_ref[...].astype(BF16), w1_ref[...]) + _dot(rw.astype(BF16), w2_ref[...])
    _residual_and_route(x_ref[...], out, mod_ref[0], fg_ref[...], rw_ref, rb_ref, xo_ref, ho_ref, go_ref)


def _merge_odd(y0, y1, gate, fo, x, mods, ln_g, ln_b, w_out, ffn_gain, router_wt, router_b, tm, batch, l_ctx, l_lat):
    d = x.shape[1]
    n = RWKV_DIM
    fw = FOURIER_WIDTH
    l_tot = l_ctx + l_lat
    nt = l_lat // tm
    t_out = batch * l_lat
    full = lambda a: pl.BlockSpec(a.shape, lambda b, i: (0,) * a.ndim)
    src = lambda b, i: b * (l_tot // tm) + l_ctx // tm + i
    bd = _block_ones(RWKV_HEADS, HEAD_DIM)
    w1, w2 = w_out[:fw].astype(BF16), w_out[fw:].astype(BF16)
    lg, lb, fg, rb = ln_g.reshape(1, n), ln_b.reshape(1, n), ffn_gain.reshape(1, d), router_b.reshape(N_EXPERTS, 1)
    return pl.pallas_call(
        _merge_odd_kernel,
        grid=(batch, nt),
        in_specs=[pl.BlockSpec((tm, n), lambda b, i: (src(b, i), 0)),
                  pl.BlockSpec((tm, n), lambda b, i: (src(b, i), 0)),
                  pl.BlockSpec((tm, n), lambda b, i: (src(b, i), 0)),
                  pl.BlockSpec((tm, fw), lambda b, i: (i, b)),
                  pl.BlockSpec((tm, d), lambda b, i: (src(b, i), 0)),
                  pl.BlockSpec((1, 6, d), lambda b, i: (b, 0, 0)),
                  full(lg), full(lb), full(bd), full(w1), full(w2), full(fg), full(router_wt), full(rb)],
        out_specs=[pl.BlockSpec((tm, d), lambda b, i: (b * nt + i, 0)),
                   pl.BlockSpec((tm * (d // LANES), LANES), lambda b, i: (b * nt + i, 0)),
                   pl.BlockSpec((8, tm), lambda b, i: (0, b * nt + i))],
        out_shape=[jax.ShapeDtypeStruct((t_out, d), F32),
                   jax.ShapeDtypeStruct((t_out * (d // LANES), LANES), F32),
                   jax.ShapeDtypeStruct((8, t_out), F32)],
        compiler_params=_cparams(("parallel", "parallel")),
        name="merge_odd",
    )(y0, y1, gate, fo, x, mods, lg, lb, bd, w1, w2, fg, router_wt, rb)


def _rope_tables(l_ctx, l_lat):
    rows = l_lat // GRID_W
    row = jnp.repeat(jnp.arange(rows, dtype=F32), GRID_W)
    col = jnp.tile(jnp.arange(GRID_W, dtype=F32), rows)
    n_freq = HEAD_DIM // 4
    inv_freq = ROPE_THETA ** (-jnp.arange(n_freq, dtype=F32) / n_freq)
    ang = jnp.concatenate([row[:, None] * inv_freq, col[:, None] * inv_freq], axis=-1)
    cos, sin = jnp.cos(ang), jnp.sin(ang)
    cos64 = jnp.concatenate([cos, cos], axis=1)
    sin64 = jnp.concatenate([-sin, sin], axis=1)
    cos64 = jnp.concatenate([jnp.ones((l_ctx, HEAD_DIM), F32), cos64], axis=0)
    sin64 = jnp.concatenate([jnp.zeros((l_ctx, HEAD_DIM), F32), sin64], axis=0)
    return cos64, sin64


def _pad_rank(w):
    _, r, n = w.shape
    out = jnp.zeros((2, RWKV_RANK_PAD, n), w.dtype)
    out = out.at[0, 0:r].set(w[0])
    return out.at[1, r:2 * r].set(w[1])


def _even_layer(ctx, x_lat, mods, p, batch, l_ctx, l_lat, tm, mod_row):
    l_tot = l_ctx + l_lat
    nc_ctx, nc_tot = l_ctx // SEQ_CHUNK, l_tot // SEQ_CHUNK
    hk, hv = GLA_HEADS * GLA_DK, GLA_HEADS * GLA_DV
    qw, kw = ATT_HEADS * HEAD_DIM, ATT_KV_HEADS * HEAD_DIM
    w_in = p['w_in']
    o = np.cumsum([0, hk, hk, hv, hv, 2 * GLA_LOWRANK, qw, kw, kw])
    w_gla = jnp.concatenate([w_in[:, o[0]:o[4]]], axis=1).astype(BF16)
    w_dec = jnp.pad(w_in[:, o[4]:o[5]], ((0, 0), (0, 128 - 2 * GLA_LOWRANK))).astype(BF16)
    w_q = w_in[:, o[5]:o[6]].astype(BF16)
    w_kv = w_in[:, o[6]:o[8]].astype(BF16)
    cos64, sin64 = _rope_tables(l_ctx, l_lat)
    x, zg, zdec, qn, kn, vn = _project_even(ctx, x_lat, mods, p['norm_mix'], [w_gla, w_dec, w_q, w_kv], cos64, sin64,
                                            p['q_norm'], p['k_norm'], tm, mod_row, batch, l_ctx, l_lat)

    dec_w_pad = _pad_rank(p['dec_w'])
    o_f, o_b = _gla(zg, zdec, dec_w_pad, p['dec_b'].reshape(2, 1, hk), batch, nc_ctx, nc_tot)
    o_att = _attention(qn, kn, vn, p['sink'], batch, l_ctx, l_tot)

    x1, h, gates = _merge_even(o_f, o_b, zg, o_att, x, mods, p['out_norm'], p['w_out'], p['norm_ffn'],
                               p['router_wt'], p['router_b'], tm, mod_row)
    pairs = _moe(h, gates, x1, mods, p['moe_g'], p['moe_u'], p['moe_d'], p['moe_layer'], tm, mod_row, tm, combine=False)
    return pairs, x1


def _odd_layer(pairs, x_prev, mods_prev, mods, p, batch, l_ctx, l_lat, tm, mod_row, tm_moe):
    l_tot = l_ctx + l_lat
    nc_ctx, nc_tot = l_ctx // SEQ_CHUNK, l_tot // SEQ_CHUNK
    n = RWKV_DIM
    fw = FOURIER_WIDTH
    w_in = p['w_in']
    rank_w, rank_a = p['w2'].shape[1], p['a2'].shape[1]
    o = np.cumsum([0, fw, n, n, n, 2 * rank_w, 2 * rank_a])
    pad_cols = lambda w: jnp.pad(w, ((0, 0), (0, RWKV_RANK_PAD - w.shape[1])))
    w_f = w_in[:, o[0]:o[1]].astype(BF16)
    w_r = jnp.concatenate([w_in[:, o[1]:o[4]], pad_cols(w_in[:, o[4]:o[5]]), pad_cols(w_in[:, o[5]:o[6]]),
                           w_in[:, o[6]:]], axis=1).astype(BF16)
    x, (zf, zr) = _project(pairs, x_prev, mods_prev, mods, p['norm_mix'], [w_f, w_r], tm, mod_row)

    mu = p['mu']
    mu_r = jnp.concatenate([mu[0:3 * n], pad_cols(mu[None, 3 * n:3 * n + 2 * rank_w])[0],
                            pad_cols(mu[None, 3 * n + 2 * rank_w:3 * n + 2 * rank_w + 2 * rank_a])[0],
                            mu[3 * n + 2 * rank_w + 2 * rank_a:]])
    y0, y1, gate = _rwkv(zr, mu_r, p['kk_scale'], p['k_a'], p['r_k'].reshape(-1), p['w0'], _pad_rank(p['w2']),
                    p['a0'], _pad_rank(p['a2']), p['g2'], batch, nc_ctx, nc_tot)

    tables, chan = _dft_tables(l_lat)
    zc = _chan_dft(zf, chan, batch, l_ctx, l_lat, tm)
    fo = _seq_dft(tables, zc, batch, min(512, l_lat), min(1024, l_lat))

    x1, h, gates = _merge_odd(y0, y1, gate, fo, x, mods, p['ln_g'], p['ln_b'], p['w_out'], p['norm_ffn'],
                              p['router_wt'], p['router_b'], tm, batch, l_ctx, l_lat)
    lat_tiles = l_lat // tm_moe
    return _moe(h, gates, x1, mods, p['moe_g'], p['moe_u'], p['moe_d'], p['moe_layer'], tm_moe,
                lambda i: i // lat_tiles, tm)


def kernel(x, c, ctx, c_ctx, ada_w, ada_b, norm_mix, norm_ffn, even_w_in, even_w_out, gla_dec_w, gla_dec_b, gla_out_norm, att_q_norm, att_k_norm, att_sink, odd_w_in, odd_w_out, rwkv_mu, rwkv_w0, rwkv_w2, rwkv_a0, rwkv_a2, rwkv_g2, rwkv_kk_scale, rwkv_k_a, rwkv_r_k, rwkv_ln_g, rwkv_ln_b, router_w, router_b, moe_w_gate, moe_w_up, moe_w_down):
    batch, l_lat, d = x.shape
    l_ctx = ctx.shape[1]
    l_tot = l_ctx + l_lat
    assert batch < 8 and ada_w.shape[0] == 2
    tm = 256 if (l_ctx % 256 == 0 and l_lat % 256 == 0) else 128
    tm_moe = 512 if (l_lat % 512 == 0 and tm == 256) else tm
    assert l_ctx % tm == 0 and l_lat % tm == 0 and l_tot % l_ctx == 0 and l_lat % GRID_W == 0

    cc = jnp.concatenate([c, c_ctx[None, :], jnp.zeros((8 - batch - 1, d), F32)], axis=0)
    tiles_per_b = l_tot // tm
    ctx_tiles = l_ctx // tm

    def mod_row(i):
        return jnp.where(i % tiles_per_b < ctx_tiles, batch, i // tiles_per_b)

    rw_pad = jnp.pad(router_w, ((0, 0), (0, LANES - N_EXPERTS)))
    rw_hi = rw_pad.astype(BF16)
    router_wt = jnp.stack([rw_hi, (rw_pad - rw_hi.astype(F32)).astype(BF16)])
    moe = lambda layer: dict(moe_g=moe_w_gate, moe_u=moe_w_up, moe_d=moe_w_down, moe_layer=layer)

    mods_all = _modvec(cc, ada_w, ada_b)
    mods0 = mods_all[0]
    p0 = dict(w_in=even_w_in[0], w_out=even_w_out[0], dec_w=gla_dec_w[0], dec_b=gla_dec_b[0],
              out_norm=gla_out_norm[0], q_norm=att_q_norm[0], k_norm=att_k_norm[0], sink=att_sink[0],
              norm_mix=norm_mix[0], norm_ffn=norm_ffn[0], router_wt=router_wt, router_b=router_b, **moe(0))
    pairs, x1 = _even_layer(ctx.reshape(batch * l_ctx, d), x.reshape(batch * l_lat, d), mods0, p0, batch, l_ctx, l_lat,
                            tm, mod_row)

    mods1 = mods_all[1]
    p1 = dict(w_in=odd_w_in[0], w_out=odd_w_out[0], mu=rwkv_mu[0], w0=rwkv_w0[0], w2=rwkv_w2[0], a0=rwkv_a0[0],
              a2=rwkv_a2[0], g2=rwkv_g2[0], kk_scale=rwkv_kk_scale[0], k_a=rwkv_k_a[0], r_k=rwkv_r_k[0],
              ln_g=rwkv_ln_g[0], ln_b=rwkv_ln_b[0], norm_mix=norm_mix[1], norm_ffn=norm_ffn[1],
              router_wt=router_wt, router_b=router_b, **moe(1))
    out = _odd_layer(pairs, x1, mods0, mods1, p1, batch, l_ctx, l_lat, tm, mod_row, tm_moe)
    return out.reshape(batch, l_lat, d)
```

```python
import functools
import math

import jax
import jax.numpy as jnp
import numpy as np
from jax import lax
from jax.experimental import pallas as pl
from jax.experimental.pallas import tpu as pltpu

F32 = jnp.float32
BF16 = jnp.bfloat16

GRID_W = 64
HEAD_DIM = 64
NORM_EPS = 1e-6
L2_EPS = 1e-12

GLA_DV = 64
GLA_DK = 32
GLA_HEADS = 8
GLA_LOWRANK = 16
GLA_TAU = 16.0

ATT_HEADS = 8
ATT_KV_HEADS = 2
ATT_GROUP = ATT_HEADS // ATT_KV_HEADS
ATT_BLOCK = 128
ROPE_THETA = 10000.0

FOURIER_GROUP_DIM = 64
FOURIER_WIDTH = 256
DFT_SPLIT = 64

RWKV_DIM = 768
RWKV_HEADS = 12
RWKV_RANK_PAD = 128
RWKV_GN_EPS = 64e-5

N_EXPERTS = 16
N_GROUPS = 4
PER_GROUP = N_EXPERTS // N_GROUPS
D_EXPERT = 512
MOE_TILE = 256
MOE_GATHER_DEPTH = 3
LANES = 128
SUBLANES = 8

SEQ_CHUNK = 64
GLA_SCAN_BATCH = 4
RWKV_SCAN_BATCH = 2
VMEM_LIMIT = 56 * 1024 * 1024


def _cparams(sem):
    return pltpu.CompilerParams(dimension_semantics=sem, vmem_limit_bytes=VMEM_LIMIT)


def _dot(a, b):
    return jnp.dot(a, b, preferred_element_type=F32)


def _dot_nt(a, b):
    return lax.dot_general(a, b, (((1,), (1,)), ((), ())), preferred_element_type=F32)


def _dot_tn(a, b):
    return lax.dot_general(a, b, (((0,), (0,)), ((), ())), preferred_element_type=F32)


def _silu(x):
    return x * jax.nn.sigmoid(x)


def _log_sigmoid(x):
    return jnp.minimum(x, 0.0) - jnp.log(1.0 + jnp.exp(-jnp.abs(x)))


def _modulated_norm(x, gain, shift, scale):
    ms = jnp.mean(x * x, axis=-1, keepdims=True)
    return (x * lax.rsqrt(ms + NORM_EPS) * gain) * (1.0 + scale) + shift


def _block_ones(n_blocks, width):
    return jnp.kron(jnp.eye(n_blocks, dtype=F32), jnp.ones((width, width), F32))


def _head_sums(x, ones_blk):
    w = ones_blk.shape[0]
    return jnp.concatenate([_dot(x[:, i:i + w], ones_blk) for i in range(0, x.shape[1], w)], axis=1)


def _store_row_tiles(ref, x, first_row=0):
    n, w = x.shape
    g = w // LANES
    for k in range(g):
        ref[pl.ds(first_row * g + k, n, stride=g), :] = x[:, k * LANES:(k + 1) * LANES]


def _load_row_tiles(ref, n, g, first_row=0):
    return jnp.concatenate([ref[pl.ds(first_row * g + k, n, stride=g), :] for k in range(g)], axis=1)


def _modvec_kernel(c_ref, w_ref, b_ref, o_ref):
    o_ref[...] = _dot(_silu(c_ref[...]), w_ref[...]) + b_ref[...]


def _modvec(cc, w, b):
    d = cc.shape[1]
    layers, _, n = w.shape
    tn = n // 4
    out = pl.pallas_call(
        _modvec_kernel,
        grid=(layers, n // tn),
        in_specs=[pl.BlockSpec((8, d), lambda l, j: (0, 0)),
                  pl.BlockSpec((None, d, tn), lambda l, j: (l, 0, j)),
                  pl.BlockSpec((None, 1, tn), lambda l, j: (l, 0, j))],
        out_specs=pl.BlockSpec((None, 8, tn), lambda l, j: (l, 0, j)),
        out_shape=jax.ShapeDtypeStruct((layers, 8, n), F32),
        compiler_params=_cparams(("parallel", "parallel")),
        name="modvec",
    )(cc, w, b.reshape(layers, 1, n))
    return out.reshape(layers, 8, 6, d)


def _proj_kernel(ya_ref, yb_ref, x_ref, mp0_ref, mp1_ref, m0_ref, m1_ref, gain_ref, *refs, n_out):
    w_refs, x_out_ref, z_refs = refs[:n_out], refs[n_out], refs[n_out + 1:]
    n, d = x_ref.shape[0] // 2, x_ref.shape[1]
    g = d // LANES
    for half, (mp_ref, m_ref) in enumerate(((mp0_ref, m0_ref), (mp1_ref, m1_ref))):
        rows = pl.ds(half * n, n)
        y = _load_row_tiles(ya_ref, n, g, first_row=half * n) + _load_row_tiles(yb_ref, n, g, first_row=half * n)
        x = x_ref[rows, :] + mp_ref[0][5:6] * y
        x_out_ref[rows, :] = x
        m = m_ref[0]
        h = _modulated_norm(x, gain_ref[...], m[0:1], m[1:2]).astype(BF16)
        for w_ref, z_ref in zip(w_refs, z_refs):
            z_ref[rows, :] = _dot(h, w_ref[...])


def _project(pairs, x, mods_prev, mods, gain, weights, tm, mod_row):
    t, d = x.shape
    g = d // LANES
    n_out = len(weights)
    assert (t // tm) % 2 == 0
    tb = 2 * tm
    mod_spec = lambda half: pl.BlockSpec((1, 6, d), lambda i: (mod_row(2 * i + half), 0, 0))
    in_specs = [pl.BlockSpec((tb * g, LANES), lambda i: (i, 0)),
                pl.BlockSpec((tb * g, LANES), lambda i: (t // tb + i, 0)),
                pl.BlockSpec((tb, d), lambda i: (i, 0)),
                mod_spec(0), mod_spec(1), mod_spec(0), mod_spec(1),
                pl.BlockSpec((1, d), lambda i: (0, 0))]
    in_specs += [pl.BlockSpec(w.shape, lambda i: (0, 0)) for w in weights]
    outs = pl.pallas_call(
        functools.partial(_proj_kernel, n_out=n_out),
        grid=(t // tb,),
        in_specs=in_specs,
        out_specs=[pl.BlockSpec((tb, d), lambda i: (i, 0))] + [pl.BlockSpec((tb, w.shape[1]), lambda i: (i, 0)) for w in weights],
        out_shape=[jax.ShapeDtypeStruct((t, d), F32)] + [jax.ShapeDtypeStruct((t, w.shape[1]), F32) for w in weights],
        compiler_params=_cparams(("parallel",)),
        name="proj",
    )(pairs, pairs, x, mods_prev, mods_prev, mods, mods, gain.reshape(1, d), *weights)
    return outs[0], outs[1:]


def _chunk_pos(s, d, nc_ctx, nc_tot):
    back = jnp.where(s < nc_ctx, nc_ctx - 1 - s, nc_tot + nc_ctx - 1 - s)
    return jnp.where(d == 0, s, back)


def _visit_order(d, c):
    sign = 1 if d == 0 else -1
    return (lax.broadcasted_iota(jnp.int32, (c, c), 0) - lax.broadcasted_iota(jnp.int32, (c, c), 1)) * sign


def _gla_prepare(d, q, k, v, dec, dw_ref, db_ref):
    c = q.shape[0]
    g = _log_sigmoid(_dot(dec, dw_ref[d]) + db_ref[d]) / GLA_TAU
    b = _dot((_visit_order(d, c) >= 0).astype(F32), g)
    b_tot = jnp.sum(g, axis=0, keepdims=True)
    return dict(q_in=q * (GLA_DK ** -0.5) * jnp.exp(b), k_out=k * jnp.exp(-b), k_end=k * jnp.exp(b_tot - b),
                decay=jnp.exp(b_tot), v=v, sign=1 if d == 0 else -1)


def _gla_kernel(qf_ref, kf_ref, vf_ref, df_ref, qb_ref, kb_ref, vb_ref, db_ref, dw_ref, dbias_ref, of_ref, ob_ref, st_ref):
    nb, c = qf_ref.shape[0], qf_ref.shape[1]
    group = 4
    kw, vw = group * GLA_DK, group * GLA_DV

    @pl.when(pl.program_id(1) == 0)
    def _():
        st_ref[...] = jnp.zeros_like(st_ref)

    ctxs = []
    for u in range(nb):
        ctxs.append(_gla_prepare(0, qf_ref[u], kf_ref[u], vf_ref[u], df_ref[u], dw_ref, dbias_ref))
        ctxs.append(_gla_prepare(1, qb_ref[u], kb_ref[u], vb_ref[u], db_ref[u], dw_ref, dbias_ref))
    n_quads = GLA_HEADS // group
    prob = [(o, i) for o in ctxs for i in range(n_quads)]
    klane = lax.broadcasted_iota(jnp.int32, (c, kw), 1) // GLA_DK
    half = lax.broadcasted_iota(jnp.int32, (c, 2 * GLA_DV), 1) < GLA_DV
    rowi = lax.broadcasted_iota(jnp.int32, (c, group * c), 0)
    coli = lax.broadcasted_iota(jnp.int32, (c, group * c), 1) % c
    own = (lax.broadcasted_iota(jnp.int32, (vw, kw), 0) // GLA_DV) == (lax.broadcasted_iota(jnp.int32, (vw, kw), 1) // GLA_DK)

    def bd_keys(y):
        return jnp.concatenate([jnp.where(klane == h, y, 0.0) for h in range(group)], axis=0)

    def bd_vals(y):
        return jnp.concatenate([jnp.where(half, y, 0.0), jnp.where(half, 0.0, y)], axis=0)

    ksl = lambda i: slice(i * kw, (i + 1) * kw)
    vsl = lambda i: slice(i * vw, (i + 1) * vw)
    q_in = [o['q_in'][:, ksl(i)] for o, i in prob]
    att = [jnp.where((rowi - coli) * o['sign'] >= 0, _dot_nt(q_, bd_keys(o['k_out'][:, ksl(i)])), 0.0)
           for q_, (o, i) in zip(q_in, prob)]
    st = [st_ref[j] for j in range(len(prob))]
    outs = []
    for j, (o, i) in enumerate(prob):
        v = o['v'][:, vsl(i)]
        intra = jnp.concatenate([_dot(att[j][:, p * 2 * c:(p + 1) * 2 * c], bd_vals(v[:, p * 2 * GLA_DV:(p + 1) * 2 * GLA_DV]))
                                 for p in range(group // 2)], axis=1)
        outs.append(intra + _dot_nt(q_in[j], st[j]))
    for j, (o, i) in enumerate(prob):
        upd = _dot_tn(o['v'][:, vsl(i)], o['k_end'][:, ksl(i)])
        st_ref[j] = st[j] * o['decay'][:, ksl(i)] + jnp.where(own, upd, 0.0)
    for u in range(nb):
        of_ref[u] = jnp.concatenate(outs[2 * u * n_quads:(2 * u + 1) * n_quads], axis=1)
        ob_ref[u] = jnp.concatenate(outs[(2 * u + 1) * n_quads:(2 * u + 2) * n_quads], axis=1)


def _gla(zg, zdec, dec_w_pad, dec_b, batch, nc_ctx, nc_tot):
    t = zg.shape[0]
    c = SEQ_CHUNK
    hk, hv = GLA_HEADS * GLA_DK, GLA_HEADS * GLA_DV
    nb = math.gcd(batch, GLA_SCAN_BATCH)
    zg3, zdec3 = zg.reshape(batch, t // batch, -1), zdec.reshape(batch, t // batch, -1)

    def specs(d):
        pos = lambda s: _chunk_pos(s, d, nc_ctx, nc_tot)
        return [pl.BlockSpec((nb, c, hk), lambda b, s: (b, pos(s), 0)),
                pl.BlockSpec((nb, c, hk), lambda b, s: (b, pos(s), 1)),
                pl.BlockSpec((nb, c, hv), lambda b, s: (b, pos(s), 1)),
                pl.BlockSpec((nb, c, 128), lambda b, s: (b, pos(s), 0))], pl.BlockSpec((nb, c, hv), lambda b, s: (b, pos(s), 0))

    in_f, out_f = specs(0)
    in_b, out_b = specs(1)
    o_f, o_b = pl.pallas_call(
        _gla_kernel,
        grid=(batch // nb, nc_tot),
        in_specs=in_f + in_b + [pl.BlockSpec(dec_w_pad.shape, lambda b, s: (0, 0, 0)),
                                pl.BlockSpec(dec_b.shape, lambda b, s: (0, 0, 0))],
        out_specs=[out_f, out_b],
        out_shape=[jax.ShapeDtypeStruct((batch, t // batch, hv), F32)] * 2,
        scratch_shapes=[pltpu.VMEM((nb * 2 * GLA_HEADS // 4, 4 * GLA_DV, 4 * GLA_DK), F32)],
        compiler_params=_cparams(("parallel", "arbitrary")),
        name="gla_scan",
    )(zg3, zg3, zg3, zdec3, zg3, zg3, zg3, zdec3, dec_w_pad, dec_b)
    return o_f.reshape(t, hv), o_b.reshape(t, hv)


def _rope_swap(x):
    n = x.shape[-1]
    lane = lax.broadcasted_iota(jnp.int32, x.shape, x.ndim - 1)
    half = HEAD_DIM // 2
    return jnp.where(lane % HEAD_DIM < half, pltpu.roll(x, n - half, x.ndim - 1), pltpu.roll(x, half, x.ndim - 1))


def _proj_even_kernel(ctx_ref, x_ref, mod_ref, gain_ref, wg_ref, wd_ref, wq_ref, wkv_ref, cos_ref, sin_ref, qg_ref, kg_ref,
                      bdq_ref, bdk_ref, xs_ref, zg_ref, zdec_ref, qo_ref, ko_ref, vo_ref, *, tiles_per_b, ctx_tiles):
    is_ctx = lax.rem(pl.program_id(0), tiles_per_b) < ctx_tiles
    x = jnp.where(is_ctx, ctx_ref[...], x_ref[...])
    xs_ref[...] = x
    m = mod_ref[0]
    h = _modulated_norm(x, gain_ref[...], m[0:1], m[1:2]).astype(BF16)
    zg_ref[...] = _dot(h, wg_ref[...])
    zdec_ref[...] = _dot(h, wd_ref[...])

    def norm_rope(z, gain, bd, n_heads):
        ms = _head_sums(z * z, bd) * (1.0 / HEAD_DIM)
        zn = z * lax.rsqrt(ms + NORM_EPS) * gain
        cos = jnp.concatenate([cos_ref[...]] * n_heads, axis=1)
        sin = jnp.concatenate([sin_ref[...]] * n_heads, axis=1)
        return zn * cos + _rope_swap(zn) * sin

    q = norm_rope(_dot(h, wq_ref[...]), qg_ref[...], bdq_ref[...], ATT_HEADS)
    qo_ref[...] = (q * (HEAD_DIM ** -0.5)).astype(BF16)
    kw = ATT_KV_HEADS * HEAD_DIM
    kv = _dot(h, wkv_ref[...])
    ko_ref[...] = norm_rope(kv[:, :kw], kg_ref[...], bdk_ref[...], ATT_KV_HEADS).astype(BF16)
    vo_ref[...] = kv[:, kw:].astype(BF16)


def _project_even(ctx, x, mods, gain, weights, cos64, sin64, q_gain, k_gain, tm, mod_row, batch, l_ctx, l_lat):
    d = x.shape[1]
    l_tot = l_ctx + l_lat
    t = batch * l_tot
    tiles_per_b, ctx_tiles, lat_tiles = l_tot // tm, l_ctx // tm, l_lat // tm
    qw, kw = ATT_HEADS * HEAD_DIM, ATT_KV_HEADS * HEAD_DIM
    w_gla, w_dec, w_q, w_kv = weights
    full = lambda a: pl.BlockSpec(a.shape, lambda i: (0,) * a.ndim)
    qg = jnp.tile(q_gain, ATT_HEADS).reshape(1, qw)
    kg = jnp.tile(k_gain, ATT_KV_HEADS).reshape(1, kw)
    bdq = bdk = _block_ones(LANES // HEAD_DIM, HEAD_DIM)
    gain = gain.reshape(1, d)
    row = lambda n: pl.BlockSpec((tm, n), lambda i: (i, 0))
    pos = pl.BlockSpec((tm, HEAD_DIM), lambda i: (i % tiles_per_b, 0))
    return pl.pallas_call(
        functools.partial(_proj_even_kernel, tiles_per_b=tiles_per_b, ctx_tiles=ctx_tiles),
        grid=(t // tm,),
        in_specs=[pl.BlockSpec((tm, d), lambda i: ((i // tiles_per_b) * ctx_tiles + jnp.minimum(i % tiles_per_b, ctx_tiles - 1), 0)),
                  pl.BlockSpec((tm, d), lambda i: ((i // tiles_per_b) * lat_tiles + jnp.maximum(i % tiles_per_b - ctx_tiles, 0), 0)),
                  pl.BlockSpec((1, 6, d), lambda i: (mod_row(i), 0, 0)),
                  full(gain), full(w_gla), full(w_dec), full(w_q), full(w_kv), pos, pos,
                  full(qg), full(kg), full(bdq), full(bdk)],
        out_specs=[row(d), row(w_gla.shape[1]), row(w_dec.shape[1]), row(qw), row(kw), row(kw)],
        out_shape=[jax.ShapeDtypeStruct((t, d), F32), jax.ShapeDtypeStruct((t, w_gla.shape[1]), F32),
                   jax.ShapeDtypeStruct((t, w_dec.shape[1]), F32), jax.ShapeDtypeStruct((t, qw), BF16),
                   jax.ShapeDtypeStruct((t, kw), BF16), jax.ShapeDtypeStruct((t, kw), BF16)],
        compiler_params=_cparams(("parallel",)),
        name="proj_even",
    )(ctx, x, mods, gain, w_gla, w_dec, w_q, w_kv, cos64, sin64, qg, kg, bdq, bdk)


def _attn_kernel(q_ref, kp_ref, kc_ref, kn_ref, kx_ref, vp_ref, vc_ref, vn_ref, vx_ref, sink_ref, o_ref,
                 *, n_ctx_blocks, n_lat_blocks):
    blk = ATT_BLOCK
    n = pl.program_id(1)
    m = n - n_ctx_blocks
    is_lat = n >= n_ctx_blocks
    l_ctx = kx_ref.shape[0]
    width = 3 * blk + l_ctx
    rows = ATT_GROUP * blk
    r = lax.broadcasted_iota(jnp.int32, (rows, width), 0) % blk
    c = lax.broadcasted_iota(jnp.int32, (rows, width), 1)
    lat = is_lat.astype(jnp.int32)
    has_prev = lat * (m >= 1).astype(jnp.int32)
    has_next = lat * (m <= n_lat_blocks - 2).astype(jnp.int32)
    valid = jnp.where(c < blk, (c >= r).astype(jnp.int32) * has_prev,
                      jnp.where(c < 2 * blk, lat,
                                jnp.where(c < 3 * blk, (c - 2 * blk <= r).astype(jnp.int32) * has_next, 1))) > 0
    q = q_ref[...]
    sink = sink_ref[...]
    groups = range(ATT_KV_HEADS)
    ks = [slice(kvh * HEAD_DIM, (kvh + 1) * HEAD_DIM) for kvh in groups]
    heads = [range(kvh * ATT_GROUP, (kvh + 1) * ATT_GROUP) for kvh in groups]
    kw = [jnp.concatenate([kp_ref[:, s_], kc_ref[:, s_], kn_ref[:, s_], kx_ref[:, s_]], axis=0) for s_ in ks]
    vw = [jnp.concatenate([vp_ref[:, s_], vc_ref[:, s_], vn_ref[:, s_], vx_ref[:, s_]], axis=0) for s_ in ks]
    qg = [jnp.concatenate([q[:, h * HEAD_DIM:(h + 1) * HEAD_DIM] for h in hs], axis=0) for hs in heads]
    sk = [jnp.concatenate([jnp.broadcast_to(sink[h:h + 1, 0:1], (blk, 1)) for h in hs], axis=0) for hs in heads]
    s = [jnp.where(valid, _dot_nt(a_, b_), -jnp.inf) for a_, b_ in zip(qg, kw)]
    mx = [jnp.maximum(jnp.max(a_, axis=-1, keepdims=True), b_) for a_, b_ in zip(s, sk)]
    p = [jnp.exp(a_ - b_) for a_, b_ in zip(s, mx)]
    denom = [jnp.sum(a_, axis=-1, keepdims=True) + jnp.exp(b_ - c_) for a_, b_, c_ in zip(p, sk, mx)]
    o = [_dot(a_.astype(BF16), b_) / c_ for a_, b_, c_ in zip(p, vw, denom)]
    outs = [o[kvh][g * blk:(g + 1) * blk] for kvh in groups for g in range(ATT_GROUP)]
    o_ref[...] = jnp.concatenate(outs, axis=1)


def _attention(qn, kn, vn, sink, batch, l_ctx, l_tot):
    t = qn.shape[0]
    blk = ATT_BLOCK
    nq = l_tot // blk
    nc = l_ctx // blk
    nl = nq - nc
    qw, kw = ATT_HEADS * HEAD_DIM, ATT_KV_HEADS * HEAD_DIM

    def win(off):
        def index(b, n):
            m = jnp.clip(n - nc + off, 0, nl - 1)
            return (b * nq + nc + m, 0)
        return pl.BlockSpec((blk, kw), index)

    ctx_spec = pl.BlockSpec((l_ctx, kw), lambda b, n: (b * (l_tot // l_ctx), 0))
    return pl.pallas_call(
        functools.partial(_attn_kernel, n_ctx_blocks=nc, n_lat_blocks=nl),
        grid=(batch, nq),
        in_specs=[pl.BlockSpec((blk, qw), lambda b, n: (b * nq + n, 0)),
                  win(-1), win(0), win(1), ctx_spec,
                  win(-1), win(0), win(1), ctx_spec,
                  pl.BlockSpec((ATT_HEADS, 128), lambda b, n: (0, 0))],
        out_specs=pl.BlockSpec((blk, qw), lambda b, n: (b * nq + n, 0)),
        out_shape=jax.ShapeDtypeStruct((t, qw), F32),
        compiler_params=_cparams(("parallel", "parallel")),
        name="window_attention",
    )(qn, kn, kn, kn, kn, vn, vn, vn, vn, jnp.broadcast_to(sink.astype(F32)[:, None], (ATT_HEADS, 128)))


def _route(logits_t, bias_col):
    scores = jax.nn.sigmoid(logits_t)
    sel = scores + bias_col
    rows = [sel[e:e + 1] for e in range(N_EXPERTS)]
    grp = []
    for g in range(N_GROUPS):
        r = rows[g * PER_GROUP:(g + 1) * PER_GROUP]
        best = None
        for i in range(PER_GROUP):
            for j in range(i + 1, PER_GROUP):
                pair = r[i] + r[j]
                best = pair if best is None else jnp.maximum(best, pair)
        grp.append(best)
    g_best = jnp.zeros_like(grp[0], dtype=jnp.int32)
    g_val = grp[0]
    for g in range(1, N_GROUPS):
        take = grp[g] > g_val
        g_best = jnp.where(take, g, g_best)
        g_val = jnp.where(take, grp[g], g_val)
    neg = -jnp.inf
    masked = [jnp.where(g_best == e // PER_GROUP, rows[e], neg) for e in range(N_EXPERTS)]
    i1 = jnp.zeros_like(g_best)
    v1 = masked[0]
    for e in range(1, N_EXPERTS):
        take = masked[e] > v1
        i1 = jnp.where(take, e, i1)
        v1 = jnp.where(take, masked[e], v1)
    i2 = jnp.full_like(g_best, -1)
    v2 = jnp.full_like(v1, neg)
    for e in range(N_EXPERTS):
        take = jnp.logical_and(i1 != e, masked[e] > v2)
        i2 = jnp.where(take, e, i2)
        v2 = jnp.where(take, masked[e], v2)
    w1 = jnp.zeros_like(v1)
    w2 = jnp.zeros_like(v1)
    for e in range(N_EXPERTS):
        w1 = jnp.where(i1 == e, scores[e:e + 1], w1)
        w2 = jnp.where(i2 == e, scores[e:e + 1], w2)
    inv = 1.0 / (w1 + w2)
    pad = jnp.zeros_like(w1)
    lane = lax.broadcasted_iota(jnp.int32, w1.shape, 1)
    hist = pad
    for e in range(N_EXPERTS):
        n_e = jnp.sum((i1 == e).astype(F32) + (i2 == e).astype(F32), axis=1, keepdims=True)
        hist = jnp.where(lane == e, n_e, hist)
    return jnp.concatenate([i1.astype(F32), i2.astype(F32), w1 * inv, w2 * inv, hist, pad, pad, pad], axis=0)


def _residual_and_route(x, out, m, ffn_gain, rw_ref, rb_ref, x_ref, h_ref, g_ref):
    x1 = x + m[2:3] * out
    x_ref[...] = x1
    h = _modulated_norm(x1, ffn_gain, m[3:4], m[4:5])
    _store_row_tiles(h_ref, h)
    h_hi = h.astype(BF16)
    h_lo = (h - h_hi.astype(F32)).astype(BF16)
    logits = _dot(h_hi, rw_ref[0]) + _dot(h_lo, rw_ref[0]) + _dot(h_hi, rw_ref[1])
    g_ref[...] = _route(logits.T[:N_EXPERTS], rb_ref[...])


def _merge_even_kernel(o0_ref, o1_ref, gg_ref, oa_ref, x_ref, mod_ref, gn_ref, bd_ref, w1_ref, w2_ref,
                       fg_ref, rw_ref, rb_ref, xo_ref, ho_ref, go_ref):
    og = o0_ref[...] + o1_ref[...]
    ms = _head_sums(og * og, bd_ref[...]) * (1.0 / GLA_DV)
    g = og * lax.rsqrt(ms + NORM_EPS) * gn_ref[...] * _silu(gg_ref[...])
    out = _dot(g.astype(BF16), w1_ref[...]) + _dot(oa_ref[...].astype(BF16), w2_ref[...])
    _residual_and_route(x_ref[...], out, mod_ref[0], fg_ref[...], rw_ref, rb_ref, xo_ref, ho_ref, go_ref)


def _merge_even(o_f, o_b, zg, o_att, x, mods, out_norm, w_out, ffn_gain, router_wt, router_b, tm, mod_row):
    t, d = x.shape
    hv = GLA_HEADS * GLA_DV
    qw = ATT_HEADS * HEAD_DIM
    full = lambda a: pl.BlockSpec(a.shape, lambda i: (0,) * a.ndim)
    gn = jnp.tile(out_norm, GLA_HEADS).reshape(1, hv)
    bd = _block_ones(LANES // GLA_DV, GLA_DV)
    w1, w2 = w_out[:hv].astype(BF16), w_out[hv:].astype(BF16)
    fg = ffn_gain.reshape(1, d)
    rb = router_b.reshape(N_EXPERTS, 1)
    return pl.pallas_call(
        _merge_even_kernel,
        grid=(t // tm,),
        in_specs=[pl.BlockSpec((tm, hv), lambda i: (i, 0)),
                  pl.BlockSpec((tm, hv), lambda i: (i, 0)),
                  pl.BlockSpec((tm, hv), lambda i: (i, 2)),
                  pl.BlockSpec((tm, qw), lambda i: (i, 0)),
                  pl.BlockSpec((tm, d), lambda i: (i, 0)),
                  pl.BlockSpec((1, 6, d), lambda i: (mod_row(i), 0, 0)),
                  full(gn), full(bd), full(w1), full(w2), full(fg), full(router_wt), full(rb)],
        out_specs=[pl.BlockSpec((tm, d), lambda i: (i, 0)),
                   pl.BlockSpec((tm * (d // LANES), LANES), lambda i: (i, 0)),
                   pl.BlockSpec((8, tm), lambda i: (0, i))],
        out_shape=[jax.ShapeDtypeStruct((t, d), F32),
                   jax.ShapeDtypeStruct((t * (d // LANES), LANES), F32),
                   jax.ShapeDtypeStruct((8, t), F32)],
        compiler_params=_cparams(("parallel",)),
        name="merge_even",
    )(o_f, o_b, zg, o_att, x, mods, gn, bd, w1, w2, fg, router_wt, rb)


def _moe_plan(route, t, rows, route_tile):
    n_tiles = 2 * t // rows + N_EXPERTS
    eid = jnp.concatenate([route[0], route[1]]).astype(jnp.int32)
    slot = jnp.arange(2 * t, dtype=jnp.int32)
    gate = jnp.concatenate([route[2], route[3]])
    _, s_slot, s_gate = lax.sort((eid, slot, gate), num_keys=1, is_stable=True)
    counts = jnp.sum(route[4].reshape(t // route_tile, route_tile)[:, :N_EXPERTS], axis=0).astype(jnp.int32)
    padded = (counts + rows - 1) // rows * rows
    p_end = jnp.cumsum(padded)
    p_start = p_end - padded
    c_start = jnp.cumsum(counts) - counts
    tile_start = jnp.arange(n_tiles, dtype=jnp.int32) * rows
    tile_e = jnp.minimum(jnp.sum((tile_start[:, None] >= p_end[None, :]).astype(jnp.int32), axis=1), N_EXPERTS - 1)
    r_in = (tile_start - p_start[tile_e])[:, None] + jnp.arange(rows, dtype=jnp.int32)[None, :]
    over = r_in - counts[tile_e][:, None]
    valid = over < 0
    src = jnp.clip(c_start[tile_e][:, None] + r_in, 0, 2 * t - 1)
    g_slot = s_slot[src]
    row_tok = jnp.where(valid, jnp.where(g_slot >= t, g_slot - t, g_slot), 0)
    row_gate = jnp.where(valid, s_gate[src], 0.0)
    row_dst = jnp.where(valid, g_slot, 2 * t + tile_e[:, None] * rows + jnp.clip(over, 0, rows - 1))
    n_used = (p_end[-1] // rows).astype(jnp.int32)
    tile_e = jnp.where(tile_start < p_end[-1], tile_e, tile_e[jnp.maximum(n_used - 1, 0)])
    return (row_tok[:, None, :], row_gate[:, None, :], row_dst[:, None, :], tile_e, n_used.reshape(1))


def _moe_experts_kernel(te_ref, nu_ref, tok0_ref, tok1_ref, tokn_ref, dst_ref, gate_ref, wg_ref, wu_ref, wd_ref, h_hbm,
                        y_hbm, hbuf, ybuf, wgb, wub, wdb, sem_g, sem_s):
    j = pl.program_id(0)
    n_used = nu_ref[0]
    slot = j % 2
    g = SUBLANES
    rows = ybuf.shape[0] // (2 * g)
    tile = rows * g
    depth = hbuf.shape[0] // tile
    gslot = lax.rem(j, depth)

    def row(ref, i):
        return ref.at[pl.ds(pl.multiple_of(i * g, g), g)]

    def start_gather(idx_ref, s):
        for r in range(rows):
            pltpu.make_async_copy(row(h_hbm, idx_ref[0, 0, r]), row(hbuf, s * rows + r), sem_g.at[s]).start(priority=r % 2)

    def wait_gather(s):
        pltpu.make_async_copy(h_hbm.at[pl.ds(0, tile)], hbuf.at[pl.ds(pl.multiple_of(s * tile, tile), tile)],
                              sem_g.at[s]).wait()

    def wait_scatter(s):
        pltpu.make_async_copy(ybuf.at[pl.ds(pl.multiple_of(s * tile, tile), tile)], y_hbm.at[pl.ds(0, tile)],
                              sem_s.at[s]).wait()

    @pl.when(j == 0)
    def _():
        start_gather(tok0_ref, 0)
        start_gather(tok1_ref, 1)
        ybuf[...] = jnp.zeros(ybuf.shape, F32)
        n_real = y_hbm.shape[0] // g - N_EXPERTS * rows
        fills = [pltpu.make_async_copy(ybuf.at[pl.ds((k % 2) * tile, tile)],
                                       y_hbm.at[pl.ds((n_real + k * rows) * g, tile)], sem_s.at[k % 2])
                 for k in range(N_EXPERTS)]
        for f in fills:
            f.start()
        for f in fills[2:]:
            f.wait()

    active = j < n_used
    changed = jnp.logical_or(j == 0, te_ref[j] != te_ref[jnp.maximum(j - 1, 0)])

    @pl.when(jnp.logical_and(active, changed))
    def _():
        wgb[...] = wg_ref[0].astype(BF16)
        wub[...] = wu_ref[0].astype(BF16)
        wdb[...] = wd_ref[0].astype(BF16)

    @pl.when(active)
    def _():
        wait_gather(gslot)
        start_gather(tokn_ref, lax.rem(j + depth - 1, depth))
        h = _load_row_tiles(hbuf, rows, g, first_row=gslot * rows).astype(BF16)
        act = _silu(_dot(h, wgb[...])) * _dot(h, wub[...])
        ri = lax.broadcasted_iota(jnp.int32, (rows, rows), 0)
        ci = lax.broadcasted_iota(jnp.int32, (rows, rows), 1)
        gate = jnp.sum(jnp.where(ri == ci, gate_ref[0], 0.0), axis=1, keepdims=True)
        y = _dot((act * gate).astype(BF16), wdb[...])
        wait_scatter(slot)
        _store_row_tiles(ybuf, y, first_row=slot * rows)
        for r in range(rows):
            pltpu.make_async_copy(row(ybuf, slot * rows + r), row(y_hbm, dst_ref[0, 0, r]), sem_s.at[slot]).start(priority=r % 2)

        @pl.when(j == n_used - 1)
        def _():
            wait_scatter(slot)
            wait_scatter(1 - slot)
            for ahead in range(1, depth):
                wait_gather(lax.rem(j + ahead, depth))


def _moe_combine_kernel(ya_ref, yb_ref, x_ref, mod_ref, o_ref):
    n, d = x_ref.shape
    y = _load_row_tiles(ya_ref, n, d // LANES) + _load_row_tiles(yb_ref, n, d // LANES)
    o_ref[...] = x_ref[...] + mod_ref[0][5:6] * y


def _moe(h, route, x, mods, w_gate, w_up, w_down, layer, tm, mod_row, route_tile, combine=True):
    t, d = x.shape
    g = d // LANES
    assert g == SUBLANES
    rows = MOE_TILE
    row_tok, row_gate, row_dst, tile_e, n_used = _moe_plan(route, t, rows, route_tile)
    n_tiles = row_tok.shape[0]
    n_pair_rows = 2 * t + N_EXPERTS * rows
    assert MOE_GATHER_DEPTH == 3
    smem_tile = lambda which: pl.BlockSpec((1, 1, rows), lambda j, te, nu: (jnp.minimum(which(j), nu[0] - 1), 0, 0),
                                           memory_space=pltpu.SMEM)
    pairs = pl.pallas_call(
        _moe_experts_kernel,
        grid_spec=pltpu.PrefetchScalarGridSpec(
            num_scalar_prefetch=2,
            grid=(n_tiles,),
            in_specs=[smem_tile(lambda j: 0), smem_tile(lambda j: 1), smem_tile(lambda j: j + MOE_GATHER_DEPTH - 1),
                      smem_tile(lambda j: j),
                      pl.BlockSpec((1, 1, rows), lambda j, te, nu: (j, 0, 0)),
                      pl.BlockSpec((None, 1, d, D_EXPERT), lambda j, te, nu: (layer, te[j], 0, 0)),
                      pl.BlockSpec((None, 1, d, D_EXPERT), lambda j, te, nu: (layer, te[j], 0, 0)),
                      pl.BlockSpec((None, 1, D_EXPERT, d), lambda j, te, nu: (layer, te[j], 0, 0)),
                      pl.BlockSpec(memory_space=pl.ANY)],
            out_specs=pl.BlockSpec(memory_space=pl.ANY),
            scratch_shapes=[pltpu.VMEM((MOE_GATHER_DEPTH * rows * g, LANES), F32), pltpu.VMEM((2 * rows * g, LANES), F32),
                            pltpu.VMEM((d, D_EXPERT), BF16), pltpu.VMEM((d, D_EXPERT), BF16),
                            pltpu.VMEM((D_EXPERT, d), BF16),
                            pltpu.SemaphoreType.DMA((MOE_GATHER_DEPTH,)), pltpu.SemaphoreType.DMA((2,))]),
        out_shape=jax.ShapeDtypeStruct((n_pair_rows * g, LANES), F32),
        compiler_params=_cparams(("arbitrary",)),
        name="moe_experts",
    )(tile_e, n_used, row_tok, row_tok, row_tok, row_dst, row_gate, w_gate, w_up, w_down, h)
    if not combine:
        return pairs
    return pl.pallas_call(
        _moe_combine_kernel,
        grid=(t // tm,),
        in_specs=[pl.BlockSpec((tm * g, LANES), lambda i: (i, 0)),
                  pl.BlockSpec((tm * g, LANES), lambda i: (t // tm + i, 0)),
                  pl.BlockSpec((tm, d), lambda i: (i, 0)),
                  pl.BlockSpec((1, 6, d), lambda i: (mod_row(i), 0, 0))],
        out_specs=pl.BlockSpec((tm, d), lambda i: (i, 0)),
        out_shape=jax.ShapeDtypeStruct((t, d), F32),
        compiler_params=_cparams(("parallel",)),
        name="moe_combine",
    )(pairs, pairs, x, mods)


def _chan_dft_kernel(z_ref, w_ref, o_ref):
    res = _dot(z_ref[...].astype(BF16), w_ref[...]).astype(BF16)
    o_ref[0] = res[:, :FOURIER_WIDTH]
    o_ref[1] = res[:, FOURIER_WIDTH:]


def _chan_dft(zf, w, batch, l_ctx, l_lat, tm):
    l_tot = l_ctx + l_lat
    nt = l_lat // tm
    fw = FOURIER_WIDTH
    return pl.pallas_call(
        _chan_dft_kernel,
        grid=(batch, nt),
        in_specs=[pl.BlockSpec((tm, fw), lambda b, i: (b * (l_tot // tm) + l_ctx // tm + i, 0)),
                  pl.BlockSpec(w.shape, lambda b, i: (0, 0))],
        out_specs=pl.BlockSpec((2, tm, fw), lambda b, i: (0, i, b)),
        out_shape=jax.ShapeDtypeStruct((2, l_lat, batch * fw), BF16),
        compiler_params=_cparams(("parallel", "parallel")),
        name="fourier_channels",
    )(zf, w)


def _seq_dft_kernel(ca_ref, sa_ref, cb_ref, sb_ref, z_ref, o_ref, acc_ref):
    k = pl.program_id(1)
    tk = z_ref.shape[1]
    sub = DFT_SPLIT
    n_a = ca_ref.shape[1]

    @pl.when(k == 0)
    def _():
        acc_ref[...] = jnp.zeros_like(acc_ref)

    col_a = k * tk + lax.broadcasted_iota(jnp.int32, (n_a, tk), 1)
    pick_a = (col_a // sub == lax.broadcasted_iota(jnp.int32, (n_a, tk), 0)).astype(F32)
    pick_b = (lax.broadcasted_iota(jnp.int32, (sub, tk), 1) % sub == lax.broadcasted_iota(jnp.int32, (sub, tk), 0)).astype(F32)
    ca, sa = _dot(ca_ref[...], pick_a), _dot(sa_ref[...], pick_a)
    cb, sb = _dot(cb_ref[...], pick_b), _dot(sb_ref[...], pick_b)
    cos_t = (ca * cb - sa * sb).astype(BF16)
    sin_t = (sa * cb + ca * sb).astype(BF16)
    acc_ref[...] += _dot(cos_t, z_ref[0]) + _dot(sin_t, z_ref[1])

    @pl.when(k == pl.num_programs(1) - 1)
    def _():
        o_ref[...] = acc_ref[...]


def _seq_dft(tables, zc, batch, tm, tk):
    ca, sa, cb, sb = tables
    l = ca.shape[0]
    fw = FOURIER_WIDTH
    small = lambda a: pl.BlockSpec((tm, a.shape[1]), lambda i, k: (i, 0))
    return pl.pallas_call(
        _seq_dft_kernel,
        grid=(l // tm, l // tk),
        in_specs=[small(ca), small(sa), small(cb), small(sb),
                  pl.BlockSpec((2, tk, batch * fw), lambda i, k: (0, k, 0))],
        out_specs=pl.BlockSpec((tm, batch * fw), lambda i, k: (i, 0)),
        out_shape=jax.ShapeDtypeStruct((l, batch * fw), F32),
        scratch_shapes=[pltpu.VMEM((tm, batch * fw), F32)],
        compiler_params=_cparams(("parallel", "arbitrary")),
        name="fourier_sequence",
    )(ca, sa, cb, sb, zc)


def _dft_tables(l):
    m = jnp.arange(l, dtype=jnp.int32)[:, None]
    n1 = l // DFT_SPLIT
    a = (m * (jnp.arange(n1, dtype=jnp.int32)[None, :] * DFT_SPLIT)) % l
    b = (m * jnp.arange(DFT_SPLIT, dtype=jnp.int32)[None, :]) % l
    wa = a.astype(F32) * (2.0 * np.pi / l)
    wb = b.astype(F32) * (2.0 * np.pi / l)
    tables = (jnp.cos(wa), jnp.sin(wa), jnp.cos(wb), jnp.sin(wb))
    gd = FOURIER_GROUP_DIM
    cc = (jnp.arange(gd, dtype=jnp.int32)[:, None] * jnp.arange(gd, dtype=jnp.int32)[None, :]) % gd
    wc = cc.astype(F32) * (2.0 * np.pi / gd)
    scale = 1.0 / np.sqrt(float(l) * gd)
    eye = jnp.eye(FOURIER_WIDTH // gd, dtype=F32)
    chan = jnp.concatenate([jnp.kron(eye, jnp.cos(wc)), -jnp.kron(eye, jnp.sin(wc))], axis=1) * scale
    return tables, chan.astype(BF16)


def _rwkv_prepare(d, pos, z, z_before, z_after, par, nc_ctx, nc_tot):
    mu, kks, ka, rk, w0, w2, a0, a2, bd = par
    c = z.shape[0]
    n = RWKV_DIM
    seg_first = jnp.logical_or(pos == 0, pos == nc_ctx)
    seg_last = jnp.logical_or(pos == nc_ctx - 1, pos == nc_tot - 1)
    row = lax.broadcasted_iota(jnp.int32, z.shape, 0)
    prev_row = jnp.where(seg_first, 0.0, z_before[7:8, :])
    next_row = jnp.where(seg_last, 0.0, z_after[0:1, :])
    z_prev = jnp.where(row == 0, prev_row, pltpu.roll(z, 1, 0))
    z_next = jnp.where(row == c - 1, next_row, pltpu.roll(z, c - 1, 0))
    zs = z + mu * (0.5 * (z_prev + z_next) - z)

    r, k, v = zs[:, 0:n], zs[:, n:2 * n], zs[:, 2 * n:3 * n]
    zw = zs[:, 3 * n:3 * n + RWKV_RANK_PAD]
    za = zs[:, 3 * n + RWKV_RANK_PAD:3 * n + 2 * RWKV_RANK_PAD]
    zg = zs[:, 3 * n + 2 * RWKV_RANK_PAD:3 * n + 3 * RWKV_RANK_PAD]

    kk = k * kks
    kk = kk * lax.rsqrt(_head_sums(kk * kk, bd) + L2_EPS)
    w_log = _log_sigmoid(w0[d] + _dot(jnp.tanh(zw), w2[d])) - 0.5
    lw = -jnp.exp(w_log)
    a = jax.nn.sigmoid(a0[d] + _dot(za, a2[d]))
    kd = k * (1.0 + (a - 1.0) * ka)
    beta = kk * a

    cl = _dot((_visit_order(d, c) >= 0).astype(F32), lw)
    c_tot = jnp.sum(lw, axis=0, keepdims=True)
    grow = jnp.exp(-cl)
    tail = jnp.exp(c_tot - cl)
    ops = dict(k_s=kd * grow,
               b_s=beta * grow,
               kap_s=kk * jnp.exp(cl - lw),
               r_s=r * jnp.exp(cl),
               k_e=kd * tail,
               b_e=beta * tail,
               gam=jnp.exp(c_tot), v=v,
               bonus=_head_sums(r * kd * rk, bd),
               sign=1 if d == 0 else -1)
    return ops, zg


def _rwkv_kernel(zf_ref, zfp_ref, zfn_ref, zb_ref, zbp_ref, zbn_ref, mu_ref, kks_ref, ka_ref, rk_ref, w0_ref, w2_ref,
                 a0_ref, a2_ref, g2_ref, bd_ref, yf_ref, yb_ref, gate_ref, st_ref, *, nc_ctx, nc_tot):
    s = pl.program_id(1)
    nb, c = zf_ref.shape[0], zf_ref.shape[1]
    hd = HEAD_DIM
    pw = 2 * hd

    @pl.when(s == 0)
    def _():
        st_ref[...] = jnp.zeros_like(st_ref)

    par = (mu_ref[...], kks_ref[...], ka_ref[...], rk_ref[...], w0_ref, w2_ref, a0_ref, a2_ref, bd_ref[...])
    ctxs = []
    for b in range(nb):
        fwd, zg = _rwkv_prepare(0, _chunk_pos(s, 0, nc_ctx, nc_tot), zf_ref[b], zfp_ref[b], zfn_ref[b], par, nc_ctx, nc_tot)
        bwd, _ = _rwkv_prepare(1, _chunk_pos(s, 1, nc_ctx, nc_tot), zb_ref[b], zbp_ref[b], zbn_ref[b], par, nc_ctx, nc_tot)
        gate_ref[b] = _dot(jax.nn.sigmoid(zg), g2_ref[...])
        ctxs += [fwd, bwd]

    lane = lax.broadcasted_iota(jnp.int32, (c, pw), 1)
    rowi = lax.broadcasted_iota(jnp.int32, (c, pw), 0)
    left = lane < hd
    eye_p = (rowi == lane % hd).astype(F32)
    same_head = (lax.broadcasted_iota(jnp.int32, (pw, pw), 0) < hd) == (lax.broadcasted_iota(jnp.int32, (pw, pw), 1) < hd)

    def bd(y):
        return jnp.concatenate([jnp.where(left, y, 0.0), jnp.where(left, 0.0, y)], axis=0)

    n_pairs = RWKV_HEADS // 2
    prob = [(o, slice(i * pw, (i + 1) * pw)) for o in ctxs for i in range(n_pairs)]
    ahead = [(rowi - lane % hd) * o['sign'] for o, _ in prob]
    get = lambda name: [o[name][:, s_] for o, s_ in prob]
    kap, r_s, k_s, b_s, k_e, b_e, vp, gam, bonus = (get(x) for x in ('kap_s', 'r_s', 'k_s', 'b_s', 'k_e', 'b_e', 'v', 'gam', 'bonus'))
    p = [_dot_nt(jnp.concatenate([a_, b_], axis=0), jnp.concatenate([bd(c_), bd(d_)], axis=0))
         for a_, b_, c_, d_ in zip(kap, r_s, k_s, b_s)]
    m1 = [jnp.where(h_ > 0, x[0:c, 0:pw], 0.0) for x, h_ in zip(p, ahead)]
    m2 = [jnp.where(h_ > 0, x[0:c, pw:2 * pw], 0.0) for x, h_ in zip(p, ahead)]
    n1 = [jnp.where(h_ >= 0, x[c:2 * c, 0:pw], 0.0) for x, h_ in zip(p, ahead)]
    n2 = [jnp.where(h_ >= 0, x[c:2 * c, pw:2 * pw], 0.0) for x, h_ in zip(p, ahead)]
    m1v = [_dot(a_, bd(b_)) for a_, b_ in zip(m1, vp)]
    t_inv = [eye_p - x for x in m2]
    q = [_dot(x, bd(x)) for x in m2]
    span = 2
    while 2 * span < c:
        both = [_dot(jnp.concatenate([t_, q_], axis=0), bd(q_)) for t_, q_ in zip(t_inv, q)]
        t_inv = [t_ + x[0:c] for t_, x in zip(t_inv, both)]
        q = [x[c:2 * c] for x in both]
        span *= 2
    t_inv = [t_ + _dot(t_, bd(q_)) for t_, q_ in zip(t_inv, q)]
    tx = [_dot(t_, jnp.concatenate([bd(a_), bd(mv)], axis=1)) for t_, a_, mv in zip(t_inv, kap, m1v)]
    st = [st_ref[i] for i in range(len(prob))]
    su = [_dot_nt(jnp.concatenate([x[:, 0:pw], r_], axis=0), s0) for x, r_, s0 in zip(tx, r_s, st)]
    u = [x[0:c] + y_[:, pw:2 * pw] for x, y_ in zip(su, tx)]
    ys = [x[c:2 * c] + _dot(jnp.concatenate([a_, -b_], axis=1), jnp.concatenate([bd(v_), bd(u_)], axis=0)) + bo * v_
          for x, a_, b_, v_, u_, bo in zip(su, n1, n2, vp, u, bonus)]
    for i in range(len(prob)):
        upd = _dot_tn(jnp.concatenate([vp[i], u[i]], axis=0), jnp.concatenate([k_e[i], -b_e[i]], axis=0))
        st_ref[i] = st[i] * gam[i] + jnp.where(same_head, upd, 0.0)
    for b in range(nb):
        yf_ref[b] = jnp.concatenate(ys[2 * b * n_pairs:(2 * b + 1) * n_pairs], axis=1)
        yb_ref[b] = jnp.concatenate(ys[(2 * b + 1) * n_pairs:(2 * b + 2) * n_pairs], axis=1)


def _rwkv(zr, mu, kk_scale, k_a, r_k, w0, w2_pad, a0, a2_pad, g2, batch, nc_ctx, nc_tot):
    t, zw_ = zr.shape
    c = SEQ_CHUNK
    n = RWKV_DIM
    bd = _block_ones(LANES // HEAD_DIM, HEAD_DIM)
    full = lambda a: pl.BlockSpec(a.shape, lambda b, s: (0,) * a.ndim)
    sub = c // SUBLANES
    l_tot = t // batch
    n_sub = l_tot // SUBLANES
    nb = math.gcd(batch, RWKV_SCAN_BATCH)
    z3 =zr.reshape(batch, l_tot, zw_)

    def z_specs(d):
        pos = lambda s: _chunk_pos(s, d, nc_ctx, nc_tot)
        return [pl.BlockSpec((nb, c, zw_), lambda b, s: (b, pos(s), 0)),
                pl.BlockSpec((nb, SUBLANES, zw_), lambda b, s: (b, jnp.maximum(pos(s) * sub - 1, 0), 0)),
                pl.BlockSpec((nb, SUBLANES, zw_), lambda b, s: (b, jnp.minimum((pos(s) + 1) * sub, n_sub - 1), 0))]

    vec = lambda a: a.reshape(1, -1)
    args = (vec(mu), vec(kk_scale), vec(k_a), vec(r_k), w0.reshape(2, 1, n), w2_pad, a0.reshape(2, 1, n), a2_pad, g2, bd)
    out = lambda d: pl.BlockSpec((nb, c, n), lambda b, s: (b, _chunk_pos(s, d, nc_ctx, nc_tot), 0))
    outs = pl.pallas_call(
        functools.partial(_rwkv_kernel, nc_ctx=nc_ctx, nc_tot=nc_tot),
        grid=(batch // nb, nc_tot),
        in_specs=z_specs(0) + z_specs(1) + [full(a) for a in args],
        out_specs=[out(0), out(1), out(0)],
        out_shape=[jax.ShapeDtypeStruct((batch, l_tot, n), F32)] * 3,
        scratch_shapes=[pltpu.VMEM((nb * RWKV_HEADS, 2 * HEAD_DIM, 2 * HEAD_DIM), F32)],
        compiler_params=_cparams(("parallel", "arbitrary")),
        name="rwkv_scan",
    )(z3, z3, z3, z3, z3, z3, *args)
    return [o.reshape(t, n) for o in outs]


def _merge_odd_kernel(y0_ref, y1_ref, gate_ref, fo_ref, x_ref, mod_ref, lg_ref, lb_ref, bd_ref, w1_ref, w2_ref,
                      fg_ref, rw_ref, rb_ref, xo_ref, ho_ref, go_ref):
    y = y0_ref[...] + y1_ref[...]
    bd = bd_ref[...]
    mean = _head_sums(y, bd) * (1.0 / HEAD_DIM)
    yc = y - mean
    var = _head_sums(yc * yc, bd) * (1.0 / HEAD_DIM)
    rw = (yc * lax.rsqrt(var + RWKV_GN_EPS) * lg_ref[...] + lb_ref[...]) * gate_ref[...]
    out = _dot(fo_ref[...].astype(BF16), w1_ref[...]) + _dot(rw.astype(BF16), w2_ref[...])
    _residual_and_route(x_ref[...], out, mod_ref[0], fg_ref[...], rw_ref, rb_ref, xo_ref, ho_ref, go_ref)


def _merge_odd(y0, y1, gate, fo, x, mods, ln_g, ln_b, w_out, ffn_gain, router_wt, router_b, tm, batch, l_ctx, l_lat):
    d = x.shape[1]
    n = RWKV_DIM
    fw = FOURIER_WIDTH
    l_tot = l_ctx + l_lat
    nt = l_lat // tm
    t_out = batch * l_lat
    full = lambda a: pl.BlockSpec(a.shape, lambda b, i: (0,) * a.ndim)
    src = lambda b, i: b * (l_tot // tm) + l_ctx // tm + i
    bd = _block_ones(LANES // HEAD_DIM, HEAD_DIM)
    w1, w2 = w_out[:fw].astype(BF16), w_out[fw:].astype(BF16)
    lg, lb, fg, rb = ln_g.reshape(1, n), ln_b.reshape(1, n), ffn_gain.reshape(1, d), router_b.reshape(N_EXPERTS, 1)
    return pl.pallas_call(
        _merge_odd_kernel,
        grid=(batch, nt),
        in_specs=[pl.BlockSpec((tm, n), lambda b, i: (src(b, i), 0)),
                  pl.BlockSpec((tm, n), lambda b, i: (src(b, i), 0)),
                  pl.BlockSpec((tm, n), lambda b, i: (src(b, i), 0)),
                  pl.BlockSpec((tm, fw), lambda b, i: (i, b)),
                  pl.BlockSpec((tm, d), lambda b, i: (src(b, i), 0)),
                  pl.BlockSpec((1, 6, d), lambda b, i: (b, 0, 0)),
                  full(lg), full(lb), full(bd), full(w1), full(w2), full(fg), full(router_wt), full(rb)],
        out_specs=[pl.BlockSpec((tm, d), lambda b, i: (b * nt + i, 0)),
                   pl.BlockSpec((tm * (d // LANES), LANES), lambda b, i: (b * nt + i, 0)),
                   pl.BlockSpec((8, tm), lambda b, i: (0, b * nt + i))],
        out_shape=[jax.ShapeDtypeStruct((t_out, d), F32),
                   jax.ShapeDtypeStruct((t_out * (d // LANES), LANES), F32),
                   jax.ShapeDtypeStruct((8, t_out), F32)],
        compiler_params=_cparams(("parallel", "parallel")),
        name="merge_odd",
    )(y0, y1, gate, fo, x, mods, lg, lb, bd, w1, w2, fg, router_wt, rb)


def _rope_tables(l_ctx, l_lat):
    rows = l_lat // GRID_W
    row = jnp.repeat(jnp.arange(rows, dtype=F32), GRID_W)
    col = jnp.tile(jnp.arange(GRID_W, dtype=F32), rows)
    n_freq = HEAD_DIM // 4
    inv_freq = ROPE_THETA ** (-jnp.arange(n_freq, dtype=F32) / n_freq)
    ang = jnp.concatenate([row[:, None] * inv_freq, col[:, None] * inv_freq], axis=-1)
    cos, sin = jnp.cos(ang), jnp.sin(ang)
    cos64 = jnp.concatenate([cos, cos], axis=1)
    sin64 = jnp.concatenate([-sin, sin], axis=1)
    cos64 = jnp.concatenate([jnp.ones((l_ctx, HEAD_DIM), F32), cos64], axis=0)
    sin64 = jnp.concatenate([jnp.zeros((l_ctx, HEAD_DIM), F32), sin64], axis=0)
    return cos64, sin64


def _pad_rank(w):
    _, r, n = w.shape
    out = jnp.zeros((2, RWKV_RANK_PAD, n), w.dtype)
    out = out.at[0, 0:r].set(w[0])
    return out.at[1, r:2 * r].set(w[1])


def _even_layer(ctx, x_lat, mods, p, batch, l_ctx, l_lat, tm, mod_row):
    l_tot = l_ctx + l_lat
    nc_ctx, nc_tot = l_ctx // SEQ_CHUNK, l_tot // SEQ_CHUNK
    hk, hv = GLA_HEADS * GLA_DK, GLA_HEADS * GLA_DV
    qw, kw = ATT_HEADS * HEAD_DIM, ATT_KV_HEADS * HEAD_DIM
    w_in = p['w_in']
    o = np.cumsum([0, hk, hk, hv, hv, 2 * GLA_LOWRANK, qw, kw, kw])
    w_gla = jnp.concatenate([w_in[:, o[0]:o[4]]], axis=1).astype(BF16)
    w_dec = jnp.pad(w_in[:, o[4]:o[5]], ((0, 0), (0, 128 - 2 * GLA_LOWRANK))).astype(BF16)
    w_q = w_in[:, o[5]:o[6]].astype(BF16)
    w_kv = w_in[:, o[6]:o[8]].astype(BF16)
    cos64, sin64 = _rope_tables(l_ctx, l_lat)
    x, zg, zdec, qn, kn, vn = _project_even(ctx, x_lat, mods, p['norm_mix'], [w_gla, w_dec, w_q, w_kv], cos64, sin64,
                                            p['q_norm'], p['k_norm'], tm, mod_row, batch, l_ctx, l_lat)

    dec_w_pad = _pad_rank(p['dec_w'])
    o_f, o_b = _gla(zg, zdec, dec_w_pad, p['dec_b'].reshape(2, 1, hk), batch, nc_ctx, nc_tot)
    o_att = _attention(qn, kn, vn, p['sink'], batch, l_ctx, l_tot)

    x1, h, gates = _merge_even(o_f, o_b, zg, o_att, x, mods, p['out_norm'], p['w_out'], p['norm_ffn'],
                               p['router_wt'], p['router_b'], tm, mod_row)
    pairs = _moe(h, gates, x1, mods, p['moe_g'], p['moe_u'], p['moe_d'], p['moe_layer'], tm, mod_row, tm, combine=False)
    return pairs, x1


def _odd_layer(pairs, x_prev, mods_prev, mods, p, batch, l_ctx, l_lat, tm, mod_row, tm_moe):
    l_tot = l_ctx + l_lat
    nc_ctx, nc_tot = l_ctx // SEQ_CHUNK, l_tot // SEQ_CHUNK
    n = RWKV_DIM
    fw = FOURIER_WIDTH
    w_in = p['w_in']
    rank_w, rank_a = p['w2'].shape[1], p['a2'].shape[1]
    o = np.cumsum([0, fw, n, n, n, 2 * rank_w, 2 * rank_a])
    pad_cols = lambda w: jnp.pad(w, ((0, 0), (0, RWKV_RANK_PAD - w.shape[1])))
    w_f = w_in[:, o[0]:o[1]].astype(BF16)
    w_r = jnp.concatenate([w_in[:, o[1]:o[4]], pad_cols(w_in[:, o[4]:o[5]]), pad_cols(w_in[:, o[5]:o[6]]),
                           w_in[:, o[6]:]], axis=1).astype(BF16)
    x, (zf, zr) = _project(pairs, x_prev, mods_prev, mods, p['norm_mix'], [w_f, w_r], tm, mod_row)

    mu = p['mu']
    mu_r = jnp.concatenate([mu[0:3 * n], pad_cols(mu[None, 3 * n:3 * n + 2 * rank_w])[0],
                            pad_cols(mu[None, 3 * n + 2 * rank_w:3 * n + 2 * rank_w + 2 * rank_a])[0],
                            mu[3 * n + 2 * rank_w + 2 * rank_a:]])
    y0, y1, gate = _rwkv(zr, mu_r, p['kk_scale'], p['k_a'], p['r_k'].reshape(-1), p['w0'], _pad_rank(p['w2']),
                    p['a0'], _pad_rank(p['a2']), p['g2'], batch, nc_ctx, nc_tot)

    tables, chan = _dft_tables(l_lat)
    zc = _chan_dft(zf, chan, batch, l_ctx, l_lat, tm)
    fo = _seq_dft(tables, zc, batch, min(512, l_lat), min(1024, l_lat))

    x1, h, gates = _merge_odd(y0, y1, gate, fo, x, mods, p['ln_g'], p['ln_b'], p['w_out'], p['norm_ffn'],
                              p['router_wt'], p['router_b'], tm, batch, l_ctx, l_lat)
    lat_tiles = l_lat // tm_moe
    return _moe(h, gates, x1, mods, p['moe_g'], p['moe_u'], p['moe_d'], p['moe_layer'], tm_moe,
                lambda i: i // lat_tiles, tm)


def kernel(x, c, ctx, c_ctx, ada_w, ada_b, norm_mix, norm_ffn, even_w_in, even_w_out, gla_dec_w, gla_dec_b, gla_out_norm, att_q_norm, att_k_norm, att_sink, odd_w_in, odd_w_out, rwkv_mu, rwkv_w0, rwkv_w2, rwkv_a0, rwkv_a2, rwkv_g2, rwkv_kk_scale, rwkv_k_a, rwkv_r_k, rwkv_ln_g, rwkv_ln_b, router_w, router_b, moe_w_gate, moe_w_up, moe_w_down):
    batch, l_lat, d = x.shape
    l_ctx = ctx.shape[1]
    l_tot = l_ctx + l_lat
    assert batch < 8 and ada_w.shape[0] == 2
    tm = 256 if (l_ctx % 256 == 0 and l_lat % 256 == 0) else 128
    tm_moe = 512 if (l_lat % 512 == 0 and tm == 256) else tm
    assert l_ctx % tm == 0 and l_lat % tm == 0 and l_tot % l_ctx == 0 and l_lat % GRID_W == 0

    cc = jnp.concatenate([c, c_ctx[None, :], jnp.zeros((8 - batch - 1, d), F32)], axis=0)
    tiles_per_b = l_tot // tm
    ctx_tiles = l_ctx // tm

    def mod_row(i):
        return jnp.where(i % tiles_per_b < ctx_tiles, batch, i // tiles_per_b)

    rw_pad = jnp.pad(router_w, ((0, 0), (0, LANES - N_EXPERTS)))
    rw_hi = rw_pad.astype(BF16)
    router_wt = jnp.stack([rw_hi, (rw_pad - rw_hi.astype(F32)).astype(BF16)])
    moe = lambda layer: dict(moe_g=moe_w_gate, moe_u=moe_w_up, moe_d=moe_w_down, moe_layer=layer)

    mods_all = _modvec(cc, ada_w, ada_b)
    mods0 = mods_all[0]
    p0 = dict(w_in=even_w_in[0], w_out=even_w_out[0], dec_w=gla_dec_w[0], dec_b=gla_dec_b[0],
              out_norm=gla_out_norm[0], q_norm=att_q_norm[0], k_norm=att_k_norm[0], sink=att_sink[0],
              norm_mix=norm_mix[0], norm_ffn=norm_ffn[0], router_wt=router_wt, router_b=router_b, **moe(0))
    pairs, x1 = _even_layer(ctx.reshape(batch * l_ctx, d), x.reshape(batch * l_lat, d), mods0, p0, batch, l_ctx, l_lat,
                            tm, mod_row)

    mods1 = mods_all[1]
    p1 = dict(w_in=odd_w_in[0], w_out=odd_w_out[0], mu=rwkv_mu[0], w0=rwkv_w0[0], w2=rwkv_w2[0], a0=rwkv_a0[0],
              a2=rwkv_a2[0], g2=rwkv_g2[0], kk_scale=rwkv_kk_scale[0], k_a=rwkv_k_a[0], r_k=rwkv_r_k[0],
              ln_g=rwkv_ln_g[0], ln_b=rwkv_ln_b[0], norm_mix=norm_mix[1], norm_ffn=norm_ffn[1],
              router_wt=router_wt, router_b=router_b, **moe(1))
    out = _odd_layer(pairs, x1, mods0, mods1, p1, batch, l_ctx, l_lat, tm, mod_row, tm_moe)
    return out.reshape(batch, l_lat, d)
```

```python
import functools
import math

import jax
import jax.numpy as jnp
import numpy as np
from jax import lax
from jax.experimental import pallas as pl
from jax.experimental.pallas import tpu as pltpu

F32 = jnp.float32
BF16 = jnp.bfloat16

GRID_W = 64
HEAD_DIM = 64
NORM_EPS = 1e-6
L2_EPS = 1e-12

GLA_DV = 64
GLA_DK = 32
GLA_HEADS = 8
GLA_LOWRANK = 16
GLA_TAU = 16.0

ATT_HEADS = 8
ATT_KV_HEADS = 2
ATT_GROUP = ATT_HEADS // ATT_KV_HEADS
ATT_BLOCK = 128
ROPE_THETA = 10000.0

FOURIER_GROUP_DIM = 64
FOURIER_WIDTH = 256
DFT_SPLIT = 64

RWKV_DIM = 768
RWKV_HEADS = 12
RWKV_RANK_PAD = 128
RWKV_GN_EPS = 64e-5

N_EXPERTS = 16
N_GROUPS = 4
PER_GROUP = N_EXPERTS // N_GROUPS
D_EXPERT = 512
MOE_TILE = 256
MOE_GATHER_DEPTH = 3
LANES = 128
SUBLANES = 8

SEQ_CHUNK = 64
GLA_SCAN_BATCH = 4
RWKV_SCAN_BATCH = 2
VMEM_LIMIT = 56 * 1024 * 1024


def _cparams(sem):
    return pltpu.CompilerParams(dimension_semantics=sem, vmem_limit_bytes=VMEM_LIMIT)


def _dot(a, b):
    return jnp.dot(a, b, preferred_element_type=F32)


def _dot_nt(a, b):
    return lax.dot_general(a, b, (((1,), (1,)), ((), ())), preferred_element_type=F32)


def _dot_tn(a, b):
    return lax.dot_general(a, b, (((0,), (0,)), ((), ())), preferred_element_type=F32)


def _silu(x):
    return x * jax.nn.sigmoid(x)


def _log_sigmoid(x):
    return jnp.minimum(x, 0.0) - jnp.log(1.0 + jnp.exp(-jnp.abs(x)))


def _modulated_norm(x, gain, shift, scale):
    ms = jnp.mean(x * x, axis=-1, keepdims=True)
    return (x * lax.rsqrt(ms + NORM_EPS) * gain) * (1.0 + scale) + shift


def _block_ones(n_blocks, width):
    return jnp.kron(jnp.eye(n_blocks, dtype=F32), jnp.ones((width, width), F32))


def _head_sums(x, ones_blk):
    w = ones_blk.shape[0]
    return jnp.concatenate([_dot(x[:, i:i + w], ones_blk) for i in range(0, x.shape[1], w)], axis=1)


def _store_row_tiles(ref, x, first_row=0):
    n, w = x.shape
    g = w // LANES
    for k in range(g):
        ref[pl.ds(first_row * g + k, n, stride=g), :] = x[:, k * LANES:(k + 1) * LANES]


def _load_row_tiles(ref, n, g, first_row=0):
    return jnp.concatenate([ref[pl.ds(first_row * g + k, n, stride=g), :] for k in range(g)], axis=1)


def _modvec_kernel(c_ref, w_ref, b_ref, o_ref):
    o_ref[...] = _dot(_silu(c_ref[...]), w_ref[...]) + b_ref[...]


def _modvec(cc, w, b):
    d = cc.shape[1]
    layers, _, n = w.shape
    tn = n // 4
    out = pl.pallas_call(
        _modvec_kernel,
        grid=(layers, n // tn),
        in_specs=[pl.BlockSpec((8, d), lambda l, j: (0, 0)),
                  pl.BlockSpec((None, d, tn), lambda l, j: (l, 0, j)),
                  pl.BlockSpec((None, 1, tn), lambda l, j: (l, 0, j))],
        out_specs=pl.BlockSpec((None, 8, tn), lambda l, j: (l, 0, j)),
        out_shape=jax.ShapeDtypeStruct((layers, 8, n), F32),
        compiler_params=_cparams(("parallel", "parallel")),
        name="modvec",
    )(cc, w, b.reshape(layers, 1, n))
    return out.reshape(layers, 8, 6, d)


def _proj_kernel(ya_ref, yb_ref, x_ref, mp0_ref, mp1_ref, m0_ref, m1_ref, gain_ref, *refs, n_out):
    w_refs, x_out_ref, z_refs = refs[:n_out], refs[n_out], refs[n_out + 1:]
    n, d = x_ref.shape[0] // 2, x_ref.shape[1]
    g = d // LANES
    for half, (mp_ref, m_ref) in enumerate(((mp0_ref, m0_ref), (mp1_ref, m1_ref))):
        rows = pl.ds(half * n, n)
        y = _load_row_tiles(ya_ref, n, g, first_row=half * n) + _load_row_tiles(yb_ref, n, g, first_row=half * n)
        x = x_ref[rows, :] + mp_ref[0][5:6] * y
        x_out_ref[rows, :] = x
        m = m_ref[0]
        h = _modulated_norm(x, gain_ref[...], m[0:1], m[1:2]).astype(BF16)
        for w_ref, z_ref in zip(w_refs, z_refs):
            z_ref[rows, :] = _dot(h, w_ref[...])


def _project(pairs, x, mods_prev, mods, gain, weights, tm, mod_row):
    t, d = x.shape
    g = d // LANES
    n_out = len(weights)
    assert (t // tm) % 2 == 0
    tb = 2 * tm
    mod_spec = lambda half: pl.BlockSpec((1, 6, d), lambda i: (mod_row(2 * i + half), 0, 0))
    in_specs = [pl.BlockSpec((tb * g, LANES), lambda i: (i, 0)),
                pl.BlockSpec((tb * g, LANES), lambda i: (t // tb + i, 0)),
                pl.BlockSpec((tb, d), lambda i: (i, 0)),
                mod_spec(0), mod_spec(1), mod_spec(0), mod_spec(1),
                pl.BlockSpec((1, d), lambda i: (0, 0))]
    in_specs += [pl.BlockSpec(w.shape, lambda i: (0, 0)) for w in weights]
    outs = pl.pallas_call(
        functools.partial(_proj_kernel, n_out=n_out),
        grid=(t // tb,),
        in_specs=in_specs,
        out_specs=[pl.BlockSpec((tb, d), lambda i: (i, 0))] + [pl.BlockSpec((tb, w.shape[1]), lambda i: (i, 0)) for w in weights],
        out_shape=[jax.ShapeDtypeStruct((t, d), F32)] + [jax.ShapeDtypeStruct((t, w.shape[1]), F32) for w in weights],
        compiler_params=_cparams(("parallel",)),
        name="proj",
    )(pairs, pairs, x, mods_prev, mods_prev, mods, mods, gain.reshape(1, d), *weights)
    return outs[0], outs[1:]


def _chunk_pos(s, d, nc_ctx, nc_tot):
    back = jnp.where(s < nc_ctx, nc_ctx - 1 - s, nc_tot + nc_ctx - 1 - s)
    return jnp.where(d == 0, s, back)


def _visit_order(d, c):
    sign = 1 if d == 0 else -1
    return (lax.broadcasted_iota(jnp.int32, (c, c), 0) - lax.broadcasted_iota(jnp.int32, (c, c), 1)) * sign


def _gla_prepare(d, q, k, v, dec, dw_ref, db_ref):
    c = q.shape[0]
    g = _log_sigmoid(_dot(dec, dw_ref[d]) + db_ref[d]) / GLA_TAU
    b = _dot((_visit_order(d, c) >= 0).astype(F32), g)
    b_tot = jnp.sum(g, axis=0, keepdims=True)
    return dict(q_in=q * (GLA_DK ** -0.5) * jnp.exp(b), k_out=k * jnp.exp(-b), k_end=k * jnp.exp(b_tot - b),
                decay=jnp.exp(b_tot), v=v, sign=1 if d == 0 else -1)


def _gla_kernel(qf_ref, kf_ref, vf_ref, df_ref, qb_ref, kb_ref, vb_ref, db_ref, dw_ref, dbias_ref, of_ref, ob_ref, st_ref):
    nb, c = qf_ref.shape[0], qf_ref.shape[1]
    group = 4
    kw, vw = group * GLA_DK, group * GLA_DV

    @pl.when(pl.program_id(1) == 0)
    def _():
        st_ref[...] = jnp.zeros_like(st_ref)

    ctxs = []
    for u in range(nb):
        ctxs.append(_gla_prepare(0, qf_ref[u], kf_ref[u], vf_ref[u], df_ref[u], dw_ref, dbias_ref))
        ctxs.append(_gla_prepare(1, qb_ref[u], kb_ref[u], vb_ref[u], db_ref[u], dw_ref, dbias_ref))
    n_quads = GLA_HEADS // group
    prob = [(o, i) for o in ctxs for i in range(n_quads)]
    klane = lax.broadcasted_iota(jnp.int32, (c, kw), 1) // GLA_DK
    half = lax.broadcasted_iota(jnp.int32, (c, 2 * GLA_DV), 1) < GLA_DV
    rowi = lax.broadcasted_iota(jnp.int32, (c, group * c), 0)
    coli = lax.broadcasted_iota(jnp.int32, (c, group * c), 1) % c
    own = (lax.broadcasted_iota(jnp.int32, (vw, kw), 0) // GLA_DV) == (lax.broadcasted_iota(jnp.int32, (vw, kw), 1) // GLA_DK)

    def bd_keys(y):
        return jnp.concatenate([jnp.where(klane == h, y, 0.0) for h in range(group)], axis=0)

    def bd_vals(y):
        return jnp.concatenate([jnp.where(half, y, 0.0), jnp.where(half, 0.0, y)], axis=0)

    ksl = lambda i: slice(i * kw, (i + 1) * kw)
    vsl = lambda i: slice(i * vw, (i + 1) * vw)
    q_in = [o['q_in'][:, ksl(i)] for o, i in prob]
    att = [jnp.where((rowi - coli) * o['sign'] >= 0, _dot_nt(q_, bd_keys(o['k_out'][:, ksl(i)])), 0.0)
           for q_, (o, i) in zip(q_in, prob)]
    st = [st_ref[j] for j in range(len(prob))]
    outs = []
    for j, (o, i) in enumerate(prob):
        v = o['v'][:, vsl(i)]
        intra = jnp.concatenate([_dot(att[j][:, p * 2 * c:(p + 1) * 2 * c], bd_vals(v[:, p * 2 * GLA_DV:(p + 1) * 2 * GLA_DV]))
                                 for p in range(group // 2)], axis=1)
        outs.append(intra + _dot_nt(q_in[j], st[j]))
    for j, (o, i) in enumerate(prob):
        upd = _dot_tn(o['v'][:, vsl(i)], o['k_end'][:, ksl(i)])
        st_ref[j] = st[j] * o['decay'][:, ksl(i)] + jnp.where(own, upd, 0.0)
    for u in range(nb):
        of_ref[u] = jnp.concatenate(outs[2 * u * n_quads:(2 * u + 1) * n_quads], axis=1)
        ob_ref[u] = jnp.concatenate(outs[(2 * u + 1) * n_quads:(2 * u + 2) * n_quads], axis=1)


def _gla(zg, zdec, dec_w_pad, dec_b, batch, nc_ctx, nc_tot):
    t = zg.shape[0]
    c = SEQ_CHUNK
    hk, hv = GLA_HEADS * GLA_DK, GLA_HEADS * GLA_DV
    nb = math.gcd(batch, GLA_SCAN_BATCH)
    zg3, zdec3 = zg.reshape(batch, t // batch, -1), zdec.reshape(batch, t // batch, -1)

    def specs(d):
        pos = lambda s: _chunk_pos(s, d, nc_ctx, nc_tot)
        return [pl.BlockSpec((nb, c, hk), lambda b, s: (b, pos(s), 0)),
                pl.BlockSpec((nb, c, hk), lambda b, s: (b, pos(s), 1)),
                pl.BlockSpec((nb, c, hv), lambda b, s: (b, pos(s), 1)),
                pl.BlockSpec((nb, c, 128), lambda b, s: (b, pos(s), 0))], pl.BlockSpec((nb, c, hv), lambda b, s: (b, pos(s), 0))

    in_f, out_f = specs(0)
    in_b, out_b = specs(1)
    o_f, o_b = pl.pallas_call(
        _gla_kernel,
        grid=(batch // nb, nc_tot),
        in_specs=in_f + in_b + [pl.BlockSpec(dec_w_pad.shape, lambda b, s: (0, 0, 0)),
                                pl.BlockSpec(dec_b.shape, lambda b, s: (0, 0, 0))],
        out_specs=[out_f, out_b],
        out_shape=[jax.ShapeDtypeStruct((batch, t // batch, hv), F32)] * 2,
        scratch_shapes=[pltpu.VMEM((nb * 2 * GLA_HEADS // 4, 4 * GLA_DV, 4 * GLA_DK), F32)],
        compiler_params=_cparams(("parallel", "arbitrary")),
        name="gla_scan",
    )(zg3, zg3, zg3, zdec3, zg3, zg3, zg3, zdec3, dec_w_pad, dec_b)
    return o_f.reshape(t, hv), o_b.reshape(t, hv)


def _rope_swap(x):
    n = x.shape[-1]
    lane = lax.broadcasted_iota(jnp.int32, x.shape, x.ndim - 1)
    half = HEAD_DIM // 2
    return jnp.where(lane % HEAD_DIM < half, pltpu.roll(x, n - half, x.ndim - 1), pltpu.roll(x, half, x.ndim - 1))


def _proj_even_kernel(ctx_ref, x_ref, mod_ref, gain_ref, wg_ref, wd_ref, wq_ref, wkv_ref, cos_ref, sin_ref, qg_ref, kg_ref,
                      bdq_ref, bdk_ref, xs_ref, zg_ref, zdec_ref, qo_ref, ko_ref, vo_ref, *, tiles_per_b, ctx_tiles):
    is_ctx = lax.rem(pl.program_id(0), tiles_per_b) < ctx_tiles
    x = jnp.where(is_ctx, ctx_ref[...], x_ref[...])
    xs_ref[...] = x
    m = mod_ref[0]
    h = _modulated_norm(x, gain_ref[...], m[0:1], m[1:2]).astype(BF16)
    zg_ref[...] = _dot(h, wg_ref[...])
    zdec_ref[...] = _dot(h, wd_ref[...])

    def norm_rope(z, gain, bd, n_heads):
        ms = _head_sums(z * z, bd) * (1.0 / HEAD_DIM)
        zn = z * lax.rsqrt(ms + NORM_EPS) * gain
        cos = jnp.concatenate([cos_ref[...]] * n_heads, axis=1)
        sin = jnp.concatenate([sin_ref[...]] * n_heads, axis=1)
        return zn * cos + _rope_swap(zn) * sin

    q = norm_rope(_dot(h, wq_ref[...]), qg_ref[...], bdq_ref[...], ATT_HEADS)
    qo_ref[...] = (q * (HEAD_DIM ** -0.5)).astype(BF16)
    kw = ATT_KV_HEADS * HEAD_DIM
    kv = _dot(h, wkv_ref[...])
    ko_ref[...] = norm_rope(kv[:, :kw], kg_ref[...], bdk_ref[...], ATT_KV_HEADS).astype(BF16)
    vo_ref[...] = kv[:, kw:].astype(BF16)


def _project_even(ctx, x, mods, gain, weights, cos64, sin64, q_gain, k_gain, tm, mod_row, batch, l_ctx, l_lat):
    d = x.shape[1]
    l_tot = l_ctx + l_lat
    t = batch * l_tot
    tiles_per_b, ctx_tiles, lat_tiles = l_tot // tm, l_ctx // tm, l_lat // tm
    qw, kw = ATT_HEADS * HEAD_DIM, ATT_KV_HEADS * HEAD_DIM
    w_gla, w_dec, w_q, w_kv = weights
    full = lambda a: pl.BlockSpec(a.shape, lambda i: (0,) * a.ndim)
    qg = jnp.tile(q_gain, ATT_HEADS).reshape(1, qw)
    kg = jnp.tile(k_gain, ATT_KV_HEADS).reshape(1, kw)
    bdq = bdk = _block_ones(LANES // HEAD_DIM, HEAD_DIM)
    gain = gain.reshape(1, d)
    row = lambda n: pl.BlockSpec((tm, n), lambda i: (i, 0))
    pos = pl.BlockSpec((tm, HEAD_DIM), lambda i: (i % tiles_per_b, 0))
    return pl.pallas_call(
        functools.partial(_proj_even_kernel, tiles_per_b=tiles_per_b, ctx_tiles=ctx_tiles),
        grid=(t // tm,),
        in_specs=[pl.BlockSpec((tm, d), lambda i: ((i // tiles_per_b) * ctx_tiles + jnp.minimum(i % tiles_per_b, ctx_tiles - 1), 0)),
                  pl.BlockSpec((tm, d), lambda i: ((i // tiles_per_b) * lat_tiles + jnp.maximum(i % tiles_per_b - ctx_tiles, 0), 0)),
                  pl.BlockSpec((1, 6, d), lambda i: (mod_row(i), 0, 0)),
                  full(gain), full(w_gla), full(w_dec), full(w_q), full(w_kv), pos, pos,
                  full(qg), full(kg), full(bdq), full(bdk)],
        out_specs=[row(d), row(w_gla.shape[1]), row(w_dec.shape[1]), row(qw), row(kw), row(kw)],
        out_shape=[jax.ShapeDtypeStruct((t, d), F32), jax.ShapeDtypeStruct((t, w_gla.shape[1]), F32),
                   jax.ShapeDtypeStruct((t, w_dec.shape[1]), F32), jax.ShapeDtypeStruct((t, qw), BF16),
                   jax.ShapeDtypeStruct((t, kw), BF16), jax.ShapeDtypeStruct((t, kw), BF16)],
        compiler_params=_cparams(("parallel",)),
        name="proj_even",
    )(ctx, x, mods, gain, w_gla, w_dec, w_q, w_kv, cos64, sin64, qg, kg, bdq, bdk)


def _attn_kernel(q_ref, kp_ref, kc_ref, kn_ref, kx_ref, vp_ref, vc_ref, vn_ref, vx_ref, sink_ref, o_ref,
                 *, n_ctx_blocks, n_lat_blocks):
    blk = ATT_BLOCK
    n = pl.program_id(1)
    m = n - n_ctx_blocks
    is_lat = n >= n_ctx_blocks
    l_ctx = kx_ref.shape[0]
    width = 3 * blk + l_ctx
    rows = ATT_GROUP * blk
    r = lax.broadcasted_iota(jnp.int32, (rows, width), 0) % blk
    c = lax.broadcasted_iota(jnp.int32, (rows, width), 1)
    lat = is_lat.astype(jnp.int32)
    has_prev = lat * (m >= 1).astype(jnp.int32)
    has_next = lat * (m <= n_lat_blocks - 2).astype(jnp.int32)
    valid = jnp.where(c < blk, (c >= r).astype(jnp.int32) * has_prev,
                      jnp.where(c < 2 * blk, lat,
                                jnp.where(c < 3 * blk, (c - 2 * blk <= r).astype(jnp.int32) * has_next, 1))) > 0
    q = q_ref[...]
    sink = sink_ref[...]
    groups = range(ATT_KV_HEADS)
    ks = [slice(kvh * HEAD_DIM, (kvh + 1) * HEAD_DIM) for kvh in groups]
    heads = [range(kvh * ATT_GROUP, (kvh + 1) * ATT_GROUP) for kvh in groups]
    kw = [jnp.concatenate([kp_ref[:, s_], kc_ref[:, s_], kn_ref[:, s_], kx_ref[:, s_]], axis=0) for s_ in ks]
    vw = [jnp.concatenate([vp_ref[:, s_], vc_ref[:, s_], vn_ref[:, s_], vx_ref[:, s_]], axis=0) for s_ in ks]
    qg = [jnp.concatenate([q[:, h * HEAD_DIM:(h + 1) * HEAD_DIM] for h in hs], axis=0) for hs in heads]
    sk = [jnp.concatenate([jnp.broadcast_to(sink[h:h + 1, 0:1], (blk, 1)) for h in hs], axis=0) for hs in heads]
    s = [jnp.where(valid, _dot_nt(a_, b_), -jnp.inf) for a_, b_ in zip(qg, kw)]
    mx = [jnp.maximum(jnp.max(a_, axis=-1, keepdims=True), b_) for a_, b_ in zip(s, sk)]
    p = [jnp.exp(a_ - b_) for a_, b_ in zip(s, mx)]
    denom = [jnp.sum(a_, axis=-1, keepdims=True) + jnp.exp(b_ - c_) for a_, b_, c_ in zip(p, sk, mx)]
    o = [_dot(a_.astype(BF16), b_) / c_ for a_, b_, c_ in zip(p, vw, denom)]
    outs = [o[kvh][g * blk:(g + 1) * blk] for kvh in groups for g in range(ATT_GROUP)]
    o_ref[...] = jnp.concatenate(outs, axis=1)


def _attention(qn, kn, vn, sink, batch, l_ctx, l_tot):
    t = qn.shape[0]
    blk = ATT_BLOCK
    nq = l_tot // blk
    nc = l_ctx // blk
    nl = nq - nc
    qw, kw = ATT_HEADS * HEAD_DIM, ATT_KV_HEADS * HEAD_DIM

    def win(off):
        def index(b, n):
            m = jnp.clip(n - nc + off, 0, nl - 1)
            return (b * nq + nc + m, 0)
        return pl.BlockSpec((blk, kw), index)

    ctx_spec = pl.BlockSpec((l_ctx, kw), lambda b, n: (b * (l_tot // l_ctx), 0))
    return pl.pallas_call(
        functools.partial(_attn_kernel, n_ctx_blocks=nc, n_lat_blocks=nl),
        grid=(batch, nq),
        in_specs=[pl.BlockSpec((blk, qw), lambda b, n: (b * nq + n, 0)),
                  win(-1), win(0), win(1), ctx_spec,
                  win(-1), win(0), win(1), ctx_spec,
                  pl.BlockSpec((ATT_HEADS, 128), lambda b, n: (0, 0))],
        out_specs=pl.BlockSpec((blk, qw), lambda b, n: (b * nq + n, 0)),
        out_shape=jax.ShapeDtypeStruct((t, qw), F32),
        compiler_params=_cparams(("parallel", "parallel")),
        name="window_attention",
    )(qn, kn, kn, kn, kn, vn, vn, vn, vn, jnp.broadcast_to(sink.astype(F32)[:, None], (ATT_HEADS, 128)))


def _route(logits_t, bias_col):
    scores = jax.nn.sigmoid(logits_t)
    sel = scores + bias_col
    rows = [sel[e:e + 1] for e in range(N_EXPERTS)]
    grp = []
    for g in range(N_GROUPS):
        r = rows[g * PER_GROUP:(g + 1) * PER_GROUP]
        best = None
        for i in range(PER_GROUP):
            for j in range(i + 1, PER_GROUP):
                pair = r[i] + r[j]
                best = pair if best is None else jnp.maximum(best, pair)
        grp.append(best)
    g_best = jnp.zeros_like(grp[0], dtype=jnp.int32)
    g_val = grp[0]
    for g in range(1, N_GROUPS):
        take = grp[g] > g_val
        g_best = jnp.where(take, g, g_best)
        g_val = jnp.where(take, grp[g], g_val)
    neg = -jnp.inf
    masked = [jnp.where(g_best == e // PER_GROUP, rows[e], neg) for e in range(N_EXPERTS)]
    i1 = jnp.zeros_like(g_best)
    v1 = masked[0]
    for e in range(1, N_EXPERTS):
        take = masked[e] > v1
        i1 = jnp.where(take, e, i1)
        v1 = jnp.where(take, masked[e], v1)
    i2 = jnp.full_like(g_best, -1)
    v2 = jnp.full_like(v1, neg)
    for e in range(N_EXPERTS):
        take = jnp.logical_and(i1 != e, masked[e] > v2)
        i2 = jnp.where(take, e, i2)
        v2 = jnp.where(take, masked[e], v2)
    w1 = jnp.zeros_like(v1)
    w2 = jnp.zeros_like(v1)
    for e in range(N_EXPERTS):
        w1 = jnp.where(i1 == e, scores[e:e + 1], w1)
        w2 = jnp.where(i2 == e, scores[e:e + 1], w2)
    inv = 1.0 / (w1 + w2)
    pad = jnp.zeros_like(w1)
    lane = lax.broadcasted_iota(jnp.int32, w1.shape, 1)
    hist = pad
    for e in range(N_EXPERTS):
        n_e = jnp.sum((i1 == e).astype(F32) + (i2 == e).astype(F32), axis=1, keepdims=True)
        hist = jnp.where(lane == e, n_e, hist)
    return jnp.concatenate([i1.astype(F32), i2.astype(F32), w1 * inv, w2 * inv, hist, pad, pad, pad], axis=0)


def _residual_and_route(x, out, m, ffn_gain, rw_ref, rb_ref, x_ref, h_ref, g_ref, first_row=0):
    n = x.shape[0]
    x1 = x + m[2:3] * out
    x_ref[pl.ds(first_row, n), :] = x1
    h = _modulated_norm(x1, ffn_gain, m[3:4], m[4:5])
    _store_row_tiles(h_ref, h, first_row=first_row)
    h_hi = h.astype(BF16)
    h_lo = (h - h_hi.astype(F32)).astype(BF16)
    logits = _dot(h_hi, rw_ref[0]) + _dot(h_lo, rw_ref[0]) + _dot(h_hi, rw_ref[1])
    g_ref[:, pl.ds(first_row, n)] = _route(logits.T[:N_EXPERTS], rb_ref[...])


def _merge_even_kernel(o0_ref, o1_ref, gg_ref, oa_ref, x_ref, mod0_ref, mod1_ref, gn_ref, bd_ref, w1_ref, w2_ref,
                       fg_ref, rw_ref, rb_ref, xo_ref, ho_ref, go_ref):
    n = x_ref.shape[0] // 2
    for half, mod_ref in enumerate((mod0_ref, mod1_ref)):
        rows = pl.ds(half * n, n)
        og = o0_ref[rows, :] + o1_ref[rows, :]
        ms = _head_sums(og * og, bd_ref[...]) * (1.0 / GLA_DV)
        g = og * lax.rsqrt(ms + NORM_EPS) * gn_ref[...] * _silu(gg_ref[rows, :])
        out = _dot(g.astype(BF16), w1_ref[...]) + _dot(oa_ref[rows, :].astype(BF16), w2_ref[...])
        _residual_and_route(x_ref[rows, :], out, mod_ref[0], fg_ref[...], rw_ref, rb_ref, xo_ref, ho_ref, go_ref,
                            first_row=half * n)


def _merge_even(o_f, o_b, zg, o_att, x, mods, out_norm, w_out, ffn_gain, router_wt, router_b, tm, mod_row):
    t, d = x.shape
    hv = GLA_HEADS * GLA_DV
    qw = ATT_HEADS * HEAD_DIM
    full = lambda a: pl.BlockSpec(a.shape, lambda i: (0,) * a.ndim)
    gn = jnp.tile(out_norm, GLA_HEADS).reshape(1, hv)
    bd = _block_ones(LANES // GLA_DV, GLA_DV)
    w1, w2 = w_out[:hv].astype(BF16), w_out[hv:].astype(BF16)
    fg = ffn_gain.reshape(1, d)
    rb = router_b.reshape(N_EXPERTS, 1)
    assert (t // tm) % 2 == 0
    tb = 2 * tm
    mod_spec = lambda half: pl.BlockSpec((1, 6, d), lambda i: (mod_row(2 * i + half), 0, 0))
    return pl.pallas_call(
        _merge_even_kernel,
        grid=(t // tb,),
        in_specs=[pl.BlockSpec((tb, hv), lambda i: (i, 0)),
                  pl.BlockSpec((tb, hv), lambda i: (i, 0)),
                  pl.BlockSpec((tb, hv), lambda i: (i, 2)),
                  pl.BlockSpec((tb, qw), lambda i: (i, 0)),
                  pl.BlockSpec((tb, d), lambda i: (i, 0)),
                  mod_spec(0), mod_spec(1),
                  full(gn), full(bd), full(w1), full(w2), full(fg), full(router_wt), full(rb)],
        out_specs=[pl.BlockSpec((tb, d), lambda i: (i, 0)),
                   pl.BlockSpec((tb * (d // LANES), LANES), lambda i: (i, 0)),
                   pl.BlockSpec((8, tb), lambda i: (0, i))],
        out_shape=[jax.ShapeDtypeStruct((t, d), F32),
                   jax.ShapeDtypeStruct((t * (d // LANES), LANES), F32),
                   jax.ShapeDtypeStruct((8, t), F32)],
        compiler_params=_cparams(("parallel",)),
        name="merge_even",
    )(o_f, o_b, zg, o_att, x, mods, mods, gn, bd, w1, w2, fg, router_wt, rb)


def _moe_plan(route, t, rows, route_tile):
    n_tiles = 2 * t // rows + N_EXPERTS
    eid = jnp.concatenate([route[0], route[1]]).astype(jnp.int32)
    slot = jnp.arange(2 * t, dtype=jnp.int32)
    gate = jnp.concatenate([route[2], route[3]])
    _, s_slot, s_gate = lax.sort((eid, slot, gate), num_keys=1, is_stable=True)
    counts = jnp.sum(route[4].reshape(t // route_tile, route_tile)[:, :N_EXPERTS], axis=0).astype(jnp.int32)
    padded = (counts + rows - 1) // rows * rows
    p_end = jnp.cumsum(padded)
    p_start = p_end - padded
    c_start = jnp.cumsum(counts) - counts
    tile_start = jnp.arange(n_tiles, dtype=jnp.int32) * rows
    tile_e = jnp.minimum(jnp.sum((tile_start[:, None] >= p_end[None, :]).astype(jnp.int32), axis=1), N_EXPERTS - 1)
    r_in = (tile_start - p_start[tile_e])[:, None] + jnp.arange(rows, dtype=jnp.int32)[None, :]
    over = r_in - counts[tile_e][:, None]
    valid = over < 0
    src = jnp.clip(c_start[tile_e][:, None] + r_in, 0, 2 * t - 1)
    g_slot = s_slot[src]
    row_tok = jnp.where(valid, jnp.where(g_slot >= t, g_slot - t, g_slot), 0)
    row_gate = jnp.where(valid, s_gate[src], 0.0)
    row_dst = jnp.where(valid, g_slot, 2 * t + tile_e[:, None] * rows + jnp.clip(over, 0, rows - 1))
    n_used = (p_end[-1] // rows).astype(jnp.int32)
    tile_e = jnp.where(tile_start < p_end[-1], tile_e, tile_e[jnp.maximum(n_used - 1, 0)])
    return (row_tok[:, None, :], row_gate[:, None, :], row_dst[:, None, :], tile_e, n_used.reshape(1))


def _moe_experts_kernel(te_ref, nu_ref, tok0_ref, tok1_ref, tokn_ref, dst_ref, gate_ref, wg_ref, wu_ref, wd_ref, h_hbm,
                        y_hbm, hbuf, ybuf, wgb, wub, wdb, sem_g, sem_s):
    j = pl.program_id(0)
    n_used = nu_ref[0]
    slot = j % 2
    g = SUBLANES
    rows = ybuf.shape[0] // (2 * g)
    tile = rows * g
    depth = hbuf.shape[0] // tile
    gslot = lax.rem(j, depth)

    def row(ref, i):
        return ref.at[pl.ds(pl.multiple_of(i * g, g), g)]

    def start_gather(idx_ref, s):
        for r in range(rows):
            pltpu.make_async_copy(row(h_hbm, idx_ref[0, 0, r]), row(hbuf, s * rows + r), sem_g.at[s]).start(priority=r % 2)

    def wait_gather(s):
        pltpu.make_async_copy(h_hbm.at[pl.ds(0, tile)], hbuf.at[pl.ds(pl.multiple_of(s * tile, tile), tile)],
                              sem_g.at[s]).wait()

    def wait_scatter(s):
        pltpu.make_async_copy(ybuf.at[pl.ds(pl.multiple_of(s * tile, tile), tile)], y_hbm.at[pl.ds(0, tile)],
                              sem_s.at[s]).wait()

    @pl.when(j == 0)
    def _():
        start_gather(tok0_ref, 0)
        start_gather(tok1_ref, 1)
        ybuf[...] = jnp.zeros(ybuf.shape, F32)
        n_real = y_hbm.shape[0] // g - N_EXPERTS * rows
        fills = [pltpu.make_async_copy(ybuf.at[pl.ds((k % 2) * tile, tile)],
                                       y_hbm.at[pl.ds((n_real + k * rows) * g, tile)], sem_s.at[k % 2])
                 for k in range(N_EXPERTS)]
        for f in fills:
            f.start()
        for f in fills[2:]:
            f.wait()

    active = j < n_used
    changed = jnp.logical_or(j == 0, te_ref[j] != te_ref[jnp.maximum(j - 1, 0)])

    @pl.when(jnp.logical_and(active, changed))
    def _():
        wgb[...] = wg_ref[0].astype(BF16)
        wub[...] = wu_ref[0].astype(BF16)
        wdb[...] = wd_ref[0].astype(BF16)

    @pl.when(active)
    def _():
        wait_gather(gslot)
        start_gather(tokn_ref, lax.rem(j + depth - 1, depth))
        h = _load_row_tiles(hbuf, rows, g, first_row=gslot * rows).astype(BF16)
        act = _silu(_dot(h, wgb[...])) * _dot(h, wub[...])
        ri = lax.broadcasted_iota(jnp.int32, (rows, rows), 0)
        ci = lax.broadcasted_iota(jnp.int32, (rows, rows), 1)
        gate = jnp.sum(jnp.where(ri == ci, gate_ref[0], 0.0), axis=1, keepdims=True)
        y = _dot((act * gate).astype(BF16), wdb[...])
        wait_scatter(slot)
        _store_row_tiles(ybuf, y, first_row=slot * rows)
        for r in range(rows):
            pltpu.make_async_copy(row(ybuf, slot * rows + r), row(y_hbm, dst_ref[0, 0, r]), sem_s.at[slot]).start(priority=r % 2)

        @pl.when(j == n_used - 1)
        def _():
            wait_scatter(slot)
            wait_scatter(1 - slot)
            for ahead in range(1, depth):
                wait_gather(lax.rem(j + ahead, depth))


def _moe_combine_kernel(ya_ref, yb_ref, x_ref, mod_ref, o_ref):
    n, d = x_ref.shape
    y = _load_row_tiles(ya_ref, n, d // LANES) + _load_row_tiles(yb_ref, n, d // LANES)
    o_ref[...] = x_ref[...] + mod_ref[0][5:6] * y


def _moe(h, route, x, mods, w_gate, w_up, w_down, layer, tm, mod_row, route_tile, combine=True):
    t, d = x.shape
    g = d // LANES
    assert g == SUBLANES
    rows = MOE_TILE
    row_tok, row_gate, row_dst, tile_e, n_used = _moe_plan(route, t, rows, route_tile)
    n_tiles = row_tok.shape[0]
    n_pair_rows = 2 * t + N_EXPERTS * rows
    assert MOE_GATHER_DEPTH == 3
    smem_tile = lambda which: pl.BlockSpec((1, 1, rows), lambda j, te, nu: (jnp.minimum(which(j), nu[0] - 1), 0, 0),
                                           memory_space=pltpu.SMEM)
    pairs = pl.pallas_call(
        _moe_experts_kernel,
        grid_spec=pltpu.PrefetchScalarGridSpec(
            num_scalar_prefetch=2,
            grid=(n_tiles,),
            in_specs=[smem_tile(lambda j: 0), smem_tile(lambda j: 1), smem_tile(lambda j: j + MOE_GATHER_DEPTH - 1),
                      smem_tile(lambda j: j),
                      pl.BlockSpec((1, 1, rows), lambda j, te, nu: (j, 0, 0)),
                      pl.BlockSpec((None, 1, d, D_EXPERT), lambda j, te, nu: (layer, te[j], 0, 0)),
                      pl.BlockSpec((None, 1, d, D_EXPERT), lambda j, te, nu: (layer, te[j], 0, 0)),
                      pl.BlockSpec((None, 1, D_EXPERT, d), lambda j, te, nu: (layer, te[j], 0, 0)),
                      pl.BlockSpec(memory_space=pl.ANY)],
            out_specs=pl.BlockSpec(memory_space=pl.ANY),
            scratch_shapes=[pltpu.VMEM((MOE_GATHER_DEPTH * rows * g, LANES), F32), pltpu.VMEM((2 * rows * g, LANES), F32),
                            pltpu.VMEM((d, D_EXPERT), BF16), pltpu.VMEM((d, D_EXPERT), BF16),
                            pltpu.VMEM((D_EXPERT, d), BF16),
                            pltpu.SemaphoreType.DMA((MOE_GATHER_DEPTH,)), pltpu.SemaphoreType.DMA((2,))]),
        out_shape=jax.ShapeDtypeStruct((n_pair_rows * g, LANES), F32),
        compiler_params=_cparams(("arbitrary",)),
        name="moe_experts",
    )(tile_e, n_used, row_tok, row_tok, row_tok, row_dst, row_gate, w_gate, w_up, w_down, h)
    if not combine:
        return pairs
    return pl.pallas_call(
        _moe_combine_kernel,
        grid=(t // tm,),
        in_specs=[pl.BlockSpec((tm * g, LANES), lambda i: (i, 0)),
                  pl.BlockSpec((tm * g, LANES), lambda i: (t // tm + i, 0)),
                  pl.BlockSpec((tm, d), lambda i: (i, 0)),
                  pl.BlockSpec((1, 6, d), lambda i: (mod_row(i), 0, 0))],
        out_specs=pl.BlockSpec((tm, d), lambda i: (i, 0)),
        out_shape=jax.ShapeDtypeStruct((t, d), F32),
        compiler_params=_cparams(("parallel",)),
        name="moe_combine",
    )(pairs, pairs, x, mods)


def _chan_dft_kernel(z_ref, w_ref, o_ref):
    res = _dot(z_ref[...].astype(BF16), w_ref[...]).astype(BF16)
    o_ref[0] = res[:, :FOURIER_WIDTH]
    o_ref[1] = res[:, FOURIER_WIDTH:]


def _chan_dft(zf, w, batch, l_ctx, l_lat, tm):
    l_tot = l_ctx + l_lat
    nt = l_lat // tm
    fw = FOURIER_WIDTH
    return pl.pallas_call(
        _chan_dft_kernel,
        grid=(batch, nt),
        in_specs=[pl.BlockSpec((tm, fw), lambda b, i: (b * (l_tot // tm) + l_ctx // tm + i, 0)),
                  pl.BlockSpec(w.shape, lambda b, i: (0, 0))],
        out_specs=pl.BlockSpec((2, tm, fw), lambda b, i: (0, i, b)),
        out_shape=jax.ShapeDtypeStruct((2, l_lat, batch * fw), BF16),
        compiler_params=_cparams(("parallel", "parallel")),
        name="fourier_channels",
    )(zf, w)


def _seq_dft_kernel(ca_ref, sa_ref, cb_ref, sb_ref, z_ref, o_ref, acc_ref):
    k = pl.program_id(1)
    tk = z_ref.shape[1]
    sub = DFT_SPLIT
    n_a = ca_ref.shape[1]

    @pl.when(k == 0)
    def _():
        acc_ref[...] = jnp.zeros_like(acc_ref)

    col_a = k * tk + lax.broadcasted_iota(jnp.int32, (n_a, tk), 1)
    pick_a = (col_a // sub == lax.broadcasted_iota(jnp.int32, (n_a, tk), 0)).astype(F32)
    pick_b = (lax.broadcasted_iota(jnp.int32, (sub, tk), 1) % sub == lax.broadcasted_iota(jnp.int32, (sub, tk), 0)).astype(F32)
    ca, sa = _dot(ca_ref[...], pick_a), _dot(sa_ref[...], pick_a)
    cb, sb = _dot(cb_ref[...], pick_b), _dot(sb_ref[...], pick_b)
    cos_t = (ca * cb - sa * sb).astype(BF16)
    sin_t = (sa * cb + ca * sb).astype(BF16)
    acc_ref[...] += _dot(cos_t, z_ref[0]) + _dot(sin_t, z_ref[1])

    @pl.when(k == pl.num_programs(1) - 1)
    def _():
        o_ref[...] = acc_ref[...]


def _seq_dft(tables, zc, batch, tm, tk):
    ca, sa, cb, sb = tables
    l = ca.shape[0]
    fw = FOURIER_WIDTH
    small = lambda a: pl.BlockSpec((tm, a.shape[1]), lambda i, k: (i, 0))
    return pl.pallas_call(
        _seq_dft_kernel,
        grid=(l // tm, l // tk),
        in_specs=[small(ca), small(sa), small(cb), small(sb),
                  pl.BlockSpec((2, tk, batch * fw), lambda i, k: (0, k, 0))],
        out_specs=pl.BlockSpec((tm, batch * fw), lambda i, k: (i, 0)),
        out_shape=jax.ShapeDtypeStruct((l, batch * fw), F32),
        scratch_shapes=[pltpu.VMEM((tm, batch * fw), F32)],
        compiler_params=_cparams(("parallel", "arbitrary")),
        name="fourier_sequence",
    )(ca, sa, cb, sb, zc)


def _dft_tables(l):
    m = jnp.arange(l, dtype=jnp.int32)[:, None]
    n1 = l // DFT_SPLIT
    a = (m * (jnp.arange(n1, dtype=jnp.int32)[None, :] * DFT_SPLIT)) % l
    b = (m * jnp.arange(DFT_SPLIT, dtype=jnp.int32)[None, :]) % l
    wa = a.astype(F32) * (2.0 * np.pi / l)
    wb = b.astype(F32) * (2.0 * np.pi / l)
    tables = (jnp.cos(wa), jnp.sin(wa), jnp.cos(wb), jnp.sin(wb))
    gd = FOURIER_GROUP_DIM
    cc = (jnp.arange(gd, dtype=jnp.int32)[:, None] * jnp.arange(gd, dtype=jnp.int32)[None, :]) % gd
    wc = cc.astype(F32) * (2.0 * np.pi / gd)
    scale = 1.0 / np.sqrt(float(l) * gd)
    eye = jnp.eye(FOURIER_WIDTH // gd, dtype=F32)
    chan = jnp.concatenate([jnp.kron(eye, jnp.cos(wc)), -jnp.kron(eye, jnp.sin(wc))], axis=1) * scale
    return tables, chan.astype(BF16)


def _rwkv_prepare(d, pos, z, z_before, z_after, par, nc_ctx, nc_tot):
    mu, kks, ka, rk, w0, w2, a0, a2, bd = par
    c = z.shape[0]
    n = RWKV_DIM
    seg_first = jnp.logical_or(pos == 0, pos == nc_ctx)
    seg_last = jnp.logical_or(pos == nc_ctx - 1, pos == nc_tot - 1)
    row = lax.broadcasted_iota(jnp.int32, z.shape, 0)
    prev_row = jnp.where(seg_first, 0.0, z_before[7:8, :])
    next_row = jnp.where(seg_last, 0.0, z_after[0:1, :])
    z_prev = jnp.where(row == 0, prev_row, pltpu.roll(z, 1, 0))
    z_next = jnp.where(row == c - 1, next_row, pltpu.roll(z, c - 1, 0))
    zs = z + mu * (0.5 * (z_prev + z_next) - z)

    r, k, v = zs[:, 0:n], zs[:, n:2 * n], zs[:, 2 * n:3 * n]
    zw = zs[:, 3 * n:3 * n + RWKV_RANK_PAD]
    za = zs[:, 3 * n + RWKV_RANK_PAD:3 * n + 2 * RWKV_RANK_PAD]
    zg = zs[:, 3 * n + 2 * RWKV_RANK_PAD:3 * n + 3 * RWKV_RANK_PAD]

    kk = k * kks
    kk = kk * lax.rsqrt(_head_sums(kk * kk, bd) + L2_EPS)
    w_log = _log_sigmoid(w0[d] + _dot(jnp.tanh(zw), w2[d])) - 0.5
    lw = -jnp.exp(w_log)
    a = jax.nn.sigmoid(a0[d] + _dot(za, a2[d]))
    kd = k * (1.0 + (a - 1.0) * ka)
    beta = kk * a

    cl = _dot((_visit_order(d, c) >= 0).astype(F32), lw)
    c_tot = jnp.sum(lw, axis=0, keepdims=True)
    grow = jnp.exp(-cl)
    tail = jnp.exp(c_tot - cl)
    ops = dict(k_s=kd * grow,
               b_s=beta * grow,
               kap_s=kk * jnp.exp(cl - lw),
               r_s=r * jnp.exp(cl),
               k_e=kd * tail,
               b_e=beta * tail,
               gam=jnp.exp(c_tot), v=v,
               bonus=_head_sums(r * kd * rk, bd),
               sign=1 if d == 0 else -1)
    return ops, zg


def _rwkv_kernel(zf_ref, zfp_ref, zfn_ref, zb_ref, zbp_ref, zbn_ref, mu_ref, kks_ref, ka_ref, rk_ref, w0_ref, w2_ref,
                 a0_ref, a2_ref, g2_ref, bd_ref, yf_ref, yb_ref, gate_ref, st_ref, *, nc_ctx, nc_tot):
    s = pl.program_id(1)
    nb, c = zf_ref.shape[0], zf_ref.shape[1]
    hd = HEAD_DIM
    pw = 2 * hd

    @pl.when(s == 0)
    def _():
        st_ref[...] = jnp.zeros_like(st_ref)

    par = (mu_ref[...], kks_ref[...], ka_ref[...], rk_ref[...], w0_ref, w2_ref, a0_ref, a2_ref, bd_ref[...])
    ctxs = []
    for b in range(nb):
        fwd, zg = _rwkv_prepare(0, _chunk_pos(s, 0, nc_ctx, nc_tot), zf_ref[b], zfp_ref[b], zfn_ref[b], par, nc_ctx, nc_tot)
        bwd, _ = _rwkv_prepare(1, _chunk_pos(s, 1, nc_ctx, nc_tot), zb_ref[b], zbp_ref[b], zbn_ref[b], par, nc_ctx, nc_tot)
        gate_ref[b] = _dot(jax.nn.sigmoid(zg), g2_ref[...])
        ctxs += [fwd, bwd]

    lane = lax.broadcasted_iota(jnp.int32, (c, pw), 1)
    rowi = lax.broadcasted_iota(jnp.int32, (c, pw), 0)
    left = lane < hd
    eye_p = (rowi == lane % hd).astype(F32)
    same_head = (lax.broadcasted_iota(jnp.int32, (pw, pw), 0) < hd) == (lax.broadcasted_iota(jnp.int32, (pw, pw), 1) < hd)

    def bd(y):
        return jnp.concatenate([jnp.where(left, y, 0.0), jnp.where(left, 0.0, y)], axis=0)

    n_pairs = RWKV_HEADS // 2
    prob = [(o, slice(i * pw, (i + 1) * pw)) for o in ctxs for i in range(n_pairs)]
    ahead = [(rowi - lane % hd) * o['sign'] for o, _ in prob]
    get = lambda name: [o[name][:, s_] for o, s_ in prob]
    kap, r_s, k_s, b_s, k_e, b_e, vp, gam, bonus = (get(x) for x in ('kap_s', 'r_s', 'k_s', 'b_s', 'k_e', 'b_e', 'v', 'gam', 'bonus'))
    p = [_dot_nt(jnp.concatenate([a_, b_], axis=0), jnp.concatenate([bd(c_), bd(d_)], axis=0))
         for a_, b_, c_, d_ in zip(kap, r_s, k_s, b_s)]
    m1 = [jnp.where(h_ > 0, x[0:c, 0:pw], 0.0) for x, h_ in zip(p, ahead)]
    m2 = [jnp.where(h_ > 0, x[0:c, pw:2 * pw], 0.0) for x, h_ in zip(p, ahead)]
    n1 = [jnp.where(h_ >= 0, x[c:2 * c, 0:pw], 0.0) for x, h_ in zip(p, ahead)]
    n2 = [jnp.where(h_ >= 0, x[c:2 * c, pw:2 * pw], 0.0) for x, h_ in zip(p, ahead)]
    m1v = [_dot(a_, bd(b_)) for a_, b_ in zip(m1, vp)]
    t_inv = [eye_p - x for x in m2]
    q = [_dot(x, bd(x)) for x in m2]
    span = 2
    while 2 * span < c:
        both = [_dot(jnp.concatenate([t_, q_], axis=0), bd(q_)) for t_, q_ in zip(t_inv, q)]
        t_inv = [t_ + x[0:c] for t_, x in zip(t_inv, both)]
        q = [x[c:2 * c] for x in both]
        span *= 2
    t_inv = [t_ + _dot(t_, bd(q_)) for t_, q_ in zip(t_inv, q)]
    tx = [_dot(t_, jnp.concatenate([bd(a_), bd(mv)], axis=1)) for t_, a_, mv in zip(t_inv, kap, m1v)]
    st = [st_ref[i] for i in range(len(prob))]
    su = [_dot_nt(jnp.concatenate([x[:, 0:pw], r_], axis=0), s0) for x, r_, s0 in zip(tx, r_s, st)]
    u = [x[0:c] + y_[:, pw:2 * pw] for x, y_ in zip(su, tx)]
    ys = [x[c:2 * c] + _dot(jnp.concatenate([a_, -b_], axis=1), jnp.concatenate([bd(v_), bd(u_)], axis=0)) + bo * v_
          for x, a_, b_, v_, u_, bo in zip(su, n1, n2, vp, u, bonus)]
    for i in range(len(prob)):
        upd = _dot_tn(jnp.concatenate([vp[i], u[i]], axis=0), jnp.concatenate([k_e[i], -b_e[i]], axis=0))
        st_ref[i] = st[i] * gam[i] + jnp.where(same_head, upd, 0.0)
    for b in range(nb):
        yf_ref[b] = jnp.concatenate(ys[2 * b * n_pairs:(2 * b + 1) * n_pairs], axis=1)
        yb_ref[b] = jnp.concatenate(ys[(2 * b + 1) * n_pairs:(2 * b + 2) * n_pairs], axis=1)


def _rwkv(zr, mu, kk_scale, k_a, r_k, w0, w2_pad, a0, a2_pad, g2, batch, nc_ctx, nc_tot):
    t, zw_ = zr.shape
    c = SEQ_CHUNK
    n = RWKV_DIM
    bd = _block_ones(LANES // HEAD_DIM, HEAD_DIM)
    full = lambda a: pl.BlockSpec(a.shape, lambda b, s: (0,) * a.ndim)
    sub = c // SUBLANES
    l_tot = t // batch
    n_sub = l_tot // SUBLANES
    nb = math.gcd(batch, RWKV_SCAN_BATCH)
    z3 =zr.reshape(batch, l_tot, zw_)

    def z_specs(d):
        pos = lambda s: _chunk_pos(s, d, nc_ctx, nc_tot)
        return [pl.BlockSpec((nb, c, zw_), lambda b, s: (b, pos(s), 0)),
                pl.BlockSpec((nb, SUBLANES, zw_), lambda b, s: (b, jnp.maximum(pos(s) * sub - 1, 0), 0)),
                pl.BlockSpec((nb, SUBLANES, zw_), lambda b, s: (b, jnp.minimum((pos(s) + 1) * sub, n_sub - 1), 0))]

    vec = lambda a: a.reshape(1, -1)
    args = (vec(mu), vec(kk_scale), vec(k_a), vec(r_k), w0.reshape(2, 1, n), w2_pad, a0.reshape(2, 1, n), a2_pad, g2, bd)
    out = lambda d: pl.BlockSpec((nb, c, n), lambda b, s: (b, _chunk_pos(s, d, nc_ctx, nc_tot), 0))
    outs = pl.pallas_call(
        functools.partial(_rwkv_kernel, nc_ctx=nc_ctx, nc_tot=nc_tot),
        grid=(batch // nb, nc_tot),
        in_specs=z_specs(0) + z_specs(1) + [full(a) for a in args],
        out_specs=[out(0), out(1), out(0)],
        out_shape=[jax.ShapeDtypeStruct((batch, l_tot, n), F32)] * 3,
        scratch_shapes=[pltpu.VMEM((nb * RWKV_HEADS, 2 * HEAD_DIM, 2 * HEAD_DIM), F32)],
        compiler_params=_cparams(("parallel", "arbitrary")),
        name="rwkv_scan",
    )(z3, z3, z3, z3, z3, z3, *args)
    return [o.reshape(t, n) for o in outs]


def _merge_odd_kernel(y0_ref, y1_ref, gate_ref, fo_ref, x_ref, mod_ref, lg_ref, lb_ref, bd_ref, w1_ref, w2_ref,
                      fg_ref, rw_ref, rb_ref, xo_ref, ho_ref, go_ref):
    y = y0_ref[...] + y1_ref[...]
    bd = bd_ref[...]
    mean = _head_sums(y, bd) * (1.0 / HEAD_DIM)
    yc = y - mean
    var = _head_sums(yc * yc, bd) * (1.0 / HEAD_DIM)
    rw = (yc * lax.rsqrt(var + RWKV_GN_EPS) * lg_ref[...] + lb_ref[...]) * gate_ref[...]
    out = _dot(fo_ref[...].astype(BF16), w1_ref[...]) + _dot(rw.astype(BF16), w2_ref[...])
    _residual_and_route(x_ref[...], out, mod_ref[0], fg_ref[...], rw_ref, rb_ref, xo_ref, ho_ref, go_ref)


def _merge_odd(y0, y1, gate, fo, x, mods, ln_g, ln_b, w_out, ffn_gain, router_wt, router_b, tm, batch, l_ctx, l_lat):
    d = x.shape[1]
    n = RWKV_DIM
    fw = FOURIER_WIDTH
    l_tot = l_ctx + l_lat
    nt = l_lat // tm
    t_out = batch * l_lat
    full = lambda a: pl.BlockSpec(a.shape, lambda b, i: (0,) * a.ndim)
    src = lambda b, i: b * (l_tot // tm) + l_ctx // tm + i
    bd = _block_ones(LANES // HEAD_DIM, HEAD_DIM)
    w1, w2 = w_out[:fw].astype(BF16), w_out[fw:].astype(BF16)
    lg, lb, fg, rb = ln_g.reshape(1, n), ln_b.reshape(1, n), ffn_gain.reshape(1, d), router_b.reshape(N_EXPERTS, 1)
    return pl.pallas_call(
        _merge_odd_kernel,
        grid=(batch, nt),
        in_specs=[pl.BlockSpec((tm, n), lambda b, i: (src(b, i), 0)),
                  pl.BlockSpec((tm, n), lambda b, i: (src(b, i), 0)),
                  pl.BlockSpec((tm, n), lambda b, i: (src(b, i), 0)),
                  pl.BlockSpec((tm, fw), lambda b, i: (i, b)),
                  pl.BlockSpec((tm, d), lambda b, i: (src(b, i), 0)),
                  pl.BlockSpec((1, 6, d), lambda b, i: (b, 0, 0)),
                  full(lg), full(lb), full(bd), full(w1), full(w2), full(fg), full(router_wt), full(rb)],
        out_specs=[pl.BlockSpec((tm, d), lambda b, i: (b * nt + i, 0)),
                   pl.BlockSpec((tm * (d // LANES), LANES), lambda b, i: (b * nt + i, 0)),
                   pl.BlockSpec((8, tm), lambda b, i: (0, b * nt + i))],
        out_shape=[jax.ShapeDtypeStruct((t_out, d), F32),
                   jax.ShapeDtypeStruct((t_out * (d // LANES), LANES), F32),
                   jax.ShapeDtypeStruct((8, t_out), F32)],
        compiler_params=_cparams(("parallel", "parallel")),
        name="merge_odd",
    )(y0, y1, gate, fo, x, mods, lg, lb, bd, w1, w2, fg, router_wt, rb)


def _rope_tables(l_ctx, l_lat):
    rows = l_lat // GRID_W
    row = jnp.repeat(jnp.arange(rows, dtype=F32), GRID_W)
    col = jnp.tile(jnp.arange(GRID_W, dtype=F32), rows)
    n_freq = HEAD_DIM // 4
    inv_freq = ROPE_THETA ** (-jnp.arange(n_freq, dtype=F32) / n_freq)
    ang = jnp.concatenate([row[:, None] * inv_freq, col[:, None] * inv_freq], axis=-1)
    cos, sin = jnp.cos(ang), jnp.sin(ang)
    cos64 = jnp.concatenate([cos, cos], axis=1)
    sin64 = jnp.concatenate([-sin, sin], axis=1)
    cos64 = jnp.concatenate([jnp.ones((l_ctx, HEAD_DIM), F32), cos64], axis=0)
    sin64 = jnp.concatenate([jnp.zeros((l_ctx, HEAD_DIM), F32), sin64], axis=0)
    return cos64, sin64


def _pad_rank(w):
    _, r, n = w.shape
    out = jnp.zeros((2, RWKV_RANK_PAD, n), w.dtype)
    out = out.at[0, 0:r].set(w[0])
    return out.at[1, r:2 * r].set(w[1])


def _even_layer(ctx, x_lat, mods, p, batch, l_ctx, l_lat, tm, mod_row):
    l_tot = l_ctx + l_lat
    nc_ctx, nc_tot = l_ctx // SEQ_CHUNK, l_tot // SEQ_CHUNK
    hk, hv = GLA_HEADS * GLA_DK, GLA_HEADS * GLA_DV
    qw, kw = ATT_HEADS * HEAD_DIM, ATT_KV_HEADS * HEAD_DIM
    w_in = p['w_in']
    o = np.cumsum([0, hk, hk, hv, hv, 2 * GLA_LOWRANK, qw, kw, kw])
    w_gla = jnp.concatenate([w_in[:, o[0]:o[4]]], axis=1).astype(BF16)
    w_dec = jnp.pad(w_in[:, o[4]:o[5]], ((0, 0), (0, 128 - 2 * GLA_LOWRANK))).astype(BF16)
    w_q = w_in[:, o[5]:o[6]].astype(BF16)
    w_kv = w_in[:, o[6]:o[8]].astype(BF16)
    cos64, sin64 = _rope_tables(l_ctx, l_lat)
    x, zg, zdec, qn, kn, vn = _project_even(ctx, x_lat, mods, p['norm_mix'], [w_gla, w_dec, w_q, w_kv], cos64, sin64,
                                            p['q_norm'], p['k_norm'], tm, mod_row, batch, l_ctx, l_lat)

    dec_w_pad = _pad_rank(p['dec_w'])
    o_f, o_b = _gla(zg, zdec, dec_w_pad, p['dec_b'].reshape(2, 1, hk), batch, nc_ctx, nc_tot)
    o_att = _attention(qn, kn, vn, p['sink'], batch, l_ctx, l_tot)

    x1, h, gates = _merge_even(o_f, o_b, zg, o_att, x, mods, p['out_norm'], p['w_out'], p['norm_ffn'],
                               p['router_wt'], p['router_b'], tm, mod_row)
    pairs = _moe(h, gates, x1, mods, p['moe_g'], p['moe_u'], p['moe_d'], p['moe_layer'], tm, mod_row, tm, combine=False)
    return pairs, x1


def _odd_layer(pairs, x_prev, mods_prev, mods, p, batch, l_ctx, l_lat, tm, mod_row, tm_moe):
    l_tot = l_ctx + l_lat
    nc_ctx, nc_tot = l_ctx // SEQ_CHUNK, l_tot // SEQ_CHUNK
    n = RWKV_DIM
    fw = FOURIER_WIDTH
    w_in = p['w_in']
    rank_w, rank_a = p['w2'].shape[1], p['a2'].shape[1]
    o = np.cumsum([0, fw, n, n, n, 2 * rank_w, 2 * rank_a])
    pad_cols = lambda w: jnp.pad(w, ((0, 0), (0, RWKV_RANK_PAD - w.shape[1])))
    w_f = w_in[:, o[0]:o[1]].astype(BF16)
    w_r = jnp.concatenate([w_in[:, o[1]:o[4]], pad_cols(w_in[:, o[4]:o[5]]), pad_cols(w_in[:, o[5]:o[6]]),
                           w_in[:, o[6]:]], axis=1).astype(BF16)
    x, (zf, zr) = _project(pairs, x_prev, mods_prev, mods, p['norm_mix'], [w_f, w_r], tm, mod_row)

    mu = p['mu']
    mu_r = jnp.concatenate([mu[0:3 * n], pad_cols(mu[None, 3 * n:3 * n + 2 * rank_w])[0],
                            pad_cols(mu[None, 3 * n + 2 * rank_w:3 * n + 2 * rank_w + 2 * rank_a])[0],
                            mu[3 * n + 2 * rank_w + 2 * rank_a:]])
    y0, y1, gate = _rwkv(zr, mu_r, p['kk_scale'], p['k_a'], p['r_k'].reshape(-1), p['w0'], _pad_rank(p['w2']),
                    p['a0'], _pad_rank(p['a2']), p['g2'], batch, nc_ctx, nc_tot)

    tables, chan = _dft_tables(l_lat)
    zc = _chan_dft(zf, chan, batch, l_ctx, l_lat, tm)
    fo = _seq_dft(tables, zc, batch, min(512, l_lat), min(1024, l_lat))

    x1, h, gates = _merge_odd(y0, y1, gate, fo, x, mods, p['ln_g'], p['ln_b'], p['w_out'], p['norm_ffn'],
                              p['router_wt'], p['router_b'], tm, batch, l_ctx, l_lat)
    lat_tiles = l_lat // tm_moe
    return _moe(h, gates, x1, mods, p['moe_g'], p['moe_u'], p['moe_d'], p['moe_layer'], tm_moe,
                lambda i: i // lat_tiles, tm)


def kernel(x, c, ctx, c_ctx, ada_w, ada_b, norm_mix, norm_ffn, even_w_in, even_w_out, gla_dec_w, gla_dec_b, gla_out_norm, att_q_norm, att_k_norm, att_sink, odd_w_in, odd_w_out, rwkv_mu, rwkv_w0, rwkv_w2, rwkv_a0, rwkv_a2, rwkv_g2, rwkv_kk_scale, rwkv_k_a, rwkv_r_k, rwkv_ln_g, rwkv_ln_b, router_w, router_b, moe_w_gate, moe_w_up, moe_w_down):
    batch, l_lat, d = x.shape
    l_ctx = ctx.shape[1]
    l_tot = l_ctx + l_lat
    assert batch < 8 and ada_w.shape[0] == 2
    tm = 256 if (l_ctx % 256 == 0 and l_lat % 256 == 0) else 128
    tm_moe = 512 if (l_lat % 512 == 0 and tm == 256) else tm
    assert l_ctx % tm == 0 and l_lat % tm == 0 and l_tot % l_ctx == 0 and l_lat % GRID_W == 0

    cc = jnp.concatenate([c, c_ctx[None, :], jnp.zeros((8 - batch - 1, d), F32)], axis=0)
    tiles_per_b = l_tot // tm
    ctx_tiles = l_ctx // tm

    def mod_row(i):
        return jnp.where(i % tiles_per_b < ctx_tiles, batch, i // tiles_per_b)

    rw_pad = jnp.pad(router_w, ((0, 0), (0, LANES - N_EXPERTS)))
    rw_hi = rw_pad.astype(BF16)
    router_wt = jnp.stack([rw_hi, (rw_pad - rw_hi.astype(F32)).astype(BF16)])
    moe = lambda layer: dict(moe_g=moe_w_gate, moe_u=moe_w_up, moe_d=moe_w_down, moe_layer=layer)

    mods_all = _modvec(cc, ada_w, ada_b)
    mods0 = mods_all[0]
    p0 = dict(w_in=even_w_in[0], w_out=even_w_out[0], dec_w=gla_dec_w[0], dec_b=gla_dec_b[0],
              out_norm=gla_out_norm[0], q_norm=att_q_norm[0], k_norm=att_k_norm[0], sink=att_sink[0],
              norm_mix=norm_mix[0], norm_ffn=norm_ffn[0], router_wt=router_wt, router_b=router_b, **moe(0))
    pairs, x1 = _even_layer(ctx.reshape(batch * l_ctx, d), x.reshape(batch * l_lat, d), mods0, p0, batch, l_ctx, l_lat,
                            tm, mod_row)

    mods1 = mods_all[1]
    p1 = dict(w_in=odd_w_in[0], w_out=odd_w_out[0], mu=rwkv_mu[0], w0=rwkv_w0[0], w2=rwkv_w2[0], a0=rwkv_a0[0],
              a2=rwkv_a2[0], g2=rwkv_g2[0], kk_scale=rwkv_kk_scale[0], k_a=rwkv_k_a[0], r_k=rwkv_r_k[0],
              ln_g=rwkv_ln_g[0], ln_b=rwkv_ln_b[0], norm_mix=norm_mix[1], norm_ffn=norm_ffn[1],
              router_wt=router_wt, router_b=router_b, **moe(1))
    out = _odd_layer(pairs, x1, mods0, mods1, p1, batch, l_ctx, l_lat, tm, mod_row, tm_moe)
    return out.reshape(batch, l_lat, d)
```

```python
import functools
import math

import jax
import jax.numpy as jnp
import numpy as np
from jax import lax
from jax.experimental import pallas as pl
from jax.experimental.pallas import tpu as pltpu

F32 = jnp.float32
BF16 = jnp.bfloat16

GRID_W = 64
HEAD_DIM = 64
NORM_EPS = 1e-6
L2_EPS = 1e-12

GLA_DV = 64
GLA_DK = 32
GLA_HEADS = 8
GLA_LOWRANK = 16
GLA_TAU = 16.0

ATT_HEADS = 8
ATT_KV_HEADS = 2
ATT_GROUP = ATT_HEADS // ATT_KV_HEADS
ATT_BLOCK = 128
ROPE_THETA = 10000.0

FOURIER_GROUP_DIM = 64
FOURIER_WIDTH = 256
DFT_SPLIT = 64

RWKV_DIM = 768
RWKV_HEADS = 12
RWKV_RANK_PAD = 128
RWKV_GN_EPS = 64e-5

N_EXPERTS = 16
N_GROUPS = 4
PER_GROUP = N_EXPERTS // N_GROUPS
D_EXPERT = 512
MOE_TILE = 512
MOE_GATHER_DEPTH = 3
LANES = 128
SUBLANES = 8

SEQ_CHUNK = 64
GLA_SCAN_BATCH = 4
RWKV_SCAN_BATCH = 2
VMEM_LIMIT = 56 * 1024 * 1024


def _cparams(sem):
    return pltpu.CompilerParams(dimension_semantics=sem, vmem_limit_bytes=VMEM_LIMIT)


def _dot(a, b):
    return jnp.dot(a, b, preferred_element_type=F32)


def _dot_nt(a, b):
    return lax.dot_general(a, b, (((1,), (1,)), ((), ())), preferred_element_type=F32)


def _dot_tn(a, b):
    return lax.dot_general(a, b, (((0,), (0,)), ((), ())), preferred_element_type=F32)


def _silu(x):
    return x * jax.nn.sigmoid(x)


def _log_sigmoid(x):
    return jnp.minimum(x, 0.0) - jnp.log(1.0 + jnp.exp(-jnp.abs(x)))


def _modulated_norm(x, gain, shift, scale):
    ms = jnp.mean(x * x, axis=-1, keepdims=True)
    return (x * lax.rsqrt(ms + NORM_EPS) * gain) * (1.0 + scale) + shift


def _block_ones(n_blocks, width):
    return jnp.kron(jnp.eye(n_blocks, dtype=F32), jnp.ones((width, width), F32))


def _head_sums(x, ones_blk):
    w = ones_blk.shape[0]
    return jnp.concatenate([_dot(x[:, i:i + w], ones_blk) for i in range(0, x.shape[1], w)], axis=1)


def _store_row_tiles(ref, x, first_row=0):
    n, w = x.shape
    g = w // LANES
    for k in range(g):
        ref[pl.ds(first_row * g + k, n, stride=g), :] = x[:, k * LANES:(k + 1) * LANES]


def _load_row_tiles(ref, n, g, first_row=0):
    return jnp.concatenate([ref[pl.ds(first_row * g + k, n, stride=g), :] for k in range(g)], axis=1)


def _modvec_kernel(c_ref, w_ref, b_ref, o_ref):
    o_ref[...] = _dot(_silu(c_ref[...]), w_ref[...]) + b_ref[...]


def _modvec(cc, w, b):
    d = cc.shape[1]
    layers, _, n = w.shape
    tn = n // 4
    out = pl.pallas_call(
        _modvec_kernel,
        grid=(layers, n // tn),
        in_specs=[pl.BlockSpec((8, d), lambda l, j: (0, 0)),
                  pl.BlockSpec((None, d, tn), lambda l, j: (l, 0, j)),
                  pl.BlockSpec((None, 1, tn), lambda l, j: (l, 0, j))],
        out_specs=pl.BlockSpec((None, 8, tn), lambda l, j: (l, 0, j)),
        out_shape=jax.ShapeDtypeStruct((layers, 8, n), F32),
        compiler_params=_cparams(("parallel", "parallel")),
        name="modvec",
    )(cc, w, b.reshape(layers, 1, n))
    return out.reshape(layers, 8, 6, d)


def _proj_kernel(ya_ref, yb_ref, x_ref, mp0_ref, mp1_ref, m0_ref, m1_ref, gain_ref, *refs, n_out):
    w_refs, x_out_ref, z_refs = refs[:n_out], refs[n_out], refs[n_out + 1:]
    n, d = x_ref.shape[0] // 2, x_ref.shape[1]
    g = d // LANES
    for half, (mp_ref, m_ref) in enumerate(((mp0_ref, m0_ref), (mp1_ref, m1_ref))):
        rows = pl.ds(half * n, n)
        y = _load_row_tiles(ya_ref, n, g, first_row=half * n) + _load_row_tiles(yb_ref, n, g, first_row=half * n)
        x = x_ref[rows, :] + mp_ref[0][5:6] * y
        x_out_ref[rows, :] = x
        m = m_ref[0]
        h = _modulated_norm(x, gain_ref[...], m[0:1], m[1:2]).astype(BF16)
        for w_ref, z_ref in zip(w_refs, z_refs):
            z_ref[rows, :] = _dot(h, w_ref[...])


def _project(pairs, x, mods_prev, mods, gain, weights, tm, mod_row):
    t, d = x.shape
    g = d // LANES
    n_out = len(weights)
    assert (t // tm) % 2 == 0
    tb = 2 * tm
    mod_spec = lambda half: pl.BlockSpec((1, 6, d), lambda i: (mod_row(2 * i + half), 0, 0))
    in_specs = [pl.BlockSpec((tb * g, LANES), lambda i: (i, 0)),
                pl.BlockSpec((tb * g, LANES), lambda i: (t // tb + i, 0)),
                pl.BlockSpec((tb, d), lambda i: (i, 0)),
                mod_spec(0), mod_spec(1), mod_spec(0), mod_spec(1),
                pl.BlockSpec((1, d), lambda i: (0, 0))]
    in_specs += [pl.BlockSpec(w.shape, lambda i: (0, 0)) for w in weights]
    outs = pl.pallas_call(
        functools.partial(_proj_kernel, n_out=n_out),
        grid=(t // tb,),
        in_specs=in_specs,
        out_specs=[pl.BlockSpec((tb, d), lambda i: (i, 0))] + [pl.BlockSpec((tb, w.shape[1]), lambda i: (i, 0)) for w in weights],
        out_shape=[jax.ShapeDtypeStruct((t, d), F32)] + [jax.ShapeDtypeStruct((t, w.shape[1]), F32) for w in weights],
        compiler_params=_cparams(("parallel",)),
        name="proj",
    )(pairs, pairs, x, mods_prev, mods_prev, mods, mods, gain.reshape(1, d), *weights)
    return outs[0], outs[1:]


def _chunk_pos(s, d, nc_ctx, nc_tot):
    back = jnp.where(s < nc_ctx, nc_ctx - 1 - s, nc_tot + nc_ctx - 1 - s)
    return jnp.where(d == 0, s, back)


def _visit_order(d, c):
    sign = 1 if d == 0 else -1
    return (lax.broadcasted_iota(jnp.int32, (c, c), 0) - lax.broadcasted_iota(jnp.int32, (c, c), 1)) * sign


def _gla_prepare(d, q, k, v, dec, dw_ref, db_ref):
    c = q.shape[0]
    g = _log_sigmoid(_dot(dec, dw_ref[d]) + db_ref[d]) / GLA_TAU
    b = _dot((_visit_order(d, c) >= 0).astype(F32), g)
    b_tot = jnp.sum(g, axis=0, keepdims=True)
    return dict(q_in=q * (GLA_DK ** -0.5) * jnp.exp(b), k_out=k * jnp.exp(-b), k_end=k * jnp.exp(b_tot - b),
                decay=jnp.exp(b_tot), v=v, sign=1 if d == 0 else -1)


def _gla_kernel(qf_ref, kf_ref, vf_ref, df_ref, qb_ref, kb_ref, vb_ref, db_ref, dw_ref, dbias_ref, of_ref, ob_ref, st_ref):
    nb, c = qf_ref.shape[0], qf_ref.shape[1]
    group = 4
    kw, vw = group * GLA_DK, group * GLA_DV

    @pl.when(pl.program_id(1) == 0)
    def _():
        st_ref[...] = jnp.zeros_like(st_ref)

    ctxs = []
    for u in range(nb):
        ctxs.append(_gla_prepare(0, qf_ref[u], kf_ref[u], vf_ref[u], df_ref[u], dw_ref, dbias_ref))
        ctxs.append(_gla_prepare(1, qb_ref[u], kb_ref[u], vb_ref[u], db_ref[u], dw_ref, dbias_ref))
    n_quads = GLA_HEADS // group
    prob = [(o, i) for o in ctxs for i in range(n_quads)]
    klane = lax.broadcasted_iota(jnp.int32, (c, kw), 1) // GLA_DK
    half = lax.broadcasted_iota(jnp.int32, (c, 2 * GLA_DV), 1) < GLA_DV
    rowi = lax.broadcasted_iota(jnp.int32, (c, group * c), 0)
    coli = lax.broadcasted_iota(jnp.int32, (c, group * c), 1) % c
    own = (lax.broadcasted_iota(jnp.int32, (vw, kw), 0) // GLA_DV) == (lax.broadcasted_iota(jnp.int32, (vw, kw), 1) // GLA_DK)

    def bd_keys(y):
        return jnp.concatenate([jnp.where(klane == h, y, 0.0) for h in range(group)], axis=0)

    def bd_vals(y):
        return jnp.concatenate([jnp.where(half, y, 0.0), jnp.where(half, 0.0, y)], axis=0)

    ksl = lambda i: slice(i * kw, (i + 1) * kw)
    vsl = lambda i: slice(i * vw, (i + 1) * vw)
    q_in = [o['q_in'][:, ksl(i)] for o, i in prob]
    att = [jnp.where((rowi - coli) * o['sign'] >= 0, _dot_nt(q_, bd_keys(o['k_out'][:, ksl(i)])), 0.0)
           for q_, (o, i) in zip(q_in, prob)]
    st = [st_ref[j] for j in range(len(prob))]
    outs = []
    for j, (o, i) in enumerate(prob):
        v = o['v'][:, vsl(i)]
        intra = jnp.concatenate([_dot(att[j][:, p * 2 * c:(p + 1) * 2 * c], bd_vals(v[:, p * 2 * GLA_DV:(p + 1) * 2 * GLA_DV]))
                                 for p in range(group // 2)], axis=1)
        outs.append(intra + _dot_nt(q_in[j], st[j]))
    for j, (o, i) in enumerate(prob):
        upd = _dot_tn(o['v'][:, vsl(i)], o['k_end'][:, ksl(i)])
        st_ref[j] = st[j] * o['decay'][:, ksl(i)] + jnp.where(own, upd, 0.0)
    for u in range(nb):
        of_ref[u] = jnp.concatenate(outs[2 * u * n_quads:(2 * u + 1) * n_quads], axis=1)
        ob_ref[u] = jnp.concatenate(outs[(2 * u + 1) * n_quads:(2 * u + 2) * n_quads], axis=1)


def _gla(zg, zdec, dec_w_pad, dec_b, batch, nc_ctx, nc_tot):
    t = zg.shape[0]
    c = SEQ_CHUNK
    hk, hv = GLA_HEADS * GLA_DK, GLA_HEADS * GLA_DV
    nb = math.gcd(batch, GLA_SCAN_BATCH)
    zg3, zdec3 = zg.reshape(batch, t // batch, -1), zdec.reshape(batch, t // batch, -1)

    def specs(d):
        pos = lambda s: _chunk_pos(s, d, nc_ctx, nc_tot)
        return [pl.BlockSpec((nb, c, hk), lambda b, s: (b, pos(s), 0)),
                pl.BlockSpec((nb, c, hk), lambda b, s: (b, pos(s), 1)),
                pl.BlockSpec((nb, c, hv), lambda b, s: (b, pos(s), 1)),
                pl.BlockSpec((nb, c, 128), lambda b, s: (b, pos(s), 0))], pl.BlockSpec((nb, c, hv), lambda b, s: (b, pos(s), 0))

    in_f, out_f = specs(0)
    in_b, out_b = specs(1)
    o_f, o_b = pl.pallas_call(
        _gla_kernel,
        grid=(batch // nb, nc_tot),
        in_specs=in_f + in_b + [pl.BlockSpec(dec_w_pad.shape, lambda b, s: (0, 0, 0)),
                                pl.BlockSpec(dec_b.shape, lambda b, s: (0, 0, 0))],
        out_specs=[out_f, out_b],
        out_shape=[jax.ShapeDtypeStruct((batch, t // batch, hv), F32)] * 2,
        scratch_shapes=[pltpu.VMEM((nb * 2 * GLA_HEADS // 4, 4 * GLA_DV, 4 * GLA_DK), F32)],
        compiler_params=_cparams(("parallel", "arbitrary")),
        name="gla_scan",
    )(zg3, zg3, zg3, zdec3, zg3, zg3, zg3, zdec3, dec_w_pad, dec_b)
    return o_f.reshape(t, hv), o_b.reshape(t, hv)


def _rope_swap(x):
    n = x.shape[-1]
    lane = lax.broadcasted_iota(jnp.int32, x.shape, x.ndim - 1)
    half = HEAD_DIM // 2
    return jnp.where(lane % HEAD_DIM < half, pltpu.roll(x, n - half, x.ndim - 1), pltpu.roll(x, half, x.ndim - 1))


def _proj_even_kernel(ctx_ref, x_ref, mod_ref, gain_ref, wg_ref, wd_ref, wq_ref, wkv_ref, cos_ref, sin_ref, qg_ref, kg_ref,
                      bdq_ref, bdk_ref, xs_ref, zg_ref, zdec_ref, qo_ref, ko_ref, vo_ref, *, tiles_per_b, ctx_tiles):
    is_ctx = lax.rem(pl.program_id(0), tiles_per_b) < ctx_tiles
    x = jnp.where(is_ctx, ctx_ref[...], x_ref[...])
    xs_ref[...] = x
    m = mod_ref[0]
    h = _modulated_norm(x, gain_ref[...], m[0:1], m[1:2]).astype(BF16)
    zg_ref[...] = _dot(h, wg_ref[...])
    zdec_ref[...] = _dot(h, wd_ref[...])

    def norm_rope(z, gain, bd, n_heads):
        ms = _head_sums(z * z, bd) * (1.0 / HEAD_DIM)
        zn = z * lax.rsqrt(ms + NORM_EPS) * gain
        cos = jnp.concatenate([cos_ref[...]] * n_heads, axis=1)
        sin = jnp.concatenate([sin_ref[...]] * n_heads, axis=1)
        return zn * cos + _rope_swap(zn) * sin

    q = norm_rope(_dot(h, wq_ref[...]), qg_ref[...], bdq_ref[...], ATT_HEADS)
    qo_ref[...] = (q * (HEAD_DIM ** -0.5)).astype(BF16)
    kw = ATT_KV_HEADS * HEAD_DIM
    kv = _dot(h, wkv_ref[...])
    ko_ref[...] = norm_rope(kv[:, :kw], kg_ref[...], bdk_ref[...], ATT_KV_HEADS).astype(BF16)
    vo_ref[...] = kv[:, kw:].astype(BF16)


def _project_even(ctx, x, mods, gain, weights, cos64, sin64, q_gain, k_gain, tm, mod_row, batch, l_ctx, l_lat):
    d = x.shape[1]
    l_tot = l_ctx + l_lat
    t = batch * l_tot
    tiles_per_b, ctx_tiles, lat_tiles = l_tot // tm, l_ctx // tm, l_lat // tm
    qw, kw = ATT_HEADS * HEAD_DIM, ATT_KV_HEADS * HEAD_DIM
    w_gla, w_dec, w_q, w_kv = weights
    full = lambda a: pl.BlockSpec(a.shape, lambda i: (0,) * a.ndim)
    qg = jnp.tile(q_gain, ATT_HEADS).reshape(1, qw)
    kg = jnp.tile(k_gain, ATT_KV_HEADS).reshape(1, kw)
    bdq = bdk = _block_ones(LANES // HEAD_DIM, HEAD_DIM)
    gain = gain.reshape(1, d)
    row = lambda n: pl.BlockSpec((tm, n), lambda i: (i, 0))
    pos = pl.BlockSpec((tm, HEAD_DIM), lambda i: (i % tiles_per_b, 0))
    return pl.pallas_call(
        functools.partial(_proj_even_kernel, tiles_per_b=tiles_per_b, ctx_tiles=ctx_tiles),
        grid=(t // tm,),
        in_specs=[pl.BlockSpec((tm, d), lambda i: ((i // tiles_per_b) * ctx_tiles + jnp.minimum(i % tiles_per_b, ctx_tiles - 1), 0)),
                  pl.BlockSpec((tm, d), lambda i: ((i // tiles_per_b) * lat_tiles + jnp.maximum(i % tiles_per_b - ctx_tiles, 0), 0)),
                  pl.BlockSpec((1, 6, d), lambda i: (mod_row(i), 0, 0)),
                  full(gain), full(w_gla), full(w_dec), full(w_q), full(w_kv), pos, pos,
                  full(qg), full(kg), full(bdq), full(bdk)],
        out_specs=[row(d), row(w_gla.shape[1]), row(w_dec.shape[1]), row(qw), row(kw), row(kw)],
        out_shape=[jax.ShapeDtypeStruct((t, d), F32), jax.ShapeDtypeStruct((t, w_gla.shape[1]), F32),
                   jax.ShapeDtypeStruct((t, w_dec.shape[1]), F32), jax.ShapeDtypeStruct((t, qw), BF16),
                   jax.ShapeDtypeStruct((t, kw), BF16), jax.ShapeDtypeStruct((t, kw), BF16)],
        compiler_params=_cparams(("parallel",)),
        name="proj_even",
    )(ctx, x, mods, gain, w_gla, w_dec, w_q, w_kv, cos64, sin64, qg, kg, bdq, bdk)


def _attn_kernel(q_ref, kp_ref, kc_ref, kn_ref, kx_ref, vp_ref, vc_ref, vn_ref, vx_ref, sink_ref, o_ref,
                 *, n_ctx_blocks, n_lat_blocks):
    blk = ATT_BLOCK
    n = pl.program_id(1)
    m = n - n_ctx_blocks
    is_lat = n >= n_ctx_blocks
    l_ctx = kx_ref.shape[0]
    width = 3 * blk + l_ctx
    rows = ATT_GROUP * blk
    r = lax.broadcasted_iota(jnp.int32, (rows, width), 0) % blk
    c = lax.broadcasted_iota(jnp.int32, (rows, width), 1)
    lat = is_lat.astype(jnp.int32)
    has_prev = lat * (m >= 1).astype(jnp.int32)
    has_next = lat * (m <= n_lat_blocks - 2).astype(jnp.int32)
    valid = jnp.where(c < blk, (c >= r).astype(jnp.int32) * has_prev,
                      jnp.where(c < 2 * blk, lat,
                                jnp.where(c < 3 * blk, (c - 2 * blk <= r).astype(jnp.int32) * has_next, 1))) > 0
    q = q_ref[...]
    sink = sink_ref[...]
    groups = range(ATT_KV_HEADS)
    ks = [slice(kvh * HEAD_DIM, (kvh + 1) * HEAD_DIM) for kvh in groups]
    heads = [range(kvh * ATT_GROUP, (kvh + 1) * ATT_GROUP) for kvh in groups]
    kw = [jnp.concatenate([kp_ref[:, s_], kc_ref[:, s_], kn_ref[:, s_], kx_ref[:, s_]], axis=0) for s_ in ks]
    vw = [jnp.concatenate([vp_ref[:, s_], vc_ref[:, s_], vn_ref[:, s_], vx_ref[:, s_]], axis=0) for s_ in ks]
    qg = [jnp.concatenate([q[:, h * HEAD_DIM:(h + 1) * HEAD_DIM] for h in hs], axis=0) for hs in heads]
    sk = [jnp.concatenate([jnp.broadcast_to(sink[h:h + 1, 0:1], (blk, 1)) for h in hs], axis=0) for hs in heads]
    s = [jnp.where(valid, _dot_nt(a_, b_), -jnp.inf) for a_, b_ in zip(qg, kw)]
    mx = [jnp.maximum(jnp.max(a_, axis=-1, keepdims=True), b_) for a_, b_ in zip(s, sk)]
    p = [jnp.exp(a_ - b_) for a_, b_ in zip(s, mx)]
    denom = [jnp.sum(a_, axis=-1, keepdims=True) + jnp.exp(b_ - c_) for a_, b_, c_ in zip(p, sk, mx)]
    o = [_dot(a_.astype(BF16), b_) / c_ for a_, b_, c_ in zip(p, vw, denom)]
    outs = [o[kvh][g * blk:(g + 1) * blk] for kvh in groups for g in range(ATT_GROUP)]
    o_ref[...] = jnp.concatenate(outs, axis=1)


def _attention(qn, kn, vn, sink, batch, l_ctx, l_tot):
    t = qn.shape[0]
    blk = ATT_BLOCK
    nq = l_tot // blk
    nc = l_ctx // blk
    nl = nq - nc
    qw, kw = ATT_HEADS * HEAD_DIM, ATT_KV_HEADS * HEAD_DIM

    def win(off):
        def index(b, n):
            m = jnp.clip(n - nc + off, 0, nl - 1)
            return (b * nq + nc + m, 0)
        return pl.BlockSpec((blk, kw), index)

    ctx_spec = pl.BlockSpec((l_ctx, kw), lambda b, n: (b * (l_tot // l_ctx), 0))
    return pl.pallas_call(
        functools.partial(_attn_kernel, n_ctx_blocks=nc, n_lat_blocks=nl),
        grid=(batch, nq),
        in_specs=[pl.BlockSpec((blk, qw), lambda b, n: (b * nq + n, 0)),
                  win(-1), win(0), win(1), ctx_spec,
                  win(-1), win(0), win(1), ctx_spec,
                  pl.BlockSpec((ATT_HEADS, 128), lambda b, n: (0, 0))],
        out_specs=pl.BlockSpec((blk, qw), lambda b, n: (b * nq + n, 0)),
        out_shape=jax.ShapeDtypeStruct((t, qw), F32),
        compiler_params=_cparams(("parallel", "parallel")),
        name="window_attention",
    )(qn, kn, kn, kn, kn, vn, vn, vn, vn, jnp.broadcast_to(sink.astype(F32)[:, None], (ATT_HEADS, 128)))


def _route(logits_t, bias_col):
    scores = jax.nn.sigmoid(logits_t)
    sel = scores + bias_col
    rows = [sel[e:e + 1] for e in range(N_EXPERTS)]
    grp = []
    for g in range(N_GROUPS):
        r = rows[g * PER_GROUP:(g + 1) * PER_GROUP]
        best = None
        for i in range(PER_GROUP):
            for j in range(i + 1, PER_GROUP):
                pair = r[i] + r[j]
                best = pair if best is None else jnp.maximum(best, pair)
        grp.append(best)
    g_best = jnp.zeros_like(grp[0], dtype=jnp.int32)
    g_val = grp[0]
    for g in range(1, N_GROUPS):
        take = grp[g] > g_val
        g_best = jnp.where(take, g, g_best)
        g_val = jnp.where(take, grp[g], g_val)
    neg = -jnp.inf
    masked = [jnp.where(g_best == e // PER_GROUP, rows[e], neg) for e in range(N_EXPERTS)]
    i1 = jnp.zeros_like(g_best)
    v1 = masked[0]
    for e in range(1, N_EXPERTS):
        take = masked[e] > v1
        i1 = jnp.where(take, e, i1)
        v1 = jnp.where(take, masked[e], v1)
    i2 = jnp.full_like(g_best, -1)
    v2 = jnp.full_like(v1, neg)
    for e in range(N_EXPERTS):
        take = jnp.logical_and(i1 != e, masked[e] > v2)
        i2 = jnp.where(take, e, i2)
        v2 = jnp.where(take, masked[e], v2)
    w1 = jnp.zeros_like(v1)
    w2 = jnp.zeros_like(v1)
    for e in range(N_EXPERTS):
        w1 = jnp.where(i1 == e, scores[e:e + 1], w1)
        w2 = jnp.where(i2 == e, scores[e:e + 1], w2)
    inv = 1.0 / (w1 + w2)
    pad = jnp.zeros_like(w1)
    lane = lax.broadcasted_iota(jnp.int32, w1.shape, 1)
    hist = pad
    for e in range(N_EXPERTS):
        n_e = jnp.sum((i1 == e).astype(F32) + (i2 == e).astype(F32), axis=1, keepdims=True)
        hist = jnp.where(lane == e, n_e, hist)
    return jnp.concatenate([i1.astype(F32), i2.astype(F32), w1 * inv, w2 * inv, hist, pad, pad, pad], axis=0)


def _residual_and_route(x, out, m, ffn_gain, rw_ref, rb_ref, x_ref, h_ref, g_ref, first_row=0):
    n = x.shape[0]
    x1 = x + m[2:3] * out
    x_ref[pl.ds(first_row, n), :] = x1
    h = _modulated_norm(x1, ffn_gain, m[3:4], m[4:5])
    _store_row_tiles(h_ref, h, first_row=first_row)
    h_hi = h.astype(BF16)
    h_lo = (h - h_hi.astype(F32)).astype(BF16)
    logits = _dot(h_hi, rw_ref[0]) + _dot(h_lo, rw_ref[0]) + _dot(h_hi, rw_ref[1])
    g_ref[:, pl.ds(first_row, n)] = _route(logits.T[:N_EXPERTS], rb_ref[...])


def _merge_even_kernel(o0_ref, o1_ref, gg_ref, oa_ref, x_ref, mod0_ref, mod1_ref, gn_ref, bd_ref, w1_ref, w2_ref,
                       fg_ref, rw_ref, rb_ref, xo_ref, ho_ref, go_ref):
    n = x_ref.shape[0] // 2
    for half, mod_ref in enumerate((mod0_ref, mod1_ref)):
        rows = pl.ds(half * n, n)
        og = o0_ref[rows, :] + o1_ref[rows, :]
        ms = _head_sums(og * og, bd_ref[...]) * (1.0 / GLA_DV)
        g = og * lax.rsqrt(ms + NORM_EPS) * gn_ref[...] * _silu(gg_ref[rows, :])
        out = _dot(g.astype(BF16), w1_ref[...]) + _dot(oa_ref[rows, :].astype(BF16), w2_ref[...])
        _residual_and_route(x_ref[rows, :], out, mod_ref[0], fg_ref[...], rw_ref, rb_ref, xo_ref, ho_ref, go_ref,
                            first_row=half * n)


def _merge_even(o_f, o_b, zg, o_att, x, mods, out_norm, w_out, ffn_gain, router_wt, router_b, tm, mod_row):
    t, d = x.shape
    hv = GLA_HEADS * GLA_DV
    qw = ATT_HEADS * HEAD_DIM
    full = lambda a: pl.BlockSpec(a.shape, lambda i: (0,) * a.ndim)
    gn = jnp.tile(out_norm, GLA_HEADS).reshape(1, hv)
    bd = _block_ones(LANES // GLA_DV, GLA_DV)
    w1, w2 = w_out[:hv].astype(BF16), w_out[hv:].astype(BF16)
    fg = ffn_gain.reshape(1, d)
    rb = router_b.reshape(N_EXPERTS, 1)
    assert (t // tm) % 2 == 0
    tb = 2 * tm
    mod_spec = lambda half: pl.BlockSpec((1, 6, d), lambda i: (mod_row(2 * i + half), 0, 0))
    return pl.pallas_call(
        _merge_even_kernel,
        grid=(t // tb,),
        in_specs=[pl.BlockSpec((tb, hv), lambda i: (i, 0)),
                  pl.BlockSpec((tb, hv), lambda i: (i, 0)),
                  pl.BlockSpec((tb, hv), lambda i: (i, 2)),
                  pl.BlockSpec((tb, qw), lambda i: (i, 0)),
                  pl.BlockSpec((tb, d), lambda i: (i, 0)),
                  mod_spec(0), mod_spec(1),
                  full(gn), full(bd), full(w1), full(w2), full(fg), full(router_wt), full(rb)],
        out_specs=[pl.BlockSpec((tb, d), lambda i: (i, 0)),
                   pl.BlockSpec((tb * (d // LANES), LANES), lambda i: (i, 0)),
                   pl.BlockSpec((8, tb), lambda i: (0, i))],
        out_shape=[jax.ShapeDtypeStruct((t, d), F32),
                   jax.ShapeDtypeStruct((t * (d // LANES), LANES), F32),
                   jax.ShapeDtypeStruct((8, t), F32)],
        compiler_params=_cparams(("parallel",)),
        name="merge_even",
    )(o_f, o_b, zg, o_att, x, mods, mods, gn, bd, w1, w2, fg, router_wt, rb)


def _moe_plan(route, t, rows, route_tile):
    n_tiles = 2 * t // rows + N_EXPERTS
    eid = jnp.concatenate([route[0], route[1]]).astype(jnp.int32)
    slot = jnp.arange(2 * t, dtype=jnp.int32)
    gate = jnp.concatenate([route[2], route[3]])
    _, s_slot, s_gate = lax.sort((eid, slot, gate), num_keys=1, is_stable=True)
    counts = jnp.sum(route[4].reshape(t // route_tile, route_tile)[:, :N_EXPERTS], axis=0).astype(jnp.int32)
    padded = (counts + rows - 1) // rows * rows
    p_end = jnp.cumsum(padded)
    p_start = p_end - padded
    c_start = jnp.cumsum(counts) - counts
    tile_start = jnp.arange(n_tiles, dtype=jnp.int32) * rows
    tile_e = jnp.minimum(jnp.sum((tile_start[:, None] >= p_end[None, :]).astype(jnp.int32), axis=1), N_EXPERTS - 1)
    r_in = (tile_start - p_start[tile_e])[:, None] + jnp.arange(rows, dtype=jnp.int32)[None, :]
    over = r_in - counts[tile_e][:, None]
    valid = over < 0
    src = jnp.clip(c_start[tile_e][:, None] + r_in, 0, 2 * t - 1)
    g_slot = s_slot[src]
    row_tok = jnp.where(valid, jnp.where(g_slot >= t, g_slot - t, g_slot), 0)
    row_gate = jnp.where(valid, s_gate[src], 0.0)
    row_dst = jnp.where(valid, g_slot, 2 * t + tile_e[:, None] * rows + jnp.clip(over, 0, rows - 1))
    n_used = (p_end[-1] // rows).astype(jnp.int32)
    tile_e = jnp.where(tile_start < p_end[-1], tile_e, tile_e[jnp.maximum(n_used - 1, 0)])
    return (row_tok[:, None, :], row_gate[:, None, :], row_dst[:, None, :], tile_e, n_used.reshape(1))


def _moe_experts_kernel(te_ref, nu_ref, tok0_ref, tok1_ref, tokn_ref, dst_ref, gate_ref, wg_ref, wu_ref, wd_ref, h_hbm,
                        y_hbm, hbuf, ybuf, wgb, wub, wdb, sem_g, sem_s):
    j = pl.program_id(0)
    n_used = nu_ref[0]
    slot = j % 2
    g = SUBLANES
    rows = ybuf.shape[0] // (2 * g)
    tile = rows * g
    depth = hbuf.shape[0] // tile
    gslot = lax.rem(j, depth)

    def row(ref, i):
        return ref.at[pl.ds(pl.multiple_of(i * g, g), g)]

    def start_gather(idx_ref, s):
        for r in range(rows):
            pltpu.make_async_copy(row(h_hbm, idx_ref[0, 0, r]), row(hbuf, s * rows + r), sem_g.at[s]).start(priority=r % 2)

    def wait_gather(s):
        pltpu.make_async_copy(h_hbm.at[pl.ds(0, tile)], hbuf.at[pl.ds(pl.multiple_of(s * tile, tile), tile)],
                              sem_g.at[s]).wait()

    def wait_scatter(s):
        pltpu.make_async_copy(ybuf.at[pl.ds(pl.multiple_of(s * tile, tile), tile)], y_hbm.at[pl.ds(0, tile)],
                              sem_s.at[s]).wait()

    @pl.when(j == 0)
    def _():
        start_gather(tok0_ref, 0)
        start_gather(tok1_ref, 1)
        ybuf[...] = jnp.zeros(ybuf.shape, F32)
        n_real = y_hbm.shape[0] // g - N_EXPERTS * rows
        fills = [pltpu.make_async_copy(ybuf.at[pl.ds((k % 2) * tile, tile)],
                                       y_hbm.at[pl.ds((n_real + k * rows) * g, tile)], sem_s.at[k % 2])
                 for k in range(N_EXPERTS)]
        for f in fills:
            f.start()
        for f in fills[2:]:
            f.wait()

    active = j < n_used
    changed = jnp.logical_or(j == 0, te_ref[j] != te_ref[jnp.maximum(j - 1, 0)])

    @pl.when(jnp.logical_and(active, changed))
    def _():
        wgb[...] = wg_ref[0].astype(BF16)
        wub[...] = wu_ref[0].astype(BF16)
        wdb[...] = wd_ref[0].astype(BF16)

    @pl.when(active)
    def _():
        wait_gather(gslot)
        start_gather(tokn_ref, lax.rem(j + depth - 1, depth))
        h = _load_row_tiles(hbuf, rows, g, first_row=gslot * rows).astype(BF16)
        act = _silu(_dot(h, wgb[...])) * _dot(h, wub[...])
        ri = lax.broadcasted_iota(jnp.int32, (rows, rows), 0)
        ci = lax.broadcasted_iota(jnp.int32, (rows, rows), 1)
        gate = jnp.sum(jnp.where(ri == ci, gate_ref[0], 0.0), axis=1, keepdims=True)
        y = _dot((act * gate).astype(BF16), wdb[...])
        wait_scatter(slot)
        _store_row_tiles(ybuf, y, first_row=slot * rows)
        for r in range(rows):
            pltpu.make_async_copy(row(ybuf, slot * rows + r), row(y_hbm, dst_ref[0, 0, r]), sem_s.at[slot]).start(priority=r % 2)

        @pl.when(j == n_used - 1)
        def _():
            wait_scatter(slot)
            wait_scatter(1 - slot)
            for ahead in range(1, depth):
                wait_gather(lax.rem(j + ahead, depth))


def _moe_combine_kernel(ya_ref, yb_ref, x_ref, mod_ref, o_ref):
    n, d = x_ref.shape
    y = _load_row_tiles(ya_ref, n, d // LANES) + _load_row_tiles(yb_ref, n, d // LANES)
    o_ref[...] = x_ref[...] + mod_ref[0][5:6] * y


def _moe(h, route, x, mods, w_gate, w_up, w_down, layer, tm, mod_row, route_tile, combine=True):
    t, d = x.shape
    g = d // LANES
    assert g == SUBLANES
    rows = MOE_TILE
    row_tok, row_gate, row_dst, tile_e, n_used = _moe_plan(route, t, rows, route_tile)
    n_tiles = row_tok.shape[0]
    n_pair_rows = 2 * t + N_EXPERTS * rows
    assert MOE_GATHER_DEPTH == 3
    smem_tile = lambda which: pl.BlockSpec((1, 1, rows), lambda j, te, nu: (jnp.minimum(which(j), nu[0] - 1), 0, 0),
                                           memory_space=pltpu.SMEM)
    pairs = pl.pallas_call(
        _moe_experts_kernel,
        grid_spec=pltpu.PrefetchScalarGridSpec(
            num_scalar_prefetch=2,
            grid=(n_tiles,),
            in_specs=[smem_tile(lambda j: 0), smem_tile(lambda j: 1), smem_tile(lambda j: j + MOE_GATHER_DEPTH - 1),
                      smem_tile(lambda j: j),
                      pl.BlockSpec((1, 1, rows), lambda j, te, nu: (j, 0, 0)),
                      pl.BlockSpec((None, 1, d, D_EXPERT), lambda j, te, nu: (layer, te[j], 0, 0)),
                      pl.BlockSpec((None, 1, d, D_EXPERT), lambda j, te, nu: (layer, te[j], 0, 0)),
                      pl.BlockSpec((None, 1, D_EXPERT, d), lambda j, te, nu: (layer, te[j], 0, 0)),
                      pl.BlockSpec(memory_space=pl.ANY)],
            out_specs=pl.BlockSpec(memory_space=pl.ANY),
            scratch_shapes=[pltpu.VMEM((MOE_GATHER_DEPTH * rows * g, LANES), F32), pltpu.VMEM((2 * rows * g, LANES), F32),
                            pltpu.VMEM((d, D_EXPERT), BF16), pltpu.VMEM((d, D_EXPERT), BF16),
                            pltpu.VMEM((D_EXPERT, d), BF16),
                            pltpu.SemaphoreType.DMA((MOE_GATHER_DEPTH,)), pltpu.SemaphoreType.DMA((2,))]),
        out_shape=jax.ShapeDtypeStruct((n_pair_rows * g, LANES), F32),
        compiler_params=_cparams(("arbitrary",)),
        name="moe_experts",
    )(tile_e, n_used, row_tok, row_tok, row_tok, row_dst, row_gate, w_gate, w_up, w_down, h)
    if not combine:
        return pairs
    return pl.pallas_call(
        _moe_combine_kernel,
        grid=(t // tm,),
        in_specs=[pl.BlockSpec((tm * g, LANES), lambda i: (i, 0)),
                  pl.BlockSpec((tm * g, LANES), lambda i: (t // tm + i, 0)),
                  pl.BlockSpec((tm, d), lambda i: (i, 0)),
                  pl.BlockSpec((1, 6, d), lambda i: (mod_row(i), 0, 0))],
        out_specs=pl.BlockSpec((tm, d), lambda i: (i, 0)),
        out_shape=jax.ShapeDtypeStruct((t, d), F32),
        compiler_params=_cparams(("parallel",)),
        name="moe_combine",
    )(pairs, pairs, x, mods)


def _chan_dft_kernel(z_ref, w_ref, o_ref):
    res = _dot(z_ref[...].astype(BF16), w_ref[...]).astype(BF16)
    o_ref[0] = res[:, :FOURIER_WIDTH]
    o_ref[1] = res[:, FOURIER_WIDTH:]


def _chan_dft(zf, w, batch, l_ctx, l_lat, tm):
    l_tot = l_ctx + l_lat
    nt = l_lat // tm
    fw = FOURIER_WIDTH
    return pl.pallas_call(
        _chan_dft_kernel,
        grid=(batch, nt),
        in_specs=[pl.BlockSpec((tm, fw), lambda b, i: (b * (l_tot // tm) + l_ctx // tm + i, 0)),
                  pl.BlockSpec(w.shape, lambda b, i: (0, 0))],
        out_specs=pl.BlockSpec((2, tm, fw), lambda b, i: (0, i, b)),
        out_shape=jax.ShapeDtypeStruct((2, l_lat, batch * fw), BF16),
        compiler_params=_cparams(("parallel", "parallel")),
        name="fourier_channels",
    )(zf, w)


def _seq_dft_kernel(ca_ref, sa_ref, cb_ref, sb_ref, z_ref, o_ref, acc_ref):
    k = pl.program_id(1)
    tk = z_ref.shape[1]
    sub = DFT_SPLIT
    n_a = ca_ref.shape[1]

    @pl.when(k == 0)
    def _():
        acc_ref[...] = jnp.zeros_like(acc_ref)

    col_a = k * tk + lax.broadcasted_iota(jnp.int32, (n_a, tk), 1)
    pick_a = (col_a // sub == lax.broadcasted_iota(jnp.int32, (n_a, tk), 0)).astype(F32)
    pick_b = (lax.broadcasted_iota(jnp.int32, (sub, tk), 1) % sub == lax.broadcasted_iota(jnp.int32, (sub, tk), 0)).astype(F32)
    ca, sa = _dot(ca_ref[...], pick_a), _dot(sa_ref[...], pick_a)
    cb, sb = _dot(cb_ref[...], pick_b), _dot(sb_ref[...], pick_b)
    cos_t = (ca * cb - sa * sb).astype(BF16)
    sin_t = (sa * cb + ca * sb).astype(BF16)
    acc_ref[...] += _dot(cos_t, z_ref[0]) + _dot(sin_t, z_ref[1])

    @pl.when(k == pl.num_programs(1) - 1)
    def _():
        o_ref[...] = acc_ref[...]


def _seq_dft(tables, zc, batch, tm, tk):
    ca, sa, cb, sb = tables
    l = ca.shape[0]
    fw = FOURIER_WIDTH
    small = lambda a: pl.BlockSpec((tm, a.shape[1]), lambda i, k: (i, 0))
    return pl.pallas_call(
        _seq_dft_kernel,
        grid=(l // tm, l // tk),
        in_specs=[small(ca), small(sa), small(cb), small(sb),
                  pl.BlockSpec((2, tk, batch * fw), lambda i, k: (0, k, 0))],
        out_specs=pl.BlockSpec((tm, batch * fw), lambda i, k: (i, 0)),
        out_shape=jax.ShapeDtypeStruct((l, batch * fw), F32),
        scratch_shapes=[pltpu.VMEM((tm, batch * fw), F32)],
        compiler_params=_cparams(("parallel", "arbitrary")),
        name="fourier_sequence",
    )(ca, sa, cb, sb, zc)


def _dft_tables(l):
    m = jnp.arange(l, dtype=jnp.int32)[:, None]
    n1 = l // DFT_SPLIT
    a = (m * (jnp.arange(n1, dtype=jnp.int32)[None, :] * DFT_SPLIT)) % l
    b = (m * jnp.arange(DFT_SPLIT, dtype=jnp.int32)[None, :]) % l
    wa = a.astype(F32) * (2.0 * np.pi / l)
    wb = b.astype(F32) * (2.0 * np.pi / l)
    tables = (jnp.cos(wa), jnp.sin(wa), jnp.cos(wb), jnp.sin(wb))
    gd = FOURIER_GROUP_DIM
    cc = (jnp.arange(gd, dtype=jnp.int32)[:, None] * jnp.arange(gd, dtype=jnp.int32)[None, :]) % gd
    wc = cc.astype(F32) * (2.0 * np.pi / gd)
    scale = 1.0 / np.sqrt(float(l) * gd)
    eye = jnp.eye(FOURIER_WIDTH // gd, dtype=F32)
    chan = jnp.concatenate([jnp.kron(eye, jnp.cos(wc)), -jnp.kron(eye, jnp.sin(wc))], axis=1) * scale
    return tables, chan.astype(BF16)


def _rwkv_prepare(d, pos, z, z_before, z_after, par, nc_ctx, nc_tot):
    mu, kks, ka, rk, w0, w2, a0, a2, bd = par
    c = z.shape[0]
    n = RWKV_DIM
    seg_first = jnp.logical_or(pos == 0, pos == nc_ctx)
    seg_last = jnp.logical_or(pos == nc_ctx - 1, pos == nc_tot - 1)
    row = lax.broadcasted_iota(jnp.int32, z.shape, 0)
    prev_row = jnp.where(seg_first, 0.0, z_before[7:8, :])
    next_row = jnp.where(seg_last, 0.0, z_after[0:1, :])
    z_prev = jnp.where(row == 0, prev_row, pltpu.roll(z, 1, 0))
    z_next = jnp.where(row == c - 1, next_row, pltpu.roll(z, c - 1, 0))
    zs = z + mu * (0.5 * (z_prev + z_next) - z)

    r, k, v = zs[:, 0:n], zs[:, n:2 * n], zs[:, 2 * n:3 * n]
    zw = zs[:, 3 * n:3 * n + RWKV_RANK_PAD]
    za = zs[:, 3 * n + RWKV_RANK_PAD:3 * n + 2 * RWKV_RANK_PAD]
    zg = zs[:, 3 * n + 2 * RWKV_RANK_PAD:3 * n + 3 * RWKV_RANK_PAD]

    kk = k * kks
    kk = kk * lax.rsqrt(_head_sums(kk * kk, bd) + L2_EPS)
    w_log = _log_sigmoid(w0[d] + _dot(jnp.tanh(zw), w2[d])) - 0.5
    lw = -jnp.exp(w_log)
    a = jax.nn.sigmoid(a0[d] + _dot(za, a2[d]))
    kd = k * (1.0 + (a - 1.0) * ka)
    beta = kk * a

    cl = _dot((_visit_order(d, c) >= 0).astype(F32), lw)
    c_tot = jnp.sum(lw, axis=0, keepdims=True)
    grow = jnp.exp(-cl)
    tail = jnp.exp(c_tot - cl)
    ops = dict(k_s=kd * grow,
               b_s=beta * grow,
               kap_s=kk * jnp.exp(cl - lw),
               r_s=r * jnp.exp(cl),
               k_e=kd * tail,
               b_e=beta * tail,
               gam=jnp.exp(c_tot), v=v,
               bonus=_head_sums(r * kd * rk, bd),
               sign=1 if d == 0 else -1)
    return ops, zg


def _rwkv_kernel(zf_ref, zfp_ref, zfn_ref, zb_ref, zbp_ref, zbn_ref, mu_ref, kks_ref, ka_ref, rk_ref, w0_ref, w2_ref,
                 a0_ref, a2_ref, g2_ref, bd_ref, yf_ref, yb_ref, gate_ref, st_ref, *, nc_ctx, nc_tot):
    s = pl.program_id(1)
    nb, c = zf_ref.shape[0], zf_ref.shape[1]
    hd = HEAD_DIM
    pw = 2 * hd

    @pl.when(s == 0)
    def _():
        st_ref[...] = jnp.zeros_like(st_ref)

    par = (mu_ref[...], kks_ref[...], ka_ref[...], rk_ref[...], w0_ref, w2_ref, a0_ref, a2_ref, bd_ref[...])
    ctxs = []
    for b in range(nb):
        fwd, zg = _rwkv_prepare(0, _chunk_pos(s, 0, nc_ctx, nc_tot), zf_ref[b], zfp_ref[b], zfn_ref[b], par, nc_ctx, nc_tot)
        bwd, _ = _rwkv_prepare(1, _chunk_pos(s, 1, nc_ctx, nc_tot), zb_ref[b], zbp_ref[b], zbn_ref[b], par, nc_ctx, nc_tot)
        gate_ref[b] = _dot(jax.nn.sigmoid(zg), g2_ref[...])
        ctxs += [fwd, bwd]

    lane = lax.broadcasted_iota(jnp.int32, (c, pw), 1)
    rowi = lax.broadcasted_iota(jnp.int32, (c, pw), 0)
    left = lane < hd
    eye_p = (rowi == lane % hd).astype(F32)
    same_head = (lax.broadcasted_iota(jnp.int32, (pw, pw), 0) < hd) == (lax.broadcasted_iota(jnp.int32, (pw, pw), 1) < hd)

    def bd(y):
        return jnp.concatenate([jnp.where(left, y, 0.0), jnp.where(left, 0.0, y)], axis=0)

    n_pairs = RWKV_HEADS // 2
    prob = [(o, slice(i * pw, (i + 1) * pw)) for o in ctxs for i in range(n_pairs)]
    ahead = [(rowi - lane % hd) * o['sign'] for o, _ in prob]
    get = lambda name: [o[name][:, s_] for o, s_ in prob]
    kap, r_s, k_s, b_s, k_e, b_e, vp, gam, bonus = (get(x) for x in ('kap_s', 'r_s', 'k_s', 'b_s', 'k_e', 'b_e', 'v', 'gam', 'bonus'))
    p = [_dot_nt(jnp.concatenate([a_, b_], axis=0), jnp.concatenate([bd(c_), bd(d_)], axis=0))
         for a_, b_, c_, d_ in zip(kap, r_s, k_s, b_s)]
    m1 = [jnp.where(h_ > 0, x[0:c, 0:pw], 0.0) for x, h_ in zip(p, ahead)]
    m2 = [jnp.where(h_ > 0, x[0:c, pw:2 * pw], 0.0) for x, h_ in zip(p, ahead)]
    n1 = [jnp.where(h_ >= 0, x[c:2 * c, 0:pw], 0.0) for x, h_ in zip(p, ahead)]
    n2 = [jnp.where(h_ >= 0, x[c:2 * c, pw:2 * pw], 0.0) for x, h_ in zip(p, ahead)]
    m1v = [_dot(a_, bd(b_)) for a_, b_ in zip(m1, vp)]
    t_inv = [eye_p - x for x in m2]
    q = [_dot(x, bd(x)) for x in m2]
    span = 2
    while 2 * span < c:
        both = [_dot(jnp.concatenate([t_, q_], axis=0), bd(q_)) for t_, q_ in zip(t_inv, q)]
        t_inv = [t_ + x[0:c] for t_, x in zip(t_inv, both)]
        q = [x[c:2 * c] for x in both]
        span *= 2
    t_inv = [t_ + _dot(t_, bd(q_)) for t_, q_ in zip(t_inv, q)]
    tx = [_dot(t_, jnp.concatenate([bd(a_), bd(mv)], axis=1)) for t_, a_, mv in zip(t_inv, kap, m1v)]
    st = [st_ref[i] for i in range(len(prob))]
    su = [_dot_nt(jnp.concatenate([x[:, 0:pw], r_], axis=0), s0) for x, r_, s0 in zip(tx, r_s, st)]
    u = [x[0:c] + y_[:, pw:2 * pw] for x, y_ in zip(su, tx)]
    ys = [x[c:2 * c] + _dot(jnp.concatenate([a_, -b_], axis=1), jnp.concatenate([bd(v_), bd(u_)], axis=0)) + bo * v_
          for x, a_, b_, v_, u_, bo in zip(su, n1, n2, vp, u, bonus)]
    for i in range(len(prob)):
        upd = _dot_tn(jnp.concatenate([vp[i], u[i]], axis=0), jnp.concatenate([k_e[i], -b_e[i]], axis=0))
        st_ref[i] = st[i] * gam[i] + jnp.where(same_head, upd, 0.0)
    for b in range(nb):
        yf_ref[b] = jnp.concatenate(ys[2 * b * n_pairs:(2 * b + 1) * n_pairs], axis=1)
        yb_ref[b] = jnp.concatenate(ys[(2 * b + 1) * n_pairs:(2 * b + 2) * n_pairs], axis=1)


def _rwkv(zr, mu, kk_scale, k_a, r_k, w0, w2_pad, a0, a2_pad, g2, batch, nc_ctx, nc_tot):
    t, zw_ = zr.shape
    c = SEQ_CHUNK
    n = RWKV_DIM
    bd = _block_ones(LANES // HEAD_DIM, HEAD_DIM)
    full = lambda a: pl.BlockSpec(a.shape, lambda b, s: (0,) * a.ndim)
    sub = c // SUBLANES
    l_tot = t // batch
    n_sub = l_tot // SUBLANES
    nb = math.gcd(batch, RWKV_SCAN_BATCH)
    z3 =zr.reshape(batch, l_tot, zw_)

    def z_specs(d):
        pos = lambda s: _chunk_pos(s, d, nc_ctx, nc_tot)
        return [pl.BlockSpec((nb, c, zw_), lambda b, s: (b, pos(s), 0)),
                pl.BlockSpec((nb, SUBLANES, zw_), lambda b, s: (b, jnp.maximum(pos(s) * sub - 1, 0), 0)),
                pl.BlockSpec((nb, SUBLANES, zw_), lambda b, s: (b, jnp.minimum((pos(s) + 1) * sub, n_sub - 1), 0))]

    vec = lambda a: a.reshape(1, -1)
    args = (vec(mu), vec(kk_scale), vec(k_a), vec(r_k), w0.reshape(2, 1, n), w2_pad, a0.reshape(2, 1, n), a2_pad, g2, bd)
    out = lambda d: pl.BlockSpec((nb, c, n), lambda b, s: (b, _chunk_pos(s, d, nc_ctx, nc_tot), 0))
    outs = pl.pallas_call(
        functools.partial(_rwkv_kernel, nc_ctx=nc_ctx, nc_tot=nc_tot),
        grid=(batch // nb, nc_tot),
        in_specs=z_specs(0) + z_specs(1) + [full(a) for a in args],
        out_specs=[out(0), out(1), out(0)],
        out_shape=[jax.ShapeDtypeStruct((batch, l_tot, n), F32)] * 3,
        scratch_shapes=[pltpu.VMEM((nb * RWKV_HEADS, 2 * HEAD_DIM, 2 * HEAD_DIM), F32)],
        compiler_params=_cparams(("parallel", "arbitrary")),
        name="rwkv_scan",
    )(z3, z3, z3, z3, z3, z3, *args)
    return [o.reshape(t, n) for o in outs]


def _merge_odd_kernel(y0_ref, y1_ref, gate_ref, fo_ref, x_ref, mod_ref, lg_ref, lb_ref, bd_ref, w1_ref, w2_ref,
                      fg_ref, rw_ref, rb_ref, xo_ref, ho_ref, go_ref):
    y = y0_ref[...] + y1_ref[...]
    bd = bd_ref[...]
    mean = _head_sums(y, bd) * (1.0 / HEAD_DIM)
    yc = y - mean
    var = _head_sums(yc * yc, bd) * (1.0 / HEAD_DIM)
    rw = (yc * lax.rsqrt(var + RWKV_GN_EPS) * lg_ref[...] + lb_ref[...]) * gate_ref[...]
    out = _dot(fo_ref[...].astype(BF16), w1_ref[...]) + _dot(rw.astype(BF16), w2_ref[...])
    _residual_and_route(x_ref[...], out, mod_ref[0], fg_ref[...], rw_ref, rb_ref, xo_ref, ho_ref, go_ref)


def _merge_odd(y0, y1, gate, fo, x, mods, ln_g, ln_b, w_out, ffn_gain, router_wt, router_b, tm, batch, l_ctx, l_lat):
    d = x.shape[1]
    n = RWKV_DIM
    fw = FOURIER_WIDTH
    l_tot = l_ctx + l_lat
    nt = l_lat // tm
    t_out = batch * l_lat
    full = lambda a: pl.BlockSpec(a.shape, lambda b, i: (0,) * a.ndim)
    src = lambda b, i: b * (l_tot // tm) + l_ctx // tm + i
    bd = _block_ones(LANES // HEAD_DIM, HEAD_DIM)
    w1, w2 = w_out[:fw].astype(BF16), w_out[fw:].astype(BF16)
    lg, lb, fg, rb = ln_g.reshape(1, n), ln_b.reshape(1, n), ffn_gain.reshape(1, d), router_b.reshape(N_EXPERTS, 1)
    return pl.pallas_call(
        _merge_odd_kernel,
        grid=(batch, nt),
        in_specs=[pl.BlockSpec((tm, n), lambda b, i: (src(b, i), 0)),
                  pl.BlockSpec((tm, n), lambda b, i: (src(b, i), 0)),
                  pl.BlockSpec((tm, n), lambda b, i: (src(b, i), 0)),
                  pl.BlockSpec((tm, fw), lambda b, i: (i, b)),
                  pl.BlockSpec((tm, d), lambda b, i: (src(b, i), 0)),
                  pl.BlockSpec((1, 6, d), lambda b, i: (b, 0, 0)),
                  full(lg), full(lb), full(bd), full(w1), full(w2), full(fg), full(router_wt), full(rb)],
        out_specs=[pl.BlockSpec((tm, d), lambda b, i: (b * nt + i, 0)),
                   pl.BlockSpec((tm * (d // LANES), LANES), lambda b, i: (b * nt + i, 0)),
                   pl.BlockSpec((8, tm), lambda b, i: (0, b * nt + i))],
        out_shape=[jax.ShapeDtypeStruct((t_out, d), F32),
                   jax.ShapeDtypeStruct((t_out * (d // LANES), LANES), F32),
                   jax.ShapeDtypeStruct((8, t_out), F32)],
        compiler_params=_cparams(("parallel", "parallel")),
        name="merge_odd",
    )(y0, y1, gate, fo, x, mods, lg, lb, bd, w1, w2, fg, router_wt, rb)


def _rope_tables(l_ctx, l_lat):
    rows = l_lat // GRID_W
    row = jnp.repeat(jnp.arange(rows, dtype=F32), GRID_W)
    col = jnp.tile(jnp.arange(GRID_W, dtype=F32), rows)
    n_freq = HEAD_DIM // 4
    inv_freq = ROPE_THETA ** (-jnp.arange(n_freq, dtype=F32) / n_freq)
    ang = jnp.concatenate([row[:, None] * inv_freq, col[:, None] * inv_freq], axis=-1)
    cos, sin = jnp.cos(ang), jnp.sin(ang)
    cos64 = jnp.concatenate([cos, cos], axis=1)
    sin64 = jnp.concatenate([-sin, sin], axis=1)
    cos64 = jnp.concatenate([jnp.ones((l_ctx, HEAD_DIM), F32), cos64], axis=0)
    sin64 = jnp.concatenate([jnp.zeros((l_ctx, HEAD_DIM), F32), sin64], axis=0)
    return cos64, sin64


def _pad_rank(w):
    _, r, n = w.shape
    out = jnp.zeros((2, RWKV_RANK_PAD, n), w.dtype)
    out = out.at[0, 0:r].set(w[0])
    return out.at[1, r:2 * r].set(w[1])


def _even_layer(ctx, x_lat, mods, p, batch, l_ctx, l_lat, tm, mod_row):
    l_tot = l_ctx + l_lat
    nc_ctx, nc_tot = l_ctx // SEQ_CHUNK, l_tot // SEQ_CHUNK
    hk, hv = GLA_HEADS * GLA_DK, GLA_HEADS * GLA_DV
    qw, kw = ATT_HEADS * HEAD_DIM, ATT_KV_HEADS * HEAD_DIM
    w_in = p['w_in']
    o = np.cumsum([0, hk, hk, hv, hv, 2 * GLA_LOWRANK, qw, kw, kw])
    w_gla = jnp.concatenate([w_in[:, o[0]:o[4]]], axis=1).astype(BF16)
    w_dec = jnp.pad(w_in[:, o[4]:o[5]], ((0, 0), (0, 128 - 2 * GLA_LOWRANK))).astype(BF16)
    w_q = w_in[:, o[5]:o[6]].astype(BF16)
    w_kv = w_in[:, o[6]:o[8]].astype(BF16)
    cos64, sin64 = _rope_tables(l_ctx, l_lat)
    x, zg, zdec, qn, kn, vn = _project_even(ctx, x_lat, mods, p['norm_mix'], [w_gla, w_dec, w_q, w_kv], cos64, sin64,
                                            p['q_norm'], p['k_norm'], tm, mod_row, batch, l_ctx, l_lat)

    dec_w_pad = _pad_rank(p['dec_w'])
    o_f, o_b = _gla(zg, zdec, dec_w_pad, p['dec_b'].reshape(2, 1, hk), batch, nc_ctx, nc_tot)
    o_att = _attention(qn, kn, vn, p['sink'], batch, l_ctx, l_tot)

    x1, h, gates = _merge_even(o_f, o_b, zg, o_att, x, mods, p['out_norm'], p['w_out'], p['norm_ffn'],
                               p['router_wt'], p['router_b'], tm, mod_row)
    pairs = _moe(h, gates, x1, mods, p['moe_g'], p['moe_u'], p['moe_d'], p['moe_layer'], tm, mod_row, tm, combine=False)
    return pairs, x1


def _odd_layer(pairs, x_prev, mods_prev, mods, p, batch, l_ctx, l_lat, tm, mod_row, tm_moe):
    l_tot = l_ctx + l_lat
    nc_ctx, nc_tot = l_ctx // SEQ_CHUNK, l_tot // SEQ_CHUNK
    n = RWKV_DIM
    fw = FOURIER_WIDTH
    w_in = p['w_in']
    rank_w, rank_a = p['w2'].shape[1], p['a2'].shape[1]
    o = np.cumsum([0, fw, n, n, n, 2 * rank_w, 2 * rank_a])
    pad_cols = lambda w: jnp.pad(w, ((0, 0), (0, RWKV_RANK_PAD - w.shape[1])))
    w_f = w_in[:, o[0]:o[1]].astype(BF16)
    w_r = jnp.concatenate([w_in[:, o[1]:o[4]], pad_cols(w_in[:, o[4]:o[5]]), pad_cols(w_in[:, o[5]:o[6]]),
                           w_in[:, o[6]:]], axis=1).astype(BF16)
    x, (zf, zr) = _project(pairs, x_prev, mods_prev, mods, p['norm_mix'], [w_f, w_r], tm, mod_row)

    mu = p['mu']
    mu_r = jnp.concatenate([mu[0:3 * n], pad_cols(mu[None, 3 * n:3 * n + 2 * rank_w])[0],
                            pad_cols(mu[None, 3 * n + 2 * rank_w:3 * n + 2 * rank_w + 2 * rank_a])[0],
                            mu[3 * n + 2 * rank_w + 2 * rank_a:]])
    y0, y1, gate = _rwkv(zr, mu_r, p['kk_scale'], p['k_a'], p['r_k'].reshape(-1), p['w0'], _pad_rank(p['w2']),
                    p['a0'], _pad_rank(p['a2']), p['g2'], batch, nc_ctx, nc_tot)

    tables, chan = _dft_tables(l_lat)
    zc = _chan_dft(zf, chan, batch, l_ctx, l_lat, tm)
    fo = _seq_dft(tables, zc, batch, min(512, l_lat), min(1024, l_lat))

    x1, h, gates = _merge_odd(y0, y1, gate, fo, x, mods, p['ln_g'], p['ln_b'], p['w_out'], p['norm_ffn'],
                              p['router_wt'], p['router_b'], tm, batch, l_ctx, l_lat)
    lat_tiles = l_lat // tm_moe
    return _moe(h, gates, x1, mods, p['moe_g'], p['moe_u'], p['moe_d'], p['moe_layer'], tm_moe,
                lambda i: i // lat_tiles, tm)


def kernel(x, c, ctx, c_ctx, ada_w, ada_b, norm_mix, norm_ffn, even_w_in, even_w_out, gla_dec_w, gla_dec_b, gla_out_norm, att_q_norm, att_k_norm, att_sink, odd_w_in, odd_w_out, rwkv_mu, rwkv_w0, rwkv_w2, rwkv_a0, rwkv_a2, rwkv_g2, rwkv_kk_scale, rwkv_k_a, rwkv_r_k, rwkv_ln_g, rwkv_ln_b, router_w, router_b, moe_w_gate, moe_w_up, moe_w_down):
    batch, l_lat, d = x.shape
    l_ctx = ctx.shape[1]
    l_tot = l_ctx + l_lat
    assert batch < 8 and ada_w.shape[0] == 2
    tm = 256 if (l_ctx % 256 == 0 and l_lat % 256 == 0) else 128
    tm_moe = 512 if (l_lat % 512 == 0 and tm == 256) else tm
    assert l_ctx % tm == 0 and l_lat % tm == 0 and l_tot % l_ctx == 0 and l_lat % GRID_W == 0

    cc = jnp.concatenate([c, c_ctx[None, :], jnp.zeros((8 - batch - 1, d), F32)], axis=0)
    tiles_per_b = l_tot // tm
    ctx_tiles = l_ctx // tm

    def mod_row(i):
        return jnp.where(i % tiles_per_b < ctx_tiles, batch, i // tiles_per_b)

    rw_pad = jnp.pad(router_w, ((0, 0), (0, LANES - N_EXPERTS)))
    rw_hi = rw_pad.astype(BF16)
    router_wt = jnp.stack([rw_hi, (rw_pad - rw_hi.astype(F32)).astype(BF16)])
    moe = lambda layer: dict(moe_g=moe_w_gate, moe_u=moe_w_up, moe_d=moe_w_down, moe_layer=layer)

    mods_all = _modvec(cc, ada_w, ada_b)
    mods0 = mods_all[0]
    p0 = dict(w_in=even_w_in[0], w_out=even_w_out[0], dec_w=gla_dec_w[0], dec_b=gla_dec_b[0],
              out_norm=gla_out_norm[0], q_norm=att_q_norm[0], k_norm=att_k_norm[0], sink=att_sink[0],
              norm_mix=norm_mix[0], norm_ffn=norm_ffn[0], router_wt=router_wt, router_b=router_b, **moe(0))
    pairs, x1 = _even_layer(ctx.reshape(batch * l_ctx, d), x.reshape(batch * l_lat, d), mods0, p0, batch, l_ctx, l_lat,
                            tm, mod_row)

    mods1 = mods_all[1]
    p1 = dict(w_in=odd_w_in[0], w_out=odd_w_out[0], mu=rwkv_mu[0], w0=rwkv_w0[0], w2=rwkv_w2[0], a0=rwkv_a0[0],
              a2=rwkv_a2[0], g2=rwkv_g2[0], kk_scale=rwkv_kk_scale[0], k_a=rwkv_k_a[0], r_k=rwkv_r_k[0],
              ln_g=rwkv_ln_g[0], ln_b=rwkv_ln_b[0], norm_mix=norm_mix[1], norm_ffn=norm_ffn[1],
              router_wt=router_wt, router_b=router_b, **moe(1))
    out = _odd_layer(pairs, x1, mods0, mods1, p1, batch, l_ctx, l_lat, tm, mod_row, tm_moe)
    return out.reshape(batch, l_lat, d)
```
